```python
import jax, jax.numpy as jnp
from jax import lax
import numpy as np

D_MODEL = 2048
BATCH = 2
SEQ = 8192
DEPTH = 2

GRID_W = 64
CTX_LEN = 256
HEAD_DIM = 128
ATT_Q_HEADS = 4
ATT_KV_HEADS = 2
LRU_WIDTH = 1024
LRU_BLOCKS = 8
LRU_BLOCK = LRU_WIDTH // LRU_BLOCKS
LRU_C = 8.0
CONV_W = 4
NA_HEADS = 4
NA_ROWS = 8
NA_COLS = 16
Q_BLOCK = 128
ATT_Q_W = ATT_Q_HEADS * HEAD_DIM
ATT_KV_W = ATT_KV_HEADS * HEAD_DIM
NA_W = NA_HEADS * HEAD_DIM
MIX_WIDTH = ATT_Q_W + LRU_WIDTH + NA_W
IN_WIDTH = ATT_Q_W + 2 * ATT_KV_W + 2 * LRU_WIDTH + 3 * NA_W
MOE_GROUPS = 4
MOE_EXPERTS_PER_GROUP = 8
MOE_EXPERTS = MOE_GROUPS * MOE_EXPERTS_PER_GROUP
MOE_TOP_K = 2
MOE_HIDDEN = 1024
MOE_BLOCK = 128
ROPE_THETA = 10000.0
EPS = 1e-6

kernel_name = 'hybrid_parallel_mixer_dit_block'


def _rmsnorm(x, g):
    x32 = x.astype(jnp.float32)
    y = x32 * lax.rsqrt(jnp.mean(x32 * x32, axis=-1, keepdims=True) + EPS)
    return y.astype(x.dtype) * g


def _heads(t, n):
    return t.reshape(*t.shape[:-1], n, HEAD_DIM)


def _split_cols(p):
    sizes = (ATT_Q_W, ATT_KV_W, ATT_KV_W, LRU_WIDTH, LRU_WIDTH, NA_W, NA_W, NA_W)
    idx = [int(i) for i in np.cumsum(sizes)[:-1]]
    return jnp.split(p, idx, axis=-1)


def _axial_rope_tables(n_tokens):
    pos = jnp.arange(n_tokens, dtype=jnp.int32)
    rows = (pos // GRID_W).astype(jnp.float32)
    cols = (pos % GRID_W).astype(jnp.float32)
    n_freq = HEAD_DIM // 4
    inv = 1.0 / (ROPE_THETA ** (jnp.arange(n_freq, dtype=jnp.float32) / n_freq))
    ang = jnp.stack([rows[:, None] * inv, cols[:, None] * inv], axis=1)
    return jnp.cos(ang), jnp.sin(ang)


def _rope(t, cos, sin):
    n_freq = HEAD_DIM // 4
    t32 = t.astype(jnp.float32).reshape(*t.shape[:-1], 2, 2, n_freq)
    t1, t2 = t32[..., 0, :], t32[..., 1, :]
    cs = cos[None, :, None]
    sn = sin[None, :, None]
    out = jnp.stack([t1 * cs - t2 * sn, t2 * cs + t1 * sn], axis=-2)
    return out.reshape(t.shape).astype(t.dtype)


def _gqa_attend(q, k, v):
    s = jnp.einsum('bqkgd,bskd->bkgqs', q, k).astype(jnp.float32) * (HEAD_DIM ** -0.5)
    p = jax.nn.softmax(s, axis=-1).astype(v.dtype)
    return jnp.einsum('bkgqs,bskd->bqkgd', p, v)


def _gqa_latent(q, k, v, k_ctx, v_ctx):
    B, S = q.shape[:2]
    g = ATT_Q_HEADS // ATT_KV_HEADS
    k_all = jnp.concatenate([k, k_ctx], axis=1)
    v_all = jnp.concatenate([v, v_ctx], axis=1)
    qb = jnp.moveaxis(q.reshape(B, S // Q_BLOCK, Q_BLOCK, ATT_KV_HEADS, g, HEAD_DIM), 1, 0)
    o = lax.map(lambda qi: _gqa_attend(qi, k_all, v_all), qb)
    return jnp.moveaxis(o, 0, 1).reshape(B, S, ATT_Q_W)


def _neighbourhood_latent(q, k, v, k_ctx, v_ctx, rpb):
    B, S = q.shape[:2]
    rows = S // GRID_W
    kr = min(NA_ROWS, rows)
    qg = q.reshape(B, rows, GRID_W, NA_HEADS, HEAD_DIM)
    kg = k.reshape(B, rows, GRID_W, NA_HEADS, HEAD_DIM)
    vg = v.reshape(B, rows, GRID_W, NA_HEADS, HEAD_DIM)
    j = np.arange(GRID_W)
    cs = np.clip(j - NA_COLS // 2, 0, GRID_W - NA_COLS)
    col_idx = cs[:, None] + np.arange(NA_COLS)[None, :]
    dc_idx = col_idx - j[:, None] + (NA_COLS - 1)
    bias_c = rpb[:, :, dc_idx]
    n_loc = kr * NA_COLS
    scale = HEAD_DIM ** -0.5

    def one_row(r):
        rs = jnp.clip(r - NA_ROWS // 2, 0, rows - kr)
        kb = lax.dynamic_slice_in_dim(kg, rs, kr, axis=1)
        vb = lax.dynamic_slice_in_dim(vg, rs, kr, axis=1)
        kn = kb[:, :, col_idx]
        vn = vb[:, :, col_idx]
        qr = lax.dynamic_index_in_dim(qg, r, axis=1, keepdims=False)
        dr_idx = rs + jnp.arange(kr) - r + (NA_ROWS - 1)
        bias = jnp.take(bias_c, dr_idx, axis=1).transpose(0, 2, 1, 3)
        s_loc = jnp.einsum('bjhd,bajchd->bhjac', qr, kn).astype(jnp.float32) * scale
        s_loc = s_loc + bias[None].astype(jnp.float32)
        s_ctx = jnp.einsum('bjhd,bnhd->bhjn', qr, k_ctx).astype(jnp.float32) * scale
        s = jnp.concatenate([s_loc.reshape(B, NA_HEADS, GRID_W, n_loc), s_ctx], axis=-1)
        p = jax.nn.softmax(s, axis=-1).astype(v.dtype)
        p_loc = p[..., :n_loc].reshape(B, NA_HEADS, GRID_W, kr, NA_COLS)
        p_ctx = p[..., n_loc:]
        o = jnp.einsum('bhjac,bajchd->bjhd', p_loc, vn) + jnp.einsum('bhjn,bnhd->bjhd', p_ctx, v_ctx)
        return o.reshape(B, GRID_W, NA_W)

    o = lax.map(one_row, jnp.arange(rows, dtype=jnp.int32))
    return jnp.moveaxis(o, 0, 1).reshape(B, S, NA_W)


def _dwconv(x, w, b):
    L = x.shape[1]
    left = (CONV_W - 1) // 2
    right = CONV_W - 1 - left
    xp = jnp.pad(x, ((0, 0), (left, right), (0, 0)))
    y = b
    for i in range(CONV_W):
        y = y + xp[:, i:i + L] * w[i]
    return y


def _blockdiag(x, w, b):
    xb = x.reshape(*x.shape[:-1], LRU_BLOCKS, LRU_BLOCK)
    return jnp.einsum('blnk,nkm->blnm', xb, w).reshape(x.shape) + b


def _lin_combine(left, right):
    return (left[0] * right[0], right[0] * left[1] + right[1])


def _rglru_dir(u, wr, br, wi, bi, lam, h0, reverse):
    if reverse:
        u = jnp.flip(u, axis=1)
    r = jax.nn.sigmoid(_blockdiag(u, wr, br)).astype(jnp.float32)
    i = jax.nn.sigmoid(_blockdiag(u, wi, bi)).astype(jnp.float32)
    log_a = -LRU_C * r * jax.nn.softplus(-lam.astype(jnp.float32))
    a = jnp.exp(log_a)
    b = jnp.sqrt(-jnp.expm1(2.0 * log_a)) * (i * u.astype(jnp.float32))
    a_cum, b_cum = lax.associative_scan(_lin_combine, (a, b), axis=1)
    h = b_cum if h0 is None else a_cum * h0[:, None] + b_cum
    final = h[:, -1]
    if reverse:
        h = jnp.flip(h, axis=1)
    return h, final


def _rglru_mixer(ul, gl, uc, gc, conv_w, conv_b, wr, br, wi, bi, lam, with_ctx_out):
    ul = _dwconv(ul, conv_w, conv_b)
    uc = _dwconv(uc, conv_w, conv_b)
    yl = None
    yc = None
    for d in range(2):
        hc, hc_final = _rglru_dir(uc, wr[d], br[d], wi[d], bi[d], lam[d], None, d == 1)
        hl, _ = _rglru_dir(ul, wr[d], br[d], wi[d], bi[d], lam[d], hc_final, d == 1)
        yl = hl if yl is None else yl + hl
        yc = hc if yc is None else yc + hc
    out_l = yl.astype(gl.dtype) * jax.nn.gelu(gl)
    out_c = yc.astype(gc.dtype) * jax.nn.gelu(gc) if with_ctx_out else None
    return out_l, out_c


def _hmoe(h, wg, bg, we, be, w_gate, w_up, w_down):
    T, D = h.shape
    lg = (h @ wg + bg).astype(jnp.float32)
    pg = jax.nn.softmax(lg, axis=-1)
    g_star = jnp.argmax(lg, axis=-1).astype(jnp.int32)
    g_gate = jnp.max(pg, axis=-1)
    le = (h @ we + be).astype(jnp.float32).reshape(T, MOE_GROUPS, MOE_EXPERTS_PER_GROUP)
    le_sel = jnp.take_along_axis(le, g_star[:, None, None], axis=1)[:, 0]
    vals, idx = lax.top_k(le_sel, MOE_TOP_K)
    w = (g_gate[:, None] * jax.nn.softmax(vals, axis=-1)).reshape(-1)
    e_ids = (g_star[:, None] * MOE_EXPERTS_PER_GROUP + idx).reshape(-1).astype(jnp.int32)
    tok = jnp.repeat(jnp.arange(T, dtype=jnp.int32), MOE_TOP_K)
    tk = T * MOE_TOP_K
    n_blocks = -(-(tk + MOE_EXPERTS * (MOE_BLOCK - 1)) // MOE_BLOCK)
    P = n_blocks * MOE_BLOCK
    order = jnp.argsort(e_ids)
    e_sorted = e_ids[order]
    counts = jnp.bincount(e_ids, length=MOE_EXPERTS)
    starts = jnp.cumsum(counts) - counts
    padded = (counts + MOE_BLOCK - 1) // MOE_BLOCK * MOE_BLOCK
    pad_ends = jnp.cumsum(padded)
    pad_starts = pad_ends - padded
    dest = pad_starts[e_sorted] + (jnp.arange(tk, dtype=jnp.int32) - starts[e_sorted])
    row_tok = jnp.full((P,), T, dtype=jnp.int32).at[dest].set(tok[order])
    row_w = jnp.zeros((P,), jnp.float32).at[dest].set(w[order])
    block_starts = jnp.arange(n_blocks, dtype=jnp.int32) * MOE_BLOCK
    block_e = jnp.minimum(jnp.searchsorted(pad_ends, block_starts, side='right'), MOE_EXPERTS - 1)
    xs = jnp.concatenate([h, jnp.zeros((1, D), h.dtype)], axis=0)[row_tok]
    xs = xs.reshape(n_blocks, MOE_BLOCK, D)

    def expert_block(args):
        xb, e = args
        return (jax.nn.silu(xb @ w_gate[e]) * (xb @ w_up[e])) @ w_down[e]

    yb = lax.map(expert_block, (xs, block_e)).reshape(P, D)
    y = jax.ops.segment_sum(yb * row_w[:, None].astype(yb.dtype), row_tok, num_segments=T + 1)
    return y[:T]


def setup_inputs(seed: int = 0) -> dict:
    key = jax.random.key(seed)
    ks = jax.random.split(key, 32)
    L = DEPTH
    f32 = jnp.float32

    def nrm(k, shape, scale):
        return jax.random.normal(k, shape, f32) * scale

    u = jax.random.uniform(ks[18], (L, 2, LRU_WIDTH), f32, 0.9, 0.999)
    return {
        'x': nrm(ks[0], (BATCH, SEQ, D_MODEL), 1.0),
        'c': nrm(ks[1], (BATCH, D_MODEL), 1.0),
        'ctx': nrm(ks[2], (BATCH, CTX_LEN, D_MODEL), 1.0),
        'c_ctx': nrm(ks[3], (D_MODEL,), 1.0),
        'ada_w': nrm(ks[4], (L, D_MODEL, 6 * D_MODEL), 0.5 * D_MODEL ** -0.5),
        'ada_b': nrm(ks[5], (L, 6 * D_MODEL), 0.01),
        'norm1_g': 1.0 + nrm(ks[6], (L, D_MODEL), 0.01),
        'norm2_g': 1.0 + nrm(ks[7], (L, D_MODEL), 0.01),
        'w_in': nrm(ks[8], (L, D_MODEL, IN_WIDTH), D_MODEL ** -0.5),
        'w_out': nrm(ks[9], (L, MIX_WIDTH, D_MODEL), MIX_WIDTH ** -0.5),
        'att_q_norm': 1.0 + nrm(ks[10], (L, HEAD_DIM), 0.01),
        'att_k_norm': 1.0 + nrm(ks[11], (L, HEAD_DIM), 0.01),
        'conv_w': nrm(ks[12], (L, CONV_W, LRU_WIDTH), CONV_W ** -0.5),
        'conv_b': nrm(ks[13], (L, LRU_WIDTH), 0.01),
        'lru_wr': nrm(ks[14], (L, 2, LRU_BLOCKS, LRU_BLOCK, LRU_BLOCK), LRU_BLOCK ** -0.5),
        'lru_br': nrm(ks[15], (L, 2, LRU_WIDTH), 0.01),
        'lru_wi': nrm(ks[16], (L, 2, LRU_BLOCKS, LRU_BLOCK, LRU_BLOCK), LRU_BLOCK ** -0.5),
        'lru_bi': nrm(ks[17], (L, 2, LRU_WIDTH), 0.01),
        'lru_lambda': jnp.log(u) - jnp.log1p(-u),
        'na_rpb': nrm(ks[19], (L, NA_HEADS, 2 * NA_ROWS - 1, 2 * NA_COLS - 1), 0.1),
        'router_wg': nrm(ks[20], (L, D_MODEL, MOE_GROUPS), D_MODEL ** -0.5),
        'router_bg': nrm(ks[21], (L, MOE_GROUPS), 0.01),
        'router_we': nrm(ks[22], (L, D_MODEL, MOE_EXPERTS), D_MODEL ** -0.5),
        'router_be': nrm(ks[23], (L, MOE_EXPERTS), 0.01),
        'moe_w_gate': nrm(ks[24], (L, MOE_EXPERTS, D_MODEL, MOE_HIDDEN), D_MODEL ** -0.5),
        'moe_w_up': nrm(ks[25], (L, MOE_EXPERTS, D_MODEL, MOE_HIDDEN), D_MODEL ** -0.5),
        'moe_w_down': nrm(ks[26], (L, MOE_EXPERTS, MOE_HIDDEN, D_MODEL), MOE_HIDDEN ** -0.5),
        'final_g': 1.0 + nrm(ks[27], (D_MODEL,), 0.01),
    }


def reference(x, c, ctx, c_ctx, ada_w, ada_b, norm1_g, norm2_g, w_in, w_out, att_q_norm, att_k_norm,
              conv_w, conv_b, lru_wr, lru_br, lru_wi, lru_bi, lru_lambda, na_rpb,
              router_wg, router_bg, router_we, router_be, moe_w_gate, moe_w_up, moe_w_down, final_g):
    B, S, D = x.shape
    C = ctx.shape[1]
    g_att = ATT_Q_HEADS // ATT_KV_HEADS
    cos, sin = _axial_rope_tables(S)
    s_lat = jax.nn.silu(c)
    s_ctx = jax.nn.silu(c_ctx)
    xc = ctx
    for l in range(DEPTH):
        update_ctx = l < DEPTH - 1
        m_lat = jnp.split((s_lat @ ada_w[l] + ada_b[l])[:, None, :], 6, axis=-1)
        m_ctx = jnp.split(s_ctx @ ada_w[l] + ada_b[l], 6, axis=-1)
        h = _rmsnorm(x, norm1_g[l]) * (1.0 + m_lat[1]) + m_lat[0]
        hc = _rmsnorm(xc, norm1_g[l]) * (1.0 + m_ctx[1]) + m_ctx[0]
        qa, ka, va, ub, gb, qn, kn, vn = _split_cols(h @ w_in[l])
        qa_c, ka_c, va_c, ub_c, gb_c, qn_c, kn_c, vn_c = _split_cols(hc @ w_in[l])

        qa = _rope(_rmsnorm(_heads(qa, ATT_Q_HEADS), att_q_norm[l]), cos, sin)
        ka = _rope(_rmsnorm(_heads(ka, ATT_KV_HEADS), att_k_norm[l]), cos, sin)
        ka_c = _rmsnorm(_heads(ka_c, ATT_KV_HEADS), att_k_norm[l])
        va_c = _heads(va_c, ATT_KV_HEADS)
        oa = _gqa_latent(qa, ka, _heads(va, ATT_KV_HEADS), ka_c, va_c)

        ob, ob_c = _rglru_mixer(ub, gb, ub_c, gb_c, conv_w[l], conv_b[l], lru_wr[l], lru_br[l],
                                lru_wi[l], lru_bi[l], lru_lambda[l], update_ctx)

        kn_c = _heads(kn_c, NA_HEADS)
        vn_c = _heads(vn_c, NA_HEADS)
        oc = _neighbourhood_latent(_heads(qn, NA_HEADS), _heads(kn, NA_HEADS), _heads(vn, NA_HEADS),
                                   kn_c, vn_c, na_rpb[l])

        x = x + m_lat[2] * (jnp.concatenate([oa, ob, oc], axis=-1) @ w_out[l])
        h2 = _rmsnorm(x, norm2_g[l]) * (1.0 + m_lat[4]) + m_lat[3]
        tokens = h2.reshape(B * S, D)
        if update_ctx:
            qa_c = _rmsnorm(_heads(qa_c, ATT_Q_HEADS), att_q_norm[l])
            oa_c = _gqa_attend(qa_c.reshape(B, C, ATT_KV_HEADS, g_att, HEAD_DIM), ka_c, va_c)
            oc_c = _gqa_attend(_heads(qn_c, NA_HEADS)[:, :, :, None], kn_c, vn_c)
            o_c = jnp.concatenate([oa_c.reshape(B, C, ATT_Q_W), ob_c, oc_c.reshape(B, C, NA_W)], axis=-1)
            xc = xc + m_ctx[2] * (o_c @ w_out[l])
            h2c = _rmsnorm(xc, norm2_g[l]) * (1.0 + m_ctx[4]) + m_ctx[3]
            y = _hmoe(jnp.concatenate([tokens, h2c.reshape(B * C, D)], axis=0), router_wg[l], router_bg[l],
                      router_we[l], router_be[l], moe_w_gate[l], moe_w_up[l], moe_w_down[l])
            x = x + m_lat[5] * y[:B * S].reshape(B, S, D)
            xc = xc + m_ctx[5] * y[B * S:].reshape(B, C, D)
        else:
            y = _hmoe(tokens, router_wg[l], router_bg[l], router_we[l], router_be[l],
                      moe_w_gate[l], moe_w_up[l], moe_w_down[l])
            x = x + m_lat[5] * y.reshape(B, S, D)
    return _rmsnorm(x, final_g)
```

```python
import functools

import numpy as np
import jax
import jax.numpy as jnp
from jax import lax
from jax.experimental import pallas as pl
from jax.experimental.pallas import tpu as pltpu

F32 = jnp.float32
BF16 = jnp.bfloat16

GRID_W = 64
HEAD_DIM = 128
ATT_Q_HEADS = 4
ATT_KV_HEADS = 2
LRU_WIDTH = 1024
LRU_BLOCKS = 8
LRU_C = 8.0
CONV_W = 4
NA_HEADS = 4
NA_ROWS = 8
NA_COLS = 16
ATT_Q_W = ATT_Q_HEADS * HEAD_DIM
ATT_KV_W = ATT_KV_HEADS * HEAD_DIM
NA_W = NA_HEADS * HEAD_DIM
MOE_GROUPS = 4
MOE_EXPERTS_PER_GROUP = 8
MOE_EXPERTS = MOE_GROUPS * MOE_EXPERTS_PER_GROUP
MOE_HIDDEN = 1024
ROPE_THETA = 10000.0
EPS = 1e-6

COL_QA = 0
COL_KA = COL_QA + ATT_Q_W
COL_VA = COL_KA + ATT_KV_W
COL_UB = COL_VA + ATT_KV_W
COL_GB = COL_UB + LRU_WIDTH
COL_QN = COL_GB + LRU_WIDTH
COL_KN = COL_QN + NA_W
COL_VN = COL_KN + NA_W
IN_WIDTH = COL_VN + NA_W

TM = 256
MOE_BM = 256
NA_QROWS = 4
NA_WROWS = 12
NEG_BIAS = -1e30
VMEM_LIMIT = 56 * 1024 * 1024


def _cparams(sem, vmem=None):
    return pltpu.CompilerParams(dimension_semantics=sem, vmem_limit_bytes=vmem)


def _rms(x):
    return x * lax.rsqrt(jnp.mean(x * x, axis=-1, keepdims=True) + EPS)


def _adaln_kernel(c_ref, w_ref, b_ref, o_ref):
    cv = c_ref[...]
    s = cv * jax.nn.sigmoid(cv)
    o_ref[0] = jnp.dot(s.astype(BF16), w_ref[0].astype(BF16), preferred_element_type=F32) + b_ref[0]


def _adaln(cvec, ada_w, ada_b):
    L, D, N = ada_w.shape
    tn = 1024
    return pl.pallas_call(
        _adaln_kernel,
        grid=(L, N // tn),
        in_specs=[
            pl.BlockSpec((8, D), lambda l, j: (0, 0)),
            pl.BlockSpec((1, D, tn), lambda l, j: (l, 0, j)),
            pl.BlockSpec((1, 1, tn), lambda l, j: (l, 0, j)),
        ],
        out_specs=pl.BlockSpec((1, 8, tn), lambda l, j: (l, 0, j)),
        out_shape=jax.ShapeDtypeStruct((L, 8, N), F32),
        compiler_params=_cparams(("parallel", "parallel"), VMEM_LIMIT),
        name="adaln",
    )(cvec, ada_w, ada_b.reshape(L, 1, N))


def _inproj_kernel(x_ref, mod_ref, g_ref, w_ref, o_ref, hn_ref):
    @pl.when(pl.program_id(1) == 0)
    def _():
        h = _rms(x_ref[...]) * g_ref[...]
        h = h * (1.0 + mod_ref[0, 1:2, :]) + mod_ref[0, 0:1, :]
        hn_ref[...] = h.astype(BF16)

    o_ref[...] = jnp.dot(hn_ref[...], w_ref[...], preferred_element_type=F32)


def _inproj(x_all, mod, g1, w_in_bf, seg_of):
    M, D = x_all.shape
    N = w_in_bf.shape[1]
    tn = 512
    return pl.pallas_call(
        _inproj_kernel,
        grid=(M // TM, N // tn),
        in_specs=[
            pl.BlockSpec((TM, D), lambda i, j: (i, 0)),
            pl.BlockSpec((1, 6, D), lambda i, j: (seg_of(i), 0, 0)),
            pl.BlockSpec((1, D), lambda i, j: (0, 0)),
            pl.BlockSpec((D, tn), lambda i, j: (0, j)),
        ],
        out_specs=pl.BlockSpec((TM, tn), lambda i, j: (i, j)),
        out_shape=jax.ShapeDtypeStruct((M, N), F32),
        scratch_shapes=[pltpu.VMEM((TM, D), BF16)],
        compiler_params=_cparams(("parallel", "arbitrary"), VMEM_LIMIT),
        name="inproj",
    )(x_all, mod, g1.reshape(1, D), w_in_bf)


def _norm_rope(t, g, cos, sin):
    y = _rms(t) * g
    lane = lax.broadcasted_iota(jnp.int32, y.shape, 1)
    first_half = (lane % 64) < 32
    partner = jnp.where(first_half, pltpu.roll(y, 96, 1), pltpu.roll(y, 32, 1))
    return y * cos + partner * sin


def _qkprep_kernel(p_ref, cos_ref, sin_ref, qg_ref, kg_ref, q_ref, k_ref, v_ref):
    cos = cos_ref[...]
    sin = sin_ref[...]
    scale = HEAD_DIM ** -0.5
    for h in range(ATT_Q_HEADS):
        t = p_ref[:, COL_QA + h * HEAD_DIM:COL_QA + (h + 1) * HEAD_DIM]
        q_ref[:, h * HEAD_DIM:(h + 1) * HEAD_DIM] = (_norm_rope(t, qg_ref[...], cos, sin) * scale).astype(BF16)
    for h in range(ATT_KV_HEADS):
        t = p_ref[:, COL_KA + h * HEAD_DIM:COL_KA + (h + 1) * HEAD_DIM]
        k_ref[0, :, h * HEAD_DIM:(h + 1) * HEAD_DIM] = _norm_rope(t, kg_ref[...], cos, sin).astype(BF16)
    v_ref[0] = p_ref[:, COL_VA:COL_VA + ATT_KV_W].astype(BF16)


def _qkprep(proj, cos_t, sin_t, qg, kg, B, S):
    M = proj.shape[0]
    nbs = S // TM
    nlat = B * nbs

    def b_of(i):
        return jnp.where(i < nlat, i // nbs, i - nlat)

    def pos_of(i):
        return jnp.where(i < nlat, i % nbs, nbs)

    kl = S + TM
    return pl.pallas_call(
        _qkprep_kernel,
        grid=(M // TM,),
        in_specs=[
            pl.BlockSpec((TM, COL_UB), lambda i: (i, 0)),
            pl.BlockSpec((TM, HEAD_DIM), lambda i: (pos_of(i), 0)),
            pl.BlockSpec((TM, HEAD_DIM), lambda i: (pos_of(i), 0)),
            pl.BlockSpec((1, HEAD_DIM), lambda i: (0, 0)),
            pl.BlockSpec((1, HEAD_DIM), lambda i: (0, 0)),
        ],
        out_specs=[
            pl.BlockSpec((TM, ATT_Q_W), lambda i: (i, 0)),
            pl.BlockSpec((1, TM, ATT_KV_W), lambda i: (b_of(i), pos_of(i), 0)),
            pl.BlockSpec((1, TM, ATT_KV_W), lambda i: (b_of(i), pos_of(i), 0)),
        ],
        out_shape=[
            jax.ShapeDtypeStruct((M, ATT_Q_W), BF16),
            jax.ShapeDtypeStruct((B, kl, ATT_KV_W), BF16),
            jax.ShapeDtypeStruct((B, kl, ATT_KV_W), BF16),
        ],
        compiler_params=_cparams(("parallel",), VMEM_LIMIT),
        name="qkprep",
    )(proj, cos_t, sin_t, qg.reshape(1, HEAD_DIM), kg.reshape(1, HEAD_DIM))


def _gqa_kernel(q_ref, k_ref, v_ref, o_in_ref, o_ref, *, tk):
    del o_in_ref
    tq = q_ref.shape[0]
    q2 = jnp.concatenate([q_ref[:, :HEAD_DIM], q_ref[:, HEAD_DIM:]], axis=0)
    nk = k_ref.shape[1] // tk

    def body(c, carry):
        m, l, acc = carry
        off = pl.multiple_of(c * tk, tk)
        kc = k_ref[0, pl.ds(off, tk), :]
        vc = v_ref[0, pl.ds(off, tk), :]
        s = lax.dot_general(q2, kc, (((1,), (1,)), ((), ())), preferred_element_type=F32)
        m_new = jnp.maximum(m, jnp.max(s, axis=-1, keepdims=True))
        p = jnp.exp(s - m_new)
        alpha = jnp.exp(m - m_new)
        l = alpha * l + jnp.sum(p, axis=-1, keepdims=True)
        acc = alpha * acc + jnp.dot(p.astype(BF16), vc, preferred_element_type=F32)
        return m_new, l, acc

    init = (jnp.full((2 * tq, 1), -jnp.inf, F32), jnp.zeros((2 * tq, 1), F32),
            jnp.zeros((2 * tq, HEAD_DIM), F32))
    _, l, acc = lax.fori_loop(0, nk, body, init)
    o = acc / l
    o_ref[:, :HEAD_DIM] = o[:tq].astype(BF16)
    o_ref[:, HEAD_DIM:] = o[tq:].astype(BF16)


def _gqa_latent(q, k_all, v_all, B, S):
    M = q.shape[0]
    kl = k_all.shape[1]
    nbs = S // TM
    tk = 768 if kl % 768 == 0 else TM
    g = ATT_Q_HEADS // ATT_KV_HEADS
    return pl.pallas_call(
        functools.partial(_gqa_kernel, tk=tk),
        grid=(B, ATT_KV_HEADS, nbs),
        in_specs=[
            pl.BlockSpec((TM, g * HEAD_DIM), lambda b, h, i: (b * nbs + i, h)),
            pl.BlockSpec((1, kl, HEAD_DIM), lambda b, h, i: (b, 0, h)),
            pl.BlockSpec((1, kl, HEAD_DIM), lambda b, h, i: (b, 0, h)),
            pl.BlockSpec(memory_space=pl.ANY),
        ],
        out_specs=pl.BlockSpec((TM, g * HEAD_DIM), lambda b, h, i: (b * nbs + i, h)),
        out_shape=jax.ShapeDtypeStruct((M, ATT_Q_W), BF16),
        input_output_aliases={3: 0},
        compiler_params=_cparams(("parallel", "parallel", "parallel"), VMEM_LIMIT),
        name="gqa_latent",
    )(q, k_all, v_all, jnp.zeros((M, ATT_Q_W), BF16))


def _ctx_attn_kernel(q_ref, k_ref, v_ref, o_in_ref, o_ref, *, scale):
    del o_in_ref
    q = (q_ref[...].astype(F32) * scale).astype(BF16)
    k = k_ref[...].astype(BF16)
    v = v_ref[...].astype(BF16)
    s = lax.dot_general(q, k, (((1,), (1,)), ((), ())), preferred_element_type=F32)
    p = jnp.exp(s - jnp.max(s, axis=-1, keepdims=True))
    l = jnp.sum(p, axis=-1, keepdims=True)
    o = jnp.dot(p.astype(BF16), v, preferred_element_type=F32) / l
    o_ref[...] = o.astype(BF16)


def _ctx_attn(q_arr, q_map, k_arr, k_spec, v_arr, v_spec, o_arr, B, n_heads, ctx_blk0, scale):
    return pl.pallas_call(
        functools.partial(_ctx_attn_kernel, scale=scale),
        grid=(B, n_heads),
        in_specs=[
            pl.BlockSpec((TM, HEAD_DIM), q_map),
            k_spec,
            v_spec,
            pl.BlockSpec(memory_space=pl.ANY),
        ],
        out_specs=pl.BlockSpec((TM, HEAD_DIM), lambda b, h: (ctx_blk0 + b, h)),
        out_shape=jax.ShapeDtypeStruct(o_arr.shape, o_arr.dtype),
        input_output_aliases={3: 0},
        compiler_params=_cparams(("parallel", "parallel"), VMEM_LIMIT),
        name="ctx_attn",
    )(q_arr, k_arr, v_arr, o_arr)


def _lru_kernel(u_ref, up_ref, un_ref, *rest, reverse, final):
    if final:
        (g_ref, hrev_ref, cw_ref, cb_ref, wr_ref, br_ref, wi_ref, bi_ref, sp_ref,
         o_ref, xpad, a_scr, b_scr, h_scr, hst) = rest
    else:
        (cw_ref, cb_ref, wr_ref, br_ref, wi_ref, bi_ref, sp_ref,
         o_ref, xpad, a_scr, b_scr, h_scr, hst) = rest
    s = pl.program_id(1)
    ns = pl.num_programs(1)
    T = u_ref.shape[0]
    W = u_ref.shape[1]

    @pl.when(s == 0)
    def _():
        hst[...] = jnp.zeros_like(hst)

    if reverse:
        j = ns - 1 - s
    else:
        j = s - 1
    is_first = jnp.logical_or(s == 0, j == 0)
    is_last = jnp.logical_or(s == 0, j == ns - 2)
    prev_rows = jnp.where(is_first, 0.0, up_ref[...])
    next_rows = jnp.where(is_last, 0.0, un_ref[...])
    xpad[0:8, :] = prev_rows
    xpad[8:8 + T, :] = u_ref[...]
    xpad[8 + T:16 + T, :] = next_rows
    u = (cb_ref[...]
         + xpad[7:7 + T, :] * cw_ref[0:1, :]
         + xpad[8:8 + T, :] * cw_ref[1:2, :]
         + xpad[9:9 + T, :] * cw_ref[2:3, :]
         + xpad[10:10 + T, :] * cw_ref[3:4, :])

    ub = u.astype(BF16)
    nb = LRU_WIDTH // LRU_BLOCKS
    for n in range(LRU_BLOCKS):
        sl = slice(n * nb, (n + 1) * nb)
        un_ = ub[:, sl]
        r = jax.nn.sigmoid(jnp.dot(un_, wr_ref[n], preferred_element_type=F32) + br_ref[:, sl])
        ig = jax.nn.sigmoid(jnp.dot(un_, wi_ref[n], preferred_element_type=F32) + bi_ref[:, sl])
        log_a = (-LRU_C) * r * sp_ref[:, sl]
        a = jnp.exp(log_a)
        a_scr[:, sl] = a
        one_minus_a2 = -jnp.tanh(log_a) * (a * a + 1.0)
        b_scr[:, sl] = jnp.sqrt(one_minus_a2) * (ig * u[:, sl])

    def step(t, h):
        row = (T - 1 - t) if reverse else t
        h = a_scr[pl.ds(row, 1), :] * h + b_scr[pl.ds(row, 1), :]
        h_scr[pl.ds(row, 1), :] = h
        return h

    hst[...] = lax.fori_loop(0, T, step, hst[...], unroll=8)

    if final:
        y = h_scr[...] + hrev_ref[...]
        o_ref[...] = (y * jax.nn.gelu(g_ref[...])).astype(o_ref.dtype)
    else:
        o_ref[...] = h_scr[...]
    del W


def _lru_dir(proj, hrev, conv_w, conv_b, wr, br, wi, bi, sp, B, S, reverse):
    M = proj.shape[0]
    nbs = S // TM
    nlat = B * nbs
    final = hrev is not None
    ub_blk = COL_UB // LRU_WIDTH
    gb_blk = COL_GB // LRU_WIDTH
    r8 = TM // 8
    n8 = M // 8

    def blk(b, s):
        j = (nbs - s) if reverse else (s - 1)
        return jnp.where(s == 0, nlat + b, b * nbs + j)

    in_specs = [
        pl.BlockSpec((TM, LRU_WIDTH), lambda b, s: (blk(b, s), ub_blk)),
        pl.BlockSpec((8, LRU_WIDTH), lambda b, s: (jnp.maximum(blk(b, s) * r8 - 1, 0), ub_blk)),
        pl.BlockSpec((8, LRU_WIDTH), lambda b, s: (jnp.minimum((blk(b, s) + 1) * r8, n8 - 1), ub_blk)),
    ]
    args = [proj, proj, proj]
    if final:
        in_specs += [
            pl.BlockSpec((TM, LRU_WIDTH), lambda b, s: (blk(b, s), gb_blk)),
            pl.BlockSpec((TM, LRU_WIDTH), lambda b, s: (blk(b, s), 0)),
        ]
        args += [proj, hrev]
    const2 = lambda b, s: (0, 0)
    in_specs += [
        pl.BlockSpec((CONV_W, LRU_WIDTH), const2),
        pl.BlockSpec((1, LRU_WIDTH), const2),
        pl.BlockSpec((LRU_BLOCKS, LRU_WIDTH // LRU_BLOCKS, LRU_WIDTH // LRU_BLOCKS), lambda b, s: (0, 0, 0)),
        pl.BlockSpec((1, LRU_WIDTH), const2),
        pl.BlockSpec((LRU_BLOCKS, LRU_WIDTH // LRU_BLOCKS, LRU_WIDTH // LRU_BLOCKS), lambda b, s: (0, 0, 0)),
        pl.BlockSpec((1, LRU_WIDTH), const2),
        pl.BlockSpec((1, LRU_WIDTH), const2),
    ]
    args += [conv_w, conv_b.reshape(1, -1), wr.astype(BF16), br.reshape(1, -1), wi.astype(BF16),
             bi.reshape(1, -1), sp.reshape(1, -1)]
    return pl.pallas_call(
        functools.partial(_lru_kernel, reverse=reverse, final=final),
        grid=(B, nbs + 1),
        in_specs=in_specs,
        out_specs=pl.BlockSpec((TM, LRU_WIDTH), lambda b, s: (blk(b, s), 0)),
        out_shape=jax.ShapeDtypeStruct((M, LRU_WIDTH), BF16 if final else F32),
        scratch_shapes=[
            pltpu.VMEM((TM + 16, LRU_WIDTH), F32),
            pltpu.VMEM((TM, LRU_WIDTH), F32),
            pltpu.VMEM((TM, LRU_WIDTH), F32),
            pltpu.VMEM((TM, LRU_WIDTH), F32),
            pltpu.VMEM((1, LRU_WIDTH), F32),
        ],
        compiler_params=_cparams(("parallel", "arbitrary"), VMEM_LIMIT),
        name="lru_fwd" if final else "lru_rev",
    )(*args)


def _na_kernel(q_ref, k0_ref, k1_ref, k2_ref, v0_ref, v1_ref, v2_ref, kc_ref, vc_ref, tl_ref, tr_ref,
               o_in_ref, o_ref, s_scr, *, n_rows):
    del o_in_ref
    i = pl.program_id(1)
    nbs = n_rows // NA_QROWS
    r0 = NA_QROWS * i
    ws = NA_QROWS * jnp.clip(i - 1, 0, nbs - 3)
    scale = HEAD_DIM ** -0.5
    nloc = NA_WROWS * GRID_W
    for h in range(NA_HEADS):
        hs = slice(h * HEAD_DIM, (h + 1) * HEAD_DIM)
        qh = (q_ref[:, hs] * scale).astype(BF16)
        kh = jnp.concatenate([k0_ref[:, hs], k1_ref[:, hs], k2_ref[:, hs], kc_ref[:, hs]], axis=0).astype(BF16)
        vh = jnp.concatenate([v0_ref[:, hs], v1_ref[:, hs], v2_ref[:, hs], vc_ref[:, hs]], axis=0).astype(BF16)
        s_scr[...] = lax.dot_general(qh, kh, (((1,), (1,)), ((), ())), preferred_element_type=F32)
        for qr in range(NA_QROWS):
            r = r0 + qr
            rs = jnp.clip(r - NA_ROWS // 2, 0, n_rows - NA_ROWS)

            def tile_of(a):
                kr = ws + a
                valid = jnp.logical_and(kr >= rs, kr < rs + NA_ROWS)
                return jnp.where(valid, kr - r + (NA_ROWS - 1), 2 * NA_ROWS - 1)

            for pr in range(NA_WROWS // 2):
                bias = tl_ref[h, tile_of(2 * pr)] + tr_ref[h, tile_of(2 * pr + 1)]
                rsl = slice(qr * GRID_W, (qr + 1) * GRID_W)
                csl = slice(pr * 2 * GRID_W, (pr + 1) * 2 * GRID_W)
                s_scr[rsl, csl] = s_scr[rsl, csl] + bias
        s = s_scr[...]
        p = jnp.exp(s - jnp.max(s, axis=-1, keepdims=True))
        l = jnp.sum(p, axis=-1, keepdims=True)
        o = jnp.dot(p.astype(BF16), vh, preferred_element_type=F32) / l
        o_ref[:, hs] = o.astype(BF16)
    del nloc


def _na_latent(proj, tl, tr, B, S):
    M = proj.shape[0]
    nbs = S // TM
    nlat = B * nbs
    n_rows = S // GRID_W
    qb, kb, vb = COL_QN // NA_W, COL_KN // NA_W, COL_VN // NA_W

    def wblk(b, i, t):
        return b * nbs + jnp.clip(i - 1, 0, nbs - 3) + t

    blk = (TM, NA_W)
    in_specs = [pl.BlockSpec(blk, lambda b, i: (b * nbs + i, qb))]
    in_specs += [pl.BlockSpec(blk, functools.partial(lambda b, i, t: (wblk(b, i, t), kb), t=t)) for t in range(3)]
    in_specs += [pl.BlockSpec(blk, functools.partial(lambda b, i, t: (wblk(b, i, t), vb), t=t)) for t in range(3)]
    in_specs += [
        pl.BlockSpec(blk, lambda b, i: (nlat + b, kb)),
        pl.BlockSpec(blk, lambda b, i: (nlat + b, vb)),
        pl.BlockSpec(tl.shape, lambda b, i: (0, 0, 0, 0)),
        pl.BlockSpec(tr.shape, lambda b, i: (0, 0, 0, 0)),
        pl.BlockSpec(memory_space=pl.ANY),
    ]
    return pl.pallas_call(
        functools.partial(_na_kernel, n_rows=n_rows),
        grid=(B, nbs),
        in_specs=in_specs,
        out_specs=pl.BlockSpec(blk, lambda b, i: (b * nbs + i, 0)),
        out_shape=jax.ShapeDtypeStruct((M, NA_W), BF16),
        input_output_aliases={11: 0},
        scratch_shapes=[pltpu.VMEM((TM, NA_WROWS * GRID_W + TM), F32)],
        compiler_params=_cparams(("parallel", "parallel"), VMEM_LIMIT),
        name="na_latent",
    )(proj, proj, proj, proj, proj, proj, proj, proj, proj, tl, tr, jnp.zeros((M, NA_W), BF16))


def _na_bias_tables(rpb):
    j = np.arange(GRID_W)
    cs = np.clip(j - NA_COLS // 2, 0, GRID_W - NA_COLS)
    kc = np.arange(GRID_W)
    inside = (kc[None, :] >= cs[:, None]) & (kc[None, :] < cs[:, None] + NA_COLS)
    dc = np.clip(kc[None, :] - j[:, None] + (NA_COLS - 1), 0, 2 * NA_COLS - 2)
    t = jnp.where(inside[None, None], rpb[:, :, dc], NEG_BIAS)
    t = jnp.concatenate([t, jnp.full_like(t[:, :1], NEG_BIAS)], axis=1)
    z = jnp.zeros_like(t)
    return jnp.concatenate([t, z], axis=-1), jnp.concatenate([z, t], axis=-1)


def _outproj_kernel(oa_ref, ob_ref, oc_ref, x_ref, mod_ref, g_ref, w_ref, wr_ref, br_ref,
                    xo_ref, h2_ref, lg_ref):
    k1 = ATT_Q_W
    k2 = k1 + LRU_WIDTH
    acc = jnp.dot(oa_ref[...], w_ref[0:k1, :], preferred_element_type=F32)
    acc = acc + jnp.dot(ob_ref[...], w_ref[k1:k2, :], preferred_element_type=F32)
    acc = acc + jnp.dot(oc_ref[...], w_ref[k2:, :], preferred_element_type=F32)
    xn = x_ref[...] + mod_ref[0, 2:3, :] * acc
    xo_ref[...] = xn
    h2 = _rms(xn) * g_ref[...]
    h2 = h2 * (1.0 + mod_ref[0, 4:5, :]) + mod_ref[0, 3:4, :]
    h2_ref[...] = h2
    lg_ref[...] = jnp.dot(h2, wr_ref[...], preferred_element_type=F32,
                          precision=lax.Precision.HIGHEST) + br_ref[...]


def _outproj(oa, ob, oc, x_all, mod, g2, w_out_bf, w_router, b_router, seg_of, nblk):
    M, D = x_all.shape
    return pl.pallas_call(
        _outproj_kernel,
        grid=(nblk,),
        in_specs=[
            pl.BlockSpec((TM, ATT_Q_W), lambda i: (i, 0)),
            pl.BlockSpec((TM, LRU_WIDTH), lambda i: (i, 0)),
            pl.BlockSpec((TM, NA_W), lambda i: (i, 0)),
            pl.BlockSpec((TM, D), lambda i: (i, 0)),
            pl.BlockSpec((1, 6, D), lambda i: (seg_of(i), 0, 0)),
            pl.BlockSpec((1, D), lambda i: (0, 0)),
            pl.BlockSpec(w_out_bf.shape, lambda i: (0, 0)),
            pl.BlockSpec(w_router.shape, lambda i: (0, 0)),
            pl.BlockSpec((1, 128), lambda i: (0, 0)),
        ],
        out_specs=[
            pl.BlockSpec((TM, D), lambda i: (i, 0)),
            pl.BlockSpec((TM, D), lambda i: (i, 0)),
            pl.BlockSpec((TM, 128), lambda i: (i, 0)),
        ],
        out_shape=[
            jax.ShapeDtypeStruct((M, D), F32),
            jax.ShapeDtypeStruct((nblk * TM, D), F32),
            jax.ShapeDtypeStruct((nblk * TM, 128), F32),
        ],
        input_output_aliases={3: 0},
        compiler_params=_cparams(("parallel",), VMEM_LIMIT),
        name="outproj",
    )(oa, ob, oc, x_all, mod, g2.reshape(1, D), w_out_bf, w_router, b_router)


def _route_kernel(lg_ref, o_ref):
    x = lg_ref[...]
    lane = lax.broadcasted_iota(jnp.int32, x.shape, 1)
    big = jnp.int32(1 << 20)
    ninf = -jnp.inf

    def first_argmax(vals):
        m = jnp.max(vals, axis=-1, keepdims=True)
        idx = jnp.min(jnp.where(vals == m, lane, big), axis=-1, keepdims=True)
        return m, idx

    lg = jnp.where(lane < MOE_GROUPS, x, ninf)
    mg, g_star = first_argmax(lg)
    g_gate = 1.0 / jnp.sum(jnp.exp(lg - mg), axis=-1, keepdims=True)
    lo = MOE_GROUPS + MOE_EXPERTS_PER_GROUP * g_star
    le = jnp.where(jnp.logical_and(lane >= lo, lane < lo + MOE_EXPERTS_PER_GROUP), x, ninf)
    v1, i1 = first_argmax(le)
    v2, i2 = first_argmax(jnp.where(lane == i1, ninf, le))
    e21 = jnp.exp(v2 - v1)
    w1 = g_gate / (1.0 + e21)
    w2 = g_gate * e21 / (1.0 + e21)
    e1 = (i1 - MOE_GROUPS).astype(F32)
    e2 = (i2 - MOE_GROUPS).astype(F32)
    o_ref[...] = jnp.where(lane == 0, e1, jnp.where(lane == 1, e2, jnp.where(lane == 2, w1,
                           jnp.where(lane == 3, w2, 0.0))))


def _route(logits, nblk):
    M = logits.shape[0]
    return pl.pallas_call(
        _route_kernel,
        grid=(nblk,),
        in_specs=[pl.BlockSpec((TM, 128), lambda i: (i, 0))],
        out_specs=pl.BlockSpec((TM, 128), lambda i: (i, 0)),
        out_shape=jax.ShapeDtypeStruct((M, 128), F32),
        compiler_params=_cparams(("parallel",)),
        name="route",
    )(logits)


def _rank_kernel(r_ref, rank_ref, cnt_ref, carry):
    @pl.when(pl.program_id(0) == 0)
    def _():
        carry[...] = jnp.zeros_like(carry)

    r = r_ref[...]
    lane = lax.broadcasted_iota(jnp.int32, r.shape, 1).astype(F32)
    oh0 = jnp.where(lane == r[:, 0:1], 1.0, 0.0)
    oh1 = jnp.where(lane == r[:, 1:2], 1.0, 0.0)
    oh = oh0 + oh1
    n = r.shape[0]
    row = lax.broadcasted_iota(jnp.int32, (n, n), 0)
    col = lax.broadcasted_iota(jnp.int32, (n, n), 1)
    tri = jnp.where(col < row, 1.0, 0.0).astype(BF16)
    tot = carry[...] + jnp.dot(tri, oh.astype(BF16), preferred_element_type=F32)
    rank0 = jnp.sum(oh0 * tot, axis=-1, keepdims=True)
    rank1 = jnp.sum(oh1 * tot, axis=-1, keepdims=True)
    lane_i = lax.broadcasted_iota(jnp.int32, r.shape, 1)
    rank_ref[...] = jnp.where(lane_i == 0, rank0, jnp.where(lane_i == 1, rank1, 0.0))
    carry[...] = carry[...] + jnp.sum(oh, axis=0, keepdims=True)
    cnt_ref[...] = jnp.broadcast_to(carry[...], cnt_ref.shape)


def _rank(route, nblk):
    M = route.shape[0]
    return pl.pallas_call(
        _rank_kernel,
        grid=(nblk,),
        in_specs=[pl.BlockSpec((TM, 128), lambda i: (i, 0))],
        out_specs=[pl.BlockSpec((TM, 128), lambda i: (i, 0)), pl.BlockSpec((8, 128), lambda i: (0, 0))],
        out_shape=[jax.ShapeDtypeStruct((M, 128), F32), jax.ShapeDtypeStruct((8, 128), F32)],
        scratch_shapes=[pltpu.VMEM((1, 128), F32)],
        compiler_params=_cparams(("arbitrary",)),
        name="rank",
    )(route)


def _dispatch_kernel(dest_ref, h_hbm, xs_in, xs_hbm, sem):
    del xs_in
    i = pl.program_id(0)
    n = pl.num_programs(0)

    def row_copy(tok, d, slot):
        return pltpu.make_async_copy(h_hbm.at[pl.ds(tok, 1)], xs_hbm.at[pl.ds(d, 1)], sem.at[slot])

    def issue(t, c):
        tok = i * TM + t
        row_copy(tok, dest_ref[2 * tok], i % 2).start()
        row_copy(tok, dest_ref[2 * tok + 1], i % 2).start()
        return c

    lax.fori_loop(0, TM, issue, 0)

    def drain(slot):
        def w(t, c):
            row_copy(0, 0, slot).wait()
            row_copy(0, 0, slot).wait()
            return c
        lax.fori_loop(0, TM, w, 0)

    @pl.when(i > 0)
    def _():
        drain((i + 1) % 2)

    @pl.when(i == n - 1)
    def _():
        drain(i % 2)


def _dispatch(dest_flat, h2, xs_zero, nblk):
    return pl.pallas_call(
        _dispatch_kernel,
        grid_spec=pltpu.PrefetchScalarGridSpec(
            num_scalar_prefetch=1,
            grid=(nblk,),
            in_specs=[pl.BlockSpec(memory_space=pl.ANY), pl.BlockSpec(memory_space=pl.ANY)],
            out_specs=pl.BlockSpec(memory_space=pl.ANY),
            scratch_shapes=[pltpu.SemaphoreType.DMA((2,))],
        ),
        out_shape=jax.ShapeDtypeStruct(xs_zero.shape, xs_zero.dtype),
        input_output_aliases={2: 0},
        compiler_params=pltpu.CompilerParams(dimension_semantics=("arbitrary",), has_side_effects=True),
        name="dispatch",
    )(dest_flat, h2, xs_zero)


def _moe1_kernel(be_ref, nu_ref, x_ref, wg_ref, wu_ref, o_ref, wcat):
    i = pl.program_id(0)

    @pl.when(i < nu_ref[0])
    def _():
        prev = be_ref[jnp.maximum(i - 1, 0)]

        @pl.when(jnp.logical_or(i == 0, be_ref[i] != prev))
        def _():
            wcat[:, :MOE_HIDDEN] = wg_ref[0].astype(BF16)
            wcat[:, MOE_HIDDEN:] = wu_ref[0].astype(BF16)

        h = jnp.dot(x_ref[...].astype(BF16), wcat[...], preferred_element_type=F32)
        g = h[:, :MOE_HIDDEN]
        u = h[:, MOE_HIDDEN:]
        o_ref[...] = (g * jax.nn.sigmoid(g) * u).astype(BF16)

    @pl.when(i >= nu_ref[0])
    def _():
        o_ref[...] = jnp.zeros_like(o_ref)


def _moe1(block_e, n_used, xs, w_gate, w_up):
    P, D = xs.shape
    nb = P // MOE_BM
    wspec = pl.BlockSpec((1, D, MOE_HIDDEN), lambda i, be, nu: (be[i], 0, 0))
    return pl.pallas_call(
        _moe1_kernel,
        grid_spec=pltpu.PrefetchScalarGridSpec(
            num_scalar_prefetch=2,
            grid=(nb,),
            in_specs=[pl.BlockSpec((MOE_BM, D), lambda i, be, nu: (i, 0)), wspec, wspec],
            out_specs=pl.BlockSpec((MOE_BM, MOE_HIDDEN), lambda i, be, nu: (i, 0)),
            scratch_shapes=[pltpu.VMEM((D, 2 * MOE_HIDDEN), BF16)],
        ),
        out_shape=jax.ShapeDtypeStruct((P, MOE_HIDDEN), BF16),
        compiler_params=_cparams(("arbitrary",), VMEM_LIMIT),
        name="moe_up",
    )(block_e, n_used, xs, w_gate, w_up)


def _moe2_kernel(be_ref, nu_ref, h_ref, wd_ref, o_ref, wbf):
    i = pl.program_id(0)

    @pl.when(i < nu_ref[0])
    def _():
        prev = be_ref[jnp.maximum(i - 1, 0)]

        @pl.when(jnp.logical_or(i == 0, be_ref[i] != prev))
        def _():
            wbf[...] = wd_ref[0].astype(BF16)

        o_ref[...] = jnp.dot(h_ref[...], wbf[...], preferred_element_type=F32)

    @pl.when(i >= nu_ref[0])
    def _():
        o_ref[...] = jnp.zeros_like(o_ref)


def _moe2(block_e, n_used, hmid, w_down):
    P = hmid.shape[0]
    D = w_down.shape[2]
    nb = P // MOE_BM
    return pl.pallas_call(
        _moe2_kernel,
        grid_spec=pltpu.PrefetchScalarGridSpec(
            num_scalar_prefetch=2,
            grid=(nb,),
            in_specs=[
                pl.BlockSpec((MOE_BM, MOE_HIDDEN), lambda i, be, nu: (i, 0)),
                pl.BlockSpec((1, MOE_HIDDEN, D), lambda i, be, nu: (be[i], 0, 0)),
            ],
            out_specs=pl.BlockSpec((MOE_BM, D), lambda i, be, nu: (i, 0)),
            scratch_shapes=[pltpu.VMEM((MOE_HIDDEN, D), BF16)],
        ),
        out_shape=jax.ShapeDtypeStruct((P, D), F32),
        compiler_params=_cparams(("arbitrary",), VMEM_LIMIT),
        name="moe_down",
    )(block_e, n_used, hmid, w_down)


def _combine_kernel(dest_ref, yb_hbm, x_ref, r_ref, mod_ref, fg_ref, o_ref, ybuf, sem, *, final):
    i = pl.program_id(0)
    n = pl.num_programs(0)

    def row_copy(tok, k, t, slot):
        return pltpu.make_async_copy(yb_hbm.at[pl.ds(dest_ref[2 * tok + k], 1)],
                                     ybuf.at[slot, k, pl.ds(t, 1)], sem.at[slot])

    def issue(blk, slot):
        def body(t, c):
            tok = blk * TM + t
            row_copy(tok, 0, t, slot).start()
            row_copy(tok, 1, t, slot).start()
            return c
        lax.fori_loop(0, TM, body, 0)

    @pl.when(i == 0)
    def _():
        issue(0, 0)

    @pl.when(i + 1 < n)
    def _():
        issue(i + 1, (i + 1) % 2)

    slot = i % 2

    def w(t, c):
        row_copy(0, 0, 0, slot).wait()
        row_copy(0, 1, 0, slot).wait()
        return c

    lax.fori_loop(0, TM, w, 0)
    r = r_ref[...]
    y = r[:, 2:3] * ybuf[slot, 0] + r[:, 3:4] * ybuf[slot, 1]
    xn = x_ref[...] + mod_ref[0, 5:6, :] * y
    if final:
        xn = _rms(xn) * fg_ref[...]
    o_ref[...] = xn


def _combine(dest_flat, yb, x_all, route, mod, final_g, seg_of, nblk, final):
    M, D = x_all.shape
    out_rows = nblk * TM if final else M
    kwargs = {} if final else {"input_output_aliases": {2: 0}}
    return pl.pallas_call(
        functools.partial(_combine_kernel, final=final),
        grid_spec=pltpu.PrefetchScalarGridSpec(
            num_scalar_prefetch=1,
            grid=(nblk,),
            in_specs=[
                pl.BlockSpec(memory_space=pl.ANY),
                pl.BlockSpec((TM, D), lambda i, d: (i, 0)),
                pl.BlockSpec((TM, 128), lambda i, d: (i, 0)),
                pl.BlockSpec((1, 6, D), lambda i, d: (seg_of(i), 0, 0)),
                pl.BlockSpec((1, D), lambda i, d: (0, 0)),
            ],
            out_specs=pl.BlockSpec((TM, D), lambda i, d: (i, 0)),
            scratch_shapes=[pltpu.VMEM((2, 2, TM, D), F32), pltpu.SemaphoreType.DMA((2,))],
        ),
        out_shape=jax.ShapeDtypeStruct((out_rows, D), F32),
        compiler_params=_cparams(("arbitrary",), VMEM_LIMIT),
        name="combine",
        **kwargs,
    )(dest_flat, yb, x_all, route, mod, final_g.reshape(1, D))


def _moe_layer(x_all, h2, logits, mod, final_g, w_gate, w_up, w_down, seg_of, nblk, final):
    T = nblk * TM
    route = _route(logits, nblk)
    rank, cnt = _rank(route, nblk)
    e_ids = route[:T, :2].astype(jnp.int32)
    ranks = rank[:T, :2].astype(jnp.int32)
    counts = cnt[0, :MOE_EXPERTS].astype(jnp.int32)
    padded = (counts + MOE_BM - 1) // MOE_BM * MOE_BM
    pad_ends = jnp.cumsum(padded)
    pad_starts = pad_ends - padded
    dest = (pad_starts[e_ids] + ranks).reshape(-1)
    n_blocks = -(-(2 * T + MOE_EXPERTS * (MOE_BM - 1)) // MOE_BM)
    block_starts = jnp.arange(n_blocks, dtype=jnp.int32) * MOE_BM
    block_e = jnp.minimum(jnp.searchsorted(pad_ends, block_starts, side="right"), MOE_EXPERTS - 1).astype(jnp.int32)
    n_used = (pad_ends[-1:] // MOE_BM).astype(jnp.int32)
    xs = _dispatch(dest, h2, jnp.zeros((n_blocks * MOE_BM, h2.shape[1]), h2.dtype), nblk)
    hmid = _moe1(block_e, n_used, xs, w_gate, w_up)
    yb = _moe2(block_e, n_used, hmid, w_down)
    return _combine(dest, yb, x_all, route, mod, final_g, seg_of, nblk, final)


def _rope_tables(S):
    pos = jnp.arange(S, dtype=jnp.int32)
    rows = (pos // GRID_W).astype(F32)
    cols = (pos % GRID_W).astype(F32)
    n_freq = HEAD_DIM // 4
    inv = 1.0 / (ROPE_THETA ** (jnp.arange(n_freq, dtype=F32) / n_freq))
    cr, sr = jnp.cos(rows[:, None] * inv), jnp.sin(rows[:, None] * inv)
    cc, sc = jnp.cos(cols[:, None] * inv), jnp.sin(cols[:, None] * inv)
    cos_t = jnp.concatenate([cr, cr, cc, cc], axis=1)
    sin_t = jnp.concatenate([-sr, sr, -sc, sc], axis=1)
    cos_t = jnp.concatenate([cos_t, jnp.ones((TM, HEAD_DIM), F32)], axis=0)
    sin_t = jnp.concatenate([sin_t, jnp.zeros((TM, HEAD_DIM), F32)], axis=0)
    return cos_t, sin_t


def kernel(x, c, ctx, c_ctx, ada_w, ada_b, norm1_g, norm2_g, w_in, w_out, att_q_norm, att_k_norm, conv_w, conv_b, lru_wr, lru_br, lru_wi, lru_bi, lru_lambda, na_rpb, router_wg, router_bg, router_we, router_be, moe_w_gate, moe_w_up, moe_w_down, final_g):
    B, S, D = x.shape
    C = ctx.shape[1]
    L = ada_w.shape[0]
    assert C == TM and S % TM == 0 and S // TM >= 3 and B + 1 <= 8
    nbs = S // TM
    nlat = B * nbs
    m_lat = B * S
    M = m_lat + B * C

    def seg_of(i):
        return jnp.minimum(i // nbs, B)

    x_all = jnp.concatenate([x.reshape(m_lat, D), ctx.reshape(B * C, D)], axis=0)
    cvec = jnp.zeros((8, D), F32).at[:B].set(c).at[B].set(c_ctx)
    mod_all = _adaln(cvec, ada_w, ada_b)[:, :B + 1].reshape(L, B + 1, 6, D)
    cos_t, sin_t = _rope_tables(S)
    g_att = ATT_Q_HEADS // ATT_KV_HEADS

    out = None
    for l in range(L):
        last = l == L - 1
        mod = mod_all[l]
        proj = _inproj(x_all, mod, norm1_g[l], w_in[l].astype(BF16), seg_of)

        q, k_all, v_all = _qkprep(proj, cos_t, sin_t, att_q_norm[l], att_k_norm[l], B, S)
        oa = _gqa_latent(q, k_all, v_all, B, S)

        sp = jax.nn.softplus(-lru_lambda[l].astype(F32))
        hrev = _lru_dir(proj, None, conv_w[l], conv_b[l], lru_wr[l, 1], lru_br[l, 1], lru_wi[l, 1],
                        lru_bi[l, 1], sp[1], B, S, reverse=True)
        ob = _lru_dir(proj, hrev, conv_w[l], conv_b[l], lru_wr[l, 0], lru_br[l, 0], lru_wi[l, 0],
                      lru_bi[l, 0], sp[0], B, S, reverse=False)

        tl, tr = _na_bias_tables(na_rpb[l])
        oc = _na_latent(proj, tl, tr, B, S)

        if not last:
            oa = _ctx_attn(
                q, lambda b, h: (nlat + b, h),
                k_all, pl.BlockSpec((None, TM, HEAD_DIM), lambda b, h: (b, nbs, h // g_att)),
                v_all, pl.BlockSpec((None, TM, HEAD_DIM), lambda b, h: (b, nbs, h // g_att)),
                oa, B, ATT_Q_HEADS, nlat, 1.0)
            oc = _ctx_attn(
                proj, lambda b, h: (nlat + b, COL_QN // HEAD_DIM + h),
                proj, pl.BlockSpec((TM, HEAD_DIM), lambda b, h: (nlat + b, COL_KN // HEAD_DIM + h)),
                proj, pl.BlockSpec((TM, HEAD_DIM), lambda b, h: (nlat + b, COL_VN // HEAD_DIM + h)),
                oc, B, NA_HEADS, nlat, HEAD_DIM ** -0.5)

        nblk = nlat if last else M // TM
        w_router = jnp.zeros((D, 128), F32).at[:, :MOE_GROUPS].set(router_wg[l])
        w_router = w_router.at[:, MOE_GROUPS:MOE_GROUPS + MOE_EXPERTS].set(router_we[l])
        b_router = jnp.zeros((1, 128), F32).at[0, :MOE_GROUPS].set(router_bg[l])
        b_router = b_router.at[0, MOE_GROUPS:MOE_GROUPS + MOE_EXPERTS].set(router_be[l])
        x_all, h2, logits = _outproj(oa, ob, oc, x_all, mod, norm2_g[l], w_out[l].astype(BF16),
                                     w_router, b_router, seg_of, nblk)
        res = _moe_layer(x_all, h2, logits, mod, final_g, moe_w_gate[l], moe_w_up[l], moe_w_down[l],
                         seg_of, nblk, last)
        if last:
            out = res
        else:
            x_all = res
    return out.reshape(B, S, D)
```

```python
import functools

import numpy as np
import jax
import jax.numpy as jnp
from jax import lax
from jax.experimental import pallas as pl
from jax.experimental.pallas import tpu as pltpu

F32 = jnp.float32
BF16 = jnp.bfloat16

GRID_W = 64
HEAD_DIM = 128
ATT_Q_HEADS = 4
ATT_KV_HEADS = 2
LRU_WIDTH = 1024
LRU_BLOCKS = 8
LRU_C = 8.0
CONV_W = 4
NA_HEADS = 4
NA_ROWS = 8
NA_COLS = 16
ATT_Q_W = ATT_Q_HEADS * HEAD_DIM
ATT_KV_W = ATT_KV_HEADS * HEAD_DIM
NA_W = NA_HEADS * HEAD_DIM
MOE_GROUPS = 4
MOE_EXPERTS_PER_GROUP = 8
MOE_EXPERTS = MOE_GROUPS * MOE_EXPERTS_PER_GROUP
MOE_HIDDEN = 1024
ROPE_THETA = 10000.0
EPS = 1e-6
LOG2E = 1.4426950408889634

COL_QA = 0
COL_KA = COL_QA + ATT_Q_W
COL_VA = COL_KA + ATT_KV_W
COL_UB = COL_VA + ATT_KV_W
COL_GB = COL_UB + LRU_WIDTH
COL_QN = COL_GB + LRU_WIDTH
COL_KN = COL_QN + NA_W
COL_VN = COL_KN + NA_W
IN_WIDTH = COL_VN + NA_W

TM = 256
MOE_BM = 256
NA_QROWS = 4
NA_WROWS = 12
NEG_BIAS = -1e30
VMEM_LIMIT = 56 * 1024 * 1024


def _cparams(sem, vmem=None):
    return pltpu.CompilerParams(dimension_semantics=sem, vmem_limit_bytes=vmem)


def _rms(x):
    return x * lax.rsqrt(jnp.mean(x * x, axis=-1, keepdims=True) + EPS)


def _adaln_kernel(c_ref, w_ref, b_ref, o_ref):
    cv = c_ref[...]
    s = cv * jax.nn.sigmoid(cv)
    o_ref[0] = jnp.dot(s.astype(BF16), w_ref[0].astype(BF16), preferred_element_type=F32) + b_ref[0]


def _adaln(cvec, ada_w, ada_b):
    L, D, N = ada_w.shape
    tn = 1024
    return pl.pallas_call(
        _adaln_kernel,
        grid=(L, N // tn),
        in_specs=[
            pl.BlockSpec((8, D), lambda l, j: (0, 0)),
            pl.BlockSpec((1, D, tn), lambda l, j: (l, 0, j)),
            pl.BlockSpec((1, 1, tn), lambda l, j: (l, 0, j)),
        ],
        out_specs=pl.BlockSpec((1, 8, tn), lambda l, j: (l, 0, j)),
        out_shape=jax.ShapeDtypeStruct((L, 8, N), F32),
        compiler_params=_cparams(("parallel", "parallel"), VMEM_LIMIT),
        name="adaln",
    )(cvec, ada_w, ada_b.reshape(L, 1, N))


def _inproj_kernel(x_ref, mod_ref, g_ref, w_ref, o_ref):
    h = _rms(x_ref[...]) * g_ref[...]
    h = h * (1.0 + mod_ref[0, 1:2, :]) + mod_ref[0, 0:1, :]
    o_ref[...] = jnp.dot(h.astype(BF16), w_ref[...], preferred_element_type=F32)


def _inproj(x_all, mod, g1, w_in_bf, B, S):
    M, D = x_all.shape
    N = w_in_bf.shape[1]
    tm = 512 if (S % 512 == 0 and (M - B * S) % 512 == 0) else TM
    per_seq = S // tm
    return pl.pallas_call(
        _inproj_kernel,
        grid=(M // tm,),
        in_specs=[
            pl.BlockSpec((tm, D), lambda i: (i, 0)),
            pl.BlockSpec((1, 6, D), lambda i: (jnp.minimum(i // per_seq, B), 0, 0)),
            pl.BlockSpec((1, D), lambda i: (0, 0)),
            pl.BlockSpec((D, N), lambda i: (0, 0), pipeline_mode=pl.Buffered(1)),
        ],
        out_specs=pl.BlockSpec((tm, N), lambda i: (i, 0)),
        out_shape=jax.ShapeDtypeStruct((M, N), F32),
        compiler_params=_cparams(("parallel",), VMEM_LIMIT),
        name="inproj",
    )(x_all, mod, g1.reshape(1, D), w_in_bf)


def _norm_rope(t, g, cos, sin):
    y = _rms(t) * g
    lane = lax.broadcasted_iota(jnp.int32, y.shape, 1)
    first_half = (lane % 64) < 32
    partner = jnp.where(first_half, pltpu.roll(y, 96, 1), pltpu.roll(y, 32, 1))
    return y * cos + partner * sin


def _qkprep_kernel(p_ref, cos_ref, sin_ref, qg_ref, kg_ref, q_ref, k_ref, v_ref):
    cos = cos_ref[...]
    sin = sin_ref[...]
    scale = HEAD_DIM ** -0.5 * LOG2E
    for h in range(ATT_Q_HEADS):
        t = p_ref[:, COL_QA + h * HEAD_DIM:COL_QA + (h + 1) * HEAD_DIM]
        q_ref[:, h * HEAD_DIM:(h + 1) * HEAD_DIM] = (_norm_rope(t, qg_ref[...], cos, sin) * scale).astype(BF16)
    for h in range(ATT_KV_HEADS):
        t = p_ref[:, COL_KA + h * HEAD_DIM:COL_KA + (h + 1) * HEAD_DIM]
        k_ref[0, :, h * HEAD_DIM:(h + 1) * HEAD_DIM] = _norm_rope(t, kg_ref[...], cos, sin).astype(BF16)
        v_ref[0, :, 2 * h * HEAD_DIM:(2 * h + 1) * HEAD_DIM] = (
            p_ref[:, COL_VA + h * HEAD_DIM:COL_VA + (h + 1) * HEAD_DIM].astype(BF16))
        v_ref[0, :, (2 * h + 1) * HEAD_DIM:(2 * h + 2) * HEAD_DIM] = jnp.ones((p_ref.shape[0], HEAD_DIM), BF16)


def _qkprep(proj, cos_t, sin_t, qg, kg, B, S):
    M = proj.shape[0]
    nbs = S // TM
    nlat = B * nbs

    def b_of(i):
        return jnp.where(i < nlat, i // nbs, i - nlat)

    def pos_of(i):
        return jnp.where(i < nlat, i % nbs, nbs)

    kl = S + TM
    return pl.pallas_call(
        _qkprep_kernel,
        grid=(M // TM,),
        in_specs=[
            pl.BlockSpec((TM, COL_UB), lambda i: (i, 0)),
            pl.BlockSpec((TM, HEAD_DIM), lambda i: (pos_of(i), 0)),
            pl.BlockSpec((TM, HEAD_DIM), lambda i: (pos_of(i), 0)),
            pl.BlockSpec((1, HEAD_DIM), lambda i: (0, 0)),
            pl.BlockSpec((1, HEAD_DIM), lambda i: (0, 0)),
        ],
        out_specs=[
            pl.BlockSpec((TM, ATT_Q_W), lambda i: (i, 0)),
            pl.BlockSpec((1, TM, ATT_KV_W), lambda i: (b_of(i), pos_of(i), 0)),
            pl.BlockSpec((1, TM, 2 * ATT_KV_W), lambda i: (b_of(i), pos_of(i), 0)),
        ],
        out_shape=[
            jax.ShapeDtypeStruct((M, ATT_Q_W), BF16),
            jax.ShapeDtypeStruct((B, kl, ATT_KV_W), BF16),
            jax.ShapeDtypeStruct((B, kl, 2 * ATT_KV_W), BF16),
        ],
        compiler_params=_cparams(("parallel",), VMEM_LIMIT),
        name="qkprep",
    )(proj, cos_t, sin_t, qg.reshape(1, HEAD_DIM), kg.reshape(1, HEAD_DIM))


def _gqa_kernel(q_ref, k_ref, v_ref, o_in_ref, o_ref, q2_ref, sa_ref, sb_ref, m_ref, acc_ref, *, tk):
    del o_in_ref
    tq = q_ref.shape[0]
    q2_ref[0:tq, :] = q_ref[:, :HEAD_DIM]
    q2_ref[tq:, :] = q_ref[:, HEAD_DIM:]
    nk = k_ref.shape[1] // tk
    m_ref[...] = jnp.full(m_ref.shape, -jnp.inf, F32)
    acc_ref[...] = jnp.zeros(acc_ref.shape, F32)

    def scores(c, s_ref):
        off = pl.multiple_of(c * tk, tk)
        s_ref[...] = lax.dot_general(q2_ref[...], k_ref[0, pl.ds(off, tk), :], (((1,), (1,)), ((), ())),
                                     preferred_element_type=F32)

    def update(c, s_ref):
        off = pl.multiple_of(c * tk, tk)
        vc = v_ref[0, pl.ds(off, tk), :]
        m = m_ref[...]
        m_new = jnp.maximum(m, jnp.max(s_ref[...], axis=-1, keepdims=True))
        m_ref[...] = m_new
        p = jnp.exp2((s_ref[...] - jnp.concatenate([m_new] * (tk // HEAD_DIM), axis=1)).astype(BF16))
        alpha = jnp.exp2(m - m_new)
        acc_ref[...] = (jnp.concatenate([alpha, alpha], axis=1) * acc_ref[...]
                        + jnp.dot(p, vc, preferred_element_type=F32))

    scores(0, sa_ref)
    n_pairs = (nk - 1) // 2

    def body(j, carry):
        scores(2 * j + 1, sb_ref)
        update(2 * j, sa_ref)
        scores(2 * j + 2, sa_ref)
        update(2 * j + 1, sb_ref)
        return carry

    lax.fori_loop(0, n_pairs, body, 0)
    if nk % 2 == 0:
        scores(nk - 1, sb_ref)
        update(nk - 2, sa_ref)
        update(nk - 1, sb_ref)
    else:
        update(nk - 1, sa_ref)
    acc = acc_ref[...]
    o = acc[:, :HEAD_DIM] / acc[:, HEAD_DIM:]
    o_ref[:, :HEAD_DIM] = o[:tq].astype(BF16)
    o_ref[:, HEAD_DIM:] = o[tq:].astype(BF16)


def _gqa_latent(q, k_all, v_all, B, S):
    M = q.shape[0]
    kl = k_all.shape[1]
    nbs = S // TM
    tk = 768 if kl % 768 == 0 else TM
    g = ATT_Q_HEADS // ATT_KV_HEADS
    return pl.pallas_call(
        functools.partial(_gqa_kernel, tk=tk),
        grid=(B, ATT_KV_HEADS, nbs),
        in_specs=[
            pl.BlockSpec((TM, g * HEAD_DIM), lambda b, h, i: (b * nbs + i, h)),
            pl.BlockSpec((1, kl, HEAD_DIM), lambda b, h, i: (b, 0, h)),
            pl.BlockSpec((1, kl, 2 * HEAD_DIM), lambda b, h, i: (b, 0, h)),
            pl.BlockSpec(memory_space=pl.ANY),
        ],
        out_specs=pl.BlockSpec((TM, g * HEAD_DIM), lambda b, h, i: (b * nbs + i, h)),
        out_shape=jax.ShapeDtypeStruct((M, ATT_Q_W), BF16),
        input_output_aliases={3: 0},
        scratch_shapes=[
            pltpu.VMEM((g * TM, HEAD_DIM), BF16),
            pltpu.VMEM((g * TM, tk), F32),
            pltpu.VMEM((g * TM, tk), F32),
            pltpu.VMEM((g * TM, HEAD_DIM), F32),
            pltpu.VMEM((g * TM, 2 * HEAD_DIM), F32),
        ],
        compiler_params=_cparams(("parallel", "parallel", "parallel"), VMEM_LIMIT),
        name="gqa_latent",
    )(q, k_all, v_all, jnp.zeros((M, ATT_Q_W), BF16))


def _ctx_attn_kernel(q_ref, k_ref, v_ref, o_in_ref, o_ref, *, scale):
    del o_in_ref
    if scale is None:
        q = q_ref[...]
    else:
        q = (q_ref[...] * (scale * LOG2E)).astype(BF16)
    k = k_ref[...].astype(BF16)
    v = v_ref[...].astype(BF16)
    s = lax.dot_general(q, k, (((1,), (1,)), ((), ())), preferred_element_type=F32)
    p = jnp.exp2(s - jnp.max(s, axis=-1, keepdims=True))
    l = jnp.sum(p, axis=-1, keepdims=True)
    o = jnp.dot(p.astype(BF16), v, preferred_element_type=F32) / l
    o_ref[...] = o.astype(BF16)


def _ctx_attn(q_arr, q_map, k_arr, k_spec, v_arr, v_spec, o_arr, B, n_heads, ctx_blk0, scale):
    return pl.pallas_call(
        functools.partial(_ctx_attn_kernel, scale=scale),
        grid=(B, n_heads),
        in_specs=[
            pl.BlockSpec((TM, HEAD_DIM), q_map),
            k_spec,
            v_spec,
            pl.BlockSpec(memory_space=pl.ANY),
        ],
        out_specs=pl.BlockSpec((TM, HEAD_DIM), lambda b, h: (ctx_blk0 + b, h)),
        out_shape=jax.ShapeDtypeStruct(o_arr.shape, o_arr.dtype),
        input_output_aliases={3: 0},
        compiler_params=_cparams(("parallel", "parallel"), VMEM_LIMIT),
        name="ctx_attn",
    )(q_arr, k_arr, v_arr, o_arr)


def _lru_kernel(u_ref, up_ref, un_ref, *rest, reverse, final):
    if final:
        (g_ref, hrev_ref, cw_ref, cb_ref, wr_ref, br_ref, wi_ref, bi_ref, sp_ref,
         o_ref, xpad, a_scr, b_scr, h_scr, hst) = rest
    else:
        (cw_ref, cb_ref, wr_ref, br_ref, wi_ref, bi_ref, sp_ref,
         o_ref, xpad, a_scr, b_scr, h_scr, hst) = rest
    s = pl.program_id(1)
    ns = pl.num_programs(1)
    T = u_ref.shape[0]
    W = u_ref.shape[1]

    @pl.when(s == 0)
    def _():
        hst[...] = jnp.zeros_like(hst)

    if reverse:
        j = ns - 1 - s
    else:
        j = s - 1
    is_first = jnp.logical_or(s == 0, j == 0)
    is_last = jnp.logical_or(s == 0, j == ns - 2)
    prev_rows = jnp.where(is_first, 0.0, up_ref[...])
    next_rows = jnp.where(is_last, 0.0, un_ref[...])
    xpad[0:8, :] = prev_rows
    xpad[8:8 + T, :] = u_ref[...]
    xpad[8 + T:16 + T, :] = next_rows
    u = (cb_ref[...]
         + xpad[7:7 + T, :] * cw_ref[0:1, :]
         + xpad[8:8 + T, :] * cw_ref[1:2, :]
         + xpad[9:9 + T, :] * cw_ref[2:3, :]
         + xpad[10:10 + T, :] * cw_ref[3:4, :])

    ub = u.astype(BF16)
    nb = LRU_WIDTH // LRU_BLOCKS
    for n in range(LRU_BLOCKS):
        sl = slice(n * nb, (n + 1) * nb)
        un_ = ub[:, sl]
        r = jax.nn.sigmoid(jnp.dot(un_, wr_ref[n], preferred_element_type=F32) + br_ref[:, sl])
        ig = jax.nn.sigmoid(jnp.dot(un_, wi_ref[n], preferred_element_type=F32) + bi_ref[:, sl])
        log_a = (-LRU_C) * r * sp_ref[:, sl]
        a = jnp.exp(log_a)
        a_scr[:, sl] = a
        one_minus_a2 = -jnp.tanh(log_a) * (a * a + 1.0)
        b_scr[:, sl] = jnp.sqrt(one_minus_a2) * (ig * u[:, sl])

    def step(t, h):
        row = (T - 1 - t) if reverse else t
        h = a_scr[pl.ds(row, 1), :] * h + b_scr[pl.ds(row, 1), :]
        h_scr[pl.ds(row, 1), :] = h
        return h

    hst[...] = lax.fori_loop(0, T, step, hst[...], unroll=8)

    if final:
        y = h_scr[...] + hrev_ref[...]
        o_ref[...] = (y * jax.nn.gelu(g_ref[...])).astype(o_ref.dtype)
    else:
        o_ref[...] = h_scr[...]
    del W


def _lru_dir(proj, hrev, conv_w, conv_b, wr, br, wi, bi, sp, B, S, reverse):
    M = proj.shape[0]
    nbs = S // TM
    nlat = B * nbs
    final = hrev is not None
    ub_blk = COL_UB // LRU_WIDTH
    gb_blk = COL_GB // LRU_WIDTH
    r8 = TM // 8
    n8 = M // 8

    def blk(b, s):
        j = (nbs - s) if reverse else (s - 1)
        return jnp.where(s == 0, nlat + b, b * nbs + j)

    in_specs = [
        pl.BlockSpec((TM, LRU_WIDTH), lambda b, s: (blk(b, s), ub_blk)),
        pl.BlockSpec((8, LRU_WIDTH), lambda b, s: (jnp.maximum(blk(b, s) * r8 - 1, 0), ub_blk)),
        pl.BlockSpec((8, LRU_WIDTH), lambda b, s: (jnp.minimum((blk(b, s) + 1) * r8, n8 - 1), ub_blk)),
    ]
    args = [proj, proj, proj]
    if final:
        in_specs += [
            pl.BlockSpec((TM, LRU_WIDTH), lambda b, s: (blk(b, s), gb_blk)),
            pl.BlockSpec((TM, LRU_WIDTH), lambda b, s: (blk(b, s), 0)),
        ]
        args += [proj, hrev]
    const2 = lambda b, s: (0, 0)
    in_specs += [
        pl.BlockSpec((CONV_W, LRU_WIDTH), const2),
        pl.BlockSpec((1, LRU_WIDTH), const2),
        pl.BlockSpec((LRU_BLOCKS, LRU_WIDTH // LRU_BLOCKS, LRU_WIDTH // LRU_BLOCKS), lambda b, s: (0, 0, 0)),
        pl.BlockSpec((1, LRU_WIDTH), const2),
        pl.BlockSpec((LRU_BLOCKS, LRU_WIDTH // LRU_BLOCKS, LRU_WIDTH // LRU_BLOCKS), lambda b, s: (0, 0, 0)),
        pl.BlockSpec((1, LRU_WIDTH), const2),
        pl.BlockSpec((1, LRU_WIDTH), const2),
    ]
    args += [conv_w, conv_b.reshape(1, -1), wr.astype(BF16), br.reshape(1, -1), wi.astype(BF16),
             bi.reshape(1, -1), sp.reshape(1, -1)]
    return pl.pallas_call(
        functools.partial(_lru_kernel, reverse=reverse, final=final),
        grid=(B, nbs + 1),
        in_specs=in_specs,
        out_specs=pl.BlockSpec((TM, LRU_WIDTH), lambda b, s: (blk(b, s), 0)),
        out_shape=jax.ShapeDtypeStruct((M, LRU_WIDTH), BF16 if final else F32),
        scratch_shapes=[
            pltpu.VMEM((TM + 16, LRU_WIDTH), F32),
            pltpu.VMEM((TM, LRU_WIDTH), F32),
            pltpu.VMEM((TM, LRU_WIDTH), F32),
            pltpu.VMEM((TM, LRU_WIDTH), F32),
            pltpu.VMEM((1, LRU_WIDTH), F32),
        ],
        compiler_params=_cparams(("parallel", "arbitrary"), VMEM_LIMIT),
        name="lru_fwd" if final else "lru_rev",
    )(*args)


def _na_kernel(q_ref, k0_ref, k1_ref, k2_ref, v0_ref, v1_ref, v2_ref, kc_ref, vc_ref, tl_ref, tr_ref,
               o_in_ref, o_ref, s_scr, *, n_rows):
    del o_in_ref
    i = pl.program_id(1)
    nbs = n_rows // NA_QROWS
    r0 = NA_QROWS * i
    ws = NA_QROWS * jnp.clip(i - 1, 0, nbs - 3)
    scale = HEAD_DIM ** -0.5
    nloc = NA_WROWS * GRID_W
    for h in range(NA_HEADS):
        hs = slice(h * HEAD_DIM, (h + 1) * HEAD_DIM)
        qh = (q_ref[:, hs] * scale).astype(BF16)
        kh = jnp.concatenate([k0_ref[:, hs], k1_ref[:, hs], k2_ref[:, hs], kc_ref[:, hs]], axis=0).astype(BF16)
        vh = jnp.concatenate([v0_ref[:, hs], v1_ref[:, hs], v2_ref[:, hs], vc_ref[:, hs]], axis=0).astype(BF16)
        s_scr[...] = lax.dot_general(qh, kh, (((1,), (1,)), ((), ())), preferred_element_type=F32)
        for qr in range(NA_QROWS):
            r = r0 + qr
            rs = jnp.clip(r - NA_ROWS // 2, 0, n_rows - NA_ROWS)

            def tile_of(a):
                kr = ws + a
                valid = jnp.logical_and(kr >= rs, kr < rs + NA_ROWS)
                return jnp.where(valid, kr - r + (NA_ROWS - 1), 2 * NA_ROWS - 1)

            for pr in range(NA_WROWS // 2):
                bias = tl_ref[h, tile_of(2 * pr)] + tr_ref[h, tile_of(2 * pr + 1)]
                rsl = slice(qr * GRID_W, (qr + 1) * GRID_W)
                csl = slice(pr * 2 * GRID_W, (pr + 1) * 2 * GRID_W)
                s_scr[rsl, csl] = s_scr[rsl, csl] + bias
        s = s_scr[...]
        p = jnp.exp(s - jnp.max(s, axis=-1, keepdims=True))
        l = jnp.sum(p, axis=-1, keepdims=True)
        o = jnp.dot(p.astype(BF16), vh, preferred_element_type=F32) / l
        o_ref[:, hs] = o.astype(BF16)
    del nloc


def _na_latent(proj, tl, tr, B, S):
    M = proj.shape[0]
    nbs = S // TM
    nlat = B * nbs
    n_rows = S // GRID_W
    qb, kb, vb = COL_QN // NA_W, COL_KN // NA_W, COL_VN // NA_W

    def wblk(b, i, t):
        return b * nbs + jnp.clip(i - 1, 0, nbs - 3) + t

    blk = (TM, NA_W)
    in_specs = [pl.BlockSpec(blk, lambda b, i: (b * nbs + i, qb))]
    in_specs += [pl.BlockSpec(blk, functools.partial(lambda b, i, t: (wblk(b, i, t), kb), t=t)) for t in range(3)]
    in_specs += [pl.BlockSpec(blk, functools.partial(lambda b, i, t: (wblk(b, i, t), vb), t=t)) for t in range(3)]
    in_specs += [
        pl.BlockSpec(blk, lambda b, i: (nlat + b, kb)),
        pl.BlockSpec(blk, lambda b, i: (nlat + b, vb)),
        pl.BlockSpec(tl.shape, lambda b, i: (0, 0, 0, 0)),
        pl.BlockSpec(tr.shape, lambda b, i: (0, 0, 0, 0)),
        pl.BlockSpec(memory_space=pl.ANY),
    ]
    return pl.pallas_call(
        functools.partial(_na_kernel, n_rows=n_rows),
        grid=(B, nbs),
        in_specs=in_specs,
        out_specs=pl.BlockSpec(blk, lambda b, i: (b * nbs + i, 0)),
        out_shape=jax.ShapeDtypeStruct((M, NA_W), BF16),
        input_output_aliases={11: 0},
        scratch_shapes=[pltpu.VMEM((TM, NA_WROWS * GRID_W + TM), F32)],
        compiler_params=_cparams(("parallel", "parallel"), VMEM_LIMIT),
        name="na_latent",
    )(proj, proj, proj, proj, proj, proj, proj, proj, proj, tl, tr, jnp.zeros((M, NA_W), BF16))


def _na_bias_tables(rpb):
    j = np.arange(GRID_W)
    cs = np.clip(j - NA_COLS // 2, 0, GRID_W - NA_COLS)
    kc = np.arange(GRID_W)
    inside = (kc[None, :] >= cs[:, None]) & (kc[None, :] < cs[:, None] + NA_COLS)
    dc = np.clip(kc[None, :] - j[:, None] + (NA_COLS - 1), 0, 2 * NA_COLS - 2)
    t = jnp.where(inside[None, None], rpb[:, :, dc], NEG_BIAS)
    t = jnp.concatenate([t, jnp.full_like(t[:, :1], NEG_BIAS)], axis=1)
    z = jnp.zeros_like(t)
    return jnp.concatenate([t, z], axis=-1), jnp.concatenate([z, t], axis=-1)


def _outproj_kernel(oa_ref, ob_ref, oc_ref, x_ref, mod_ref, g_ref, w_ref, wr_ref, br_ref,
                    xo_ref, h2_ref, lg_ref):
    k1 = ATT_Q_W
    k2 = k1 + LRU_WIDTH
    acc = jnp.dot(oa_ref[...], w_ref[0:k1, :], preferred_element_type=F32)
    acc = acc + jnp.dot(ob_ref[...], w_ref[k1:k2, :], preferred_element_type=F32)
    acc = acc + jnp.dot(oc_ref[...], w_ref[k2:, :], preferred_element_type=F32)
    xn = x_ref[...] + mod_ref[0, 2:3, :] * acc
    xo_ref[...] = xn
    h2 = _rms(xn) * g_ref[...]
    h2 = h2 * (1.0 + mod_ref[0, 4:5, :]) + mod_ref[0, 3:4, :]
    h2_ref[...] = h2
    lg_ref[...] = jnp.dot(h2, wr_ref[...], preferred_element_type=F32,
                          precision=lax.Precision.HIGHEST) + br_ref[...]


def _outproj(oa, ob, oc, x_all, mod, g2, w_out_bf, w_router, b_router, seg_of, nblk):
    M, D = x_all.shape
    return pl.pallas_call(
        _outproj_kernel,
        grid=(nblk,),
        in_specs=[
            pl.BlockSpec((TM, ATT_Q_W), lambda i: (i, 0)),
            pl.BlockSpec((TM, LRU_WIDTH), lambda i: (i, 0)),
            pl.BlockSpec((TM, NA_W), lambda i: (i, 0)),
            pl.BlockSpec((TM, D), lambda i: (i, 0)),
            pl.BlockSpec((1, 6, D), lambda i: (seg_of(i), 0, 0)),
            pl.BlockSpec((1, D), lambda i: (0, 0)),
            pl.BlockSpec(w_out_bf.shape, lambda i: (0, 0)),
            pl.BlockSpec(w_router.shape, lambda i: (0, 0)),
            pl.BlockSpec((1, 128), lambda i: (0, 0)),
        ],
        out_specs=[
            pl.BlockSpec((TM, D), lambda i: (i, 0)),
            pl.BlockSpec((TM, D), lambda i: (i, 0)),
            pl.BlockSpec((TM, 128), lambda i: (i, 0)),
        ],
        out_shape=[
            jax.ShapeDtypeStruct((M, D), F32),
            jax.ShapeDtypeStruct((nblk * TM, D), F32),
            jax.ShapeDtypeStruct((nblk * TM, 128), F32),
        ],
        input_output_aliases={3: 0},
        compiler_params=_cparams(("parallel",), VMEM_LIMIT),
        name="outproj",
    )(oa, ob, oc, x_all, mod, g2.reshape(1, D), w_out_bf, w_router, b_router)


def _route_kernel(lg_ref, o_ref):
    x = lg_ref[...]
    lane = lax.broadcasted_iota(jnp.int32, x.shape, 1)
    big = jnp.int32(1 << 20)
    ninf = -jnp.inf

    def first_argmax(vals):
        m = jnp.max(vals, axis=-1, keepdims=True)
        idx = jnp.min(jnp.where(vals == m, lane, big), axis=-1, keepdims=True)
        return m, idx

    lg = jnp.where(lane < MOE_GROUPS, x, ninf)
    mg, g_star = first_argmax(lg)
    g_gate = 1.0 / jnp.sum(jnp.exp(lg - mg), axis=-1, keepdims=True)
    lo = MOE_GROUPS + MOE_EXPERTS_PER_GROUP * g_star
    le = jnp.where(jnp.logical_and(lane >= lo, lane < lo + MOE_EXPERTS_PER_GROUP), x, ninf)
    v1, i1 = first_argmax(le)
    v2, i2 = first_argmax(jnp.where(lane == i1, ninf, le))
    e21 = jnp.exp(v2 - v1)
    w1 = g_gate / (1.0 + e21)
    w2 = g_gate * e21 / (1.0 + e21)
    e1 = (i1 - MOE_GROUPS).astype(F32)
    e2 = (i2 - MOE_GROUPS).astype(F32)
    o_ref[...] = jnp.where(lane == 0, e1, jnp.where(lane == 1, e2, jnp.where(lane == 2, w1,
                           jnp.where(lane == 3, w2, 0.0))))


def _route(logits, nblk):
    M = logits.shape[0]
    return pl.pallas_call(
        _route_kernel,
        grid=(nblk,),
        in_specs=[pl.BlockSpec((TM, 128), lambda i: (i, 0))],
        out_specs=pl.BlockSpec((TM, 128), lambda i: (i, 0)),
        out_shape=jax.ShapeDtypeStruct((M, 128), F32),
        compiler_params=_cparams(("parallel",)),
        name="route",
    )(logits)


def _rank_kernel(r_ref, rank_ref, cnt_ref, carry):
    @pl.when(pl.program_id(0) == 0)
    def _():
        carry[...] = jnp.zeros_like(carry)

    r = r_ref[...]
    lane = lax.broadcasted_iota(jnp.int32, r.shape, 1).astype(F32)
    oh0 = jnp.where(lane == r[:, 0:1], 1.0, 0.0)
    oh1 = jnp.where(lane == r[:, 1:2], 1.0, 0.0)
    oh = oh0 + oh1
    n = r.shape[0]
    row = lax.broadcasted_iota(jnp.int32, (n, n), 0)
    col = lax.broadcasted_iota(jnp.int32, (n, n), 1)
    tri = jnp.where(col < row, 1.0, 0.0).astype(BF16)
    tot = carry[...] + jnp.dot(tri, oh.astype(BF16), preferred_element_type=F32)
    rank0 = jnp.sum(oh0 * tot, axis=-1, keepdims=True)
    rank1 = jnp.sum(oh1 * tot, axis=-1, keepdims=True)
    lane_i = lax.broadcasted_iota(jnp.int32, r.shape, 1)
    rank_ref[...] = jnp.where(lane_i == 0, rank0, jnp.where(lane_i == 1, rank1, 0.0))
    carry[...] = carry[...] + jnp.sum(oh, axis=0, keepdims=True)
    cnt_ref[...] = jnp.broadcast_to(carry[...], cnt_ref.shape)


def _rank(route, nblk):
    M = route.shape[0]
    return pl.pallas_call(
        _rank_kernel,
        grid=(nblk,),
        in_specs=[pl.BlockSpec((TM, 128), lambda i: (i, 0))],
        out_specs=[pl.BlockSpec((TM, 128), lambda i: (i, 0)), pl.BlockSpec((8, 128), lambda i: (0, 0))],
        out_shape=[jax.ShapeDtypeStruct((M, 128), F32), jax.ShapeDtypeStruct((8, 128), F32)],
        scratch_shapes=[pltpu.VMEM((1, 128), F32)],
        compiler_params=_cparams(("arbitrary",)),
        name="rank",
    )(route)


def _dispatch_kernel(dest_ref, h_ref, xs_in, xs_hbm, sem):
    del xs_in
    i = pl.program_id(0)

    def issue(t, c):
        tok = i * TM + t
        for k in range(2):
            pltpu.make_async_copy(h_ref.at[pl.ds(t, 1)], xs_hbm.at[pl.ds(dest_ref[2 * tok + k], 1)], sem).start()
        return c

    lax.fori_loop(0, TM, issue, 0, unroll=8)
    for k in range(2):
        pltpu.make_async_copy(h_ref, xs_hbm.at[pl.ds(0, TM)], sem).wait()


def _dispatch(dest_flat, h2, xs_zero, nblk):
    D = h2.shape[1]
    return pl.pallas_call(
        _dispatch_kernel,
        grid_spec=pltpu.PrefetchScalarGridSpec(
            num_scalar_prefetch=1,
            grid=(nblk,),
            in_specs=[pl.BlockSpec((TM, D), lambda i, d: (i, 0)), pl.BlockSpec(memory_space=pl.ANY)],
            out_specs=pl.BlockSpec(memory_space=pl.ANY),
            scratch_shapes=[pltpu.SemaphoreType.DMA(())],
        ),
        out_shape=jax.ShapeDtypeStruct(xs_zero.shape, xs_zero.dtype),
        input_output_aliases={2: 0},
        compiler_params=pltpu.CompilerParams(dimension_semantics=("arbitrary",), has_side_effects=True),
        name="dispatch",
    )(dest_flat, h2, xs_zero)


def _moe1_kernel(be_ref, nu_ref, x_ref, wg_ref, wu_ref, o_ref, wcat):
    i = pl.program_id(0)

    @pl.when(i < nu_ref[0])
    def _():
        prev = be_ref[jnp.maximum(i - 1, 0)]

        @pl.when(jnp.logical_or(i == 0, be_ref[i] != prev))
        def _():
            wcat[:, :MOE_HIDDEN] = wg_ref[0].astype(BF16)
            wcat[:, MOE_HIDDEN:] = wu_ref[0].astype(BF16)

        h = jnp.dot(x_ref[...].astype(BF16), wcat[...], preferred_element_type=F32)
        g = h[:, :MOE_HIDDEN]
        u = h[:, MOE_HIDDEN:]
        o_ref[...] = (g * jax.nn.sigmoid(g) * u).astype(BF16)

    @pl.when(i >= nu_ref[0])
    def _():
        o_ref[...] = jnp.zeros_like(o_ref)


def _moe1(block_e, n_used, xs, w_gate, w_up, layer):
    P, D = xs.shape
    nb = P // MOE_BM
    wspec = pl.BlockSpec((None, 1, D, MOE_HIDDEN), lambda i, be, nu: (layer, be[i], 0, 0))
    return pl.pallas_call(
        _moe1_kernel,
        grid_spec=pltpu.PrefetchScalarGridSpec(
            num_scalar_prefetch=2,
            grid=(nb,),
            in_specs=[pl.BlockSpec((MOE_BM, D), lambda i, be, nu: (i, 0)), wspec, wspec],
            out_specs=pl.BlockSpec((MOE_BM, MOE_HIDDEN), lambda i, be, nu: (i, 0)),
            scratch_shapes=[pltpu.VMEM((D, 2 * MOE_HIDDEN), BF16)],
        ),
        out_shape=jax.ShapeDtypeStruct((P, MOE_HIDDEN), BF16),
        compiler_params=_cparams(("arbitrary",), VMEM_LIMIT),
        name="moe_up",
    )(block_e, n_used, xs, w_gate, w_up)


def _moe2_kernel(be_ref, nu_ref, h_ref, wd_ref, o_ref, wbf):
    i = pl.program_id(0)

    @pl.when(i < nu_ref[0])
    def _():
        prev = be_ref[jnp.maximum(i - 1, 0)]

        @pl.when(jnp.logical_or(i == 0, be_ref[i] != prev))
        def _():
            wbf[...] = wd_ref[0].astype(BF16)

        o_ref[...] = jnp.dot(h_ref[...], wbf[...], preferred_element_type=F32)

    @pl.when(i >= nu_ref[0])
    def _():
        o_ref[...] = jnp.zeros_like(o_ref)


def _moe2(block_e, n_used, hmid, w_down, layer):
    P = hmid.shape[0]
    D = w_down.shape[3]
    nb = P // MOE_BM
    return pl.pallas_call(
        _moe2_kernel,
        grid_spec=pltpu.PrefetchScalarGridSpec(
            num_scalar_prefetch=2,
            grid=(nb,),
            in_specs=[
                pl.BlockSpec((MOE_BM, MOE_HIDDEN), lambda i, be, nu: (i, 0)),
                pl.BlockSpec((None, 1, MOE_HIDDEN, D), lambda i, be, nu: (layer, be[i], 0, 0)),
            ],
            out_specs=pl.BlockSpec((MOE_BM, D), lambda i, be, nu: (i, 0)),
            scratch_shapes=[pltpu.VMEM((MOE_HIDDEN, D), BF16)],
        ),
        out_shape=jax.ShapeDtypeStruct((P, D), F32),
        compiler_params=_cparams(("arbitrary",), VMEM_LIMIT),
        name="moe_down",
    )(block_e, n_used, hmid, w_down)


def _combine_kernel(dest_ref, yb_hbm, x_ref, r_ref, mod_ref, fg_ref, o_ref, ybuf, sem, *, final):
    i = pl.program_id(0)
    n = pl.num_programs(0)

    def row_copy(tok, k, t, slot):
        return pltpu.make_async_copy(yb_hbm.at[pl.ds(dest_ref[2 * tok + k], 1)],
                                     ybuf.at[slot, k, pl.ds(t, 1)], sem.at[slot])

    def issue(blk, slot):
        def body(t, c):
            tok = blk * TM + t
            row_copy(tok, 0, t, slot).start()
            row_copy(tok, 1, t, slot).start()
            return c
        lax.fori_loop(0, TM, body, 0, unroll=8)

    @pl.when(i == 0)
    def _():
        issue(0, 0)

    @pl.when(i + 1 < n)
    def _():
        issue(i + 1, (i + 1) % 2)

    slot = i % 2

    for k in range(2):
        pltpu.make_async_copy(yb_hbm.at[pl.ds(0, TM)], ybuf.at[slot, k], sem.at[slot]).wait()
    r = r_ref[...]
    y = r[:, 2:3] * ybuf[slot, 0] + r[:, 3:4] * ybuf[slot, 1]
    xn = x_ref[...] + mod_ref[0, 5:6, :] * y
    if final:
        xn = _rms(xn) * fg_ref[...]
    o_ref[...] = xn


def _combine(dest_flat, yb, x_all, route, mod, final_g, seg_of, nblk, final):
    M, D = x_all.shape
    out_rows = nblk * TM if final else M
    kwargs = {} if final else {"input_output_aliases": {2: 0}}
    return pl.pallas_call(
        functools.partial(_combine_kernel, final=final),
        grid_spec=pltpu.PrefetchScalarGridSpec(
            num_scalar_prefetch=1,
            grid=(nblk,),
            in_specs=[
                pl.BlockSpec(memory_space=pl.ANY),
                pl.BlockSpec((TM, D), lambda i, d: (i, 0)),
                pl.BlockSpec((TM, 128), lambda i, d: (i, 0)),
                pl.BlockSpec((1, 6, D), lambda i, d: (seg_of(i), 0, 0)),
                pl.BlockSpec((1, D), lambda i, d: (0, 0)),
            ],
            out_specs=pl.BlockSpec((TM, D), lambda i, d: (i, 0)),
            scratch_shapes=[pltpu.VMEM((2, 2, TM, D), F32), pltpu.SemaphoreType.DMA((2,))],
        ),
        out_shape=jax.ShapeDtypeStruct((out_rows, D), F32),
        compiler_params=_cparams(("arbitrary",), VMEM_LIMIT),
        name="combine",
        **kwargs,
    )(dest_flat, yb, x_all, route, mod, final_g.reshape(1, D))


def _moe_layer(x_all, h2, logits, mod, final_g, w_gate, w_up, w_down, layer, seg_of, nblk, final):
    T = nblk * TM
    route = _route(logits, nblk)
    rank, cnt = _rank(route, nblk)
    e_ids = route[:T, :2].astype(jnp.int32)
    ranks = rank[:T, :2].astype(jnp.int32)
    counts = cnt[0, :MOE_EXPERTS].astype(jnp.int32)
    padded = (counts + MOE_BM - 1) // MOE_BM * MOE_BM
    pad_ends = jnp.cumsum(padded)
    pad_starts = pad_ends - padded
    dest = (pad_starts[e_ids] + ranks).reshape(-1)
    n_blocks = -(-(2 * T + MOE_EXPERTS * (MOE_BM - 1)) // MOE_BM)
    block_starts = jnp.arange(n_blocks, dtype=jnp.int32) * MOE_BM
    block_e = jnp.minimum(jnp.searchsorted(pad_ends, block_starts, side="right"), MOE_EXPERTS - 1).astype(jnp.int32)
    n_used = (pad_ends[-1:] // MOE_BM).astype(jnp.int32)
    xs = _dispatch(dest, h2, jnp.zeros((n_blocks * MOE_BM, h2.shape[1]), h2.dtype), nblk)
    hmid = _moe1(block_e, n_used, xs, w_gate, w_up, layer)
    yb = _moe2(block_e, n_used, hmid, w_down, layer)
    return _combine(dest, yb, x_all, route, mod, final_g, seg_of, nblk, final)


def _rope_tables(S):
    pos = jnp.arange(S, dtype=jnp.int32)
    rows = (pos // GRID_W).astype(F32)
    cols = (pos % GRID_W).astype(F32)
    n_freq = HEAD_DIM // 4
    inv = 1.0 / (ROPE_THETA ** (jnp.arange(n_freq, dtype=F32) / n_freq))
    cr, sr = jnp.cos(rows[:, None] * inv), jnp.sin(rows[:, None] * inv)
    cc, sc = jnp.cos(cols[:, None] * inv), jnp.sin(cols[:, None] * inv)
    cos_t = jnp.concatenate([cr, cr, cc, cc], axis=1)
    sin_t = jnp.concatenate([-sr, sr, -sc, sc], axis=1)
    cos_t = jnp.concatenate([cos_t, jnp.ones((TM, HEAD_DIM), F32)], axis=0)
    sin_t = jnp.concatenate([sin_t, jnp.zeros((TM, HEAD_DIM), F32)], axis=0)
    return cos_t, sin_t


def kernel(x, c, ctx, c_ctx, ada_w, ada_b, norm1_g, norm2_g, w_in, w_out, att_q_norm, att_k_norm, conv_w, conv_b, lru_wr, lru_br, lru_wi, lru_bi, lru_lambda, na_rpb, router_wg, router_bg, router_we, router_be, moe_w_gate, moe_w_up, moe_w_down, final_g):
    B, S, D = x.shape
    C = ctx.shape[1]
    L = ada_w.shape[0]
    assert C == TM and S % TM == 0 and S // TM >= 3 and B + 1 <= 8
    nbs = S // TM
    nlat = B * nbs
    m_lat = B * S
    M = m_lat + B * C

    def seg_of(i):
        return jnp.minimum(i // nbs, B)

    x_all = jnp.concatenate([x.reshape(m_lat, D), ctx.reshape(B * C, D)], axis=0)
    cvec = jnp.zeros((8, D), F32).at[:B].set(c).at[B].set(c_ctx)
    mod_all = _adaln(cvec, ada_w, ada_b)[:, :B + 1].reshape(L, B + 1, 6, D)
    cos_t, sin_t = _rope_tables(S)
    g_att = ATT_Q_HEADS // ATT_KV_HEADS

    out = None
    for l in range(L):
        last = l == L - 1
        mod = mod_all[l]
        proj = _inproj(x_all, mod, norm1_g[l], w_in[l].astype(BF16), B, S)

        q, k_all, v_all = _qkprep(proj, cos_t, sin_t, att_q_norm[l], att_k_norm[l], B, S)
        oa = _gqa_latent(q, k_all, v_all, B, S)

        sp = jax.nn.softplus(-lru_lambda[l].astype(F32))
        hrev = _lru_dir(proj, None, conv_w[l], conv_b[l], lru_wr[l, 1], lru_br[l, 1], lru_wi[l, 1],
                        lru_bi[l, 1], sp[1], B, S, reverse=True)
        ob = _lru_dir(proj, hrev, conv_w[l], conv_b[l], lru_wr[l, 0], lru_br[l, 0], lru_wi[l, 0],
                      lru_bi[l, 0], sp[0], B, S, reverse=False)

        tl, tr = _na_bias_tables(na_rpb[l])
        oc = _na_latent(proj, tl, tr, B, S)

        if not last:
            oa = _ctx_attn(
                q, lambda b, h: (nlat + b, h),
                k_all, pl.BlockSpec((None, TM, HEAD_DIM), lambda b, h: (b, nbs, h // g_att)),
                v_all, pl.BlockSpec((None, TM, HEAD_DIM), lambda b, h: (b, nbs, 2 * (h // g_att))),
                oa, B, ATT_Q_HEADS, nlat, None)
            oc = _ctx_attn(
                proj, lambda b, h: (nlat + b, COL_QN // HEAD_DIM + h),
                proj, pl.BlockSpec((TM, HEAD_DIM), lambda b, h: (nlat + b, COL_KN // HEAD_DIM + h)),
                proj, pl.BlockSpec((TM, HEAD_DIM), lambda b, h: (nlat + b, COL_VN // HEAD_DIM + h)),
                oc, B, NA_HEADS, nlat, HEAD_DIM ** -0.5)

        nblk = nlat if last else M // TM
        w_router = jnp.zeros((D, 128), F32).at[:, :MOE_GROUPS].set(router_wg[l])
        w_router = w_router.at[:, MOE_GROUPS:MOE_GROUPS + MOE_EXPERTS].set(router_we[l])
        b_router = jnp.zeros((1, 128), F32).at[0, :MOE_GROUPS].set(router_bg[l])
        b_router = b_router.at[0, MOE_GROUPS:MOE_GROUPS + MOE_EXPERTS].set(router_be[l])
        x_all, h2, logits = _outproj(oa, ob, oc, x_all, mod, norm2_g[l], w_out[l].astype(BF16),
                                     w_router, b_router, seg_of, nblk)
        res = _moe_layer(x_all, h2, logits, mod, final_g, moe_w_gate, moe_w_up, moe_w_down, l,
                         seg_of, nblk, last)
        if last:
            out = res
        else:
            x_all = res
    return out.reshape(B, S, D)
```

```python
import functools

import numpy as np
import jax
import jax.numpy as jnp
from jax import lax
from jax.experimental import pallas as pl
from jax.experimental.pallas import tpu as pltpu

F32 = jnp.float32
BF16 = jnp.bfloat16

GRID_W = 64
HEAD_DIM = 128
ATT_Q_HEADS = 4
ATT_KV_HEADS = 2
LRU_WIDTH = 1024
LRU_BLOCKS = 8
LRU_C = 8.0
CONV_W = 4
NA_HEADS = 4
NA_ROWS = 8
NA_COLS = 16
ATT_Q_W = ATT_Q_HEADS * HEAD_DIM
ATT_KV_W = ATT_KV_HEADS * HEAD_DIM
NA_W = NA_HEADS * HEAD_DIM
MOE_GROUPS = 4
MOE_EXPERTS_PER_GROUP = 8
MOE_EXPERTS = MOE_GROUPS * MOE_EXPERTS_PER_GROUP
MOE_HIDDEN = 1024
ROPE_THETA = 10000.0
EPS = 1e-6
LOG2E = 1.4426950408889634

COL_QA = 0
COL_KA = COL_QA + ATT_Q_W
COL_VA = COL_KA + ATT_KV_W
COL_UB = COL_VA + ATT_KV_W
COL_GB = COL_UB + LRU_WIDTH
COL_QN = COL_GB + LRU_WIDTH
COL_KN = COL_QN + NA_W
COL_VN = COL_KN + NA_W
IN_WIDTH = COL_VN + NA_W

TM = 256
MOE_BM = 256
NA_QROWS = 4
NA_WROWS = 12
NEG_BIAS = -1e30
VMEM_LIMIT = 56 * 1024 * 1024


def _cparams(sem, vmem=None):
    return pltpu.CompilerParams(dimension_semantics=sem, vmem_limit_bytes=vmem)


def _rms(x):
    return x * lax.rsqrt(jnp.mean(x * x, axis=-1, keepdims=True) + EPS)


def _adaln_kernel(c_ref, w_ref, b_ref, o_ref):
    cv = c_ref[...]
    s = cv * jax.nn.sigmoid(cv)
    o_ref[0] = jnp.dot(s.astype(BF16), w_ref[0].astype(BF16), preferred_element_type=F32) + b_ref[0]


def _adaln(cvec, ada_w, ada_b):
    L, D, N = ada_w.shape
    tn = 1024
    return pl.pallas_call(
        _adaln_kernel,
        grid=(L, N // tn),
        in_specs=[
            pl.BlockSpec((8, D), lambda l, j: (0, 0)),
            pl.BlockSpec((1, D, tn), lambda l, j: (l, 0, j)),
            pl.BlockSpec((1, 1, tn), lambda l, j: (l, 0, j)),
        ],
        out_specs=pl.BlockSpec((1, 8, tn), lambda l, j: (l, 0, j)),
        out_shape=jax.ShapeDtypeStruct((L, 8, N), F32),
        compiler_params=_cparams(("parallel", "parallel"), VMEM_LIMIT),
        name="adaln",
    )(cvec, ada_w, ada_b.reshape(L, 1, N))


def _inproj_kernel(x_ref, mod_ref, g_ref, w_ref, o_ref):
    h = _rms(x_ref[...]) * g_ref[...]
    h = h * (1.0 + mod_ref[0, 1:2, :]) + mod_ref[0, 0:1, :]
    o_ref[...] = jnp.dot(h.astype(BF16), w_ref[...], preferred_element_type=F32)


def _inproj(x_all, mod, g1, w_in_bf, B, S):
    M, D = x_all.shape
    N = w_in_bf.shape[1]
    tm = 512 if (S % 512 == 0 and (M - B * S) % 512 == 0) else TM
    per_seq = S // tm
    return pl.pallas_call(
        _inproj_kernel,
        grid=(M // tm,),
        in_specs=[
            pl.BlockSpec((tm, D), lambda i: (i, 0)),
            pl.BlockSpec((1, 6, D), lambda i: (jnp.minimum(i // per_seq, B), 0, 0)),
            pl.BlockSpec((1, D), lambda i: (0, 0)),
            pl.BlockSpec((D, N), lambda i: (0, 0), pipeline_mode=pl.Buffered(1)),
        ],
        out_specs=pl.BlockSpec((tm, N), lambda i: (i, 0)),
        out_shape=jax.ShapeDtypeStruct((M, N), F32),
        compiler_params=_cparams(("parallel",), VMEM_LIMIT),
        name="inproj",
    )(x_all, mod, g1.reshape(1, D), w_in_bf)


def _norm_rope(t, g, cos, sin):
    y = _rms(t) * g
    lane = lax.broadcasted_iota(jnp.int32, y.shape, 1)
    first_half = (lane % 64) < 32
    partner = jnp.where(first_half, pltpu.roll(y, 96, 1), pltpu.roll(y, 32, 1))
    return y * cos + partner * sin


def _qkprep_kernel(p_ref, cos_ref, sin_ref, qg_ref, kg_ref, q_ref, k_ref, v_ref):
    cos = cos_ref[...]
    sin = sin_ref[...]
    scale = HEAD_DIM ** -0.5 * LOG2E
    for h in range(ATT_Q_HEADS):
        t = p_ref[:, COL_QA + h * HEAD_DIM:COL_QA + (h + 1) * HEAD_DIM]
        q_ref[:, h * HEAD_DIM:(h + 1) * HEAD_DIM] = (_norm_rope(t, qg_ref[...], cos, sin) * scale).astype(BF16)
    for h in range(ATT_KV_HEADS):
        t = p_ref[:, COL_KA + h * HEAD_DIM:COL_KA + (h + 1) * HEAD_DIM]
        k_ref[0, :, h * HEAD_DIM:(h + 1) * HEAD_DIM] = _norm_rope(t, kg_ref[...], cos, sin).astype(BF16)
        v_ref[0, :, 2 * h * HEAD_DIM:(2 * h + 1) * HEAD_DIM] = (
            p_ref[:, COL_VA + h * HEAD_DIM:COL_VA + (h + 1) * HEAD_DIM].astype(BF16))
        v_ref[0, :, (2 * h + 1) * HEAD_DIM:(2 * h + 2) * HEAD_DIM] = jnp.ones((p_ref.shape[0], HEAD_DIM), BF16)


def _qkprep(proj, cos_t, sin_t, qg, kg, B, S):
    M = proj.shape[0]
    nbs = S // TM
    nlat = B * nbs

    def b_of(i):
        return jnp.where(i < nlat, i // nbs, i - nlat)

    def pos_of(i):
        return jnp.where(i < nlat, i % nbs, nbs)

    kl = S + TM
    return pl.pallas_call(
        _qkprep_kernel,
        grid=(M // TM,),
        in_specs=[
            pl.BlockSpec((TM, COL_UB), lambda i: (i, 0)),
            pl.BlockSpec((TM, HEAD_DIM), lambda i: (pos_of(i), 0)),
            pl.BlockSpec((TM, HEAD_DIM), lambda i: (pos_of(i), 0)),
            pl.BlockSpec((1, HEAD_DIM), lambda i: (0, 0)),
            pl.BlockSpec((1, HEAD_DIM), lambda i: (0, 0)),
        ],
        out_specs=[
            pl.BlockSpec((TM, ATT_Q_W), lambda i: (i, 0)),
            pl.BlockSpec((1, TM, ATT_KV_W), lambda i: (b_of(i), pos_of(i), 0)),
            pl.BlockSpec((1, TM, 2 * ATT_KV_W), lambda i: (b_of(i), pos_of(i), 0)),
        ],
        out_shape=[
            jax.ShapeDtypeStruct((M, ATT_Q_W), BF16),
            jax.ShapeDtypeStruct((B, kl, ATT_KV_W), BF16),
            jax.ShapeDtypeStruct((B, kl, 2 * ATT_KV_W), BF16),
        ],
        compiler_params=_cparams(("parallel",), VMEM_LIMIT),
        name="qkprep",
    )(proj, cos_t, sin_t, qg.reshape(1, HEAD_DIM), kg.reshape(1, HEAD_DIM))


def _gqa_kernel(q_ref, k_ref, v_ref, o_in_ref, o_ref, q2_ref, sa_ref, sb_ref, m_ref, acc_ref, *, tk):
    del o_in_ref
    tq = q_ref.shape[0]
    q2_ref[0:tq, :] = q_ref[:, :HEAD_DIM]
    q2_ref[tq:, :] = q_ref[:, HEAD_DIM:]
    nk = k_ref.shape[1] // tk
    m_ref[...] = jnp.full(m_ref.shape, -jnp.inf, F32)
    acc_ref[...] = jnp.zeros(acc_ref.shape, F32)

    def scores(c, s_ref):
        off = pl.multiple_of(c * tk, tk)
        s_ref[...] = lax.dot_general(q2_ref[...], k_ref[0, pl.ds(off, tk), :], (((1,), (1,)), ((), ())),
                                     preferred_element_type=F32)

    def update(c, s_ref):
        off = pl.multiple_of(c * tk, tk)
        vc = v_ref[0, pl.ds(off, tk), :]
        m = m_ref[...]
        m_new = jnp.maximum(m, jnp.max(s_ref[...], axis=-1, keepdims=True))
        m_ref[...] = m_new
        p = jnp.exp2((s_ref[...] - jnp.concatenate([m_new] * (tk // HEAD_DIM), axis=1)).astype(BF16))
        alpha = jnp.exp2(m - m_new)
        acc_ref[...] = (jnp.concatenate([alpha, alpha], axis=1) * acc_ref[...]
                        + jnp.dot(p, vc, preferred_element_type=F32))

    scores(0, sa_ref)
    n_pairs = (nk - 1) // 2

    def body(j, carry):
        scores(2 * j + 1, sb_ref)
        update(2 * j, sa_ref)
        scores(2 * j + 2, sa_ref)
        update(2 * j + 1, sb_ref)
        return carry

    lax.fori_loop(0, n_pairs, body, 0)
    if nk % 2 == 0:
        scores(nk - 1, sb_ref)
        update(nk - 2, sa_ref)
        update(nk - 1, sb_ref)
    else:
        update(nk - 1, sa_ref)
    acc = acc_ref[...]
    o = acc[:, :HEAD_DIM] / acc[:, HEAD_DIM:]
    o_ref[:, :HEAD_DIM] = o[:tq].astype(BF16)
    o_ref[:, HEAD_DIM:] = o[tq:].astype(BF16)


def _gqa_latent(q, k_all, v_all, B, S):
    M = q.shape[0]
    kl = k_all.shape[1]
    nbs = S // TM
    tk = 768 if kl % 768 == 0 else TM
    g = ATT_Q_HEADS // ATT_KV_HEADS
    return pl.pallas_call(
        functools.partial(_gqa_kernel, tk=tk),
        grid=(B, ATT_KV_HEADS, nbs),
        in_specs=[
            pl.BlockSpec((TM, g * HEAD_DIM), lambda b, h, i: (b * nbs + i, h)),
            pl.BlockSpec((1, kl, HEAD_DIM), lambda b, h, i: (b, 0, h)),
            pl.BlockSpec((1, kl, 2 * HEAD_DIM), lambda b, h, i: (b, 0, h)),
            pl.BlockSpec(memory_space=pl.ANY),
        ],
        out_specs=pl.BlockSpec((TM, g * HEAD_DIM), lambda b, h, i: (b * nbs + i, h)),
        out_shape=jax.ShapeDtypeStruct((M, ATT_Q_W), BF16),
        input_output_aliases={3: 0},
        scratch_shapes=[
            pltpu.VMEM((g * TM, HEAD_DIM), BF16),
            pltpu.VMEM((g * TM, tk), F32),
            pltpu.VMEM((g * TM, tk), F32),
            pltpu.VMEM((g * TM, HEAD_DIM), F32),
            pltpu.VMEM((g * TM, 2 * HEAD_DIM), F32),
        ],
        compiler_params=_cparams(("parallel", "parallel", "parallel"), VMEM_LIMIT),
        name="gqa_latent",
    )(q, k_all, v_all, jnp.zeros((M, ATT_Q_W), BF16))


def _ctx_attn_kernel(q_ref, k_ref, v_ref, o_in_ref, o_ref, *, scale):
    del o_in_ref
    if scale is None:
        q = q_ref[...]
    else:
        q = (q_ref[...] * (scale * LOG2E)).astype(BF16)
    k = k_ref[...].astype(BF16)
    v = v_ref[...].astype(BF16)
    s = lax.dot_general(q, k, (((1,), (1,)), ((), ())), preferred_element_type=F32)
    p = jnp.exp2(s - jnp.max(s, axis=-1, keepdims=True))
    l = jnp.sum(p, axis=-1, keepdims=True)
    o = jnp.dot(p.astype(BF16), v, preferred_element_type=F32) / l
    o_ref[...] = o.astype(BF16)


def _ctx_attn(q_arr, q_map, k_arr, k_spec, v_arr, v_spec, o_arr, B, n_heads, ctx_blk0, scale):
    return pl.pallas_call(
        functools.partial(_ctx_attn_kernel, scale=scale),
        grid=(B, n_heads),
        in_specs=[
            pl.BlockSpec((TM, HEAD_DIM), q_map),
            k_spec,
            v_spec,
            pl.BlockSpec(memory_space=pl.ANY),
        ],
        out_specs=pl.BlockSpec((TM, HEAD_DIM), lambda b, h: (ctx_blk0 + b, h)),
        out_shape=jax.ShapeDtypeStruct(o_arr.shape, o_arr.dtype),
        input_output_aliases={3: 0},
        compiler_params=_cparams(("parallel", "parallel"), VMEM_LIMIT),
        name="ctx_attn",
    )(q_arr, k_arr, v_arr, o_arr)


def _lru_kernel(u_ref, up_ref, un_ref, *rest, reverse, final):
    if final:
        (g_ref, hrev_ref, cw_ref, cb_ref, wr_ref, br_ref, wi_ref, bi_ref, sp_ref,
         o_ref, xpad, a_scr, b_scr, h_scr, hst) = rest
    else:
        (cw_ref, cb_ref, wr_ref, br_ref, wi_ref, bi_ref, sp_ref,
         o_ref, xpad, a_scr, b_scr, h_scr, hst) = rest
    s = pl.program_id(1)
    ns = pl.num_programs(1)
    T = u_ref.shape[0]
    W = u_ref.shape[1]

    @pl.when(s == 0)
    def _():
        hst[...] = jnp.zeros_like(hst)

    if reverse:
        j = ns - 1 - s
    else:
        j = s - 1
    is_first = jnp.logical_or(s == 0, j == 0)
    is_last = jnp.logical_or(s == 0, j == ns - 2)
    prev_rows = jnp.where(is_first, 0.0, up_ref[...])
    next_rows = jnp.where(is_last, 0.0, un_ref[...])
    xpad[0:8, :] = prev_rows
    xpad[8:8 + T, :] = u_ref[...]
    xpad[8 + T:16 + T, :] = next_rows
    u = (cb_ref[...]
         + xpad[7:7 + T, :] * cw_ref[0:1, :]
         + xpad[8:8 + T, :] * cw_ref[1:2, :]
         + xpad[9:9 + T, :] * cw_ref[2:3, :]
         + xpad[10:10 + T, :] * cw_ref[3:4, :])

    ub = u.astype(BF16)
    nb = LRU_WIDTH // LRU_BLOCKS
    for n in range(LRU_BLOCKS):
        sl = slice(n * nb, (n + 1) * nb)
        un_ = ub[:, sl]
        r = jax.nn.sigmoid(jnp.dot(un_, wr_ref[n], preferred_element_type=F32) + br_ref[:, sl])
        ig = jax.nn.sigmoid(jnp.dot(un_, wi_ref[n], preferred_element_type=F32) + bi_ref[:, sl])
        log_a = (-LRU_C) * r * sp_ref[:, sl]
        a = jnp.exp(log_a)
        a_scr[:, sl] = a
        one_minus_a2 = -jnp.tanh(log_a) * (a * a + 1.0)
        b_scr[:, sl] = jnp.sqrt(one_minus_a2) * (ig * u[:, sl])

    def step(t, h):
        row = (T - 1 - t) if reverse else t
        h = a_scr[pl.ds(row, 1), :] * h + b_scr[pl.ds(row, 1), :]
        h_scr[pl.ds(row, 1), :] = h
        return h

    hst[...] = lax.fori_loop(0, T, step, hst[...], unroll=8)

    if final:
        y = h_scr[...] + hrev_ref[...]
        o_ref[...] = (y * jax.nn.gelu(g_ref[...])).astype(o_ref.dtype)
    else:
        o_ref[...] = h_scr[...]
    del W


def _lru_dir(proj, hrev, conv_w, conv_b, wr, br, wi, bi, sp, B, S, reverse):
    M = proj.shape[0]
    nbs = S // TM
    nlat = B * nbs
    final = hrev is not None
    ub_blk = COL_UB // LRU_WIDTH
    gb_blk = COL_GB // LRU_WIDTH
    r8 = TM // 8
    n8 = M // 8

    def blk(b, s):
        j = (nbs - s) if reverse else (s - 1)
        return jnp.where(s == 0, nlat + b, b * nbs + j)

    in_specs = [
        pl.BlockSpec((TM, LRU_WIDTH), lambda b, s: (blk(b, s), ub_blk)),
        pl.BlockSpec((8, LRU_WIDTH), lambda b, s: (jnp.maximum(blk(b, s) * r8 - 1, 0), ub_blk)),
        pl.BlockSpec((8, LRU_WIDTH), lambda b, s: (jnp.minimum((blk(b, s) + 1) * r8, n8 - 1), ub_blk)),
    ]
    args = [proj, proj, proj]
    if final:
        in_specs += [
            pl.BlockSpec((TM, LRU_WIDTH), lambda b, s: (blk(b, s), gb_blk)),
            pl.BlockSpec((TM, LRU_WIDTH), lambda b, s: (blk(b, s), 0)),
        ]
        args += [proj, hrev]
    const2 = lambda b, s: (0, 0)
    in_specs += [
        pl.BlockSpec((CONV_W, LRU_WIDTH), const2),
        pl.BlockSpec((1, LRU_WIDTH), const2),
        pl.BlockSpec((LRU_BLOCKS, LRU_WIDTH // LRU_BLOCKS, LRU_WIDTH // LRU_BLOCKS), lambda b, s: (0, 0, 0)),
        pl.BlockSpec((1, LRU_WIDTH), const2),
        pl.BlockSpec((LRU_BLOCKS, LRU_WIDTH // LRU_BLOCKS, LRU_WIDTH // LRU_BLOCKS), lambda b, s: (0, 0, 0)),
        pl.BlockSpec((1, LRU_WIDTH), const2),
        pl.BlockSpec((1, LRU_WIDTH), const2),
    ]
    args += [conv_w, conv_b.reshape(1, -1), wr.astype(BF16), br.reshape(1, -1), wi.astype(BF16),
             bi.reshape(1, -1), sp.reshape(1, -1)]
    return pl.pallas_call(
        functools.partial(_lru_kernel, reverse=reverse, final=final),
        grid=(B, nbs + 1),
        in_specs=in_specs,
        out_specs=pl.BlockSpec((TM, LRU_WIDTH), lambda b, s: (blk(b, s), 0)),
        out_shape=jax.ShapeDtypeStruct((M, LRU_WIDTH), BF16 if final else F32),
        scratch_shapes=[
            pltpu.VMEM((TM + 16, LRU_WIDTH), F32),
            pltpu.VMEM((TM, LRU_WIDTH), F32),
            pltpu.VMEM((TM, LRU_WIDTH), F32),
            pltpu.VMEM((TM, LRU_WIDTH), F32),
            pltpu.VMEM((1, LRU_WIDTH), F32),
        ],
        compiler_params=_cparams(("parallel", "arbitrary"), VMEM_LIMIT),
        name="lru_fwd" if final else "lru_rev",
    )(*args)


def _na_kernel(q_ref, k0_ref, k1_ref, k2_ref, v0_ref, v1_ref, v2_ref, kc_ref, vc_ref, tl_ref, tr_ref,
               o_in_ref, o_ref, s_scr, *, n_rows):
    del o_in_ref
    i = pl.program_id(1)
    nbs = n_rows // NA_QROWS
    r0 = NA_QROWS * i
    ws = NA_QROWS * jnp.clip(i - 1, 0, nbs - 3)
    scale = HEAD_DIM ** -0.5
    nloc = NA_WROWS * GRID_W
    for h in range(NA_HEADS):
        hs = slice(h * HEAD_DIM, (h + 1) * HEAD_DIM)
        qh = (q_ref[:, hs] * scale).astype(BF16)
        kh = jnp.concatenate([k0_ref[:, hs], k1_ref[:, hs], k2_ref[:, hs], kc_ref[:, hs]], axis=0).astype(BF16)
        vh = jnp.concatenate([v0_ref[:, hs], v1_ref[:, hs], v2_ref[:, hs], vc_ref[:, hs]], axis=0).astype(BF16)
        s_scr[...] = lax.dot_general(qh, kh, (((1,), (1,)), ((), ())), preferred_element_type=F32)
        for qr in range(NA_QROWS):
            r = r0 + qr
            rs = jnp.clip(r - NA_ROWS // 2, 0, n_rows - NA_ROWS)

            def tile_of(a):
                kr = ws + a
                valid = jnp.logical_and(kr >= rs, kr < rs + NA_ROWS)
                return jnp.where(valid, kr - r + (NA_ROWS - 1), 2 * NA_ROWS - 1)

            for pr in range(NA_WROWS // 2):
                bias = tl_ref[h, tile_of(2 * pr)] + tr_ref[h, tile_of(2 * pr + 1)]
                rsl = slice(qr * GRID_W, (qr + 1) * GRID_W)
                csl = slice(pr * 2 * GRID_W, (pr + 1) * 2 * GRID_W)
                s_scr[rsl, csl] = s_scr[rsl, csl] + bias
        s = s_scr[...]
        p = jnp.exp(s - jnp.max(s, axis=-1, keepdims=True))
        l = jnp.sum(p, axis=-1, keepdims=True)
        o = jnp.dot(p.astype(BF16), vh, preferred_element_type=F32) / l
        o_ref[:, hs] = o.astype(BF16)
    del nloc


def _na_latent(proj, tl, tr, B, S):
    M = proj.shape[0]
    nbs = S // TM
    nlat = B * nbs
    n_rows = S // GRID_W
    qb, kb, vb = COL_QN // NA_W, COL_KN // NA_W, COL_VN // NA_W

    def wblk(b, i, t):
        return b * nbs + jnp.clip(i - 1, 0, nbs - 3) + t

    blk = (TM, NA_W)
    in_specs = [pl.BlockSpec(blk, lambda b, i: (b * nbs + i, qb))]
    in_specs += [pl.BlockSpec(blk, functools.partial(lambda b, i, t: (wblk(b, i, t), kb), t=t)) for t in range(3)]
    in_specs += [pl.BlockSpec(blk, functools.partial(lambda b, i, t: (wblk(b, i, t), vb), t=t)) for t in range(3)]
    in_specs += [
        pl.BlockSpec(blk, lambda b, i: (nlat + b, kb)),
        pl.BlockSpec(blk, lambda b, i: (nlat + b, vb)),
        pl.BlockSpec(tl.shape, lambda b, i: (0, 0, 0, 0)),
        pl.BlockSpec(tr.shape, lambda b, i: (0, 0, 0, 0)),
        pl.BlockSpec(memory_space=pl.ANY),
    ]
    return pl.pallas_call(
        functools.partial(_na_kernel, n_rows=n_rows),
        grid=(B, nbs),
        in_specs=in_specs,
        out_specs=pl.BlockSpec(blk, lambda b, i: (b * nbs + i, 0)),
        out_shape=jax.ShapeDtypeStruct((M, NA_W), BF16),
        input_output_aliases={11: 0},
        scratch_shapes=[pltpu.VMEM((TM, NA_WROWS * GRID_W + TM), F32)],
        compiler_params=_cparams(("parallel", "parallel"), VMEM_LIMIT),
        name="na_latent",
    )(proj, proj, proj, proj, proj, proj, proj, proj, proj, tl, tr, jnp.zeros((M, NA_W), BF16))


def _na_bias_tables(rpb):
    j = np.arange(GRID_W)
    cs = np.clip(j - NA_COLS // 2, 0, GRID_W - NA_COLS)
    kc = np.arange(GRID_W)
    inside = (kc[None, :] >= cs[:, None]) & (kc[None, :] < cs[:, None] + NA_COLS)
    dc = np.clip(kc[None, :] - j[:, None] + (NA_COLS - 1), 0, 2 * NA_COLS - 2)
    t = jnp.where(inside[None, None], rpb[:, :, dc], NEG_BIAS)
    t = jnp.concatenate([t, jnp.full_like(t[:, :1], NEG_BIAS)], axis=1)
    z = jnp.zeros_like(t)
    return jnp.concatenate([t, z], axis=-1), jnp.concatenate([z, t], axis=-1)


def _outproj_kernel(oa_ref, ob_ref, oc_ref, x_ref, mod_ref, g_ref, w_ref, wr_ref, br_ref,
                    xo_ref, h2_ref, lg_ref):
    k1 = ATT_Q_W
    k2 = k1 + LRU_WIDTH
    acc = jnp.dot(oa_ref[...], w_ref[0:k1, :], preferred_element_type=F32)
    acc = acc + jnp.dot(ob_ref[...], w_ref[k1:k2, :], preferred_element_type=F32)
    acc = acc + jnp.dot(oc_ref[...], w_ref[k2:, :], preferred_element_type=F32)
    xn = x_ref[...] + mod_ref[0, 2:3, :] * acc
    xo_ref[...] = xn
    h2 = _rms(xn) * g_ref[...]
    h2 = h2 * (1.0 + mod_ref[0, 4:5, :]) + mod_ref[0, 3:4, :]
    h2_ref[...] = _pack_bf16_pair(h2)
    lg_ref[...] = jnp.dot(h2.astype(BF16), wr_ref[...], preferred_element_type=F32) + br_ref[...]


def _pack_bf16_pair(x):
    w = x.shape[1] // 2
    lo = pltpu.bitcast(x[:, :w].astype(BF16).astype(F32), jnp.uint32)
    hi = pltpu.bitcast(x[:, w:].astype(BF16).astype(F32), jnp.uint32)
    return hi | (lo >> 16)


def _unpack_bf16_pair(words):
    lo = pltpu.bitcast(words << 16, F32)
    hi = pltpu.bitcast(words & jnp.uint32(0xFFFF0000), F32)
    return jnp.concatenate([lo, hi], axis=1).astype(BF16)


def _outproj(oa, ob, oc, x_all, mod, g2, w_out_bf, w_router, b_router, seg_of, nblk):
    M, D = x_all.shape
    packed = jax.eval_shape(_pack_bf16_pair, jax.ShapeDtypeStruct((TM, D), F32))
    return pl.pallas_call(
        _outproj_kernel,
        grid=(nblk,),
        in_specs=[
            pl.BlockSpec((TM, ATT_Q_W), lambda i: (i, 0)),
            pl.BlockSpec((TM, LRU_WIDTH), lambda i: (i, 0)),
            pl.BlockSpec((TM, NA_W), lambda i: (i, 0)),
            pl.BlockSpec((TM, D), lambda i: (i, 0)),
            pl.BlockSpec((1, 6, D), lambda i: (seg_of(i), 0, 0)),
            pl.BlockSpec((1, D), lambda i: (0, 0)),
            pl.BlockSpec(w_out_bf.shape, lambda i: (0, 0)),
            pl.BlockSpec(w_router.shape, lambda i: (0, 0)),
            pl.BlockSpec((1, 128), lambda i: (0, 0)),
        ],
        out_specs=[
            pl.BlockSpec((TM, D), lambda i: (i, 0)),
            pl.BlockSpec(packed.shape, lambda i: (i, 0)),
            pl.BlockSpec((TM, 128), lambda i: (i, 0)),
        ],
        out_shape=[
            jax.ShapeDtypeStruct((M, D), F32),
            jax.ShapeDtypeStruct((nblk * TM, packed.shape[1]), packed.dtype),
            jax.ShapeDtypeStruct((nblk * TM, 128), F32),
        ],
        input_output_aliases={3: 0},
        compiler_params=_cparams(("parallel",), VMEM_LIMIT),
        name="outproj",
    )(oa, ob, oc, x_all, mod, g2.reshape(1, D), w_out_bf, w_router, b_router)


def _route_kernel(lg_ref, o_ref):
    x = lg_ref[...]
    lane = lax.broadcasted_iota(jnp.int32, x.shape, 1)
    big = jnp.int32(1 << 20)
    ninf = -jnp.inf

    def first_argmax(vals):
        m = jnp.max(vals, axis=-1, keepdims=True)
        idx = jnp.min(jnp.where(vals == m, lane, big), axis=-1, keepdims=True)
        return m, idx

    lg = jnp.where(lane < MOE_GROUPS, x, ninf)
    mg, g_star = first_argmax(lg)
    g_gate = 1.0 / jnp.sum(jnp.exp(lg - mg), axis=-1, keepdims=True)
    lo = MOE_GROUPS + MOE_EXPERTS_PER_GROUP * g_star
    le = jnp.where(jnp.logical_and(lane >= lo, lane < lo + MOE_EXPERTS_PER_GROUP), x, ninf)
    v1, i1 = first_argmax(le)
    v2, i2 = first_argmax(jnp.where(lane == i1, ninf, le))
    e21 = jnp.exp(v2 - v1)
    w1 = g_gate / (1.0 + e21)
    w2 = g_gate * e21 / (1.0 + e21)
    e1 = (i1 - MOE_GROUPS).astype(F32)
    e2 = (i2 - MOE_GROUPS).astype(F32)
    o_ref[...] = jnp.where(lane == 0, e1, jnp.where(lane == 1, e2, jnp.where(lane == 2, w1,
                           jnp.where(lane == 3, w2, 0.0))))


def _route(logits, nblk):
    M = logits.shape[0]
    return pl.pallas_call(
        _route_kernel,
        grid=(nblk,),
        in_specs=[pl.BlockSpec((TM, 128), lambda i: (i, 0))],
        out_specs=pl.BlockSpec((TM, 128), lambda i: (i, 0)),
        out_shape=jax.ShapeDtypeStruct((M, 128), F32),
        compiler_params=_cparams(("parallel",)),
        name="route",
    )(logits)


def _rank_kernel(r_ref, rank_ref, cnt_ref, carry):
    @pl.when(pl.program_id(0) == 0)
    def _():
        carry[...] = jnp.zeros_like(carry)

    r = r_ref[...]
    lane = lax.broadcasted_iota(jnp.int32, r.shape, 1).astype(F32)
    oh0 = jnp.where(lane == r[:, 0:1], 1.0, 0.0)
    oh1 = jnp.where(lane == r[:, 1:2], 1.0, 0.0)
    oh = oh0 + oh1
    n = r.shape[0]
    row = lax.broadcasted_iota(jnp.int32, (n, n), 0)
    col = lax.broadcasted_iota(jnp.int32, (n, n), 1)
    tri = jnp.where(col < row, 1.0, 0.0).astype(BF16)
    tot = carry[...] + jnp.dot(tri, oh.astype(BF16), preferred_element_type=F32)
    rank0 = jnp.sum(oh0 * tot, axis=-1, keepdims=True)
    rank1 = jnp.sum(oh1 * tot, axis=-1, keepdims=True)
    lane_i = lax.broadcasted_iota(jnp.int32, r.shape, 1)
    rank_ref[...] = jnp.where(lane_i == 0, rank0, jnp.where(lane_i == 1, rank1, 0.0))
    carry[...] = carry[...] + jnp.sum(oh, axis=0, keepdims=True)
    cnt_ref[...] = jnp.broadcast_to(carry[...], cnt_ref.shape)


def _rank(route, nblk):
    M = route.shape[0]
    return pl.pallas_call(
        _rank_kernel,
        grid=(nblk,),
        in_specs=[pl.BlockSpec((TM, 128), lambda i: (i, 0))],
        out_specs=[pl.BlockSpec((TM, 128), lambda i: (i, 0)), pl.BlockSpec((8, 128), lambda i: (0, 0))],
        out_shape=[jax.ShapeDtypeStruct((M, 128), F32), jax.ShapeDtypeStruct((8, 128), F32)],
        scratch_shapes=[pltpu.VMEM((1, 128), F32)],
        compiler_params=_cparams(("arbitrary",)),
        name="rank",
    )(route)


def _dispatch_kernel(dest_ref, h_ref, xs_in, xs_hbm, sem):
    del xs_in
    i = pl.program_id(0)

    def issue(t, c):
        tok = i * TM + t
        for k in range(2):
            pltpu.make_async_copy(h_ref.at[pl.ds(t, 1)], xs_hbm.at[pl.ds(dest_ref[2 * tok + k], 1)], sem).start()
        return c

    lax.fori_loop(0, TM, issue, 0, unroll=8)
    for k in range(2):
        pltpu.make_async_copy(h_ref, xs_hbm.at[pl.ds(0, TM)], sem).wait()


def _dispatch(dest_flat, h2, xs_zero, nblk):
    D = h2.shape[1]
    return pl.pallas_call(
        _dispatch_kernel,
        grid_spec=pltpu.PrefetchScalarGridSpec(
            num_scalar_prefetch=1,
            grid=(nblk,),
            in_specs=[pl.BlockSpec((TM, D), lambda i, d: (i, 0)), pl.BlockSpec(memory_space=pl.ANY)],
            out_specs=pl.BlockSpec(memory_space=pl.ANY),
            scratch_shapes=[pltpu.SemaphoreType.DMA(())],
        ),
        out_shape=jax.ShapeDtypeStruct(xs_zero.shape, xs_zero.dtype),
        input_output_aliases={2: 0},
        compiler_params=pltpu.CompilerParams(dimension_semantics=("arbitrary",), has_side_effects=True),
        name="dispatch",
    )(dest_flat, h2, xs_zero)


PLAN_EXPERT, PLAN_FIRST, PLAN_SLOT, PLAN_NEXT = 0, 1, 2, 3


def _block_plan(block_e, n_used):
    nb = block_e.shape[0]
    idx = jnp.arange(nb, dtype=jnp.int32)
    is_first = (idx < n_used[0]) & ((idx == 0) | (block_e != jnp.roll(block_e, 1)))
    seg = jnp.cumsum(is_first.astype(jnp.int32)) - 1
    first_pos = jnp.where(is_first, idx, nb)
    at_or_after = jnp.flip(lax.cummin(jnp.flip(first_pos)))
    next_first = jnp.concatenate([at_or_after[1:], jnp.full((1,), nb, jnp.int32)])
    next_e = jnp.where(next_first < nb, block_e[jnp.minimum(next_first, nb - 1)], -1)
    return jnp.stack([block_e, is_first.astype(jnp.int32), seg % 2, next_e]).astype(jnp.int32)


def _expert_weights_step(i, plan_ref, w_hbms, wbuf, wbf, sem, layer):
    def copies(e, slot):
        return [pltpu.make_async_copy(w.at[layer, e], wbuf.at[slot, n], sem.at[slot, n])
                for n, w in enumerate(w_hbms)]

    @pl.when(plan_ref[PLAN_FIRST, i] == 1)
    def _():
        slot = plan_ref[PLAN_SLOT, i]

        @pl.when(i == 0)
        def _():
            for c in copies(plan_ref[PLAN_EXPERT, 0], 0):
                c.start()

        for c in copies(plan_ref[PLAN_EXPERT, i], slot):
            c.wait()
        nrow, ncol = wbuf.shape[2], wbuf.shape[3]
        rows = 256

        def cast_rows(r, c):
            off = pl.multiple_of(r * rows, rows)
            for n in range(len(w_hbms)):
                wbf[pl.ds(off, rows), n * ncol:(n + 1) * ncol] = wbuf[slot, n, pl.ds(off, rows), :].astype(BF16)
            return c

        lax.fori_loop(0, nrow // rows, cast_rows, 0)
        nxt = plan_ref[PLAN_NEXT, i]

        @pl.when(nxt >= 0)
        def _():
            for c in copies(nxt, 1 - slot):
                c.start()


def _moe1_kernel(plan_ref, nu_ref, x_ref, wg_hbm, wu_hbm, o_ref, wbuf, wcat, sem, *, layer):
    i = pl.program_id(0)

    @pl.when(i < nu_ref[0])
    def _():
        _expert_weights_step(i, plan_ref, (wg_hbm, wu_hbm), wbuf, wcat, sem, layer)
        h = jnp.dot(_unpack_bf16_pair(x_ref[...]), wcat[...], preferred_element_type=F32)
        g = h[:, :MOE_HIDDEN]
        u = h[:, MOE_HIDDEN:]
        o_ref[...] = (g * jax.nn.sigmoid(g) * u).astype(BF16)

    @pl.when(i >= nu_ref[0])
    def _():
        o_ref[...] = jnp.zeros_like(o_ref)


def _moe1(plan, n_used, xs, w_gate, w_up, layer):
    P, W = xs.shape
    D = w_gate.shape[2]
    nb = P // MOE_BM
    used_blk = lambda i, plan, nu: (jnp.minimum(i, nu[0] - 1), 0)
    return pl.pallas_call(
        functools.partial(_moe1_kernel, layer=layer),
        grid_spec=pltpu.PrefetchScalarGridSpec(
            num_scalar_prefetch=2,
            grid=(nb,),
            in_specs=[pl.BlockSpec((MOE_BM, W), used_blk), pl.BlockSpec(memory_space=pl.ANY),
                      pl.BlockSpec(memory_space=pl.ANY)],
            out_specs=pl.BlockSpec((MOE_BM, MOE_HIDDEN), lambda i, plan, nu: (i, 0)),
            scratch_shapes=[pltpu.VMEM((2, 2, D, MOE_HIDDEN), F32), pltpu.VMEM((D, 2 * MOE_HIDDEN), BF16),
                            pltpu.SemaphoreType.DMA((2, 2))],
        ),
        out_shape=jax.ShapeDtypeStruct((P, MOE_HIDDEN), BF16),
        compiler_params=_cparams(("arbitrary",), VMEM_LIMIT),
        name="moe_up",
    )(plan, n_used, xs, w_gate, w_up)


def _moe2_kernel(plan_ref, nu_ref, h_ref, wd_hbm, o_ref, wbuf, wbf, sem, *, layer):
    i = pl.program_id(0)

    @pl.when(i < nu_ref[0])
    def _():
        _expert_weights_step(i, plan_ref, (wd_hbm,), wbuf, wbf, sem, layer)
        o_ref[...] = jnp.dot(h_ref[...], wbf[...], preferred_element_type=F32)

    @pl.when(i >= nu_ref[0])
    def _():
        o_ref[...] = jnp.zeros_like(o_ref)


def _moe2(plan, n_used, hmid, w_down, layer):
    P = hmid.shape[0]
    D = w_down.shape[3]
    nb = P // MOE_BM
    return pl.pallas_call(
        functools.partial(_moe2_kernel, layer=layer),
        grid_spec=pltpu.PrefetchScalarGridSpec(
            num_scalar_prefetch=2,
            grid=(nb,),
            in_specs=[
                pl.BlockSpec((MOE_BM, MOE_HIDDEN), lambda i, plan, nu: (i, 0)),
                pl.BlockSpec(memory_space=pl.ANY),
            ],
            out_specs=pl.BlockSpec((MOE_BM, D), lambda i, plan, nu: (i, 0)),
            scratch_shapes=[pltpu.VMEM((2, 1, MOE_HIDDEN, D), F32), pltpu.VMEM((MOE_HIDDEN, D), BF16),
                            pltpu.SemaphoreType.DMA((2, 1))],
        ),
        out_shape=jax.ShapeDtypeStruct((P, D), F32),
        compiler_params=_cparams(("arbitrary",), VMEM_LIMIT),
        name="moe_down",
    )(plan, n_used, hmid, w_down)


def _combine_kernel(dest_ref, yb_hbm, x_ref, r_ref, mod_ref, fg_ref, o_ref, ybuf, sem, *, final):
    i = pl.program_id(0)
    n = pl.num_programs(0)

    def row_copy(tok, k, t, slot):
        return pltpu.make_async_copy(yb_hbm.at[pl.ds(dest_ref[2 * tok + k], 1)],
                                     ybuf.at[slot, k, pl.ds(t, 1)], sem.at[slot])

    def issue(blk, slot):
        def body(t, c):
            tok = blk * TM + t
            row_copy(tok, 0, t, slot).start()
            row_copy(tok, 1, t, slot).start()
            return c
        lax.fori_loop(0, TM, body, 0, unroll=8)

    @pl.when(i == 0)
    def _():
        issue(0, 0)

    @pl.when(i + 1 < n)
    def _():
        issue(i + 1, (i + 1) % 2)

    slot = i % 2

    for k in range(2):
        pltpu.make_async_copy(yb_hbm.at[pl.ds(0, TM)], ybuf.at[slot, k], sem.at[slot]).wait()
    r = r_ref[...]
    y = r[:, 2:3] * ybuf[slot, 0] + r[:, 3:4] * ybuf[slot, 1]
    xn = x_ref[...] + mod_ref[0, 5:6, :] * y
    if final:
        xn = _rms(xn) * fg_ref[...]
    o_ref[...] = xn


def _combine(dest_flat, yb, x_all, route, mod, final_g, seg_of, nblk, final):
    M, D = x_all.shape
    out_rows = nblk * TM if final else M
    kwargs = {} if final else {"input_output_aliases": {2: 0}}
    return pl.pallas_call(
        functools.partial(_combine_kernel, final=final),
        grid_spec=pltpu.PrefetchScalarGridSpec(
            num_scalar_prefetch=1,
            grid=(nblk,),
            in_specs=[
                pl.BlockSpec(memory_space=pl.ANY),
                pl.BlockSpec((TM, D), lambda i, d: (i, 0)),
                pl.BlockSpec((TM, 128), lambda i, d: (i, 0)),
                pl.BlockSpec((1, 6, D), lambda i, d: (seg_of(i), 0, 0)),
                pl.BlockSpec((1, D), lambda i, d: (0, 0)),
            ],
            out_specs=pl.BlockSpec((TM, D), lambda i, d: (i, 0)),
            scratch_shapes=[pltpu.VMEM((2, 2, TM, D), F32), pltpu.SemaphoreType.DMA((2,))],
        ),
        out_shape=jax.ShapeDtypeStruct((out_rows, D), F32),
        compiler_params=_cparams(("arbitrary",), VMEM_LIMIT),
        name="combine",
        **kwargs,
    )(dest_flat, yb, x_all, route, mod, final_g.reshape(1, D))


def _moe_layer(x_all, h2, logits, mod, final_g, w_gate, w_up, w_down, layer, seg_of, nblk, final):
    T = nblk * TM
    route = _route(logits, nblk)
    rank, cnt = _rank(route, nblk)
    e_ids = route[:T, :2].astype(jnp.int32)
    ranks = rank[:T, :2].astype(jnp.int32)
    counts = cnt[0, :MOE_EXPERTS].astype(jnp.int32)
    padded = (counts + MOE_BM - 1) // MOE_BM * MOE_BM
    pad_ends = jnp.cumsum(padded)
    pad_starts = pad_ends - padded
    dest = (pad_starts[e_ids] + ranks).reshape(-1)
    n_blocks = -(-(2 * T + MOE_EXPERTS * (MOE_BM - 1)) // MOE_BM)
    block_starts = jnp.arange(n_blocks, dtype=jnp.int32) * MOE_BM
    block_e = jnp.sum((block_starts[:, None] >= pad_ends[None, :]).astype(jnp.int32), axis=1)
    block_e = jnp.minimum(block_e, MOE_EXPERTS - 1)
    n_used = (pad_ends[-1:] // MOE_BM).astype(jnp.int32)
    plan = _block_plan(block_e, n_used)
    xs = _dispatch(dest, h2, jnp.zeros((n_blocks * MOE_BM, h2.shape[1]), h2.dtype), nblk)
    hmid = _moe1(plan, n_used, xs, w_gate, w_up, layer)
    yb = _moe2(plan, n_used, hmid, w_down, layer)
    return _combine(dest, yb, x_all, route, mod, final_g, seg_of, nblk, final)


def _rope_tables(S):
    pos = jnp.arange(S, dtype=jnp.int32)
    rows = (pos // GRID_W).astype(F32)
    cols = (pos % GRID_W).astype(F32)
    n_freq = HEAD_DIM // 4
    inv = 1.0 / (ROPE_THETA ** (jnp.arange(n_freq, dtype=F32) / n_freq))
    cr, sr = jnp.cos(rows[:, None] * inv), jnp.sin(rows[:, None] * inv)
    cc, sc = jnp.cos(cols[:, None] * inv), jnp.sin(cols[:, None] * inv)
    cos_t = jnp.concatenate([cr, cr, cc, cc], axis=1)
    sin_t = jnp.concatenate([-sr, sr, -sc, sc], axis=1)
    cos_t = jnp.concatenate([cos_t, jnp.ones((TM, HEAD_DIM), F32)], axis=0)
    sin_t = jnp.concatenate([sin_t, jnp.zeros((TM, HEAD_DIM), F32)], axis=0)
    return cos_t, sin_t


def kernel(x, c, ctx, c_ctx, ada_w, ada_b, norm1_g, norm2_g, w_in, w_out, att_q_norm, att_k_norm, conv_w, conv_b, lru_wr, lru_br, lru_wi, lru_bi, lru_lambda, na_rpb, router_wg, router_bg, router_we, router_be, moe_w_gate, moe_w_up, moe_w_down, final_g):
    B, S, D = x.shape
    C = ctx.shape[1]
    L = ada_w.shape[0]
    assert C == TM and S % TM == 0 and S // TM >= 3 and B + 1 <= 8
    nbs = S // TM
    nlat = B * nbs
    m_lat = B * S
    M = m_lat + B * C

    def seg_of(i):
        return jnp.minimum(i // nbs, B)

    x_all = jnp.concatenate([x.reshape(m_lat, D), ctx.reshape(B * C, D)], axis=0)
    cvec = jnp.zeros((8, D), F32).at[:B].set(c).at[B].set(c_ctx)
    mod_all = _adaln(cvec, ada_w, ada_b)[:, :B + 1].reshape(L, B + 1, 6, D)
    cos_t, sin_t = _rope_tables(S)
    g_att = ATT_Q_HEADS // ATT_KV_HEADS

    out = None
    for l in range(L):
        last = l == L - 1
        mod = mod_all[l]
        proj = _inproj(x_all, mod, norm1_g[l], w_in[l].astype(BF16), B, S)

        q, k_all, v_all = _qkprep(proj, cos_t, sin_t, att_q_norm[l], att_k_norm[l], B, S)
        oa = _gqa_latent(q, k_all, v_all, B, S)

        sp = jax.nn.softplus(-lru_lambda[l].astype(F32))
        hrev = _lru_dir(proj, None, conv_w[l], conv_b[l], lru_wr[l, 1], lru_br[l, 1], lru_wi[l, 1],
                        lru_bi[l, 1], sp[1], B, S, reverse=True)
        ob = _lru_dir(proj, hrev, conv_w[l], conv_b[l], lru_wr[l, 0], lru_br[l, 0], lru_wi[l, 0],
                      lru_bi[l, 0], sp[0], B, S, reverse=False)

        tl, tr = _na_bias_tables(na_rpb[l])
        oc = _na_latent(proj, tl, tr, B, S)

        if not last:
            oa = _ctx_attn(
                q, lambda b, h: (nlat + b, h),
                k_all, pl.BlockSpec((None, TM, HEAD_DIM), lambda b, h: (b, nbs, h // g_att)),
                v_all, pl.BlockSpec((None, TM, HEAD_DIM), lambda b, h: (b, nbs, 2 * (h // g_att))),
                oa, B, ATT_Q_HEADS, nlat, None)
            oc = _ctx_attn(
                proj, lambda b, h: (nlat + b, COL_QN // HEAD_DIM + h),
                proj, pl.BlockSpec((TM, HEAD_DIM), lambda b, h: (nlat + b, COL_KN // HEAD_DIM + h)),
                proj, pl.BlockSpec((TM, HEAD_DIM), lambda b, h: (nlat + b, COL_VN // HEAD_DIM + h)),
                oc, B, NA_HEADS, nlat, HEAD_DIM ** -0.5)

        nblk = nlat if last else M // TM
        w_router = jnp.zeros((D, 128), F32).at[:, :MOE_GROUPS].set(router_wg[l])
        w_router = w_router.at[:, MOE_GROUPS:MOE_GROUPS + MOE_EXPERTS].set(router_we[l])
        b_router = jnp.zeros((1, 128), F32).at[0, :MOE_GROUPS].set(router_bg[l])
        b_router = b_router.at[0, MOE_GROUPS:MOE_GROUPS + MOE_EXPERTS].set(router_be[l])
        x_all, h2, logits = _outproj(oa, ob, oc, x_all, mod, norm2_g[l], w_out[l].astype(BF16),
                                     w_router.astype(BF16), b_router, seg_of, nblk)
        res = _moe_layer(x_all, h2, logits, mod, final_g, moe_w_gate, moe_w_up, moe_w_down, l,
                         seg_of, nblk, last)
        if last:
            out = res
        else:
            x_all = res
    return out.reshape(B, S, D)
```

```python
import functools

import numpy as np
import jax
import jax.numpy as jnp
from jax import lax
from jax.experimental import pallas as pl
from jax.experimental.pallas import tpu as pltpu

F32 = jnp.float32
BF16 = jnp.bfloat16

GRID_W = 64
HEAD_DIM = 128
ATT_Q_HEADS = 4
ATT_KV_HEADS = 2
LRU_WIDTH = 1024
LRU_BLOCKS = 8
LRU_C = 8.0
CONV_W = 4
NA_HEADS = 4
NA_ROWS = 8
NA_COLS = 16
ATT_Q_W = ATT_Q_HEADS * HEAD_DIM
ATT_KV_W = ATT_KV_HEADS * HEAD_DIM
NA_W = NA_HEADS * HEAD_DIM
MOE_GROUPS = 4
MOE_EXPERTS_PER_GROUP = 8
MOE_EXPERTS = MOE_GROUPS * MOE_EXPERTS_PER_GROUP
MOE_HIDDEN = 1024
ROPE_THETA = 10000.0
EPS = 1e-6
LOG2E = 1.4426950408889634

COL_QA = 0
COL_KA = COL_QA + ATT_Q_W
COL_VA = COL_KA + ATT_KV_W
COL_UB = COL_VA + ATT_KV_W
COL_GB = COL_UB + LRU_WIDTH
COL_QN = COL_GB + LRU_WIDTH
COL_KN = COL_QN + NA_W
COL_VN = COL_KN + NA_W
IN_WIDTH = COL_VN + NA_W

TM = 256
MOE_BM = 256
NA_QROWS = 4
NA_WROWS = 12
NEG_BIAS = -1e30
VMEM_LIMIT = 56 * 1024 * 1024


def _cparams(sem, vmem=None):
    return pltpu.CompilerParams(dimension_semantics=sem, vmem_limit_bytes=vmem)


def _rms(x):
    return x * lax.rsqrt(jnp.mean(x * x, axis=-1, keepdims=True) + EPS)


def _adaln_kernel(c_ref, w_ref, b_ref, o_ref):
    cv = c_ref[...]
    s = cv * jax.nn.sigmoid(cv)
    o_ref[0] = jnp.dot(s.astype(BF16), w_ref[0].astype(BF16), preferred_element_type=F32) + b_ref[0]


def _adaln(cvec, ada_w, ada_b):
    L, D, N = ada_w.shape
    tn = 1024
    return pl.pallas_call(
        _adaln_kernel,
        grid=(L, N // tn),
        in_specs=[
            pl.BlockSpec((8, D), lambda l, j: (0, 0)),
            pl.BlockSpec((1, D, tn), lambda l, j: (l, 0, j)),
            pl.BlockSpec((1, 1, tn), lambda l, j: (l, 0, j)),
        ],
        out_specs=pl.BlockSpec((1, 8, tn), lambda l, j: (l, 0, j)),
        out_shape=jax.ShapeDtypeStruct((L, 8, N), F32),
        compiler_params=_cparams(("parallel", "parallel"), VMEM_LIMIT),
        name="adaln",
    )(cvec, ada_w, ada_b.reshape(L, 1, N))


def _inproj_kernel(*refs, n_first, two_sources):
    if two_sources:
        x_ref, xb_ref, mod_ref, g_ref, w_ref, o_ref = refs
        x = jnp.where(pl.program_id(0) < n_first, x_ref[...], xb_ref[...])
    else:
        x_ref, mod_ref, g_ref, w_ref, o_ref = refs
        x = x_ref[...]
    h = _rms(x) * g_ref[...]
    h = h * (1.0 + mod_ref[0, 1:2, :]) + mod_ref[0, 0:1, :]
    o_ref[...] = jnp.dot(h.astype(BF16), w_ref[...], preferred_element_type=F32)


def _inproj(x_rows, ctx_rows, mod, g1, w_in_bf, B, S):
    two = ctx_rows is not None
    D = x_rows.shape[1]
    M = x_rows.shape[0] + (ctx_rows.shape[0] if two else 0)
    N = w_in_bf.shape[1]
    tm = 512 if (S % 512 == 0 and (M - B * S) % 512 == 0) else TM
    per_seq = S // tm
    n_first = B * per_seq
    if two:
        n_ctx_blocks = ctx_rows.shape[0] // tm
        x_specs = [pl.BlockSpec((tm, D), lambda i: (jnp.minimum(i, n_first - 1), 0)),
                   pl.BlockSpec((tm, D), lambda i: (jnp.maximum(i - n_first, 0), 0),
                                pipeline_mode=pl.Buffered(1 if n_ctx_blocks == 1 else 2))]
        x_args = [x_rows, ctx_rows]
    else:
        x_specs = [pl.BlockSpec((tm, D), lambda i: (i, 0))]
        x_args = [x_rows]
    return pl.pallas_call(
        functools.partial(_inproj_kernel, n_first=n_first, two_sources=two),
        grid=(M // tm,),
        in_specs=[
            *x_specs,
            pl.BlockSpec((1, 6, D), lambda i: (jnp.minimum(i // per_seq, B), 0, 0)),
            pl.BlockSpec((1, D), lambda i: (0, 0)),
            pl.BlockSpec((D, N), lambda i: (0, 0), pipeline_mode=pl.Buffered(1)),
        ],
        out_specs=pl.BlockSpec((tm, N), lambda i: (i, 0)),
        out_shape=jax.ShapeDtypeStruct((M, N), F32),
        compiler_params=_cparams(("parallel",), VMEM_LIMIT),
        name="inproj",
    )(*x_args, mod, g1.reshape(1, D), w_in_bf)


def _norm_rope(t, g, cos, sin):
    y = _rms(t) * g
    lane = lax.broadcasted_iota(jnp.int32, y.shape, 1)
    first_half = (lane % 64) < 32
    partner = jnp.where(first_half, pltpu.roll(y, 96, 1), pltpu.roll(y, 32, 1))
    return y * cos + partner * sin


def _qkprep_kernel(p_ref, cos_ref, sin_ref, qg_ref, kg_ref, q_ref, k_ref, v_ref):
    cos = cos_ref[...]
    sin = sin_ref[...]
    scale = HEAD_DIM ** -0.5 * LOG2E
    for h in range(ATT_Q_HEADS):
        t = p_ref[:, COL_QA + h * HEAD_DIM:COL_QA + (h + 1) * HEAD_DIM]
        q_ref[:, h * HEAD_DIM:(h + 1) * HEAD_DIM] = (_norm_rope(t, qg_ref[...], cos, sin) * scale).astype(BF16)
    for h in range(ATT_KV_HEADS):
        t = p_ref[:, COL_KA + h * HEAD_DIM:COL_KA + (h + 1) * HEAD_DIM]
        k_ref[0, :, h * HEAD_DIM:(h + 1) * HEAD_DIM] = _norm_rope(t, kg_ref[...], cos, sin).astype(BF16)
        v_ref[0, :, 2 * h * HEAD_DIM:(2 * h + 1) * HEAD_DIM] = (
            p_ref[:, COL_VA + h * HEAD_DIM:COL_VA + (h + 1) * HEAD_DIM].astype(BF16))
        v_ref[0, :, (2 * h + 1) * HEAD_DIM:(2 * h + 2) * HEAD_DIM] = jnp.ones((p_ref.shape[0], HEAD_DIM), BF16)


def _qkprep(proj, cos_t, sin_t, qg, kg, B, S):
    M = proj.shape[0]
    nbs = S // TM
    nlat = B * nbs

    def b_of(i):
        return jnp.where(i < nlat, i // nbs, i - nlat)

    def pos_of(i):
        return jnp.where(i < nlat, i % nbs, nbs)

    kl = S + TM
    return pl.pallas_call(
        _qkprep_kernel,
        grid=(M // TM,),
        in_specs=[
            pl.BlockSpec((TM, COL_UB), lambda i: (i, 0)),
            pl.BlockSpec((TM, HEAD_DIM), lambda i: (pos_of(i), 0)),
            pl.BlockSpec((TM, HEAD_DIM), lambda i: (pos_of(i), 0)),
            pl.BlockSpec((1, HEAD_DIM), lambda i: (0, 0)),
            pl.BlockSpec((1, HEAD_DIM), lambda i: (0, 0)),
        ],
        out_specs=[
            pl.BlockSpec((TM, ATT_Q_W), lambda i: (i, 0)),
            pl.BlockSpec((1, TM, ATT_KV_W), lambda i: (b_of(i), pos_of(i), 0)),
            pl.BlockSpec((1, TM, 2 * ATT_KV_W), lambda i: (b_of(i), pos_of(i), 0)),
        ],
        out_shape=[
            jax.ShapeDtypeStruct((M, ATT_Q_W), BF16),
            jax.ShapeDtypeStruct((B, kl, ATT_KV_W), BF16),
            jax.ShapeDtypeStruct((B, kl, 2 * ATT_KV_W), BF16),
        ],
        compiler_params=_cparams(("parallel",), VMEM_LIMIT),
        name="qkprep",
    )(proj, cos_t, sin_t, qg.reshape(1, HEAD_DIM), kg.reshape(1, HEAD_DIM))


def _gqa_kernel(q_ref, k_ref, v_ref, o_in_ref, o_ref, q2_ref, sa_ref, sb_ref, m_ref, acc_ref, *, tk):
    del o_in_ref
    tq = q_ref.shape[0]
    q2_ref[0:tq, :] = q_ref[:, :HEAD_DIM]
    q2_ref[tq:, :] = q_ref[:, HEAD_DIM:]
    nk = k_ref.shape[1] // tk
    m_ref[...] = jnp.full(m_ref.shape, -jnp.inf, F32)
    acc_ref[...] = jnp.zeros(acc_ref.shape, F32)

    def scores(c, s_ref):
        off = pl.multiple_of(c * tk, tk)
        s_ref[...] = lax.dot_general(q2_ref[...], k_ref[0, pl.ds(off, tk), :], (((1,), (1,)), ((), ())),
                                     preferred_element_type=F32)

    def update(c, s_ref):
        off = pl.multiple_of(c * tk, tk)
        vc = v_ref[0, pl.ds(off, tk), :]
        m = m_ref[...]
        m_new = jnp.maximum(m, jnp.max(s_ref[...], axis=-1, keepdims=True))
        m_ref[...] = m_new
        p = jnp.exp2((s_ref[...] - jnp.concatenate([m_new] * (tk // HEAD_DIM), axis=1)).astype(BF16))
        alpha = jnp.exp2(m - m_new)
        acc_ref[...] = (jnp.concatenate([alpha, alpha], axis=1) * acc_ref[...]
                        + jnp.dot(p, vc, preferred_element_type=F32))

    scores(0, sa_ref)
    n_pairs = (nk - 1) // 2

    def body(j, carry):
        scores(2 * j + 1, sb_ref)
        update(2 * j, sa_ref)
        scores(2 * j + 2, sa_ref)
        update(2 * j + 1, sb_ref)
        return carry

    lax.fori_loop(0, n_pairs, body, 0)
    if nk % 2 == 0:
        scores(nk - 1, sb_ref)
        update(nk - 2, sa_ref)
        update(nk - 1, sb_ref)
    else:
        update(nk - 1, sa_ref)
    acc = acc_ref[...]
    o = acc[:, :HEAD_DIM] / acc[:, HEAD_DIM:]
    o_ref[:, :HEAD_DIM] = o[:tq].astype(BF16)
    o_ref[:, HEAD_DIM:] = o[tq:].astype(BF16)


def _gqa_latent(q, k_all, v_all, B, S):
    M = q.shape[0]
    kl = k_all.shape[1]
    nbs = S // TM
    tk = 768 if kl % 768 == 0 else TM
    g = ATT_Q_HEADS // ATT_KV_HEADS
    return pl.pallas_call(
        functools.partial(_gqa_kernel, tk=tk),
        grid=(B, ATT_KV_HEADS, nbs),
        in_specs=[
            pl.BlockSpec((TM, g * HEAD_DIM), lambda b, h, i: (b * nbs + i, h)),
            pl.BlockSpec((1, kl, HEAD_DIM), lambda b, h, i: (b, 0, h)),
            pl.BlockSpec((1, kl, 2 * HEAD_DIM), lambda b, h, i: (b, 0, h)),
            pl.BlockSpec(memory_space=pl.ANY),
        ],
        out_specs=pl.BlockSpec((TM, g * HEAD_DIM), lambda b, h, i: (b * nbs + i, h)),
        out_shape=jax.ShapeDtypeStruct((M, ATT_Q_W), BF16),
        input_output_aliases={3: 0},
        scratch_shapes=[
            pltpu.VMEM((g * TM, HEAD_DIM), BF16),
            pltpu.VMEM((g * TM, tk), F32),
            pltpu.VMEM((g * TM, tk), F32),
            pltpu.VMEM((g * TM, HEAD_DIM), F32),
            pltpu.VMEM((g * TM, 2 * HEAD_DIM), F32),
        ],
        compiler_params=_cparams(("parallel", "parallel", "parallel"), VMEM_LIMIT),
        name="gqa_latent",
    )(q, k_all, v_all, jnp.zeros((M, ATT_Q_W), BF16))


def _ctx_attn_kernel(q_ref, k_ref, v_ref, o_in_ref, o_ref, *, scale):
    del o_in_ref
    if scale is None:
        q = q_ref[...]
    else:
        q = (q_ref[...] * (scale * LOG2E)).astype(BF16)
    k = k_ref[...].astype(BF16)
    v = v_ref[...].astype(BF16)
    s = lax.dot_general(q, k, (((1,), (1,)), ((), ())), preferred_element_type=F32)
    p = jnp.exp2(s - jnp.max(s, axis=-1, keepdims=True))
    l = jnp.sum(p, axis=-1, keepdims=True)
    o = jnp.dot(p.astype(BF16), v, preferred_element_type=F32) / l
    o_ref[...] = o.astype(BF16)


def _ctx_attn(q_arr, q_map, k_arr, k_spec, v_arr, v_spec, o_arr, B, n_heads, ctx_blk0, scale):
    return pl.pallas_call(
        functools.partial(_ctx_attn_kernel, scale=scale),
        grid=(B, n_heads),
        in_specs=[
            pl.BlockSpec((TM, HEAD_DIM), q_map),
            k_spec,
            v_spec,
            pl.BlockSpec(memory_space=pl.ANY),
        ],
        out_specs=pl.BlockSpec((TM, HEAD_DIM), lambda b, h: (ctx_blk0 + b, h)),
        out_shape=jax.ShapeDtypeStruct(o_arr.shape, o_arr.dtype),
        input_output_aliases={3: 0},
        compiler_params=_cparams(("parallel", "parallel"), VMEM_LIMIT),
        name="ctx_attn",
    )(q_arr, k_arr, v_arr, o_arr)


def _lru_kernel(u_ref, up_ref, un_ref, *rest, reverse, final):
    if final:
        (g_ref, hrev_ref, cw_ref, cb_ref, wr_ref, br_ref, wi_ref, bi_ref, sp_ref,
         o_ref, xpad, a_scr, b_scr, h_scr, hst) = rest
    else:
        (cw_ref, cb_ref, wr_ref, br_ref, wi_ref, bi_ref, sp_ref,
         o_ref, xpad, a_scr, b_scr, h_scr, hst) = rest
    s = pl.program_id(1)
    ns = pl.num_programs(1)
    T = u_ref.shape[0]
    W = u_ref.shape[1]

    @pl.when(s == 0)
    def _():
        hst[...] = jnp.zeros_like(hst)

    if reverse:
        j = ns - 1 - s
    else:
        j = s - 1
    is_first = jnp.logical_or(s == 0, j == 0)
    is_last = jnp.logical_or(s == 0, j == ns - 2)
    prev_rows = jnp.where(is_first, 0.0, up_ref[...])
    next_rows = jnp.where(is_last, 0.0, un_ref[...])
    xpad[0:8, :] = prev_rows
    xpad[8:8 + T, :] = u_ref[...]
    xpad[8 + T:16 + T, :] = next_rows
    u = (cb_ref[...]
         + xpad[7:7 + T, :] * cw_ref[0:1, :]
         + xpad[8:8 + T, :] * cw_ref[1:2, :]
         + xpad[9:9 + T, :] * cw_ref[2:3, :]
         + xpad[10:10 + T, :] * cw_ref[3:4, :])

    ub = u.astype(BF16)
    nb = LRU_WIDTH // LRU_BLOCKS
    for n in range(LRU_BLOCKS):
        sl = slice(n * nb, (n + 1) * nb)
        un_ = ub[:, sl]
        r = jax.nn.sigmoid(jnp.dot(un_, wr_ref[n], preferred_element_type=F32) + br_ref[:, sl])
        ig = jax.nn.sigmoid(jnp.dot(un_, wi_ref[n], preferred_element_type=F32) + bi_ref[:, sl])
        log_a = (-LRU_C) * r * sp_ref[:, sl]
        a = jnp.exp(log_a)
        a_scr[:, sl] = a
        one_minus_a2 = -jnp.tanh(log_a) * (a * a + 1.0)
        b_scr[:, sl] = jnp.sqrt(one_minus_a2) * (ig * u[:, sl])

    def step(t, h):
        row = (T - 1 - t) if reverse else t
        h = a_scr[pl.ds(row, 1), :] * h + b_scr[pl.ds(row, 1), :]
        h_scr[pl.ds(row, 1), :] = h
        return h

    hst[...] = lax.fori_loop(0, T, step, hst[...], unroll=8)

    if final:
        y = h_scr[...] + hrev_ref[...]
        o_ref[...] = (y * jax.nn.gelu(g_ref[...])).astype(o_ref.dtype)
    else:
        o_ref[...] = h_scr[...]
    del W


def _lru_dir(proj, hrev, conv_w, conv_b, wr, br, wi, bi, sp, B, S, reverse):
    M = proj.shape[0]
    nbs = S // TM
    nlat = B * nbs
    final = hrev is not None
    ub_blk = COL_UB // LRU_WIDTH
    gb_blk = COL_GB // LRU_WIDTH
    r8 = TM // 8
    n8 = M // 8

    def blk(b, s):
        j = (nbs - s) if reverse else (s - 1)
        return jnp.where(s == 0, nlat + b, b * nbs + j)

    in_specs = [
        pl.BlockSpec((TM, LRU_WIDTH), lambda b, s: (blk(b, s), ub_blk)),
        pl.BlockSpec((8, LRU_WIDTH), lambda b, s: (jnp.maximum(blk(b, s) * r8 - 1, 0), ub_blk)),
        pl.BlockSpec((8, LRU_WIDTH), lambda b, s: (jnp.minimum((blk(b, s) + 1) * r8, n8 - 1), ub_blk)),
    ]
    args = [proj, proj, proj]
    if final:
        in_specs += [
            pl.BlockSpec((TM, LRU_WIDTH), lambda b, s: (blk(b, s), gb_blk)),
            pl.BlockSpec((TM, LRU_WIDTH), lambda b, s: (blk(b, s), 0)),
        ]
        args += [proj, hrev]
    const2 = lambda b, s: (0, 0)
    in_specs += [
        pl.BlockSpec((CONV_W, LRU_WIDTH), const2),
        pl.BlockSpec((1, LRU_WIDTH), const2),
        pl.BlockSpec((LRU_BLOCKS, LRU_WIDTH // LRU_BLOCKS, LRU_WIDTH // LRU_BLOCKS), lambda b, s: (0, 0, 0)),
        pl.BlockSpec((1, LRU_WIDTH), const2),
        pl.BlockSpec((LRU_BLOCKS, LRU_WIDTH // LRU_BLOCKS, LRU_WIDTH // LRU_BLOCKS), lambda b, s: (0, 0, 0)),
        pl.BlockSpec((1, LRU_WIDTH), const2),
        pl.BlockSpec((1, LRU_WIDTH), const2),
    ]
    args += [conv_w, conv_b.reshape(1, -1), wr.astype(BF16), br.reshape(1, -1), wi.astype(BF16),
             bi.reshape(1, -1), sp.reshape(1, -1)]
    return pl.pallas_call(
        functools.partial(_lru_kernel, reverse=reverse, final=final),
        grid=(B, nbs + 1),
        in_specs=in_specs,
        out_specs=pl.BlockSpec((TM, LRU_WIDTH), lambda b, s: (blk(b, s), 0)),
        out_shape=jax.ShapeDtypeStruct((M, LRU_WIDTH), BF16 if final else F32),
        scratch_shapes=[
            pltpu.VMEM((TM + 16, LRU_WIDTH), F32),
            pltpu.VMEM((TM, LRU_WIDTH), F32),
            pltpu.VMEM((TM, LRU_WIDTH), F32),
            pltpu.VMEM((TM, LRU_WIDTH), F32),
            pltpu.VMEM((1, LRU_WIDTH), F32),
        ],
        compiler_params=_cparams(("parallel", "arbitrary"), VMEM_LIMIT),
        name="lru_fwd" if final else "lru_rev",
    )(*args)


def _na_kernel(q_ref, k0_ref, k1_ref, k2_ref, v0_ref, v1_ref, v2_ref, kc_ref, vc_ref, tl_ref, tr_ref,
               o_in_ref, o_ref, s_scr, *, n_rows):
    del o_in_ref
    i = pl.program_id(1)
    nbs = n_rows // NA_QROWS
    r0 = NA_QROWS * i
    ws = NA_QROWS * jnp.clip(i - 1, 0, nbs - 3)
    scale = HEAD_DIM ** -0.5
    nloc = NA_WROWS * GRID_W
    for h in range(NA_HEADS):
        hs = slice(h * HEAD_DIM, (h + 1) * HEAD_DIM)
        qh = (q_ref[:, hs] * scale).astype(BF16)
        kh = jnp.concatenate([k0_ref[:, hs], k1_ref[:, hs], k2_ref[:, hs], kc_ref[:, hs]], axis=0).astype(BF16)
        vh = jnp.concatenate([v0_ref[:, hs], v1_ref[:, hs], v2_ref[:, hs], vc_ref[:, hs]], axis=0).astype(BF16)
        s_scr[...] = lax.dot_general(qh, kh, (((1,), (1,)), ((), ())), preferred_element_type=F32)
        for qr in range(NA_QROWS):
            r = r0 + qr
            rs = jnp.clip(r - NA_ROWS // 2, 0, n_rows - NA_ROWS)

            def tile_of(a):
                kr = ws + a
                valid = jnp.logical_and(kr >= rs, kr < rs + NA_ROWS)
                return jnp.where(valid, kr - r + (NA_ROWS - 1), 2 * NA_ROWS - 1)

            for pr in range(NA_WROWS // 2):
                bias = tl_ref[h, tile_of(2 * pr)] + tr_ref[h, tile_of(2 * pr + 1)]
                rsl = slice(qr * GRID_W, (qr + 1) * GRID_W)
                csl = slice(pr * 2 * GRID_W, (pr + 1) * 2 * GRID_W)
                s_scr[rsl, csl] = s_scr[rsl, csl] + bias
        s = s_scr[...]
        p = jnp.exp(s - jnp.max(s, axis=-1, keepdims=True))
        l = jnp.sum(p, axis=-1, keepdims=True)
        o = jnp.dot(p.astype(BF16), vh, preferred_element_type=F32) / l
        o_ref[:, hs] = o.astype(BF16)
    del nloc


def _na_latent(proj, tl, tr, B, S):
    M = proj.shape[0]
    nbs = S // TM
    nlat = B * nbs
    n_rows = S // GRID_W
    qb, kb, vb = COL_QN // NA_W, COL_KN // NA_W, COL_VN // NA_W

    def wblk(b, i, t):
        return b * nbs + jnp.clip(i - 1, 0, nbs - 3) + t

    blk = (TM, NA_W)
    in_specs = [pl.BlockSpec(blk, lambda b, i: (b * nbs + i, qb))]
    in_specs += [pl.BlockSpec(blk, functools.partial(lambda b, i, t: (wblk(b, i, t), kb), t=t)) for t in range(3)]
    in_specs += [pl.BlockSpec(blk, functools.partial(lambda b, i, t: (wblk(b, i, t), vb), t=t)) for t in range(3)]
    in_specs += [
        pl.BlockSpec(blk, lambda b, i: (nlat + b, kb)),
        pl.BlockSpec(blk, lambda b, i: (nlat + b, vb)),
        pl.BlockSpec(tl.shape, lambda b, i: (0, 0, 0, 0)),
        pl.BlockSpec(tr.shape, lambda b, i: (0, 0, 0, 0)),
        pl.BlockSpec(memory_space=pl.ANY),
    ]
    return pl.pallas_call(
        functools.partial(_na_kernel, n_rows=n_rows),
        grid=(B, nbs),
        in_specs=in_specs,
        out_specs=pl.BlockSpec(blk, lambda b, i: (b * nbs + i, 0)),
        out_shape=jax.ShapeDtypeStruct((M, NA_W), BF16),
        input_output_aliases={11: 0},
        scratch_shapes=[pltpu.VMEM((TM, NA_WROWS * GRID_W + TM), F32)],
        compiler_params=_cparams(("parallel", "parallel"), VMEM_LIMIT),
        name="na_latent",
    )(proj, proj, proj, proj, proj, proj, proj, proj, proj, tl, tr, jnp.zeros((M, NA_W), BF16))


def _na_bias_tables(rpb):
    j = np.arange(GRID_W)
    cs = np.clip(j - NA_COLS // 2, 0, GRID_W - NA_COLS)
    kc = np.arange(GRID_W)
    inside = (kc[None, :] >= cs[:, None]) & (kc[None, :] < cs[:, None] + NA_COLS)
    dc = kc[None, :] - j[:, None] + (NA_COLS - 1)
    sel = (dc[:, :, None] == np.arange(2 * NA_COLS - 1)[None, None, :]) & inside[:, :, None]
    t = jnp.einsum("hrd,jkd->hrjk", rpb, jnp.asarray(sel, F32), precision=lax.Precision.HIGHEST)
    t = jnp.where(inside[None, None], t, NEG_BIAS)
    t = jnp.concatenate([t, jnp.full_like(t[:, :1], NEG_BIAS)], axis=1)
    z = jnp.zeros_like(t)
    return jnp.concatenate([t, z], axis=-1), jnp.concatenate([z, t], axis=-1)


def _outproj_kernel(*refs, n_first, two_sources):
    if two_sources:
        (oa_ref, ob_ref, oc_ref, x_ref, xb_ref, mod_ref, g_ref, w_ref, wr_ref, br_ref,
         xo_ref, h2_ref, rt_ref, cnt_ref, carry) = refs
    else:
        (oa_ref, ob_ref, oc_ref, x_ref, mod_ref, g_ref, w_ref, wr_ref, br_ref,
         xo_ref, h2_ref, rt_ref, cnt_ref, carry) = refs
    i = pl.program_id(0)

    @pl.when(i == 0)
    def _():
        carry[...] = jnp.zeros_like(carry)

    k1 = ATT_Q_W
    k2 = k1 + LRU_WIDTH
    acc = jnp.dot(oa_ref[...], w_ref[0:k1, :], preferred_element_type=F32)
    acc = acc + jnp.dot(ob_ref[...], w_ref[k1:k2, :], preferred_element_type=F32)
    acc = acc + jnp.dot(oc_ref[...], w_ref[k2:, :], preferred_element_type=F32)
    x_in = x_ref[...]
    if two_sources:
        x_in = jnp.where(i < n_first, x_in, xb_ref[...])
    xn = x_in + mod_ref[0, 2:3, :] * acc
    xo_ref[...] = xn
    h2 = _rms(xn) * g_ref[...]
    h2 = h2 * (1.0 + mod_ref[0, 4:5, :]) + mod_ref[0, 3:4, :]
    h2_ref[...] = _pack_bf16_pair(h2)
    logits = jnp.dot(h2.astype(BF16), wr_ref[...], preferred_element_type=F32) + br_ref[...]
    route = _route_math(logits)
    rt_ref[...] = _rank_math(route, carry)
    cnt_ref[...] = jnp.broadcast_to(carry[...], cnt_ref.shape)


def _pack_bf16_pair(x):
    w = x.shape[1] // 2
    lo = pltpu.bitcast(x[:, :w].astype(BF16).astype(F32), jnp.uint32)
    hi = pltpu.bitcast(x[:, w:].astype(BF16).astype(F32), jnp.uint32)
    return hi | (lo >> 16)


def _unpack_bf16_pair(words):
    lo = pltpu.bitcast(words << 16, F32)
    hi = pltpu.bitcast(words & jnp.uint32(0xFFFF0000), F32)
    return jnp.concatenate([lo, hi], axis=1).astype(BF16)


def _outproj(oa, ob, oc, x_rows, ctx_rows, mod, g2, w_out_bf, w_router, b_router, seg_of, nblk):
    M = oa.shape[0]
    D = x_rows.shape[1]
    two = ctx_rows is not None
    n_first = x_rows.shape[0] // TM
    packed = jax.eval_shape(_pack_bf16_pair, jax.ShapeDtypeStruct((TM, D), F32))
    if two:
        x_specs = [pl.BlockSpec((TM, D), lambda i: (jnp.minimum(i, n_first - 1), 0)),
                   pl.BlockSpec((TM, D), lambda i: (jnp.maximum(i - n_first, 0), 0))]
        x_args = [x_rows, ctx_rows]
        alias = {}
    else:
        x_specs = [pl.BlockSpec((TM, D), lambda i: (i, 0))]
        x_args = [x_rows]
        alias = {3: 0}
    return pl.pallas_call(
        functools.partial(_outproj_kernel, n_first=n_first, two_sources=two),
        grid=(nblk,),
        in_specs=[
            pl.BlockSpec((TM, ATT_Q_W), lambda i: (i, 0)),
            pl.BlockSpec((TM, LRU_WIDTH), lambda i: (i, 0)),
            pl.BlockSpec((TM, NA_W), lambda i: (i, 0)),
            *x_specs,
            pl.BlockSpec((1, 6, D), lambda i: (seg_of(i), 0, 0)),
            pl.BlockSpec((1, D), lambda i: (0, 0)),
            pl.BlockSpec(w_out_bf.shape, lambda i: (0, 0)),
            pl.BlockSpec(w_router.shape, lambda i: (0, 0)),
            pl.BlockSpec((1, 128), lambda i: (0, 0)),
        ],
        out_specs=[
            pl.BlockSpec((TM, D), lambda i: (i, 0)),
            pl.BlockSpec(packed.shape, lambda i: (i, 0)),
            pl.BlockSpec((TM, 128), lambda i: (i, 0)),
            pl.BlockSpec((8, 128), lambda i: (0, 0)),
        ],
        out_shape=[
            jax.ShapeDtypeStruct((M, D), F32),
            jax.ShapeDtypeStruct((nblk * TM, packed.shape[1]), packed.dtype),
            jax.ShapeDtypeStruct((nblk * TM, 128), F32),
            jax.ShapeDtypeStruct((8, 128), F32),
        ],
        input_output_aliases=alias,
        scratch_shapes=[pltpu.VMEM((1, 128), F32)],
        compiler_params=_cparams(("arbitrary",), VMEM_LIMIT),
        name="outproj",
    )(oa, ob, oc, *x_args, mod, g2.reshape(1, D), w_out_bf, w_router, b_router)


def _route_math(x):
    lane = lax.broadcasted_iota(jnp.int32, x.shape, 1)
    big = jnp.int32(1 << 20)
    ninf = -jnp.inf

    def first_argmax(vals):
        m = jnp.max(vals, axis=-1, keepdims=True)
        idx = jnp.min(jnp.where(vals == m, lane, big), axis=-1, keepdims=True)
        return m, idx

    lg = jnp.where(lane < MOE_GROUPS, x, ninf)
    mg, g_star = first_argmax(lg)
    g_gate = 1.0 / jnp.sum(jnp.exp(lg - mg), axis=-1, keepdims=True)
    lo = MOE_GROUPS + MOE_EXPERTS_PER_GROUP * g_star
    le = jnp.where(jnp.logical_and(lane >= lo, lane < lo + MOE_EXPERTS_PER_GROUP), x, ninf)
    v1, i1 = first_argmax(le)
    v2, i2 = first_argmax(jnp.where(lane == i1, ninf, le))
    e21 = jnp.exp(v2 - v1)
    w1 = g_gate / (1.0 + e21)
    w2 = g_gate * e21 / (1.0 + e21)
    e1 = (i1 - MOE_GROUPS).astype(F32)
    e2 = (i2 - MOE_GROUPS).astype(F32)
    return jnp.where(lane == 0, e1, jnp.where(lane == 1, e2, jnp.where(lane == 2, w1,
                     jnp.where(lane == 3, w2, 0.0))))


def _rank_math(r, carry):
    lane = lax.broadcasted_iota(jnp.int32, r.shape, 1).astype(F32)
    oh0 = jnp.where(lane == r[:, 0:1], 1.0, 0.0)
    oh1 = jnp.where(lane == r[:, 1:2], 1.0, 0.0)
    oh = oh0 + oh1
    n = r.shape[0]
    row = lax.broadcasted_iota(jnp.int32, (n, n), 0)
    col = lax.broadcasted_iota(jnp.int32, (n, n), 1)
    tri = jnp.where(col < row, 1.0, 0.0).astype(BF16)
    tot = carry[...] + jnp.dot(tri, oh.astype(BF16), preferred_element_type=F32)
    rank0 = jnp.sum(oh0 * tot, axis=-1, keepdims=True)
    rank1 = jnp.sum(oh1 * tot, axis=-1, keepdims=True)
    lane_i = lax.broadcasted_iota(jnp.int32, r.shape, 1)
    carry[...] = carry[...] + jnp.sum(oh, axis=0, keepdims=True)
    return jnp.where(lane_i == 4, rank0, jnp.where(lane_i == 5, rank1, r))


def _dispatch_kernel(dest_ref, pad_ref, h_ref, xs_hbm, zeros, sem, zsem, *, n_tok):
    i = pl.program_id(0)

    @pl.when(i == 0)
    def _():
        zeros[...] = jnp.zeros_like(zeros)
        pieces = [1 << b for b in range(MOE_BM.bit_length() - 2, 2, -1)]

        def copies(e):
            start, n = pad_ref[0, e], pad_ref[1, e]
            head = jnp.minimum((-start) & 7, n)
            out = []
            for r in range(7):
                out.append((r < head, pltpu.make_async_copy(zeros.at[pl.ds(0, 1)],
                                                            xs_hbm.at[pl.ds(start + r, 1)], zsem)))
            rest = n - head
            for p in pieces:
                at = pl.multiple_of(start + head + (rest & ~(2 * p - 1)), 8)
                out.append(((rest & p) != 0, pltpu.make_async_copy(zeros.at[pl.ds(0, p)],
                                                                   xs_hbm.at[pl.ds(at, p)], zsem)))
            return out

        def fill(e, c):
            for on, cp in copies(e):
                @pl.when(on)
                def _():
                    cp.start()
            return c

        def drain(e, c):
            for on, cp in copies(e):
                @pl.when(on)
                def _():
                    cp.wait()
            return c

        lax.fori_loop(0, MOE_EXPERTS, fill, 0)
        lax.fori_loop(0, MOE_EXPERTS, drain, 0)

        half = zeros.shape[0]
        n_tail = (xs_hbm.shape[0] - pad_ref[2, 0]) // half

        def tail_copy(j):
            at = pl.multiple_of(pad_ref[2, 0] + j * half, half)
            return pltpu.make_async_copy(zeros, xs_hbm.at[pl.ds(at, half)], zsem)

        def tail_fill(j, c):
            tail_copy(j).start()
            return c

        def tail_drain(j, c):
            tail_copy(j).wait()
            return c

        lax.fori_loop(0, n_tail, tail_fill, 0)
        lax.fori_loop(0, n_tail, tail_drain, 0)

    def issue(t, c):
        tok = i * TM + t
        for k in range(2):
            pltpu.make_async_copy(h_ref.at[pl.ds(t, 1)], xs_hbm.at[pl.ds(dest_ref[k * n_tok + tok], 1)], sem).start()
        return c

    lax.fori_loop(0, TM, issue, 0, unroll=8)
    for k in range(2):
        pltpu.make_async_copy(h_ref, xs_hbm.at[pl.ds(0, TM)], sem).wait()


def _dispatch(dest_flat, pad_info, h2, n_rows, nblk):
    W = h2.shape[1]
    return pl.pallas_call(
        functools.partial(_dispatch_kernel, n_tok=nblk * TM),
        grid_spec=pltpu.PrefetchScalarGridSpec(
            num_scalar_prefetch=2,
            grid=(nblk,),
            in_specs=[pl.BlockSpec((TM, W), lambda i, d, p: (i, 0))],
            out_specs=pl.BlockSpec(memory_space=pl.ANY),
            scratch_shapes=[pltpu.VMEM((MOE_BM // 2, W), h2.dtype), pltpu.SemaphoreType.DMA(()),
                            pltpu.SemaphoreType.DMA(())],
        ),
        out_shape=jax.ShapeDtypeStruct((n_rows, W), h2.dtype),
        compiler_params=pltpu.CompilerParams(dimension_semantics=("arbitrary",), has_side_effects=True),
        name="dispatch",
    )(dest_flat, pad_info, h2)


PLAN_EXPERT, PLAN_FIRST, PLAN_SLOT, PLAN_NEXT = 0, 1, 2, 3


def _block_plan(block_e, n_used):
    nb = block_e.shape[0]
    idx = jnp.arange(nb, dtype=jnp.int32)
    is_first = (idx < n_used[0]) & ((idx == 0) | (block_e != jnp.roll(block_e, 1)))
    seg = jnp.cumsum(is_first.astype(jnp.int32)) - 1
    first_pos = jnp.where(is_first, idx, nb)
    at_or_after = jnp.flip(lax.cummin(jnp.flip(first_pos)))
    next_first = jnp.concatenate([at_or_after[1:], jnp.full((1,), nb, jnp.int32)])
    next_e = jnp.where(next_first < nb, block_e[jnp.minimum(next_first, nb - 1)], -1)
    return jnp.stack([block_e, is_first.astype(jnp.int32), seg % 2, next_e]).astype(jnp.int32)


def _expert_weights_step(i, plan_ref, w_hbms, wbuf, wbf, sem, layer):
    def copies(e, slot):
        return [pltpu.make_async_copy(w.at[layer, e], wbuf.at[slot, n], sem.at[slot, n])
                for n, w in enumerate(w_hbms)]

    @pl.when(plan_ref[PLAN_FIRST, i] == 1)
    def _():
        slot = plan_ref[PLAN_SLOT, i]

        @pl.when(i == 0)
        def _():
            for c in copies(plan_ref[PLAN_EXPERT, 0], 0):
                c.start()

        for c in copies(plan_ref[PLAN_EXPERT, i], slot):
            c.wait()
        nrow, ncol = wbuf.shape[2], wbuf.shape[3]
        rows = 256

        def cast_rows(r, c):
            off = pl.multiple_of(r * rows, rows)
            for n in range(len(w_hbms)):
                wbf[pl.ds(off, rows), n * ncol:(n + 1) * ncol] = wbuf[slot, n, pl.ds(off, rows), :].astype(BF16)
            return c

        lax.fori_loop(0, nrow // rows, cast_rows, 0)
        nxt = plan_ref[PLAN_NEXT, i]

        @pl.when(nxt >= 0)
        def _():
            for c in copies(nxt, 1 - slot):
                c.start()


def _moe1_kernel(plan_ref, nu_ref, x_ref, wg_hbm, wu_hbm, o_ref, wbuf, wcat, sem, *, layer):
    i = pl.program_id(0)

    @pl.when(i < nu_ref[0])
    def _():
        _expert_weights_step(i, plan_ref, (wg_hbm, wu_hbm), wbuf, wcat, sem, layer)
        h = jnp.dot(_unpack_bf16_pair(x_ref[...]), wcat[...], preferred_element_type=F32)
        g = h[:, :MOE_HIDDEN]
        u = h[:, MOE_HIDDEN:]
        o_ref[...] = (g * jax.nn.sigmoid(g) * u).astype(BF16)

    @pl.when(i >= nu_ref[0])
    def _():
        o_ref[...] = jnp.zeros_like(o_ref)


def _moe1(plan, n_used, xs, w_gate, w_up, layer):
    P, W = xs.shape
    D = w_gate.shape[2]
    nb = P // MOE_BM
    used_blk = lambda i, plan, nu: (jnp.minimum(i, nu[0] - 1), 0)
    return pl.pallas_call(
        functools.partial(_moe1_kernel, layer=layer),
        grid_spec=pltpu.PrefetchScalarGridSpec(
            num_scalar_prefetch=2,
            grid=(nb,),
            in_specs=[pl.BlockSpec((MOE_BM, W), used_blk), pl.BlockSpec(memory_space=pl.ANY),
                      pl.BlockSpec(memory_space=pl.ANY)],
            out_specs=pl.BlockSpec((MOE_BM, MOE_HIDDEN), lambda i, plan, nu: (i, 0)),
            scratch_shapes=[pltpu.VMEM((2, 2, D, MOE_HIDDEN), F32), pltpu.VMEM((D, 2 * MOE_HIDDEN), BF16),
                            pltpu.SemaphoreType.DMA((2, 2))],
        ),
        out_shape=jax.ShapeDtypeStruct((P, MOE_HIDDEN), BF16),
        compiler_params=_cparams(("arbitrary",), VMEM_LIMIT),
        name="moe_up",
    )(plan, n_used, xs, w_gate, w_up)


def _moe2_kernel(plan_ref, nu_ref, h_ref, wd_hbm, o_ref, wbuf, wbf, sem, *, layer):
    i = pl.program_id(0)

    @pl.when(i < nu_ref[0])
    def _():
        _expert_weights_step(i, plan_ref, (wd_hbm,), wbuf, wbf, sem, layer)
        o_ref[...] = jnp.dot(h_ref[...], wbf[...], preferred_element_type=F32)

    @pl.when(i >= nu_ref[0])
    def _():
        o_ref[...] = jnp.zeros_like(o_ref)


def _moe2(plan, n_used, hmid, w_down, layer):
    P = hmid.shape[0]
    D = w_down.shape[3]
    nb = P // MOE_BM
    return pl.pallas_call(
        functools.partial(_moe2_kernel, layer=layer),
        grid_spec=pltpu.PrefetchScalarGridSpec(
            num_scalar_prefetch=2,
            grid=(nb,),
            in_specs=[
                pl.BlockSpec((MOE_BM, MOE_HIDDEN), lambda i, plan, nu: (i, 0)),
                pl.BlockSpec(memory_space=pl.ANY),
            ],
            out_specs=pl.BlockSpec((MOE_BM, D), lambda i, plan, nu: (i, 0)),
            scratch_shapes=[pltpu.VMEM((2, 1, MOE_HIDDEN, D), F32), pltpu.VMEM((MOE_HIDDEN, D), BF16),
                            pltpu.SemaphoreType.DMA((2, 1))],
        ),
        out_shape=jax.ShapeDtypeStruct((P, D), F32),
        compiler_params=_cparams(("arbitrary",), VMEM_LIMIT),
        name="moe_down",
    )(plan, n_used, hmid, w_down)


def _combine_kernel(dest_ref, yb_hbm, x_ref, r_ref, mod_ref, fg_ref, o_ref, ybuf, sem, *, final):
    i = pl.program_id(0)
    n = pl.num_programs(0)

    n_tok = n * TM

    def row_copy(tok, k, t, slot):
        return pltpu.make_async_copy(yb_hbm.at[pl.ds(dest_ref[k * n_tok + tok], 1)],
                                     ybuf.at[slot, k, pl.ds(t, 1)], sem.at[slot])

    def issue(blk, slot):
        def body(t, c):
            tok = blk * TM + t
            row_copy(tok, 0, t, slot).start()
            row_copy(tok, 1, t, slot).start()
            return c
        lax.fori_loop(0, TM, body, 0, unroll=8)

    @pl.when(i == 0)
    def _():
        issue(0, 0)

    @pl.when(i + 1 < n)
    def _():
        issue(i + 1, (i + 1) % 2)

    slot = i % 2

    for k in range(2):
        pltpu.make_async_copy(yb_hbm.at[pl.ds(0, TM)], ybuf.at[slot, k], sem.at[slot]).wait()
    r = r_ref[...]
    y = r[:, 2:3] * ybuf[slot, 0] + r[:, 3:4] * ybuf[slot, 1]
    xn = x_ref[...] + mod_ref[0, 5:6, :] * y
    if final:
        xn = _rms(xn) * fg_ref[...]
    o_ref[...] = xn


def _combine(dest_flat, yb, x_all, route, mod, final_g, seg_of, nblk, final):
    M, D = x_all.shape
    out_rows = nblk * TM if final else M
    kwargs = {} if final else {"input_output_aliases": {2: 0}}
    return pl.pallas_call(
        functools.partial(_combine_kernel, final=final),
        grid_spec=pltpu.PrefetchScalarGridSpec(
            num_scalar_prefetch=1,
            grid=(nblk,),
            in_specs=[
                pl.BlockSpec(memory_space=pl.ANY),
                pl.BlockSpec((TM, D), lambda i, d: (i, 0)),
                pl.BlockSpec((TM, 128), lambda i, d: (i, 0)),
                pl.BlockSpec((1, 6, D), lambda i, d: (seg_of(i), 0, 0)),
                pl.BlockSpec((1, D), lambda i, d: (0, 0)),
            ],
            out_specs=pl.BlockSpec((TM, D), lambda i, d: (i, 0)),
            scratch_shapes=[pltpu.VMEM((2, 2, TM, D), F32), pltpu.SemaphoreType.DMA((2,))],
        ),
        out_shape=jax.ShapeDtypeStruct((out_rows, D), F32),
        compiler_params=_cparams(("arbitrary",), VMEM_LIMIT),
        name="combine",
        **kwargs,
    )(dest_flat, yb, x_all, route, mod, final_g.reshape(1, D))


def _moe_layer(x_all, h2, route, cnt, mod, final_g, w_gate, w_up, w_down, layer, seg_of, nblk, final):
    T = nblk * TM
    e_ids = route[:, 0:2].T.astype(jnp.int32)
    ranks = route[:, 4:6].T.astype(jnp.int32)
    counts = cnt[0, :MOE_EXPERTS].astype(jnp.int32)
    padded = (counts + MOE_BM - 1) // MOE_BM * MOE_BM
    pad_ends = jnp.cumsum(padded)
    pad_starts = pad_ends - padded
    dest = (pad_starts[e_ids] + ranks).reshape(-1)
    pad_info = jnp.stack([pad_starts + counts, padded - counts,
                          jnp.broadcast_to(pad_ends[-1], (MOE_EXPERTS,))]).astype(jnp.int32)
    n_blocks = -(-(2 * T + MOE_EXPERTS * (MOE_BM - 1)) // MOE_BM)
    block_starts = jnp.arange(n_blocks, dtype=jnp.int32) * MOE_BM
    block_e = jnp.sum((block_starts[:, None] >= pad_ends[None, :]).astype(jnp.int32), axis=1)
    block_e = jnp.minimum(block_e, MOE_EXPERTS - 1)
    n_used = (pad_ends[-1:] // MOE_BM).astype(jnp.int32)
    plan = _block_plan(block_e, n_used)
    xs = _dispatch(dest, pad_info, h2, n_blocks * MOE_BM, nblk)
    hmid = _moe1(plan, n_used, xs, w_gate, w_up, layer)
    yb = _moe2(plan, n_used, hmid, w_down, layer)
    return _combine(dest, yb, x_all, route, mod, final_g, seg_of, nblk, final)


def _rope_tables(S):
    pos = jnp.arange(S, dtype=jnp.int32)
    rows = (pos // GRID_W).astype(F32)
    cols = (pos % GRID_W).astype(F32)
    n_freq = HEAD_DIM // 4
    inv = 1.0 / (ROPE_THETA ** (jnp.arange(n_freq, dtype=F32) / n_freq))
    cr, sr = jnp.cos(rows[:, None] * inv), jnp.sin(rows[:, None] * inv)
    cc, sc = jnp.cos(cols[:, None] * inv), jnp.sin(cols[:, None] * inv)
    cos_t = jnp.concatenate([cr, cr, cc, cc], axis=1)
    sin_t = jnp.concatenate([-sr, sr, -sc, sc], axis=1)
    cos_t = jnp.concatenate([cos_t, jnp.ones((TM, HEAD_DIM), F32)], axis=0)
    sin_t = jnp.concatenate([sin_t, jnp.zeros((TM, HEAD_DIM), F32)], axis=0)
    return cos_t, sin_t


def kernel(x, c, ctx, c_ctx, ada_w, ada_b, norm1_g, norm2_g, w_in, w_out, att_q_norm, att_k_norm, conv_w, conv_b, lru_wr, lru_br, lru_wi, lru_bi, lru_lambda, na_rpb, router_wg, router_bg, router_we, router_be, moe_w_gate, moe_w_up, moe_w_down, final_g):
    B, S, D = x.shape
    C = ctx.shape[1]
    L = ada_w.shape[0]
    assert C == TM and S % TM == 0 and S // TM >= 3 and B + 1 <= 8
    nbs = S // TM
    nlat = B * nbs
    m_lat = B * S
    M = m_lat + B * C

    def seg_of(i):
        return jnp.minimum(i // nbs, B)

    x_rows, ctx_rows = x.reshape(m_lat, D), ctx.reshape(B * C, D)
    cvec = jnp.zeros((8, D), F32).at[:B].set(c).at[B].set(c_ctx)
    mod_all = _adaln(cvec, ada_w, ada_b)[:, :B + 1].reshape(L, B + 1, 6, D)
    cos_t, sin_t = _rope_tables(S)
    g_att = ATT_Q_HEADS // ATT_KV_HEADS

    out = None
    for l in range(L):
        last = l == L - 1
        mod = mod_all[l]
        proj = _inproj(x_rows, ctx_rows, mod, norm1_g[l], w_in[l].astype(BF16), B, S)

        q, k_all, v_all = _qkprep(proj, cos_t, sin_t, att_q_norm[l], att_k_norm[l], B, S)
        oa = _gqa_latent(q, k_all, v_all, B, S)

        sp = jax.nn.softplus(-lru_lambda[l].astype(F32))
        hrev = _lru_dir(proj, None, conv_w[l], conv_b[l], lru_wr[l, 1], lru_br[l, 1], lru_wi[l, 1],
                        lru_bi[l, 1], sp[1], B, S, reverse=True)
        ob = _lru_dir(proj, hrev, conv_w[l], conv_b[l], lru_wr[l, 0], lru_br[l, 0], lru_wi[l, 0],
                      lru_bi[l, 0], sp[0], B, S, reverse=False)

        tl, tr = _na_bias_tables(na_rpb[l])
        oc = _na_latent(proj, tl, tr, B, S)

        if not last:
            oa = _ctx_attn(
                q, lambda b, h: (nlat + b, h),
                k_all, pl.BlockSpec((None, TM, HEAD_DIM), lambda b, h: (b, nbs, h // g_att)),
                v_all, pl.BlockSpec((None, TM, HEAD_DIM), lambda b, h: (b, nbs, 2 * (h // g_att))),
                oa, B, ATT_Q_HEADS, nlat, None)
            oc = _ctx_attn(
                proj, lambda b, h: (nlat + b, COL_QN // HEAD_DIM + h),
                proj, pl.BlockSpec((TM, HEAD_DIM), lambda b, h: (nlat + b, COL_KN // HEAD_DIM + h)),
                proj, pl.BlockSpec((TM, HEAD_DIM), lambda b, h: (nlat + b, COL_VN // HEAD_DIM + h)),
                oc, B, NA_HEADS, nlat, HEAD_DIM ** -0.5)

        nblk = nlat if last else M // TM
        w_router = jnp.zeros((D, 128), F32).at[:, :MOE_GROUPS].set(router_wg[l])
        w_router = w_router.at[:, MOE_GROUPS:MOE_GROUPS + MOE_EXPERTS].set(router_we[l])
        b_router = jnp.zeros((1, 128), F32).at[0, :MOE_GROUPS].set(router_bg[l])
        b_router = b_router.at[0, MOE_GROUPS:MOE_GROUPS + MOE_EXPERTS].set(router_be[l])
        x_all, h2, route, cnt = _outproj(oa, ob, oc, x_rows, ctx_rows, mod, norm2_g[l], w_out[l].astype(BF16),
                                         w_router.astype(BF16), b_router, seg_of, nblk)
        res = _moe_layer(x_all, h2, route, cnt, mod, final_g, moe_w_gate, moe_w_up, moe_w_down, l,
                         seg_of, nblk, last)
        if last:
            out = res
        else:
            x_rows, ctx_rows = res, None
    return out.reshape(B, S, D)
```

```python
import functools

import numpy as np
import jax
import jax.numpy as jnp
from jax import lax
from jax.experimental import pallas as pl
from jax.experimental.pallas import tpu as pltpu

F32 = jnp.float32
BF16 = jnp.bfloat16

GRID_W = 64
HEAD_DIM = 128
ATT_Q_HEADS = 4
ATT_KV_HEADS = 2
LRU_WIDTH = 1024
LRU_BLOCKS = 8
LRU_C = 8.0
CONV_W = 4
NA_HEADS = 4
NA_ROWS = 8
NA_COLS = 16
ATT_Q_W = ATT_Q_HEADS * HEAD_DIM
ATT_KV_W = ATT_KV_HEADS * HEAD_DIM
NA_W = NA_HEADS * HEAD_DIM
MOE_GROUPS = 4
MOE_EXPERTS_PER_GROUP = 8
MOE_EXPERTS = MOE_GROUPS * MOE_EXPERTS_PER_GROUP
MOE_HIDDEN = 1024
ROPE_THETA = 10000.0
EPS = 1e-6
LOG2E = 1.4426950408889634

COL_QA = 0
COL_KA = COL_QA + ATT_Q_W
COL_VA = COL_KA + ATT_KV_W
COL_UB = COL_VA + ATT_KV_W
COL_GB = COL_UB + LRU_WIDTH
COL_QN = COL_GB + LRU_WIDTH
COL_KN = COL_QN + NA_W
COL_VN = COL_KN + NA_W
IN_WIDTH = COL_VN + NA_W

TM = 256
MOE_BM = 256
NA_QROWS = 4
NA_WROWS = 12
NEG_BIAS = -1e30
VMEM_LIMIT = 56 * 1024 * 1024


def _cparams(sem, vmem=None):
    return pltpu.CompilerParams(dimension_semantics=sem, vmem_limit_bytes=vmem)


def _rms(x):
    return x * lax.rsqrt(jnp.mean(x * x, axis=-1, keepdims=True) + EPS)


def _adaln_kernel(c_ref, w_ref, b_ref, o_ref):
    cv = c_ref[...]
    s = cv * jax.nn.sigmoid(cv)
    o_ref[0] = jnp.dot(s.astype(BF16), w_ref[0].astype(BF16), preferred_element_type=F32) + b_ref[0]


def _adaln(cvec, ada_w, ada_b):
    L, D, N = ada_w.shape
    tn = 1024
    return pl.pallas_call(
        _adaln_kernel,
        grid=(L, N // tn),
        in_specs=[
            pl.BlockSpec((8, D), lambda l, j: (0, 0)),
            pl.BlockSpec((1, D, tn), lambda l, j: (l, 0, j)),
            pl.BlockSpec((1, 1, tn), lambda l, j: (l, 0, j)),
        ],
        out_specs=pl.BlockSpec((1, 8, tn), lambda l, j: (l, 0, j)),
        out_shape=jax.ShapeDtypeStruct((L, 8, N), F32),
        compiler_params=_cparams(("parallel", "parallel"), VMEM_LIMIT),
        name="adaln",
    )(cvec, ada_w, ada_b.reshape(L, 1, N))


def _inproj_kernel(*refs, n_first, two_sources):
    if two_sources:
        x_ref, xb_ref, mod_ref, g_ref, w_ref, o_ref = refs
        x = jnp.where(pl.program_id(0) < n_first, x_ref[...], xb_ref[...])
    else:
        x_ref, mod_ref, g_ref, w_ref, o_ref = refs
        x = x_ref[...]
    h = _rms(x) * g_ref[...]
    h = h * (1.0 + mod_ref[0, 1:2, :]) + mod_ref[0, 0:1, :]
    o_ref[...] = jnp.dot(h.astype(BF16), w_ref[...], preferred_element_type=F32)


def _inproj(x_rows, ctx_rows, mod, g1, w_in_bf, B, S):
    two = ctx_rows is not None
    D = x_rows.shape[1]
    M = x_rows.shape[0] + (ctx_rows.shape[0] if two else 0)
    N = w_in_bf.shape[1]
    tm = 512 if (S % 512 == 0 and (M - B * S) % 512 == 0) else TM
    per_seq = S // tm
    n_first = B * per_seq
    if two:
        n_ctx_blocks = ctx_rows.shape[0] // tm
        x_specs = [pl.BlockSpec((tm, D), lambda i: (jnp.minimum(i, n_first - 1), 0)),
                   pl.BlockSpec((tm, D), lambda i: (jnp.maximum(i - n_first, 0), 0),
                                pipeline_mode=pl.Buffered(1 if n_ctx_blocks == 1 else 2))]
        x_args = [x_rows, ctx_rows]
    else:
        x_specs = [pl.BlockSpec((tm, D), lambda i: (i, 0))]
        x_args = [x_rows]
    return pl.pallas_call(
        functools.partial(_inproj_kernel, n_first=n_first, two_sources=two),
        grid=(M // tm,),
        in_specs=[
            *x_specs,
            pl.BlockSpec((1, 6, D), lambda i: (jnp.minimum(i // per_seq, B), 0, 0)),
            pl.BlockSpec((1, D), lambda i: (0, 0)),
            pl.BlockSpec((D, N), lambda i: (0, 0), pipeline_mode=pl.Buffered(1)),
        ],
        out_specs=pl.BlockSpec((tm, N), lambda i: (i, 0)),
        out_shape=jax.ShapeDtypeStruct((M, N), F32),
        compiler_params=_cparams(("parallel",), VMEM_LIMIT),
        name="inproj",
    )(*x_args, mod, g1.reshape(1, D), w_in_bf)


def _norm_rope(t, g, cos, sin):
    y = _rms(t) * g
    lane = lax.broadcasted_iota(jnp.int32, y.shape, 1)
    first_half = (lane % 64) < 32
    partner = jnp.where(first_half, pltpu.roll(y, 96, 1), pltpu.roll(y, 32, 1))
    return y * cos + partner * sin


def _qkprep_kernel(p_ref, cos_ref, sin_ref, qg_ref, kg_ref, q_ref, k_ref, v_ref):
    cos = cos_ref[...]
    sin = sin_ref[...]
    scale = HEAD_DIM ** -0.5 * LOG2E
    for h in range(ATT_Q_HEADS):
        t = p_ref[:, COL_QA + h * HEAD_DIM:COL_QA + (h + 1) * HEAD_DIM]
        q_ref[:, h * HEAD_DIM:(h + 1) * HEAD_DIM] = (_norm_rope(t, qg_ref[...], cos, sin) * scale).astype(BF16)
    for h in range(ATT_KV_HEADS):
        t = p_ref[:, COL_KA + h * HEAD_DIM:COL_KA + (h + 1) * HEAD_DIM]
        k_ref[0, :, h * HEAD_DIM:(h + 1) * HEAD_DIM] = _norm_rope(t, kg_ref[...], cos, sin).astype(BF16)
        v_ref[0, :, 2 * h * HEAD_DIM:(2 * h + 1) * HEAD_DIM] = (
            p_ref[:, COL_VA + h * HEAD_DIM:COL_VA + (h + 1) * HEAD_DIM].astype(BF16))
        v_ref[0, :, (2 * h + 1) * HEAD_DIM:(2 * h + 2) * HEAD_DIM] = jnp.ones((p_ref.shape[0], HEAD_DIM), BF16)


def _qkprep(proj, cos_t, sin_t, qg, kg, B, S):
    M = proj.shape[0]
    nbs = S // TM
    nlat = B * nbs

    def b_of(i):
        return jnp.where(i < nlat, i // nbs, i - nlat)

    def pos_of(i):
        return jnp.where(i < nlat, i % nbs, nbs)

    kl = S + TM
    return pl.pallas_call(
        _qkprep_kernel,
        grid=(M // TM,),
        in_specs=[
            pl.BlockSpec((TM, COL_UB), lambda i: (i, 0)),
            pl.BlockSpec((TM, HEAD_DIM), lambda i: (pos_of(i), 0)),
            pl.BlockSpec((TM, HEAD_DIM), lambda i: (pos_of(i), 0)),
            pl.BlockSpec((1, HEAD_DIM), lambda i: (0, 0)),
            pl.BlockSpec((1, HEAD_DIM), lambda i: (0, 0)),
        ],
        out_specs=[
            pl.BlockSpec((TM, ATT_Q_W), lambda i: (i, 0)),
            pl.BlockSpec((1, TM, ATT_KV_W), lambda i: (b_of(i), pos_of(i), 0)),
            pl.BlockSpec((1, TM, 2 * ATT_KV_W), lambda i: (b_of(i), pos_of(i), 0)),
        ],
        out_shape=[
            jax.ShapeDtypeStruct((M, ATT_Q_W), BF16),
            jax.ShapeDtypeStruct((B, kl, ATT_KV_W), BF16),
            jax.ShapeDtypeStruct((B, kl, 2 * ATT_KV_W), BF16),
        ],
        compiler_params=_cparams(("parallel",), VMEM_LIMIT),
        name="qkprep",
    )(proj, cos_t, sin_t, qg.reshape(1, HEAD_DIM), kg.reshape(1, HEAD_DIM))


def _gqa_kernel(q_ref, k_ref, v_ref, o_in_ref, o_ref, q2_ref, sa_ref, sb_ref, m_ref, acc_ref, *, tk):
    del o_in_ref
    tq = q_ref.shape[0]
    q2_ref[0:tq, :] = q_ref[:, :HEAD_DIM]
    q2_ref[tq:, :] = q_ref[:, HEAD_DIM:]
    nk = k_ref.shape[1] // tk
    m_ref[...] = jnp.full(m_ref.shape, -jnp.inf, F32)
    acc_ref[...] = jnp.zeros(acc_ref.shape, F32)

    def scores(c, s_ref):
        off = pl.multiple_of(c * tk, tk)
        s_ref[...] = lax.dot_general(q2_ref[...], k_ref[0, pl.ds(off, tk), :], (((1,), (1,)), ((), ())),
                                     preferred_element_type=F32)

    def update(c, s_ref):
        off = pl.multiple_of(c * tk, tk)
        vc = v_ref[0, pl.ds(off, tk), :]
        for hh in range(2):
            rows = slice(hh * tq, (hh + 1) * tq)
            m = m_ref[rows, :]
            m_new = jnp.maximum(m, jnp.max(s_ref[rows, :], axis=-1, keepdims=True))
            m_ref[rows, :] = m_new
            p = jnp.exp2((s_ref[rows, :] - jnp.concatenate([m_new] * (tk // HEAD_DIM), axis=1)).astype(BF16))
            alpha = jnp.exp2(m - m_new)
            acc_ref[rows, :] = (jnp.concatenate([alpha, alpha], axis=1) * acc_ref[rows, :]
                                + jnp.dot(p, vc, preferred_element_type=F32))

    scores(0, sa_ref)
    n_pairs = (nk - 1) // 2

    def body(j, carry):
        scores(2 * j + 1, sb_ref)
        update(2 * j, sa_ref)
        scores(2 * j + 2, sa_ref)
        update(2 * j + 1, sb_ref)
        return carry

    lax.fori_loop(0, n_pairs, body, 0)
    if nk % 2 == 0:
        scores(nk - 1, sb_ref)
        update(nk - 2, sa_ref)
        update(nk - 1, sb_ref)
    else:
        update(nk - 1, sa_ref)
    acc = acc_ref[...]
    o = acc[:, :HEAD_DIM] / acc[:, HEAD_DIM:]
    o_ref[:, :HEAD_DIM] = o[:tq].astype(BF16)
    o_ref[:, HEAD_DIM:] = o[tq:].astype(BF16)


def _gqa_latent(q, k_all, v_all, B, S):
    M = q.shape[0]
    kl = k_all.shape[1]
    nbs = S // TM
    tk = 768 if kl % 768 == 0 else TM
    g = ATT_Q_HEADS // ATT_KV_HEADS
    return pl.pallas_call(
        functools.partial(_gqa_kernel, tk=tk),
        grid=(B, ATT_KV_HEADS, nbs),
        in_specs=[
            pl.BlockSpec((TM, g * HEAD_DIM), lambda b, h, i: (b * nbs + i, h)),
            pl.BlockSpec((1, kl, HEAD_DIM), lambda b, h, i: (b, 0, h)),
            pl.BlockSpec((1, kl, 2 * HEAD_DIM), lambda b, h, i: (b, 0, h)),
            pl.BlockSpec(memory_space=pl.ANY),
        ],
        out_specs=pl.BlockSpec((TM, g * HEAD_DIM), lambda b, h, i: (b * nbs + i, h)),
        out_shape=jax.ShapeDtypeStruct((M, ATT_Q_W), BF16),
        input_output_aliases={3: 0},
        scratch_shapes=[
            pltpu.VMEM((g * TM, HEAD_DIM), BF16),
            pltpu.VMEM((g * TM, tk), F32),
            pltpu.VMEM((g * TM, tk), F32),
            pltpu.VMEM((g * TM, HEAD_DIM), F32),
            pltpu.VMEM((g * TM, 2 * HEAD_DIM), F32),
        ],
        compiler_params=_cparams(("parallel", "parallel", "parallel"), VMEM_LIMIT),
        name="gqa_latent",
    )(q, k_all, v_all, jnp.zeros((M, ATT_Q_W), BF16))


def _ctx_attn_kernel(q_ref, k_ref, v_ref, o_in_ref, o_ref, *, scale):
    del o_in_ref
    if scale is None:
        q = q_ref[...]
    else:
        q = (q_ref[...] * (scale * LOG2E)).astype(BF16)
    k = k_ref[...].astype(BF16)
    s = lax.dot_general(q, k, (((1,), (1,)), ((), ())), preferred_element_type=F32)
    p = jnp.exp2(s - jnp.max(s, axis=-1, keepdims=True))
    l = jnp.sum(p, axis=-1, keepdims=True)
    pv = jnp.dot(p.astype(BF16), v_ref[...].astype(BF16), preferred_element_type=F32)
    o_ref[...] = (pv / l).astype(BF16)


def _ctx_attn(q_arr, q_map, k_arr, k_spec, v_arr, v_spec, o_arr, B, n_heads, ctx_blk0, scale):
    return pl.pallas_call(
        functools.partial(_ctx_attn_kernel, scale=scale),
        grid=(B, n_heads),
        in_specs=[
            pl.BlockSpec((TM, HEAD_DIM), q_map),
            k_spec,
            v_spec,
            pl.BlockSpec(memory_space=pl.ANY),
        ],
        out_specs=pl.BlockSpec((TM, HEAD_DIM), lambda b, h: (ctx_blk0 + b, h)),
        out_shape=jax.ShapeDtypeStruct(o_arr.shape, o_arr.dtype),
        input_output_aliases={3: 0},
        compiler_params=_cparams(("parallel", "parallel"), VMEM_LIMIT),
        name="ctx_attn",
    )(q_arr, k_arr, v_arr, o_arr)


def _lru_kernel(u_ref, up_ref, un_ref, *rest, reverse, final):
    if final:
        (g_ref, hrev_ref, cw_ref, cb_ref, wr_ref, br_ref, wi_ref, bi_ref, sp_ref,
         o_ref, xpad, a_scr, b_scr, h_scr, hst) = rest
    else:
        (cw_ref, cb_ref, wr_ref, br_ref, wi_ref, bi_ref, sp_ref,
         o_ref, xpad, a_scr, b_scr, h_scr, hst) = rest
    s = pl.program_id(1)
    ns = pl.num_programs(1)
    T = u_ref.shape[0]
    W = u_ref.shape[1]

    @pl.when(s == 0)
    def _():
        hst[...] = jnp.zeros_like(hst)

    if reverse:
        j = ns - 1 - s
    else:
        j = s - 1
    is_first = jnp.logical_or(s == 0, j == 0)
    is_last = jnp.logical_or(s == 0, j == ns - 2)
    prev_rows = jnp.where(is_first, 0.0, up_ref[...])
    next_rows = jnp.where(is_last, 0.0, un_ref[...])
    xpad[0:8, :] = prev_rows
    xpad[8:8 + T, :] = u_ref[...]
    xpad[8 + T:16 + T, :] = next_rows
    u = (cb_ref[...]
         + xpad[7:7 + T, :] * cw_ref[0:1, :]
         + xpad[8:8 + T, :] * cw_ref[1:2, :]
         + xpad[9:9 + T, :] * cw_ref[2:3, :]
         + xpad[10:10 + T, :] * cw_ref[3:4, :])

    ub = u.astype(BF16)
    nb = LRU_WIDTH // LRU_BLOCKS
    for n in range(LRU_BLOCKS):
        sl = slice(n * nb, (n + 1) * nb)
        un_ = ub[:, sl]
        r = jax.nn.sigmoid(jnp.dot(un_, wr_ref[n], preferred_element_type=F32) + br_ref[:, sl])
        ig = jax.nn.sigmoid(jnp.dot(un_, wi_ref[n], preferred_element_type=F32) + bi_ref[:, sl])
        log_a = (-LRU_C) * r * sp_ref[:, sl]
        a = jnp.exp(log_a)
        a_scr[:, sl] = a
        one_minus_a2 = -jnp.tanh(log_a) * (a * a + 1.0)
        b_scr[:, sl] = jnp.sqrt(one_minus_a2) * (ig * u[:, sl])

    def step(t, h):
        row = (T - 1 - t) if reverse else t
        h = a_scr[pl.ds(row, 1), :] * h + b_scr[pl.ds(row, 1), :]
        h_scr[pl.ds(row, 1), :] = h
        return h

    hst[...] = lax.fori_loop(0, T, step, hst[...], unroll=8)

    if final:
        y = h_scr[...] + hrev_ref[...]
        o_ref[...] = (y * jax.nn.gelu(g_ref[...])).astype(o_ref.dtype)
    else:
        o_ref[...] = h_scr[...]
    del W


def _lru_dir(proj, hrev, conv_w, conv_b, wr, br, wi, bi, sp, B, S, reverse):
    M = proj.shape[0]
    nbs = S // TM
    nlat = B * nbs
    final = hrev is not None
    ub_blk = COL_UB // LRU_WIDTH
    gb_blk = COL_GB // LRU_WIDTH
    r8 = TM // 8
    n8 = M // 8

    def blk(b, s):
        j = (nbs - s) if reverse else (s - 1)
        return jnp.where(s == 0, nlat + b, b * nbs + j)

    in_specs = [
        pl.BlockSpec((TM, LRU_WIDTH), lambda b, s: (blk(b, s), ub_blk)),
        pl.BlockSpec((8, LRU_WIDTH), lambda b, s: (jnp.maximum(blk(b, s) * r8 - 1, 0), ub_blk)),
        pl.BlockSpec((8, LRU_WIDTH), lambda b, s: (jnp.minimum((blk(b, s) + 1) * r8, n8 - 1), ub_blk)),
    ]
    args = [proj, proj, proj]
    if final:
        in_specs += [
            pl.BlockSpec((TM, LRU_WIDTH), lambda b, s: (blk(b, s), gb_blk)),
            pl.BlockSpec((TM, LRU_WIDTH), lambda b, s: (blk(b, s), 0)),
        ]
        args += [proj, hrev]
    const2 = lambda b, s: (0, 0)
    in_specs += [
        pl.BlockSpec((CONV_W, LRU_WIDTH), const2),
        pl.BlockSpec((1, LRU_WIDTH), const2),
        pl.BlockSpec((LRU_BLOCKS, LRU_WIDTH // LRU_BLOCKS, LRU_WIDTH // LRU_BLOCKS), lambda b, s: (0, 0, 0)),
        pl.BlockSpec((1, LRU_WIDTH), const2),
        pl.BlockSpec((LRU_BLOCKS, LRU_WIDTH // LRU_BLOCKS, LRU_WIDTH // LRU_BLOCKS), lambda b, s: (0, 0, 0)),
        pl.BlockSpec((1, LRU_WIDTH), const2),
        pl.BlockSpec((1, LRU_WIDTH), const2),
    ]
    args += [conv_w, conv_b.reshape(1, -1), wr.astype(BF16), br.reshape(1, -1), wi.astype(BF16),
             bi.reshape(1, -1), sp.reshape(1, -1)]
    return pl.pallas_call(
        functools.partial(_lru_kernel, reverse=reverse, final=final),
        grid=(B, nbs + 1),
        in_specs=in_specs,
        out_specs=pl.BlockSpec((TM, LRU_WIDTH), lambda b, s: (blk(b, s), 0)),
        out_shape=jax.ShapeDtypeStruct((M, LRU_WIDTH), BF16 if final else F32),
        scratch_shapes=[
            pltpu.VMEM((TM + 16, LRU_WIDTH), F32),
            pltpu.VMEM((TM, LRU_WIDTH), F32),
            pltpu.VMEM((TM, LRU_WIDTH), F32),
            pltpu.VMEM((TM, LRU_WIDTH), F32),
            pltpu.VMEM((1, LRU_WIDTH), F32),
        ],
        compiler_params=_cparams(("parallel", "arbitrary"), VMEM_LIMIT),
        name="lru_fwd" if final else "lru_rev",
    )(*args)


def _na_kernel(q_ref, k0_ref, k1_ref, k2_ref, v0_ref, v1_ref, v2_ref, kc_ref, vc_ref, tl_ref, tr_ref,
               o_in_ref, o_ref, s_scr, *, n_rows):
    del o_in_ref
    i = pl.program_id(1)
    nbs = n_rows // NA_QROWS
    r0 = NA_QROWS * i
    ws = NA_QROWS * jnp.clip(i - 1, 0, nbs - 3)
    scale = HEAD_DIM ** -0.5
    nloc = NA_WROWS * GRID_W
    for h in range(NA_HEADS):
        hs = slice(h * HEAD_DIM, (h + 1) * HEAD_DIM)
        qh = (q_ref[:, hs] * scale).astype(BF16)
        kh = jnp.concatenate([k0_ref[:, hs], k1_ref[:, hs], k2_ref[:, hs], kc_ref[:, hs]], axis=0).astype(BF16)
        vh = jnp.concatenate([v0_ref[:, hs], v1_ref[:, hs], v2_ref[:, hs], vc_ref[:, hs]], axis=0).astype(BF16)
        s_scr[...] = lax.dot_general(qh, kh, (((1,), (1,)), ((), ())), preferred_element_type=F32)
        for qr in range(NA_QROWS):
            r = r0 + qr
            rs = jnp.clip(r - NA_ROWS // 2, 0, n_rows - NA_ROWS)

            def tile_of(a):
                kr = ws + a
                valid = jnp.logical_and(kr >= rs, kr < rs + NA_ROWS)
                return jnp.where(valid, kr - r + (NA_ROWS - 1), 2 * NA_ROWS - 1)

            for pr in range(NA_WROWS // 2):
                bias = tl_ref[h, tile_of(2 * pr)] + tr_ref[h, tile_of(2 * pr + 1)]
                rsl = slice(qr * GRID_W, (qr + 1) * GRID_W)
                csl = slice(pr * 2 * GRID_W, (pr + 1) * 2 * GRID_W)
                s_scr[rsl, csl] = s_scr[rsl, csl] + bias
        s = s_scr[...]
        p = jnp.exp(s - jnp.max(s, axis=-1, keepdims=True))
        l = jnp.sum(p, axis=-1, keepdims=True)
        o = jnp.dot(p.astype(BF16), vh, preferred_element_type=F32) / l
        o_ref[:, hs] = o.astype(BF16)
    del nloc


def _na_latent(proj, tl, tr, B, S):
    M = proj.shape[0]
    nbs = S // TM
    nlat = B * nbs
    n_rows = S // GRID_W
    qb, kb, vb = COL_QN // NA_W, COL_KN // NA_W, COL_VN // NA_W

    def wblk(b, i, t):
        return b * nbs + jnp.clip(i - 1, 0, nbs - 3) + t

    blk = (TM, NA_W)
    in_specs = [pl.BlockSpec(blk, lambda b, i: (b * nbs + i, qb))]
    in_specs += [pl.BlockSpec(blk, functools.partial(lambda b, i, t: (wblk(b, i, t), kb), t=t)) for t in range(3)]
    in_specs += [pl.BlockSpec(blk, functools.partial(lambda b, i, t: (wblk(b, i, t), vb), t=t)) for t in range(3)]
    in_specs += [
        pl.BlockSpec(blk, lambda b, i: (nlat + b, kb)),
        pl.BlockSpec(blk, lambda b, i: (nlat + b, vb)),
        pl.BlockSpec(tl.shape, lambda b, i: (0, 0, 0, 0)),
        pl.BlockSpec(tr.shape, lambda b, i: (0, 0, 0, 0)),
        pl.BlockSpec(memory_space=pl.ANY),
    ]
    return pl.pallas_call(
        functools.partial(_na_kernel, n_rows=n_rows),
        grid=(B, nbs),
        in_specs=in_specs,
        out_specs=pl.BlockSpec(blk, lambda b, i: (b * nbs + i, 0)),
        out_shape=jax.ShapeDtypeStruct((M, NA_W), BF16),
        input_output_aliases={11: 0},
        scratch_shapes=[pltpu.VMEM((TM, NA_WROWS * GRID_W + TM), F32)],
        compiler_params=_cparams(("parallel", "parallel"), VMEM_LIMIT),
        name="na_latent",
    )(proj, proj, proj, proj, proj, proj, proj, proj, proj, tl, tr, jnp.zeros((M, NA_W), BF16))


def _na_bias_tables(rpb):
    j = np.arange(GRID_W)
    cs = np.clip(j - NA_COLS // 2, 0, GRID_W - NA_COLS)
    kc = np.arange(GRID_W)
    inside = (kc[None, :] >= cs[:, None]) & (kc[None, :] < cs[:, None] + NA_COLS)
    dc = kc[None, :] - j[:, None] + (NA_COLS - 1)
    sel = (dc[:, :, None] == np.arange(2 * NA_COLS - 1)[None, None, :]) & inside[:, :, None]
    t = jnp.einsum("hrd,jkd->hrjk", rpb, jnp.asarray(sel, F32), precision=lax.Precision.HIGHEST)
    t = jnp.where(inside[None, None], t, NEG_BIAS)
    t = jnp.concatenate([t, jnp.full_like(t[:, :1], NEG_BIAS)], axis=1)
    z = jnp.zeros_like(t)
    return jnp.concatenate([t, z], axis=-1), jnp.concatenate([z, t], axis=-1)


def _outproj_kernel(*refs, n_first, two_sources):
    if two_sources:
        (oa_ref, ob_ref, oc_ref, x_ref, xb_ref, mod_ref, g_ref, w_ref, wr_ref, br_ref,
         xo_ref, h2_ref, rt_ref, cnt_ref, carry) = refs
    else:
        (oa_ref, ob_ref, oc_ref, x_ref, mod_ref, g_ref, w_ref, wr_ref, br_ref,
         xo_ref, h2_ref, rt_ref, cnt_ref, carry) = refs
    i = pl.program_id(0)

    @pl.when(i == 0)
    def _():
        carry[...] = jnp.zeros_like(carry)

    k1 = ATT_Q_W
    k2 = k1 + LRU_WIDTH
    acc = jnp.dot(oa_ref[...], w_ref[0:k1, :], preferred_element_type=F32)
    acc = acc + jnp.dot(ob_ref[...], w_ref[k1:k2, :], preferred_element_type=F32)
    acc = acc + jnp.dot(oc_ref[...], w_ref[k2:, :], preferred_element_type=F32)
    x_in = x_ref[...]
    if two_sources:
        x_in = jnp.where(i < n_first, x_in, xb_ref[...])
    xn = x_in + mod_ref[0, 2:3, :] * acc
    xo_ref[...] = xn
    h2 = _rms(xn) * g_ref[...]
    h2 = h2 * (1.0 + mod_ref[0, 4:5, :]) + mod_ref[0, 3:4, :]
    h2_ref[...] = _pack_bf16_pair(h2)
    logits = jnp.dot(h2.astype(BF16), wr_ref[...], preferred_element_type=F32) + br_ref[...]
    route = _route_math(logits)
    rt_ref[...] = _rank_math(route, carry)
    cnt_ref[...] = jnp.broadcast_to(carry[...], cnt_ref.shape)


def _pack_bf16_pair(x):
    w = x.shape[1] // 2
    lo = pltpu.bitcast(x[:, :w].astype(BF16).astype(F32), jnp.uint32)
    hi = pltpu.bitcast(x[:, w:].astype(BF16).astype(F32), jnp.uint32)
    return hi | (lo >> 16)


def _unpack_bf16_pair(words):
    lo = pltpu.bitcast(words << 16, F32)
    hi = pltpu.bitcast(words & jnp.uint32(0xFFFF0000), F32)
    return jnp.concatenate([lo, hi], axis=1).astype(BF16)


def _outproj(oa, ob, oc, x_rows, ctx_rows, mod, g2, w_out_bf, w_router, b_router, B, S, n_rows):
    M = oa.shape[0]
    D = x_rows.shape[1]
    two = ctx_rows is not None
    tm = 512 if (S % 512 == 0 and (M - B * S) % 512 == 0) else TM
    per_seq = S // tm
    n_first = B * per_seq
    packed = jax.eval_shape(_pack_bf16_pair, jax.ShapeDtypeStruct((tm, D), F32))
    if two:
        x_specs = [pl.BlockSpec((tm, D), lambda i: (jnp.minimum(i, n_first - 1), 0)),
                   pl.BlockSpec((tm, D), lambda i: (jnp.maximum(i - n_first, 0), 0))]
        x_args = [x_rows, ctx_rows]
        alias = {}
    else:
        x_specs = [pl.BlockSpec((tm, D), lambda i: (i, 0))]
        x_args = [x_rows]
        alias = {3: 0}
    return pl.pallas_call(
        functools.partial(_outproj_kernel, n_first=n_first, two_sources=two),
        grid=(n_rows // tm,),
        in_specs=[
            pl.BlockSpec((tm, ATT_Q_W), lambda i: (i, 0)),
            pl.BlockSpec((tm, LRU_WIDTH), lambda i: (i, 0)),
            pl.BlockSpec((tm, NA_W), lambda i: (i, 0)),
            *x_specs,
            pl.BlockSpec((1, 6, D), lambda i: (jnp.minimum(i // per_seq, B), 0, 0)),
            pl.BlockSpec((1, D), lambda i: (0, 0)),
            pl.BlockSpec(w_out_bf.shape, lambda i: (0, 0)),
            pl.BlockSpec(w_router.shape, lambda i: (0, 0)),
            pl.BlockSpec((1, 128), lambda i: (0, 0)),
        ],
        out_specs=[
            pl.BlockSpec((tm, D), lambda i: (i, 0)),
            pl.BlockSpec(packed.shape, lambda i: (i, 0)),
            pl.BlockSpec((tm, 128), lambda i: (i, 0)),
            pl.BlockSpec((8, 128), lambda i: (0, 0)),
        ],
        out_shape=[
            jax.ShapeDtypeStruct((M, D), F32),
            jax.ShapeDtypeStruct((n_rows, packed.shape[1]), packed.dtype),
            jax.ShapeDtypeStruct((n_rows, 128), F32),
            jax.ShapeDtypeStruct((8, 128), F32),
        ],
        input_output_aliases=alias,
        scratch_shapes=[pltpu.VMEM((1, 128), F32)],
        compiler_params=_cparams(("arbitrary",), VMEM_LIMIT),
        name="outproj",
    )(oa, ob, oc, *x_args, mod, g2.reshape(1, D), w_out_bf, w_router, b_router)


def _route_math(x):
    lane = lax.broadcasted_iota(jnp.int32, x.shape, 1)
    big = jnp.int32(1 << 20)
    ninf = -jnp.inf

    def first_argmax(vals):
        m = jnp.max(vals, axis=-1, keepdims=True)
        idx = jnp.min(jnp.where(vals == m, lane, big), axis=-1, keepdims=True)
        return m, idx

    lg = jnp.where(lane < MOE_GROUPS, x, ninf)
    mg, g_star = first_argmax(lg)
    g_gate = 1.0 / jnp.sum(jnp.exp(lg - mg), axis=-1, keepdims=True)
    lo = MOE_GROUPS + MOE_EXPERTS_PER_GROUP * g_star
    le = jnp.where(jnp.logical_and(lane >= lo, lane < lo + MOE_EXPERTS_PER_GROUP), x, ninf)
    v1, i1 = first_argmax(le)
    v2, i2 = first_argmax(jnp.where(lane == i1, ninf, le))
    e21 = jnp.exp(v2 - v1)
    w1 = g_gate / (1.0 + e21)
    w2 = g_gate * e21 / (1.0 + e21)
    e1 = (i1 - MOE_GROUPS).astype(F32)
    e2 = (i2 - MOE_GROUPS).astype(F32)
    return jnp.where(lane == 0, e1, jnp.where(lane == 1, e2, jnp.where(lane == 2, w1,
                     jnp.where(lane == 3, w2, 0.0))))


def _rank_math(r, carry):
    lane = lax.broadcasted_iota(jnp.int32, r.shape, 1).astype(F32)
    oh0 = jnp.where(lane == r[:, 0:1], 1.0, 0.0)
    oh1 = jnp.where(lane == r[:, 1:2], 1.0, 0.0)
    oh = oh0 + oh1
    n = r.shape[0]
    row = lax.broadcasted_iota(jnp.int32, (n, n), 0)
    col = lax.broadcasted_iota(jnp.int32, (n, n), 1)
    tri = jnp.where(col < row, 1.0, 0.0).astype(BF16)
    tot = carry[...] + jnp.dot(tri, oh.astype(BF16), preferred_element_type=F32)
    rank0 = jnp.sum(oh0 * tot, axis=-1, keepdims=True)
    rank1 = jnp.sum(oh1 * tot, axis=-1, keepdims=True)
    lane_i = lax.broadcasted_iota(jnp.int32, r.shape, 1)
    carry[...] = carry[...] + jnp.sum(oh, axis=0, keepdims=True)
    return jnp.where(lane_i == 4, rank0, jnp.where(lane_i == 5, rank1, r))


def _dispatch_kernel(dest_ref, pad_ref, h_ref, xs_hbm, zeros, sem, zsem, *, n_tok):
    i = pl.program_id(0)

    @pl.when(i == 0)
    def _():
        zeros[...] = jnp.zeros_like(zeros)
        pieces = [1 << b for b in range(MOE_BM.bit_length() - 2, 2, -1)]

        def copies(e):
            start, n = pad_ref[0, e], pad_ref[1, e]
            head = jnp.minimum((-start) & 7, n)
            out = []
            for r in range(7):
                out.append((r < head, pltpu.make_async_copy(zeros.at[pl.ds(0, 1)],
                                                            xs_hbm.at[pl.ds(start + r, 1)], zsem)))
            rest = n - head
            for p in pieces:
                at = pl.multiple_of(start + head + (rest & ~(2 * p - 1)), 8)
                out.append(((rest & p) != 0, pltpu.make_async_copy(zeros.at[pl.ds(0, p)],
                                                                   xs_hbm.at[pl.ds(at, p)], zsem)))
            return out

        def fill(e, c):
            for on, cp in copies(e):
                @pl.when(on)
                def _():
                    cp.start()
            return c

        def drain(e, c):
            for on, cp in copies(e):
                @pl.when(on)
                def _():
                    cp.wait()
            return c

        lax.fori_loop(0, MOE_EXPERTS, fill, 0)
        lax.fori_loop(0, MOE_EXPERTS, drain, 0)

        half = zeros.shape[0]
        n_tail = (xs_hbm.shape[0] - pad_ref[2, 0]) // half

        def tail_copy(j):
            at = pl.multiple_of(pad_ref[2, 0] + j * half, half)
            return pltpu.make_async_copy(zeros, xs_hbm.at[pl.ds(at, half)], zsem)

        def tail_fill(j, c):
            tail_copy(j).start()
            return c

        def tail_drain(j, c):
            tail_copy(j).wait()
            return c

        lax.fori_loop(0, n_tail, tail_fill, 0)
        lax.fori_loop(0, n_tail, tail_drain, 0)

    def issue(t, c):
        tok = i * TM + t
        for k in range(2):
            pltpu.make_async_copy(h_ref.at[pl.ds(t, 1)], xs_hbm.at[pl.ds(dest_ref[k * n_tok + tok], 1)], sem).start()
        return c

    lax.fori_loop(0, TM, issue, 0, unroll=8)
    for k in range(2):
        pltpu.make_async_copy(h_ref, xs_hbm.at[pl.ds(0, TM)], sem).wait()


def _dispatch(dest_flat, pad_info, h2, n_rows, nblk):
    W = h2.shape[1]
    return pl.pallas_call(
        functools.partial(_dispatch_kernel, n_tok=nblk * TM),
        grid_spec=pltpu.PrefetchScalarGridSpec(
            num_scalar_prefetch=2,
            grid=(nblk,),
            in_specs=[pl.BlockSpec((TM, W), lambda i, d, p: (i, 0))],
            out_specs=pl.BlockSpec(memory_space=pl.ANY),
            scratch_shapes=[pltpu.VMEM((MOE_BM // 2, W), h2.dtype), pltpu.SemaphoreType.DMA(()),
                            pltpu.SemaphoreType.DMA(())],
        ),
        out_shape=jax.ShapeDtypeStruct((n_rows, W), h2.dtype),
        compiler_params=pltpu.CompilerParams(dimension_semantics=("arbitrary",), has_side_effects=True),
        name="dispatch",
    )(dest_flat, pad_info, h2)


PLAN_EXPERT, PLAN_FIRST, PLAN_SLOT, PLAN_NEXT = 0, 1, 2, 3


def _block_plan(block_e, n_used):
    nb = block_e.shape[0]
    idx = jnp.arange(nb, dtype=jnp.int32)
    is_first = (idx < n_used[0]) & ((idx == 0) | (block_e != jnp.roll(block_e, 1)))
    seg = jnp.cumsum(is_first.astype(jnp.int32)) - 1
    first_pos = jnp.where(is_first, idx, nb)
    at_or_after = jnp.flip(lax.cummin(jnp.flip(first_pos)))
    next_first = jnp.concatenate([at_or_after[1:], jnp.full((1,), nb, jnp.int32)])
    next_e = jnp.where(next_first < nb, block_e[jnp.minimum(next_first, nb - 1)], -1)
    return jnp.stack([block_e, is_first.astype(jnp.int32), seg % 2, next_e]).astype(jnp.int32)


def _expert_weights_step(i, plan_ref, w_hbms, wbuf, wbf, sem, layer):
    def copies(e, slot):
        return [pltpu.make_async_copy(w.at[layer, e], wbuf.at[slot, n], sem.at[slot, n])
                for n, w in enumerate(w_hbms)]

    @pl.when(plan_ref[PLAN_FIRST, i] == 1)
    def _():
        slot = plan_ref[PLAN_SLOT, i]

        @pl.when(i == 0)
        def _():
            for c in copies(plan_ref[PLAN_EXPERT, 0], 0):
                c.start()

        for c in copies(plan_ref[PLAN_EXPERT, i], slot):
            c.wait()
        nrow, ncol = wbuf.shape[2], wbuf.shape[3]
        rows = 256

        def cast_rows(r, c):
            off = pl.multiple_of(r * rows, rows)
            for n in range(len(w_hbms)):
                wbf[pl.ds(off, rows), n * ncol:(n + 1) * ncol] = wbuf[slot, n, pl.ds(off, rows), :].astype(BF16)
            return c

        lax.fori_loop(0, nrow // rows, cast_rows, 0)
        nxt = plan_ref[PLAN_NEXT, i]

        @pl.when(nxt >= 0)
        def _():
            for c in copies(nxt, 1 - slot):
                c.start()


def _moe1_kernel(plan_ref, nu_ref, x_ref, wg_hbm, wu_hbm, o_ref, wbuf, wcat, sem, *, layer):
    i = pl.program_id(0)

    @pl.when(i < nu_ref[0])
    def _():
        _expert_weights_step(i, plan_ref, (wg_hbm, wu_hbm), wbuf, wcat, sem, layer)
        h = jnp.dot(_unpack_bf16_pair(x_ref[...]), wcat[...], preferred_element_type=F32)
        g = h[:, :MOE_HIDDEN]
        u = h[:, MOE_HIDDEN:]
        o_ref[...] = (g * jax.nn.sigmoid(g) * u).astype(BF16)

    @pl.when(i >= nu_ref[0])
    def _():
        o_ref[...] = jnp.zeros_like(o_ref)


def _moe1(plan, n_used, xs, w_gate, w_up, layer):
    P, W = xs.shape
    D = w_gate.shape[2]
    nb = P // MOE_BM
    used_blk = lambda i, plan, nu: (jnp.minimum(i, nu[0] - 1), 0)
    return pl.pallas_call(
        functools.partial(_moe1_kernel, layer=layer),
        grid_spec=pltpu.PrefetchScalarGridSpec(
            num_scalar_prefetch=2,
            grid=(nb,),
            in_specs=[pl.BlockSpec((MOE_BM, W), used_blk), pl.BlockSpec(memory_space=pl.ANY),
                      pl.BlockSpec(memory_space=pl.ANY)],
            out_specs=pl.BlockSpec((MOE_BM, MOE_HIDDEN), lambda i, plan, nu: (i, 0)),
            scratch_shapes=[pltpu.VMEM((2, 2, D, MOE_HIDDEN), F32), pltpu.VMEM((D, 2 * MOE_HIDDEN), BF16),
                            pltpu.SemaphoreType.DMA((2, 2))],
        ),
        out_shape=jax.ShapeDtypeStruct((P, MOE_HIDDEN), BF16),
        compiler_params=_cparams(("arbitrary",), VMEM_LIMIT),
        name="moe_up",
    )(plan, n_used, xs, w_gate, w_up)


def _moe2_kernel(plan_ref, nu_ref, h_ref, wd_hbm, o_ref, wbuf, wbf, sem, *, layer):
    i = pl.program_id(0)

    @pl.when(i < nu_ref[0])
    def _():
        _expert_weights_step(i, plan_ref, (wd_hbm,), wbuf, wbf, sem, layer)
        o_ref[...] = jnp.dot(h_ref[...], wbf[...], preferred_element_type=F32)

    @pl.when(i >= nu_ref[0])
    def _():
        o_ref[...] = jnp.zeros_like(o_ref)


def _moe2(plan, n_used, hmid, w_down, layer):
    P = hmid.shape[0]
    D = w_down.shape[3]
    nb = P // MOE_BM
    return pl.pallas_call(
        functools.partial(_moe2_kernel, layer=layer),
        grid_spec=pltpu.PrefetchScalarGridSpec(
            num_scalar_prefetch=2,
            grid=(nb,),
            in_specs=[
                pl.BlockSpec((MOE_BM, MOE_HIDDEN), lambda i, plan, nu: (i, 0)),
                pl.BlockSpec(memory_space=pl.ANY),
            ],
            out_specs=pl.BlockSpec((MOE_BM, D), lambda i, plan, nu: (i, 0)),
            scratch_shapes=[pltpu.VMEM((2, 1, MOE_HIDDEN, D), F32), pltpu.VMEM((MOE_HIDDEN, D), BF16),
                            pltpu.SemaphoreType.DMA((2, 1))],
        ),
        out_shape=jax.ShapeDtypeStruct((P, D), F32),
        compiler_params=_cparams(("arbitrary",), VMEM_LIMIT),
        name="moe_down",
    )(plan, n_used, hmid, w_down)


def _combine_kernel(dest_ref, yb_hbm, x_ref, r_ref, mod_ref, fg_ref, o_ref, ybuf, sem, *, final):
    i = pl.program_id(0)
    n = pl.num_programs(0)

    n_tok = n * TM

    def row_copy(tok, k, t, slot):
        return pltpu.make_async_copy(yb_hbm.at[pl.ds(dest_ref[k * n_tok + tok], 1)],
                                     ybuf.at[slot, k, pl.ds(t, 1)], sem.at[slot])

    def issue(blk, slot):
        def body(t, c):
            tok = blk * TM + t
            row_copy(tok, 0, t, slot).start()
            row_copy(tok, 1, t, slot).start()
            return c
        lax.fori_loop(0, TM, body, 0, unroll=8)

    @pl.when(i == 0)
    def _():
        issue(0, 0)

    @pl.when(i + 1 < n)
    def _():
        issue(i + 1, (i + 1) % 2)

    slot = i % 2

    for k in range(2):
        pltpu.make_async_copy(yb_hbm.at[pl.ds(0, TM)], ybuf.at[slot, k], sem.at[slot]).wait()
    r = r_ref[...]
    y = r[:, 2:3] * ybuf[slot, 0] + r[:, 3:4] * ybuf[slot, 1]
    xn = x_ref[...] + mod_ref[0, 5:6, :] * y
    if final:
        xn = _rms(xn) * fg_ref[...]
    o_ref[...] = xn


def _combine(dest_flat, yb, x_all, route, mod, final_g, seg_of, nblk, final):
    M, D = x_all.shape
    out_rows = nblk * TM if final else M
    kwargs = {} if final else {"input_output_aliases": {2: 0}}
    return pl.pallas_call(
        functools.partial(_combine_kernel, final=final),
        grid_spec=pltpu.PrefetchScalarGridSpec(
            num_scalar_prefetch=1,
            grid=(nblk,),
            in_specs=[
                pl.BlockSpec(memory_space=pl.ANY),
                pl.BlockSpec((TM, D), lambda i, d: (i, 0)),
                pl.BlockSpec((TM, 128), lambda i, d: (i, 0)),
                pl.BlockSpec((1, 6, D), lambda i, d: (seg_of(i), 0, 0)),
                pl.BlockSpec((1, D), lambda i, d: (0, 0)),
            ],
            out_specs=pl.BlockSpec((TM, D), lambda i, d: (i, 0)),
            scratch_shapes=[pltpu.VMEM((2, 2, TM, D), F32), pltpu.SemaphoreType.DMA((2,))],
        ),
        out_shape=jax.ShapeDtypeStruct((out_rows, D), F32),
        compiler_params=_cparams(("arbitrary",), VMEM_LIMIT),
        name="combine",
        **kwargs,
    )(dest_flat, yb, x_all, route, mod, final_g.reshape(1, D))


def _moe_layer(x_all, h2, route, cnt, mod, final_g, w_gate, w_up, w_down, layer, seg_of, nblk, final):
    T = nblk * TM
    e_ids = route[:, 0:2].T.astype(jnp.int32)
    ranks = route[:, 4:6].T.astype(jnp.int32)
    counts = cnt[0, :MOE_EXPERTS].astype(jnp.int32)
    padded = (counts + MOE_BM - 1) // MOE_BM * MOE_BM
    pad_ends = jnp.cumsum(padded)
    pad_starts = pad_ends - padded
    expert = jnp.arange(MOE_EXPERTS, dtype=jnp.int32)[:, None, None]
    start_of = jnp.sum(jnp.where(e_ids[None] == expert, pad_starts[:, None, None], 0), axis=0)
    dest = (start_of + ranks).reshape(-1)
    pad_info = jnp.stack([pad_starts + counts, padded - counts,
                          jnp.broadcast_to(pad_ends[-1], (MOE_EXPERTS,))]).astype(jnp.int32)
    n_blocks = -(-(2 * T + MOE_EXPERTS * (MOE_BM - 1)) // MOE_BM)
    block_starts = jnp.arange(n_blocks, dtype=jnp.int32) * MOE_BM
    block_e = jnp.sum((block_starts[:, None] >= pad_ends[None, :]).astype(jnp.int32), axis=1)
    block_e = jnp.minimum(block_e, MOE_EXPERTS - 1)
    n_used = (pad_ends[-1:] // MOE_BM).astype(jnp.int32)
    plan = _block_plan(block_e, n_used)
    xs = _dispatch(dest, pad_info, h2, n_blocks * MOE_BM, nblk)
    hmid = _moe1(plan, n_used, xs, w_gate, w_up, layer)
    yb = _moe2(plan, n_used, hmid, w_down, layer)
    return _combine(dest, yb, x_all, route, mod, final_g, seg_of, nblk, final)


def _rope_tables(S):
    pos = jnp.arange(S, dtype=jnp.int32)
    rows = (pos // GRID_W).astype(F32)
    cols = (pos % GRID_W).astype(F32)
    n_freq = HEAD_DIM // 4
    inv = 1.0 / (ROPE_THETA ** (jnp.arange(n_freq, dtype=F32) / n_freq))
    cr, sr = jnp.cos(rows[:, None] * inv), jnp.sin(rows[:, None] * inv)
    cc, sc = jnp.cos(cols[:, None] * inv), jnp.sin(cols[:, None] * inv)
    cos_t = jnp.concatenate([cr, cr, cc, cc], axis=1)
    sin_t = jnp.concatenate([-sr, sr, -sc, sc], axis=1)
    cos_t = jnp.concatenate([cos_t, jnp.ones((TM, HEAD_DIM), F32)], axis=0)
    sin_t = jnp.concatenate([sin_t, jnp.zeros((TM, HEAD_DIM), F32)], axis=0)
    return cos_t, sin_t


def kernel(x, c, ctx, c_ctx, ada_w, ada_b, norm1_g, norm2_g, w_in, w_out, att_q_norm, att_k_norm, conv_w, conv_b, lru_wr, lru_br, lru_wi, lru_bi, lru_lambda, na_rpb, router_wg, router_bg, router_we, router_be, moe_w_gate, moe_w_up, moe_w_down, final_g):
    B, S, D = x.shape
    C = ctx.shape[1]
    L = ada_w.shape[0]
    assert C == TM and S % TM == 0 and S // TM >= 3 and B + 1 <= 8
    nbs = S // TM
    nlat = B * nbs
    m_lat = B * S
    M = m_lat + B * C

    def seg_of(i):
        return jnp.minimum(i // nbs, B)

    x_rows, ctx_rows = x.reshape(m_lat, D), ctx.reshape(B * C, D)
    cvec = jnp.zeros((8, D), F32).at[:B].set(c).at[B].set(c_ctx)
    mod_all = _adaln(cvec, ada_w, ada_b)[:, :B + 1].reshape(L, B + 1, 6, D)
    cos_t, sin_t = _rope_tables(S)
    g_att = ATT_Q_HEADS // ATT_KV_HEADS

    out = None
    for l in range(L):
        last = l == L - 1
        mod = mod_all[l]
        proj = _inproj(x_rows, ctx_rows, mod, norm1_g[l], w_in[l].astype(BF16), B, S)

        q, k_all, v_all = _qkprep(proj, cos_t, sin_t, att_q_norm[l], att_k_norm[l], B, S)
        oa = _gqa_latent(q, k_all, v_all, B, S)

        sp = jax.nn.softplus(-lru_lambda[l].astype(F32))
        hrev = _lru_dir(proj, None, conv_w[l], conv_b[l], lru_wr[l, 1], lru_br[l, 1], lru_wi[l, 1],
                        lru_bi[l, 1], sp[1], B, S, reverse=True)
        ob = _lru_dir(proj, hrev, conv_w[l], conv_b[l], lru_wr[l, 0], lru_br[l, 0], lru_wi[l, 0],
                      lru_bi[l, 0], sp[0], B, S, reverse=False)

        tl, tr = _na_bias_tables(na_rpb[l])
        oc = _na_latent(proj, tl, tr, B, S)

        if not last:
            oa = _ctx_attn(
                q, lambda b, h: (nlat + b, h),
                k_all, pl.BlockSpec((None, TM, HEAD_DIM), lambda b, h: (b, nbs, h // g_att)),
                v_all, pl.BlockSpec((None, TM, HEAD_DIM), lambda b, h: (b, nbs, 2 * (h // g_att))),
                oa, B, ATT_Q_HEADS, nlat, None)
            oc = _ctx_attn(
                proj, lambda b, h: (nlat + b, COL_QN // HEAD_DIM + h),
                proj, pl.BlockSpec((TM, HEAD_DIM), lambda b, h: (nlat + b, COL_KN // HEAD_DIM + h)),
                proj, pl.BlockSpec((TM, HEAD_DIM), lambda b, h: (nlat + b, COL_VN // HEAD_DIM + h)),
                oc, B, NA_HEADS, nlat, HEAD_DIM ** -0.5)

        nblk = nlat if last else M // TM
        w_router = jnp.zeros((D, 128), F32).at[:, :MOE_GROUPS].set(router_wg[l])
        w_router = w_router.at[:, MOE_GROUPS:MOE_GROUPS + MOE_EXPERTS].set(router_we[l])
        b_router = jnp.zeros((1, 128), F32).at[0, :MOE_GROUPS].set(router_bg[l])
        b_router = b_router.at[0, MOE_GROUPS:MOE_GROUPS + MOE_EXPERTS].set(router_be[l])
        x_all, h2, route, cnt = _outproj(oa, ob, oc, x_rows, ctx_rows, mod, norm2_g[l], w_out[l].astype(BF16),
                                         w_router.astype(BF16), b_router, B, S, nblk * TM)
        res = _moe_layer(x_all, h2, route, cnt, mod, final_g, moe_w_gate, moe_w_up, moe_w_down, l,
                         seg_of, nblk, last)
        if last:
            out = res
        else:
            x_rows, ctx_rows = res, None
    return out.reshape(B, S, D)
```

```python
import functools

import numpy as np
import jax
import jax.numpy as jnp
from jax import lax
from jax.experimental import pallas as pl
from jax.experimental.pallas import tpu as pltpu

F32 = jnp.float32
BF16 = jnp.bfloat16

GRID_W = 64
HEAD_DIM = 128
ATT_Q_HEADS = 4
ATT_KV_HEADS = 2
LRU_WIDTH = 1024
LRU_BLOCKS = 8
LRU_C = 8.0
CONV_W = 4
NA_HEADS = 4
NA_ROWS = 8
NA_COLS = 16
ATT_Q_W = ATT_Q_HEADS * HEAD_DIM
ATT_KV_W = ATT_KV_HEADS * HEAD_DIM
NA_W = NA_HEADS * HEAD_DIM
MOE_GROUPS = 4
MOE_EXPERTS_PER_GROUP = 8
MOE_EXPERTS = MOE_GROUPS * MOE_EXPERTS_PER_GROUP
MOE_HIDDEN = 1024
ROPE_THETA = 10000.0
EPS = 1e-6
LOG2E = 1.4426950408889634

COL_QA = 0
COL_KA = COL_QA + ATT_Q_W
COL_VA = COL_KA + ATT_KV_W
COL_UB = COL_VA + ATT_KV_W
COL_GB = COL_UB + LRU_WIDTH
COL_QN = COL_GB + LRU_WIDTH
COL_KN = COL_QN + NA_W
COL_VN = COL_KN + NA_W
IN_WIDTH = COL_VN + NA_W

TM = 256
MOE_BM = 256
NA_QROWS = 4
NA_WROWS = 12
NEG_BIAS = -1e30
VMEM_LIMIT = 56 * 1024 * 1024


def _cparams(sem, vmem=None):
    return pltpu.CompilerParams(dimension_semantics=sem, vmem_limit_bytes=vmem)


def _rms(x):
    return x * lax.rsqrt(jnp.mean(x * x, axis=-1, keepdims=True) + EPS)


def _adaln_kernel(c_ref, w_ref, b_ref, o_ref):
    cv = c_ref[...]
    s = cv * jax.nn.sigmoid(cv)
    o_ref[0] = jnp.dot(s.astype(BF16), w_ref[0].astype(BF16), preferred_element_type=F32) + b_ref[0]


def _adaln(cvec, ada_w, ada_b):
    L, D, N = ada_w.shape
    tn = 1024
    return pl.pallas_call(
        _adaln_kernel,
        grid=(L, N // tn),
        in_specs=[
            pl.BlockSpec((8, D), lambda l, j: (0, 0)),
            pl.BlockSpec((1, D, tn), lambda l, j: (l, 0, j)),
            pl.BlockSpec((1, 1, tn), lambda l, j: (l, 0, j)),
        ],
        out_specs=pl.BlockSpec((1, 8, tn), lambda l, j: (l, 0, j)),
        out_shape=jax.ShapeDtypeStruct((L, 8, N), F32),
        compiler_params=_cparams(("parallel", "parallel"), VMEM_LIMIT),
        name="adaln",
    )(cvec, ada_w, ada_b.reshape(L, 1, N))


def _inproj_kernel(*refs, n_first, two_sources):
    if two_sources:
        x_ref, xb_ref, mod_ref, g_ref, w_ref, o_ref = refs
        x = jnp.where(pl.program_id(0) < n_first, x_ref[...], xb_ref[...])
    else:
        x_ref, mod_ref, g_ref, w_ref, o_ref = refs
        x = x_ref[...]
    h = _rms(x) * g_ref[...]
    h = h * (1.0 + mod_ref[0, 1:2, :]) + mod_ref[0, 0:1, :]
    o_ref[...] = jnp.dot(h.astype(BF16), w_ref[...], preferred_element_type=F32)


def _inproj(x_rows, ctx_rows, mod, g1, w_in_bf, B, S):
    two = ctx_rows is not None
    D = x_rows.shape[1]
    M = x_rows.shape[0] + (ctx_rows.shape[0] if two else 0)
    N = w_in_bf.shape[1]
    tm = 512 if (S % 512 == 0 and (M - B * S) % 512 == 0) else TM
    per_seq = S // tm
    n_first = B * per_seq
    if two:
        n_ctx_blocks = ctx_rows.shape[0] // tm
        x_specs = [pl.BlockSpec((tm, D), lambda i: (jnp.minimum(i, n_first - 1), 0)),
                   pl.BlockSpec((tm, D), lambda i: (jnp.maximum(i - n_first, 0), 0),
                                pipeline_mode=pl.Buffered(1 if n_ctx_blocks == 1 else 2))]
        x_args = [x_rows, ctx_rows]
    else:
        x_specs = [pl.BlockSpec((tm, D), lambda i: (i, 0))]
        x_args = [x_rows]
    return pl.pallas_call(
        functools.partial(_inproj_kernel, n_first=n_first, two_sources=two),
        grid=(M // tm,),
        in_specs=[
            *x_specs,
            pl.BlockSpec((1, 6, D), lambda i: (jnp.minimum(i // per_seq, B), 0, 0)),
            pl.BlockSpec((1, D), lambda i: (0, 0)),
            pl.BlockSpec((D, N), lambda i: (0, 0), pipeline_mode=pl.Buffered(1)),
        ],
        out_specs=pl.BlockSpec((tm, N), lambda i: (i, 0)),
        out_shape=jax.ShapeDtypeStruct((M, N), F32),
        compiler_params=_cparams(("parallel",), VMEM_LIMIT),
        name="inproj",
    )(*x_args, mod, g1.reshape(1, D), w_in_bf)


def _norm_rope(t, g, cos, sin):
    y = _rms(t) * g
    lane = lax.broadcasted_iota(jnp.int32, y.shape, 1)
    first_half = (lane % 64) < 32
    partner = jnp.where(first_half, pltpu.roll(y, 96, 1), pltpu.roll(y, 32, 1))
    return y * cos + partner * sin


def _qkprep_kernel(p_ref, cos_ref, sin_ref, qg_ref, kg_ref, q_ref, k_ref, v_ref):
    cos = cos_ref[...]
    sin = sin_ref[...]
    scale = HEAD_DIM ** -0.5 * LOG2E
    for h in range(ATT_Q_HEADS):
        t = p_ref[:, COL_QA + h * HEAD_DIM:COL_QA + (h + 1) * HEAD_DIM]
        q_ref[:, h * HEAD_DIM:(h + 1) * HEAD_DIM] = (_norm_rope(t, qg_ref[...], cos, sin) * scale).astype(BF16)
    for h in range(ATT_KV_HEADS):
        t = p_ref[:, COL_KA + h * HEAD_DIM:COL_KA + (h + 1) * HEAD_DIM]
        k_ref[0, :, h * HEAD_DIM:(h + 1) * HEAD_DIM] = _norm_rope(t, kg_ref[...], cos, sin).astype(BF16)
        v_ref[0, :, 2 * h * HEAD_DIM:(2 * h + 1) * HEAD_DIM] = (
            p_ref[:, COL_VA + h * HEAD_DIM:COL_VA + (h + 1) * HEAD_DIM].astype(BF16))
        v_ref[0, :, (2 * h + 1) * HEAD_DIM:(2 * h + 2) * HEAD_DIM] = jnp.ones((p_ref.shape[0], HEAD_DIM), BF16)


def _qkprep(proj, cos_t, sin_t, qg, kg, B, S):
    M = proj.shape[0]
    nbs = S // TM
    nlat = B * nbs

    def b_of(i):
        return jnp.where(i < nlat, i // nbs, i - nlat)

    def pos_of(i):
        return jnp.where(i < nlat, i % nbs, nbs)

    kl = S + TM
    return pl.pallas_call(
        _qkprep_kernel,
        grid=(M // TM,),
        in_specs=[
            pl.BlockSpec((TM, COL_UB), lambda i: (i, 0)),
            pl.BlockSpec((TM, HEAD_DIM), lambda i: (pos_of(i), 0)),
            pl.BlockSpec((TM, HEAD_DIM), lambda i: (pos_of(i), 0)),
            pl.BlockSpec((1, HEAD_DIM), lambda i: (0, 0)),
            pl.BlockSpec((1, HEAD_DIM), lambda i: (0, 0)),
        ],
        out_specs=[
            pl.BlockSpec((TM, ATT_Q_W), lambda i: (i, 0)),
            pl.BlockSpec((1, TM, ATT_KV_W), lambda i: (b_of(i), pos_of(i), 0)),
            pl.BlockSpec((1, TM, 2 * ATT_KV_W), lambda i: (b_of(i), pos_of(i), 0)),
        ],
        out_shape=[
            jax.ShapeDtypeStruct((M, ATT_Q_W), BF16),
            jax.ShapeDtypeStruct((B, kl, ATT_KV_W), BF16),
            jax.ShapeDtypeStruct((B, kl, 2 * ATT_KV_W), BF16),
        ],
        compiler_params=_cparams(("parallel",), VMEM_LIMIT),
        name="qkprep",
    )(proj, cos_t, sin_t, qg.reshape(1, HEAD_DIM), kg.reshape(1, HEAD_DIM))


def _gqa_kernel(q_ref, k_ref, v_ref, o_in_ref, o_ref, q2_ref, sa_ref, sb_ref, m_ref, acc_ref, *, tk):
    del o_in_ref
    tq = q_ref.shape[0]
    q2_ref[0:tq, :] = q_ref[:, :HEAD_DIM]
    q2_ref[tq:, :] = q_ref[:, HEAD_DIM:]
    nk = k_ref.shape[1] // tk
    m_ref[...] = jnp.full(m_ref.shape, -jnp.inf, F32)
    acc_ref[...] = jnp.zeros(acc_ref.shape, F32)

    def scores(c, s_ref):
        off = pl.multiple_of(c * tk, tk)
        s_ref[...] = lax.dot_general(q2_ref[...], k_ref[0, pl.ds(off, tk), :], (((1,), (1,)), ((), ())),
                                     preferred_element_type=F32)

    def update(c, s_ref):
        off = pl.multiple_of(c * tk, tk)
        vc = v_ref[0, pl.ds(off, tk), :]
        for hh in range(2):
            rows = slice(hh * tq, (hh + 1) * tq)
            m = m_ref[rows, :]
            m_new = jnp.maximum(m, jnp.max(s_ref[rows, :], axis=-1, keepdims=True))
            m_ref[rows, :] = m_new
            p = jnp.exp2((s_ref[rows, :] - jnp.concatenate([m_new] * (tk // HEAD_DIM), axis=1)).astype(BF16))
            alpha = jnp.exp2(m - m_new)
            acc_ref[rows, :] = (jnp.concatenate([alpha, alpha], axis=1) * acc_ref[rows, :]
                                + jnp.dot(p, vc, preferred_element_type=F32))

    scores(0, sa_ref)
    n_pairs = (nk - 1) // 2

    def body(j, carry):
        scores(2 * j + 1, sb_ref)
        update(2 * j, sa_ref)
        scores(2 * j + 2, sa_ref)
        update(2 * j + 1, sb_ref)
        return carry

    lax.fori_loop(0, n_pairs, body, 0)
    if nk % 2 == 0:
        scores(nk - 1, sb_ref)
        update(nk - 2, sa_ref)
        update(nk - 1, sb_ref)
    else:
        update(nk - 1, sa_ref)
    acc = acc_ref[...]
    o = acc[:, :HEAD_DIM] / acc[:, HEAD_DIM:]
    o_ref[:, :HEAD_DIM] = o[:tq].astype(BF16)
    o_ref[:, HEAD_DIM:] = o[tq:].astype(BF16)


def _gqa_latent(q, k_all, v_all, B, S):
    M = q.shape[0]
    kl = k_all.shape[1]
    tq = 512 if S % 512 == 0 else TM
    nbs = S // tq
    tk = 768 if kl % 768 == 0 else TM
    g = ATT_Q_HEADS // ATT_KV_HEADS
    return pl.pallas_call(
        functools.partial(_gqa_kernel, tk=tk),
        grid=(B, ATT_KV_HEADS, nbs),
        in_specs=[
            pl.BlockSpec((tq, g * HEAD_DIM), lambda b, h, i: (b * nbs + i, h)),
            pl.BlockSpec((1, kl, HEAD_DIM), lambda b, h, i: (b, 0, h)),
            pl.BlockSpec((1, kl, 2 * HEAD_DIM), lambda b, h, i: (b, 0, h)),
            pl.BlockSpec(memory_space=pl.ANY),
        ],
        out_specs=pl.BlockSpec((tq, g * HEAD_DIM), lambda b, h, i: (b * nbs + i, h)),
        out_shape=jax.ShapeDtypeStruct((M, ATT_Q_W), BF16),
        input_output_aliases={3: 0},
        scratch_shapes=[
            pltpu.VMEM((g * tq, HEAD_DIM), BF16),
            pltpu.VMEM((g * tq, tk), F32),
            pltpu.VMEM((g * tq, tk), F32),
            pltpu.VMEM((g * tq, HEAD_DIM), F32),
            pltpu.VMEM((g * tq, 2 * HEAD_DIM), F32),
        ],
        compiler_params=_cparams(("parallel", "parallel", "parallel"), VMEM_LIMIT),
        name="gqa_latent",
    )(q, k_all, v_all, jnp.zeros((M, ATT_Q_W), BF16))


def _ctx_attn_kernel(q_ref, k_ref, v_ref, o_in_ref, o_ref, *, scale):
    del o_in_ref
    if scale is None:
        q = q_ref[...]
    else:
        q = (q_ref[...] * (scale * LOG2E)).astype(BF16)
    k = k_ref[...].astype(BF16)
    s = lax.dot_general(q, k, (((1,), (1,)), ((), ())), preferred_element_type=F32)
    p = jnp.exp2(s - jnp.max(s, axis=-1, keepdims=True))
    l = jnp.sum(p, axis=-1, keepdims=True)
    pv = jnp.dot(p.astype(BF16), v_ref[...].astype(BF16), preferred_element_type=F32)
    o_ref[...] = (pv / l).astype(BF16)


def _ctx_attn(q_arr, q_map, k_arr, k_spec, v_arr, v_spec, o_arr, B, n_heads, ctx_blk0, scale):
    return pl.pallas_call(
        functools.partial(_ctx_attn_kernel, scale=scale),
        grid=(B, n_heads),
        in_specs=[
            pl.BlockSpec((TM, HEAD_DIM), q_map),
            k_spec,
            v_spec,
            pl.BlockSpec(memory_space=pl.ANY),
        ],
        out_specs=pl.BlockSpec((TM, HEAD_DIM), lambda b, h: (ctx_blk0 + b, h)),
        out_shape=jax.ShapeDtypeStruct(o_arr.shape, o_arr.dtype),
        input_output_aliases={3: 0},
        compiler_params=_cparams(("parallel", "parallel"), VMEM_LIMIT),
        name="ctx_attn",
    )(q_arr, k_arr, v_arr, o_arr)


def _lru_kernel(u_ref, up_ref, un_ref, *rest, reverse, final):
    if final:
        (g_ref, hrev_ref, cw_ref, cb_ref, wr_ref, br_ref, wi_ref, bi_ref, sp_ref,
         o_ref, xpad, a_scr, b_scr, h_scr, hst) = rest
    else:
        (cw_ref, cb_ref, wr_ref, br_ref, wi_ref, bi_ref, sp_ref,
         o_ref, xpad, a_scr, b_scr, h_scr, hst) = rest
    s = pl.program_id(1)
    ns = pl.num_programs(1)
    T = u_ref.shape[0]
    W = u_ref.shape[1]

    @pl.when(s == 0)
    def _():
        hst[...] = jnp.zeros_like(hst)

    if reverse:
        j = ns - 1 - s
    else:
        j = s - 1
    is_first = jnp.logical_or(s == 0, j == 0)
    is_last = jnp.logical_or(s == 0, j == ns - 2)
    prev_rows = jnp.where(is_first, 0.0, up_ref[...])
    next_rows = jnp.where(is_last, 0.0, un_ref[...])
    xpad[0:8, :] = prev_rows
    xpad[8:8 + T, :] = u_ref[...]
    xpad[8 + T:16 + T, :] = next_rows
    u = (cb_ref[...]
         + xpad[7:7 + T, :] * cw_ref[0:1, :]
         + xpad[8:8 + T, :] * cw_ref[1:2, :]
         + xpad[9:9 + T, :] * cw_ref[2:3, :]
         + xpad[10:10 + T, :] * cw_ref[3:4, :])

    ub = u.astype(BF16)
    nb = LRU_WIDTH // LRU_BLOCKS
    for n in range(LRU_BLOCKS):
        sl = slice(n * nb, (n + 1) * nb)
        un_ = ub[:, sl]
        r = jax.nn.sigmoid(jnp.dot(un_, wr_ref[n], preferred_element_type=F32) + br_ref[:, sl])
        ig = jax.nn.sigmoid(jnp.dot(un_, wi_ref[n], preferred_element_type=F32) + bi_ref[:, sl])
        log_a = (-LRU_C) * r * sp_ref[:, sl]
        a = jnp.exp(log_a)
        a_scr[:, sl] = a
        one_minus_a2 = -jnp.tanh(log_a) * (a * a + 1.0)
        b_scr[:, sl] = jnp.sqrt(one_minus_a2) * (ig * u[:, sl])

    def step(t, h):
        row = (T - 1 - t) if reverse else t
        h = a_scr[pl.ds(row, 1), :] * h + b_scr[pl.ds(row, 1), :]
        h_scr[pl.ds(row, 1), :] = h
        return h

    hst[...] = lax.fori_loop(0, T, step, hst[...], unroll=8)

    if final:
        y = h_scr[...] + hrev_ref[...]
        o_ref[...] = (y * jax.nn.gelu(g_ref[...])).astype(o_ref.dtype)
    else:
        o_ref[...] = h_scr[...]
    del W


def _lru_dir(proj, hrev, conv_w, conv_b, wr, br, wi, bi, sp, B, S, reverse):
    M = proj.shape[0]
    nbs = S // TM
    nlat = B * nbs
    final = hrev is not None
    ub_blk = COL_UB // LRU_WIDTH
    gb_blk = COL_GB // LRU_WIDTH
    r8 = TM // 8
    n8 = M // 8

    def blk(b, s):
        j = (nbs - s) if reverse else (s - 1)
        return jnp.where(s == 0, nlat + b, b * nbs + j)

    in_specs = [
        pl.BlockSpec((TM, LRU_WIDTH), lambda b, s: (blk(b, s), ub_blk)),
        pl.BlockSpec((8, LRU_WIDTH), lambda b, s: (jnp.maximum(blk(b, s) * r8 - 1, 0), ub_blk)),
        pl.BlockSpec((8, LRU_WIDTH), lambda b, s: (jnp.minimum((blk(b, s) + 1) * r8, n8 - 1), ub_blk)),
    ]
    args = [proj, proj, proj]
    if final:
        in_specs += [
            pl.BlockSpec((TM, LRU_WIDTH), lambda b, s: (blk(b, s), gb_blk)),
            pl.BlockSpec((TM, LRU_WIDTH), lambda b, s: (blk(b, s), 0)),
        ]
        args += [proj, hrev]
    const2 = lambda b, s: (0, 0)
    in_specs += [
        pl.BlockSpec((CONV_W, LRU_WIDTH), const2),
        pl.BlockSpec((1, LRU_WIDTH), const2),
        pl.BlockSpec((LRU_BLOCKS, LRU_WIDTH // LRU_BLOCKS, LRU_WIDTH // LRU_BLOCKS), lambda b, s: (0, 0, 0)),
        pl.BlockSpec((1, LRU_WIDTH), const2),
        pl.BlockSpec((LRU_BLOCKS, LRU_WIDTH // LRU_BLOCKS, LRU_WIDTH // LRU_BLOCKS), lambda b, s: (0, 0, 0)),
        pl.BlockSpec((1, LRU_WIDTH), const2),
        pl.BlockSpec((1, LRU_WIDTH), const2),
    ]
    args += [conv_w, conv_b.reshape(1, -1), wr.astype(BF16), br.reshape(1, -1), wi.astype(BF16),
             bi.reshape(1, -1), sp.reshape(1, -1)]
    return pl.pallas_call(
        functools.partial(_lru_kernel, reverse=reverse, final=final),
        grid=(B, nbs + 1),
        in_specs=in_specs,
        out_specs=pl.BlockSpec((TM, LRU_WIDTH), lambda b, s: (blk(b, s), 0)),
        out_shape=jax.ShapeDtypeStruct((M, LRU_WIDTH), BF16 if final else F32),
        scratch_shapes=[
            pltpu.VMEM((TM + 16, LRU_WIDTH), F32),
            pltpu.VMEM((TM, LRU_WIDTH), F32),
            pltpu.VMEM((TM, LRU_WIDTH), F32),
            pltpu.VMEM((TM, LRU_WIDTH), F32),
            pltpu.VMEM((1, LRU_WIDTH), F32),
        ],
        compiler_params=_cparams(("parallel", "arbitrary"), VMEM_LIMIT),
        name="lru_fwd" if final else "lru_rev",
    )(*args)


def _na_kernel(q_ref, k0_ref, k1_ref, k2_ref, v0_ref, v1_ref, v2_ref, kc_ref, vc_ref, tl_ref, tr_ref,
               o_in_ref, o_ref, s_scr, *, n_rows):
    del o_in_ref
    i = pl.program_id(1)
    nbs = n_rows // NA_QROWS
    r0 = NA_QROWS * i
    ws = NA_QROWS * jnp.clip(i - 1, 0, nbs - 3)
    scale = HEAD_DIM ** -0.5 * LOG2E
    nkeys = s_scr.shape[1]
    ones = jnp.ones((nkeys, HEAD_DIM), BF16)
    for h in range(NA_HEADS):
        hs = slice(h * HEAD_DIM, (h + 1) * HEAD_DIM)
        qh = (q_ref[:, hs] * scale).astype(BF16)
        kh = jnp.concatenate([k0_ref[:, hs], k1_ref[:, hs], k2_ref[:, hs], kc_ref[:, hs]], axis=0).astype(BF16)
        vh = jnp.concatenate([v0_ref[:, hs], v1_ref[:, hs], v2_ref[:, hs], vc_ref[:, hs]], axis=0).astype(BF16)
        vh = jnp.concatenate([vh, ones], axis=1)
        s_scr[...] = lax.dot_general(qh, kh, (((1,), (1,)), ((), ())), preferred_element_type=F32)
        for qr in range(NA_QROWS):
            r = r0 + qr
            rs = jnp.clip(r - NA_ROWS // 2, 0, n_rows - NA_ROWS)

            def tile_of(a):
                kr = ws + a
                valid = jnp.logical_and(kr >= rs, kr < rs + NA_ROWS)
                return jnp.where(valid, kr - r + (NA_ROWS - 1), 2 * NA_ROWS - 1)

            for pr in range(NA_WROWS // 2):
                bias = tl_ref[h, tile_of(2 * pr)] + tr_ref[h, tile_of(2 * pr + 1)]
                rsl = slice(qr * GRID_W, (qr + 1) * GRID_W)
                csl = slice(pr * 2 * GRID_W, (pr + 1) * 2 * GRID_W)
                s_scr[rsl, csl] = s_scr[rsl, csl] + bias
        s = s_scr[...]
        p = jnp.exp2((s - jnp.max(s, axis=-1, keepdims=True)).astype(BF16))
        pv = jnp.dot(p, vh, preferred_element_type=F32)
        o_ref[:, hs] = (pv[:, :HEAD_DIM] / pv[:, HEAD_DIM:]).astype(BF16)


def _na_latent(proj, tl, tr, B, S):
    M = proj.shape[0]
    nbs = S // TM
    nlat = B * nbs
    n_rows = S // GRID_W
    qb, kb, vb = COL_QN // NA_W, COL_KN // NA_W, COL_VN // NA_W

    def wblk(b, i, t):
        return b * nbs + jnp.clip(i - 1, 0, nbs - 3) + t

    blk = (TM, NA_W)
    in_specs = [pl.BlockSpec(blk, lambda b, i: (b * nbs + i, qb))]
    in_specs += [pl.BlockSpec(blk, functools.partial(lambda b, i, t: (wblk(b, i, t), kb), t=t)) for t in range(3)]
    in_specs += [pl.BlockSpec(blk, functools.partial(lambda b, i, t: (wblk(b, i, t), vb), t=t)) for t in range(3)]
    in_specs += [
        pl.BlockSpec(blk, lambda b, i: (nlat + b, kb)),
        pl.BlockSpec(blk, lambda b, i: (nlat + b, vb)),
        pl.BlockSpec(tl.shape, lambda b, i: (0, 0, 0, 0)),
        pl.BlockSpec(tr.shape, lambda b, i: (0, 0, 0, 0)),
        pl.BlockSpec(memory_space=pl.ANY),
    ]
    return pl.pallas_call(
        functools.partial(_na_kernel, n_rows=n_rows),
        grid=(B, nbs),
        in_specs=in_specs,
        out_specs=pl.BlockSpec(blk, lambda b, i: (b * nbs + i, 0)),
        out_shape=jax.ShapeDtypeStruct((M, NA_W), BF16),
        input_output_aliases={11: 0},
        scratch_shapes=[pltpu.VMEM((TM, NA_WROWS * GRID_W + TM), F32)],
        compiler_params=_cparams(("parallel", "parallel"), VMEM_LIMIT),
        name="na_latent",
    )(proj, proj, proj, proj, proj, proj, proj, proj, proj, tl, tr, jnp.zeros((M, NA_W), BF16))


def _na_bias_tables(rpb):
    j = np.arange(GRID_W)
    cs = np.clip(j - NA_COLS // 2, 0, GRID_W - NA_COLS)
    kc = np.arange(GRID_W)
    inside = (kc[None, :] >= cs[:, None]) & (kc[None, :] < cs[:, None] + NA_COLS)
    dc = kc[None, :] - j[:, None] + (NA_COLS - 1)
    sel = (dc[:, :, None] == np.arange(2 * NA_COLS - 1)[None, None, :]) & inside[:, :, None]
    t = jnp.einsum("hrd,jkd->hrjk", rpb * LOG2E, jnp.asarray(sel, F32), precision=lax.Precision.HIGHEST)
    t = jnp.where(inside[None, None], t, NEG_BIAS)
    t = jnp.concatenate([t, jnp.full_like(t[:, :1], NEG_BIAS)], axis=1)
    z = jnp.zeros_like(t)
    return jnp.concatenate([t, z], axis=-1), jnp.concatenate([z, t], axis=-1)


def _outproj_kernel(*refs, n_first, two_sources):
    if two_sources:
        (oa_ref, ob_ref, oc_ref, x_ref, xb_ref, mod_ref, g_ref, w_ref, wr_ref, br_ref,
         xo_ref, h2_ref, rt_ref, cnt_ref, carry) = refs
    else:
        (oa_ref, ob_ref, oc_ref, x_ref, mod_ref, g_ref, w_ref, wr_ref, br_ref,
         xo_ref, h2_ref, rt_ref, cnt_ref, carry) = refs
    i = pl.program_id(0)

    @pl.when(i == 0)
    def _():
        carry[...] = jnp.zeros_like(carry)

    k1 = ATT_Q_W
    k2 = k1 + LRU_WIDTH
    acc = jnp.dot(oa_ref[...], w_ref[0:k1, :], preferred_element_type=F32)
    acc = acc + jnp.dot(ob_ref[...], w_ref[k1:k2, :], preferred_element_type=F32)
    acc = acc + jnp.dot(oc_ref[...], w_ref[k2:, :], preferred_element_type=F32)
    x_in = x_ref[...]
    if two_sources:
        x_in = jnp.where(i < n_first, x_in, xb_ref[...])
    xn = x_in + mod_ref[0, 2:3, :] * acc
    xo_ref[...] = xn
    h2 = _rms(xn) * g_ref[...]
    h2 = h2 * (1.0 + mod_ref[0, 4:5, :]) + mod_ref[0, 3:4, :]
    h2_ref[...] = _pack_bf16_pair(h2)
    logits = jnp.dot(h2.astype(BF16), wr_ref[...], preferred_element_type=F32) + br_ref[...]
    route = _route_math(logits)
    rt_ref[...] = _rank_math(route, carry)
    cnt_ref[...] = jnp.broadcast_to(carry[...], cnt_ref.shape)


def _pack_bf16_pair(x):
    w = x.shape[1] // 2
    lo = pltpu.bitcast(x[:, :w].astype(BF16).astype(F32), jnp.uint32)
    hi = pltpu.bitcast(x[:, w:].astype(BF16).astype(F32), jnp.uint32)
    return hi | (lo >> 16)


def _unpack_bf16_pair(words):
    lo = pltpu.bitcast(words << 16, F32)
    hi = pltpu.bitcast(words & jnp.uint32(0xFFFF0000), F32)
    return jnp.concatenate([lo, hi], axis=1).astype(BF16)


def _outproj(oa, ob, oc, x_rows, ctx_rows, mod, g2, w_out_bf, w_router, b_router, B, S, n_rows):
    M = oa.shape[0]
    D = x_rows.shape[1]
    two = ctx_rows is not None
    tm = 512 if (S % 512 == 0 and (M - B * S) % 512 == 0) else TM
    per_seq = S // tm
    n_first = B * per_seq
    packed = jax.eval_shape(_pack_bf16_pair, jax.ShapeDtypeStruct((tm, D), F32))
    if two:
        x_specs = [pl.BlockSpec((tm, D), lambda i: (jnp.minimum(i, n_first - 1), 0)),
                   pl.BlockSpec((tm, D), lambda i: (jnp.maximum(i - n_first, 0), 0))]
        x_args = [x_rows, ctx_rows]
        alias = {}
    else:
        x_specs = [pl.BlockSpec((tm, D), lambda i: (i, 0))]
        x_args = [x_rows]
        alias = {3: 0}
    return pl.pallas_call(
        functools.partial(_outproj_kernel, n_first=n_first, two_sources=two),
        grid=(n_rows // tm,),
        in_specs=[
            pl.BlockSpec((tm, ATT_Q_W), lambda i: (i, 0)),
            pl.BlockSpec((tm, LRU_WIDTH), lambda i: (i, 0)),
            pl.BlockSpec((tm, NA_W), lambda i: (i, 0)),
            *x_specs,
            pl.BlockSpec((1, 6, D), lambda i: (jnp.minimum(i // per_seq, B), 0, 0)),
            pl.BlockSpec((1, D), lambda i: (0, 0)),
            pl.BlockSpec(w_out_bf.shape, lambda i: (0, 0)),
            pl.BlockSpec(w_router.shape, lambda i: (0, 0)),
            pl.BlockSpec((1, 128), lambda i: (0, 0)),
        ],
        out_specs=[
            pl.BlockSpec((tm, D), lambda i: (i, 0)),
            pl.BlockSpec(packed.shape, lambda i: (i, 0)),
            pl.BlockSpec((tm, 128), lambda i: (i, 0)),
            pl.BlockSpec((8, 128), lambda i: (0, 0)),
        ],
        out_shape=[
            jax.ShapeDtypeStruct((M, D), F32),
            jax.ShapeDtypeStruct((n_rows, packed.shape[1]), packed.dtype),
            jax.ShapeDtypeStruct((n_rows, 128), F32),
            jax.ShapeDtypeStruct((8, 128), F32),
        ],
        input_output_aliases=alias,
        scratch_shapes=[pltpu.VMEM((1, 128), F32)],
        compiler_params=_cparams(("arbitrary",), VMEM_LIMIT),
        name="outproj",
    )(oa, ob, oc, *x_args, mod, g2.reshape(1, D), w_out_bf, w_router, b_router)


def _route_math(x):
    lane = lax.broadcasted_iota(jnp.int32, x.shape, 1)
    big = jnp.int32(1 << 20)
    ninf = -jnp.inf

    def first_argmax(vals):
        m = jnp.max(vals, axis=-1, keepdims=True)
        idx = jnp.min(jnp.where(vals == m, lane, big), axis=-1, keepdims=True)
        return m, idx

    lg = jnp.where(lane < MOE_GROUPS, x, ninf)
    mg, g_star = first_argmax(lg)
    g_gate = 1.0 / jnp.sum(jnp.exp(lg - mg), axis=-1, keepdims=True)
    lo = MOE_GROUPS + MOE_EXPERTS_PER_GROUP * g_star
    le = jnp.where(jnp.logical_and(lane >= lo, lane < lo + MOE_EXPERTS_PER_GROUP), x, ninf)
    v1, i1 = first_argmax(le)
    v2, i2 = first_argmax(jnp.where(lane == i1, ninf, le))
    e21 = jnp.exp(v2 - v1)
    w1 = g_gate / (1.0 + e21)
    w2 = g_gate * e21 / (1.0 + e21)
    e1 = (i1 - MOE_GROUPS).astype(F32)
    e2 = (i2 - MOE_GROUPS).astype(F32)
    return jnp.where(lane == 0, e1, jnp.where(lane == 1, e2, jnp.where(lane == 2, w1,
                     jnp.where(lane == 3, w2, 0.0))))


def _rank_math(r, carry):
    lane = lax.broadcasted_iota(jnp.int32, r.shape, 1).astype(F32)
    oh0 = jnp.where(lane == r[:, 0:1], 1.0, 0.0)
    oh1 = jnp.where(lane == r[:, 1:2], 1.0, 0.0)
    oh = oh0 + oh1
    n = r.shape[0]
    row = lax.broadcasted_iota(jnp.int32, (n, n), 0)
    col = lax.broadcasted_iota(jnp.int32, (n, n), 1)
    tri = jnp.where(col < row, 1.0, 0.0).astype(BF16)
    tot = carry[...] + jnp.dot(tri, oh.astype(BF16), preferred_element_type=F32)
    rank0 = jnp.sum(oh0 * tot, axis=-1, keepdims=True)
    rank1 = jnp.sum(oh1 * tot, axis=-1, keepdims=True)
    lane_i = lax.broadcasted_iota(jnp.int32, r.shape, 1)
    carry[...] = carry[...] + jnp.sum(oh, axis=0, keepdims=True)
    return jnp.where(lane_i == 4, rank0, jnp.where(lane_i == 5, rank1, r))


def _dispatch_kernel(dest_ref, pad_ref, h_ref, xs_hbm, zeros, sem, zsem, *, n_tok):
    i = pl.program_id(0)

    @pl.when(i == 0)
    def _():
        zeros[...] = jnp.zeros_like(zeros)
        pieces = [1 << b for b in range(MOE_BM.bit_length() - 2, 2, -1)]

        def copies(e):
            start, n = pad_ref[0, e], pad_ref[1, e]
            head = jnp.minimum((-start) & 7, n)
            out = []
            for r in range(7):
                out.append((r < head, pltpu.make_async_copy(zeros.at[pl.ds(0, 1)],
                                                            xs_hbm.at[pl.ds(start + r, 1)], zsem)))
            rest = n - head
            for p in pieces:
                at = pl.multiple_of(start + head + (rest & ~(2 * p - 1)), 8)
                out.append(((rest & p) != 0, pltpu.make_async_copy(zeros.at[pl.ds(0, p)],
                                                                   xs_hbm.at[pl.ds(at, p)], zsem)))
            return out

        def fill(e, c):
            for on, cp in copies(e):
                @pl.when(on)
                def _():
                    cp.start()
            return c

        def drain(e, c):
            for on, cp in copies(e):
                @pl.when(on)
                def _():
                    cp.wait()
            return c

        lax.fori_loop(0, MOE_EXPERTS, fill, 0)
        lax.fori_loop(0, MOE_EXPERTS, drain, 0)

        half = zeros.shape[0]
        n_tail = (xs_hbm.shape[0] - pad_ref[2, 0]) // half

        def tail_copy(j):
            at = pl.multiple_of(pad_ref[2, 0] + j * half, half)
            return pltpu.make_async_copy(zeros, xs_hbm.at[pl.ds(at, half)], zsem)

        def tail_fill(j, c):
            tail_copy(j).start()
            return c

        def tail_drain(j, c):
            tail_copy(j).wait()
            return c

        lax.fori_loop(0, n_tail, tail_fill, 0)
        lax.fori_loop(0, n_tail, tail_drain, 0)

    def issue(j, c):
        t0 = pl.multiple_of(j * 8, 8)
        for k in range(2):
            d0 = k * n_tok + i * TM + t0
            for r in range(8):
                pltpu.make_async_copy(h_ref.at[pl.ds(t0 + r, 1)], xs_hbm.at[pl.ds(dest_ref[d0 + r], 1)], sem).start()
        return c

    lax.fori_loop(0, TM // 8, issue, 0)
    for k in range(2):
        pltpu.make_async_copy(h_ref, xs_hbm.at[pl.ds(0, TM)], sem).wait()


def _dispatch(dest_flat, pad_info, h2, n_rows, nblk):
    W = h2.shape[1]
    return pl.pallas_call(
        functools.partial(_dispatch_kernel, n_tok=nblk * TM),
        grid_spec=pltpu.PrefetchScalarGridSpec(
            num_scalar_prefetch=2,
            grid=(nblk,),
            in_specs=[pl.BlockSpec((TM, W), lambda i, d, p: (i, 0))],
            out_specs=pl.BlockSpec(memory_space=pl.ANY),
            scratch_shapes=[pltpu.VMEM((MOE_BM // 2, W), h2.dtype), pltpu.SemaphoreType.DMA(()),
                            pltpu.SemaphoreType.DMA(())],
        ),
        out_shape=jax.ShapeDtypeStruct((n_rows, W), h2.dtype),
        compiler_params=pltpu.CompilerParams(dimension_semantics=("arbitrary",), has_side_effects=True),
        name="dispatch",
    )(dest_flat, pad_info, h2)


PLAN_EXPERT, PLAN_FIRST, PLAN_SLOT, PLAN_NEXT, PLAN_HALF = 0, 1, 2, 3, 4


def _block_plan(block_e, n_used, pad_starts, counts):
    nb = block_e.shape[0]
    idx = jnp.arange(nb, dtype=jnp.int32)
    is_first = (idx < n_used[0]) & ((idx == 0) | (block_e != jnp.roll(block_e, 1)))
    seg = jnp.cumsum(is_first.astype(jnp.int32)) - 1
    first_pos = jnp.where(is_first, idx, nb)
    at_or_after = jnp.flip(lax.cummin(jnp.flip(first_pos)))
    next_first = jnp.concatenate([at_or_after[1:], jnp.full((1,), nb, jnp.int32)])
    next_e = jnp.where(next_first < nb, block_e[jnp.minimum(next_first, nb - 1)], -1)
    valid = counts[block_e] - (idx * MOE_BM - pad_starts[block_e])
    half_only = (valid <= MOE_BM // 2).astype(jnp.int32)
    return jnp.stack([block_e, is_first.astype(jnp.int32), seg % 2, next_e, half_only]).astype(jnp.int32)


def _expert_weights_step(i, plan_ref, w_hbms, wbuf, wbf, sem, layer):
    def copies(e, slot):
        return [pltpu.make_async_copy(w.at[layer, e], wbuf.at[slot, n], sem.at[slot, n])
                for n, w in enumerate(w_hbms)]

    @pl.when(plan_ref[PLAN_FIRST, i] == 1)
    def _():
        slot = plan_ref[PLAN_SLOT, i]

        @pl.when(i == 0)
        def _():
            for c in copies(plan_ref[PLAN_EXPERT, 0], 0):
                c.start()

        for c in copies(plan_ref[PLAN_EXPERT, i], slot):
            c.wait()
        nrow, ncol = wbuf.shape[2], wbuf.shape[3]
        rows = 256

        def cast_rows(r, c):
            off = pl.multiple_of(r * rows, rows)
            for n in range(len(w_hbms)):
                wbf[pl.ds(off, rows), n * ncol:(n + 1) * ncol] = wbuf[slot, n, pl.ds(off, rows), :].astype(BF16)
            return c

        lax.fori_loop(0, nrow // rows, cast_rows, 0)
        nxt = plan_ref[PLAN_NEXT, i]

        @pl.when(nxt >= 0)
        def _():
            for c in copies(nxt, 1 - slot):
                c.start()


def _moe1_kernel(plan_ref, nu_ref, x_ref, wg_hbm, wu_hbm, o_ref, wbuf, wcat, sem, *, layer):
    i = pl.program_id(0)

    @pl.when(i < nu_ref[0])
    def _():
        _expert_weights_step(i, plan_ref, (wg_hbm, wu_hbm), wbuf, wcat, sem, layer)

        def gated(rows):
            h = jnp.dot(_unpack_bf16_pair(x_ref[rows, :]), wcat[...], preferred_element_type=F32)
            g = h[:, :MOE_HIDDEN]
            u = h[:, MOE_HIDDEN:]
            return (g * jax.nn.sigmoid(g) * u).astype(BF16)

        half = MOE_BM // 2

        @pl.when(plan_ref[PLAN_HALF, i] == 1)
        def _():
            o_ref[0:half, :] = gated(slice(0, half))
            o_ref[half:, :] = jnp.zeros((MOE_BM - half, MOE_HIDDEN), BF16)

        @pl.when(plan_ref[PLAN_HALF, i] == 0)
        def _():
            o_ref[...] = gated(slice(None))

    @pl.when(i >= nu_ref[0])
    def _():
        o_ref[...] = jnp.zeros_like(o_ref)


def _moe1(plan, n_used, xs, w_gate, w_up, layer):
    P, W = xs.shape
    D = w_gate.shape[2]
    nb = P // MOE_BM
    used_blk = lambda i, plan, nu: (jnp.minimum(i, nu[0] - 1), 0)
    return pl.pallas_call(
        functools.partial(_moe1_kernel, layer=layer),
        grid_spec=pltpu.PrefetchScalarGridSpec(
            num_scalar_prefetch=2,
            grid=(nb,),
            in_specs=[pl.BlockSpec((MOE_BM, W), used_blk), pl.BlockSpec(memory_space=pl.ANY),
                      pl.BlockSpec(memory_space=pl.ANY)],
            out_specs=pl.BlockSpec((MOE_BM, MOE_HIDDEN), lambda i, plan, nu: (i, 0)),
            scratch_shapes=[pltpu.VMEM((2, 2, D, MOE_HIDDEN), F32), pltpu.VMEM((D, 2 * MOE_HIDDEN), BF16),
                            pltpu.SemaphoreType.DMA((2, 2))],
        ),
        out_shape=jax.ShapeDtypeStruct((P, MOE_HIDDEN), BF16),
        compiler_params=_cparams(("arbitrary",), VMEM_LIMIT),
        name="moe_up",
    )(plan, n_used, xs, w_gate, w_up)


def _moe2_kernel(plan_ref, nu_ref, h_ref, wd_hbm, o_ref, wbuf, wbf, sem, *, layer):
    i = pl.program_id(0)

    @pl.when(i < nu_ref[0])
    def _():
        _expert_weights_step(i, plan_ref, (wd_hbm,), wbuf, wbf, sem, layer)
        half = MOE_BM // 2

        @pl.when(plan_ref[PLAN_HALF, i] == 1)
        def _():
            o_ref[0:half, :] = jnp.dot(h_ref[0:half, :], wbf[...], preferred_element_type=F32)
            o_ref[half:, :] = jnp.zeros((MOE_BM - half, o_ref.shape[1]), F32)

        @pl.when(plan_ref[PLAN_HALF, i] == 0)
        def _():
            o_ref[...] = jnp.dot(h_ref[...], wbf[...], preferred_element_type=F32)

    @pl.when(i >= nu_ref[0])
    def _():
        o_ref[...] = jnp.zeros_like(o_ref)


def _moe2(plan, n_used, hmid, w_down, layer):
    P = hmid.shape[0]
    D = w_down.shape[3]
    nb = P // MOE_BM
    return pl.pallas_call(
        functools.partial(_moe2_kernel, layer=layer),
        grid_spec=pltpu.PrefetchScalarGridSpec(
            num_scalar_prefetch=2,
            grid=(nb,),
            in_specs=[
                pl.BlockSpec((MOE_BM, MOE_HIDDEN), lambda i, plan, nu: (i, 0)),
                pl.BlockSpec(memory_space=pl.ANY),
            ],
            out_specs=pl.BlockSpec((MOE_BM, D), lambda i, plan, nu: (i, 0)),
            scratch_shapes=[pltpu.VMEM((2, 1, MOE_HIDDEN, D), F32), pltpu.VMEM((MOE_HIDDEN, D), BF16),
                            pltpu.SemaphoreType.DMA((2, 1))],
        ),
        out_shape=jax.ShapeDtypeStruct((P, D), F32),
        compiler_params=_cparams(("arbitrary",), VMEM_LIMIT),
        name="moe_down",
    )(plan, n_used, hmid, w_down)


def _combine_kernel(dest_ref, yb_hbm, x_ref, r_ref, mod_ref, fg_ref, o_ref, ybuf, sem, *, final):
    i = pl.program_id(0)
    n = pl.num_programs(0)

    n_tok = n * TM

    def row_copy(tok, k, t, slot):
        return pltpu.make_async_copy(yb_hbm.at[pl.ds(dest_ref[k * n_tok + tok], 1)],
                                     ybuf.at[slot, k, pl.ds(t, 1)], sem.at[slot])

    def issue(blk, slot):
        def body(j, c):
            t0 = pl.multiple_of(j * 8, 8)
            for k in range(2):
                for r in range(8):
                    row_copy(blk * TM + t0 + r, k, t0 + r, slot).start()
            return c
        lax.fori_loop(0, TM // 8, body, 0)

    @pl.when(i == 0)
    def _():
        issue(0, 0)

    @pl.when(i + 1 < n)
    def _():
        issue(i + 1, (i + 1) % 2)

    slot = i % 2

    for k in range(2):
        pltpu.make_async_copy(yb_hbm.at[pl.ds(0, TM)], ybuf.at[slot, k], sem.at[slot]).wait()
    r = r_ref[...]
    y = r[:, 2:3] * ybuf[slot, 0] + r[:, 3:4] * ybuf[slot, 1]
    xn = x_ref[...] + mod_ref[0, 5:6, :] * y
    if final:
        xn = _rms(xn) * fg_ref[...]
    o_ref[...] = xn


def _combine(dest_flat, yb, x_all, route, mod, final_g, seg_of, nblk, final):
    M, D = x_all.shape
    out_rows = nblk * TM if final else M
    kwargs = {} if final else {"input_output_aliases": {2: 0}}
    return pl.pallas_call(
        functools.partial(_combine_kernel, final=final),
        grid_spec=pltpu.PrefetchScalarGridSpec(
            num_scalar_prefetch=1,
            grid=(nblk,),
            in_specs=[
                pl.BlockSpec(memory_space=pl.ANY),
                pl.BlockSpec((TM, D), lambda i, d: (i, 0)),
                pl.BlockSpec((TM, 128), lambda i, d: (i, 0)),
                pl.BlockSpec((1, 6, D), lambda i, d: (seg_of(i), 0, 0)),
                pl.BlockSpec((1, D), lambda i, d: (0, 0)),
            ],
            out_specs=pl.BlockSpec((TM, D), lambda i, d: (i, 0)),
            scratch_shapes=[pltpu.VMEM((2, 2, TM, D), F32), pltpu.SemaphoreType.DMA((2,))],
        ),
        out_shape=jax.ShapeDtypeStruct((out_rows, D), F32),
        compiler_params=_cparams(("arbitrary",), VMEM_LIMIT),
        name="combine",
        **kwargs,
    )(dest_flat, yb, x_all, route, mod, final_g.reshape(1, D))


def _moe_layer(x_all, h2, route, cnt, mod, final_g, w_gate, w_up, w_down, layer, seg_of, nblk, final):
    T = nblk * TM
    e_ids = route[:, 0:2].T.astype(jnp.int32)
    ranks = route[:, 4:6].T.astype(jnp.int32)
    counts = cnt[0, :MOE_EXPERTS].astype(jnp.int32)
    padded = (counts + MOE_BM - 1) // MOE_BM * MOE_BM
    pad_ends = jnp.cumsum(padded)
    pad_starts = pad_ends - padded
    expert = jnp.arange(MOE_EXPERTS, dtype=jnp.int32)[:, None, None]
    start_of = jnp.sum(jnp.where(e_ids[None] == expert, pad_starts[:, None, None], 0), axis=0)
    dest = (start_of + ranks).reshape(-1)
    pad_info = jnp.stack([pad_starts + counts, padded - counts,
                          jnp.broadcast_to(pad_ends[-1], (MOE_EXPERTS,))]).astype(jnp.int32)
    n_blocks = -(-(2 * T + MOE_EXPERTS * (MOE_BM - 1)) // MOE_BM)
    block_starts = jnp.arange(n_blocks, dtype=jnp.int32) * MOE_BM
    block_e = jnp.sum((block_starts[:, None] >= pad_ends[None, :]).astype(jnp.int32), axis=1)
    block_e = jnp.minimum(block_e, MOE_EXPERTS - 1)
    n_used = (pad_ends[-1:] // MOE_BM).astype(jnp.int32)
    plan = _block_plan(block_e, n_used, pad_starts, counts)
    xs = _dispatch(dest, pad_info, h2, n_blocks * MOE_BM, nblk)
    hmid = _moe1(plan, n_used, xs, w_gate, w_up, layer)
    yb = _moe2(plan, n_used, hmid, w_down, layer)
    return _combine(dest, yb, x_all, route, mod, final_g, seg_of, nblk, final)


def _rope_tables(S):
    pos = jnp.arange(S, dtype=jnp.int32)
    rows = (pos // GRID_W).astype(F32)
    cols = (pos % GRID_W).astype(F32)
    n_freq = HEAD_DIM // 4
    inv = 1.0 / (ROPE_THETA ** (jnp.arange(n_freq, dtype=F32) / n_freq))
    cr, sr = jnp.cos(rows[:, None] * inv), jnp.sin(rows[:, None] * inv)
    cc, sc = jnp.cos(cols[:, None] * inv), jnp.sin(cols[:, None] * inv)
    cos_t = jnp.concatenate([cr, cr, cc, cc], axis=1)
    sin_t = jnp.concatenate([-sr, sr, -sc, sc], axis=1)
    cos_t = jnp.concatenate([cos_t, jnp.ones((TM, HEAD_DIM), F32)], axis=0)
    sin_t = jnp.concatenate([sin_t, jnp.zeros((TM, HEAD_DIM), F32)], axis=0)
    return cos_t, sin_t


def kernel(x, c, ctx, c_ctx, ada_w, ada_b, norm1_g, norm2_g, w_in, w_out, att_q_norm, att_k_norm, conv_w, conv_b, lru_wr, lru_br, lru_wi, lru_bi, lru_lambda, na_rpb, router_wg, router_bg, router_we, router_be, moe_w_gate, moe_w_up, moe_w_down, final_g):
    B, S, D = x.shape
    C = ctx.shape[1]
    L = ada_w.shape[0]
    assert C == TM and S % TM == 0 and S // TM >= 3 and B + 1 <= 8
    nbs = S // TM
    nlat = B * nbs
    m_lat = B * S
    M = m_lat + B * C

    def seg_of(i):
        return jnp.minimum(i // nbs, B)

    x_rows, ctx_rows = x.reshape(m_lat, D), ctx.reshape(B * C, D)
    cvec = jnp.zeros((8, D), F32).at[:B].set(c).at[B].set(c_ctx)
    mod_all = _adaln(cvec, ada_w, ada_b)[:, :B + 1].reshape(L, B + 1, 6, D)
    cos_t, sin_t = _rope_tables(S)
    g_att = ATT_Q_HEADS // ATT_KV_HEADS

    out = None
    for l in range(L):
        last = l == L - 1
        mod = mod_all[l]
        proj = _inproj(x_rows, ctx_rows, mod, norm1_g[l], w_in[l].astype(BF16), B, S)

        q, k_all, v_all = _qkprep(proj, cos_t, sin_t, att_q_norm[l], att_k_norm[l], B, S)
        oa = _gqa_latent(q, k_all, v_all, B, S)

        sp = jax.nn.softplus(-lru_lambda[l].astype(F32))
        hrev = _lru_dir(proj, None, conv_w[l], conv_b[l], lru_wr[l, 1], lru_br[l, 1], lru_wi[l, 1],
                        lru_bi[l, 1], sp[1], B, S, reverse=True)
        ob = _lru_dir(proj, hrev, conv_w[l], conv_b[l], lru_wr[l, 0], lru_br[l, 0], lru_wi[l, 0],
                      lru_bi[l, 0], sp[0], B, S, reverse=False)

        tl, tr = _na_bias_tables(na_rpb[l])
        oc = _na_latent(proj, tl, tr, B, S)

        if not last:
            oa = _ctx_attn(
                q, lambda b, h: (nlat + b, h),
                k_all, pl.BlockSpec((None, TM, HEAD_DIM), lambda b, h: (b, nbs, h // g_att)),
                v_all, pl.BlockSpec((None, TM, HEAD_DIM), lambda b, h: (b, nbs, 2 * (h // g_att))),
                oa, B, ATT_Q_HEADS, nlat, None)
            oc = _ctx_attn(
                proj, lambda b, h: (nlat + b, COL_QN // HEAD_DIM + h),
                proj, pl.BlockSpec((TM, HEAD_DIM), lambda b, h: (nlat + b, COL_KN // HEAD_DIM + h)),
                proj, pl.BlockSpec((TM, HEAD_DIM), lambda b, h: (nlat + b, COL_VN // HEAD_DIM + h)),
                oc, B, NA_HEADS, nlat, HEAD_DIM ** -0.5)

        nblk = nlat if last else M // TM
        w_router = jnp.zeros((D, 128), F32).at[:, :MOE_GROUPS].set(router_wg[l])
        w_router = w_router.at[:, MOE_GROUPS:MOE_GROUPS + MOE_EXPERTS].set(router_we[l])
        b_router = jnp.zeros((1, 128), F32).at[0, :MOE_GROUPS].set(router_bg[l])
        b_router = b_router.at[0, MOE_GROUPS:MOE_GROUPS + MOE_EXPERTS].set(router_be[l])
        x_all, h2, route, cnt = _outproj(oa, ob, oc, x_rows, ctx_rows, mod, norm2_g[l], w_out[l].astype(BF16),
                                         w_router.astype(BF16), b_router, B, S, nblk * TM)
        res = _moe_layer(x_all, h2, route, cnt, mod, final_g, moe_w_gate, moe_w_up, moe_w_down, l,
                         seg_of, nblk, last)
        if last:
            out = res
        else:
            x_rows, ctx_rows = res, None
    return out.reshape(B, S, D)
```

```python
import functools

import numpy as np
import jax
import jax.numpy as jnp
from jax import lax
from jax.experimental import pallas as pl
from jax.experimental.pallas import tpu as pltpu

F32 = jnp.float32
BF16 = jnp.bfloat16

GRID_W = 64
HEAD_DIM = 128
ATT_Q_HEADS = 4
ATT_KV_HEADS = 2
LRU_WIDTH = 1024
LRU_BLOCKS = 8
LRU_C = 8.0
CONV_W = 4
NA_HEADS = 4
NA_ROWS = 8
NA_COLS = 16
ATT_Q_W = ATT_Q_HEADS * HEAD_DIM
ATT_KV_W = ATT_KV_HEADS * HEAD_DIM
NA_W = NA_HEADS * HEAD_DIM
MOE_GROUPS = 4
MOE_EXPERTS_PER_GROUP = 8
MOE_EXPERTS = MOE_GROUPS * MOE_EXPERTS_PER_GROUP
MOE_HIDDEN = 1024
ROPE_THETA = 10000.0
EPS = 1e-6
LOG2E = 1.4426950408889634

COL_QA = 0
COL_KA = COL_QA + ATT_Q_W
COL_VA = COL_KA + ATT_KV_W
COL_UB = COL_VA + ATT_KV_W
COL_GB = COL_UB + LRU_WIDTH
COL_QN = COL_GB + LRU_WIDTH
COL_KN = COL_QN + NA_W
COL_VN = COL_KN + NA_W
IN_WIDTH = COL_VN + NA_W

TM = 256
MOE_BM = 256
NA_QROWS = 4
NA_WROWS = 12
NEG_BIAS = -1e30
VMEM_LIMIT = 56 * 1024 * 1024


def _cparams(sem, vmem=None):
    return pltpu.CompilerParams(dimension_semantics=sem, vmem_limit_bytes=vmem)


def _rms(x):
    return x * lax.rsqrt(jnp.mean(x * x, axis=-1, keepdims=True) + EPS)


def _adaln_kernel(c_ref, w_ref, b_ref, o_ref):
    cv = c_ref[...]
    s = cv * jax.nn.sigmoid(cv)
    o_ref[0] = jnp.dot(s.astype(BF16), w_ref[0].astype(BF16), preferred_element_type=F32) + b_ref[0]


def _adaln(cvec, ada_w, ada_b):
    L, D, N = ada_w.shape
    tn = 1024
    return pl.pallas_call(
        _adaln_kernel,
        grid=(L, N // tn),
        in_specs=[
            pl.BlockSpec((8, D), lambda l, j: (0, 0)),
            pl.BlockSpec((1, D, tn), lambda l, j: (l, 0, j)),
            pl.BlockSpec((1, 1, tn), lambda l, j: (l, 0, j)),
        ],
        out_specs=pl.BlockSpec((1, 8, tn), lambda l, j: (l, 0, j)),
        out_shape=jax.ShapeDtypeStruct((L, 8, N), F32),
        compiler_params=_cparams(("parallel", "parallel"), VMEM_LIMIT),
        name="adaln",
    )(cvec, ada_w, ada_b.reshape(L, 1, N))


def _inproj_kernel(*refs, n_first, two_sources):
    if two_sources:
        x_ref, xb_ref, mod_ref, g_ref, w_ref, o_ref = refs
        x = jnp.where(pl.program_id(0) < n_first, x_ref[...], xb_ref[...])
    else:
        x_ref, mod_ref, g_ref, w_ref, o_ref = refs
        x = x_ref[...]
    h = _rms(x) * g_ref[...]
    h = h * (1.0 + mod_ref[0, 1:2, :]) + mod_ref[0, 0:1, :]
    o_ref[...] = jnp.dot(h.astype(BF16), w_ref[...], preferred_element_type=F32)


def _inproj(x_rows, ctx_rows, mod, g1, w_in_bf, B, S):
    two = ctx_rows is not None
    D = x_rows.shape[1]
    M = x_rows.shape[0] + (ctx_rows.shape[0] if two else 0)
    N = w_in_bf.shape[1]
    tm = 512 if (S % 512 == 0 and (M - B * S) % 512 == 0) else TM
    per_seq = S // tm
    n_first = B * per_seq
    if two:
        n_ctx_blocks = ctx_rows.shape[0] // tm
        x_specs = [pl.BlockSpec((tm, D), lambda i: (jnp.minimum(i, n_first - 1), 0)),
                   pl.BlockSpec((tm, D), lambda i: (jnp.maximum(i - n_first, 0), 0),
                                pipeline_mode=pl.Buffered(1 if n_ctx_blocks == 1 else 2))]
        x_args = [x_rows, ctx_rows]
    else:
        x_specs = [pl.BlockSpec((tm, D), lambda i: (i, 0))]
        x_args = [x_rows]
    return pl.pallas_call(
        functools.partial(_inproj_kernel, n_first=n_first, two_sources=two),
        grid=(M // tm,),
        in_specs=[
            *x_specs,
            pl.BlockSpec((1, 6, D), lambda i: (jnp.minimum(i // per_seq, B), 0, 0)),
            pl.BlockSpec((1, D), lambda i: (0, 0)),
            pl.BlockSpec((D, N), lambda i: (0, 0), pipeline_mode=pl.Buffered(1)),
        ],
        out_specs=pl.BlockSpec((tm, N), lambda i: (i, 0)),
        out_shape=jax.ShapeDtypeStruct((M, N), F32),
        compiler_params=_cparams(("parallel",), VMEM_LIMIT),
        name="inproj",
    )(*x_args, mod, g1.reshape(1, D), w_in_bf)


def _norm_rope(t, g, cos, sin):
    y = _rms(t) * g
    lane = lax.broadcasted_iota(jnp.int32, y.shape, 1)
    first_half = (lane % 64) < 32
    partner = jnp.where(first_half, pltpu.roll(y, 96, 1), pltpu.roll(y, 32, 1))
    return y * cos + partner * sin


def _qkprep_kernel(p_ref, cos_ref, sin_ref, qg_ref, kg_ref, q_ref, k_ref, v_ref):
    cos = cos_ref[...]
    sin = sin_ref[...]
    scale = HEAD_DIM ** -0.5 * LOG2E
    for h in range(ATT_Q_HEADS):
        t = p_ref[:, COL_QA + h * HEAD_DIM:COL_QA + (h + 1) * HEAD_DIM]
        q_ref[:, h * HEAD_DIM:(h + 1) * HEAD_DIM] = (_norm_rope(t, qg_ref[...], cos, sin) * scale).astype(BF16)
    for h in range(ATT_KV_HEADS):
        t = p_ref[:, COL_KA + h * HEAD_DIM:COL_KA + (h + 1) * HEAD_DIM]
        k_ref[0, :, h * HEAD_DIM:(h + 1) * HEAD_DIM] = _norm_rope(t, kg_ref[...], cos, sin).astype(BF16)
        v_ref[0, :, 2 * h * HEAD_DIM:(2 * h + 1) * HEAD_DIM] = (
            p_ref[:, COL_VA + h * HEAD_DIM:COL_VA + (h + 1) * HEAD_DIM].astype(BF16))
        v_ref[0, :, (2 * h + 1) * HEAD_DIM:(2 * h + 2) * HEAD_DIM] = jnp.ones((p_ref.shape[0], HEAD_DIM), BF16)


def _qkprep(proj, cos_t, sin_t, qg, kg, B, S):
    M = proj.shape[0]
    nbs = S // TM
    nlat = B * nbs

    def b_of(i):
        return jnp.where(i < nlat, i // nbs, i - nlat)

    def pos_of(i):
        return jnp.where(i < nlat, i % nbs, nbs)

    kl = S + TM
    return pl.pallas_call(
        _qkprep_kernel,
        grid=(M // TM,),
        in_specs=[
            pl.BlockSpec((TM, COL_UB), lambda i: (i, 0)),
            pl.BlockSpec((TM, HEAD_DIM), lambda i: (pos_of(i), 0)),
            pl.BlockSpec((TM, HEAD_DIM), lambda i: (pos_of(i), 0)),
            pl.BlockSpec((1, HEAD_DIM), lambda i: (0, 0)),
            pl.BlockSpec((1, HEAD_DIM), lambda i: (0, 0)),
        ],
        out_specs=[
            pl.BlockSpec((TM, ATT_Q_W), lambda i: (i, 0)),
            pl.BlockSpec((1, TM, ATT_KV_W), lambda i: (b_of(i), pos_of(i), 0)),
            pl.BlockSpec((1, TM, 2 * ATT_KV_W), lambda i: (b_of(i), pos_of(i), 0)),
        ],
        out_shape=[
            jax.ShapeDtypeStruct((M, ATT_Q_W), BF16),
            jax.ShapeDtypeStruct((B, kl, ATT_KV_W), BF16),
            jax.ShapeDtypeStruct((B, kl, 2 * ATT_KV_W), BF16),
        ],
        compiler_params=_cparams(("parallel",), VMEM_LIMIT),
        name="qkprep",
    )(proj, cos_t, sin_t, qg.reshape(1, HEAD_DIM), kg.reshape(1, HEAD_DIM))


def _gqa_kernel(q_ref, k_ref, v_ref, o_in_ref, o_ref, q2_ref, sa_ref, sb_ref, m_ref, acc_ref, *, tk):
    del o_in_ref
    tq = q_ref.shape[0]
    q2_ref[0:tq, :] = q_ref[:, :HEAD_DIM]
    q2_ref[tq:, :] = q_ref[:, HEAD_DIM:]
    nk = k_ref.shape[1] // tk
    m_ref[...] = jnp.full(m_ref.shape, -jnp.inf, F32)
    acc_ref[...] = jnp.zeros(acc_ref.shape, F32)

    def scores(c, s_ref):
        off = pl.multiple_of(c * tk, tk)
        s_ref[...] = lax.dot_general(q2_ref[...], k_ref[0, pl.ds(off, tk), :], (((1,), (1,)), ((), ())),
                                     preferred_element_type=F32)

    def update(c, s_ref):
        off = pl.multiple_of(c * tk, tk)
        vc = v_ref[0, pl.ds(off, tk), :]
        for hh in range(2):
            rows = slice(hh * tq, (hh + 1) * tq)
            m = m_ref[rows, :]
            m_new = jnp.maximum(m, jnp.max(s_ref[rows, :], axis=-1, keepdims=True))
            m_ref[rows, :] = m_new
            p = jnp.exp2((s_ref[rows, :] - jnp.concatenate([m_new] * (tk // HEAD_DIM), axis=1)).astype(BF16))
            alpha = jnp.exp2(m - m_new)
            acc_ref[rows, :] = (jnp.concatenate([alpha, alpha], axis=1) * acc_ref[rows, :]
                                + jnp.dot(p, vc, preferred_element_type=F32))

    scores(0, sa_ref)
    n_pairs = (nk - 1) // 2

    def body(j, carry):
        scores(2 * j + 1, sb_ref)
        update(2 * j, sa_ref)
        scores(2 * j + 2, sa_ref)
        update(2 * j + 1, sb_ref)
        return carry

    lax.fori_loop(0, n_pairs, body, 0)
    if nk % 2 == 0:
        scores(nk - 1, sb_ref)
        update(nk - 2, sa_ref)
        update(nk - 1, sb_ref)
    else:
        update(nk - 1, sa_ref)
    acc = acc_ref[...]
    o = acc[:, :HEAD_DIM] / acc[:, HEAD_DIM:]
    o_ref[:, :HEAD_DIM] = o[:tq].astype(BF16)
    o_ref[:, HEAD_DIM:] = o[tq:].astype(BF16)


def _gqa_latent(q, k_all, v_all, B, S):
    M = q.shape[0]
    kl = k_all.shape[1]
    tq = 512 if S % 512 == 0 else TM
    nbs = S // tq
    tk = 768 if kl % 768 == 0 else TM
    g = ATT_Q_HEADS // ATT_KV_HEADS
    return pl.pallas_call(
        functools.partial(_gqa_kernel, tk=tk),
        grid=(B, ATT_KV_HEADS, nbs),
        in_specs=[
            pl.BlockSpec((tq, g * HEAD_DIM), lambda b, h, i: (b * nbs + i, h)),
            pl.BlockSpec((1, kl, HEAD_DIM), lambda b, h, i: (b, 0, h)),
            pl.BlockSpec((1, kl, 2 * HEAD_DIM), lambda b, h, i: (b, 0, h)),
            pl.BlockSpec(memory_space=pl.ANY),
        ],
        out_specs=pl.BlockSpec((tq, g * HEAD_DIM), lambda b, h, i: (b * nbs + i, h)),
        out_shape=jax.ShapeDtypeStruct((M, ATT_Q_W), BF16),
        input_output_aliases={3: 0},
        scratch_shapes=[
            pltpu.VMEM((g * tq, HEAD_DIM), BF16),
            pltpu.VMEM((g * tq, tk), F32),
            pltpu.VMEM((g * tq, tk), F32),
            pltpu.VMEM((g * tq, HEAD_DIM), F32),
            pltpu.VMEM((g * tq, 2 * HEAD_DIM), F32),
        ],
        compiler_params=_cparams(("parallel", "parallel", "parallel"), VMEM_LIMIT),
        name="gqa_latent",
    )(q, k_all, v_all, jnp.zeros((M, ATT_Q_W), BF16))


def _ctx_attn_kernel(q_ref, k_ref, v_ref, o_in_ref, o_ref, *, scale):
    del o_in_ref
    if scale is None:
        q = q_ref[...]
    else:
        q = (q_ref[...] * (scale * LOG2E)).astype(BF16)
    k = k_ref[...].astype(BF16)
    s = lax.dot_general(q, k, (((1,), (1,)), ((), ())), preferred_element_type=F32)
    p = jnp.exp2(s - jnp.max(s, axis=-1, keepdims=True))
    l = jnp.sum(p, axis=-1, keepdims=True)
    pv = jnp.dot(p.astype(BF16), v_ref[...].astype(BF16), preferred_element_type=F32)
    o_ref[...] = (pv / l).astype(BF16)


def _ctx_attn(q_arr, q_map, k_arr, k_spec, v_arr, v_spec, o_arr, B, n_heads, ctx_blk0, scale):
    return pl.pallas_call(
        functools.partial(_ctx_attn_kernel, scale=scale),
        grid=(B, n_heads),
        in_specs=[
            pl.BlockSpec((TM, HEAD_DIM), q_map),
            k_spec,
            v_spec,
            pl.BlockSpec(memory_space=pl.ANY),
        ],
        out_specs=pl.BlockSpec((TM, HEAD_DIM), lambda b, h: (ctx_blk0 + b, h)),
        out_shape=jax.ShapeDtypeStruct(o_arr.shape, o_arr.dtype),
        input_output_aliases={3: 0},
        compiler_params=_cparams(("parallel", "parallel"), VMEM_LIMIT),
        name="ctx_attn",
    )(q_arr, k_arr, v_arr, o_arr)


def _lru_kernel(u_ref, up_ref, un_ref, *rest, reverse, final):
    if final:
        (g_ref, hrev_ref, cw_ref, cb_ref, wr_ref, br_ref, wi_ref, bi_ref, sp_ref,
         o_ref, xpad, a_scr, b_scr, h_scr, hst) = rest
    else:
        (cw_ref, cb_ref, wr_ref, br_ref, wi_ref, bi_ref, sp_ref,
         o_ref, xpad, a_scr, b_scr, h_scr, hst) = rest
    s = pl.program_id(1)
    ns = pl.num_programs(1)
    T = u_ref.shape[0]
    W = u_ref.shape[1]

    @pl.when(s == 0)
    def _():
        hst[...] = jnp.zeros_like(hst)

    if reverse:
        j = ns - 1 - s
    else:
        j = s - 1
    is_first = jnp.logical_or(s == 0, j == 0)
    is_last = jnp.logical_or(s == 0, j == ns - 2)
    prev_rows = jnp.where(is_first, 0.0, up_ref[...])
    next_rows = jnp.where(is_last, 0.0, un_ref[...])
    xpad[0:8, :] = prev_rows
    xpad[8:8 + T, :] = u_ref[...]
    xpad[8 + T:16 + T, :] = next_rows
    u = (cb_ref[...]
         + xpad[7:7 + T, :] * cw_ref[0:1, :]
         + xpad[8:8 + T, :] * cw_ref[1:2, :]
         + xpad[9:9 + T, :] * cw_ref[2:3, :]
         + xpad[10:10 + T, :] * cw_ref[3:4, :])

    ub = u.astype(BF16)
    nb = LRU_WIDTH // LRU_BLOCKS
    for n in range(LRU_BLOCKS):
        sl = slice(n * nb, (n + 1) * nb)
        un_ = ub[:, sl]
        r = jax.nn.sigmoid(jnp.dot(un_, wr_ref[n], preferred_element_type=F32) + br_ref[:, sl])
        ig = jax.nn.sigmoid(jnp.dot(un_, wi_ref[n], preferred_element_type=F32) + bi_ref[:, sl])
        log_a = (-LRU_C) * r * sp_ref[:, sl]
        a = jnp.exp(log_a)
        a_scr[:, sl] = a
        one_minus_a2 = -jnp.tanh(log_a) * (a * a + 1.0)
        b_scr[:, sl] = jnp.sqrt(one_minus_a2) * (ig * u[:, sl])

    def step(t, h):
        row = (T - 1 - t) if reverse else t
        h = a_scr[pl.ds(row, 1), :] * h + b_scr[pl.ds(row, 1), :]
        h_scr[pl.ds(row, 1), :] = h
        return h

    hst[...] = lax.fori_loop(0, T, step, hst[...], unroll=8)

    if final:
        y = h_scr[...] + hrev_ref[...]
        o_ref[...] = (y * jax.nn.gelu(g_ref[...])).astype(o_ref.dtype)
    else:
        o_ref[...] = h_scr[...]
    del W


def _lru_dir(proj, hrev, conv_w, conv_b, wr, br, wi, bi, sp, B, S, reverse):
    M = proj.shape[0]
    nbs = S // TM
    nlat = B * nbs
    final = hrev is not None
    ub_blk = COL_UB // LRU_WIDTH
    gb_blk = COL_GB // LRU_WIDTH
    r8 = TM // 8
    n8 = M // 8

    def blk(b, s):
        j = (nbs - s) if reverse else (s - 1)
        return jnp.where(s == 0, nlat + b, b * nbs + j)

    in_specs = [
        pl.BlockSpec((TM, LRU_WIDTH), lambda b, s: (blk(b, s), ub_blk)),
        pl.BlockSpec((8, LRU_WIDTH), lambda b, s: (jnp.maximum(blk(b, s) * r8 - 1, 0), ub_blk)),
        pl.BlockSpec((8, LRU_WIDTH), lambda b, s: (jnp.minimum((blk(b, s) + 1) * r8, n8 - 1), ub_blk)),
    ]
    args = [proj, proj, proj]
    if final:
        in_specs += [
            pl.BlockSpec((TM, LRU_WIDTH), lambda b, s: (blk(b, s), gb_blk)),
            pl.BlockSpec((TM, LRU_WIDTH), lambda b, s: (blk(b, s), 0)),
        ]
        args += [proj, hrev]
    const2 = lambda b, s: (0, 0)
    in_specs += [
        pl.BlockSpec((CONV_W, LRU_WIDTH), const2),
        pl.BlockSpec((1, LRU_WIDTH), const2),
        pl.BlockSpec((LRU_BLOCKS, LRU_WIDTH // LRU_BLOCKS, LRU_WIDTH // LRU_BLOCKS), lambda b, s: (0, 0, 0)),
        pl.BlockSpec((1, LRU_WIDTH), const2),
        pl.BlockSpec((LRU_BLOCKS, LRU_WIDTH // LRU_BLOCKS, LRU_WIDTH // LRU_BLOCKS), lambda b, s: (0, 0, 0)),
        pl.BlockSpec((1, LRU_WIDTH), const2),
        pl.BlockSpec((1, LRU_WIDTH), const2),
    ]
    args += [conv_w, conv_b.reshape(1, -1), wr.astype(BF16), br.reshape(1, -1), wi.astype(BF16),
             bi.reshape(1, -1), sp.reshape(1, -1)]
    return pl.pallas_call(
        functools.partial(_lru_kernel, reverse=reverse, final=final),
        grid=(B, nbs + 1),
        in_specs=in_specs,
        out_specs=pl.BlockSpec((TM, LRU_WIDTH), lambda b, s: (blk(b, s), 0)),
        out_shape=jax.ShapeDtypeStruct((M, LRU_WIDTH), BF16 if final else F32),
        scratch_shapes=[
            pltpu.VMEM((TM + 16, LRU_WIDTH), F32),
            pltpu.VMEM((TM, LRU_WIDTH), F32),
            pltpu.VMEM((TM, LRU_WIDTH), F32),
            pltpu.VMEM((TM, LRU_WIDTH), F32),
            pltpu.VMEM((1, LRU_WIDTH), F32),
        ],
        compiler_params=_cparams(("parallel", "arbitrary"), VMEM_LIMIT),
        name="lru_fwd" if final else "lru_rev",
    )(*args)


def _na_kernel(q_ref, k0_ref, k1_ref, k2_ref, v0_ref, v1_ref, v2_ref, kc_ref, vc_ref, tl_ref, tr_ref,
               o_in_ref, o_ref, s_scr, *, n_rows):
    del o_in_ref
    i = pl.program_id(1)
    nbs = n_rows // NA_QROWS
    r0 = NA_QROWS * i
    ws = NA_QROWS * jnp.clip(i - 1, 0, nbs - 3)
    scale = HEAD_DIM ** -0.5 * LOG2E
    nkeys = s_scr.shape[1]
    ones = jnp.ones((nkeys, HEAD_DIM), BF16)
    for h in range(NA_HEADS):
        hs = slice(h * HEAD_DIM, (h + 1) * HEAD_DIM)
        qh = (q_ref[:, hs] * scale).astype(BF16)
        kh = jnp.concatenate([k0_ref[:, hs], k1_ref[:, hs], k2_ref[:, hs], kc_ref[:, hs]], axis=0).astype(BF16)
        vh = jnp.concatenate([v0_ref[:, hs], v1_ref[:, hs], v2_ref[:, hs], vc_ref[:, hs]], axis=0).astype(BF16)
        vh = jnp.concatenate([vh, ones], axis=1)
        s_scr[...] = lax.dot_general(qh, kh, (((1,), (1,)), ((), ())), preferred_element_type=F32)
        for qr in range(NA_QROWS):
            r = r0 + qr
            rs = jnp.clip(r - NA_ROWS // 2, 0, n_rows - NA_ROWS)

            def tile_of(a):
                kr = ws + a
                valid = jnp.logical_and(kr >= rs, kr < rs + NA_ROWS)
                return jnp.where(valid, kr - r + (NA_ROWS - 1), 2 * NA_ROWS - 1)

            for pr in range(NA_WROWS // 2):
                bias = tl_ref[h, tile_of(2 * pr)] + tr_ref[h, tile_of(2 * pr + 1)]
                rsl = slice(qr * GRID_W, (qr + 1) * GRID_W)
                csl = slice(pr * 2 * GRID_W, (pr + 1) * 2 * GRID_W)
                s_scr[rsl, csl] = s_scr[rsl, csl] + bias
        s = s_scr[...]
        p = jnp.exp2((s - jnp.max(s, axis=-1, keepdims=True)).astype(BF16))
        pv = jnp.dot(p, vh, preferred_element_type=F32)
        o_ref[:, hs] = (pv[:, :HEAD_DIM] / pv[:, HEAD_DIM:]).astype(BF16)


def _na_latent(proj, tl, tr, B, S):
    M = proj.shape[0]
    nbs = S // TM
    nlat = B * nbs
    n_rows = S // GRID_W
    qb, kb, vb = COL_QN // NA_W, COL_KN // NA_W, COL_VN // NA_W

    def wblk(b, i, t):
        return b * nbs + jnp.clip(i - 1, 0, nbs - 3) + t

    blk = (TM, NA_W)
    in_specs = [pl.BlockSpec(blk, lambda b, i: (b * nbs + i, qb))]
    in_specs += [pl.BlockSpec(blk, functools.partial(lambda b, i, t: (wblk(b, i, t), kb), t=t)) for t in range(3)]
    in_specs += [pl.BlockSpec(blk, functools.partial(lambda b, i, t: (wblk(b, i, t), vb), t=t)) for t in range(3)]
    in_specs += [
        pl.BlockSpec(blk, lambda b, i: (nlat + b, kb)),
        pl.BlockSpec(blk, lambda b, i: (nlat + b, vb)),
        pl.BlockSpec(tl.shape, lambda b, i: (0, 0, 0, 0)),
        pl.BlockSpec(tr.shape, lambda b, i: (0, 0, 0, 0)),
        pl.BlockSpec(memory_space=pl.ANY),
    ]
    return pl.pallas_call(
        functools.partial(_na_kernel, n_rows=n_rows),
        grid=(B, nbs),
        in_specs=in_specs,
        out_specs=pl.BlockSpec(blk, lambda b, i: (b * nbs + i, 0)),
        out_shape=jax.ShapeDtypeStruct((M, NA_W), BF16),
        input_output_aliases={11: 0},
        scratch_shapes=[pltpu.VMEM((TM, NA_WROWS * GRID_W + TM), F32)],
        compiler_params=_cparams(("parallel", "parallel"), VMEM_LIMIT),
        name="na_latent",
    )(proj, proj, proj, proj, proj, proj, proj, proj, proj, tl, tr, jnp.zeros((M, NA_W), BF16))


def _na_bias_tables(rpb):
    j = np.arange(GRID_W)
    cs = np.clip(j - NA_COLS // 2, 0, GRID_W - NA_COLS)
    kc = np.arange(GRID_W)
    inside = (kc[None, :] >= cs[:, None]) & (kc[None, :] < cs[:, None] + NA_COLS)
    dc = kc[None, :] - j[:, None] + (NA_COLS - 1)
    sel = (dc[:, :, None] == np.arange(2 * NA_COLS - 1)[None, None, :]) & inside[:, :, None]
    t = jnp.einsum("hrd,jkd->hrjk", rpb * LOG2E, jnp.asarray(sel, F32), precision=lax.Precision.HIGHEST)
    t = jnp.where(inside[None, None], t, NEG_BIAS)
    t = jnp.concatenate([t, jnp.full_like(t[:, :1], NEG_BIAS)], axis=1)
    z = jnp.zeros_like(t)
    return jnp.concatenate([t, z], axis=-1), jnp.concatenate([z, t], axis=-1)


def _outproj_kernel(*refs, n_first, two_sources):
    if two_sources:
        (oa_ref, ob_ref, oc_ref, x_ref, xb_ref, mod_ref, g_ref, w_ref, wr_ref, br_ref,
         xo_ref, h2_ref, rt_ref, cnt_ref, carry) = refs
    else:
        (oa_ref, ob_ref, oc_ref, x_ref, mod_ref, g_ref, w_ref, wr_ref, br_ref,
         xo_ref, h2_ref, rt_ref, cnt_ref, carry) = refs
    i = pl.program_id(0)

    @pl.when(i == 0)
    def _():
        carry[...] = jnp.zeros_like(carry)

    k1 = ATT_Q_W
    k2 = k1 + LRU_WIDTH
    acc = jnp.dot(oa_ref[...], w_ref[0:k1, :], preferred_element_type=F32)
    acc = acc + jnp.dot(ob_ref[...], w_ref[k1:k2, :], preferred_element_type=F32)
    acc = acc + jnp.dot(oc_ref[...], w_ref[k2:, :], preferred_element_type=F32)
    x_in = x_ref[...]
    if two_sources:
        x_in = jnp.where(i < n_first, x_in, xb_ref[...])
    xn = x_in + mod_ref[0, 2:3, :] * acc
    xo_ref[...] = xn
    h2 = _rms(xn) * g_ref[...]
    h2 = h2 * (1.0 + mod_ref[0, 4:5, :]) + mod_ref[0, 3:4, :]
    h2_ref[...] = _pack_bf16_pair(h2)
    logits = jnp.dot(h2.astype(BF16), wr_ref[...], preferred_element_type=F32) + br_ref[...]
    route = _route_math(logits)
    rt_ref[...] = _rank_math(route, carry)
    cnt_ref[...] = jnp.broadcast_to(carry[...], cnt_ref.shape)


def _pack_bf16_pair(x):
    w = x.shape[1] // 2
    lo = pltpu.bitcast(x[:, :w].astype(BF16).astype(F32), jnp.uint32)
    hi = pltpu.bitcast(x[:, w:].astype(BF16).astype(F32), jnp.uint32)
    return hi | (lo >> 16)


def _unpack_bf16_pair(words):
    lo = pltpu.bitcast(words << 16, F32)
    hi = pltpu.bitcast(words & jnp.uint32(0xFFFF0000), F32)
    return jnp.concatenate([lo, hi], axis=1).astype(BF16)


def _outproj(oa, ob, oc, x_rows, ctx_rows, mod, g2, w_out_bf, w_router, b_router, B, S, n_rows):
    M = oa.shape[0]
    D = x_rows.shape[1]
    two = ctx_rows is not None
    tm = 512 if (S % 512 == 0 and (M - B * S) % 512 == 0) else TM
    per_seq = S // tm
    n_first = B * per_seq
    packed = jax.eval_shape(_pack_bf16_pair, jax.ShapeDtypeStruct((tm, D), F32))
    if two:
        x_specs = [pl.BlockSpec((tm, D), lambda i: (jnp.minimum(i, n_first - 1), 0)),
                   pl.BlockSpec((tm, D), lambda i: (jnp.maximum(i - n_first, 0), 0))]
        x_args = [x_rows, ctx_rows]
        alias = {}
    else:
        x_specs = [pl.BlockSpec((tm, D), lambda i: (i, 0))]
        x_args = [x_rows]
        alias = {3: 0}
    return pl.pallas_call(
        functools.partial(_outproj_kernel, n_first=n_first, two_sources=two),
        grid=(n_rows // tm,),
        in_specs=[
            pl.BlockSpec((tm, ATT_Q_W), lambda i: (i, 0)),
            pl.BlockSpec((tm, LRU_WIDTH), lambda i: (i, 0)),
            pl.BlockSpec((tm, NA_W), lambda i: (i, 0)),
            *x_specs,
            pl.BlockSpec((1, 6, D), lambda i: (jnp.minimum(i // per_seq, B), 0, 0)),
            pl.BlockSpec((1, D), lambda i: (0, 0)),
            pl.BlockSpec(w_out_bf.shape, lambda i: (0, 0)),
            pl.BlockSpec(w_router.shape, lambda i: (0, 0)),
            pl.BlockSpec((1, 128), lambda i: (0, 0)),
        ],
        out_specs=[
            pl.BlockSpec((tm, D), lambda i: (i, 0)),
            pl.BlockSpec(packed.shape, lambda i: (i, 0)),
            pl.BlockSpec((tm, 128), lambda i: (i, 0)),
            pl.BlockSpec((8, 128), lambda i: (0, 0)),
        ],
        out_shape=[
            jax.ShapeDtypeStruct((M, D), F32),
            jax.ShapeDtypeStruct((n_rows, packed.shape[1]), packed.dtype),
            jax.ShapeDtypeStruct((n_rows, 128), F32),
            jax.ShapeDtypeStruct((8, 128), F32),
        ],
        input_output_aliases=alias,
        scratch_shapes=[pltpu.VMEM((1, 128), F32)],
        compiler_params=_cparams(("arbitrary",), VMEM_LIMIT),
        name="outproj",
    )(oa, ob, oc, *x_args, mod, g2.reshape(1, D), w_out_bf, w_router, b_router)


def _route_math(x):
    lane = lax.broadcasted_iota(jnp.int32, x.shape, 1)
    big = jnp.int32(1 << 20)
    ninf = -jnp.inf

    def first_argmax(vals):
        m = jnp.max(vals, axis=-1, keepdims=True)
        idx = jnp.min(jnp.where(vals == m, lane, big), axis=-1, keepdims=True)
        return m, idx

    lg = jnp.where(lane < MOE_GROUPS, x, ninf)
    mg, g_star = first_argmax(lg)
    g_gate = 1.0 / jnp.sum(jnp.exp(lg - mg), axis=-1, keepdims=True)
    lo = MOE_GROUPS + MOE_EXPERTS_PER_GROUP * g_star
    le = jnp.where(jnp.logical_and(lane >= lo, lane < lo + MOE_EXPERTS_PER_GROUP), x, ninf)
    v1, i1 = first_argmax(le)
    v2, i2 = first_argmax(jnp.where(lane == i1, ninf, le))
    e21 = jnp.exp(v2 - v1)
    w1 = g_gate / (1.0 + e21)
    w2 = g_gate * e21 / (1.0 + e21)
    e1 = (i1 - MOE_GROUPS).astype(F32)
    e2 = (i2 - MOE_GROUPS).astype(F32)
    return jnp.where(lane == 0, e1, jnp.where(lane == 1, e2, jnp.where(lane == 2, w1,
                     jnp.where(lane == 3, w2, 0.0))))


def _rank_math(r, carry):
    lane = lax.broadcasted_iota(jnp.int32, r.shape, 1).astype(F32)
    oh0 = jnp.where(lane == r[:, 0:1], 1.0, 0.0)
    oh1 = jnp.where(lane == r[:, 1:2], 1.0, 0.0)
    oh = oh0 + oh1
    n = r.shape[0]
    row = lax.broadcasted_iota(jnp.int32, (n, n), 0)
    col = lax.broadcasted_iota(jnp.int32, (n, n), 1)
    tri = jnp.where(col < row, 1.0, 0.0).astype(BF16)
    tot = carry[...] + jnp.dot(tri, oh.astype(BF16), preferred_element_type=F32)
    rank0 = jnp.sum(oh0 * tot, axis=-1, keepdims=True)
    rank1 = jnp.sum(oh1 * tot, axis=-1, keepdims=True)
    lane_i = lax.broadcasted_iota(jnp.int32, r.shape, 1)
    carry[...] = carry[...] + jnp.sum(oh, axis=0, keepdims=True)
    return jnp.where(lane_i == 4, rank0, jnp.where(lane_i == 5, rank1, r))


def _dispatch_kernel(dest_ref, pad_ref, h_ref, xs_hbm, zeros, sem, zsem, *, n_tok):
    i = pl.program_id(0)

    @pl.when(i == 0)
    def _():
        zeros[...] = jnp.zeros_like(zeros)
        pieces = [1 << b for b in range(MOE_BM.bit_length() - 2, 2, -1)]

        def copies(e):
            start, n = pad_ref[0, e], pad_ref[1, e]
            head = jnp.minimum((-start) & 7, n)
            out = []
            for r in range(7):
                out.append((r < head, pltpu.make_async_copy(zeros.at[pl.ds(0, 1)],
                                                            xs_hbm.at[pl.ds(start + r, 1)], zsem)))
            rest = n - head
            for p in pieces:
                at = pl.multiple_of(start + head + (rest & ~(2 * p - 1)), 8)
                out.append(((rest & p) != 0, pltpu.make_async_copy(zeros.at[pl.ds(0, p)],
                                                                   xs_hbm.at[pl.ds(at, p)], zsem)))
            return out

        def fill(e, c):
            for on, cp in copies(e):
                @pl.when(on)
                def _():
                    cp.start()
            return c

        def drain(e, c):
            for on, cp in copies(e):
                @pl.when(on)
                def _():
                    cp.wait()
            return c

        lax.fori_loop(0, MOE_EXPERTS, fill, 0)
        lax.fori_loop(0, MOE_EXPERTS, drain, 0)

        half = zeros.shape[0]
        n_tail = (xs_hbm.shape[0] - pad_ref[2, 0]) // half

        def tail_copy(j):
            at = pl.multiple_of(pad_ref[2, 0] + j * half, half)
            return pltpu.make_async_copy(zeros, xs_hbm.at[pl.ds(at, half)], zsem)

        def tail_fill(j, c):
            tail_copy(j).start()
            return c

        def tail_drain(j, c):
            tail_copy(j).wait()
            return c

        lax.fori_loop(0, n_tail, tail_fill, 0)
        lax.fori_loop(0, n_tail, tail_drain, 0)

    def issue(t, c):
        tok = i * TM + t
        for k in range(2):
            pltpu.make_async_copy(h_ref.at[pl.ds(t, 1)], xs_hbm.at[pl.ds(dest_ref[k * n_tok + tok], 1)], sem).start()
        return c

    lax.fori_loop(0, TM, issue, 0, unroll=8)
    for k in range(2):
        pltpu.make_async_copy(h_ref, xs_hbm.at[pl.ds(0, TM)], sem).wait()


def _dispatch(dest_flat, pad_info, h2, n_rows, nblk):
    W = h2.shape[1]
    return pl.pallas_call(
        functools.partial(_dispatch_kernel, n_tok=nblk * TM),
        grid_spec=pltpu.PrefetchScalarGridSpec(
            num_scalar_prefetch=2,
            grid=(nblk,),
            in_specs=[pl.BlockSpec((TM, W), lambda i, d, p: (i, 0))],
            out_specs=pl.BlockSpec(memory_space=pl.ANY),
            scratch_shapes=[pltpu.VMEM((MOE_BM // 2, W), h2.dtype), pltpu.SemaphoreType.DMA(()),
                            pltpu.SemaphoreType.DMA(())],
        ),
        out_shape=jax.ShapeDtypeStruct((n_rows, W), h2.dtype),
        compiler_params=pltpu.CompilerParams(dimension_semantics=("arbitrary",), has_side_effects=True),
        name="dispatch",
    )(dest_flat, pad_info, h2)


PLAN_EXPERT, PLAN_FIRST, PLAN_SLOT, PLAN_NEXT, PLAN_HALF = 0, 1, 2, 3, 4


def _block_plan(block_e, n_used, pad_starts, counts):
    nb = block_e.shape[0]
    idx = jnp.arange(nb, dtype=jnp.int32)
    is_first = (idx < n_used[0]) & ((idx == 0) | (block_e != jnp.roll(block_e, 1)))
    seg = jnp.cumsum(is_first.astype(jnp.int32)) - 1
    first_pos = jnp.where(is_first, idx, nb)
    at_or_after = jnp.flip(lax.cummin(jnp.flip(first_pos)))
    next_first = jnp.concatenate([at_or_after[1:], jnp.full((1,), nb, jnp.int32)])
    next_e = jnp.where(next_first < nb, block_e[jnp.minimum(next_first, nb - 1)], -1)
    valid = counts[block_e] - (idx * MOE_BM - pad_starts[block_e])
    half_only = (valid <= MOE_BM // 2).astype(jnp.int32)
    return jnp.stack([block_e, is_first.astype(jnp.int32), seg % 2, next_e, half_only]).astype(jnp.int32)


def _expert_weights_step(i, plan_ref, w_hbms, wbuf, wbf, sem, layer):
    def copies(e, slot):
        return [pltpu.make_async_copy(w.at[layer, e], wbuf.at[slot, n], sem.at[slot, n])
                for n, w in enumerate(w_hbms)]

    @pl.when(plan_ref[PLAN_FIRST, i] == 1)
    def _():
        slot = plan_ref[PLAN_SLOT, i]

        @pl.when(i == 0)
        def _():
            for c in copies(plan_ref[PLAN_EXPERT, 0], 0):
                c.start()

        for c in copies(plan_ref[PLAN_EXPERT, i], slot):
            c.wait()
        nrow, ncol = wbuf.shape[2], wbuf.shape[3]
        rows = 256

        def cast_rows(r, c):
            off = pl.multiple_of(r * rows, rows)
            for n in range(len(w_hbms)):
                wbf[pl.ds(off, rows), n * ncol:(n + 1) * ncol] = wbuf[slot, n, pl.ds(off, rows), :].astype(BF16)
            return c

        lax.fori_loop(0, nrow // rows, cast_rows, 0)
        nxt = plan_ref[PLAN_NEXT, i]

        @pl.when(nxt >= 0)
        def _():
            for c in copies(nxt, 1 - slot):
                c.start()


def _moe1_kernel(plan_ref, nu_ref, x_ref, wg_hbm, wu_hbm, o_ref, wbuf, wcat, sem, *, layer):
    i = pl.program_id(0)

    @pl.when(i < nu_ref[0])
    def _():
        _expert_weights_step(i, plan_ref, (wg_hbm, wu_hbm), wbuf, wcat, sem, layer)

        def gated(rows):
            h = jnp.dot(_unpack_bf16_pair(x_ref[rows, :]), wcat[...], preferred_element_type=F32)
            g = h[:, :MOE_HIDDEN]
            u = h[:, MOE_HIDDEN:]
            return (g * jax.nn.sigmoid(g) * u).astype(BF16)

        half = MOE_BM // 2

        @pl.when(plan_ref[PLAN_HALF, i] == 1)
        def _():
            o_ref[0:half, :] = gated(slice(0, half))
            o_ref[half:, :] = jnp.zeros((MOE_BM - half, MOE_HIDDEN), BF16)

        @pl.when(plan_ref[PLAN_HALF, i] == 0)
        def _():
            o_ref[...] = gated(slice(None))

    @pl.when(i >= nu_ref[0])
    def _():
        o_ref[...] = jnp.zeros_like(o_ref)


def _moe1(plan, n_used, xs, w_gate, w_up, layer):
    P, W = xs.shape
    D = w_gate.shape[2]
    nb = P // MOE_BM
    used_blk = lambda i, plan, nu: (jnp.minimum(i, nu[0] - 1), 0)
    return pl.pallas_call(
        functools.partial(_moe1_kernel, layer=layer),
        grid_spec=pltpu.PrefetchScalarGridSpec(
            num_scalar_prefetch=2,
            grid=(nb,),
            in_specs=[pl.BlockSpec((MOE_BM, W), used_blk), pl.BlockSpec(memory_space=pl.ANY),
                      pl.BlockSpec(memory_space=pl.ANY)],
            out_specs=pl.BlockSpec((MOE_BM, MOE_HIDDEN), lambda i, plan, nu: (i, 0)),
            scratch_shapes=[pltpu.VMEM((2, 2, D, MOE_HIDDEN), F32), pltpu.VMEM((D, 2 * MOE_HIDDEN), BF16),
                            pltpu.SemaphoreType.DMA((2, 2))],
        ),
        out_shape=jax.ShapeDtypeStruct((P, MOE_HIDDEN), BF16),
        compiler_params=_cparams(("arbitrary",), VMEM_LIMIT),
        name="moe_up",
    )(plan, n_used, xs, w_gate, w_up)


def _moe2_kernel(plan_ref, nu_ref, h_ref, wd_hbm, o_ref, wbuf, wbf, sem, *, layer):
    i = pl.program_id(0)

    @pl.when(i < nu_ref[0])
    def _():
        _expert_weights_step(i, plan_ref, (wd_hbm,), wbuf, wbf, sem, layer)
        half = MOE_BM // 2

        @pl.when(plan_ref[PLAN_HALF, i] == 1)
        def _():
            o_ref[0:half, :] = jnp.dot(h_ref[0:half, :], wbf[...], preferred_element_type=F32)
            o_ref[half:, :] = jnp.zeros((MOE_BM - half, o_ref.shape[1]), F32)

        @pl.when(plan_ref[PLAN_HALF, i] == 0)
        def _():
            o_ref[...] = jnp.dot(h_ref[...], wbf[...], preferred_element_type=F32)

    @pl.when(i >= nu_ref[0])
    def _():
        o_ref[...] = jnp.zeros_like(o_ref)


def _moe2(plan, n_used, hmid, w_down, layer):
    P = hmid.shape[0]
    D = w_down.shape[3]
    nb = P // MOE_BM
    return pl.pallas_call(
        functools.partial(_moe2_kernel, layer=layer),
        grid_spec=pltpu.PrefetchScalarGridSpec(
            num_scalar_prefetch=2,
            grid=(nb,),
            in_specs=[
                pl.BlockSpec((MOE_BM, MOE_HIDDEN), lambda i, plan, nu: (i, 0)),
                pl.BlockSpec(memory_space=pl.ANY),
            ],
            out_specs=pl.BlockSpec((MOE_BM, D), lambda i, plan, nu: (i, 0)),
            scratch_shapes=[pltpu.VMEM((2, 1, MOE_HIDDEN, D), F32), pltpu.VMEM((MOE_HIDDEN, D), BF16),
                            pltpu.SemaphoreType.DMA((2, 1))],
        ),
        out_shape=jax.ShapeDtypeStruct((P, D), F32),
        compiler_params=_cparams(("arbitrary",), VMEM_LIMIT),
        name="moe_down",
    )(plan, n_used, hmid, w_down)


def _combine_kernel(dest_ref, yb_hbm, x_ref, r_ref, mod_ref, fg_ref, o_ref, ybuf, sem, *, final):
    i = pl.program_id(0)
    n = pl.num_programs(0)

    n_tok = n * TM

    def row_copy(tok, k, t, slot):
        return pltpu.make_async_copy(yb_hbm.at[pl.ds(dest_ref[k * n_tok + tok], 1)],
                                     ybuf.at[slot, k, pl.ds(t, 1)], sem.at[slot])

    def issue_rows(blk, slot, t0):
        for r in range(8):
            for k in range(2):
                row_copy(blk * TM + t0 + r, k, t0 + r, slot).start()

    def wait_slot(slot):
        for k in range(2):
            pltpu.make_async_copy(yb_hbm.at[pl.ds(0, TM)], ybuf.at[slot, k], sem.at[slot]).wait()

    @pl.when(i == 0)
    def _():
        def first(j, c):
            issue_rows(0, 0, pl.multiple_of(j * 8, 8))
            return c
        lax.fori_loop(0, TM // 8, first, 0)

    def step(cur):
        nxt = 1 - cur
        wait_slot(cur)
        nb = jnp.minimum(i + 1, n - 1)
        gate = mod_ref[0, 5:6, :]

        def combine_rows(rows):
            r = r_ref[rows, :]
            y = r[:, 2:3] * ybuf[cur, 0, rows, :] + r[:, 3:4] * ybuf[cur, 1, rows, :]
            xn = x_ref[rows, :] + gate * y
            if final:
                xn = _rms(xn) * fg_ref[...]
            o_ref[rows, :] = xn

        def body(j, c):
            t0 = pl.multiple_of(j * 8, 8)
            issue_rows(nb, nxt, t0)
            if not final:
                combine_rows(pl.ds(t0, 8))
            return c

        lax.fori_loop(0, TM // 8, body, 0)
        if final:
            combine_rows(slice(None))

        @pl.when(i == n - 1)
        def _():
            wait_slot(nxt)

    @pl.when(i % 2 == 0)
    def _():
        step(0)

    @pl.when(i % 2 == 1)
    def _():
        step(1)


def _combine(dest_flat, yb, x_all, route, mod, final_g, seg_of, nblk, final):
    M, D = x_all.shape
    out_rows = nblk * TM if final else M
    kwargs = {} if final else {"input_output_aliases": {2: 0}}
    return pl.pallas_call(
        functools.partial(_combine_kernel, final=final),
        grid_spec=pltpu.PrefetchScalarGridSpec(
            num_scalar_prefetch=1,
            grid=(nblk,),
            in_specs=[
                pl.BlockSpec(memory_space=pl.ANY),
                pl.BlockSpec((TM, D), lambda i, d: (i, 0)),
                pl.BlockSpec((TM, 128), lambda i, d: (i, 0)),
                pl.BlockSpec((1, 6, D), lambda i, d: (seg_of(i), 0, 0)),
                pl.BlockSpec((1, D), lambda i, d: (0, 0)),
            ],
            out_specs=pl.BlockSpec((TM, D), lambda i, d: (i, 0)),
            scratch_shapes=[pltpu.VMEM((2, 2, TM, D), F32), pltpu.SemaphoreType.DMA((2,))],
        ),
        out_shape=jax.ShapeDtypeStruct((out_rows, D), F32),
        compiler_params=_cparams(("arbitrary",), VMEM_LIMIT),
        name="combine",
        **kwargs,
    )(dest_flat, yb, x_all, route, mod, final_g.reshape(1, D))


def _moe_layer(x_all, h2, route, cnt, mod, final_g, w_gate, w_up, w_down, layer, seg_of, nblk, final):
    T = nblk * TM
    e_ids = route[:, 0:2].T.astype(jnp.int32)
    ranks = route[:, 4:6].T.astype(jnp.int32)
    counts = cnt[0, :MOE_EXPERTS].astype(jnp.int32)
    padded = (counts + MOE_BM - 1) // MOE_BM * MOE_BM
    pad_ends = jnp.cumsum(padded)
    pad_starts = pad_ends - padded
    expert = jnp.arange(MOE_EXPERTS, dtype=jnp.int32)[:, None, None]
    start_of = jnp.sum(jnp.where(e_ids[None] == expert, pad_starts[:, None, None], 0), axis=0)
    dest = (start_of + ranks).reshape(-1)
    pad_info = jnp.stack([pad_starts + counts, padded - counts,
                          jnp.broadcast_to(pad_ends[-1], (MOE_EXPERTS,))]).astype(jnp.int32)
    n_blocks = -(-(2 * T + MOE_EXPERTS * (MOE_BM - 1)) // MOE_BM)
    block_starts = jnp.arange(n_blocks, dtype=jnp.int32) * MOE_BM
    block_e = jnp.sum((block_starts[:, None] >= pad_ends[None, :]).astype(jnp.int32), axis=1)
    block_e = jnp.minimum(block_e, MOE_EXPERTS - 1)
    n_used = (pad_ends[-1:] // MOE_BM).astype(jnp.int32)
    plan = _block_plan(block_e, n_used, pad_starts, counts)
    xs = _dispatch(dest, pad_info, h2, n_blocks * MOE_BM, nblk)
    hmid = _moe1(plan, n_used, xs, w_gate, w_up, layer)
    yb = _moe2(plan, n_used, hmid, w_down, layer)
    return _combine(dest, yb, x_all, route, mod, final_g, seg_of, nblk, final)


def _rope_tables(S):
    pos = jnp.arange(S, dtype=jnp.int32)
    rows = (pos // GRID_W).astype(F32)
    cols = (pos % GRID_W).astype(F32)
    n_freq = HEAD_DIM // 4
    inv = 1.0 / (ROPE_THETA ** (jnp.arange(n_freq, dtype=F32) / n_freq))
    cr, sr = jnp.cos(rows[:, None] * inv), jnp.sin(rows[:, None] * inv)
    cc, sc = jnp.cos(cols[:, None] * inv), jnp.sin(cols[:, None] * inv)
    cos_t = jnp.concatenate([cr, cr, cc, cc], axis=1)
    sin_t = jnp.concatenate([-sr, sr, -sc, sc], axis=1)
    cos_t = jnp.concatenate([cos_t, jnp.ones((TM, HEAD_DIM), F32)], axis=0)
    sin_t = jnp.concatenate([sin_t, jnp.zeros((TM, HEAD_DIM), F32)], axis=0)
    return cos_t, sin_t


def kernel(x, c, ctx, c_ctx, ada_w, ada_b, norm1_g, norm2_g, w_in, w_out, att_q_norm, att_k_norm, conv_w, conv_b, lru_wr, lru_br, lru_wi, lru_bi, lru_lambda, na_rpb, router_wg, router_bg, router_we, router_be, moe_w_gate, moe_w_up, moe_w_down, final_g):
    B, S, D = x.shape
    C = ctx.shape[1]
    L = ada_w.shape[0]
    assert C == TM and S % TM == 0 and S // TM >= 3 and B + 1 <= 8
    nbs = S // TM
    nlat = B * nbs
    m_lat = B * S
    M = m_lat + B * C

    def seg_of(i):
        return jnp.minimum(i // nbs, B)

    x_rows, ctx_rows = x.reshape(m_lat, D), ctx.reshape(B * C, D)
    cvec = jnp.zeros((8, D), F32).at[:B].set(c).at[B].set(c_ctx)
    mod_all = _adaln(cvec, ada_w, ada_b)[:, :B + 1].reshape(L, B + 1, 6, D)
    cos_t, sin_t = _rope_tables(S)
    g_att = ATT_Q_HEADS // ATT_KV_HEADS

    out = None
    for l in range(L):
        last = l == L - 1
        mod = mod_all[l]
        proj = _inproj(x_rows, ctx_rows, mod, norm1_g[l], w_in[l].astype(BF16), B, S)

        q, k_all, v_all = _qkprep(proj, cos_t, sin_t, att_q_norm[l], att_k_norm[l], B, S)
        oa = _gqa_latent(q, k_all, v_all, B, S)

        sp = jax.nn.softplus(-lru_lambda[l].astype(F32))
        hrev = _lru_dir(proj, None, conv_w[l], conv_b[l], lru_wr[l, 1], lru_br[l, 1], lru_wi[l, 1],
                        lru_bi[l, 1], sp[1], B, S, reverse=True)
        ob = _lru_dir(proj, hrev, conv_w[l], conv_b[l], lru_wr[l, 0], lru_br[l, 0], lru_wi[l, 0],
                      lru_bi[l, 0], sp[0], B, S, reverse=False)

        tl, tr = _na_bias_tables(na_rpb[l])
        oc = _na_latent(proj, tl, tr, B, S)

        if not last:
            oa = _ctx_attn(
                q, lambda b, h: (nlat + b, h),
                k_all, pl.BlockSpec((None, TM, HEAD_DIM), lambda b, h: (b, nbs, h // g_att)),
                v_all, pl.BlockSpec((None, TM, HEAD_DIM), lambda b, h: (b, nbs, 2 * (h // g_att))),
                oa, B, ATT_Q_HEADS, nlat, None)
            oc = _ctx_attn(
                proj, lambda b, h: (nlat + b, COL_QN // HEAD_DIM + h),
                proj, pl.BlockSpec((TM, HEAD_DIM), lambda b, h: (nlat + b, COL_KN // HEAD_DIM + h)),
                proj, pl.BlockSpec((TM, HEAD_DIM), lambda b, h: (nlat + b, COL_VN // HEAD_DIM + h)),
                oc, B, NA_HEADS, nlat, HEAD_DIM ** -0.5)

        nblk = nlat if last else M // TM
        w_router = jnp.zeros((D, 128), F32).at[:, :MOE_GROUPS].set(router_wg[l])
        w_router = w_router.at[:, MOE_GROUPS:MOE_GROUPS + MOE_EXPERTS].set(router_we[l])
        b_router = jnp.zeros((1, 128), F32).at[0, :MOE_GROUPS].set(router_bg[l])
        b_router = b_router.at[0, MOE_GROUPS:MOE_GROUPS + MOE_EXPERTS].set(router_be[l])
        x_all, h2, route, cnt = _outproj(oa, ob, oc, x_rows, ctx_rows, mod, norm2_g[l], w_out[l].astype(BF16),
                                         w_router.astype(BF16), b_router, B, S, nblk * TM)
        res = _moe_layer(x_all, h2, route, cnt, mod, final_g, moe_w_gate, moe_w_up, moe_w_down, l,
                         seg_of, nblk, last)
        if last:
            out = res
        else:
            x_rows, ctx_rows = res, None
    return out.reshape(B, S, D)
```

```python
import functools

import numpy as np
import jax
import jax.numpy as jnp
from jax import lax
from jax.experimental import pallas as pl
from jax.experimental.pallas import tpu as pltpu

F32 = jnp.float32
BF16 = jnp.bfloat16

GRID_W = 64
HEAD_DIM = 128
ATT_Q_HEADS = 4
ATT_KV_HEADS = 2
LRU_WIDTH = 1024
LRU_BLOCKS = 8
LRU_C = 8.0
CONV_W = 4
NA_HEADS = 4
NA_ROWS = 8
NA_COLS = 16
ATT_Q_W = ATT_Q_HEADS * HEAD_DIM
ATT_KV_W = ATT_KV_HEADS * HEAD_DIM
NA_W = NA_HEADS * HEAD_DIM
MOE_GROUPS = 4
MOE_EXPERTS_PER_GROUP = 8
MOE_EXPERTS = MOE_GROUPS * MOE_EXPERTS_PER_GROUP
MOE_HIDDEN = 1024
ROPE_THETA = 10000.0
EPS = 1e-6
LOG2E = 1.4426950408889634

COL_QA = 0
COL_KA = COL_QA + ATT_Q_W
COL_VA = COL_KA + ATT_KV_W
COL_UB = COL_VA + ATT_KV_W
COL_GB = COL_UB + LRU_WIDTH
COL_QN = COL_GB + LRU_WIDTH
COL_KN = COL_QN + NA_W
COL_VN = COL_KN + NA_W
IN_WIDTH = COL_VN + NA_W

TM = 256
MOE_BM = 256
NA_QROWS = 4
NA_WROWS = 12
NEG_BIAS = -1e30
VMEM_LIMIT = 56 * 1024 * 1024


def _cparams(sem, vmem=None):
    return pltpu.CompilerParams(dimension_semantics=sem, vmem_limit_bytes=vmem)


def _rms(x):
    return x * lax.rsqrt(jnp.mean(x * x, axis=-1, keepdims=True) + EPS)


def _adaln_kernel(c_ref, w_ref, b_ref, o_ref):
    cv = c_ref[...]
    s = cv * jax.nn.sigmoid(cv)
    o_ref[0] = jnp.dot(s.astype(BF16), w_ref[0].astype(BF16), preferred_element_type=F32) + b_ref[0]


def _adaln(cvec, ada_w, ada_b):
    L, D, N = ada_w.shape
    tn = 1024
    return pl.pallas_call(
        _adaln_kernel,
        grid=(L, N // tn),
        in_specs=[
            pl.BlockSpec((8, D), lambda l, j: (0, 0)),
            pl.BlockSpec((1, D, tn), lambda l, j: (l, 0, j)),
            pl.BlockSpec((1, 1, tn), lambda l, j: (l, 0, j)),
        ],
        out_specs=pl.BlockSpec((1, 8, tn), lambda l, j: (l, 0, j)),
        out_shape=jax.ShapeDtypeStruct((L, 8, N), F32),
        compiler_params=_cparams(("parallel", "parallel"), VMEM_LIMIT),
        name="adaln",
    )(cvec, ada_w, ada_b.reshape(L, 1, N))


def _inproj_kernel(*refs, n_first, two_sources):
    if two_sources:
        x_ref, xb_ref, mod_ref, g_ref, w_ref, o_ref = refs
        x = jnp.where(pl.program_id(0) < n_first, x_ref[...], xb_ref[...])
    else:
        x_ref, mod_ref, g_ref, w_ref, o_ref = refs
        x = x_ref[...]
    h = _rms(x) * g_ref[...]
    h = h * (1.0 + mod_ref[0, 1:2, :]) + mod_ref[0, 0:1, :]
    o_ref[...] = jnp.dot(h.astype(BF16), w_ref[...], preferred_element_type=F32)


def _inproj(x_rows, ctx_rows, mod, g1, w_in_bf, B, S):
    two = ctx_rows is not None
    D = x_rows.shape[1]
    M = x_rows.shape[0] + (ctx_rows.shape[0] if two else 0)
    N = w_in_bf.shape[1]
    tm = 512 if (S % 512 == 0 and (M - B * S) % 512 == 0) else TM
    per_seq = S // tm
    n_first = B * per_seq
    if two:
        n_ctx_blocks = ctx_rows.shape[0] // tm
        x_specs = [pl.BlockSpec((tm, D), lambda i: (jnp.minimum(i, n_first - 1), 0)),
                   pl.BlockSpec((tm, D), lambda i: (jnp.maximum(i - n_first, 0), 0),
                                pipeline_mode=pl.Buffered(1 if n_ctx_blocks == 1 else 2))]
        x_args = [x_rows, ctx_rows]
    else:
        x_specs = [pl.BlockSpec((tm, D), lambda i: (i, 0))]
        x_args = [x_rows]
    return pl.pallas_call(
        functools.partial(_inproj_kernel, n_first=n_first, two_sources=two),
        grid=(M // tm,),
        in_specs=[
            *x_specs,
            pl.BlockSpec((1, 6, D), lambda i: (jnp.minimum(i // per_seq, B), 0, 0)),
            pl.BlockSpec((1, D), lambda i: (0, 0)),
            pl.BlockSpec((D, N), lambda i: (0, 0), pipeline_mode=pl.Buffered(1)),
        ],
        out_specs=pl.BlockSpec((tm, N), lambda i: (i, 0)),
        out_shape=jax.ShapeDtypeStruct((M, N), F32),
        compiler_params=_cparams(("parallel",), VMEM_LIMIT),
        name="inproj",
    )(*x_args, mod, g1.reshape(1, D), w_in_bf)


def _norm_rope(t, g, cos, sin):
    y = _rms(t) * g
    lane = lax.broadcasted_iota(jnp.int32, y.shape, 1)
    first_half = (lane % 64) < 32
    partner = jnp.where(first_half, pltpu.roll(y, 96, 1), pltpu.roll(y, 32, 1))
    return y * cos + partner * sin


def _qkprep_kernel(p_ref, cos_ref, sin_ref, qg_ref, kg_ref, q_ref, k_ref, v_ref):
    cos = cos_ref[...]
    sin = sin_ref[...]
    scale = HEAD_DIM ** -0.5 * LOG2E
    for h in range(ATT_Q_HEADS):
        t = p_ref[:, COL_QA + h * HEAD_DIM:COL_QA + (h + 1) * HEAD_DIM]
        q_ref[:, h * HEAD_DIM:(h + 1) * HEAD_DIM] = (_norm_rope(t, qg_ref[...], cos, sin) * scale).astype(BF16)
    for h in range(ATT_KV_HEADS):
        t = p_ref[:, COL_KA + h * HEAD_DIM:COL_KA + (h + 1) * HEAD_DIM]
        k_ref[0, :, h * HEAD_DIM:(h + 1) * HEAD_DIM] = _norm_rope(t, kg_ref[...], cos, sin).astype(BF16)
        v_ref[0, :, 2 * h * HEAD_DIM:(2 * h + 1) * HEAD_DIM] = (
            p_ref[:, COL_VA + h * HEAD_DIM:COL_VA + (h + 1) * HEAD_DIM].astype(BF16))
        v_ref[0, :, (2 * h + 1) * HEAD_DIM:(2 * h + 2) * HEAD_DIM] = jnp.ones((p_ref.shape[0], HEAD_DIM), BF16)


def _qkprep(proj, cos_t, sin_t, qg, kg, B, S):
    M = proj.shape[0]
    nbs = S // TM
    nlat = B * nbs

    def b_of(i):
        return jnp.where(i < nlat, i // nbs, i - nlat)

    def pos_of(i):
        return jnp.where(i < nlat, i % nbs, nbs)

    kl = S + TM
    return pl.pallas_call(
        _qkprep_kernel,
        grid=(M // TM,),
        in_specs=[
            pl.BlockSpec((TM, COL_UB), lambda i: (i, 0)),
            pl.BlockSpec((TM, HEAD_DIM), lambda i: (pos_of(i), 0)),
            pl.BlockSpec((TM, HEAD_DIM), lambda i: (pos_of(i), 0)),
            pl.BlockSpec((1, HEAD_DIM), lambda i: (0, 0)),
            pl.BlockSpec((1, HEAD_DIM), lambda i: (0, 0)),
        ],
        out_specs=[
            pl.BlockSpec((TM, ATT_Q_W), lambda i: (i, 0)),
            pl.BlockSpec((1, TM, ATT_KV_W), lambda i: (b_of(i), pos_of(i), 0)),
            pl.BlockSpec((1, TM, 2 * ATT_KV_W), lambda i: (b_of(i), pos_of(i), 0)),
        ],
        out_shape=[
            jax.ShapeDtypeStruct((M, ATT_Q_W), BF16),
            jax.ShapeDtypeStruct((B, kl, ATT_KV_W), BF16),
            jax.ShapeDtypeStruct((B, kl, 2 * ATT_KV_W), BF16),
        ],
        compiler_params=_cparams(("parallel",), VMEM_LIMIT),
        name="qkprep",
    )(proj, cos_t, sin_t, qg.reshape(1, HEAD_DIM), kg.reshape(1, HEAD_DIM))


def _gqa_kernel(q_ref, k_ref, v_ref, o_in_ref, o_ref, q2_ref, sa_ref, sb_ref, m_ref, acc_ref, *, tk):
    del o_in_ref
    tq = q_ref.shape[0]
    q2_ref[0:tq, :] = q_ref[:, :HEAD_DIM]
    q2_ref[tq:, :] = q_ref[:, HEAD_DIM:]
    nk = k_ref.shape[1] // tk
    m_ref[...] = jnp.full(m_ref.shape, -jnp.inf, F32)
    acc_ref[...] = jnp.zeros(acc_ref.shape, F32)

    def scores(c, s_ref):
        off = pl.multiple_of(c * tk, tk)
        s_ref[...] = lax.dot_general(q2_ref[...], k_ref[0, pl.ds(off, tk), :], (((1,), (1,)), ((), ())),
                                     preferred_element_type=F32)

    def update(c, s_ref):
        off = pl.multiple_of(c * tk, tk)
        vc = v_ref[0, pl.ds(off, tk), :]
        for hh in range(2):
            rows = slice(hh * tq, (hh + 1) * tq)
            m = m_ref[rows, :]
            m_new = jnp.maximum(m, jnp.max(s_ref[rows, :], axis=-1, keepdims=True))
            m_ref[rows, :] = m_new
            p = jnp.exp2((s_ref[rows, :] - jnp.concatenate([m_new] * (tk // HEAD_DIM), axis=1)).astype(BF16))
            alpha = jnp.exp2(m - m_new)
            acc_ref[rows, :] = (jnp.concatenate([alpha, alpha], axis=1) * acc_ref[rows, :]
                                + jnp.dot(p, vc, preferred_element_type=F32))

    scores(0, sa_ref)
    n_pairs = (nk - 1) // 2

    def body(j, carry):
        scores(2 * j + 1, sb_ref)
        update(2 * j, sa_ref)
        scores(2 * j + 2, sa_ref)
        update(2 * j + 1, sb_ref)
        return carry

    lax.fori_loop(0, n_pairs, body, 0)
    if nk % 2 == 0:
        scores(nk - 1, sb_ref)
        update(nk - 2, sa_ref)
        update(nk - 1, sb_ref)
    else:
        update(nk - 1, sa_ref)
    acc = acc_ref[...]
    o = acc[:, :HEAD_DIM] / acc[:, HEAD_DIM:]
    o_ref[:, :HEAD_DIM] = o[:tq].astype(BF16)
    o_ref[:, HEAD_DIM:] = o[tq:].astype(BF16)


def _gqa_latent(q, k_all, v_all, B, S):
    M = q.shape[0]
    kl = k_all.shape[1]
    tq = 512 if S % 512 == 0 else TM
    nbs = S // tq
    tk = 768 if kl % 768 == 0 else TM
    g = ATT_Q_HEADS // ATT_KV_HEADS
    return pl.pallas_call(
        functools.partial(_gqa_kernel, tk=tk),
        grid=(B, ATT_KV_HEADS, nbs),
        in_specs=[
            pl.BlockSpec((tq, g * HEAD_DIM), lambda b, h, i: (b * nbs + i, h)),
            pl.BlockSpec((1, kl, HEAD_DIM), lambda b, h, i: (b, 0, h)),
            pl.BlockSpec((1, kl, 2 * HEAD_DIM), lambda b, h, i: (b, 0, h)),
            pl.BlockSpec(memory_space=pl.ANY),
        ],
        out_specs=pl.BlockSpec((tq, g * HEAD_DIM), lambda b, h, i: (b * nbs + i, h)),
        out_shape=jax.ShapeDtypeStruct((M, ATT_Q_W), BF16),
        input_output_aliases={3: 0},
        scratch_shapes=[
            pltpu.VMEM((g * tq, HEAD_DIM), BF16),
            pltpu.VMEM((g * tq, tk), F32),
            pltpu.VMEM((g * tq, tk), F32),
            pltpu.VMEM((g * tq, HEAD_DIM), F32),
            pltpu.VMEM((g * tq, 2 * HEAD_DIM), F32),
        ],
        compiler_params=_cparams(("parallel", "parallel", "parallel"), VMEM_LIMIT),
        name="gqa_latent",
    )(q, k_all, v_all, jnp.zeros((M, ATT_Q_W), BF16))


def _ctx_attn_kernel(q_ref, k_ref, v_ref, o_in_ref, o_ref, *, scale):
    del o_in_ref
    if scale is None:
        q = q_ref[...]
    else:
        q = (q_ref[...] * (scale * LOG2E)).astype(BF16)
    k = k_ref[...].astype(BF16)
    s = lax.dot_general(q, k, (((1,), (1,)), ((), ())), preferred_element_type=F32)
    p = jnp.exp2(s - jnp.max(s, axis=-1, keepdims=True))
    l = jnp.sum(p, axis=-1, keepdims=True)
    pv = jnp.dot(p.astype(BF16), v_ref[...].astype(BF16), preferred_element_type=F32)
    o_ref[...] = (pv / l).astype(BF16)


def _ctx_attn(q_arr, q_map, k_arr, k_spec, v_arr, v_spec, o_arr, B, n_heads, ctx_blk0, scale):
    return pl.pallas_call(
        functools.partial(_ctx_attn_kernel, scale=scale),
        grid=(B, n_heads),
        in_specs=[
            pl.BlockSpec((TM, HEAD_DIM), q_map),
            k_spec,
            v_spec,
            pl.BlockSpec(memory_space=pl.ANY),
        ],
        out_specs=pl.BlockSpec((TM, HEAD_DIM), lambda b, h: (ctx_blk0 + b, h)),
        out_shape=jax.ShapeDtypeStruct(o_arr.shape, o_arr.dtype),
        input_output_aliases={3: 0},
        compiler_params=_cparams(("parallel", "parallel"), VMEM_LIMIT),
        name="ctx_attn",
    )(q_arr, k_arr, v_arr, o_arr)


def _lru_kernel(u_ref, up_ref, un_ref, *rest, reverse, final):
    if final:
        (g_ref, hrev_ref, cw_ref, cb_ref, wr_ref, br_ref, wi_ref, bi_ref, sp_ref,
         o_ref, xpad, a_scr, b_scr, h_scr, hst) = rest
    else:
        (cw_ref, cb_ref, wr_ref, br_ref, wi_ref, bi_ref, sp_ref,
         o_ref, xpad, a_scr, b_scr, h_scr, hst) = rest
    s = pl.program_id(1)
    ns = pl.num_programs(1)
    T = u_ref.shape[0]
    W = u_ref.shape[1]

    @pl.when(s == 0)
    def _():
        hst[...] = jnp.zeros_like(hst)

    if reverse:
        j = ns - 1 - s
    else:
        j = s - 1
    is_first = jnp.logical_or(s == 0, j == 0)
    is_last = jnp.logical_or(s == 0, j == ns - 2)
    prev_rows = jnp.where(is_first, 0.0, up_ref[...])
    next_rows = jnp.where(is_last, 0.0, un_ref[...])
    xpad[0:8, :] = prev_rows
    xpad[8:8 + T, :] = u_ref[...]
    xpad[8 + T:16 + T, :] = next_rows
    u = (cb_ref[...]
         + xpad[7:7 + T, :] * cw_ref[0:1, :]
         + xpad[8:8 + T, :] * cw_ref[1:2, :]
         + xpad[9:9 + T, :] * cw_ref[2:3, :]
         + xpad[10:10 + T, :] * cw_ref[3:4, :])

    ub = u.astype(BF16)
    nb = LRU_WIDTH // LRU_BLOCKS
    for n in range(LRU_BLOCKS):
        sl = slice(n * nb, (n + 1) * nb)
        un_ = ub[:, sl]
        r = jax.nn.sigmoid(jnp.dot(un_, wr_ref[n], preferred_element_type=F32) + br_ref[:, sl])
        ig = jax.nn.sigmoid(jnp.dot(un_, wi_ref[n], preferred_element_type=F32) + bi_ref[:, sl])
        log_a = (-LRU_C) * r * sp_ref[:, sl]
        a = jnp.exp(log_a)
        a_scr[:, sl] = a
        one_minus_a2 = -jnp.tanh(log_a) * (a * a + 1.0)
        b_scr[:, sl] = jnp.sqrt(one_minus_a2) * (ig * u[:, sl])

    def step(t, h):
        row = (T - 1 - t) if reverse else t
        h = a_scr[pl.ds(row, 1), :] * h + b_scr[pl.ds(row, 1), :]
        h_scr[pl.ds(row, 1), :] = h
        return h

    hst[...] = lax.fori_loop(0, T, step, hst[...], unroll=8)

    if final:
        y = h_scr[...] + hrev_ref[...]
        o_ref[...] = (y * jax.nn.gelu(g_ref[...])).astype(o_ref.dtype)
    else:
        o_ref[...] = h_scr[...]
    del W


def _lru_dir(proj, hrev, conv_w, conv_b, wr, br, wi, bi, sp, B, S, reverse):
    M = proj.shape[0]
    nbs = S // TM
    nlat = B * nbs
    final = hrev is not None
    ub_blk = COL_UB // LRU_WIDTH
    gb_blk = COL_GB // LRU_WIDTH
    r8 = TM // 8
    n8 = M // 8

    def blk(b, s):
        j = (nbs - s) if reverse else (s - 1)
        return jnp.where(s == 0, nlat + b, b * nbs + j)

    in_specs = [
        pl.BlockSpec((TM, LRU_WIDTH), lambda b, s: (blk(b, s), ub_blk)),
        pl.BlockSpec((8, LRU_WIDTH), lambda b, s: (jnp.maximum(blk(b, s) * r8 - 1, 0), ub_blk)),
        pl.BlockSpec((8, LRU_WIDTH), lambda b, s: (jnp.minimum((blk(b, s) + 1) * r8, n8 - 1), ub_blk)),
    ]
    args = [proj, proj, proj]
    if final:
        in_specs += [
            pl.BlockSpec((TM, LRU_WIDTH), lambda b, s: (blk(b, s), gb_blk)),
            pl.BlockSpec((TM, LRU_WIDTH), lambda b, s: (blk(b, s), 0)),
        ]
        args += [proj, hrev]
    const2 = lambda b, s: (0, 0)
    in_specs += [
        pl.BlockSpec((CONV_W, LRU_WIDTH), const2),
        pl.BlockSpec((1, LRU_WIDTH), const2),
        pl.BlockSpec((LRU_BLOCKS, LRU_WIDTH // LRU_BLOCKS, LRU_WIDTH // LRU_BLOCKS), lambda b, s: (0, 0, 0)),
        pl.BlockSpec((1, LRU_WIDTH), const2),
        pl.BlockSpec((LRU_BLOCKS, LRU_WIDTH // LRU_BLOCKS, LRU_WIDTH // LRU_BLOCKS), lambda b, s: (0, 0, 0)),
        pl.BlockSpec((1, LRU_WIDTH), const2),
        pl.BlockSpec((1, LRU_WIDTH), const2),
    ]
    args += [conv_w, conv_b.reshape(1, -1), wr.astype(BF16), br.reshape(1, -1), wi.astype(BF16),
             bi.reshape(1, -1), sp.reshape(1, -1)]
    return pl.pallas_call(
        functools.partial(_lru_kernel, reverse=reverse, final=final),
        grid=(B, nbs + 1),
        in_specs=in_specs,
        out_specs=pl.BlockSpec((TM, LRU_WIDTH), lambda b, s: (blk(b, s), 0)),
        out_shape=jax.ShapeDtypeStruct((M, LRU_WIDTH), BF16 if final else F32),
        scratch_shapes=[
            pltpu.VMEM((TM + 16, LRU_WIDTH), F32),
            pltpu.VMEM((TM, LRU_WIDTH), F32),
            pltpu.VMEM((TM, LRU_WIDTH), F32),
            pltpu.VMEM((TM, LRU_WIDTH), F32),
            pltpu.VMEM((1, LRU_WIDTH), F32),
        ],
        compiler_params=_cparams(("parallel", "arbitrary"), VMEM_LIMIT),
        name="lru_fwd" if final else "lru_rev",
    )(*args)


def _na_kernel(q_ref, k0_ref, k1_ref, k2_ref, v0_ref, v1_ref, v2_ref, kc_ref, vc_ref, tl_ref, tr_ref,
               o_in_ref, o_ref, s_scr, *, n_rows):
    del o_in_ref
    i = pl.program_id(1)
    nbs = n_rows // NA_QROWS
    r0 = NA_QROWS * i
    ws = NA_QROWS * jnp.clip(i - 1, 0, nbs - 3)
    scale = HEAD_DIM ** -0.5 * LOG2E
    nkeys = s_scr.shape[1]
    ones = jnp.ones((nkeys, HEAD_DIM), BF16)
    for h in range(NA_HEADS):
        hs = slice(h * HEAD_DIM, (h + 1) * HEAD_DIM)
        qh = (q_ref[:, hs] * scale).astype(BF16)
        kh = jnp.concatenate([k0_ref[:, hs], k1_ref[:, hs], k2_ref[:, hs], kc_ref[:, hs]], axis=0).astype(BF16)
        vh = jnp.concatenate([v0_ref[:, hs], v1_ref[:, hs], v2_ref[:, hs], vc_ref[:, hs]], axis=0).astype(BF16)
        vh = jnp.concatenate([vh, ones], axis=1)
        s_scr[...] = lax.dot_general(qh, kh, (((1,), (1,)), ((), ())), preferred_element_type=F32)
        for qr in range(NA_QROWS):
            r = r0 + qr
            rs = jnp.clip(r - NA_ROWS // 2, 0, n_rows - NA_ROWS)

            def tile_of(a):
                kr = ws + a
                valid = jnp.logical_and(kr >= rs, kr < rs + NA_ROWS)
                return jnp.where(valid, kr - r + (NA_ROWS - 1), 2 * NA_ROWS - 1)

            for pr in range(NA_WROWS // 2):
                bias = tl_ref[h, tile_of(2 * pr)] + tr_ref[h, tile_of(2 * pr + 1)]
                rsl = slice(qr * GRID_W, (qr + 1) * GRID_W)
                csl = slice(pr * 2 * GRID_W, (pr + 1) * 2 * GRID_W)
                s_scr[rsl, csl] = s_scr[rsl, csl] + bias
        s = s_scr[...]
        p = jnp.exp2((s - jnp.max(s, axis=-1, keepdims=True)).astype(BF16))
        pv = jnp.dot(p, vh, preferred_element_type=F32)
        o_ref[:, hs] = (pv[:, :HEAD_DIM] / pv[:, HEAD_DIM:]).astype(BF16)


def _na_latent(proj, tl, tr, B, S):
    M = proj.shape[0]
    nbs = S // TM
    nlat = B * nbs
    n_rows = S // GRID_W
    qb, kb, vb = COL_QN // NA_W, COL_KN // NA_W, COL_VN // NA_W

    def wblk(b, i, t):
        return b * nbs + jnp.clip(i - 1, 0, nbs - 3) + t

    blk = (TM, NA_W)
    in_specs = [pl.BlockSpec(blk, lambda b, i: (b * nbs + i, qb))]
    in_specs += [pl.BlockSpec(blk, functools.partial(lambda b, i, t: (wblk(b, i, t), kb), t=t)) for t in range(3)]
    in_specs += [pl.BlockSpec(blk, functools.partial(lambda b, i, t: (wblk(b, i, t), vb), t=t)) for t in range(3)]
    in_specs += [
        pl.BlockSpec(blk, lambda b, i: (nlat + b, kb)),
        pl.BlockSpec(blk, lambda b, i: (nlat + b, vb)),
        pl.BlockSpec(tl.shape, lambda b, i: (0, 0, 0, 0)),
        pl.BlockSpec(tr.shape, lambda b, i: (0, 0, 0, 0)),
        pl.BlockSpec(memory_space=pl.ANY),
    ]
    return pl.pallas_call(
        functools.partial(_na_kernel, n_rows=n_rows),
        grid=(B, nbs),
        in_specs=in_specs,
        out_specs=pl.BlockSpec(blk, lambda b, i: (b * nbs + i, 0)),
        out_shape=jax.ShapeDtypeStruct((M, NA_W), BF16),
        input_output_aliases={11: 0},
        scratch_shapes=[pltpu.VMEM((TM, NA_WROWS * GRID_W + TM), F32)],
        compiler_params=_cparams(("parallel", "parallel"), VMEM_LIMIT),
        name="na_latent",
    )(proj, proj, proj, proj, proj, proj, proj, proj, proj, tl, tr, jnp.zeros((M, NA_W), BF16))


def _na_bias_tables(rpb):
    j = np.arange(GRID_W)
    cs = np.clip(j - NA_COLS // 2, 0, GRID_W - NA_COLS)
    kc = np.arange(GRID_W)
    inside = (kc[None, :] >= cs[:, None]) & (kc[None, :] < cs[:, None] + NA_COLS)
    dc = kc[None, :] - j[:, None] + (NA_COLS - 1)
    sel = (dc[:, :, None] == np.arange(2 * NA_COLS - 1)[None, None, :]) & inside[:, :, None]
    t = jnp.einsum("hrd,jkd->hrjk", rpb * LOG2E, jnp.asarray(sel, F32), precision=lax.Precision.HIGHEST)
    t = jnp.where(inside[None, None], t, NEG_BIAS)
    t = jnp.concatenate([t, jnp.full_like(t[:, :1], NEG_BIAS)], axis=1)
    z = jnp.zeros_like(t)
    return jnp.concatenate([t, z], axis=-1), jnp.concatenate([z, t], axis=-1)


def _outproj_kernel(*refs, n_first, two_sources):
    if two_sources:
        (oa_ref, ob_ref, oc_ref, x_ref, xb_ref, mod_ref, g_ref, w_ref, wr_ref, br_ref,
         xo_ref, h2_ref, rt_ref, cnt_ref, carry) = refs
    else:
        (oa_ref, ob_ref, oc_ref, x_ref, mod_ref, g_ref, w_ref, wr_ref, br_ref,
         xo_ref, h2_ref, rt_ref, cnt_ref, carry) = refs
    i = pl.program_id(0)

    @pl.when(i == 0)
    def _():
        carry[...] = jnp.zeros_like(carry)

    k1 = ATT_Q_W
    k2 = k1 + LRU_WIDTH
    acc = jnp.dot(oa_ref[...], w_ref[0:k1, :], preferred_element_type=F32)
    acc = acc + jnp.dot(ob_ref[...], w_ref[k1:k2, :], preferred_element_type=F32)
    acc = acc + jnp.dot(oc_ref[...], w_ref[k2:, :], preferred_element_type=F32)
    x_in = x_ref[...]
    if two_sources:
        x_in = jnp.where(i < n_first, x_in, xb_ref[...])
    xn = x_in + mod_ref[0, 2:3, :] * acc
    xo_ref[...] = xn
    h2 = _rms(xn) * g_ref[...]
    h2 = h2 * (1.0 + mod_ref[0, 4:5, :]) + mod_ref[0, 3:4, :]
    h2_ref[...] = _pack_bf16_pair(h2)
    logits = jnp.dot(h2.astype(BF16), wr_ref[...], preferred_element_type=F32) + br_ref[...]
    route = _route_math(logits)
    rt_ref[...] = _rank_math(route, carry)
    cnt_ref[...] = jnp.broadcast_to(carry[...], cnt_ref.shape)


def _pack_bf16_pair(x):
    w = x.shape[1] // 2
    lo = pltpu.bitcast(x[:, :w].astype(BF16).astype(F32), jnp.uint32)
    hi = pltpu.bitcast(x[:, w:].astype(BF16).astype(F32), jnp.uint32)
    return hi | (lo >> 16)


def _unpack_bf16_pair(words):
    lo = pltpu.bitcast(words << 16, F32)
    hi = pltpu.bitcast(words & jnp.uint32(0xFFFF0000), F32)
    return jnp.concatenate([lo, hi], axis=1).astype(BF16)


def _outproj(oa, ob, oc, x_rows, ctx_rows, mod, g2, w_out_bf, w_router, b_router, B, S, n_rows):
    M = oa.shape[0]
    D = x_rows.shape[1]
    two = ctx_rows is not None
    tm = 512 if (S % 512 == 0 and (M - B * S) % 512 == 0) else TM
    per_seq = S // tm
    n_first = B * per_seq
    packed = jax.eval_shape(_pack_bf16_pair, jax.ShapeDtypeStruct((tm, D), F32))
    if two:
        x_specs = [pl.BlockSpec((tm, D), lambda i: (jnp.minimum(i, n_first - 1), 0)),
                   pl.BlockSpec((tm, D), lambda i: (jnp.maximum(i - n_first, 0), 0))]
        x_args = [x_rows, ctx_rows]
        alias = {}
    else:
        x_specs = [pl.BlockSpec((tm, D), lambda i: (i, 0))]
        x_args = [x_rows]
        alias = {3: 0}
    return pl.pallas_call(
        functools.partial(_outproj_kernel, n_first=n_first, two_sources=two),
        grid=(n_rows // tm,),
        in_specs=[
            pl.BlockSpec((tm, ATT_Q_W), lambda i: (i, 0)),
            pl.BlockSpec((tm, LRU_WIDTH), lambda i: (i, 0)),
            pl.BlockSpec((tm, NA_W), lambda i: (i, 0)),
            *x_specs,
            pl.BlockSpec((1, 6, D), lambda i: (jnp.minimum(i // per_seq, B), 0, 0)),
            pl.BlockSpec((1, D), lambda i: (0, 0)),
            pl.BlockSpec(w_out_bf.shape, lambda i: (0, 0)),
            pl.BlockSpec(w_router.shape, lambda i: (0, 0)),
            pl.BlockSpec((1, 128), lambda i: (0, 0)),
        ],
        out_specs=[
            pl.BlockSpec((tm, D), lambda i: (i, 0)),
            pl.BlockSpec(packed.shape, lambda i: (i, 0)),
            pl.BlockSpec((tm, 128), lambda i: (i, 0)),
            pl.BlockSpec((8, 128), lambda i: (0, 0)),
        ],
        out_shape=[
            jax.ShapeDtypeStruct((M, D), F32),
            jax.ShapeDtypeStruct((n_rows, packed.shape[1]), packed.dtype),
            jax.ShapeDtypeStruct((n_rows, 128), F32),
            jax.ShapeDtypeStruct((8, 128), F32),
        ],
        input_output_aliases=alias,
        scratch_shapes=[pltpu.VMEM((1, 128), F32)],
        compiler_params=_cparams(("arbitrary",), VMEM_LIMIT),
        name="outproj",
    )(oa, ob, oc, *x_args, mod, g2.reshape(1, D), w_out_bf, w_router, b_router)


def _route_math(x):
    lane = lax.broadcasted_iota(jnp.int32, x.shape, 1)
    big = jnp.int32(1 << 20)
    ninf = -jnp.inf

    def first_argmax(vals):
        m = jnp.max(vals, axis=-1, keepdims=True)
        idx = jnp.min(jnp.where(vals == m, lane, big), axis=-1, keepdims=True)
        return m, idx

    lg = jnp.where(lane < MOE_GROUPS, x, ninf)
    mg, g_star = first_argmax(lg)
    g_gate = 1.0 / jnp.sum(jnp.exp(lg - mg), axis=-1, keepdims=True)
    lo = MOE_GROUPS + MOE_EXPERTS_PER_GROUP * g_star
    le = jnp.where(jnp.logical_and(lane >= lo, lane < lo + MOE_EXPERTS_PER_GROUP), x, ninf)
    v1, i1 = first_argmax(le)
    v2, i2 = first_argmax(jnp.where(lane == i1, ninf, le))
    e21 = jnp.exp(v2 - v1)
    w1 = g_gate / (1.0 + e21)
    w2 = g_gate * e21 / (1.0 + e21)
    e1 = (i1 - MOE_GROUPS).astype(F32)
    e2 = (i2 - MOE_GROUPS).astype(F32)
    return jnp.where(lane == 0, e1, jnp.where(lane == 1, e2, jnp.where(lane == 2, w1,
                     jnp.where(lane == 3, w2, 0.0))))


def _rank_math(r, carry):
    lane = lax.broadcasted_iota(jnp.int32, r.shape, 1).astype(F32)
    oh0 = jnp.where(lane == r[:, 0:1], 1.0, 0.0)
    oh1 = jnp.where(lane == r[:, 1:2], 1.0, 0.0)
    oh = oh0 + oh1
    n = r.shape[0]
    row = lax.broadcasted_iota(jnp.int32, (n, n), 0)
    col = lax.broadcasted_iota(jnp.int32, (n, n), 1)
    tri = jnp.where(col < row, 1.0, 0.0).astype(BF16)
    tot = carry[...] + jnp.dot(tri, oh.astype(BF16), preferred_element_type=F32)
    rank0 = jnp.sum(oh0 * tot, axis=-1, keepdims=True)
    rank1 = jnp.sum(oh1 * tot, axis=-1, keepdims=True)
    lane_i = lax.broadcasted_iota(jnp.int32, r.shape, 1)
    carry[...] = carry[...] + jnp.sum(oh, axis=0, keepdims=True)
    return jnp.where(lane_i == 4, rank0, jnp.where(lane_i == 5, rank1, r))


def _dispatch_kernel(dest_ref, pad_ref, h_hbm, xs_hbm, hbuf, zeros, fsem, rsem, zsem, *, n_tok):
    i = pl.program_id(0)

    @pl.when(i == 0)
    def _():
        zeros[...] = jnp.zeros_like(zeros)
        pieces = [1 << b for b in range(MOE_BM.bit_length() - 2, 2, -1)]

        def copies(e):
            start, n = pad_ref[0, e], pad_ref[1, e]
            head = jnp.minimum((-start) & 7, n)
            out = []
            for r in range(7):
                out.append((r < head, pltpu.make_async_copy(zeros.at[pl.ds(0, 1)],
                                                            xs_hbm.at[pl.ds(start + r, 1)], zsem)))
            rest = n - head
            for p in pieces:
                at = pl.multiple_of(start + head + (rest & ~(2 * p - 1)), 8)
                out.append(((rest & p) != 0, pltpu.make_async_copy(zeros.at[pl.ds(0, p)],
                                                                   xs_hbm.at[pl.ds(at, p)], zsem)))
            return out

        def fill(e, c):
            for on, cp in copies(e):
                @pl.when(on)
                def _():
                    cp.start()
            return c

        def drain(e, c):
            for on, cp in copies(e):
                @pl.when(on)
                def _():
                    cp.wait()
            return c

        lax.fori_loop(0, MOE_EXPERTS, fill, 0)
        lax.fori_loop(0, MOE_EXPERTS, drain, 0)

        half = zeros.shape[0]
        n_tail = (xs_hbm.shape[0] - pad_ref[2, 0]) // half

        def tail_copy(j):
            at = pl.multiple_of(pad_ref[2, 0] + j * half, half)
            return pltpu.make_async_copy(zeros, xs_hbm.at[pl.ds(at, half)], zsem)

        def tail_fill(j, c):
            tail_copy(j).start()
            return c

        def tail_drain(j, c):
            tail_copy(j).wait()
            return c

        lax.fori_loop(0, n_tail, tail_fill, 0)
        lax.fori_loop(0, n_tail, tail_drain, 0)

    n = pl.num_programs(0)
    slot = i % 3
    nslot = (i + 1) % 3

    def fetch(blk, s):
        return pltpu.make_async_copy(h_hbm.at[pl.ds(pl.multiple_of(blk * TM, TM), TM)], hbuf.at[s], fsem.at[s])

    def rows_done(s):
        for k in range(2):
            pltpu.make_async_copy(hbuf.at[s], xs_hbm.at[pl.ds(0, TM)], rsem.at[s]).wait()

    @pl.when(i == 0)
    def _():
        fetch(0, 0).start()

    @pl.when(i >= 2)
    def _():
        rows_done(nslot)

    @pl.when(i + 1 < n)
    def _():
        fetch(i + 1, nslot).start()

    fetch(i, slot).wait()

    def issue(t, c):
        tok = i * TM + t
        for k in range(2):
            pltpu.make_async_copy(hbuf.at[slot, pl.ds(t, 1)], xs_hbm.at[pl.ds(dest_ref[k * n_tok + tok], 1)],
                                  rsem.at[slot]).start()
        return c

    lax.fori_loop(0, TM, issue, 0, unroll=8)

    @pl.when(i == n - 1)
    def _():
        rows_done(slot)

    @pl.when(jnp.logical_and(i == n - 1, i >= 1))
    def _():
        rows_done((i + 2) % 3)


def _dispatch(dest_flat, pad_info, h2, n_rows, nblk):
    W = h2.shape[1]
    return pl.pallas_call(
        functools.partial(_dispatch_kernel, n_tok=nblk * TM),
        grid_spec=pltpu.PrefetchScalarGridSpec(
            num_scalar_prefetch=2,
            grid=(nblk,),
            in_specs=[pl.BlockSpec(memory_space=pl.ANY)],
            out_specs=pl.BlockSpec(memory_space=pl.ANY),
            scratch_shapes=[pltpu.VMEM((3, TM, W), h2.dtype), pltpu.VMEM((MOE_BM // 2, W), h2.dtype),
                            pltpu.SemaphoreType.DMA((3,)), pltpu.SemaphoreType.DMA((3,)),
                            pltpu.SemaphoreType.DMA(())],
        ),
        out_shape=jax.ShapeDtypeStruct((n_rows, W), h2.dtype),
        compiler_params=pltpu.CompilerParams(dimension_semantics=("arbitrary",), has_side_effects=True),
        name="dispatch",
    )(dest_flat, pad_info, h2)


PLAN_EXPERT, PLAN_FIRST, PLAN_SLOT, PLAN_NEXT, PLAN_HALF = 0, 1, 2, 3, 4


def _block_plan(block_e, n_used, pad_starts, counts):
    nb = block_e.shape[0]
    idx = jnp.arange(nb, dtype=jnp.int32)
    is_first = (idx < n_used[0]) & ((idx == 0) | (block_e != jnp.roll(block_e, 1)))
    seg = jnp.cumsum(is_first.astype(jnp.int32)) - 1
    first_pos = jnp.where(is_first, idx, nb)
    at_or_after = jnp.flip(lax.cummin(jnp.flip(first_pos)))
    next_first = jnp.concatenate([at_or_after[1:], jnp.full((1,), nb, jnp.int32)])
    next_e = jnp.where(next_first < nb, block_e[jnp.minimum(next_first, nb - 1)], -1)
    valid = counts[block_e] - (idx * MOE_BM - pad_starts[block_e])
    half_only = (valid <= MOE_BM // 2).astype(jnp.int32)
    return jnp.stack([block_e, is_first.astype(jnp.int32), seg % 2, next_e, half_only]).astype(jnp.int32)


def _expert_weights_step(i, plan_ref, w_hbms, wbuf, wbf, sem, layer):
    def copies(e, slot):
        return [pltpu.make_async_copy(w.at[layer, e], wbuf.at[slot, n], sem.at[slot, n])
                for n, w in enumerate(w_hbms)]

    @pl.when(plan_ref[PLAN_FIRST, i] == 1)
    def _():
        slot = plan_ref[PLAN_SLOT, i]

        @pl.when(i == 0)
        def _():
            for c in copies(plan_ref[PLAN_EXPERT, 0], 0):
                c.start()

        for c in copies(plan_ref[PLAN_EXPERT, i], slot):
            c.wait()
        nrow, ncol = wbuf.shape[2], wbuf.shape[3]
        rows = 256

        def cast_rows(r, c):
            off = pl.multiple_of(r * rows, rows)
            for n in range(len(w_hbms)):
                wbf[pl.ds(off, rows), n * ncol:(n + 1) * ncol] = wbuf[slot, n, pl.ds(off, rows), :].astype(BF16)
            return c

        lax.fori_loop(0, nrow // rows, cast_rows, 0)
        nxt = plan_ref[PLAN_NEXT, i]

        @pl.when(nxt >= 0)
        def _():
            for c in copies(nxt, 1 - slot):
                c.start()


def _moe1_kernel(plan_ref, nu_ref, x_ref, wg_hbm, wu_hbm, o_ref, wbuf, wcat, sem, *, layer):
    i = pl.program_id(0)

    @pl.when(i < nu_ref[0])
    def _():
        _expert_weights_step(i, plan_ref, (wg_hbm, wu_hbm), wbuf, wcat, sem, layer)

        def gated(rows):
            h = jnp.dot(_unpack_bf16_pair(x_ref[rows, :]), wcat[...], preferred_element_type=F32)
            g = h[:, :MOE_HIDDEN]
            u = h[:, MOE_HIDDEN:]
            return (g * jax.nn.sigmoid(g) * u).astype(BF16)

        half = MOE_BM // 2

        @pl.when(plan_ref[PLAN_HALF, i] == 1)
        def _():
            o_ref[0:half, :] = gated(slice(0, half))
            o_ref[half:, :] = jnp.zeros((MOE_BM - half, MOE_HIDDEN), BF16)

        @pl.when(plan_ref[PLAN_HALF, i] == 0)
        def _():
            o_ref[...] = gated(slice(None))

    @pl.when(i >= nu_ref[0])
    def _():
        o_ref[...] = jnp.zeros_like(o_ref)


def _moe1(plan, n_used, xs, w_gate, w_up, layer):
    P, W = xs.shape
    D = w_gate.shape[2]
    nb = P // MOE_BM
    used_blk = lambda i, plan, nu: (jnp.minimum(i, nu[0] - 1), 0)
    return pl.pallas_call(
        functools.partial(_moe1_kernel, layer=layer),
        grid_spec=pltpu.PrefetchScalarGridSpec(
            num_scalar_prefetch=2,
            grid=(nb,),
            in_specs=[pl.BlockSpec((MOE_BM, W), used_blk), pl.BlockSpec(memory_space=pl.ANY),
                      pl.BlockSpec(memory_space=pl.ANY)],
            out_specs=pl.BlockSpec((MOE_BM, MOE_HIDDEN), lambda i, plan, nu: (i, 0)),
            scratch_shapes=[pltpu.VMEM((2, 2, D, MOE_HIDDEN), F32), pltpu.VMEM((D, 2 * MOE_HIDDEN), BF16),
                            pltpu.SemaphoreType.DMA((2, 2))],
        ),
        out_shape=jax.ShapeDtypeStruct((P, MOE_HIDDEN), BF16),
        compiler_params=_cparams(("arbitrary",), VMEM_LIMIT),
        name="moe_up",
    )(plan, n_used, xs, w_gate, w_up)


def _moe2_kernel(plan_ref, nu_ref, h_ref, wd_hbm, o_ref, wbuf, wbf, sem, *, layer):
    i = pl.program_id(0)

    @pl.when(i < nu_ref[0])
    def _():
        _expert_weights_step(i, plan_ref, (wd_hbm,), wbuf, wbf, sem, layer)
        half = MOE_BM // 2

        @pl.when(plan_ref[PLAN_HALF, i] == 1)
        def _():
            o_ref[0:half, :] = jnp.dot(h_ref[0:half, :], wbf[...], preferred_element_type=F32)
            o_ref[half:, :] = jnp.zeros((MOE_BM - half, o_ref.shape[1]), F32)

        @pl.when(plan_ref[PLAN_HALF, i] == 0)
        def _():
            o_ref[...] = jnp.dot(h_ref[...], wbf[...], preferred_element_type=F32)

    @pl.when(i >= nu_ref[0])
    def _():
        o_ref[...] = jnp.zeros_like(o_ref)


def _moe2(plan, n_used, hmid, w_down, layer):
    P = hmid.shape[0]
    D = w_down.shape[3]
    nb = P // MOE_BM
    return pl.pallas_call(
        functools.partial(_moe2_kernel, layer=layer),
        grid_spec=pltpu.PrefetchScalarGridSpec(
            num_scalar_prefetch=2,
            grid=(nb,),
            in_specs=[
                pl.BlockSpec((MOE_BM, MOE_HIDDEN), lambda i, plan, nu: (i, 0)),
                pl.BlockSpec(memory_space=pl.ANY),
            ],
            out_specs=pl.BlockSpec((MOE_BM, D), lambda i, plan, nu: (i, 0)),
            scratch_shapes=[pltpu.VMEM((2, 1, MOE_HIDDEN, D), F32), pltpu.VMEM((MOE_HIDDEN, D), BF16),
                            pltpu.SemaphoreType.DMA((2, 1))],
        ),
        out_shape=jax.ShapeDtypeStruct((P, D), F32),
        compiler_params=_cparams(("arbitrary",), VMEM_LIMIT),
        name="moe_down",
    )(plan, n_used, hmid, w_down)


def _combine_kernel(dest_ref, yb_hbm, x_ref, r_ref, mod_ref, fg_ref, o_ref, ybuf, sem, *, final):
    i = pl.program_id(0)
    n = pl.num_programs(0)

    n_tok = n * TM

    def row_copy(tok, k, t, slot):
        return pltpu.make_async_copy(yb_hbm.at[pl.ds(dest_ref[k * n_tok + tok], 1)],
                                     ybuf.at[slot, k, pl.ds(t, 1)], sem.at[slot])

    def issue_rows(blk, slot, t0):
        for r in range(8):
            for k in range(2):
                row_copy(blk * TM + t0 + r, k, t0 + r, slot).start()

    def wait_slot(slot):
        for k in range(2):
            pltpu.make_async_copy(yb_hbm.at[pl.ds(0, TM)], ybuf.at[slot, k], sem.at[slot]).wait()

    @pl.when(i == 0)
    def _():
        def first(j, c):
            issue_rows(0, 0, pl.multiple_of(j * 8, 8))
            return c
        lax.fori_loop(0, TM // 8, first, 0)

    def step(cur):
        @pl.when(i + 1 < n)
        def _():
            def body(j, c):
                issue_rows(i + 1, 1 - cur, pl.multiple_of(j * 8, 8))
                return c
            lax.fori_loop(0, TM // 8, body, 0)

        wait_slot(cur)
        r = r_ref[...]
        y = r[:, 2:3] * ybuf[cur, 0] + r[:, 3:4] * ybuf[cur, 1]
        xn = x_ref[...] + mod_ref[0, 5:6, :] * y
        if final:
            xn = _rms(xn) * fg_ref[...]
        o_ref[...] = xn

    @pl.when(i % 2 == 0)
    def _():
        step(0)

    @pl.when(i % 2 == 1)
    def _():
        step(1)


def _combine(dest_flat, yb, x_all, route, mod, final_g, seg_of, nblk, final):
    M, D = x_all.shape
    out_rows = nblk * TM if final else M
    kwargs = {} if final else {"input_output_aliases": {2: 0}}
    return pl.pallas_call(
        functools.partial(_combine_kernel, final=final),
        grid_spec=pltpu.PrefetchScalarGridSpec(
            num_scalar_prefetch=1,
            grid=(nblk,),
            in_specs=[
                pl.BlockSpec(memory_space=pl.ANY),
                pl.BlockSpec((TM, D), lambda i, d: (i, 0)),
                pl.BlockSpec((TM, 128), lambda i, d: (i, 0)),
                pl.BlockSpec((1, 6, D), lambda i, d: (seg_of(i), 0, 0)),
                pl.BlockSpec((1, D), lambda i, d: (0, 0)),
            ],
            out_specs=pl.BlockSpec((TM, D), lambda i, d: (i, 0)),
            scratch_shapes=[pltpu.VMEM((2, 2, TM, D), F32), pltpu.SemaphoreType.DMA((2,))],
        ),
        out_shape=jax.ShapeDtypeStruct((out_rows, D), F32),
        compiler_params=_cparams(("arbitrary",), VMEM_LIMIT),
        name="combine",
        **kwargs,
    )(dest_flat, yb, x_all, route, mod, final_g.reshape(1, D))


def _moe_layer(x_all, h2, route, cnt, mod, final_g, w_gate, w_up, w_down, layer, seg_of, nblk, final):
    T = nblk * TM
    e_ids = route[:, 0:2].T.astype(jnp.int32)
    ranks = route[:, 4:6].T.astype(jnp.int32)
    counts = cnt[0, :MOE_EXPERTS].astype(jnp.int32)
    padded = (counts + MOE_BM - 1) // MOE_BM * MOE_BM
    pad_ends = jnp.cumsum(padded)
    pad_starts = pad_ends - padded
    expert = jnp.arange(MOE_EXPERTS, dtype=jnp.int32)[:, None, None]
    start_of = jnp.sum(jnp.where(e_ids[None] == expert, pad_starts[:, None, None], 0), axis=0)
    dest = (start_of + ranks).reshape(-1)
    pad_info = jnp.stack([pad_starts + counts, padded - counts,
                          jnp.broadcast_to(pad_ends[-1], (MOE_EXPERTS,))]).astype(jnp.int32)
    n_blocks = -(-(2 * T + MOE_EXPERTS * (MOE_BM - 1)) // MOE_BM)
    block_starts = jnp.arange(n_blocks, dtype=jnp.int32) * MOE_BM
    block_e = jnp.sum((block_starts[:, None] >= pad_ends[None, :]).astype(jnp.int32), axis=1)
    block_e = jnp.minimum(block_e, MOE_EXPERTS - 1)
    n_used = (pad_ends[-1:] // MOE_BM).astype(jnp.int32)
    plan = _block_plan(block_e, n_used, pad_starts, counts)
    xs = _dispatch(dest, pad_info, h2, n_blocks * MOE_BM, nblk)
    hmid = _moe1(plan, n_used, xs, w_gate, w_up, layer)
    yb = _moe2(plan, n_used, hmid, w_down, layer)
    return _combine(dest, yb, x_all, route, mod, final_g, seg_of, nblk, final)


def _rope_tables(S):
    pos = jnp.arange(S, dtype=jnp.int32)
    rows = (pos // GRID_W).astype(F32)
    cols = (pos % GRID_W).astype(F32)
    n_freq = HEAD_DIM // 4
    inv = 1.0 / (ROPE_THETA ** (jnp.arange(n_freq, dtype=F32) / n_freq))
    cr, sr = jnp.cos(rows[:, None] * inv), jnp.sin(rows[:, None] * inv)
    cc, sc = jnp.cos(cols[:, None] * inv), jnp.sin(cols[:, None] * inv)
    cos_t = jnp.concatenate([cr, cr, cc, cc], axis=1)
    sin_t = jnp.concatenate([-sr, sr, -sc, sc], axis=1)
    cos_t = jnp.concatenate([cos_t, jnp.ones((TM, HEAD_DIM), F32)], axis=0)
    sin_t = jnp.concatenate([sin_t, jnp.zeros((TM, HEAD_DIM), F32)], axis=0)
    return cos_t, sin_t


def kernel(x, c, ctx, c_ctx, ada_w, ada_b, norm1_g, norm2_g, w_in, w_out, att_q_norm, att_k_norm, conv_w, conv_b, lru_wr, lru_br, lru_wi, lru_bi, lru_lambda, na_rpb, router_wg, router_bg, router_we, router_be, moe_w_gate, moe_w_up, moe_w_down, final_g):
    B, S, D = x.shape
    C = ctx.shape[1]
    L = ada_w.shape[0]
    assert C == TM and S % TM == 0 and S // TM >= 3 and B + 1 <= 8
    nbs = S // TM
    nlat = B * nbs
    m_lat = B * S
    M = m_lat + B * C

    def seg_of(i):
        return jnp.minimum(i // nbs, B)

    x_rows, ctx_rows = x.reshape(m_lat, D), ctx.reshape(B * C, D)
    cvec = jnp.zeros((8, D), F32).at[:B].set(c).at[B].set(c_ctx)
    mod_all = _adaln(cvec, ada_w, ada_b)[:, :B + 1].reshape(L, B + 1, 6, D)
    cos_t, sin_t = _rope_tables(S)
    g_att = ATT_Q_HEADS // ATT_KV_HEADS

    out = None
    for l in range(L):
        last = l == L - 1
        mod = mod_all[l]
        proj = _inproj(x_rows, ctx_rows, mod, norm1_g[l], w_in[l].astype(BF16), B, S)

        q, k_all, v_all = _qkprep(proj, cos_t, sin_t, att_q_norm[l], att_k_norm[l], B, S)
        oa = _gqa_latent(q, k_all, v_all, B, S)

        sp = jax.nn.softplus(-lru_lambda[l].astype(F32))
        hrev = _lru_dir(proj, None, conv_w[l], conv_b[l], lru_wr[l, 1], lru_br[l, 1], lru_wi[l, 1],
                        lru_bi[l, 1], sp[1], B, S, reverse=True)
        ob = _lru_dir(proj, hrev, conv_w[l], conv_b[l], lru_wr[l, 0], lru_br[l, 0], lru_wi[l, 0],
                      lru_bi[l, 0], sp[0], B, S, reverse=False)

        tl, tr = _na_bias_tables(na_rpb[l])
        oc = _na_latent(proj, tl, tr, B, S)

        if not last:
            oa = _ctx_attn(
                q, lambda b, h: (nlat + b, h),
                k_all, pl.BlockSpec((None, TM, HEAD_DIM), lambda b, h: (b, nbs, h // g_att)),
                v_all, pl.BlockSpec((None, TM, HEAD_DIM), lambda b, h: (b, nbs, 2 * (h // g_att))),
                oa, B, ATT_Q_HEADS, nlat, None)
            oc = _ctx_attn(
                proj, lambda b, h: (nlat + b, COL_QN // HEAD_DIM + h),
                proj, pl.BlockSpec((TM, HEAD_DIM), lambda b, h: (nlat + b, COL_KN // HEAD_DIM + h)),
                proj, pl.BlockSpec((TM, HEAD_DIM), lambda b, h: (nlat + b, COL_VN // HEAD_DIM + h)),
                oc, B, NA_HEADS, nlat, HEAD_DIM ** -0.5)

        nblk = nlat if last else M // TM
        w_router = jnp.zeros((D, 128), F32).at[:, :MOE_GROUPS].set(router_wg[l])
        w_router = w_router.at[:, MOE_GROUPS:MOE_GROUPS + MOE_EXPERTS].set(router_we[l])
        b_router = jnp.zeros((1, 128), F32).at[0, :MOE_GROUPS].set(router_bg[l])
        b_router = b_router.at[0, MOE_GROUPS:MOE_GROUPS + MOE_EXPERTS].set(router_be[l])
        x_all, h2, route, cnt = _outproj(oa, ob, oc, x_rows, ctx_rows, mod, norm2_g[l], w_out[l].astype(BF16),
                                         w_router.astype(BF16), b_router, B, S, nblk * TM)
        res = _moe_layer(x_all, h2, route, cnt, mod, final_g, moe_w_gate, moe_w_up, moe_w_down, l,
                         seg_of, nblk, last)
        if last:
            out = res
        else:
            x_rows, ctx_rows = res, None
    return out.reshape(B, S, D)
```

```python
import functools

import numpy as np
import jax
import jax.numpy as jnp
from jax import lax
from jax.experimental import pallas as pl
from jax.experimental.pallas import tpu as pltpu

F32 = jnp.float32
BF16 = jnp.bfloat16

GRID_W = 64
HEAD_DIM = 128
ATT_Q_HEADS = 4
ATT_KV_HEADS = 2
LRU_WIDTH = 1024
LRU_BLOCKS = 8
LRU_C = 8.0
CONV_W = 4
NA_HEADS = 4
NA_ROWS = 8
NA_COLS = 16
ATT_Q_W = ATT_Q_HEADS * HEAD_DIM
ATT_KV_W = ATT_KV_HEADS * HEAD_DIM
NA_W = NA_HEADS * HEAD_DIM
MOE_GROUPS = 4
MOE_EXPERTS_PER_GROUP = 8
MOE_EXPERTS = MOE_GROUPS * MOE_EXPERTS_PER_GROUP
MOE_HIDDEN = 1024
ROPE_THETA = 10000.0
EPS = 1e-6
LOG2E = 1.4426950408889634

COL_QA = 0
COL_KA = COL_QA + ATT_Q_W
COL_VA = COL_KA + ATT_KV_W
COL_UB = COL_VA + ATT_KV_W
COL_GB = COL_UB + LRU_WIDTH
COL_QN = COL_GB + LRU_WIDTH
COL_KN = COL_QN + NA_W
COL_VN = COL_KN + NA_W
IN_WIDTH = COL_VN + NA_W

TM = 256
MOE_BM = 256
NA_QROWS = 4
NA_WROWS = 12
NEG_BIAS = -1e30
VMEM_LIMIT = 56 * 1024 * 1024


def _cparams(sem, vmem=None):
    return pltpu.CompilerParams(dimension_semantics=sem, vmem_limit_bytes=vmem)


def _rms(x):
    return x * lax.rsqrt(jnp.mean(x * x, axis=-1, keepdims=True) + EPS)


def _adaln_kernel(c_ref, w_ref, b_ref, o_ref):
    cv = c_ref[...]
    s = cv * jax.nn.sigmoid(cv)
    o_ref[0] = jnp.dot(s.astype(BF16), w_ref[0].astype(BF16), preferred_element_type=F32) + b_ref[0]


def _adaln(cvec, ada_w, ada_b):
    L, D, N = ada_w.shape
    tn = 1024
    return pl.pallas_call(
        _adaln_kernel,
        grid=(L, N // tn),
        in_specs=[
            pl.BlockSpec((8, D), lambda l, j: (0, 0)),
            pl.BlockSpec((1, D, tn), lambda l, j: (l, 0, j)),
            pl.BlockSpec((1, 1, tn), lambda l, j: (l, 0, j)),
        ],
        out_specs=pl.BlockSpec((1, 8, tn), lambda l, j: (l, 0, j)),
        out_shape=jax.ShapeDtypeStruct((L, 8, N), F32),
        compiler_params=_cparams(("parallel", "parallel"), VMEM_LIMIT),
        name="adaln",
    )(cvec, ada_w, ada_b.reshape(L, 1, N))


def _inproj_kernel(*refs, n_first, two_sources):
    if two_sources:
        x_ref, xb_ref, mod_ref, g_ref, w_ref, o_ref = refs
        x = jnp.where(pl.program_id(0) < n_first, x_ref[...], xb_ref[...])
    else:
        x_ref, mod_ref, g_ref, w_ref, o_ref = refs
        x = x_ref[...]
    h = _rms(x) * g_ref[...]
    h = h * (1.0 + mod_ref[0, 1:2, :]) + mod_ref[0, 0:1, :]
    o_ref[...] = jnp.dot(h.astype(BF16), w_ref[...], preferred_element_type=F32)


def _inproj(x_rows, ctx_rows, mod, g1, w_in_bf, B, S):
    two = ctx_rows is not None
    D = x_rows.shape[1]
    M = x_rows.shape[0] + (ctx_rows.shape[0] if two else 0)
    N = w_in_bf.shape[1]
    tm = 512 if (S % 512 == 0 and (M - B * S) % 512 == 0) else TM
    per_seq = S // tm
    n_first = B * per_seq
    if two:
        n_ctx_blocks = ctx_rows.shape[0] // tm
        x_specs = [pl.BlockSpec((tm, D), lambda i: (jnp.minimum(i, n_first - 1), 0)),
                   pl.BlockSpec((tm, D), lambda i: (jnp.maximum(i - n_first, 0), 0),
                                pipeline_mode=pl.Buffered(1 if n_ctx_blocks == 1 else 2))]
        x_args = [x_rows, ctx_rows]
    else:
        x_specs = [pl.BlockSpec((tm, D), lambda i: (i, 0))]
        x_args = [x_rows]
    return pl.pallas_call(
        functools.partial(_inproj_kernel, n_first=n_first, two_sources=two),
        grid=(M // tm,),
        in_specs=[
            *x_specs,
            pl.BlockSpec((1, 6, D), lambda i: (jnp.minimum(i // per_seq, B), 0, 0)),
            pl.BlockSpec((1, D), lambda i: (0, 0)),
            pl.BlockSpec((D, N), lambda i: (0, 0), pipeline_mode=pl.Buffered(1)),
        ],
        out_specs=pl.BlockSpec((tm, N), lambda i: (i, 0)),
        out_shape=jax.ShapeDtypeStruct((M, N), F32),
        compiler_params=_cparams(("parallel",), VMEM_LIMIT),
        name="inproj",
    )(*x_args, mod, g1.reshape(1, D), w_in_bf)


def _norm_rope(t, g, cos, sin):
    y = _rms(t) * g
    lane = lax.broadcasted_iota(jnp.int32, y.shape, 1)
    first_half = (lane % 64) < 32
    partner = jnp.where(first_half, pltpu.roll(y, 96, 1), pltpu.roll(y, 32, 1))
    return y * cos + partner * sin


def _qkprep_kernel(p_ref, cos_ref, sin_ref, qg_ref, kg_ref, q_ref, k_ref, v_ref):
    cos = cos_ref[...]
    sin = sin_ref[...]
    scale = HEAD_DIM ** -0.5 * LOG2E
    for h in range(ATT_Q_HEADS):
        t = p_ref[:, COL_QA + h * HEAD_DIM:COL_QA + (h + 1) * HEAD_DIM]
        q_ref[:, h * HEAD_DIM:(h + 1) * HEAD_DIM] = (_norm_rope(t, qg_ref[...], cos, sin) * scale).astype(BF16)
    for h in range(ATT_KV_HEADS):
        t = p_ref[:, COL_KA + h * HEAD_DIM:COL_KA + (h + 1) * HEAD_DIM]
        k_ref[0, :, h * HEAD_DIM:(h + 1) * HEAD_DIM] = _norm_rope(t, kg_ref[...], cos, sin).astype(BF16)
        v_ref[0, :, 2 * h * HEAD_DIM:(2 * h + 1) * HEAD_DIM] = (
            p_ref[:, COL_VA + h * HEAD_DIM:COL_VA + (h + 1) * HEAD_DIM].astype(BF16))
        v_ref[0, :, (2 * h + 1) * HEAD_DIM:(2 * h + 2) * HEAD_DIM] = jnp.ones((p_ref.shape[0], HEAD_DIM), BF16)


def _qkprep(proj, cos_t, sin_t, qg, kg, B, S):
    M = proj.shape[0]
    nbs = S // TM
    nlat = B * nbs

    def b_of(i):
        return jnp.where(i < nlat, i // nbs, i - nlat)

    def pos_of(i):
        return jnp.where(i < nlat, i % nbs, nbs)

    kl = S + TM
    return pl.pallas_call(
        _qkprep_kernel,
        grid=(M // TM,),
        in_specs=[
            pl.BlockSpec((TM, COL_UB), lambda i: (i, 0)),
            pl.BlockSpec((TM, HEAD_DIM), lambda i: (pos_of(i), 0)),
            pl.BlockSpec((TM, HEAD_DIM), lambda i: (pos_of(i), 0)),
            pl.BlockSpec((1, HEAD_DIM), lambda i: (0, 0)),
            pl.BlockSpec((1, HEAD_DIM), lambda i: (0, 0)),
        ],
        out_specs=[
            pl.BlockSpec((TM, ATT_Q_W), lambda i: (i, 0)),
            pl.BlockSpec((1, TM, ATT_KV_W), lambda i: (b_of(i), pos_of(i), 0)),
            pl.BlockSpec((1, TM, 2 * ATT_KV_W), lambda i: (b_of(i), pos_of(i), 0)),
        ],
        out_shape=[
            jax.ShapeDtypeStruct((M, ATT_Q_W), BF16),
            jax.ShapeDtypeStruct((B, kl, ATT_KV_W), BF16),
            jax.ShapeDtypeStruct((B, kl, 2 * ATT_KV_W), BF16),
        ],
        compiler_params=_cparams(("parallel",), VMEM_LIMIT),
        name="qkprep",
    )(proj, cos_t, sin_t, qg.reshape(1, HEAD_DIM), kg.reshape(1, HEAD_DIM))


def _gqa_kernel(q_ref, k_ref, v_ref, o_in_ref, o_ref, q2_ref, sa_ref, sb_ref, m_ref, acc_ref, *, tk):
    del o_in_ref
    tq = q_ref.shape[0]
    q2_ref[0:tq, :] = q_ref[:, :HEAD_DIM]
    q2_ref[tq:, :] = q_ref[:, HEAD_DIM:]
    nk = k_ref.shape[1] // tk
    m_ref[...] = jnp.full(m_ref.shape, -jnp.inf, F32)
    acc_ref[...] = jnp.zeros(acc_ref.shape, F32)

    def scores(c, s_ref):
        off = pl.multiple_of(c * tk, tk)
        s_ref[...] = lax.dot_general(q2_ref[...], k_ref[0, pl.ds(off, tk), :], (((1,), (1,)), ((), ())),
                                     preferred_element_type=F32)

    def update(c, s_ref):
        off = pl.multiple_of(c * tk, tk)
        vc = v_ref[0, pl.ds(off, tk), :]
        for hh in range(2):
            rows = slice(hh * tq, (hh + 1) * tq)
            m = m_ref[rows, :]
            m_new = jnp.maximum(m, jnp.max(s_ref[rows, :], axis=-1, keepdims=True))
            m_ref[rows, :] = m_new
            p = jnp.exp2((s_ref[rows, :] - jnp.concatenate([m_new] * (tk // HEAD_DIM), axis=1)).astype(BF16))
            alpha = jnp.exp2(m - m_new)
            acc_ref[rows, :] = (jnp.concatenate([alpha, alpha], axis=1) * acc_ref[rows, :]
                                + jnp.dot(p, vc, preferred_element_type=F32))

    scores(0, sa_ref)
    n_pairs = (nk - 1) // 2

    def body(j, carry):
        scores(2 * j + 1, sb_ref)
        update(2 * j, sa_ref)
        scores(2 * j + 2, sa_ref)
        update(2 * j + 1, sb_ref)
        return carry

    lax.fori_loop(0, n_pairs, body, 0)
    if nk % 2 == 0:
        scores(nk - 1, sb_ref)
        update(nk - 2, sa_ref)
        update(nk - 1, sb_ref)
    else:
        update(nk - 1, sa_ref)
    acc = acc_ref[...]
    o = acc[:, :HEAD_DIM] / acc[:, HEAD_DIM:]
    o_ref[:, :HEAD_DIM] = o[:tq].astype(BF16)
    o_ref[:, HEAD_DIM:] = o[tq:].astype(BF16)


def _gqa_latent(q, k_all, v_all, B, S):
    M = q.shape[0]
    kl = k_all.shape[1]
    tq = 512 if S % 512 == 0 else TM
    nbs = S // tq
    tk = 768 if kl % 768 == 0 else TM
    g = ATT_Q_HEADS // ATT_KV_HEADS
    return pl.pallas_call(
        functools.partial(_gqa_kernel, tk=tk),
        grid=(B, ATT_KV_HEADS, nbs),
        in_specs=[
            pl.BlockSpec((tq, g * HEAD_DIM), lambda b, h, i: (b * nbs + i, h)),
            pl.BlockSpec((1, kl, HEAD_DIM), lambda b, h, i: (b, 0, h)),
            pl.BlockSpec((1, kl, 2 * HEAD_DIM), lambda b, h, i: (b, 0, h)),
            pl.BlockSpec(memory_space=pl.ANY),
        ],
        out_specs=pl.BlockSpec((tq, g * HEAD_DIM), lambda b, h, i: (b * nbs + i, h)),
        out_shape=jax.ShapeDtypeStruct((M, ATT_Q_W), BF16),
        input_output_aliases={3: 0},
        scratch_shapes=[
            pltpu.VMEM((g * tq, HEAD_DIM), BF16),
            pltpu.VMEM((g * tq, tk), F32),
            pltpu.VMEM((g * tq, tk), F32),
            pltpu.VMEM((g * tq, HEAD_DIM), F32),
            pltpu.VMEM((g * tq, 2 * HEAD_DIM), F32),
        ],
        compiler_params=_cparams(("parallel", "parallel", "parallel"), VMEM_LIMIT),
        name="gqa_latent",
    )(q, k_all, v_all, jnp.zeros((M, ATT_Q_W), BF16))


def _ctx_attn_kernel(q_ref, k_ref, v_ref, o_in_ref, o_ref, *, scale):
    del o_in_ref
    if scale is None:
        q = q_ref[...]
    else:
        q = (q_ref[...] * (scale * LOG2E)).astype(BF16)
    k = k_ref[...].astype(BF16)
    s = lax.dot_general(q, k, (((1,), (1,)), ((), ())), preferred_element_type=F32)
    p = jnp.exp2(s - jnp.max(s, axis=-1, keepdims=True))
    l = jnp.sum(p, axis=-1, keepdims=True)
    pv = jnp.dot(p.astype(BF16), v_ref[...].astype(BF16), preferred_element_type=F32)
    o_ref[...] = (pv / l).astype(BF16)


def _ctx_attn(q_arr, q_map, k_arr, k_spec, v_arr, v_spec, o_arr, B, n_heads, ctx_blk0, scale):
    return pl.pallas_call(
        functools.partial(_ctx_attn_kernel, scale=scale),
        grid=(B, n_heads),
        in_specs=[
            pl.BlockSpec((TM, HEAD_DIM), q_map),
            k_spec,
            v_spec,
            pl.BlockSpec(memory_space=pl.ANY),
        ],
        out_specs=pl.BlockSpec((TM, HEAD_DIM), lambda b, h: (ctx_blk0 + b, h)),
        out_shape=jax.ShapeDtypeStruct(o_arr.shape, o_arr.dtype),
        input_output_aliases={3: 0},
        compiler_params=_cparams(("parallel", "parallel"), VMEM_LIMIT),
        name="ctx_attn",
    )(q_arr, k_arr, v_arr, o_arr)


def _lru_kernel(u_ref, up_ref, un_ref, *rest, reverse, final):
    if final:
        (g_ref, hrev_ref, cw_ref, cb_ref, wr_ref, br_ref, wi_ref, bi_ref, sp_ref,
         o_ref, xpad, a_scr, b_scr, h_scr, hst) = rest
    else:
        (cw_ref, cb_ref, wr_ref, br_ref, wi_ref, bi_ref, sp_ref,
         o_ref, xpad, a_scr, b_scr, h_scr, hst) = rest
    s = pl.program_id(1)
    ns = pl.num_programs(1)
    T = u_ref.shape[0]
    W = u_ref.shape[1]

    @pl.when(s == 0)
    def _():
        hst[...] = jnp.zeros_like(hst)

    if reverse:
        j = ns - 1 - s
    else:
        j = s - 1
    is_first = jnp.logical_or(s == 0, j == 0)
    is_last = jnp.logical_or(s == 0, j == ns - 2)
    prev_rows = jnp.where(is_first, 0.0, up_ref[...])
    next_rows = jnp.where(is_last, 0.0, un_ref[...])
    xpad[0:8, :] = prev_rows
    xpad[8:8 + T, :] = u_ref[...]
    xpad[8 + T:16 + T, :] = next_rows
    u = (cb_ref[...]
         + xpad[7:7 + T, :] * cw_ref[0:1, :]
         + xpad[8:8 + T, :] * cw_ref[1:2, :]
         + xpad[9:9 + T, :] * cw_ref[2:3, :]
         + xpad[10:10 + T, :] * cw_ref[3:4, :])

    ub = u.astype(BF16)
    nb = LRU_WIDTH // LRU_BLOCKS
    for n in range(LRU_BLOCKS):
        sl = slice(n * nb, (n + 1) * nb)
        un_ = ub[:, sl]
        tr = jnp.tanh(jnp.dot(un_, wr_ref[n], preferred_element_type=F32) + br_ref[:, sl])
        ti = jnp.tanh(jnp.dot(un_, wi_ref[n], preferred_element_type=F32) + bi_ref[:, sl])
        ig = 0.5 * ti + 0.5
        log_a = sp_ref[:, sl] * tr + sp_ref[:, sl]
        a = jnp.exp(log_a)
        a_scr[:, sl] = a
        one_minus_a2 = -jnp.tanh(log_a) * (a * a + 1.0)
        root = one_minus_a2 * lax.rsqrt(jnp.maximum(one_minus_a2, 1e-30))
        b_scr[:, sl] = root * (ig * u[:, sl])

    def step(t, h):
        row = (T - 1 - t) if reverse else t
        h = a_scr[pl.ds(row, 1), :] * h + b_scr[pl.ds(row, 1), :]
        h_scr[pl.ds(row, 1), :] = h
        return h

    hst[...] = lax.fori_loop(0, T, step, hst[...], unroll=8)

    if final:
        y = h_scr[...] + hrev_ref[...]
        o_ref[...] = (y * jax.nn.gelu(g_ref[...])).astype(o_ref.dtype)
    else:
        o_ref[...] = h_scr[...]
    del W


def _lru_dir(proj, hrev, conv_w, conv_b, wr, br, wi, bi, sp, B, S, reverse):
    M = proj.shape[0]
    nbs = S // TM
    nlat = B * nbs
    final = hrev is not None
    ub_blk = COL_UB // LRU_WIDTH
    gb_blk = COL_GB // LRU_WIDTH
    r8 = TM // 8
    n8 = M // 8

    def blk(b, s):
        j = (nbs - s) if reverse else (s - 1)
        return jnp.where(s == 0, nlat + b, b * nbs + j)

    in_specs = [
        pl.BlockSpec((TM, LRU_WIDTH), lambda b, s: (blk(b, s), ub_blk)),
        pl.BlockSpec((8, LRU_WIDTH), lambda b, s: (jnp.maximum(blk(b, s) * r8 - 1, 0), ub_blk)),
        pl.BlockSpec((8, LRU_WIDTH), lambda b, s: (jnp.minimum((blk(b, s) + 1) * r8, n8 - 1), ub_blk)),
    ]
    args = [proj, proj, proj]
    if final:
        in_specs += [
            pl.BlockSpec((TM, LRU_WIDTH), lambda b, s: (blk(b, s), gb_blk)),
            pl.BlockSpec((TM, LRU_WIDTH), lambda b, s: (blk(b, s), 0)),
        ]
        args += [proj, hrev]
    const2 = lambda b, s: (0, 0)
    in_specs += [
        pl.BlockSpec((CONV_W, LRU_WIDTH), const2),
        pl.BlockSpec((1, LRU_WIDTH), const2),
        pl.BlockSpec((LRU_BLOCKS, LRU_WIDTH // LRU_BLOCKS, LRU_WIDTH // LRU_BLOCKS), lambda b, s: (0, 0, 0)),
        pl.BlockSpec((1, LRU_WIDTH), const2),
        pl.BlockSpec((LRU_BLOCKS, LRU_WIDTH // LRU_BLOCKS, LRU_WIDTH // LRU_BLOCKS), lambda b, s: (0, 0, 0)),
        pl.BlockSpec((1, LRU_WIDTH), const2),
        pl.BlockSpec((1, LRU_WIDTH), const2),
    ]
    args += [conv_w, conv_b.reshape(1, -1), (0.5 * wr).astype(BF16), 0.5 * br.reshape(1, -1),
             (0.5 * wi).astype(BF16), 0.5 * bi.reshape(1, -1), (-0.5 * LRU_C) * sp.reshape(1, -1)]
    return pl.pallas_call(
        functools.partial(_lru_kernel, reverse=reverse, final=final),
        grid=(B, nbs + 1),
        in_specs=in_specs,
        out_specs=pl.BlockSpec((TM, LRU_WIDTH), lambda b, s: (blk(b, s), 0)),
        out_shape=jax.ShapeDtypeStruct((M, LRU_WIDTH), BF16 if final else F32),
        scratch_shapes=[
            pltpu.VMEM((TM + 16, LRU_WIDTH), F32),
            pltpu.VMEM((TM, LRU_WIDTH), F32),
            pltpu.VMEM((TM, LRU_WIDTH), F32),
            pltpu.VMEM((TM, LRU_WIDTH), F32),
            pltpu.VMEM((1, LRU_WIDTH), F32),
        ],
        compiler_params=_cparams(("parallel", "arbitrary"), VMEM_LIMIT),
        name="lru_fwd" if final else "lru_rev",
    )(*args)


def _na_kernel(q_ref, k0_ref, k1_ref, k2_ref, v0_ref, v1_ref, v2_ref, kc_ref, vc_ref, tl_ref, tr_ref,
               o_in_ref, o_ref, s_scr, *, n_rows):
    del o_in_ref
    i = pl.program_id(1)
    nbs = n_rows // NA_QROWS
    r0 = NA_QROWS * i
    ws = NA_QROWS * jnp.clip(i - 1, 0, nbs - 3)
    scale = HEAD_DIM ** -0.5 * LOG2E
    nkeys = s_scr.shape[1]
    ones = jnp.ones((nkeys, HEAD_DIM), BF16)
    for h in range(NA_HEADS):
        hs = slice(h * HEAD_DIM, (h + 1) * HEAD_DIM)
        qh = (q_ref[:, hs] * scale).astype(BF16)
        kh = jnp.concatenate([k0_ref[:, hs], k1_ref[:, hs], k2_ref[:, hs], kc_ref[:, hs]], axis=0).astype(BF16)
        vh = jnp.concatenate([v0_ref[:, hs], v1_ref[:, hs], v2_ref[:, hs], vc_ref[:, hs]], axis=0).astype(BF16)
        vh = jnp.concatenate([vh, ones], axis=1)
        s_scr[...] = lax.dot_general(qh, kh, (((1,), (1,)), ((), ())), preferred_element_type=F32)
        for qr in range(NA_QROWS):
            r = r0 + qr
            rs = jnp.clip(r - NA_ROWS // 2, 0, n_rows - NA_ROWS)

            def tile_of(a):
                kr = ws + a
                valid = jnp.logical_and(kr >= rs, kr < rs + NA_ROWS)
                return jnp.where(valid, kr - r + (NA_ROWS - 1), 2 * NA_ROWS - 1)

            for pr in range(NA_WROWS // 2):
                bias = tl_ref[h, tile_of(2 * pr)] + tr_ref[h, tile_of(2 * pr + 1)]
                rsl = slice(qr * GRID_W, (qr + 1) * GRID_W)
                csl = slice(pr * 2 * GRID_W, (pr + 1) * 2 * GRID_W)
                s_scr[rsl, csl] = s_scr[rsl, csl] + bias
        s = s_scr[...]
        p = jnp.exp2((s - jnp.max(s, axis=-1, keepdims=True)).astype(BF16))
        pv = jnp.dot(p, vh, preferred_element_type=F32)
        o_ref[:, hs] = (pv[:, :HEAD_DIM] / pv[:, HEAD_DIM:]).astype(BF16)


def _na_latent(proj, tl, tr, B, S):
    M = proj.shape[0]
    nbs = S // TM
    nlat = B * nbs
    n_rows = S // GRID_W
    qb, kb, vb = COL_QN // NA_W, COL_KN // NA_W, COL_VN // NA_W

    def wblk(b, i, t):
        return b * nbs + jnp.clip(i - 1, 0, nbs - 3) + t

    blk = (TM, NA_W)
    in_specs = [pl.BlockSpec(blk, lambda b, i: (b * nbs + i, qb))]
    in_specs += [pl.BlockSpec(blk, functools.partial(lambda b, i, t: (wblk(b, i, t), kb), t=t)) for t in range(3)]
    in_specs += [pl.BlockSpec(blk, functools.partial(lambda b, i, t: (wblk(b, i, t), vb), t=t)) for t in range(3)]
    in_specs += [
        pl.BlockSpec(blk, lambda b, i: (nlat + b, kb)),
        pl.BlockSpec(blk, lambda b, i: (nlat + b, vb)),
        pl.BlockSpec(tl.shape, lambda b, i: (0, 0, 0, 0)),
        pl.BlockSpec(tr.shape, lambda b, i: (0, 0, 0, 0)),
        pl.BlockSpec(memory_space=pl.ANY),
    ]
    return pl.pallas_call(
        functools.partial(_na_kernel, n_rows=n_rows),
        grid=(B, nbs),
        in_specs=in_specs,
        out_specs=pl.BlockSpec(blk, lambda b, i: (b * nbs + i, 0)),
        out_shape=jax.ShapeDtypeStruct((M, NA_W), BF16),
        input_output_aliases={11: 0},
        scratch_shapes=[pltpu.VMEM((TM, NA_WROWS * GRID_W + TM), F32)],
        compiler_params=_cparams(("parallel", "parallel"), VMEM_LIMIT),
        name="na_latent",
    )(proj, proj, proj, proj, proj, proj, proj, proj, proj, tl, tr, jnp.zeros((M, NA_W), BF16))


def _na_bias_tables(rpb):
    j = np.arange(GRID_W)
    cs = np.clip(j - NA_COLS // 2, 0, GRID_W - NA_COLS)
    kc = np.arange(GRID_W)
    inside = (kc[None, :] >= cs[:, None]) & (kc[None, :] < cs[:, None] + NA_COLS)
    dc = kc[None, :] - j[:, None] + (NA_COLS - 1)
    sel = (dc[:, :, None] == np.arange(2 * NA_COLS - 1)[None, None, :]) & inside[:, :, None]
    t = jnp.einsum("hrd,jkd->hrjk", rpb * LOG2E, jnp.asarray(sel, F32), precision=lax.Precision.HIGHEST)
    t = jnp.where(inside[None, None], t, NEG_BIAS)
    t = jnp.concatenate([t, jnp.full_like(t[:, :1], NEG_BIAS)], axis=1)
    z = jnp.zeros_like(t)
    return jnp.concatenate([t, z], axis=-1), jnp.concatenate([z, t], axis=-1)


def _outproj_kernel(*refs, n_first, two_sources):
    if two_sources:
        (oa_ref, ob_ref, oc_ref, x_ref, xb_ref, mod_ref, g_ref, w_ref, wr_ref, br_ref,
         xo_ref, h2_ref, rt_ref, cnt_ref, carry) = refs
    else:
        (oa_ref, ob_ref, oc_ref, x_ref, mod_ref, g_ref, w_ref, wr_ref, br_ref,
         xo_ref, h2_ref, rt_ref, cnt_ref, carry) = refs
    i = pl.program_id(0)

    @pl.when(i == 0)
    def _():
        carry[...] = jnp.zeros_like(carry)

    k1 = ATT_Q_W
    k2 = k1 + LRU_WIDTH
    acc = jnp.dot(oa_ref[...], w_ref[0:k1, :], preferred_element_type=F32)
    acc = acc + jnp.dot(ob_ref[...], w_ref[k1:k2, :], preferred_element_type=F32)
    acc = acc + jnp.dot(oc_ref[...], w_ref[k2:, :], preferred_element_type=F32)
    x_in = x_ref[...]
    if two_sources:
        x_in = jnp.where(i < n_first, x_in, xb_ref[...])
    xn = x_in + mod_ref[0, 2:3, :] * acc
    xo_ref[...] = xn
    h2 = _rms(xn) * g_ref[...]
    h2 = h2 * (1.0 + mod_ref[0, 4:5, :]) + mod_ref[0, 3:4, :]
    h2_ref[...] = _pack_bf16_pair(h2)
    logits = jnp.dot(h2.astype(BF16), wr_ref[...], preferred_element_type=F32) + br_ref[...]
    route = _route_math(logits)
    rt_ref[...] = _rank_math(route, carry)
    cnt_ref[...] = jnp.broadcast_to(carry[...], cnt_ref.shape)


def _pack_bf16_pair(x):
    w = x.shape[1] // 2
    lo = pltpu.bitcast(x[:, :w].astype(BF16).astype(F32), jnp.uint32)
    hi = pltpu.bitcast(x[:, w:].astype(BF16).astype(F32), jnp.uint32)
    return hi | (lo >> 16)


def _unpack_bf16_pair(words):
    lo = pltpu.bitcast(words << 16, F32)
    hi = pltpu.bitcast(words & jnp.uint32(0xFFFF0000), F32)
    return jnp.concatenate([lo, hi], axis=1).astype(BF16)


def _outproj(oa, ob, oc, x_rows, ctx_rows, mod, g2, w_out_bf, w_router, b_router, B, S, n_rows):
    M = oa.shape[0]
    D = x_rows.shape[1]
    two = ctx_rows is not None
    tm = 512 if (S % 512 == 0 and (M - B * S) % 512 == 0) else TM
    per_seq = S // tm
    n_first = B * per_seq
    packed = jax.eval_shape(_pack_bf16_pair, jax.ShapeDtypeStruct((tm, D), F32))
    if two:
        x_specs = [pl.BlockSpec((tm, D), lambda i: (jnp.minimum(i, n_first - 1), 0)),
                   pl.BlockSpec((tm, D), lambda i: (jnp.maximum(i - n_first, 0), 0))]
        x_args = [x_rows, ctx_rows]
        alias = {}
    else:
        x_specs = [pl.BlockSpec((tm, D), lambda i: (i, 0))]
        x_args = [x_rows]
        alias = {3: 0}
    return pl.pallas_call(
        functools.partial(_outproj_kernel, n_first=n_first, two_sources=two),
        grid=(n_rows // tm,),
        in_specs=[
            pl.BlockSpec((tm, ATT_Q_W), lambda i: (i, 0)),
            pl.BlockSpec((tm, LRU_WIDTH), lambda i: (i, 0)),
            pl.BlockSpec((tm, NA_W), lambda i: (i, 0)),
            *x_specs,
            pl.BlockSpec((1, 6, D), lambda i: (jnp.minimum(i // per_seq, B), 0, 0)),
            pl.BlockSpec((1, D), lambda i: (0, 0)),
            pl.BlockSpec(w_out_bf.shape, lambda i: (0, 0)),
            pl.BlockSpec(w_router.shape, lambda i: (0, 0)),
            pl.BlockSpec((1, 128), lambda i: (0, 0)),
        ],
        out_specs=[
            pl.BlockSpec((tm, D), lambda i: (i, 0)),
            pl.BlockSpec(packed.shape, lambda i: (i, 0)),
            pl.BlockSpec((tm, 128), lambda i: (i, 0)),
            pl.BlockSpec((8, 128), lambda i: (0, 0)),
        ],
        out_shape=[
            jax.ShapeDtypeStruct((M, D), F32),
            jax.ShapeDtypeStruct((n_rows, packed.shape[1]), packed.dtype),
            jax.ShapeDtypeStruct((n_rows, 128), F32),
            jax.ShapeDtypeStruct((8, 128), F32),
        ],
        input_output_aliases=alias,
        scratch_shapes=[pltpu.VMEM((1, 128), F32)],
        compiler_params=_cparams(("arbitrary",), VMEM_LIMIT),
        name="outproj",
    )(oa, ob, oc, *x_args, mod, g2.reshape(1, D), w_out_bf, w_router, b_router)


def _route_math(x):
    lane = lax.broadcasted_iota(jnp.int32, x.shape, 1)
    big = jnp.int32(1 << 20)
    ninf = -jnp.inf

    def first_argmax(vals):
        m = jnp.max(vals, axis=-1, keepdims=True)
        idx = jnp.min(jnp.where(vals == m, lane, big), axis=-1, keepdims=True)
        return m, idx

    lg = jnp.where(lane < MOE_GROUPS, x, ninf)
    mg, g_star = first_argmax(lg)
    g_gate = 1.0 / jnp.sum(jnp.exp(lg - mg), axis=-1, keepdims=True)
    lo = MOE_GROUPS + MOE_EXPERTS_PER_GROUP * g_star
    le = jnp.where(jnp.logical_and(lane >= lo, lane < lo + MOE_EXPERTS_PER_GROUP), x, ninf)
    v1, i1 = first_argmax(le)
    v2, i2 = first_argmax(jnp.where(lane == i1, ninf, le))
    e21 = jnp.exp(v2 - v1)
    w1 = g_gate / (1.0 + e21)
    w2 = g_gate * e21 / (1.0 + e21)
    e1 = (i1 - MOE_GROUPS).astype(F32)
    e2 = (i2 - MOE_GROUPS).astype(F32)
    return jnp.where(lane == 0, e1, jnp.where(lane == 1, e2, jnp.where(lane == 2, w1,
                     jnp.where(lane == 3, w2, 0.0))))


def _rank_math(r, carry):
    lane = lax.broadcasted_iota(jnp.int32, r.shape, 1).astype(F32)
    oh0 = jnp.where(lane == r[:, 0:1], 1.0, 0.0)
    oh1 = jnp.where(lane == r[:, 1:2], 1.0, 0.0)
    oh = oh0 + oh1
    n = r.shape[0]
    row = lax.broadcasted_iota(jnp.int32, (n, n), 0)
    col = lax.broadcasted_iota(jnp.int32, (n, n), 1)
    tri = jnp.where(col < row, 1.0, 0.0).astype(BF16)
    tot = carry[...] + jnp.dot(tri, oh.astype(BF16), preferred_element_type=F32)
    rank0 = jnp.sum(oh0 * tot, axis=-1, keepdims=True)
    rank1 = jnp.sum(oh1 * tot, axis=-1, keepdims=True)
    lane_i = lax.broadcasted_iota(jnp.int32, r.shape, 1)
    carry[...] = carry[...] + jnp.sum(oh, axis=0, keepdims=True)
    return jnp.where(lane_i == 4, rank0, jnp.where(lane_i == 5, rank1, r))


def _dispatch_kernel(dest_ref, pad_ref, h_hbm, xs_hbm, hbuf, zeros, fsem, rsem, zsem, *, n_tok):
    i = pl.program_id(0)

    @pl.when(i == 0)
    def _():
        zeros[...] = jnp.zeros_like(zeros)
        pieces = [1 << b for b in range(MOE_BM.bit_length() - 2, 2, -1)]

        def copies(e):
            start, n = pad_ref[0, e], pad_ref[1, e]
            head = jnp.minimum((-start) & 7, n)
            out = []
            for r in range(7):
                out.append((r < head, pltpu.make_async_copy(zeros.at[pl.ds(0, 1)],
                                                            xs_hbm.at[pl.ds(start + r, 1)], zsem)))
            rest = n - head
            for p in pieces:
                at = pl.multiple_of(start + head + (rest & ~(2 * p - 1)), 8)
                out.append(((rest & p) != 0, pltpu.make_async_copy(zeros.at[pl.ds(0, p)],
                                                                   xs_hbm.at[pl.ds(at, p)], zsem)))
            return out

        def fill(e, c):
            for on, cp in copies(e):
                @pl.when(on)
                def _():
                    cp.start()
            return c

        def drain(e, c):
            for on, cp in copies(e):
                @pl.when(on)
                def _():
                    cp.wait()
            return c

        lax.fori_loop(0, MOE_EXPERTS, fill, 0)
        lax.fori_loop(0, MOE_EXPERTS, drain, 0)

        half = zeros.shape[0]
        n_tail = (xs_hbm.shape[0] - pad_ref[2, 0]) // half

        def tail_copy(j):
            at = pl.multiple_of(pad_ref[2, 0] + j * half, half)
            return pltpu.make_async_copy(zeros, xs_hbm.at[pl.ds(at, half)], zsem)

        def tail_fill(j, c):
            tail_copy(j).start()
            return c

        def tail_drain(j, c):
            tail_copy(j).wait()
            return c

        lax.fori_loop(0, n_tail, tail_fill, 0)
        lax.fori_loop(0, n_tail, tail_drain, 0)

    n = pl.num_programs(0)
    slot = i % 3
    nslot = (i + 1) % 3

    def fetch(blk, s):
        return pltpu.make_async_copy(h_hbm.at[pl.ds(pl.multiple_of(blk * TM, TM), TM)], hbuf.at[s], fsem.at[s])

    def rows_done(s):
        for k in range(2):
            pltpu.make_async_copy(hbuf.at[s], xs_hbm.at[pl.ds(0, TM)], rsem.at[s]).wait()

    @pl.when(i == 0)
    def _():
        fetch(0, 0).start()

    @pl.when(i >= 2)
    def _():
        rows_done(nslot)

    @pl.when(i + 1 < n)
    def _():
        fetch(i + 1, nslot).start()

    fetch(i, slot).wait()

    def issue(t, c):
        tok = i * TM + t
        for k in range(2):
            pltpu.make_async_copy(hbuf.at[slot, pl.ds(t, 1)], xs_hbm.at[pl.ds(dest_ref[k * n_tok + tok], 1)],
                                  rsem.at[slot]).start()
        return c

    lax.fori_loop(0, TM, issue, 0, unroll=8)

    @pl.when(i == n - 1)
    def _():
        rows_done(slot)

    @pl.when(jnp.logical_and(i == n - 1, i >= 1))
    def _():
        rows_done((i + 2) % 3)


def _dispatch(dest_flat, pad_info, h2, n_rows, nblk):
    W = h2.shape[1]
    return pl.pallas_call(
        functools.partial(_dispatch_kernel, n_tok=nblk * TM),
        grid_spec=pltpu.PrefetchScalarGridSpec(
            num_scalar_prefetch=2,
            grid=(nblk,),
            in_specs=[pl.BlockSpec(memory_space=pl.ANY)],
            out_specs=pl.BlockSpec(memory_space=pl.ANY),
            scratch_shapes=[pltpu.VMEM((3, TM, W), h2.dtype), pltpu.VMEM((MOE_BM // 2, W), h2.dtype),
                            pltpu.SemaphoreType.DMA((3,)), pltpu.SemaphoreType.DMA((3,)),
                            pltpu.SemaphoreType.DMA(())],
        ),
        out_shape=jax.ShapeDtypeStruct((n_rows, W), h2.dtype),
        compiler_params=pltpu.CompilerParams(dimension_semantics=("arbitrary",), has_side_effects=True),
        name="dispatch",
    )(dest_flat, pad_info, h2)


PLAN_EXPERT, PLAN_FIRST, PLAN_SLOT, PLAN_NEXT, PLAN_HALF = 0, 1, 2, 3, 4


def _block_plan(block_e, n_used, pad_starts, counts):
    nb = block_e.shape[0]
    idx = jnp.arange(nb, dtype=jnp.int32)
    is_first = (idx < n_used[0]) & ((idx == 0) | (block_e != jnp.roll(block_e, 1)))
    seg = jnp.cumsum(is_first.astype(jnp.int32)) - 1
    first_pos = jnp.where(is_first, idx, nb)
    at_or_after = jnp.flip(lax.cummin(jnp.flip(first_pos)))
    next_first = jnp.concatenate([at_or_after[1:], jnp.full((1,), nb, jnp.int32)])
    next_e = jnp.where(next_first < nb, block_e[jnp.minimum(next_first, nb - 1)], -1)
    valid = counts[block_e] - (idx * MOE_BM - pad_starts[block_e])
    half_only = (valid <= MOE_BM // 2).astype(jnp.int32)
    return jnp.stack([block_e, is_first.astype(jnp.int32), seg % 2, next_e, half_only]).astype(jnp.int32)


def _expert_weights_step(i, plan_ref, w_hbms, wbuf, wbf, sem, layer):
    def copies(e, slot):
        return [pltpu.make_async_copy(w.at[layer, e], wbuf.at[slot, n], sem.at[slot, n])
                for n, w in enumerate(w_hbms)]

    @pl.when(plan_ref[PLAN_FIRST, i] == 1)
    def _():
        slot = plan_ref[PLAN_SLOT, i]

        @pl.when(i == 0)
        def _():
            for c in copies(plan_ref[PLAN_EXPERT, 0], 0):
                c.start()

        for c in copies(plan_ref[PLAN_EXPERT, i], slot):
            c.wait()
        nrow, ncol = wbuf.shape[2], wbuf.shape[3]
        rows = 256

        def cast_rows(r, c):
            off = pl.multiple_of(r * rows, rows)
            for n in range(len(w_hbms)):
                wbf[pl.ds(off, rows), n * ncol:(n + 1) * ncol] = wbuf[slot, n, pl.ds(off, rows), :].astype(BF16)
            return c

        lax.fori_loop(0, nrow // rows, cast_rows, 0)
        nxt = plan_ref[PLAN_NEXT, i]

        @pl.when(nxt >= 0)
        def _():
            for c in copies(nxt, 1 - slot):
                c.start()


def _moe1_kernel(plan_ref, nu_ref, x_ref, wg_hbm, wu_hbm, o_ref, wbuf, wcat, sem, *, layer):
    i = pl.program_id(0)

    @pl.when(i < nu_ref[0])
    def _():
        _expert_weights_step(i, plan_ref, (wg_hbm, wu_hbm), wbuf, wcat, sem, layer)

        def gated(rows):
            h = jnp.dot(_unpack_bf16_pair(x_ref[rows, :]), wcat[...], preferred_element_type=F32)
            g = h[:, :MOE_HIDDEN]
            u = h[:, MOE_HIDDEN:]
            return (g * jax.nn.sigmoid(g) * u).astype(BF16)

        half = MOE_BM // 2

        @pl.when(plan_ref[PLAN_HALF, i] == 1)
        def _():
            o_ref[0:half, :] = gated(slice(0, half))
            o_ref[half:, :] = jnp.zeros((MOE_BM - half, MOE_HIDDEN), BF16)

        @pl.when(plan_ref[PLAN_HALF, i] == 0)
        def _():
            o_ref[...] = gated(slice(None))

    @pl.when(i >= nu_ref[0])
    def _():
        o_ref[...] = jnp.zeros_like(o_ref)


def _moe1(plan, n_used, xs, w_gate, w_up, layer):
    P, W = xs.shape
    D = w_gate.shape[2]
    nb = P // MOE_BM
    used_blk = lambda i, plan, nu: (jnp.minimum(i, nu[0] - 1), 0)
    return pl.pallas_call(
        functools.partial(_moe1_kernel, layer=layer),
        grid_spec=pltpu.PrefetchScalarGridSpec(
            num_scalar_prefetch=2,
            grid=(nb,),
            in_specs=[pl.BlockSpec((MOE_BM, W), used_blk), pl.BlockSpec(memory_space=pl.ANY),
                      pl.BlockSpec(memory_space=pl.ANY)],
            out_specs=pl.BlockSpec((MOE_BM, MOE_HIDDEN), lambda i, plan, nu: (i, 0)),
            scratch_shapes=[pltpu.VMEM((2, 2, D, MOE_HIDDEN), F32), pltpu.VMEM((D, 2 * MOE_HIDDEN), BF16),
                            pltpu.SemaphoreType.DMA((2, 2))],
        ),
        out_shape=jax.ShapeDtypeStruct((P, MOE_HIDDEN), BF16),
        compiler_params=_cparams(("arbitrary",), VMEM_LIMIT),
        name="moe_up",
    )(plan, n_used, xs, w_gate, w_up)


def _moe2_kernel(plan_ref, nu_ref, h_ref, wd_hbm, o_ref, wbuf, wbf, sem, *, layer):
    i = pl.program_id(0)

    @pl.when(i < nu_ref[0])
    def _():
        _expert_weights_step(i, plan_ref, (wd_hbm,), wbuf, wbf, sem, layer)
        half = MOE_BM // 2

        @pl.when(plan_ref[PLAN_HALF, i] == 1)
        def _():
            o_ref[0:half, :] = jnp.dot(h_ref[0:half, :], wbf[...], preferred_element_type=F32)
            o_ref[half:, :] = jnp.zeros((MOE_BM - half, o_ref.shape[1]), F32)

        @pl.when(plan_ref[PLAN_HALF, i] == 0)
        def _():
            o_ref[...] = jnp.dot(h_ref[...], wbf[...], preferred_element_type=F32)

    @pl.when(i >= nu_ref[0])
    def _():
        o_ref[...] = jnp.zeros_like(o_ref)


def _moe2(plan, n_used, hmid, w_down, layer):
    P = hmid.shape[0]
    D = w_down.shape[3]
    nb = P // MOE_BM
    return pl.pallas_call(
        functools.partial(_moe2_kernel, layer=layer),
        grid_spec=pltpu.PrefetchScalarGridSpec(
            num_scalar_prefetch=2,
            grid=(nb,),
            in_specs=[
                pl.BlockSpec((MOE_BM, MOE_HIDDEN), lambda i, plan, nu: (i, 0)),
                pl.BlockSpec(memory_space=pl.ANY),
            ],
            out_specs=pl.BlockSpec((MOE_BM, D), lambda i, plan, nu: (i, 0)),
            scratch_shapes=[pltpu.VMEM((2, 1, MOE_HIDDEN, D), F32), pltpu.VMEM((MOE_HIDDEN, D), BF16),
                            pltpu.SemaphoreType.DMA((2, 1))],
        ),
        out_shape=jax.ShapeDtypeStruct((P, D), F32),
        compiler_params=_cparams(("arbitrary",), VMEM_LIMIT),
        name="moe_down",
    )(plan, n_used, hmid, w_down)


def _combine_kernel(dest_ref, yb_hbm, x_ref, r_ref, mod_ref, fg_ref, o_ref, ybuf, sem, *, final):
    i = pl.program_id(0)
    n = pl.num_programs(0)

    n_tok = n * TM

    def row_copy(tok, k, t, slot):
        return pltpu.make_async_copy(yb_hbm.at[pl.ds(dest_ref[k * n_tok + tok], 1)],
                                     ybuf.at[slot, k, pl.ds(t, 1)], sem.at[slot])

    def issue_rows(blk, slot, t0):
        for r in range(8):
            for k in range(2):
                row_copy(blk * TM + t0 + r, k, t0 + r, slot).start()

    def wait_slot(slot):
        for k in range(2):
            pltpu.make_async_copy(yb_hbm.at[pl.ds(0, TM)], ybuf.at[slot, k], sem.at[slot]).wait()

    @pl.when(i == 0)
    def _():
        def first(j, c):
            issue_rows(0, 0, pl.multiple_of(j * 8, 8))
            return c
        lax.fori_loop(0, TM // 8, first, 0)

    def step(cur):
        @pl.when(i + 1 < n)
        def _():
            def body(j, c):
                issue_rows(i + 1, 1 - cur, pl.multiple_of(j * 8, 8))
                return c
            lax.fori_loop(0, TM // 8, body, 0)

        wait_slot(cur)
        r = r_ref[...]
        y = r[:, 2:3] * ybuf[cur, 0] + r[:, 3:4] * ybuf[cur, 1]
        xn = x_ref[...] + mod_ref[0, 5:6, :] * y
        if final:
            xn = _rms(xn) * fg_ref[...]
        o_ref[...] = xn

    @pl.when(i % 2 == 0)
    def _():
        step(0)

    @pl.when(i % 2 == 1)
    def _():
        step(1)


def _combine(dest_flat, yb, x_all, route, mod, final_g, seg_of, nblk, final):
    M, D = x_all.shape
    out_rows = nblk * TM if final else M
    kwargs = {} if final else {"input_output_aliases": {2: 0}}
    return pl.pallas_call(
        functools.partial(_combine_kernel, final=final),
        grid_spec=pltpu.PrefetchScalarGridSpec(
            num_scalar_prefetch=1,
            grid=(nblk,),
            in_specs=[
                pl.BlockSpec(memory_space=pl.ANY),
                pl.BlockSpec((TM, D), lambda i, d: (i, 0)),
                pl.BlockSpec((TM, 128), lambda i, d: (i, 0)),
                pl.BlockSpec((1, 6, D), lambda i, d: (seg_of(i), 0, 0)),
                pl.BlockSpec((1, D), lambda i, d: (0, 0)),
            ],
            out_specs=pl.BlockSpec((TM, D), lambda i, d: (i, 0)),
            scratch_shapes=[pltpu.VMEM((2, 2, TM, D), F32), pltpu.SemaphoreType.DMA((2,))],
        ),
        out_shape=jax.ShapeDtypeStruct((out_rows, D), F32),
        compiler_params=_cparams(("arbitrary",), VMEM_LIMIT),
        name="combine",
        **kwargs,
    )(dest_flat, yb, x_all, route, mod, final_g.reshape(1, D))


def _moe_layer(x_all, h2, route, cnt, mod, final_g, w_gate, w_up, w_down, layer, seg_of, nblk, final):
    T = nblk * TM
    e_ids = route[:, 0:2].T.astype(jnp.int32)
    ranks = route[:, 4:6].T.astype(jnp.int32)
    counts = cnt[0, :MOE_EXPERTS].astype(jnp.int32)
    padded = (counts + MOE_BM - 1) // MOE_BM * MOE_BM
    pad_ends = jnp.cumsum(padded)
    pad_starts = pad_ends - padded
    expert = jnp.arange(MOE_EXPERTS, dtype=jnp.int32)[:, None, None]
    start_of = jnp.sum(jnp.where(e_ids[None] == expert, pad_starts[:, None, None], 0), axis=0)
    dest = (start_of + ranks).reshape(-1)
    pad_info = jnp.stack([pad_starts + counts, padded - counts,
                          jnp.broadcast_to(pad_ends[-1], (MOE_EXPERTS,))]).astype(jnp.int32)
    n_blocks = -(-(2 * T + MOE_EXPERTS * (MOE_BM - 1)) // MOE_BM)
    block_starts = jnp.arange(n_blocks, dtype=jnp.int32) * MOE_BM
    block_e = jnp.sum((block_starts[:, None] >= pad_ends[None, :]).astype(jnp.int32), axis=1)
    block_e = jnp.minimum(block_e, MOE_EXPERTS - 1)
    n_used = (pad_ends[-1:] // MOE_BM).astype(jnp.int32)
    plan = _block_plan(block_e, n_used, pad_starts, counts)
    xs = _dispatch(dest, pad_info, h2, n_blocks * MOE_BM, nblk)
    hmid = _moe1(plan, n_used, xs, w_gate, w_up, layer)
    yb = _moe2(plan, n_used, hmid, w_down, layer)
    return _combine(dest, yb, x_all, route, mod, final_g, seg_of, nblk, final)


def _rope_tables(S):
    pos = jnp.arange(S, dtype=jnp.int32)
    rows = (pos // GRID_W).astype(F32)
    cols = (pos % GRID_W).astype(F32)
    n_freq = HEAD_DIM // 4
    inv = 1.0 / (ROPE_THETA ** (jnp.arange(n_freq, dtype=F32) / n_freq))
    cr, sr = jnp.cos(rows[:, None] * inv), jnp.sin(rows[:, None] * inv)
    cc, sc = jnp.cos(cols[:, None] * inv), jnp.sin(cols[:, None] * inv)
    cos_t = jnp.concatenate([cr, cr, cc, cc], axis=1)
    sin_t = jnp.concatenate([-sr, sr, -sc, sc], axis=1)
    cos_t = jnp.concatenate([cos_t, jnp.ones((TM, HEAD_DIM), F32)], axis=0)
    sin_t = jnp.concatenate([sin_t, jnp.zeros((TM, HEAD_DIM), F32)], axis=0)
    return cos_t, sin_t


def kernel(x, c, ctx, c_ctx, ada_w, ada_b, norm1_g, norm2_g, w_in, w_out, att_q_norm, att_k_norm, conv_w, conv_b, lru_wr, lru_br, lru_wi, lru_bi, lru_lambda, na_rpb, router_wg, router_bg, router_we, router_be, moe_w_gate, moe_w_up, moe_w_down, final_g):
    B, S, D = x.shape
    C = ctx.shape[1]
    L = ada_w.shape[0]
    assert C == TM and S % TM == 0 and S // TM >= 3 and B + 1 <= 8
    nbs = S // TM
    nlat = B * nbs
    m_lat = B * S
    M = m_lat + B * C

    def seg_of(i):
        return jnp.minimum(i // nbs, B)

    x_rows, ctx_rows = x.reshape(m_lat, D), ctx.reshape(B * C, D)
    cvec = jnp.zeros((8, D), F32).at[:B].set(c).at[B].set(c_ctx)
    mod_all = _adaln(cvec, ada_w, ada_b)[:, :B + 1].reshape(L, B + 1, 6, D)
    cos_t, sin_t = _rope_tables(S)
    g_att = ATT_Q_HEADS // ATT_KV_HEADS

    out = None
    for l in range(L):
        last = l == L - 1
        mod = mod_all[l]
        proj = _inproj(x_rows, ctx_rows, mod, norm1_g[l], w_in[l].astype(BF16), B, S)

        q, k_all, v_all = _qkprep(proj, cos_t, sin_t, att_q_norm[l], att_k_norm[l], B, S)
        oa = _gqa_latent(q, k_all, v_all, B, S)

        sp = jax.nn.softplus(-lru_lambda[l].astype(F32))
        hrev = _lru_dir(proj, None, conv_w[l], conv_b[l], lru_wr[l, 1], lru_br[l, 1], lru_wi[l, 1],
                        lru_bi[l, 1], sp[1], B, S, reverse=True)
        ob = _lru_dir(proj, hrev, conv_w[l], conv_b[l], lru_wr[l, 0], lru_br[l, 0], lru_wi[l, 0],
                      lru_bi[l, 0], sp[0], B, S, reverse=False)

        tl, tr = _na_bias_tables(na_rpb[l])
        oc = _na_latent(proj, tl, tr, B, S)

        if not last:
            oa = _ctx_attn(
                q, lambda b, h: (nlat + b, h),
                k_all, pl.BlockSpec((None, TM, HEAD_DIM), lambda b, h: (b, nbs, h // g_att)),
                v_all, pl.BlockSpec((None, TM, HEAD_DIM), lambda b, h: (b, nbs, 2 * (h // g_att))),
                oa, B, ATT_Q_HEADS, nlat, None)
            oc = _ctx_attn(
                proj, lambda b, h: (nlat + b, COL_QN // HEAD_DIM + h),
                proj, pl.BlockSpec((TM, HEAD_DIM), lambda b, h: (nlat + b, COL_KN // HEAD_DIM + h)),
                proj, pl.BlockSpec((TM, HEAD_DIM), lambda b, h: (nlat + b, COL_VN // HEAD_DIM + h)),
                oc, B, NA_HEADS, nlat, HEAD_DIM ** -0.5)

        nblk = nlat if last else M // TM
        w_router = jnp.zeros((D, 128), F32).at[:, :MOE_GROUPS].set(router_wg[l])
        w_router = w_router.at[:, MOE_GROUPS:MOE_GROUPS + MOE_EXPERTS].set(router_we[l])
        b_router = jnp.zeros((1, 128), F32).at[0, :MOE_GROUPS].set(router_bg[l])
        b_router = b_router.at[0, MOE_GROUPS:MOE_GROUPS + MOE_EXPERTS].set(router_be[l])
        x_all, h2, route, cnt = _outproj(oa, ob, oc, x_rows, ctx_rows, mod, norm2_g[l], w_out[l].astype(BF16),
                                         w_router.astype(BF16), b_router, B, S, nblk * TM)
        res = _moe_layer(x_all, h2, route, cnt, mod, final_g, moe_w_gate, moe_w_up, moe_w_down, l,
                         seg_of, nblk, last)
        if last:
            out = res
        else:
            x_rows, ctx_rows = res, None
    return out.reshape(B, S, D)
```

```python
import functools

import numpy as np
import jax
import jax.numpy as jnp
from jax import lax
from jax.experimental import pallas as pl
from jax.experimental.pallas import tpu as pltpu

F32 = jnp.float32
BF16 = jnp.bfloat16

GRID_W = 64
HEAD_DIM = 128
ATT_Q_HEADS = 4
ATT_KV_HEADS = 2
LRU_WIDTH = 1024
LRU_BLOCKS = 8
LRU_C = 8.0
CONV_W = 4
NA_HEADS = 4
NA_ROWS = 8
NA_COLS = 16
ATT_Q_W = ATT_Q_HEADS * HEAD_DIM
ATT_KV_W = ATT_KV_HEADS * HEAD_DIM
NA_W = NA_HEADS * HEAD_DIM
MOE_GROUPS = 4
MOE_EXPERTS_PER_GROUP = 8
MOE_EXPERTS = MOE_GROUPS * MOE_EXPERTS_PER_GROUP
MOE_HIDDEN = 1024
ROPE_THETA = 10000.0
EPS = 1e-6
LOG2E = 1.4426950408889634

COL_QA = 0
COL_KA = COL_QA + ATT_Q_W
COL_VA = COL_KA + ATT_KV_W
COL_UB = COL_VA + ATT_KV_W
COL_GB = COL_UB + LRU_WIDTH
COL_QN = COL_GB + LRU_WIDTH
COL_KN = COL_QN + NA_W
COL_VN = COL_KN + NA_W
IN_WIDTH = COL_VN + NA_W

TM = 256
MOE_BM = 256
NA_QROWS = 4
NA_WROWS = 12
NEG_BIAS = -1e30
VMEM_LIMIT = 56 * 1024 * 1024


def _cparams(sem, vmem=None):
    return pltpu.CompilerParams(dimension_semantics=sem, vmem_limit_bytes=vmem)


def _rms(x):
    return x * lax.rsqrt(jnp.mean(x * x, axis=-1, keepdims=True) + EPS)


def _adaln_kernel(c_ref, w_ref, b_ref, o_ref):
    cv = c_ref[...]
    s = cv * jax.nn.sigmoid(cv)
    o_ref[0] = jnp.dot(s.astype(BF16), w_ref[0].astype(BF16), preferred_element_type=F32) + b_ref[0]


def _adaln(cvec, ada_w, ada_b):
    L, D, N = ada_w.shape
    tn = 1024
    return pl.pallas_call(
        _adaln_kernel,
        grid=(L, N // tn),
        in_specs=[
            pl.BlockSpec((8, D), lambda l, j: (0, 0)),
            pl.BlockSpec((1, D, tn), lambda l, j: (l, 0, j)),
            pl.BlockSpec((1, 1, tn), lambda l, j: (l, 0, j)),
        ],
        out_specs=pl.BlockSpec((1, 8, tn), lambda l, j: (l, 0, j)),
        out_shape=jax.ShapeDtypeStruct((L, 8, N), F32),
        compiler_params=_cparams(("parallel", "parallel"), VMEM_LIMIT),
        name="adaln",
    )(cvec, ada_w, ada_b.reshape(L, 1, N))


def _inproj_kernel(*refs, n_first, two_sources):
    if two_sources:
        x_ref, xb_ref, mod_ref, g_ref, w_ref, o_ref = refs
        x = jnp.where(pl.program_id(0) < n_first, x_ref[...], xb_ref[...])
    else:
        x_ref, mod_ref, g_ref, w_ref, o_ref = refs
        x = x_ref[...]
    h = _rms(x) * g_ref[...]
    h = h * (1.0 + mod_ref[0, 1:2, :]) + mod_ref[0, 0:1, :]
    o_ref[...] = jnp.dot(h.astype(BF16), w_ref[...], preferred_element_type=F32)


def _inproj(x_rows, ctx_rows, mod, g1, w_in_bf, B, S):
    two = ctx_rows is not None
    D = x_rows.shape[1]
    M = x_rows.shape[0] + (ctx_rows.shape[0] if two else 0)
    N = w_in_bf.shape[1]
    tm = 512 if (S % 512 == 0 and (M - B * S) % 512 == 0) else TM
    per_seq = S // tm
    n_first = B * per_seq
    if two:
        n_ctx_blocks = ctx_rows.shape[0] // tm
        x_specs = [pl.BlockSpec((tm, D), lambda i: (jnp.minimum(i, n_first - 1), 0)),
                   pl.BlockSpec((tm, D), lambda i: (jnp.maximum(i - n_first, 0), 0),
                                pipeline_mode=pl.Buffered(1 if n_ctx_blocks == 1 else 2))]
        x_args = [x_rows, ctx_rows]
    else:
        x_specs = [pl.BlockSpec((tm, D), lambda i: (i, 0))]
        x_args = [x_rows]
    return pl.pallas_call(
        functools.partial(_inproj_kernel, n_first=n_first, two_sources=two),
        grid=(M // tm,),
        in_specs=[
            *x_specs,
            pl.BlockSpec((1, 6, D), lambda i: (jnp.minimum(i // per_seq, B), 0, 0)),
            pl.BlockSpec((1, D), lambda i: (0, 0)),
            pl.BlockSpec((D, N), lambda i: (0, 0), pipeline_mode=pl.Buffered(1)),
        ],
        out_specs=pl.BlockSpec((tm, N), lambda i: (i, 0)),
        out_shape=jax.ShapeDtypeStruct((M, N), F32),
        compiler_params=_cparams(("parallel",), VMEM_LIMIT),
        name="inproj",
    )(*x_args, mod, g1.reshape(1, D), w_in_bf)


def _norm_rope(t, g, cos, sin):
    y = _rms(t) * g
    lane = lax.broadcasted_iota(jnp.int32, y.shape, 1)
    first_half = (lane % 64) < 32
    partner = jnp.where(first_half, pltpu.roll(y, 96, 1), pltpu.roll(y, 32, 1))
    return y * cos + partner * sin


def _qkprep_kernel(p_ref, cos_ref, sin_ref, qg_ref, kg_ref, q_ref, k_ref, v_ref):
    cos = cos_ref[...]
    sin = sin_ref[...]
    scale = HEAD_DIM ** -0.5 * LOG2E
    for h in range(ATT_Q_HEADS):
        t = p_ref[:, COL_QA + h * HEAD_DIM:COL_QA + (h + 1) * HEAD_DIM]
        q_ref[:, h * HEAD_DIM:(h + 1) * HEAD_DIM] = (_norm_rope(t, qg_ref[...], cos, sin) * scale).astype(BF16)
    for h in range(ATT_KV_HEADS):
        t = p_ref[:, COL_KA + h * HEAD_DIM:COL_KA + (h + 1) * HEAD_DIM]
        k_ref[0, :, h * HEAD_DIM:(h + 1) * HEAD_DIM] = _norm_rope(t, kg_ref[...], cos, sin).astype(BF16)
        v_ref[0, :, 2 * h * HEAD_DIM:(2 * h + 1) * HEAD_DIM] = (
            p_ref[:, COL_VA + h * HEAD_DIM:COL_VA + (h + 1) * HEAD_DIM].astype(BF16))
        v_ref[0, :, (2 * h + 1) * HEAD_DIM:(2 * h + 2) * HEAD_DIM] = jnp.ones((p_ref.shape[0], HEAD_DIM), BF16)


def _qkprep(proj, cos_t, sin_t, qg, kg, B, S):
    M = proj.shape[0]
    nbs = S // TM
    nlat = B * nbs

    def b_of(i):
        return jnp.where(i < nlat, i // nbs, i - nlat)

    def pos_of(i):
        return jnp.where(i < nlat, i % nbs, nbs)

    kl = S + TM
    return pl.pallas_call(
        _qkprep_kernel,
        grid=(M // TM,),
        in_specs=[
            pl.BlockSpec((TM, COL_UB), lambda i: (i, 0)),
            pl.BlockSpec((TM, HEAD_DIM), lambda i: (pos_of(i), 0)),
            pl.BlockSpec((TM, HEAD_DIM), lambda i: (pos_of(i), 0)),
            pl.BlockSpec((1, HEAD_DIM), lambda i: (0, 0)),
            pl.BlockSpec((1, HEAD_DIM), lambda i: (0, 0)),
        ],
        out_specs=[
            pl.BlockSpec((TM, ATT_Q_W), lambda i: (i, 0)),
            pl.BlockSpec((1, TM, ATT_KV_W), lambda i: (b_of(i), pos_of(i), 0)),
            pl.BlockSpec((1, TM, 2 * ATT_KV_W), lambda i: (b_of(i), pos_of(i), 0)),
        ],
        out_shape=[
            jax.ShapeDtypeStruct((M, ATT_Q_W), BF16),
            jax.ShapeDtypeStruct((B, kl, ATT_KV_W), BF16),
            jax.ShapeDtypeStruct((B, kl, 2 * ATT_KV_W), BF16),
        ],
        compiler_params=_cparams(("parallel",), VMEM_LIMIT),
        name="qkprep",
    )(proj, cos_t, sin_t, qg.reshape(1, HEAD_DIM), kg.reshape(1, HEAD_DIM))


def _gqa_kernel(q_ref, k_ref, v_ref, o_in_ref, o_ref, q2_ref, sa_ref, sb_ref, m_ref, acc_ref, *, tk):
    del o_in_ref
    tq = q_ref.shape[0]
    q2_ref[0:tq, :] = q_ref[:, :HEAD_DIM]
    q2_ref[tq:, :] = q_ref[:, HEAD_DIM:]
    nk = k_ref.shape[1] // tk
    m_ref[...] = jnp.full(m_ref.shape, -jnp.inf, F32)
    acc_ref[...] = jnp.zeros(acc_ref.shape, F32)

    def scores(c, s_ref):
        off = pl.multiple_of(c * tk, tk)
        s_ref[...] = lax.dot_general(q2_ref[...], k_ref[0, pl.ds(off, tk), :], (((1,), (1,)), ((), ())),
                                     preferred_element_type=F32)

    def update(c, s_ref):
        off = pl.multiple_of(c * tk, tk)
        vc = v_ref[0, pl.ds(off, tk), :]
        for hh in range(2):
            rows = slice(hh * tq, (hh + 1) * tq)
            m = m_ref[rows, :]
            m_new = jnp.maximum(m, jnp.max(s_ref[rows, :], axis=-1, keepdims=True))
            m_ref[rows, :] = m_new
            p = jnp.exp2((s_ref[rows, :] - jnp.concatenate([m_new] * (tk // HEAD_DIM), axis=1)).astype(BF16))
            alpha = jnp.exp2(m - m_new)
            acc_ref[rows, :] = (jnp.concatenate([alpha, alpha], axis=1) * acc_ref[rows, :]
                                + jnp.dot(p, vc, preferred_element_type=F32))

    scores(0, sa_ref)
    n_pairs = (nk - 1) // 2

    def body(j, carry):
        scores(2 * j + 1, sb_ref)
        update(2 * j, sa_ref)
        scores(2 * j + 2, sa_ref)
        update(2 * j + 1, sb_ref)
        return carry

    lax.fori_loop(0, n_pairs, body, 0)
    if nk % 2 == 0:
        scores(nk - 1, sb_ref)
        update(nk - 2, sa_ref)
        update(nk - 1, sb_ref)
    else:
        update(nk - 1, sa_ref)
    acc = acc_ref[...]
    o = acc[:, :HEAD_DIM] / acc[:, HEAD_DIM:]
    o_ref[:, :HEAD_DIM] = o[:tq].astype(BF16)
    o_ref[:, HEAD_DIM:] = o[tq:].astype(BF16)


def _gqa_latent(q, k_all, v_all, B, S):
    M = q.shape[0]
    kl = k_all.shape[1]
    tq = 512 if S % 512 == 0 else TM
    nbs = S // tq
    tk = 768 if kl % 768 == 0 else TM
    g = ATT_Q_HEADS // ATT_KV_HEADS
    return pl.pallas_call(
        functools.partial(_gqa_kernel, tk=tk),
        grid=(B, ATT_KV_HEADS, nbs),
        in_specs=[
            pl.BlockSpec((tq, g * HEAD_DIM), lambda b, h, i: (b * nbs + i, h)),
            pl.BlockSpec((1, kl, HEAD_DIM), lambda b, h, i: (b, 0, h)),
            pl.BlockSpec((1, kl, 2 * HEAD_DIM), lambda b, h, i: (b, 0, h)),
            pl.BlockSpec(memory_space=pl.ANY),
        ],
        out_specs=pl.BlockSpec((tq, g * HEAD_DIM), lambda b, h, i: (b * nbs + i, h)),
        out_shape=jax.ShapeDtypeStruct((M, ATT_Q_W), BF16),
        input_output_aliases={3: 0},
        scratch_shapes=[
            pltpu.VMEM((g * tq, HEAD_DIM), BF16),
            pltpu.VMEM((g * tq, tk), F32),
            pltpu.VMEM((g * tq, tk), F32),
            pltpu.VMEM((g * tq, HEAD_DIM), F32),
            pltpu.VMEM((g * tq, 2 * HEAD_DIM), F32),
        ],
        compiler_params=_cparams(("parallel", "parallel", "parallel"), VMEM_LIMIT),
        name="gqa_latent",
    )(q, k_all, v_all, jnp.zeros((M, ATT_Q_W), BF16))


def _ctx_attn_kernel(q_ref, k_ref, v_ref, o_in_ref, o_ref, *, scale):
    del o_in_ref
    if scale is None:
        q = q_ref[...]
    else:
        q = (q_ref[...] * (scale * LOG2E)).astype(BF16)
    k = k_ref[...].astype(BF16)
    s = lax.dot_general(q, k, (((1,), (1,)), ((), ())), preferred_element_type=F32)
    p = jnp.exp2(s - jnp.max(s, axis=-1, keepdims=True))
    l = jnp.sum(p, axis=-1, keepdims=True)
    pv = jnp.dot(p.astype(BF16), v_ref[...].astype(BF16), preferred_element_type=F32)
    o_ref[...] = (pv / l).astype(BF16)


def _ctx_attn(q_arr, q_map, k_arr, k_spec, v_arr, v_spec, o_arr, B, n_heads, ctx_blk0, scale):
    return pl.pallas_call(
        functools.partial(_ctx_attn_kernel, scale=scale),
        grid=(B, n_heads),
        in_specs=[
            pl.BlockSpec((TM, HEAD_DIM), q_map),
            k_spec,
            v_spec,
            pl.BlockSpec(memory_space=pl.ANY),
        ],
        out_specs=pl.BlockSpec((TM, HEAD_DIM), lambda b, h: (ctx_blk0 + b, h)),
        out_shape=jax.ShapeDtypeStruct(o_arr.shape, o_arr.dtype),
        input_output_aliases={3: 0},
        compiler_params=_cparams(("parallel", "parallel"), VMEM_LIMIT),
        name="ctx_attn",
    )(q_arr, k_arr, v_arr, o_arr)


def _lru_kernel(u_ref, up_ref, un_ref, *rest, reverse, final):
    if final:
        (g_ref, hrev_ref, cw_ref, cb_ref, wr_ref, br_ref, wi_ref, bi_ref, sp_ref,
         o_ref, xpad, a_scr, b_scr, h_scr, hst) = rest
    else:
        (cw_ref, cb_ref, wr_ref, br_ref, wi_ref, bi_ref, sp_ref,
         o_ref, xpad, a_scr, b_scr, h_scr, hst) = rest
    s = pl.program_id(1)
    ns = pl.num_programs(1)
    T = u_ref.shape[0]
    W = u_ref.shape[1]

    @pl.when(s == 0)
    def _():
        hst[...] = jnp.zeros_like(hst)

    if reverse:
        j = ns - 1 - s
    else:
        j = s - 1
    is_first = jnp.logical_or(s == 0, j == 0)
    is_last = jnp.logical_or(s == 0, j == ns - 2)
    prev_rows = jnp.where(is_first, 0.0, up_ref[...])
    next_rows = jnp.where(is_last, 0.0, un_ref[...])
    xpad[0:8, :] = prev_rows
    xpad[8:8 + T, :] = u_ref[...]
    xpad[8 + T:16 + T, :] = next_rows
    u = (cb_ref[...]
         + xpad[7:7 + T, :] * cw_ref[0:1, :]
         + xpad[8:8 + T, :] * cw_ref[1:2, :]
         + xpad[9:9 + T, :] * cw_ref[2:3, :]
         + xpad[10:10 + T, :] * cw_ref[3:4, :])

    ub = u.astype(BF16)
    nb = LRU_WIDTH // LRU_BLOCKS
    for n in range(LRU_BLOCKS):
        sl = slice(n * nb, (n + 1) * nb)
        un_ = ub[:, sl]
        tr = jnp.tanh(jnp.dot(un_, wr_ref[n], preferred_element_type=F32) + br_ref[:, sl])
        ti = jnp.tanh(jnp.dot(un_, wi_ref[n], preferred_element_type=F32) + bi_ref[:, sl])
        ig = 0.5 * ti + 0.5
        log_a = sp_ref[:, sl] * tr + sp_ref[:, sl]
        a = jnp.exp(log_a)
        a_scr[:, sl] = a
        one_minus_a2 = -jnp.tanh(log_a) * (a * a + 1.0)
        root = one_minus_a2 * lax.rsqrt(jnp.maximum(one_minus_a2, 1e-30))
        b_scr[:, sl] = root * (ig * u[:, sl])

    def step(t, h):
        row = (T - 1 - t) if reverse else t
        h = a_scr[pl.ds(row, 1), :] * h + b_scr[pl.ds(row, 1), :]
        h_scr[pl.ds(row, 1), :] = h
        return h

    hst[...] = lax.fori_loop(0, T, step, hst[...], unroll=8)

    if final:
        y = h_scr[...] + hrev_ref[...]
        o_ref[...] = (y * jax.nn.gelu(g_ref[...])).astype(o_ref.dtype)
    else:
        o_ref[...] = h_scr[...]
    del W


def _lru_dir(proj, hrev, conv_w, conv_b, wr, br, wi, bi, sp, B, S, reverse):
    M = proj.shape[0]
    nbs = S // TM
    nlat = B * nbs
    final = hrev is not None
    ub_blk = COL_UB // LRU_WIDTH
    gb_blk = COL_GB // LRU_WIDTH
    r8 = TM // 8
    n8 = M // 8

    def blk(b, s):
        j = (nbs - s) if reverse else (s - 1)
        return jnp.where(s == 0, nlat + b, b * nbs + j)

    in_specs = [
        pl.BlockSpec((TM, LRU_WIDTH), lambda b, s: (blk(b, s), ub_blk)),
        pl.BlockSpec((8, LRU_WIDTH), lambda b, s: (jnp.maximum(blk(b, s) * r8 - 1, 0), ub_blk)),
        pl.BlockSpec((8, LRU_WIDTH), lambda b, s: (jnp.minimum((blk(b, s) + 1) * r8, n8 - 1), ub_blk)),
    ]
    args = [proj, proj, proj]
    if final:
        in_specs += [
            pl.BlockSpec((TM, LRU_WIDTH), lambda b, s: (blk(b, s), gb_blk)),
            pl.BlockSpec((TM, LRU_WIDTH), lambda b, s: (blk(b, s), 0)),
        ]
        args += [proj, hrev]
    const2 = lambda b, s: (0, 0)
    in_specs += [
        pl.BlockSpec((CONV_W, LRU_WIDTH), const2),
        pl.BlockSpec((1, LRU_WIDTH), const2),
        pl.BlockSpec((LRU_BLOCKS, LRU_WIDTH // LRU_BLOCKS, LRU_WIDTH // LRU_BLOCKS), lambda b, s: (0, 0, 0)),
        pl.BlockSpec((1, LRU_WIDTH), const2),
        pl.BlockSpec((LRU_BLOCKS, LRU_WIDTH // LRU_BLOCKS, LRU_WIDTH // LRU_BLOCKS), lambda b, s: (0, 0, 0)),
        pl.BlockSpec((1, LRU_WIDTH), const2),
        pl.BlockSpec((1, LRU_WIDTH), const2),
    ]
    args += [conv_w, conv_b.reshape(1, -1), (0.5 * wr).astype(BF16), 0.5 * br.reshape(1, -1),
             (0.5 * wi).astype(BF16), 0.5 * bi.reshape(1, -1), (-0.5 * LRU_C) * sp.reshape(1, -1)]
    return pl.pallas_call(
        functools.partial(_lru_kernel, reverse=reverse, final=final),
        grid=(B, nbs + 1),
        in_specs=in_specs,
        out_specs=pl.BlockSpec((TM, LRU_WIDTH), lambda b, s: (blk(b, s), 0)),
        out_shape=jax.ShapeDtypeStruct((M, LRU_WIDTH), BF16 if final else F32),
        scratch_shapes=[
            pltpu.VMEM((TM + 16, LRU_WIDTH), F32),
            pltpu.VMEM((TM, LRU_WIDTH), F32),
            pltpu.VMEM((TM, LRU_WIDTH), F32),
            pltpu.VMEM((TM, LRU_WIDTH), F32),
            pltpu.VMEM((1, LRU_WIDTH), F32),
        ],
        compiler_params=_cparams(("parallel", "arbitrary"), VMEM_LIMIT),
        name="lru_fwd" if final else "lru_rev",
    )(*args)


def _na_kernel(q_ref, k0_ref, k1_ref, k2_ref, v0_ref, v1_ref, v2_ref, kc_ref, vc_ref, tl_ref, tr_ref,
               o_in_ref, o_ref, s_scr, *, n_rows):
    del o_in_ref
    i = pl.program_id(1)
    nbs = n_rows // NA_QROWS
    r0 = NA_QROWS * i
    ws = NA_QROWS * jnp.clip(i - 1, 0, nbs - 3)
    scale = HEAD_DIM ** -0.5 * LOG2E
    nkeys = s_scr.shape[1]
    ones = jnp.ones((nkeys, HEAD_DIM), BF16)
    for h in range(NA_HEADS):
        hs = slice(h * HEAD_DIM, (h + 1) * HEAD_DIM)
        qh = (q_ref[:, hs] * scale).astype(BF16)
        kh = jnp.concatenate([k0_ref[:, hs], k1_ref[:, hs], k2_ref[:, hs], kc_ref[:, hs]], axis=0).astype(BF16)
        vh = jnp.concatenate([v0_ref[:, hs], v1_ref[:, hs], v2_ref[:, hs], vc_ref[:, hs]], axis=0).astype(BF16)
        vh = jnp.concatenate([vh, ones], axis=1)
        s_scr[...] = lax.dot_general(qh, kh, (((1,), (1,)), ((), ())), preferred_element_type=F32)
        for qr in range(NA_QROWS):
            r = r0 + qr
            rs = jnp.clip(r - NA_ROWS // 2, 0, n_rows - NA_ROWS)

            def tile_of(a):
                kr = ws + a
                valid = jnp.logical_and(kr >= rs, kr < rs + NA_ROWS)
                return jnp.where(valid, kr - r + (NA_ROWS - 1), 2 * NA_ROWS - 1)

            for pr in range(NA_WROWS // 2):
                bias = tl_ref[h, tile_of(2 * pr)] + tr_ref[h, tile_of(2 * pr + 1)]
                rsl = slice(qr * GRID_W, (qr + 1) * GRID_W)
                csl = slice(pr * 2 * GRID_W, (pr + 1) * 2 * GRID_W)
                s_scr[rsl, csl] = s_scr[rsl, csl] + bias
        s = s_scr[...]
        p = jnp.exp2((s - jnp.max(s, axis=-1, keepdims=True)).astype(BF16))
        pv = jnp.dot(p, vh, preferred_element_type=F32)
        o_ref[:, hs] = (pv[:, :HEAD_DIM] / pv[:, HEAD_DIM:]).astype(BF16)


def _na_latent(proj, tl, tr, B, S):
    M = proj.shape[0]
    nbs = S // TM
    nlat = B * nbs
    n_rows = S // GRID_W
    qb, kb, vb = COL_QN // NA_W, COL_KN // NA_W, COL_VN // NA_W

    def wblk(b, i, t):
        return b * nbs + jnp.clip(i - 1, 0, nbs - 3) + t

    blk = (TM, NA_W)
    in_specs = [pl.BlockSpec(blk, lambda b, i: (b * nbs + i, qb))]
    in_specs += [pl.BlockSpec(blk, functools.partial(lambda b, i, t: (wblk(b, i, t), kb), t=t)) for t in range(3)]
    in_specs += [pl.BlockSpec(blk, functools.partial(lambda b, i, t: (wblk(b, i, t), vb), t=t)) for t in range(3)]
    in_specs += [
        pl.BlockSpec(blk, lambda b, i: (nlat + b, kb)),
        pl.BlockSpec(blk, lambda b, i: (nlat + b, vb)),
        pl.BlockSpec(tl.shape, lambda b, i: (0, 0, 0, 0)),
        pl.BlockSpec(tr.shape, lambda b, i: (0, 0, 0, 0)),
        pl.BlockSpec(memory_space=pl.ANY),
    ]
    return pl.pallas_call(
        functools.partial(_na_kernel, n_rows=n_rows),
        grid=(B, nbs),
        in_specs=in_specs,
        out_specs=pl.BlockSpec(blk, lambda b, i: (b * nbs + i, 0)),
        out_shape=jax.ShapeDtypeStruct((M, NA_W), BF16),
        input_output_aliases={11: 0},
        scratch_shapes=[pltpu.VMEM((TM, NA_WROWS * GRID_W + TM), F32)],
        compiler_params=_cparams(("parallel", "parallel"), VMEM_LIMIT),
        name="na_latent",
    )(proj, proj, proj, proj, proj, proj, proj, proj, proj, tl, tr, jnp.zeros((M, NA_W), BF16))


def _na_bias_tables(rpb):
    j = np.arange(GRID_W)
    cs = np.clip(j - NA_COLS // 2, 0, GRID_W - NA_COLS)
    kc = np.arange(GRID_W)
    inside = (kc[None, :] >= cs[:, None]) & (kc[None, :] < cs[:, None] + NA_COLS)
    dc = kc[None, :] - j[:, None] + (NA_COLS - 1)
    sel = (dc[:, :, None] == np.arange(2 * NA_COLS - 1)[None, None, :]) & inside[:, :, None]
    t = jnp.einsum("hrd,jkd->hrjk", rpb * LOG2E, jnp.asarray(sel, F32), precision=lax.Precision.HIGHEST)
    t = jnp.where(inside[None, None], t, NEG_BIAS)
    t = jnp.concatenate([t, jnp.full_like(t[:, :1], NEG_BIAS)], axis=1)
    z = jnp.zeros_like(t)
    return jnp.concatenate([t, z], axis=-1), jnp.concatenate([z, t], axis=-1)


def _outproj_kernel(*refs, n_first, n_steps, two_sources):
    if two_sources:
        (oa_ref, ob_ref, oc_ref, x_ref, xb_ref, mod_ref, g_ref, w_ref, wr_ref, br_ref,
         xo_ref, h2_ref, rt_ref, cnt_ref, carry, lg_scr) = refs
    else:
        (oa_ref, ob_ref, oc_ref, x_ref, mod_ref, g_ref, w_ref, wr_ref, br_ref,
         xo_ref, h2_ref, rt_ref, cnt_ref, carry, lg_scr) = refs
    step = pl.program_id(0)
    i = jnp.minimum(step, n_steps - 1)

    @pl.when(step == 0)
    def _():
        carry[...] = jnp.zeros_like(carry)
        lg_scr[...] = jnp.zeros_like(lg_scr)

    route = _route_math(lg_scr[...])
    rt_ref[...] = _rank_math(route, carry, step > 0)
    cnt_ref[...] = jnp.broadcast_to(carry[...], cnt_ref.shape)

    k1 = ATT_Q_W
    k2 = k1 + LRU_WIDTH
    for r0 in range(0, x_ref.shape[0], TM):
        rows = slice(r0, r0 + TM)
        acc = jnp.dot(oa_ref[rows, :], w_ref[0:k1, :], preferred_element_type=F32)
        acc = acc + jnp.dot(ob_ref[rows, :], w_ref[k1:k2, :], preferred_element_type=F32)
        acc = acc + jnp.dot(oc_ref[rows, :], w_ref[k2:, :], preferred_element_type=F32)
        x_in = x_ref[rows, :]
        if two_sources:
            x_in = jnp.where(i < n_first, x_in, xb_ref[rows, :])
        xn = x_in + mod_ref[0, 2:3, :] * acc
        xo_ref[rows, :] = xn
        h2 = _rms(xn) * g_ref[...]
        h2 = h2 * (1.0 + mod_ref[0, 4:5, :]) + mod_ref[0, 3:4, :]
        h2_ref[rows, :] = _pack_bf16_pair(h2)
        lg_scr[rows, :] = jnp.dot(h2.astype(BF16), wr_ref[...], preferred_element_type=F32) + br_ref[...]


def _pack_bf16_pair(x):
    w = x.shape[1] // 2
    lo = pltpu.bitcast(x[:, :w].astype(BF16).astype(F32), jnp.uint32)
    hi = pltpu.bitcast(x[:, w:].astype(BF16).astype(F32), jnp.uint32)
    return hi | (lo >> 16)


def _unpack_bf16_pair(words):
    lo = pltpu.bitcast(words << 16, F32)
    hi = pltpu.bitcast(words & jnp.uint32(0xFFFF0000), F32)
    return jnp.concatenate([lo, hi], axis=1).astype(BF16)


def _outproj(oa, ob, oc, x_rows, ctx_rows, mod, g2, w_out_bf, w_router, b_router, B, S, n_rows):
    M = oa.shape[0]
    D = x_rows.shape[1]
    two = ctx_rows is not None
    tm = 512 if (S % 512 == 0 and (M - B * S) % 512 == 0) else TM
    per_seq = S // tm
    n_first = B * per_seq
    packed = jax.eval_shape(_pack_bf16_pair, jax.ShapeDtypeStruct((tm, D), F32))
    n_steps = n_rows // tm

    def blk(s):
        return jnp.minimum(s, n_steps - 1)

    if two:
        x_specs = [pl.BlockSpec((tm, D), lambda s: (jnp.minimum(blk(s), n_first - 1), 0)),
                   pl.BlockSpec((tm, D), lambda s: (jnp.maximum(blk(s) - n_first, 0), 0))]
        x_args = [x_rows, ctx_rows]
    else:
        x_specs = [pl.BlockSpec((tm, D), lambda s: (blk(s), 0))]
        x_args = [x_rows]
    return pl.pallas_call(
        functools.partial(_outproj_kernel, n_first=n_first, n_steps=n_steps, two_sources=two),
        grid=(n_steps + 1,),
        in_specs=[
            pl.BlockSpec((tm, ATT_Q_W), lambda s: (blk(s), 0)),
            pl.BlockSpec((tm, LRU_WIDTH), lambda s: (blk(s), 0)),
            pl.BlockSpec((tm, NA_W), lambda s: (blk(s), 0)),
            *x_specs,
            pl.BlockSpec((1, 6, D), lambda s: (jnp.minimum(blk(s) // per_seq, B), 0, 0)),
            pl.BlockSpec((1, D), lambda s: (0, 0)),
            pl.BlockSpec(w_out_bf.shape, lambda s: (0, 0)),
            pl.BlockSpec(w_router.shape, lambda s: (0, 0)),
            pl.BlockSpec((1, 128), lambda s: (0, 0)),
        ],
        out_specs=[
            pl.BlockSpec((tm, D), lambda s: (blk(s), 0)),
            pl.BlockSpec(packed.shape, lambda s: (blk(s), 0)),
            pl.BlockSpec((tm, 128), lambda s: (jnp.maximum(s - 1, 0), 0)),
            pl.BlockSpec((8, 128), lambda s: (0, 0)),
        ],
        out_shape=[
            jax.ShapeDtypeStruct((n_rows, D), F32),
            jax.ShapeDtypeStruct((n_rows, packed.shape[1]), packed.dtype),
            jax.ShapeDtypeStruct((n_rows, 128), F32),
            jax.ShapeDtypeStruct((8, 128), F32),
        ],
        scratch_shapes=[pltpu.VMEM((1, 128), F32), pltpu.VMEM((tm, 128), F32)],
        compiler_params=_cparams(("arbitrary",), VMEM_LIMIT),
        name="outproj",
    )(oa, ob, oc, *x_args, mod, g2.reshape(1, D), w_out_bf, w_router, b_router)


def _route_math(x):
    lane = lax.broadcasted_iota(jnp.int32, x.shape, 1)
    big = jnp.int32(1 << 20)
    ninf = -jnp.inf

    def first_argmax(vals):
        m = jnp.max(vals, axis=-1, keepdims=True)
        idx = jnp.min(jnp.where(vals == m, lane, big), axis=-1, keepdims=True)
        return m, idx

    lg = jnp.where(lane < MOE_GROUPS, x, ninf)
    mg, g_star = first_argmax(lg)
    g_gate = 1.0 / jnp.sum(jnp.exp(lg - mg), axis=-1, keepdims=True)
    lo = MOE_GROUPS + MOE_EXPERTS_PER_GROUP * g_star
    le = jnp.where(jnp.logical_and(lane >= lo, lane < lo + MOE_EXPERTS_PER_GROUP), x, ninf)
    v1, i1 = first_argmax(le)
    v2, i2 = first_argmax(jnp.where(lane == i1, ninf, le))
    e21 = jnp.exp(v2 - v1)
    w1 = g_gate / (1.0 + e21)
    w2 = g_gate * e21 / (1.0 + e21)
    e1 = (i1 - MOE_GROUPS).astype(F32)
    e2 = (i2 - MOE_GROUPS).astype(F32)
    return jnp.where(lane == 0, e1, jnp.where(lane == 1, e2, jnp.where(lane == 2, w1,
                     jnp.where(lane == 3, w2, 0.0))))


def _rank_math(r, carry, live):
    lane = lax.broadcasted_iota(jnp.int32, r.shape, 1).astype(F32)
    oh0 = jnp.where(lane == r[:, 0:1], 1.0, 0.0)
    oh1 = jnp.where(lane == r[:, 1:2], 1.0, 0.0)
    oh = oh0 + oh1
    n = r.shape[0]
    row = lax.broadcasted_iota(jnp.int32, (n, n), 0)
    col = lax.broadcasted_iota(jnp.int32, (n, n), 1)
    tri = jnp.where(col < row, 1.0, 0.0).astype(BF16)
    tot = carry[...] + jnp.dot(tri, oh.astype(BF16), preferred_element_type=F32)
    rank0 = jnp.sum(oh0 * tot, axis=-1, keepdims=True)
    rank1 = jnp.sum(oh1 * tot, axis=-1, keepdims=True)
    lane_i = lax.broadcasted_iota(jnp.int32, r.shape, 1)
    carry[...] = carry[...] + jnp.where(live, jnp.sum(oh, axis=0, keepdims=True), 0.0)
    return jnp.where(lane_i == 4, rank0, jnp.where(lane_i == 5, rank1, r))


def _dispatch_kernel(dest_ref, pad_ref, h_hbm, xs_hbm, hbuf, zeros, fsem, rsem, zsem, *, n_tok):
    i = pl.program_id(0)

    @pl.when(i == 0)
    def _():
        zeros[...] = jnp.zeros_like(zeros)
        pieces = [1 << b for b in range(MOE_BM.bit_length() - 2, 2, -1)]

        def copies(e):
            start, n = pad_ref[0, e], pad_ref[1, e]
            head = jnp.minimum((-start) & 7, n)
            out = []
            for r in range(7):
                out.append((r < head, pltpu.make_async_copy(zeros.at[pl.ds(0, 1)],
                                                            xs_hbm.at[pl.ds(start + r, 1)], zsem)))
            rest = n - head
            for p in pieces:
                at = pl.multiple_of(start + head + (rest & ~(2 * p - 1)), 8)
                out.append(((rest & p) != 0, pltpu.make_async_copy(zeros.at[pl.ds(0, p)],
                                                                   xs_hbm.at[pl.ds(at, p)], zsem)))
            return out

        def fill(e, c):
            for on, cp in copies(e):
                @pl.when(on)
                def _():
                    cp.start()
            return c

        def drain(e, c):
            for on, cp in copies(e):
                @pl.when(on)
                def _():
                    cp.wait()
            return c

        lax.fori_loop(0, MOE_EXPERTS, fill, 0)
        lax.fori_loop(0, MOE_EXPERTS, drain, 0)

        half = zeros.shape[0]
        n_tail = (xs_hbm.shape[0] - pad_ref[2, 0]) // half

        def tail_copy(j):
            at = pl.multiple_of(pad_ref[2, 0] + j * half, half)
            return pltpu.make_async_copy(zeros, xs_hbm.at[pl.ds(at, half)], zsem)

        def tail_fill(j, c):
            tail_copy(j).start()
            return c

        def tail_drain(j, c):
            tail_copy(j).wait()
            return c

        lax.fori_loop(0, n_tail, tail_fill, 0)
        lax.fori_loop(0, n_tail, tail_drain, 0)

    n = pl.num_programs(0)
    slot = i % 3
    nslot = (i + 1) % 3

    def fetch(blk, s):
        return pltpu.make_async_copy(h_hbm.at[pl.ds(pl.multiple_of(blk * TM, TM), TM)], hbuf.at[s], fsem.at[s])

    def rows_done(s):
        for k in range(2):
            pltpu.make_async_copy(hbuf.at[s], xs_hbm.at[pl.ds(0, TM)], rsem.at[s]).wait()

    @pl.when(i == 0)
    def _():
        fetch(0, 0).start()

    @pl.when(i >= 2)
    def _():
        rows_done(nslot)

    @pl.when(i + 1 < n)
    def _():
        fetch(i + 1, nslot).start()

    fetch(i, slot).wait()

    def issue(t, c):
        tok = i * TM + t
        for k in range(2):
            pltpu.make_async_copy(hbuf.at[slot, pl.ds(t, 1)], xs_hbm.at[pl.ds(dest_ref[k * n_tok + tok], 1)],
                                  rsem.at[slot]).start()
        return c

    lax.fori_loop(0, TM, issue, 0, unroll=8)

    @pl.when(i == n - 1)
    def _():
        rows_done(slot)

    @pl.when(jnp.logical_and(i == n - 1, i >= 1))
    def _():
        rows_done((i + 2) % 3)


def _dispatch(dest_flat, pad_info, h2, n_rows, nblk):
    W = h2.shape[1]
    return pl.pallas_call(
        functools.partial(_dispatch_kernel, n_tok=nblk * TM),
        grid_spec=pltpu.PrefetchScalarGridSpec(
            num_scalar_prefetch=2,
            grid=(nblk,),
            in_specs=[pl.BlockSpec(memory_space=pl.ANY)],
            out_specs=pl.BlockSpec(memory_space=pl.ANY),
            scratch_shapes=[pltpu.VMEM((3, TM, W), h2.dtype), pltpu.VMEM((MOE_BM // 2, W), h2.dtype),
                            pltpu.SemaphoreType.DMA((3,)), pltpu.SemaphoreType.DMA((3,)),
                            pltpu.SemaphoreType.DMA(())],
        ),
        out_shape=jax.ShapeDtypeStruct((n_rows, W), h2.dtype),
        compiler_params=pltpu.CompilerParams(dimension_semantics=("arbitrary",), has_side_effects=True),
        name="dispatch",
    )(dest_flat, pad_info, h2)


PLAN_EXPERT, PLAN_FIRST, PLAN_SLOT, PLAN_NEXT, PLAN_HALF = 0, 1, 2, 3, 4


def _block_plan(block_e, n_used, pad_starts, counts):
    nb = block_e.shape[0]
    idx = jnp.arange(nb, dtype=jnp.int32)
    is_first = (idx < n_used[0]) & ((idx == 0) | (block_e != jnp.roll(block_e, 1)))
    seg = jnp.cumsum(is_first.astype(jnp.int32)) - 1
    first_pos = jnp.where(is_first, idx, nb)
    at_or_after = jnp.flip(lax.cummin(jnp.flip(first_pos)))
    next_first = jnp.concatenate([at_or_after[1:], jnp.full((1,), nb, jnp.int32)])
    next_e = jnp.where(next_first < nb, block_e[jnp.minimum(next_first, nb - 1)], -1)
    valid = counts[block_e] - (idx * MOE_BM - pad_starts[block_e])
    half_only = (valid <= MOE_BM // 2).astype(jnp.int32)
    return jnp.stack([block_e, is_first.astype(jnp.int32), seg % 2, next_e, half_only]).astype(jnp.int32)


def _expert_weights_step(i, plan_ref, w_hbms, wbuf, wbf, sem, layer):
    def copies(e, slot):
        return [pltpu.make_async_copy(w.at[layer, e], wbuf.at[slot, n], sem.at[slot, n])
                for n, w in enumerate(w_hbms)]

    @pl.when(plan_ref[PLAN_FIRST, i] == 1)
    def _():
        slot = plan_ref[PLAN_SLOT, i]

        @pl.when(i == 0)
        def _():
            for c in copies(plan_ref[PLAN_EXPERT, 0], 0):
                c.start()

        for c in copies(plan_ref[PLAN_EXPERT, i], slot):
            c.wait()
        nrow, ncol = wbuf.shape[2], wbuf.shape[3]
        rows = 256

        def cast_rows(r, c):
            off = pl.multiple_of(r * rows, rows)
            for n in range(len(w_hbms)):
                wbf[pl.ds(off, rows), n * ncol:(n + 1) * ncol] = wbuf[slot, n, pl.ds(off, rows), :].astype(BF16)
            return c

        lax.fori_loop(0, nrow // rows, cast_rows, 0)
        nxt = plan_ref[PLAN_NEXT, i]

        @pl.when(nxt >= 0)
        def _():
            for c in copies(nxt, 1 - slot):
                c.start()


def _moe1_kernel(plan_ref, nu_ref, x_ref, wg_hbm, wu_hbm, o_ref, wbuf, wcat, sem, *, layer):
    i = pl.program_id(0)

    @pl.when(i < nu_ref[0])
    def _():
        _expert_weights_step(i, plan_ref, (wg_hbm, wu_hbm), wbuf, wcat, sem, layer)

        def gated(rows):
            h = jnp.dot(_unpack_bf16_pair(x_ref[rows, :]), wcat[...], preferred_element_type=F32)
            g = h[:, :MOE_HIDDEN]
            u = h[:, MOE_HIDDEN:]
            return (g * jax.nn.sigmoid(g) * u).astype(BF16)

        half = MOE_BM // 2

        @pl.when(plan_ref[PLAN_HALF, i] == 1)
        def _():
            o_ref[0:half, :] = gated(slice(0, half))
            o_ref[half:, :] = jnp.zeros((MOE_BM - half, MOE_HIDDEN), BF16)

        @pl.when(plan_ref[PLAN_HALF, i] == 0)
        def _():
            o_ref[...] = gated(slice(None))

    @pl.when(i >= nu_ref[0])
    def _():
        o_ref[...] = jnp.zeros_like(o_ref)


def _moe1(plan, n_used, xs, w_gate, w_up, layer):
    P, W = xs.shape
    D = w_gate.shape[2]
    nb = P // MOE_BM
    used_blk = lambda i, plan, nu: (jnp.minimum(i, nu[0] - 1), 0)
    return pl.pallas_call(
        functools.partial(_moe1_kernel, layer=layer),
        grid_spec=pltpu.PrefetchScalarGridSpec(
            num_scalar_prefetch=2,
            grid=(nb,),
            in_specs=[pl.BlockSpec((MOE_BM, W), used_blk), pl.BlockSpec(memory_space=pl.ANY),
                      pl.BlockSpec(memory_space=pl.ANY)],
            out_specs=pl.BlockSpec((MOE_BM, MOE_HIDDEN), lambda i, plan, nu: (i, 0)),
            scratch_shapes=[pltpu.VMEM((2, 2, D, MOE_HIDDEN), F32), pltpu.VMEM((D, 2 * MOE_HIDDEN), BF16),
                            pltpu.SemaphoreType.DMA((2, 2))],
        ),
        out_shape=jax.ShapeDtypeStruct((P, MOE_HIDDEN), BF16),
        compiler_params=_cparams(("arbitrary",), VMEM_LIMIT),
        name="moe_up",
    )(plan, n_used, xs, w_gate, w_up)


def _moe2_kernel(plan_ref, nu_ref, h_ref, wd_hbm, o_ref, wbuf, wbf, sem, *, layer):
    i = pl.program_id(0)

    @pl.when(i < nu_ref[0])
    def _():
        _expert_weights_step(i, plan_ref, (wd_hbm,), wbuf, wbf, sem, layer)
        half = MOE_BM // 2

        @pl.when(plan_ref[PLAN_HALF, i] == 1)
        def _():
            o_ref[0:half, :] = jnp.dot(h_ref[0:half, :], wbf[...], preferred_element_type=F32)
            o_ref[half:, :] = jnp.zeros((MOE_BM - half, o_ref.shape[1]), F32)

        @pl.when(plan_ref[PLAN_HALF, i] == 0)
        def _():
            o_ref[...] = jnp.dot(h_ref[...], wbf[...], preferred_element_type=F32)

    @pl.when(i >= nu_ref[0])
    def _():
        o_ref[...] = jnp.zeros_like(o_ref)


def _moe2(plan, n_used, hmid, w_down, layer):
    P = hmid.shape[0]
    D = w_down.shape[3]
    nb = P // MOE_BM
    return pl.pallas_call(
        functools.partial(_moe2_kernel, layer=layer),
        grid_spec=pltpu.PrefetchScalarGridSpec(
            num_scalar_prefetch=2,
            grid=(nb,),
            in_specs=[
                pl.BlockSpec((MOE_BM, MOE_HIDDEN), lambda i, plan, nu: (i, 0)),
                pl.BlockSpec(memory_space=pl.ANY),
            ],
            out_specs=pl.BlockSpec((MOE_BM, D), lambda i, plan, nu: (i, 0)),
            scratch_shapes=[pltpu.VMEM((2, 1, MOE_HIDDEN, D), F32), pltpu.VMEM((MOE_HIDDEN, D), BF16),
                            pltpu.SemaphoreType.DMA((2, 1))],
        ),
        out_shape=jax.ShapeDtypeStruct((P, D), F32),
        compiler_params=_cparams(("arbitrary",), VMEM_LIMIT),
        name="moe_down",
    )(plan, n_used, hmid, w_down)


def _combine_kernel(dest_ref, yb_hbm, x_ref, r_ref, mod_ref, fg_ref, o_ref, ybuf, sem, *, final):
    i = pl.program_id(0)
    n = pl.num_programs(0)

    n_tok = n * TM

    def row_copy(tok, k, t, slot):
        return pltpu.make_async_copy(yb_hbm.at[pl.ds(dest_ref[k * n_tok + tok], 1)],
                                     ybuf.at[slot, k, pl.ds(t, 1)], sem.at[slot])

    def issue_rows(blk, slot, t0):
        for r in range(8):
            for k in range(2):
                row_copy(blk * TM + t0 + r, k, t0 + r, slot).start()

    def wait_slot(slot):
        for k in range(2):
            pltpu.make_async_copy(yb_hbm.at[pl.ds(0, TM)], ybuf.at[slot, k], sem.at[slot]).wait()

    @pl.when(i == 0)
    def _():
        def first(j, c):
            issue_rows(0, 0, pl.multiple_of(j * 8, 8))
            return c
        lax.fori_loop(0, TM // 8, first, 0)

    def step(cur):
        @pl.when(i + 1 < n)
        def _():
            def body(j, c):
                issue_rows(i + 1, 1 - cur, pl.multiple_of(j * 8, 8))
                return c
            lax.fori_loop(0, TM // 8, body, 0)

        wait_slot(cur)
        r = r_ref[...]
        y = r[:, 2:3] * ybuf[cur, 0] + r[:, 3:4] * ybuf[cur, 1]
        xn = x_ref[...] + mod_ref[0, 5:6, :] * y
        if final:
            xn = _rms(xn) * fg_ref[...]
        o_ref[...] = xn

    @pl.when(i % 2 == 0)
    def _():
        step(0)

    @pl.when(i % 2 == 1)
    def _():
        step(1)


def _combine(dest_flat, yb, x_all, route, mod, final_g, seg_of, nblk, final):
    M, D = x_all.shape
    out_rows = nblk * TM if final else M
    kwargs = {} if final else {"input_output_aliases": {2: 0}}
    return pl.pallas_call(
        functools.partial(_combine_kernel, final=final),
        grid_spec=pltpu.PrefetchScalarGridSpec(
            num_scalar_prefetch=1,
            grid=(nblk,),
            in_specs=[
                pl.BlockSpec(memory_space=pl.ANY),
                pl.BlockSpec((TM, D), lambda i, d: (i, 0)),
                pl.BlockSpec((TM, 128), lambda i, d: (i, 0)),
                pl.BlockSpec((1, 6, D), lambda i, d: (seg_of(i), 0, 0)),
                pl.BlockSpec((1, D), lambda i, d: (0, 0)),
            ],
            out_specs=pl.BlockSpec((TM, D), lambda i, d: (i, 0)),
            scratch_shapes=[pltpu.VMEM((2, 2, TM, D), F32), pltpu.SemaphoreType.DMA((2,))],
        ),
        out_shape=jax.ShapeDtypeStruct((out_rows, D), F32),
        compiler_params=_cparams(("arbitrary",), VMEM_LIMIT),
        name="combine",
        **kwargs,
    )(dest_flat, yb, x_all, route, mod, final_g.reshape(1, D))


def _moe_layer(x_all, h2, route, cnt, mod, final_g, w_gate, w_up, w_down, layer, seg_of, nblk, final):
    T = nblk * TM
    e_ids = route[:, 0:2].T.astype(jnp.int32)
    ranks = route[:, 4:6].T.astype(jnp.int32)
    counts = cnt[0, :MOE_EXPERTS].astype(jnp.int32)
    padded = (counts + MOE_BM - 1) // MOE_BM * MOE_BM
    pad_ends = jnp.cumsum(padded)
    pad_starts = pad_ends - padded
    expert = jnp.arange(MOE_EXPERTS, dtype=jnp.int32)[:, None, None]
    start_of = jnp.sum(jnp.where(e_ids[None] == expert, pad_starts[:, None, None], 0), axis=0)
    dest = (start_of + ranks).reshape(-1)
    pad_info = jnp.stack([pad_starts + counts, padded - counts,
                          jnp.broadcast_to(pad_ends[-1], (MOE_EXPERTS,))]).astype(jnp.int32)
    n_blocks = -(-(2 * T + MOE_EXPERTS * (MOE_BM - 1)) // MOE_BM)
    block_starts = jnp.arange(n_blocks, dtype=jnp.int32) * MOE_BM
    block_e = jnp.sum((block_starts[:, None] >= pad_ends[None, :]).astype(jnp.int32), axis=1)
    block_e = jnp.minimum(block_e, MOE_EXPERTS - 1)
    n_used = (pad_ends[-1:] // MOE_BM).astype(jnp.int32)
    plan = _block_plan(block_e, n_used, pad_starts, counts)
    xs = _dispatch(dest, pad_info, h2, n_blocks * MOE_BM, nblk)
    hmid = _moe1(plan, n_used, xs, w_gate, w_up, layer)
    yb = _moe2(plan, n_used, hmid, w_down, layer)
    return _combine(dest, yb, x_all, route, mod, final_g, seg_of, nblk, final)


def _rope_tables(S):
    pos = jnp.arange(S, dtype=jnp.int32)
    rows = (pos // GRID_W).astype(F32)
    cols = (pos % GRID_W).astype(F32)
    n_freq = HEAD_DIM // 4
    inv = 1.0 / (ROPE_THETA ** (jnp.arange(n_freq, dtype=F32) / n_freq))
    cr, sr = jnp.cos(rows[:, None] * inv), jnp.sin(rows[:, None] * inv)
    cc, sc = jnp.cos(cols[:, None] * inv), jnp.sin(cols[:, None] * inv)
    cos_t = jnp.concatenate([cr, cr, cc, cc], axis=1)
    sin_t = jnp.concatenate([-sr, sr, -sc, sc], axis=1)
    cos_t = jnp.concatenate([cos_t, jnp.ones((TM, HEAD_DIM), F32)], axis=0)
    sin_t = jnp.concatenate([sin_t, jnp.zeros((TM, HEAD_DIM), F32)], axis=0)
    return cos_t, sin_t


def kernel(x, c, ctx, c_ctx, ada_w, ada_b, norm1_g, norm2_g, w_in, w_out, att_q_norm, att_k_norm, conv_w, conv_b, lru_wr, lru_br, lru_wi, lru_bi, lru_lambda, na_rpb, router_wg, router_bg, router_we, router_be, moe_w_gate, moe_w_up, moe_w_down, final_g):
    B, S, D = x.shape
    C = ctx.shape[1]
    L = ada_w.shape[0]
    assert C == TM and S % TM == 0 and S // TM >= 3 and B + 1 <= 8
    nbs = S // TM
    nlat = B * nbs
    m_lat = B * S
    M = m_lat + B * C

    def seg_of(i):
        return jnp.minimum(i // nbs, B)

    x_rows, ctx_rows = x.reshape(m_lat, D), ctx.reshape(B * C, D)
    cvec = jnp.zeros((8, D), F32).at[:B].set(c).at[B].set(c_ctx)
    mod_all = _adaln(cvec, ada_w, ada_b)[:, :B + 1].reshape(L, B + 1, 6, D)
    cos_t, sin_t = _rope_tables(S)
    g_att = ATT_Q_HEADS // ATT_KV_HEADS

    out = None
    for l in range(L):
        last = l == L - 1
        mod = mod_all[l]
        proj = _inproj(x_rows, ctx_rows, mod, norm1_g[l], w_in[l].astype(BF16), B, S)

        q, k_all, v_all = _qkprep(proj, cos_t, sin_t, att_q_norm[l], att_k_norm[l], B, S)
        oa = _gqa_latent(q, k_all, v_all, B, S)

        sp = jax.nn.softplus(-lru_lambda[l].astype(F32))
        hrev = _lru_dir(proj, None, conv_w[l], conv_b[l], lru_wr[l, 1], lru_br[l, 1], lru_wi[l, 1],
                        lru_bi[l, 1], sp[1], B, S, reverse=True)
        ob = _lru_dir(proj, hrev, conv_w[l], conv_b[l], lru_wr[l, 0], lru_br[l, 0], lru_wi[l, 0],
                      lru_bi[l, 0], sp[0], B, S, reverse=False)

        tl, tr = _na_bias_tables(na_rpb[l])
        oc = _na_latent(proj, tl, tr, B, S)

        if not last:
            oa = _ctx_attn(
                q, lambda b, h: (nlat + b, h),
                k_all, pl.BlockSpec((None, TM, HEAD_DIM), lambda b, h: (b, nbs, h // g_att)),
                v_all, pl.BlockSpec((None, TM, HEAD_DIM), lambda b, h: (b, nbs, 2 * (h // g_att))),
                oa, B, ATT_Q_HEADS, nlat, None)
            oc = _ctx_attn(
                proj, lambda b, h: (nlat + b, COL_QN // HEAD_DIM + h),
                proj, pl.BlockSpec((TM, HEAD_DIM), lambda b, h: (nlat + b, COL_KN // HEAD_DIM + h)),
                proj, pl.BlockSpec((TM, HEAD_DIM), lambda b, h: (nlat + b, COL_VN // HEAD_DIM + h)),
                oc, B, NA_HEADS, nlat, HEAD_DIM ** -0.5)

        nblk = nlat if last else M // TM
        w_router = jnp.zeros((D, 128), F32).at[:, :MOE_GROUPS].set(router_wg[l])
        w_router = w_router.at[:, MOE_GROUPS:MOE_GROUPS + MOE_EXPERTS].set(router_we[l])
        b_router = jnp.zeros((1, 128), F32).at[0, :MOE_GROUPS].set(router_bg[l])
        b_router = b_router.at[0, MOE_GROUPS:MOE_GROUPS + MOE_EXPERTS].set(router_be[l])
        x_all, h2, route, cnt = _outproj(oa, ob, oc, x_rows, ctx_rows, mod, norm2_g[l], w_out[l].astype(BF16),
                                         w_router.astype(BF16), b_router, B, S, nblk * TM)
        res = _moe_layer(x_all, h2, route, cnt, mod, final_g, moe_w_gate, moe_w_up, moe_w_down, l,
                         seg_of, nblk, last)
        if last:
            out = res
        else:
            x_rows, ctx_rows = res, None
    return out.reshape(B, S, D)
```

```python
import functools

import numpy as np
import jax
import jax.numpy as jnp
from jax import lax
from jax.experimental import pallas as pl
from jax.experimental.pallas import tpu as pltpu

F32 = jnp.float32
BF16 = jnp.bfloat16

GRID_W = 64
HEAD_DIM = 128
ATT_Q_HEADS = 4
ATT_KV_HEADS = 2
LRU_WIDTH = 1024
LRU_BLOCKS = 8
LRU_C = 8.0
CONV_W = 4
NA_HEADS = 4
NA_ROWS = 8
NA_COLS = 16
ATT_Q_W = ATT_Q_HEADS * HEAD_DIM
ATT_KV_W = ATT_KV_HEADS * HEAD_DIM
NA_W = NA_HEADS * HEAD_DIM
MOE_GROUPS = 4
MOE_EXPERTS_PER_GROUP = 8
MOE_EXPERTS = MOE_GROUPS * MOE_EXPERTS_PER_GROUP
MOE_HIDDEN = 1024
ROPE_THETA = 10000.0
EPS = 1e-6
LOG2E = 1.4426950408889634

COL_QA = 0
COL_KA = COL_QA + ATT_Q_W
COL_VA = COL_KA + ATT_KV_W
COL_UB = COL_VA + ATT_KV_W
COL_GB = COL_UB + LRU_WIDTH
COL_QN = COL_GB + LRU_WIDTH
COL_KN = COL_QN + NA_W
COL_VN = COL_KN + NA_W
IN_WIDTH = COL_VN + NA_W

TM = 256
MOE_BM = 256
NA_QROWS = 4
NA_WROWS = 12
NEG_BIAS = -1e30
VMEM_LIMIT = 56 * 1024 * 1024


def _cparams(sem, vmem=None):
    return pltpu.CompilerParams(dimension_semantics=sem, vmem_limit_bytes=vmem)


def _rms(x):
    return x * lax.rsqrt(jnp.mean(x * x, axis=-1, keepdims=True) + EPS)


def _adaln_kernel(c_ref, w_ref, b_ref, o_ref):
    cv = c_ref[...]
    s = cv * jax.nn.sigmoid(cv)
    o_ref[0] = jnp.dot(s.astype(BF16), w_ref[0].astype(BF16), preferred_element_type=F32) + b_ref[0]


def _adaln(cvec, ada_w, ada_b):
    L, D, N = ada_w.shape
    tn = 1024
    return pl.pallas_call(
        _adaln_kernel,
        grid=(L, N // tn),
        in_specs=[
            pl.BlockSpec((8, D), lambda l, j: (0, 0)),
            pl.BlockSpec((1, D, tn), lambda l, j: (l, 0, j)),
            pl.BlockSpec((1, 1, tn), lambda l, j: (l, 0, j)),
        ],
        out_specs=pl.BlockSpec((1, 8, tn), lambda l, j: (l, 0, j)),
        out_shape=jax.ShapeDtypeStruct((L, 8, N), F32),
        compiler_params=_cparams(("parallel", "parallel"), VMEM_LIMIT),
        name="adaln",
    )(cvec, ada_w, ada_b.reshape(L, 1, N))


def _inproj_kernel(*refs, n_first, two_sources):
    if two_sources:
        x_ref, xb_ref, mod_ref, g_ref, w_ref, o_ref = refs
        x = jnp.where(pl.program_id(0) < n_first, x_ref[...], xb_ref[...])
    else:
        x_ref, mod_ref, g_ref, w_ref, o_ref = refs
        x = x_ref[...]
    h = _rms(x) * g_ref[...]
    h = h * (1.0 + mod_ref[0, 1:2, :]) + mod_ref[0, 0:1, :]
    o_ref[...] = jnp.dot(h.astype(BF16), w_ref[...], preferred_element_type=F32)


def _inproj(x_rows, ctx_rows, mod, g1, w_in_bf, B, S):
    two = ctx_rows is not None
    D = x_rows.shape[1]
    M = x_rows.shape[0] + (ctx_rows.shape[0] if two else 0)
    N = w_in_bf.shape[1]
    tm = 512 if (S % 512 == 0 and (M - B * S) % 512 == 0) else TM
    per_seq = S // tm
    n_first = B * per_seq
    if two:
        n_ctx_blocks = ctx_rows.shape[0] // tm
        x_specs = [pl.BlockSpec((tm, D), lambda i: (jnp.minimum(i, n_first - 1), 0)),
                   pl.BlockSpec((tm, D), lambda i: (jnp.maximum(i - n_first, 0), 0),
                                pipeline_mode=pl.Buffered(1 if n_ctx_blocks == 1 else 2))]
        x_args = [x_rows, ctx_rows]
    else:
        x_specs = [pl.BlockSpec((tm, D), lambda i: (i, 0))]
        x_args = [x_rows]
    return pl.pallas_call(
        functools.partial(_inproj_kernel, n_first=n_first, two_sources=two),
        grid=(M // tm,),
        in_specs=[
            *x_specs,
            pl.BlockSpec((1, 6, D), lambda i: (jnp.minimum(i // per_seq, B), 0, 0)),
            pl.BlockSpec((1, D), lambda i: (0, 0)),
            pl.BlockSpec((D, N), lambda i: (0, 0), pipeline_mode=pl.Buffered(1)),
        ],
        out_specs=pl.BlockSpec((tm, N), lambda i: (i, 0)),
        out_shape=jax.ShapeDtypeStruct((M, N), F32),
        compiler_params=_cparams(("parallel",), VMEM_LIMIT),
        name="inproj",
    )(*x_args, mod, g1.reshape(1, D), w_in_bf)


def _norm_rope(t, g, cos, sin):
    y = _rms(t) * g
    lane = lax.broadcasted_iota(jnp.int32, y.shape, 1)
    first_half = (lane % 64) < 32
    partner = jnp.where(first_half, pltpu.roll(y, 96, 1), pltpu.roll(y, 32, 1))
    return y * cos + partner * sin


def _qkprep_kernel(p_ref, cos_ref, sin_ref, qg_ref, kg_ref, q_ref, k_ref, v_ref):
    cos = cos_ref[...]
    sin = sin_ref[...]
    scale = HEAD_DIM ** -0.5 * LOG2E
    for h in range(ATT_Q_HEADS):
        t = p_ref[:, COL_QA + h * HEAD_DIM:COL_QA + (h + 1) * HEAD_DIM]
        q_ref[:, h * HEAD_DIM:(h + 1) * HEAD_DIM] = (_norm_rope(t, qg_ref[...], cos, sin) * scale).astype(BF16)
    for h in range(ATT_KV_HEADS):
        t = p_ref[:, COL_KA + h * HEAD_DIM:COL_KA + (h + 1) * HEAD_DIM]
        k_ref[0, :, h * HEAD_DIM:(h + 1) * HEAD_DIM] = _norm_rope(t, kg_ref[...], cos, sin).astype(BF16)
        v_ref[0, :, 2 * h * HEAD_DIM:(2 * h + 1) * HEAD_DIM] = (
            p_ref[:, COL_VA + h * HEAD_DIM:COL_VA + (h + 1) * HEAD_DIM].astype(BF16))
        v_ref[0, :, (2 * h + 1) * HEAD_DIM:(2 * h + 2) * HEAD_DIM] = jnp.ones((p_ref.shape[0], HEAD_DIM), BF16)


def _qkprep(proj, cos_t, sin_t, qg, kg, B, S):
    M = proj.shape[0]
    nbs = S // TM
    nlat = B * nbs

    def b_of(i):
        return jnp.where(i < nlat, i // nbs, i - nlat)

    def pos_of(i):
        return jnp.where(i < nlat, i % nbs, nbs)

    kl = S + TM
    return pl.pallas_call(
        _qkprep_kernel,
        grid=(M // TM,),
        in_specs=[
            pl.BlockSpec((TM, COL_UB), lambda i: (i, 0)),
            pl.BlockSpec((TM, HEAD_DIM), lambda i: (pos_of(i), 0)),
            pl.BlockSpec((TM, HEAD_DIM), lambda i: (pos_of(i), 0)),
            pl.BlockSpec((1, HEAD_DIM), lambda i: (0, 0)),
            pl.BlockSpec((1, HEAD_DIM), lambda i: (0, 0)),
        ],
        out_specs=[
            pl.BlockSpec((TM, ATT_Q_W), lambda i: (i, 0)),
            pl.BlockSpec((1, TM, ATT_KV_W), lambda i: (b_of(i), pos_of(i), 0)),
            pl.BlockSpec((1, TM, 2 * ATT_KV_W), lambda i: (b_of(i), pos_of(i), 0)),
        ],
        out_shape=[
            jax.ShapeDtypeStruct((M, ATT_Q_W), BF16),
            jax.ShapeDtypeStruct((B, kl, ATT_KV_W), BF16),
            jax.ShapeDtypeStruct((B, kl, 2 * ATT_KV_W), BF16),
        ],
        compiler_params=_cparams(("parallel",), VMEM_LIMIT),
        name="qkprep",
    )(proj, cos_t, sin_t, qg.reshape(1, HEAD_DIM), kg.reshape(1, HEAD_DIM))


def _gqa_kernel(q_ref, k_ref, v_ref, o_in_ref, o_ref, q2_ref, sa_ref, sb_ref, m_ref, acc_ref, *, tk):
    del o_in_ref
    tq = q_ref.shape[0]
    q2_ref[0:tq, :] = q_ref[:, :HEAD_DIM]
    q2_ref[tq:, :] = q_ref[:, HEAD_DIM:]
    nk = k_ref.shape[1] // tk
    m_ref[...] = jnp.full(m_ref.shape, -jnp.inf, F32)
    acc_ref[...] = jnp.zeros(acc_ref.shape, F32)

    def scores(c, s_ref):
        off = pl.multiple_of(c * tk, tk)
        s_ref[...] = lax.dot_general(q2_ref[...], k_ref[0, pl.ds(off, tk), :], (((1,), (1,)), ((), ())),
                                     preferred_element_type=F32)

    def update(c, s_ref):
        off = pl.multiple_of(c * tk, tk)
        vc = v_ref[0, pl.ds(off, tk), :]
        for hh in range(2):
            rows = slice(hh * tq, (hh + 1) * tq)
            m = m_ref[rows, :]
            m_new = jnp.maximum(m, jnp.max(s_ref[rows, :], axis=-1, keepdims=True))
            m_ref[rows, :] = m_new
            p = jnp.exp2((s_ref[rows, :] - jnp.concatenate([m_new] * (tk // HEAD_DIM), axis=1)).astype(BF16))
            alpha = jnp.exp2(m - m_new)
            acc_ref[rows, :] = (jnp.concatenate([alpha, alpha], axis=1) * acc_ref[rows, :]
                                + jnp.dot(p, vc, preferred_element_type=F32))

    scores(0, sa_ref)
    n_pairs = (nk - 1) // 2

    def body(j, carry):
        scores(2 * j + 1, sb_ref)
        update(2 * j, sa_ref)
        scores(2 * j + 2, sa_ref)
        update(2 * j + 1, sb_ref)
        return carry

    lax.fori_loop(0, n_pairs, body, 0)
    if nk % 2 == 0:
        scores(nk - 1, sb_ref)
        update(nk - 2, sa_ref)
        update(nk - 1, sb_ref)
    else:
        update(nk - 1, sa_ref)
    acc = acc_ref[...]
    o = acc[:, :HEAD_DIM] / acc[:, HEAD_DIM:]
    o_ref[:, :HEAD_DIM] = o[:tq].astype(BF16)
    o_ref[:, HEAD_DIM:] = o[tq:].astype(BF16)


def _gqa_latent(q, k_all, v_all, B, S):
    M = q.shape[0]
    kl = k_all.shape[1]
    tq = 512 if S % 512 == 0 else TM
    nbs = S // tq
    tk = 768 if kl % 768 == 0 else TM
    g = ATT_Q_HEADS // ATT_KV_HEADS
    return pl.pallas_call(
        functools.partial(_gqa_kernel, tk=tk),
        grid=(B, ATT_KV_HEADS, nbs),
        in_specs=[
            pl.BlockSpec((tq, g * HEAD_DIM), lambda b, h, i: (b * nbs + i, h)),
            pl.BlockSpec((1, kl, HEAD_DIM), lambda b, h, i: (b, 0, h)),
            pl.BlockSpec((1, kl, 2 * HEAD_DIM), lambda b, h, i: (b, 0, h)),
            pl.BlockSpec(memory_space=pl.ANY),
        ],
        out_specs=pl.BlockSpec((tq, g * HEAD_DIM), lambda b, h, i: (b * nbs + i, h)),
        out_shape=jax.ShapeDtypeStruct((M, ATT_Q_W), BF16),
        input_output_aliases={3: 0},
        scratch_shapes=[
            pltpu.VMEM((g * tq, HEAD_DIM), BF16),
            pltpu.VMEM((g * tq, tk), F32),
            pltpu.VMEM((g * tq, tk), F32),
            pltpu.VMEM((g * tq, HEAD_DIM), F32),
            pltpu.VMEM((g * tq, 2 * HEAD_DIM), F32),
        ],
        compiler_params=_cparams(("parallel", "parallel", "parallel"), VMEM_LIMIT),
        name="gqa_latent",
    )(q, k_all, v_all, jnp.zeros((M, ATT_Q_W), BF16))


def _ctx_attn_kernel(q_ref, k_ref, v_ref, o_in_ref, o_ref, *, scale):
    del o_in_ref
    if scale is None:
        q = q_ref[...]
    else:
        q = (q_ref[...] * (scale * LOG2E)).astype(BF16)
    k = k_ref[...].astype(BF16)
    s = lax.dot_general(q, k, (((1,), (1,)), ((), ())), preferred_element_type=F32)
    p = jnp.exp2(s - jnp.max(s, axis=-1, keepdims=True))
    l = jnp.sum(p, axis=-1, keepdims=True)
    pv = jnp.dot(p.astype(BF16), v_ref[...].astype(BF16), preferred_element_type=F32)
    o_ref[...] = (pv / l).astype(BF16)


def _ctx_attn(q_arr, q_map, k_arr, k_spec, v_arr, v_spec, o_arr, B, n_heads, ctx_blk0, scale):
    return pl.pallas_call(
        functools.partial(_ctx_attn_kernel, scale=scale),
        grid=(B, n_heads),
        in_specs=[
            pl.BlockSpec((TM, HEAD_DIM), q_map),
            k_spec,
            v_spec,
            pl.BlockSpec(memory_space=pl.ANY),
        ],
        out_specs=pl.BlockSpec((TM, HEAD_DIM), lambda b, h: (ctx_blk0 + b, h)),
        out_shape=jax.ShapeDtypeStruct(o_arr.shape, o_arr.dtype),
        input_output_aliases={3: 0},
        compiler_params=_cparams(("parallel", "parallel"), VMEM_LIMIT),
        name="ctx_attn",
    )(q_arr, k_arr, v_arr, o_arr)


def _lru_kernel(u_ref, up_ref, un_ref, *rest, reverse, final):
    if final:
        (g_ref, hrev_ref, cw_ref, cb_ref, wr_ref, br_ref, wi_ref, bi_ref, sp_ref,
         o_ref, xpad, a_scr, b_scr, h_scr, hst) = rest
    else:
        (cw_ref, cb_ref, wr_ref, br_ref, wi_ref, bi_ref, sp_ref,
         o_ref, xpad, a_scr, b_scr, h_scr, hst) = rest
    s = pl.program_id(1)
    ns = pl.num_programs(1)
    T = u_ref.shape[0]
    W = u_ref.shape[1]

    @pl.when(s == 0)
    def _():
        hst[...] = jnp.zeros_like(hst)

    if reverse:
        j = ns - 1 - s
    else:
        j = s - 1
    is_first = jnp.logical_or(s == 0, j == 0)
    is_last = jnp.logical_or(s == 0, j == ns - 2)
    prev_rows = jnp.where(is_first, 0.0, up_ref[...])
    next_rows = jnp.where(is_last, 0.0, un_ref[...])
    xpad[0:8, :] = prev_rows
    xpad[8:8 + T, :] = u_ref[...]
    xpad[8 + T:16 + T, :] = next_rows
    u = (cb_ref[...]
         + xpad[7:7 + T, :] * cw_ref[0:1, :]
         + xpad[8:8 + T, :] * cw_ref[1:2, :]
         + xpad[9:9 + T, :] * cw_ref[2:3, :]
         + xpad[10:10 + T, :] * cw_ref[3:4, :])

    ub = u.astype(BF16)
    nb = LRU_WIDTH // LRU_BLOCKS
    for n in range(LRU_BLOCKS):
        sl = slice(n * nb, (n + 1) * nb)
        un_ = ub[:, sl]
        tr = jnp.tanh(jnp.dot(un_, wr_ref[n], preferred_element_type=F32) + br_ref[:, sl])
        ti = jnp.tanh(jnp.dot(un_, wi_ref[n], preferred_element_type=F32) + bi_ref[:, sl])
        ig = 0.5 * ti + 0.5
        log_a = sp_ref[:, sl] * tr + sp_ref[:, sl]
        a = jnp.exp(log_a)
        a_scr[:, sl] = a
        one_minus_a2 = -jnp.tanh(log_a) * (a * a + 1.0)
        root = one_minus_a2 * lax.rsqrt(jnp.maximum(one_minus_a2, 1e-30))
        b_scr[:, sl] = root * (ig * u[:, sl])

    def step(t, h):
        row = (T - 1 - t) if reverse else t
        h = a_scr[pl.ds(row, 1), :] * h + b_scr[pl.ds(row, 1), :]
        h_scr[pl.ds(row, 1), :] = h
        return h

    hst[...] = lax.fori_loop(0, T, step, hst[...], unroll=8)

    if final:
        y = h_scr[...] + hrev_ref[...]
        o_ref[...] = (y * jax.nn.gelu(g_ref[...])).astype(o_ref.dtype)
    else:
        o_ref[...] = h_scr[...]
    del W


def _lru_dir(proj, hrev, conv_w, conv_b, wr, br, wi, bi, sp, B, S, reverse):
    M = proj.shape[0]
    nbs = S // TM
    nlat = B * nbs
    final = hrev is not None
    ub_blk = COL_UB // LRU_WIDTH
    gb_blk = COL_GB // LRU_WIDTH
    r8 = TM // 8
    n8 = M // 8

    def blk(b, s):
        j = (nbs - s) if reverse else (s - 1)
        return jnp.where(s == 0, nlat + b, b * nbs + j)

    in_specs = [
        pl.BlockSpec((TM, LRU_WIDTH), lambda b, s: (blk(b, s), ub_blk)),
        pl.BlockSpec((8, LRU_WIDTH), lambda b, s: (jnp.maximum(blk(b, s) * r8 - 1, 0), ub_blk)),
        pl.BlockSpec((8, LRU_WIDTH), lambda b, s: (jnp.minimum((blk(b, s) + 1) * r8, n8 - 1), ub_blk)),
    ]
    args = [proj, proj, proj]
    if final:
        in_specs += [
            pl.BlockSpec((TM, LRU_WIDTH), lambda b, s: (blk(b, s), gb_blk)),
            pl.BlockSpec((TM, LRU_WIDTH), lambda b, s: (blk(b, s), 0)),
        ]
        args += [proj, hrev]
    const2 = lambda b, s: (0, 0)
    in_specs += [
        pl.BlockSpec((CONV_W, LRU_WIDTH), const2),
        pl.BlockSpec((1, LRU_WIDTH), const2),
        pl.BlockSpec((LRU_BLOCKS, LRU_WIDTH // LRU_BLOCKS, LRU_WIDTH // LRU_BLOCKS), lambda b, s: (0, 0, 0)),
        pl.BlockSpec((1, LRU_WIDTH), const2),
        pl.BlockSpec((LRU_BLOCKS, LRU_WIDTH // LRU_BLOCKS, LRU_WIDTH // LRU_BLOCKS), lambda b, s: (0, 0, 0)),
        pl.BlockSpec((1, LRU_WIDTH), const2),
        pl.BlockSpec((1, LRU_WIDTH), const2),
    ]
    args += [conv_w, conv_b.reshape(1, -1), (0.5 * wr).astype(BF16), 0.5 * br.reshape(1, -1),
             (0.5 * wi).astype(BF16), 0.5 * bi.reshape(1, -1), (-0.5 * LRU_C) * sp.reshape(1, -1)]
    return pl.pallas_call(
        functools.partial(_lru_kernel, reverse=reverse, final=final),
        grid=(B, nbs + 1),
        in_specs=in_specs,
        out_specs=pl.BlockSpec((TM, LRU_WIDTH), lambda b, s: (blk(b, s), 0)),
        out_shape=jax.ShapeDtypeStruct((M, LRU_WIDTH), BF16 if final else F32),
        scratch_shapes=[
            pltpu.VMEM((TM + 16, LRU_WIDTH), F32),
            pltpu.VMEM((TM, LRU_WIDTH), F32),
            pltpu.VMEM((TM, LRU_WIDTH), F32),
            pltpu.VMEM((TM, LRU_WIDTH), F32),
            pltpu.VMEM((1, LRU_WIDTH), F32),
        ],
        compiler_params=_cparams(("parallel", "arbitrary"), VMEM_LIMIT),
        name="lru_fwd" if final else "lru_rev",
    )(*args)


def _na_kernel(q_ref, k0_ref, k1_ref, k2_ref, v0_ref, v1_ref, v2_ref, kc_ref, vc_ref, tl_ref, tr_ref,
               o_in_ref, o_ref, s_scr, *, n_rows):
    del o_in_ref
    i = pl.program_id(1)
    nbs = n_rows // NA_QROWS
    r0 = NA_QROWS * i
    ws = NA_QROWS * jnp.clip(i - 1, 0, nbs - 3)
    scale = HEAD_DIM ** -0.5 * LOG2E
    nkeys = s_scr.shape[1]
    ones = jnp.ones((nkeys, HEAD_DIM), BF16)
    for h in range(NA_HEADS):
        hs = slice(h * HEAD_DIM, (h + 1) * HEAD_DIM)
        qh = (q_ref[:, hs] * scale).astype(BF16)
        kh = jnp.concatenate([k0_ref[:, hs], k1_ref[:, hs], k2_ref[:, hs], kc_ref[:, hs]], axis=0).astype(BF16)
        vh = jnp.concatenate([v0_ref[:, hs], v1_ref[:, hs], v2_ref[:, hs], vc_ref[:, hs]], axis=0).astype(BF16)
        vh = jnp.concatenate([vh, ones], axis=1)
        s_scr[...] = lax.dot_general(qh, kh, (((1,), (1,)), ((), ())), preferred_element_type=F32)
        for qr in range(NA_QROWS):
            r = r0 + qr
            rs = jnp.clip(r - NA_ROWS // 2, 0, n_rows - NA_ROWS)

            def tile_of(a):
                kr = ws + a
                valid = jnp.logical_and(kr >= rs, kr < rs + NA_ROWS)
                return jnp.where(valid, kr - r + (NA_ROWS - 1), 2 * NA_ROWS - 1)

            for pr in range(NA_WROWS // 2):
                bias = tl_ref[h, tile_of(2 * pr)] + tr_ref[h, tile_of(2 * pr + 1)]
                rsl = slice(qr * GRID_W, (qr + 1) * GRID_W)
                csl = slice(pr * 2 * GRID_W, (pr + 1) * 2 * GRID_W)
                s_scr[rsl, csl] = s_scr[rsl, csl] + bias
        s = s_scr[...]
        p = jnp.exp2((s - jnp.max(s, axis=-1, keepdims=True)).astype(BF16))
        pv = jnp.dot(p, vh, preferred_element_type=F32)
        o_ref[:, hs] = (pv[:, :HEAD_DIM] / pv[:, HEAD_DIM:]).astype(BF16)


def _na_latent(proj, tl, tr, B, S):
    M = proj.shape[0]
    nbs = S // TM
    nlat = B * nbs
    n_rows = S // GRID_W
    qb, kb, vb = COL_QN // NA_W, COL_KN // NA_W, COL_VN // NA_W

    def wblk(b, i, t):
        return b * nbs + jnp.clip(i - 1, 0, nbs - 3) + t

    blk = (TM, NA_W)
    in_specs = [pl.BlockSpec(blk, lambda b, i: (b * nbs + i, qb))]
    in_specs += [pl.BlockSpec(blk, functools.partial(lambda b, i, t: (wblk(b, i, t), kb), t=t)) for t in range(3)]
    in_specs += [pl.BlockSpec(blk, functools.partial(lambda b, i, t: (wblk(b, i, t), vb), t=t)) for t in range(3)]
    in_specs += [
        pl.BlockSpec(blk, lambda b, i: (nlat + b, kb)),
        pl.BlockSpec(blk, lambda b, i: (nlat + b, vb)),
        pl.BlockSpec(tl.shape, lambda b, i: (0, 0, 0, 0)),
        pl.BlockSpec(tr.shape, lambda b, i: (0, 0, 0, 0)),
        pl.BlockSpec(memory_space=pl.ANY),
    ]
    return pl.pallas_call(
        functools.partial(_na_kernel, n_rows=n_rows),
        grid=(B, nbs),
        in_specs=in_specs,
        out_specs=pl.BlockSpec(blk, lambda b, i: (b * nbs + i, 0)),
        out_shape=jax.ShapeDtypeStruct((M, NA_W), BF16),
        input_output_aliases={11: 0},
        scratch_shapes=[pltpu.VMEM((TM, NA_WROWS * GRID_W + TM), F32)],
        compiler_params=_cparams(("parallel", "parallel"), VMEM_LIMIT),
        name="na_latent",
    )(proj, proj, proj, proj, proj, proj, proj, proj, proj, tl, tr, jnp.zeros((M, NA_W), BF16))


def _na_bias_tables(rpb):
    j = np.arange(GRID_W)
    cs = np.clip(j - NA_COLS // 2, 0, GRID_W - NA_COLS)
    kc = np.arange(GRID_W)
    inside = (kc[None, :] >= cs[:, None]) & (kc[None, :] < cs[:, None] + NA_COLS)
    dc = kc[None, :] - j[:, None] + (NA_COLS - 1)
    sel = (dc[:, :, None] == np.arange(2 * NA_COLS - 1)[None, None, :]) & inside[:, :, None]
    t = jnp.einsum("hrd,jkd->hrjk", rpb * LOG2E, jnp.asarray(sel, F32), precision=lax.Precision.HIGHEST)
    t = jnp.where(inside[None, None], t, NEG_BIAS)
    t = jnp.concatenate([t, jnp.full_like(t[:, :1], NEG_BIAS)], axis=1)
    z = jnp.zeros_like(t)
    return jnp.concatenate([t, z], axis=-1), jnp.concatenate([z, t], axis=-1)


def _outproj_kernel(*refs, n_first, n_steps, two_sources):
    if two_sources:
        (oa_ref, ob_ref, oc_ref, x_ref, xb_ref, mod_ref, g_ref, w_ref, wr_ref, br_ref,
         xo_ref, h2_ref, rt_ref, cnt_ref, carry, lg_scr, mix_scr) = refs
    else:
        (oa_ref, ob_ref, oc_ref, x_ref, mod_ref, g_ref, w_ref, wr_ref, br_ref,
         xo_ref, h2_ref, rt_ref, cnt_ref, carry, lg_scr, mix_scr) = refs
    step = pl.program_id(0)
    i = jnp.minimum(step, n_steps - 1)

    @pl.when(step == 0)
    def _():
        carry[...] = jnp.zeros_like(carry)
        lg_scr[...] = jnp.zeros_like(lg_scr)

    lg_prev = lg_scr[...]

    k1 = ATT_Q_W
    k2 = k1 + LRU_WIDTH
    for r0 in range(0, x_ref.shape[0], TM):
        rows = slice(r0, r0 + TM)
        mix_scr[rows, 0:k1] = oa_ref[rows, :]
        mix_scr[rows, k1:k2] = ob_ref[rows, :]
        mix_scr[rows, k2:] = oc_ref[rows, :]
        acc = jnp.dot(mix_scr[rows, :], w_ref[...], preferred_element_type=F32)
        x_in = x_ref[rows, :]
        if two_sources:
            x_in = jnp.where(i < n_first, x_in, xb_ref[rows, :])
        xn = x_in + mod_ref[0, 2:3, :] * acc
        xo_ref[rows, :] = xn
        h2 = _rms(xn) * g_ref[...]
        h2 = h2 * (1.0 + mod_ref[0, 4:5, :]) + mod_ref[0, 3:4, :]
        h2_ref[rows, :] = _pack_bf16_pair(h2)
        lg_scr[rows, :] = jnp.dot(h2.astype(BF16), wr_ref[...], preferred_element_type=F32) + br_ref[...]

    route = _route_math(lg_prev)
    rt_ref[...] = _rank_math(route, carry, step > 0)
    cnt_ref[...] = jnp.broadcast_to(carry[...], cnt_ref.shape)


def _pack_bf16_pair(x):
    w = x.shape[1] // 2
    lo = pltpu.bitcast(x[:, :w].astype(BF16).astype(F32), jnp.uint32)
    hi = pltpu.bitcast(x[:, w:].astype(BF16).astype(F32), jnp.uint32)
    return hi | (lo >> 16)


def _unpack_bf16_pair(words):
    lo = pltpu.bitcast(words << 16, F32)
    hi = pltpu.bitcast(words & jnp.uint32(0xFFFF0000), F32)
    return jnp.concatenate([lo, hi], axis=1).astype(BF16)


def _outproj(oa, ob, oc, x_rows, ctx_rows, mod, g2, w_out_bf, w_router, b_router, B, S, n_rows):
    M = oa.shape[0]
    D = x_rows.shape[1]
    two = ctx_rows is not None
    tm = 512 if (S % 512 == 0 and (M - B * S) % 512 == 0) else TM
    per_seq = S // tm
    n_first = B * per_seq
    packed = jax.eval_shape(_pack_bf16_pair, jax.ShapeDtypeStruct((tm, D), F32))
    n_steps = n_rows // tm

    def blk(s):
        return jnp.minimum(s, n_steps - 1)

    if two:
        x_specs = [pl.BlockSpec((tm, D), lambda s: (jnp.minimum(blk(s), n_first - 1), 0)),
                   pl.BlockSpec((tm, D), lambda s: (jnp.maximum(blk(s) - n_first, 0), 0))]
        x_args = [x_rows, ctx_rows]
    else:
        x_specs = [pl.BlockSpec((tm, D), lambda s: (blk(s), 0))]
        x_args = [x_rows]
    return pl.pallas_call(
        functools.partial(_outproj_kernel, n_first=n_first, n_steps=n_steps, two_sources=two),
        grid=(n_steps + 1,),
        in_specs=[
            pl.BlockSpec((tm, ATT_Q_W), lambda s: (blk(s), 0)),
            pl.BlockSpec((tm, LRU_WIDTH), lambda s: (blk(s), 0)),
            pl.BlockSpec((tm, NA_W), lambda s: (blk(s), 0)),
            *x_specs,
            pl.BlockSpec((1, 6, D), lambda s: (jnp.minimum(blk(s) // per_seq, B), 0, 0)),
            pl.BlockSpec((1, D), lambda s: (0, 0)),
            pl.BlockSpec(w_out_bf.shape, lambda s: (0, 0)),
            pl.BlockSpec(w_router.shape, lambda s: (0, 0)),
            pl.BlockSpec((1, 128), lambda s: (0, 0)),
        ],
        out_specs=[
            pl.BlockSpec((tm, D), lambda s: (blk(s), 0)),
            pl.BlockSpec(packed.shape, lambda s: (blk(s), 0)),
            pl.BlockSpec((tm, 128), lambda s: (jnp.maximum(s - 1, 0), 0)),
            pl.BlockSpec((8, 128), lambda s: (0, 0)),
        ],
        out_shape=[
            jax.ShapeDtypeStruct((n_rows, D), F32),
            jax.ShapeDtypeStruct((n_rows, packed.shape[1]), packed.dtype),
            jax.ShapeDtypeStruct((n_rows, 128), F32),
            jax.ShapeDtypeStruct((8, 128), F32),
        ],
        scratch_shapes=[pltpu.VMEM((1, 128), F32), pltpu.VMEM((tm, 128), F32),
                        pltpu.VMEM((tm, w_out_bf.shape[0]), BF16)],
        compiler_params=_cparams(("arbitrary",), VMEM_LIMIT),
        name="outproj",
    )(oa, ob, oc, *x_args, mod, g2.reshape(1, D), w_out_bf, w_router, b_router)


def _route_math(x):
    lane = lax.broadcasted_iota(jnp.int32, x.shape, 1)
    big = jnp.int32(1 << 20)
    ninf = -jnp.inf

    def first_argmax(vals):
        m = jnp.max(vals, axis=-1, keepdims=True)
        idx = jnp.min(jnp.where(vals == m, lane, big), axis=-1, keepdims=True)
        return m, idx

    lg = jnp.where(lane < MOE_GROUPS, x, ninf)
    mg, g_star = first_argmax(lg)
    g_gate = 1.0 / jnp.sum(jnp.exp(lg - mg), axis=-1, keepdims=True)
    lo = MOE_GROUPS + MOE_EXPERTS_PER_GROUP * g_star
    le = jnp.where(jnp.logical_and(lane >= lo, lane < lo + MOE_EXPERTS_PER_GROUP), x, ninf)
    v1, i1 = first_argmax(le)
    v2, i2 = first_argmax(jnp.where(lane == i1, ninf, le))
    e21 = jnp.exp(v2 - v1)
    w1 = g_gate / (1.0 + e21)
    w2 = g_gate * e21 / (1.0 + e21)
    e1 = (i1 - MOE_GROUPS).astype(F32)
    e2 = (i2 - MOE_GROUPS).astype(F32)
    return jnp.where(lane == 0, e1, jnp.where(lane == 1, e2, jnp.where(lane == 2, w1,
                     jnp.where(lane == 3, w2, 0.0))))


def _rank_math(r, carry, live):
    lane = lax.broadcasted_iota(jnp.int32, r.shape, 1).astype(F32)
    oh0 = jnp.where(lane == r[:, 0:1], 1.0, 0.0)
    oh1 = jnp.where(lane == r[:, 1:2], 1.0, 0.0)
    oh = oh0 + oh1
    n = r.shape[0]
    row = lax.broadcasted_iota(jnp.int32, (n, n), 0)
    col = lax.broadcasted_iota(jnp.int32, (n, n), 1)
    tri = jnp.where(col < row, 1.0, 0.0).astype(BF16)
    tot = carry[...] + jnp.dot(tri, oh.astype(BF16), preferred_element_type=F32)
    rank0 = jnp.sum(oh0 * tot, axis=-1, keepdims=True)
    rank1 = jnp.sum(oh1 * tot, axis=-1, keepdims=True)
    lane_i = lax.broadcasted_iota(jnp.int32, r.shape, 1)
    carry[...] = carry[...] + jnp.where(live, jnp.sum(oh, axis=0, keepdims=True), 0.0)
    return jnp.where(lane_i == 4, rank0, jnp.where(lane_i == 5, rank1, r))


def _dispatch_kernel(dest_ref, pad_ref, h_hbm, xs_hbm, hbuf, zeros, fsem, rsem, zsem, *, n_tok):
    i = pl.program_id(0)

    @pl.when(i == 0)
    def _():
        zeros[...] = jnp.zeros_like(zeros)
        pieces = [1 << b for b in range(MOE_BM.bit_length() - 2, 2, -1)]

        def copies(e):
            start, n = pad_ref[0, e], pad_ref[1, e]
            head = jnp.minimum((-start) & 7, n)
            out = []
            for r in range(7):
                out.append((r < head, pltpu.make_async_copy(zeros.at[pl.ds(0, 1)],
                                                            xs_hbm.at[pl.ds(start + r, 1)], zsem)))
            rest = n - head
            for p in pieces:
                at = pl.multiple_of(start + head + (rest & ~(2 * p - 1)), 8)
                out.append(((rest & p) != 0, pltpu.make_async_copy(zeros.at[pl.ds(0, p)],
                                                                   xs_hbm.at[pl.ds(at, p)], zsem)))
            return out

        def fill(e, c):
            for on, cp in copies(e):
                @pl.when(on)
                def _():
                    cp.start()
            return c

        def drain(e, c):
            for on, cp in copies(e):
                @pl.when(on)
                def _():
                    cp.wait()
            return c

        lax.fori_loop(0, MOE_EXPERTS, fill, 0)
        lax.fori_loop(0, MOE_EXPERTS, drain, 0)

        half = zeros.shape[0]
        n_tail = (xs_hbm.shape[0] - pad_ref[2, 0]) // half

        def tail_copy(j):
            at = pl.multiple_of(pad_ref[2, 0] + j * half, half)
            return pltpu.make_async_copy(zeros, xs_hbm.at[pl.ds(at, half)], zsem)

        def tail_fill(j, c):
            tail_copy(j).start()
            return c

        def tail_drain(j, c):
            tail_copy(j).wait()
            return c

        lax.fori_loop(0, n_tail, tail_fill, 0)
        lax.fori_loop(0, n_tail, tail_drain, 0)

    n = pl.num_programs(0)
    slot = i % 3
    nslot = (i + 1) % 3

    def fetch(blk, s):
        return pltpu.make_async_copy(h_hbm.at[pl.ds(pl.multiple_of(blk * TM, TM), TM)], hbuf.at[s], fsem.at[s])

    def rows_done(s):
        for k in range(2):
            pltpu.make_async_copy(hbuf.at[s], xs_hbm.at[pl.ds(0, TM)], rsem.at[s]).wait()

    @pl.when(i == 0)
    def _():
        fetch(0, 0).start()

    @pl.when(i >= 2)
    def _():
        rows_done(nslot)

    @pl.when(i + 1 < n)
    def _():
        fetch(i + 1, nslot).start()

    fetch(i, slot).wait()

    def issue(t, c):
        tok = i * TM + t
        for k in range(2):
            pltpu.make_async_copy(hbuf.at[slot, pl.ds(t, 1)], xs_hbm.at[pl.ds(dest_ref[k * n_tok + tok], 1)],
                                  rsem.at[slot]).start()
        return c

    lax.fori_loop(0, TM, issue, 0, unroll=8)

    @pl.when(i == n - 1)
    def _():
        rows_done(slot)

    @pl.when(jnp.logical_and(i == n - 1, i >= 1))
    def _():
        rows_done((i + 2) % 3)


def _dispatch(dest_flat, pad_info, h2, n_rows, nblk):
    W = h2.shape[1]
    return pl.pallas_call(
        functools.partial(_dispatch_kernel, n_tok=nblk * TM),
        grid_spec=pltpu.PrefetchScalarGridSpec(
            num_scalar_prefetch=2,
            grid=(nblk,),
            in_specs=[pl.BlockSpec(memory_space=pl.ANY)],
            out_specs=pl.BlockSpec(memory_space=pl.ANY),
            scratch_shapes=[pltpu.VMEM((3, TM, W), h2.dtype), pltpu.VMEM((MOE_BM // 2, W), h2.dtype),
                            pltpu.SemaphoreType.DMA((3,)), pltpu.SemaphoreType.DMA((3,)),
                            pltpu.SemaphoreType.DMA(())],
        ),
        out_shape=jax.ShapeDtypeStruct((n_rows, W), h2.dtype),
        compiler_params=pltpu.CompilerParams(dimension_semantics=("arbitrary",), has_side_effects=True),
        name="dispatch",
    )(dest_flat, pad_info, h2)


PLAN_EXPERT, PLAN_FIRST, PLAN_SLOT, PLAN_NEXT, PLAN_HALF = 0, 1, 2, 3, 4


def _block_plan(block_e, n_used, pad_starts, counts):
    nb = block_e.shape[0]
    idx = jnp.arange(nb, dtype=jnp.int32)
    is_first = (idx < n_used[0]) & ((idx == 0) | (block_e != jnp.roll(block_e, 1)))
    seg = jnp.cumsum(is_first.astype(jnp.int32)) - 1
    first_pos = jnp.where(is_first, idx, nb)
    at_or_after = jnp.flip(lax.cummin(jnp.flip(first_pos)))
    next_first = jnp.concatenate([at_or_after[1:], jnp.full((1,), nb, jnp.int32)])
    next_e = jnp.where(next_first < nb, block_e[jnp.minimum(next_first, nb - 1)], -1)
    valid = counts[block_e] - (idx * MOE_BM - pad_starts[block_e])
    half_only = (valid <= MOE_BM // 2).astype(jnp.int32)
    return jnp.stack([block_e, is_first.astype(jnp.int32), seg % 2, next_e, half_only]).astype(jnp.int32)


def _expert_weights_step(i, plan_ref, w_hbms, wbuf, wbf, sem, layer):
    def copies(e, slot):
        return [pltpu.make_async_copy(w.at[layer, e], wbuf.at[slot, n], sem.at[slot, n])
                for n, w in enumerate(w_hbms)]

    @pl.when(plan_ref[PLAN_FIRST, i] == 1)
    def _():
        slot = plan_ref[PLAN_SLOT, i]

        @pl.when(i == 0)
        def _():
            for c in copies(plan_ref[PLAN_EXPERT, 0], 0):
                c.start()

        for c in copies(plan_ref[PLAN_EXPERT, i], slot):
            c.wait()
        nrow, ncol = wbuf.shape[2], wbuf.shape[3]
        rows = 256

        def cast_rows(r, c):
            off = pl.multiple_of(r * rows, rows)
            for n in range(len(w_hbms)):
                wbf[pl.ds(off, rows), n * ncol:(n + 1) * ncol] = wbuf[slot, n, pl.ds(off, rows), :].astype(BF16)
            return c

        lax.fori_loop(0, nrow // rows, cast_rows, 0)
        nxt = plan_ref[PLAN_NEXT, i]

        @pl.when(nxt >= 0)
        def _():
            for c in copies(nxt, 1 - slot):
                c.start()


def _moe1_kernel(plan_ref, nu_ref, x_ref, wg_hbm, wu_hbm, o_ref, wbuf, wcat, sem, *, layer):
    i = pl.program_id(0)

    @pl.when(i < nu_ref[0])
    def _():
        _expert_weights_step(i, plan_ref, (wg_hbm, wu_hbm), wbuf, wcat, sem, layer)

        def gated(rows):
            h = jnp.dot(_unpack_bf16_pair(x_ref[rows, :]), wcat[...], preferred_element_type=F32)
            g = h[:, :MOE_HIDDEN]
            u = h[:, MOE_HIDDEN:]
            return (g * jax.nn.sigmoid(g) * u).astype(BF16)

        half = MOE_BM // 2

        @pl.when(plan_ref[PLAN_HALF, i] == 1)
        def _():
            o_ref[0:half, :] = gated(slice(0, half))
            o_ref[half:, :] = jnp.zeros((MOE_BM - half, MOE_HIDDEN), BF16)

        @pl.when(plan_ref[PLAN_HALF, i] == 0)
        def _():
            o_ref[...] = gated(slice(None))

    @pl.when(i >= nu_ref[0])
    def _():
        o_ref[...] = jnp.zeros_like(o_ref)


def _moe1(plan, n_used, xs, w_gate, w_up, layer):
    P, W = xs.shape
    D = w_gate.shape[2]
    nb = P // MOE_BM
    used_blk = lambda i, plan, nu: (jnp.minimum(i, nu[0] - 1), 0)
    return pl.pallas_call(
        functools.partial(_moe1_kernel, layer=layer),
        grid_spec=pltpu.PrefetchScalarGridSpec(
            num_scalar_prefetch=2,
            grid=(nb,),
            in_specs=[pl.BlockSpec((MOE_BM, W), used_blk), pl.BlockSpec(memory_space=pl.ANY),
                      pl.BlockSpec(memory_space=pl.ANY)],
            out_specs=pl.BlockSpec((MOE_BM, MOE_HIDDEN), lambda i, plan, nu: (i, 0)),
            scratch_shapes=[pltpu.VMEM((2, 2, D, MOE_HIDDEN), F32), pltpu.VMEM((D, 2 * MOE_HIDDEN), BF16),
                            pltpu.SemaphoreType.DMA((2, 2))],
        ),
        out_shape=jax.ShapeDtypeStruct((P, MOE_HIDDEN), BF16),
        compiler_params=_cparams(("arbitrary",), VMEM_LIMIT),
        name="moe_up",
    )(plan, n_used, xs, w_gate, w_up)


def _moe2_kernel(plan_ref, nu_ref, h_ref, wd_hbm, o_ref, wbuf, wbf, sem, *, layer):
    i = pl.program_id(0)

    @pl.when(i < nu_ref[0])
    def _():
        _expert_weights_step(i, plan_ref, (wd_hbm,), wbuf, wbf, sem, layer)
        half = MOE_BM // 2

        @pl.when(plan_ref[PLAN_HALF, i] == 1)
        def _():
            o_ref[0:half, :] = jnp.dot(h_ref[0:half, :], wbf[...], preferred_element_type=F32)
            o_ref[half:, :] = jnp.zeros((MOE_BM - half, o_ref.shape[1]), F32)

        @pl.when(plan_ref[PLAN_HALF, i] == 0)
        def _():
            o_ref[...] = jnp.dot(h_ref[...], wbf[...], preferred_element_type=F32)

    @pl.when(i >= nu_ref[0])
    def _():
        o_ref[...] = jnp.zeros_like(o_ref)


def _moe2(plan, n_used, hmid, w_down, layer):
    P = hmid.shape[0]
    D = w_down.shape[3]
    nb = P // MOE_BM
    return pl.pallas_call(
        functools.partial(_moe2_kernel, layer=layer),
        grid_spec=pltpu.PrefetchScalarGridSpec(
            num_scalar_prefetch=2,
            grid=(nb,),
            in_specs=[
                pl.BlockSpec((MOE_BM, MOE_HIDDEN), lambda i, plan, nu: (i, 0)),
                pl.BlockSpec(memory_space=pl.ANY),
            ],
            out_specs=pl.BlockSpec((MOE_BM, D), lambda i, plan, nu: (i, 0)),
            scratch_shapes=[pltpu.VMEM((2, 1, MOE_HIDDEN, D), F32), pltpu.VMEM((MOE_HIDDEN, D), BF16),
                            pltpu.SemaphoreType.DMA((2, 1))],
        ),
        out_shape=jax.ShapeDtypeStruct((P, D), F32),
        compiler_params=_cparams(("arbitrary",), VMEM_LIMIT),
        name="moe_down",
    )(plan, n_used, hmid, w_down)


def _combine_kernel(dest_ref, yb_hbm, x_ref, r_ref, mod_ref, fg_ref, o_ref, ybuf, sem, *, final):
    i = pl.program_id(0)
    n = pl.num_programs(0)

    n_tok = n * TM

    def row_copy(tok, k, t, slot):
        return pltpu.make_async_copy(yb_hbm.at[pl.ds(dest_ref[k * n_tok + tok], 1)],
                                     ybuf.at[slot, k, pl.ds(t, 1)], sem.at[slot])

    def issue_rows(blk, slot, t0):
        for r in range(8):
            for k in range(2):
                row_copy(blk * TM + t0 + r, k, t0 + r, slot).start()

    def wait_slot(slot):
        for k in range(2):
            pltpu.make_async_copy(yb_hbm.at[pl.ds(0, TM)], ybuf.at[slot, k], sem.at[slot]).wait()

    @pl.when(i == 0)
    def _():
        def first(j, c):
            issue_rows(0, 0, pl.multiple_of(j * 8, 8))
            return c
        lax.fori_loop(0, TM // 8, first, 0)

    def step(cur):
        @pl.when(i + 1 < n)
        def _():
            def body(j, c):
                issue_rows(i + 1, 1 - cur, pl.multiple_of(j * 8, 8))
                return c
            lax.fori_loop(0, TM // 8, body, 0)

        wait_slot(cur)
        r = r_ref[...]
        y = r[:, 2:3] * ybuf[cur, 0] + r[:, 3:4] * ybuf[cur, 1]
        xn = x_ref[...] + mod_ref[0, 5:6, :] * y
        if final:
            xn = _rms(xn) * fg_ref[...]
        o_ref[...] = xn

    @pl.when(i % 2 == 0)
    def _():
        step(0)

    @pl.when(i % 2 == 1)
    def _():
        step(1)


def _combine(dest_flat, yb, x_all, route, mod, final_g, seg_of, nblk, final):
    M, D = x_all.shape
    out_rows = nblk * TM if final else M
    kwargs = {} if final else {"input_output_aliases": {2: 0}}
    return pl.pallas_call(
        functools.partial(_combine_kernel, final=final),
        grid_spec=pltpu.PrefetchScalarGridSpec(
            num_scalar_prefetch=1,
            grid=(nblk,),
            in_specs=[
                pl.BlockSpec(memory_space=pl.ANY),
                pl.BlockSpec((TM, D), lambda i, d: (i, 0)),
                pl.BlockSpec((TM, 128), lambda i, d: (i, 0)),
                pl.BlockSpec((1, 6, D), lambda i, d: (seg_of(i), 0, 0)),
                pl.BlockSpec((1, D), lambda i, d: (0, 0)),
            ],
            out_specs=pl.BlockSpec((TM, D), lambda i, d: (i, 0)),
            scratch_shapes=[pltpu.VMEM((2, 2, TM, D), F32), pltpu.SemaphoreType.DMA((2,))],
        ),
        out_shape=jax.ShapeDtypeStruct((out_rows, D), F32),
        compiler_params=_cparams(("arbitrary",), VMEM_LIMIT),
        name="combine",
        **kwargs,
    )(dest_flat, yb, x_all, route, mod, final_g.reshape(1, D))


def _moe_layer(x_all, h2, route, cnt, mod, final_g, w_gate, w_up, w_down, layer, seg_of, nblk, final):
    T = nblk * TM
    e_ids = route[:, 0:2].T.astype(jnp.int32)
    ranks = route[:, 4:6].T.astype(jnp.int32)
    counts = cnt[0, :MOE_EXPERTS].astype(jnp.int32)
    padded = (counts + MOE_BM - 1) // MOE_BM * MOE_BM
    pad_ends = jnp.cumsum(padded)
    pad_starts = pad_ends - padded
    expert = jnp.arange(MOE_EXPERTS, dtype=jnp.int32)[:, None, None]
    start_of = jnp.sum(jnp.where(e_ids[None] == expert, pad_starts[:, None, None], 0), axis=0)
    dest = (start_of + ranks).reshape(-1)
    pad_info = jnp.stack([pad_starts + counts, padded - counts,
                          jnp.broadcast_to(pad_ends[-1], (MOE_EXPERTS,))]).astype(jnp.int32)
    n_blocks = -(-(2 * T + MOE_EXPERTS * (MOE_BM - 1)) // MOE_BM)
    block_starts = jnp.arange(n_blocks, dtype=jnp.int32) * MOE_BM
    block_e = jnp.sum((block_starts[:, None] >= pad_ends[None, :]).astype(jnp.int32), axis=1)
    block_e = jnp.minimum(block_e, MOE_EXPERTS - 1)
    n_used = (pad_ends[-1:] // MOE_BM).astype(jnp.int32)
    plan = _block_plan(block_e, n_used, pad_starts, counts)
    xs = _dispatch(dest, pad_info, h2, n_blocks * MOE_BM, nblk)
    hmid = _moe1(plan, n_used, xs, w_gate, w_up, layer)
    yb = _moe2(plan, n_used, hmid, w_down, layer)
    return _combine(dest, yb, x_all, route, mod, final_g, seg_of, nblk, final)


def _rope_tables(S):
    pos = jnp.arange(S, dtype=jnp.int32)
    rows = (pos // GRID_W).astype(F32)
    cols = (pos % GRID_W).astype(F32)
    n_freq = HEAD_DIM // 4
    inv = 1.0 / (ROPE_THETA ** (jnp.arange(n_freq, dtype=F32) / n_freq))
    cr, sr = jnp.cos(rows[:, None] * inv), jnp.sin(rows[:, None] * inv)
    cc, sc = jnp.cos(cols[:, None] * inv), jnp.sin(cols[:, None] * inv)
    cos_t = jnp.concatenate([cr, cr, cc, cc], axis=1)
    sin_t = jnp.concatenate([-sr, sr, -sc, sc], axis=1)
    cos_t = jnp.concatenate([cos_t, jnp.ones((TM, HEAD_DIM), F32)], axis=0)
    sin_t = jnp.concatenate([sin_t, jnp.zeros((TM, HEAD_DIM), F32)], axis=0)
    return cos_t, sin_t


def kernel(x, c, ctx, c_ctx, ada_w, ada_b, norm1_g, norm2_g, w_in, w_out, att_q_norm, att_k_norm, conv_w, conv_b, lru_wr, lru_br, lru_wi, lru_bi, lru_lambda, na_rpb, router_wg, router_bg, router_we, router_be, moe_w_gate, moe_w_up, moe_w_down, final_g):
    B, S, D = x.shape
    C = ctx.shape[1]
    L = ada_w.shape[0]
    assert C == TM and S % TM == 0 and S // TM >= 3 and B + 1 <= 8
    nbs = S // TM
    nlat = B * nbs
    m_lat = B * S
    M = m_lat + B * C

    def seg_of(i):
        return jnp.minimum(i // nbs, B)

    x_rows, ctx_rows = x.reshape(m_lat, D), ctx.reshape(B * C, D)
    cvec = jnp.zeros((8, D), F32).at[:B].set(c).at[B].set(c_ctx)
    mod_all = _adaln(cvec, ada_w, ada_b)[:, :B + 1].reshape(L, B + 1, 6, D)
    cos_t, sin_t = _rope_tables(S)
    g_att = ATT_Q_HEADS // ATT_KV_HEADS

    out = None
    for l in range(L):
        last = l == L - 1
        mod = mod_all[l]
        proj = _inproj(x_rows, ctx_rows, mod, norm1_g[l], w_in[l].astype(BF16), B, S)

        q, k_all, v_all = _qkprep(proj, cos_t, sin_t, att_q_norm[l], att_k_norm[l], B, S)
        oa = _gqa_latent(q, k_all, v_all, B, S)

        sp = jax.nn.softplus(-lru_lambda[l].astype(F32))
        hrev = _lru_dir(proj, None, conv_w[l], conv_b[l], lru_wr[l, 1], lru_br[l, 1], lru_wi[l, 1],
                        lru_bi[l, 1], sp[1], B, S, reverse=True)
        ob = _lru_dir(proj, hrev, conv_w[l], conv_b[l], lru_wr[l, 0], lru_br[l, 0], lru_wi[l, 0],
                      lru_bi[l, 0], sp[0], B, S, reverse=False)

        tl, tr = _na_bias_tables(na_rpb[l])
        oc = _na_latent(proj, tl, tr, B, S)

        if not last:
            oa = _ctx_attn(
                q, lambda b, h: (nlat + b, h),
                k_all, pl.BlockSpec((None, TM, HEAD_DIM), lambda b, h: (b, nbs, h // g_att)),
                v_all, pl.BlockSpec((None, TM, HEAD_DIM), lambda b, h: (b, nbs, 2 * (h // g_att))),
                oa, B, ATT_Q_HEADS, nlat, None)
            oc = _ctx_attn(
                proj, lambda b, h: (nlat + b, COL_QN // HEAD_DIM + h),
                proj, pl.BlockSpec((TM, HEAD_DIM), lambda b, h: (nlat + b, COL_KN // HEAD_DIM + h)),
                proj, pl.BlockSpec((TM, HEAD_DIM), lambda b, h: (nlat + b, COL_VN // HEAD_DIM + h)),
                oc, B, NA_HEADS, nlat, HEAD_DIM ** -0.5)

        nblk = nlat if last else M // TM
        w_router = jnp.zeros((D, 128), F32).at[:, :MOE_GROUPS].set(router_wg[l])
        w_router = w_router.at[:, MOE_GROUPS:MOE_GROUPS + MOE_EXPERTS].set(router_we[l])
        b_router = jnp.zeros((1, 128), F32).at[0, :MOE_GROUPS].set(router_bg[l])
        b_router = b_router.at[0, MOE_GROUPS:MOE_GROUPS + MOE_EXPERTS].set(router_be[l])
        x_all, h2, route, cnt = _outproj(oa, ob, oc, x_rows, ctx_rows, mod, norm2_g[l], w_out[l].astype(BF16),
                                         w_router.astype(BF16), b_router, B, S, nblk * TM)
        res = _moe_layer(x_all, h2, route, cnt, mod, final_g, moe_w_gate, moe_w_up, moe_w_down, l,
                         seg_of, nblk, last)
        if last:
            out = res
        else:
            x_rows, ctx_rows = res, None
    return out.reshape(B, S, D)
```

```python
import functools

import numpy as np
import jax
import jax.numpy as jnp
from jax import lax
from jax.experimental import pallas as pl
from jax.experimental.pallas import tpu as pltpu

F32 = jnp.float32
BF16 = jnp.bfloat16

GRID_W = 64
HEAD_DIM = 128
ATT_Q_HEADS = 4
ATT_KV_HEADS = 2
LRU_WIDTH = 1024
LRU_BLOCKS = 8
LRU_C = 8.0
CONV_W = 4
NA_HEADS = 4
NA_ROWS = 8
NA_COLS = 16
ATT_Q_W = ATT_Q_HEADS * HEAD_DIM
ATT_KV_W = ATT_KV_HEADS * HEAD_DIM
NA_W = NA_HEADS * HEAD_DIM
MOE_GROUPS = 4
MOE_EXPERTS_PER_GROUP = 8
MOE_EXPERTS = MOE_GROUPS * MOE_EXPERTS_PER_GROUP
MOE_HIDDEN = 1024
ROPE_THETA = 10000.0
EPS = 1e-6
LOG2E = 1.4426950408889634

COL_QA = 0
COL_KA = COL_QA + ATT_Q_W
COL_VA = COL_KA + ATT_KV_W
COL_UB = COL_VA + ATT_KV_W
COL_GB = COL_UB + LRU_WIDTH
COL_QN = COL_GB + LRU_WIDTH
COL_KN = COL_QN + NA_W
COL_VN = COL_KN + NA_W
IN_WIDTH = COL_VN + NA_W

TM = 256
MOE_BM = 256
NA_QROWS = 4
NA_WROWS = 12
NEG_BIAS = -1e30
VMEM_LIMIT = 56 * 1024 * 1024


def _cparams(sem, vmem=None):
    return pltpu.CompilerParams(dimension_semantics=sem, vmem_limit_bytes=vmem)


def _rms(x):
    return x * lax.rsqrt(jnp.mean(x * x, axis=-1, keepdims=True) + EPS)


def _adaln_kernel(c_ref, w_ref, b_ref, o_ref):
    cv = c_ref[...]
    s = cv * jax.nn.sigmoid(cv)
    o_ref[0] = jnp.dot(s.astype(BF16), w_ref[0].astype(BF16), preferred_element_type=F32) + b_ref[0]


def _adaln(cvec, ada_w, ada_b):
    L, D, N = ada_w.shape
    tn = 1024
    return pl.pallas_call(
        _adaln_kernel,
        grid=(L, N // tn),
        in_specs=[
            pl.BlockSpec((8, D), lambda l, j: (0, 0)),
            pl.BlockSpec((1, D, tn), lambda l, j: (l, 0, j)),
            pl.BlockSpec((1, 1, tn), lambda l, j: (l, 0, j)),
        ],
        out_specs=pl.BlockSpec((1, 8, tn), lambda l, j: (l, 0, j)),
        out_shape=jax.ShapeDtypeStruct((L, 8, N), F32),
        compiler_params=_cparams(("parallel", "parallel"), VMEM_LIMIT),
        name="adaln",
    )(cvec, ada_w, ada_b.reshape(L, 1, N))


def _inproj_kernel(*refs, n_first, two_sources):
    if two_sources:
        x_ref, xb_ref, mod_ref, g_ref, w_ref, o_ref = refs
        x = jnp.where(pl.program_id(0) < n_first, x_ref[...], xb_ref[...])
    else:
        x_ref, mod_ref, g_ref, w_ref, o_ref = refs
        x = x_ref[...]
    h = _rms(x) * g_ref[...]
    h = h * (1.0 + mod_ref[0, 1:2, :]) + mod_ref[0, 0:1, :]
    o_ref[...] = jnp.dot(h.astype(BF16), w_ref[...], preferred_element_type=F32)


def _inproj(x_rows, ctx_rows, mod, g1, w_in_bf, B, S):
    two = ctx_rows is not None
    D = x_rows.shape[1]
    M = x_rows.shape[0] + (ctx_rows.shape[0] if two else 0)
    N = w_in_bf.shape[1]
    tm = 512 if (S % 512 == 0 and (M - B * S) % 512 == 0) else TM
    per_seq = S // tm
    n_first = B * per_seq
    if two:
        n_ctx_blocks = ctx_rows.shape[0] // tm
        x_specs = [pl.BlockSpec((tm, D), lambda i: (jnp.minimum(i, n_first - 1), 0)),
                   pl.BlockSpec((tm, D), lambda i: (jnp.maximum(i - n_first, 0), 0),
                                pipeline_mode=pl.Buffered(1 if n_ctx_blocks == 1 else 2))]
        x_args = [x_rows, ctx_rows]
    else:
        x_specs = [pl.BlockSpec((tm, D), lambda i: (i, 0))]
        x_args = [x_rows]
    return pl.pallas_call(
        functools.partial(_inproj_kernel, n_first=n_first, two_sources=two),
        grid=(M // tm,),
        in_specs=[
            *x_specs,
            pl.BlockSpec((1, 6, D), lambda i: (jnp.minimum(i // per_seq, B), 0, 0)),
            pl.BlockSpec((1, D), lambda i: (0, 0)),
            pl.BlockSpec((D, N), lambda i: (0, 0), pipeline_mode=pl.Buffered(1)),
        ],
        out_specs=pl.BlockSpec((tm, N), lambda i: (i, 0)),
        out_shape=jax.ShapeDtypeStruct((M, N), F32),
        compiler_params=_cparams(("parallel",), VMEM_LIMIT),
        name="inproj",
    )(*x_args, mod, g1.reshape(1, D), w_in_bf)


def _norm_rope(t, g, cos, sin):
    y = _rms(t) * g
    return y * cos + pltpu.roll(y, HEAD_DIM // 2, 1) * sin


def _rope_order(a, n_heads):
    nf = HEAD_DIM // 4
    lead = a.shape[:-1]
    return a.reshape(*lead, n_heads, 2, 2, nf).swapaxes(-3, -2).reshape(*lead, n_heads * HEAD_DIM)


def _qkprep_kernel(p_ref, cos_ref, sin_ref, qg_ref, kg_ref, q_ref, k_ref, v_ref):
    cos = cos_ref[...]
    sin = sin_ref[...]
    scale = HEAD_DIM ** -0.5 * LOG2E
    for h in range(ATT_Q_HEADS):
        t = p_ref[:, COL_QA + h * HEAD_DIM:COL_QA + (h + 1) * HEAD_DIM]
        q_ref[:, h * HEAD_DIM:(h + 1) * HEAD_DIM] = (_norm_rope(t, qg_ref[...], cos, sin) * scale).astype(BF16)
    for h in range(ATT_KV_HEADS):
        t = p_ref[:, COL_KA + h * HEAD_DIM:COL_KA + (h + 1) * HEAD_DIM]
        k_ref[0, :, h * HEAD_DIM:(h + 1) * HEAD_DIM] = _norm_rope(t, kg_ref[...], cos, sin).astype(BF16)
        v_ref[0, :, 2 * h * HEAD_DIM:(2 * h + 1) * HEAD_DIM] = (
            p_ref[:, COL_VA + h * HEAD_DIM:COL_VA + (h + 1) * HEAD_DIM].astype(BF16))
        v_ref[0, :, (2 * h + 1) * HEAD_DIM:(2 * h + 2) * HEAD_DIM] = jnp.ones((p_ref.shape[0], HEAD_DIM), BF16)


def _qkprep(proj, cos_t, sin_t, qg, kg, B, S):
    M = proj.shape[0]
    nbs = S // TM
    nlat = B * nbs

    def b_of(i):
        return jnp.where(i < nlat, i // nbs, i - nlat)

    def pos_of(i):
        return jnp.where(i < nlat, i % nbs, nbs)

    kl = S + TM
    return pl.pallas_call(
        _qkprep_kernel,
        grid=(M // TM,),
        in_specs=[
            pl.BlockSpec((TM, COL_UB), lambda i: (i, 0)),
            pl.BlockSpec((TM, HEAD_DIM), lambda i: (pos_of(i), 0)),
            pl.BlockSpec((TM, HEAD_DIM), lambda i: (pos_of(i), 0)),
            pl.BlockSpec((1, HEAD_DIM), lambda i: (0, 0)),
            pl.BlockSpec((1, HEAD_DIM), lambda i: (0, 0)),
        ],
        out_specs=[
            pl.BlockSpec((TM, ATT_Q_W), lambda i: (i, 0)),
            pl.BlockSpec((1, TM, ATT_KV_W), lambda i: (b_of(i), pos_of(i), 0)),
            pl.BlockSpec((1, TM, 2 * ATT_KV_W), lambda i: (b_of(i), pos_of(i), 0)),
        ],
        out_shape=[
            jax.ShapeDtypeStruct((M, ATT_Q_W), BF16),
            jax.ShapeDtypeStruct((B, kl, ATT_KV_W), BF16),
            jax.ShapeDtypeStruct((B, kl, 2 * ATT_KV_W), BF16),
        ],
        compiler_params=_cparams(("parallel",), VMEM_LIMIT),
        name="qkprep",
    )(proj, cos_t, sin_t, qg.reshape(1, HEAD_DIM), kg.reshape(1, HEAD_DIM))


def _gqa_kernel(q_ref, k_ref, v_ref, o_in_ref, o_ref, q2_ref, sa_ref, sb_ref, m_ref, acc_ref, *, tk):
    del o_in_ref
    tq = q_ref.shape[0]
    q2_ref[0:tq, :] = q_ref[:, :HEAD_DIM]
    q2_ref[tq:, :] = q_ref[:, HEAD_DIM:]
    nk = k_ref.shape[1] // tk
    m_ref[...] = jnp.full(m_ref.shape, -jnp.inf, F32)
    acc_ref[...] = jnp.zeros(acc_ref.shape, F32)

    def scores(c, s_ref):
        off = pl.multiple_of(c * tk, tk)
        s_ref[...] = lax.dot_general(q2_ref[...], k_ref[0, pl.ds(off, tk), :], (((1,), (1,)), ((), ())),
                                     preferred_element_type=F32)

    def update(c, s_ref):
        off = pl.multiple_of(c * tk, tk)
        vc = v_ref[0, pl.ds(off, tk), :]
        for hh in range(2):
            rows = slice(hh * tq, (hh + 1) * tq)
            m = m_ref[rows, :]
            m_new = jnp.maximum(m, jnp.max(s_ref[rows, :], axis=-1, keepdims=True))
            m_ref[rows, :] = m_new
            p = jnp.exp2((s_ref[rows, :] - jnp.concatenate([m_new] * (tk // HEAD_DIM), axis=1)).astype(BF16))
            alpha = jnp.exp2(m - m_new)
            acc_ref[rows, :] = (jnp.concatenate([alpha, alpha], axis=1) * acc_ref[rows, :]
                                + jnp.dot(p, vc, preferred_element_type=F32))

    scores(0, sa_ref)
    n_pairs = (nk - 1) // 2

    def body(j, carry):
        scores(2 * j + 1, sb_ref)
        update(2 * j, sa_ref)
        scores(2 * j + 2, sa_ref)
        update(2 * j + 1, sb_ref)
        return carry

    lax.fori_loop(0, n_pairs, body, 0)
    if nk % 2 == 0:
        scores(nk - 1, sb_ref)
        update(nk - 2, sa_ref)
        update(nk - 1, sb_ref)
    else:
        update(nk - 1, sa_ref)
    acc = acc_ref[...]
    o = acc[:, :HEAD_DIM] / acc[:, HEAD_DIM:]
    o_ref[:, :HEAD_DIM] = o[:tq].astype(BF16)
    o_ref[:, HEAD_DIM:] = o[tq:].astype(BF16)


def _gqa_latent(q, k_all, v_all, B, S):
    M = q.shape[0]
    kl = k_all.shape[1]
    tq = 512 if S % 512 == 0 else TM
    nbs = S // tq
    tk = 768 if kl % 768 == 0 else TM
    g = ATT_Q_HEADS // ATT_KV_HEADS
    return pl.pallas_call(
        functools.partial(_gqa_kernel, tk=tk),
        grid=(B, ATT_KV_HEADS, nbs),
        in_specs=[
            pl.BlockSpec((tq, g * HEAD_DIM), lambda b, h, i: (b * nbs + i, h)),
            pl.BlockSpec((1, kl, HEAD_DIM), lambda b, h, i: (b, 0, h)),
            pl.BlockSpec((1, kl, 2 * HEAD_DIM), lambda b, h, i: (b, 0, h)),
            pl.BlockSpec(memory_space=pl.ANY),
        ],
        out_specs=pl.BlockSpec((tq, g * HEAD_DIM), lambda b, h, i: (b * nbs + i, h)),
        out_shape=jax.ShapeDtypeStruct((M, ATT_Q_W), BF16),
        input_output_aliases={3: 0},
        scratch_shapes=[
            pltpu.VMEM((g * tq, HEAD_DIM), BF16),
            pltpu.VMEM((g * tq, tk), F32),
            pltpu.VMEM((g * tq, tk), F32),
            pltpu.VMEM((g * tq, HEAD_DIM), F32),
            pltpu.VMEM((g * tq, 2 * HEAD_DIM), F32),
        ],
        compiler_params=_cparams(("parallel", "parallel", "parallel"), VMEM_LIMIT),
        name="gqa_latent",
    )(q, k_all, v_all, jnp.zeros((M, ATT_Q_W), BF16))


def _ctx_attn_kernel(q_ref, k_ref, v_ref, o_in_ref, o_ref, *, scale):
    del o_in_ref
    if scale is None:
        q = q_ref[...]
    else:
        q = (q_ref[...] * (scale * LOG2E)).astype(BF16)
    k = k_ref[...].astype(BF16)
    s = lax.dot_general(q, k, (((1,), (1,)), ((), ())), preferred_element_type=F32)
    p = jnp.exp2(s - jnp.max(s, axis=-1, keepdims=True))
    l = jnp.sum(p, axis=-1, keepdims=True)
    pv = jnp.dot(p.astype(BF16), v_ref[...].astype(BF16), preferred_element_type=F32)
    o_ref[...] = (pv / l).astype(BF16)


def _ctx_attn(q_arr, q_map, k_arr, k_spec, v_arr, v_spec, o_arr, B, n_heads, ctx_blk0, scale):
    return pl.pallas_call(
        functools.partial(_ctx_attn_kernel, scale=scale),
        grid=(B, n_heads),
        in_specs=[
            pl.BlockSpec((TM, HEAD_DIM), q_map),
            k_spec,
            v_spec,
            pl.BlockSpec(memory_space=pl.ANY),
        ],
        out_specs=pl.BlockSpec((TM, HEAD_DIM), lambda b, h: (ctx_blk0 + b, h)),
        out_shape=jax.ShapeDtypeStruct(o_arr.shape, o_arr.dtype),
        input_output_aliases={3: 0},
        compiler_params=_cparams(("parallel", "parallel"), VMEM_LIMIT),
        name="ctx_attn",
    )(q_arr, k_arr, v_arr, o_arr)


def _lru_kernel(u_ref, up_ref, un_ref, *rest, reverse, final):
    if final:
        (g_ref, hrev_ref, cw_ref, cb_ref, wr_ref, br_ref, wi_ref, bi_ref, sp_ref,
         o_ref, xpad, a_scr, b_scr, h_scr, hst) = rest
    else:
        (cw_ref, cb_ref, wr_ref, br_ref, wi_ref, bi_ref, sp_ref,
         o_ref, xpad, a_scr, b_scr, h_scr, hst) = rest
    s = pl.program_id(1)
    ns = pl.num_programs(1)
    T = u_ref.shape[0]
    W = u_ref.shape[1]

    @pl.when(s == 0)
    def _():
        hst[...] = jnp.zeros_like(hst)

    if reverse:
        j = ns - 1 - s
    else:
        j = s - 1
    is_first = jnp.logical_or(s == 0, j == 0)
    is_last = jnp.logical_or(s == 0, j == ns - 2)
    prev_rows = jnp.where(is_first, 0.0, up_ref[...])
    next_rows = jnp.where(is_last, 0.0, un_ref[...])
    xpad[0:8, :] = prev_rows
    xpad[8:8 + T, :] = u_ref[...]
    xpad[8 + T:16 + T, :] = next_rows
    u = (cb_ref[...]
         + xpad[7:7 + T, :] * cw_ref[0:1, :]
         + xpad[8:8 + T, :] * cw_ref[1:2, :]
         + xpad[9:9 + T, :] * cw_ref[2:3, :]
         + xpad[10:10 + T, :] * cw_ref[3:4, :])

    ub = u.astype(BF16)
    nb = LRU_WIDTH // LRU_BLOCKS
    for n in range(LRU_BLOCKS):
        sl = slice(n * nb, (n + 1) * nb)
        un_ = ub[:, sl]
        tr = jnp.tanh(jnp.dot(un_, wr_ref[n], preferred_element_type=F32) + br_ref[:, sl])
        ti = jnp.tanh(jnp.dot(un_, wi_ref[n], preferred_element_type=F32) + bi_ref[:, sl])
        ig = 0.5 * ti + 0.5
        log_a = sp_ref[:, sl] * tr + sp_ref[:, sl]
        a = jnp.exp(log_a)
        a_scr[:, sl] = a
        one_minus_a2 = -jnp.tanh(log_a) * (a * a + 1.0)
        root = one_minus_a2 * lax.rsqrt(jnp.maximum(one_minus_a2, 1e-30))
        b_scr[:, sl] = root * (ig * u[:, sl])

    def step(t, h):
        row = (T - 1 - t) if reverse else t
        h = a_scr[pl.ds(row, 1), :] * h + b_scr[pl.ds(row, 1), :]
        h_scr[pl.ds(row, 1), :] = h
        return h

    hst[...] = lax.fori_loop(0, T, step, hst[...], unroll=8)

    if final:
        y = h_scr[...] + hrev_ref[...]
        o_ref[...] = (y * jax.nn.gelu(g_ref[...])).astype(o_ref.dtype)
    else:
        o_ref[...] = h_scr[...]
    del W


def _lru_dir(proj, hrev, conv_w, conv_b, wr, br, wi, bi, sp, B, S, reverse):
    M = proj.shape[0]
    nbs = S // TM
    nlat = B * nbs
    final = hrev is not None
    ub_blk = COL_UB // LRU_WIDTH
    gb_blk = COL_GB // LRU_WIDTH
    r8 = TM // 8
    n8 = M // 8

    def blk(b, s):
        j = (nbs - s) if reverse else (s - 1)
        return jnp.where(s == 0, nlat + b, b * nbs + j)

    in_specs = [
        pl.BlockSpec((TM, LRU_WIDTH), lambda b, s: (blk(b, s), ub_blk)),
        pl.BlockSpec((8, LRU_WIDTH), lambda b, s: (jnp.maximum(blk(b, s) * r8 - 1, 0), ub_blk)),
        pl.BlockSpec((8, LRU_WIDTH), lambda b, s: (jnp.minimum((blk(b, s) + 1) * r8, n8 - 1), ub_blk)),
    ]
    args = [proj, proj, proj]
    if final:
        in_specs += [
            pl.BlockSpec((TM, LRU_WIDTH), lambda b, s: (blk(b, s), gb_blk)),
            pl.BlockSpec((TM, LRU_WIDTH), lambda b, s: (blk(b, s), 0)),
        ]
        args += [proj, hrev]
    const2 = lambda b, s: (0, 0)
    in_specs += [
        pl.BlockSpec((CONV_W, LRU_WIDTH), const2),
        pl.BlockSpec((1, LRU_WIDTH), const2),
        pl.BlockSpec((LRU_BLOCKS, LRU_WIDTH // LRU_BLOCKS, LRU_WIDTH // LRU_BLOCKS), lambda b, s: (0, 0, 0)),
        pl.BlockSpec((1, LRU_WIDTH), const2),
        pl.BlockSpec((LRU_BLOCKS, LRU_WIDTH // LRU_BLOCKS, LRU_WIDTH // LRU_BLOCKS), lambda b, s: (0, 0, 0)),
        pl.BlockSpec((1, LRU_WIDTH), const2),
        pl.BlockSpec((1, LRU_WIDTH), const2),
    ]
    args += [conv_w, conv_b.reshape(1, -1), (0.5 * wr).astype(BF16), 0.5 * br.reshape(1, -1),
             (0.5 * wi).astype(BF16), 0.5 * bi.reshape(1, -1), (-0.5 * LRU_C) * sp.reshape(1, -1)]
    return pl.pallas_call(
        functools.partial(_lru_kernel, reverse=reverse, final=final),
        grid=(B, nbs + 1),
        in_specs=in_specs,
        out_specs=pl.BlockSpec((TM, LRU_WIDTH), lambda b, s: (blk(b, s), 0)),
        out_shape=jax.ShapeDtypeStruct((M, LRU_WIDTH), BF16 if final else F32),
        scratch_shapes=[
            pltpu.VMEM((TM + 16, LRU_WIDTH), F32),
            pltpu.VMEM((TM, LRU_WIDTH), F32),
            pltpu.VMEM((TM, LRU_WIDTH), F32),
            pltpu.VMEM((TM, LRU_WIDTH), F32),
            pltpu.VMEM((1, LRU_WIDTH), F32),
        ],
        compiler_params=_cparams(("parallel", "arbitrary"), VMEM_LIMIT),
        name="lru_fwd" if final else "lru_rev",
    )(*args)


def _na_kernel(q_ref, k0_ref, k1_ref, k2_ref, v0_ref, v1_ref, v2_ref, kc_ref, vc_ref, tl_ref, tr_ref,
               o_in_ref, o_ref, s_scr0, s_scr1, *, n_rows):
    del o_in_ref
    i = pl.program_id(1)
    nbs = n_rows // NA_QROWS
    r0 = NA_QROWS * i
    ws = NA_QROWS * jnp.clip(i - 1, 0, nbs - 3)
    scale = HEAD_DIM ** -0.5 * LOG2E
    nkeys = s_scr0.shape[1]
    ones = jnp.ones((nkeys, HEAD_DIM), BF16)

    def scores(h, s_ref):
        hs = slice(h * HEAD_DIM, (h + 1) * HEAD_DIM)
        qh = (q_ref[:, hs] * scale).astype(BF16)
        kh = jnp.concatenate([k0_ref[:, hs], k1_ref[:, hs], k2_ref[:, hs], kc_ref[:, hs]], axis=0).astype(BF16)
        s_ref[...] = lax.dot_general(qh, kh, (((1,), (1,)), ((), ())), preferred_element_type=F32)

    def finish(h, s_scr):
        hs = slice(h * HEAD_DIM, (h + 1) * HEAD_DIM)
        vh = jnp.concatenate([v0_ref[:, hs], v1_ref[:, hs], v2_ref[:, hs], vc_ref[:, hs]], axis=0).astype(BF16)
        vh = jnp.concatenate([vh, ones], axis=1)
        for qr in range(NA_QROWS):
            r = r0 + qr
            rs = jnp.clip(r - NA_ROWS // 2, 0, n_rows - NA_ROWS)

            def tile_of(a):
                kr = ws + a
                valid = jnp.logical_and(kr >= rs, kr < rs + NA_ROWS)
                return jnp.where(valid, kr - r + (NA_ROWS - 1), 2 * NA_ROWS - 1)

            for pr in range(NA_WROWS // 2):
                bias = tl_ref[h, tile_of(2 * pr)] + tr_ref[h, tile_of(2 * pr + 1)]
                rsl = slice(qr * GRID_W, (qr + 1) * GRID_W)
                csl = slice(pr * 2 * GRID_W, (pr + 1) * 2 * GRID_W)
                s_scr[rsl, csl] = s_scr[rsl, csl] + bias
        s = s_scr[...]
        p = jnp.exp2((s - jnp.max(s, axis=-1, keepdims=True)).astype(BF16))
        pv = jnp.dot(p, vh, preferred_element_type=F32)
        o_ref[:, hs] = (pv[:, :HEAD_DIM] / pv[:, HEAD_DIM:]).astype(BF16)

    bufs = (s_scr0, s_scr1)
    scores(0, bufs[0])
    for h in range(NA_HEADS):
        if h + 1 < NA_HEADS:
            scores(h + 1, bufs[(h + 1) % 2])
        finish(h, bufs[h % 2])


def _na_latent(proj, tl, tr, B, S):
    M = proj.shape[0]
    nbs = S // TM
    nlat = B * nbs
    n_rows = S // GRID_W
    qb, kb, vb = COL_QN // NA_W, COL_KN // NA_W, COL_VN // NA_W

    def wblk(b, i, t):
        return b * nbs + jnp.clip(i - 1, 0, nbs - 3) + t

    blk = (TM, NA_W)
    in_specs = [pl.BlockSpec(blk, lambda b, i: (b * nbs + i, qb))]
    in_specs += [pl.BlockSpec(blk, functools.partial(lambda b, i, t: (wblk(b, i, t), kb), t=t)) for t in range(3)]
    in_specs += [pl.BlockSpec(blk, functools.partial(lambda b, i, t: (wblk(b, i, t), vb), t=t)) for t in range(3)]
    in_specs += [
        pl.BlockSpec(blk, lambda b, i: (nlat + b, kb)),
        pl.BlockSpec(blk, lambda b, i: (nlat + b, vb)),
        pl.BlockSpec(tl.shape, lambda b, i: (0, 0, 0, 0)),
        pl.BlockSpec(tr.shape, lambda b, i: (0, 0, 0, 0)),
        pl.BlockSpec(memory_space=pl.ANY),
    ]
    return pl.pallas_call(
        functools.partial(_na_kernel, n_rows=n_rows),
        grid=(B, nbs),
        in_specs=in_specs,
        out_specs=pl.BlockSpec(blk, lambda b, i: (b * nbs + i, 0)),
        out_shape=jax.ShapeDtypeStruct((M, NA_W), BF16),
        input_output_aliases={11: 0},
        scratch_shapes=[pltpu.VMEM((TM, NA_WROWS * GRID_W + TM), F32),
                        pltpu.VMEM((TM, NA_WROWS * GRID_W + TM), F32)],
        compiler_params=_cparams(("parallel", "parallel"), VMEM_LIMIT),
        name="na_latent",
    )(proj, proj, proj, proj, proj, proj, proj, proj, proj, tl, tr, jnp.zeros((M, NA_W), BF16))


def _na_bias_tables(rpb):
    j = np.arange(GRID_W)
    cs = np.clip(j - NA_COLS // 2, 0, GRID_W - NA_COLS)
    kc = np.arange(GRID_W)
    inside = (kc[None, :] >= cs[:, None]) & (kc[None, :] < cs[:, None] + NA_COLS)
    dc = kc[None, :] - j[:, None] + (NA_COLS - 1)
    sel = (dc[:, :, None] == np.arange(2 * NA_COLS - 1)[None, None, :]) & inside[:, :, None]
    t = jnp.einsum("hrd,jkd->hrjk", rpb * LOG2E, jnp.asarray(sel, F32), precision=lax.Precision.HIGHEST)
    t = jnp.where(inside[None, None], t, NEG_BIAS)
    t = jnp.concatenate([t, jnp.full_like(t[:, :1], NEG_BIAS)], axis=1)
    z = jnp.zeros_like(t)
    return jnp.concatenate([t, z], axis=-1), jnp.concatenate([z, t], axis=-1)


def _outproj_kernel(*refs, n_first, n_steps, two_sources):
    if two_sources:
        (oa_ref, ob_ref, oc_ref, x_ref, xb_ref, mod_ref, g_ref, w_ref, wr_ref, br_ref,
         xo_ref, h2_ref, rt_ref, cnt_ref, carry, lg_scr, mix_scr) = refs
    else:
        (oa_ref, ob_ref, oc_ref, x_ref, mod_ref, g_ref, w_ref, wr_ref, br_ref,
         xo_ref, h2_ref, rt_ref, cnt_ref, carry, lg_scr, mix_scr) = refs
    step = pl.program_id(0)
    i = jnp.minimum(step, n_steps - 1)

    @pl.when(step == 0)
    def _():
        carry[...] = jnp.zeros_like(carry)
        lg_scr[...] = jnp.zeros_like(lg_scr)

    lg_prev = lg_scr[...]

    k1 = ATT_Q_W
    k2 = k1 + LRU_WIDTH
    for r0 in range(0, x_ref.shape[0], TM):
        rows = slice(r0, r0 + TM)
        mix_scr[rows, 0:k1] = oa_ref[rows, :]
        mix_scr[rows, k1:k2] = ob_ref[rows, :]
        mix_scr[rows, k2:] = oc_ref[rows, :]
        acc = jnp.dot(mix_scr[rows, :], w_ref[...], preferred_element_type=F32)
        x_in = x_ref[rows, :]
        if two_sources:
            x_in = jnp.where(i < n_first, x_in, xb_ref[rows, :])
        xn = x_in + mod_ref[0, 2:3, :] * acc
        xo_ref[rows, :] = xn
        h2 = _rms(xn) * g_ref[...]
        h2 = h2 * (1.0 + mod_ref[0, 4:5, :]) + mod_ref[0, 3:4, :]
        h2_ref[rows, :] = _pack_bf16_pair(h2)
        lg_scr[rows, :] = jnp.dot(h2.astype(BF16), wr_ref[...], preferred_element_type=F32) + br_ref[...]

    route = _route_math(lg_prev)
    rt_ref[...] = _rank_math(route, carry, step > 0)
    cnt_ref[...] = jnp.broadcast_to(carry[...], cnt_ref.shape)


def _pack_bf16_pair(x):
    w = x.shape[1] // 2
    lo = pltpu.bitcast(x[:, :w].astype(BF16).astype(F32), jnp.uint32)
    hi = pltpu.bitcast(x[:, w:].astype(BF16).astype(F32), jnp.uint32)
    return hi | (lo >> 16)


def _unpack_bf16_pair(words):
    lo = pltpu.bitcast(words << 16, F32)
    hi = pltpu.bitcast(words & jnp.uint32(0xFFFF0000), F32)
    return jnp.concatenate([lo, hi], axis=1).astype(BF16)


def _outproj(oa, ob, oc, x_rows, ctx_rows, mod, g2, w_out_bf, w_router, b_router, B, S, n_rows):
    M = oa.shape[0]
    D = x_rows.shape[1]
    two = ctx_rows is not None
    tm = 512 if (S % 512 == 0 and (M - B * S) % 512 == 0) else TM
    per_seq = S // tm
    n_first = B * per_seq
    packed = jax.eval_shape(_pack_bf16_pair, jax.ShapeDtypeStruct((tm, D), F32))
    n_steps = n_rows // tm

    def blk(s):
        return jnp.minimum(s, n_steps - 1)

    if two:
        x_specs = [pl.BlockSpec((tm, D), lambda s: (jnp.minimum(blk(s), n_first - 1), 0)),
                   pl.BlockSpec((tm, D), lambda s: (jnp.maximum(blk(s) - n_first, 0), 0))]
        x_args = [x_rows, ctx_rows]
    else:
        x_specs = [pl.BlockSpec((tm, D), lambda s: (blk(s), 0))]
        x_args = [x_rows]
    return pl.pallas_call(
        functools.partial(_outproj_kernel, n_first=n_first, n_steps=n_steps, two_sources=two),
        grid=(n_steps + 1,),
        in_specs=[
            pl.BlockSpec((tm, ATT_Q_W), lambda s: (blk(s), 0)),
            pl.BlockSpec((tm, LRU_WIDTH), lambda s: (blk(s), 0)),
            pl.BlockSpec((tm, NA_W), lambda s: (blk(s), 0)),
            *x_specs,
            pl.BlockSpec((1, 6, D), lambda s: (jnp.minimum(blk(s) // per_seq, B), 0, 0)),
            pl.BlockSpec((1, D), lambda s: (0, 0)),
            pl.BlockSpec(w_out_bf.shape, lambda s: (0, 0)),
            pl.BlockSpec(w_router.shape, lambda s: (0, 0)),
            pl.BlockSpec((1, 128), lambda s: (0, 0)),
        ],
        out_specs=[
            pl.BlockSpec((tm, D), lambda s: (blk(s), 0)),
            pl.BlockSpec(packed.shape, lambda s: (blk(s), 0)),
            pl.BlockSpec((tm, 128), lambda s: (jnp.maximum(s - 1, 0), 0)),
            pl.BlockSpec((8, 128), lambda s: (0, 0)),
        ],
        out_shape=[
            jax.ShapeDtypeStruct((n_rows, D), F32),
            jax.ShapeDtypeStruct((n_rows, packed.shape[1]), packed.dtype),
            jax.ShapeDtypeStruct((n_rows, 128), F32),
            jax.ShapeDtypeStruct((8, 128), F32),
        ],
        scratch_shapes=[pltpu.VMEM((1, 128), F32), pltpu.VMEM((tm, 128), F32),
                        pltpu.VMEM((tm, w_out_bf.shape[0]), BF16)],
        compiler_params=_cparams(("arbitrary",), VMEM_LIMIT),
        name="outproj",
    )(oa, ob, oc, *x_args, mod, g2.reshape(1, D), w_out_bf, w_router, b_router)


def _route_math(x):
    lane = lax.broadcasted_iota(jnp.int32, x.shape, 1)
    big = jnp.int32(1 << 20)
    ninf = -jnp.inf

    def first_argmax(vals):
        m = jnp.max(vals, axis=-1, keepdims=True)
        idx = jnp.min(jnp.where(vals == m, lane, big), axis=-1, keepdims=True)
        return m, idx

    lg = jnp.where(lane < MOE_GROUPS, x, ninf)
    mg, g_star = first_argmax(lg)
    g_gate = 1.0 / jnp.sum(jnp.exp(lg - mg), axis=-1, keepdims=True)
    lo = MOE_GROUPS + MOE_EXPERTS_PER_GROUP * g_star
    le = jnp.where(jnp.logical_and(lane >= lo, lane < lo + MOE_EXPERTS_PER_GROUP), x, ninf)
    v1, i1 = first_argmax(le)
    v2, i2 = first_argmax(jnp.where(lane == i1, ninf, le))
    e21 = jnp.exp(v2 - v1)
    w1 = g_gate / (1.0 + e21)
    w2 = g_gate * e21 / (1.0 + e21)
    e1 = (i1 - MOE_GROUPS).astype(F32)
    e2 = (i2 - MOE_GROUPS).astype(F32)
    return jnp.where(lane == 0, e1, jnp.where(lane == 1, e2, jnp.where(lane == 2, w1,
                     jnp.where(lane == 3, w2, 0.0))))


def _rank_math(r, carry, live):
    lane = lax.broadcasted_iota(jnp.int32, r.shape, 1).astype(F32)
    oh0 = jnp.where(lane == r[:, 0:1], 1.0, 0.0)
    oh1 = jnp.where(lane == r[:, 1:2], 1.0, 0.0)
    oh = oh0 + oh1
    n = r.shape[0]
    row = lax.broadcasted_iota(jnp.int32, (n, n), 0)
    col = lax.broadcasted_iota(jnp.int32, (n, n), 1)
    tri = jnp.where(col < row, 1.0, 0.0).astype(BF16)
    tot = carry[...] + jnp.dot(tri, oh.astype(BF16), preferred_element_type=F32)
    rank0 = jnp.sum(oh0 * tot, axis=-1, keepdims=True)
    rank1 = jnp.sum(oh1 * tot, axis=-1, keepdims=True)
    lane_i = lax.broadcasted_iota(jnp.int32, r.shape, 1)
    carry[...] = carry[...] + jnp.where(live, jnp.sum(oh, axis=0, keepdims=True), 0.0)
    return jnp.where(lane_i == 4, rank0, jnp.where(lane_i == 5, rank1, r))


def _dispatch_kernel(dest_ref, pad_ref, h_hbm, xs_hbm, hbuf, zeros, fsem, rsem, zsem, *, n_tok):
    i = pl.program_id(0)

    @pl.when(i == 0)
    def _():
        zeros[...] = jnp.zeros_like(zeros)
        pieces = [1 << b for b in range(MOE_BM.bit_length() - 2, 2, -1)]

        def copies(e):
            start, n = pad_ref[0, e], pad_ref[1, e]
            head = jnp.minimum((-start) & 7, n)
            out = []
            for r in range(7):
                out.append((r < head, pltpu.make_async_copy(zeros.at[pl.ds(0, 1)],
                                                            xs_hbm.at[pl.ds(start + r, 1)], zsem)))
            rest = n - head
            for p in pieces:
                at = pl.multiple_of(start + head + (rest & ~(2 * p - 1)), 8)
                out.append(((rest & p) != 0, pltpu.make_async_copy(zeros.at[pl.ds(0, p)],
                                                                   xs_hbm.at[pl.ds(at, p)], zsem)))
            return out

        def fill(e, c):
            for on, cp in copies(e):
                @pl.when(on)
                def _():
                    cp.start()
            return c

        def drain(e, c):
            for on, cp in copies(e):
                @pl.when(on)
                def _():
                    cp.wait()
            return c

        lax.fori_loop(0, MOE_EXPERTS, fill, 0)
        lax.fori_loop(0, MOE_EXPERTS, drain, 0)

        half = zeros.shape[0]
        n_tail = (xs_hbm.shape[0] - pad_ref[2, 0]) // half

        def tail_copy(j):
            at = pl.multiple_of(pad_ref[2, 0] + j * half, half)
            return pltpu.make_async_copy(zeros, xs_hbm.at[pl.ds(at, half)], zsem)

        def tail_fill(j, c):
            tail_copy(j).start()
            return c

        def tail_drain(j, c):
            tail_copy(j).wait()
            return c

        lax.fori_loop(0, n_tail, tail_fill, 0)
        lax.fori_loop(0, n_tail, tail_drain, 0)

    n = pl.num_programs(0)
    slot = i % 3
    nslot = (i + 1) % 3

    def fetch(blk, s):
        return pltpu.make_async_copy(h_hbm.at[pl.ds(pl.multiple_of(blk * TM, TM), TM)], hbuf.at[s], fsem.at[s])

    def rows_done(s):
        for k in range(2):
            pltpu.make_async_copy(hbuf.at[s], xs_hbm.at[pl.ds(0, TM)], rsem.at[s]).wait()

    @pl.when(i == 0)
    def _():
        fetch(0, 0).start()

    @pl.when(i >= 2)
    def _():
        rows_done(nslot)

    @pl.when(i + 1 < n)
    def _():
        fetch(i + 1, nslot).start()

    fetch(i, slot).wait()

    def issue(t, c):
        tok = i * TM + t
        for k in range(2):
            pltpu.make_async_copy(hbuf.at[slot, pl.ds(t, 1)], xs_hbm.at[pl.ds(dest_ref[k * n_tok + tok], 1)],
                                  rsem.at[slot]).start()
        return c

    lax.fori_loop(0, TM, issue, 0, unroll=8)

    @pl.when(i == n - 1)
    def _():
        rows_done(slot)

    @pl.when(jnp.logical_and(i == n - 1, i >= 1))
    def _():
        rows_done((i + 2) % 3)


def _dispatch(dest_flat, pad_info, h2, n_rows, nblk):
    W = h2.shape[1]
    return pl.pallas_call(
        functools.partial(_dispatch_kernel, n_tok=nblk * TM),
        grid_spec=pltpu.PrefetchScalarGridSpec(
            num_scalar_prefetch=2,
            grid=(nblk,),
            in_specs=[pl.BlockSpec(memory_space=pl.ANY)],
            out_specs=pl.BlockSpec(memory_space=pl.ANY),
            scratch_shapes=[pltpu.VMEM((3, TM, W), h2.dtype), pltpu.VMEM((MOE_BM // 2, W), h2.dtype),
                            pltpu.SemaphoreType.DMA((3,)), pltpu.SemaphoreType.DMA((3,)),
                            pltpu.SemaphoreType.DMA(())],
        ),
        out_shape=jax.ShapeDtypeStruct((n_rows, W), h2.dtype),
        compiler_params=pltpu.CompilerParams(dimension_semantics=("arbitrary",), has_side_effects=True),
        name="dispatch",
    )(dest_flat, pad_info, h2)


PLAN_EXPERT, PLAN_FIRST, PLAN_SLOT, PLAN_NEXT, PLAN_HALF = 0, 1, 2, 3, 4


def _block_plan(block_e, n_used, pad_starts, counts):
    nb = block_e.shape[0]
    idx = jnp.arange(nb, dtype=jnp.int32)
    is_first = (idx < n_used[0]) & ((idx == 0) | (block_e != jnp.roll(block_e, 1)))
    seg = jnp.cumsum(is_first.astype(jnp.int32)) - 1
    first_pos = jnp.where(is_first, idx, nb)
    at_or_after = jnp.flip(lax.cummin(jnp.flip(first_pos)))
    next_first = jnp.concatenate([at_or_after[1:], jnp.full((1,), nb, jnp.int32)])
    next_e = jnp.where(next_first < nb, block_e[jnp.minimum(next_first, nb - 1)], -1)
    valid = counts[block_e] - (idx * MOE_BM - pad_starts[block_e])
    half_only = (valid <= MOE_BM // 2).astype(jnp.int32)
    return jnp.stack([block_e, is_first.astype(jnp.int32), seg % 2, next_e, half_only]).astype(jnp.int32)


def _expert_weights_step(i, plan_ref, w_hbms, wbuf, wbf, sem, layer):
    def copies(e, slot):
        return [pltpu.make_async_copy(w.at[layer, e], wbuf.at[slot, n], sem.at[slot, n])
                for n, w in enumerate(w_hbms)]

    @pl.when(plan_ref[PLAN_FIRST, i] == 1)
    def _():
        slot = plan_ref[PLAN_SLOT, i]

        @pl.when(i == 0)
        def _():
            for c in copies(plan_ref[PLAN_EXPERT, 0], 0):
                c.start()

        for c in copies(plan_ref[PLAN_EXPERT, i], slot):
            c.wait()
        nrow, ncol = wbuf.shape[2], wbuf.shape[3]
        rows = 256

        def cast_rows(r, c):
            off = pl.multiple_of(r * rows, rows)
            for n in range(len(w_hbms)):
                wbf[pl.ds(off, rows), n * ncol:(n + 1) * ncol] = wbuf[slot, n, pl.ds(off, rows), :].astype(BF16)
            return c

        lax.fori_loop(0, nrow // rows, cast_rows, 0)
        nxt = plan_ref[PLAN_NEXT, i]

        @pl.when(nxt >= 0)
        def _():
            for c in copies(nxt, 1 - slot):
                c.start()


def _moe1_kernel(plan_ref, nu_ref, x_ref, wg_hbm, wu_hbm, o_ref, wbuf, wcat, sem, *, layer):
    i = pl.program_id(0)

    @pl.when(i < nu_ref[0])
    def _():
        _expert_weights_step(i, plan_ref, (wg_hbm, wu_hbm), wbuf, wcat, sem, layer)

        def gated(rows):
            x = _unpack_bf16_pair(x_ref[rows, :])
            hc = MOE_HIDDEN // 2
            for c in range(2):
                g = jnp.dot(x, wcat[:, c * hc:(c + 1) * hc], preferred_element_type=F32)
                u = jnp.dot(x, wcat[:, MOE_HIDDEN + c * hc:MOE_HIDDEN + (c + 1) * hc],
                            preferred_element_type=F32)
                o_ref[rows, c * hc:(c + 1) * hc] = (g * jax.nn.sigmoid(g) * u).astype(BF16)

        half = MOE_BM // 2

        @pl.when(plan_ref[PLAN_HALF, i] == 1)
        def _():
            gated(slice(0, half))
            o_ref[half:, :] = jnp.zeros((MOE_BM - half, MOE_HIDDEN), BF16)

        @pl.when(plan_ref[PLAN_HALF, i] == 0)
        def _():
            gated(slice(None))

    @pl.when(i >= nu_ref[0])
    def _():
        o_ref[...] = jnp.zeros_like(o_ref)


def _moe1(plan, n_used, xs, w_gate, w_up, layer):
    P, W = xs.shape
    D = w_gate.shape[2]
    nb = P // MOE_BM
    used_blk = lambda i, plan, nu: (jnp.minimum(i, nu[0] - 1), 0)
    return pl.pallas_call(
        functools.partial(_moe1_kernel, layer=layer),
        grid_spec=pltpu.PrefetchScalarGridSpec(
            num_scalar_prefetch=2,
            grid=(nb,),
            in_specs=[pl.BlockSpec((MOE_BM, W), used_blk), pl.BlockSpec(memory_space=pl.ANY),
                      pl.BlockSpec(memory_space=pl.ANY)],
            out_specs=pl.BlockSpec((MOE_BM, MOE_HIDDEN), lambda i, plan, nu: (i, 0)),
            scratch_shapes=[pltpu.VMEM((2, 2, D, MOE_HIDDEN), F32), pltpu.VMEM((D, 2 * MOE_HIDDEN), BF16),
                            pltpu.SemaphoreType.DMA((2, 2))],
        ),
        out_shape=jax.ShapeDtypeStruct((P, MOE_HIDDEN), BF16),
        compiler_params=_cparams(("arbitrary",), VMEM_LIMIT),
        name="moe_up",
    )(plan, n_used, xs, w_gate, w_up)


def _moe2_kernel(plan_ref, nu_ref, h_ref, wd_hbm, o_ref, wbuf, wbf, sem, *, layer):
    i = pl.program_id(0)

    @pl.when(i < nu_ref[0])
    def _():
        _expert_weights_step(i, plan_ref, (wd_hbm,), wbuf, wbf, sem, layer)
        half = MOE_BM // 2

        @pl.when(plan_ref[PLAN_HALF, i] == 1)
        def _():
            o_ref[0:half, :] = jnp.dot(h_ref[0:half, :], wbf[...], preferred_element_type=F32)
            o_ref[half:, :] = jnp.zeros((MOE_BM - half, o_ref.shape[1]), F32)

        @pl.when(plan_ref[PLAN_HALF, i] == 0)
        def _():
            o_ref[...] = jnp.dot(h_ref[...], wbf[...], preferred_element_type=F32)

    @pl.when(i >= nu_ref[0])
    def _():
        o_ref[...] = jnp.zeros_like(o_ref)


def _moe2(plan, n_used, hmid, w_down, layer):
    P = hmid.shape[0]
    D = w_down.shape[3]
    nb = P // MOE_BM
    return pl.pallas_call(
        functools.partial(_moe2_kernel, layer=layer),
        grid_spec=pltpu.PrefetchScalarGridSpec(
            num_scalar_prefetch=2,
            grid=(nb,),
            in_specs=[
                pl.BlockSpec((MOE_BM, MOE_HIDDEN), lambda i, plan, nu: (i, 0)),
                pl.BlockSpec(memory_space=pl.ANY),
            ],
            out_specs=pl.BlockSpec((MOE_BM, D), lambda i, plan, nu: (i, 0)),
            scratch_shapes=[pltpu.VMEM((2, 1, MOE_HIDDEN, D), F32), pltpu.VMEM((MOE_HIDDEN, D), BF16),
                            pltpu.SemaphoreType.DMA((2, 1))],
        ),
        out_shape=jax.ShapeDtypeStruct((P, D), F32),
        compiler_params=_cparams(("arbitrary",), VMEM_LIMIT),
        name="moe_down",
    )(plan, n_used, hmid, w_down)


def _combine_kernel(dest_ref, yb_hbm, x_ref, r_ref, mod_ref, fg_ref, o_ref, ybuf, sem, *, final):
    i = pl.program_id(0)
    n = pl.num_programs(0)

    n_tok = n * TM

    def row_copy(tok, k, t, slot):
        return pltpu.make_async_copy(yb_hbm.at[pl.ds(dest_ref[k * n_tok + tok], 1)],
                                     ybuf.at[slot, k, pl.ds(t, 1)], sem.at[slot])

    def issue_rows(blk, slot, t0):
        for r in range(8):
            for k in range(2):
                row_copy(blk * TM + t0 + r, k, t0 + r, slot).start()

    def wait_slot(slot):
        for k in range(2):
            pltpu.make_async_copy(yb_hbm.at[pl.ds(0, TM)], ybuf.at[slot, k], sem.at[slot]).wait()

    @pl.when(i == 0)
    def _():
        def first(j, c):
            issue_rows(0, 0, pl.multiple_of(j * 8, 8))
            return c
        lax.fori_loop(0, TM // 8, first, 0)

    def step(cur):
        @pl.when(i + 1 < n)
        def _():
            def body(j, c):
                issue_rows(i + 1, 1 - cur, pl.multiple_of(j * 8, 8))
                return c
            lax.fori_loop(0, TM // 8, body, 0)

        wait_slot(cur)
        r = r_ref[...]
        y = r[:, 2:3] * ybuf[cur, 0] + r[:, 3:4] * ybuf[cur, 1]
        xn = x_ref[...] + mod_ref[0, 5:6, :] * y
        if final:
            xn = _rms(xn) * fg_ref[...]
        o_ref[...] = xn

    @pl.when(i % 2 == 0)
    def _():
        step(0)

    @pl.when(i % 2 == 1)
    def _():
        step(1)


def _combine(dest_flat, yb, x_all, route, mod, final_g, seg_of, nblk, final):
    M, D = x_all.shape
    out_rows = nblk * TM if final else M
    kwargs = {} if final else {"input_output_aliases": {2: 0}}
    return pl.pallas_call(
        functools.partial(_combine_kernel, final=final),
        grid_spec=pltpu.PrefetchScalarGridSpec(
            num_scalar_prefetch=1,
            grid=(nblk,),
            in_specs=[
                pl.BlockSpec(memory_space=pl.ANY),
                pl.BlockSpec((TM, D), lambda i, d: (i, 0)),
                pl.BlockSpec((TM, 128), lambda i, d: (i, 0)),
                pl.BlockSpec((1, 6, D), lambda i, d: (seg_of(i), 0, 0)),
                pl.BlockSpec((1, D), lambda i, d: (0, 0)),
            ],
            out_specs=pl.BlockSpec((TM, D), lambda i, d: (i, 0)),
            scratch_shapes=[pltpu.VMEM((2, 2, TM, D), F32), pltpu.SemaphoreType.DMA((2,))],
        ),
        out_shape=jax.ShapeDtypeStruct((out_rows, D), F32),
        compiler_params=_cparams(("arbitrary",), VMEM_LIMIT),
        name="combine",
        **kwargs,
    )(dest_flat, yb, x_all, route, mod, final_g.reshape(1, D))


def _moe_layer(x_all, h2, route, cnt, mod, final_g, w_gate, w_up, w_down, layer, seg_of, nblk, final):
    T = nblk * TM
    e_ids = route[:, 0:2].T.astype(jnp.int32)
    ranks = route[:, 4:6].T.astype(jnp.int32)
    counts = cnt[0, :MOE_EXPERTS].astype(jnp.int32)
    padded = (counts + MOE_BM - 1) // MOE_BM * MOE_BM
    pad_ends = jnp.cumsum(padded)
    pad_starts = pad_ends - padded
    expert = jnp.arange(MOE_EXPERTS, dtype=jnp.int32)[:, None, None]
    start_of = jnp.sum(jnp.where(e_ids[None] == expert, pad_starts[:, None, None], 0), axis=0)
    dest = (start_of + ranks).reshape(-1)
    pad_info = jnp.stack([pad_starts + counts, padded - counts,
                          jnp.broadcast_to(pad_ends[-1], (MOE_EXPERTS,))]).astype(jnp.int32)
    n_blocks = -(-(2 * T + MOE_EXPERTS * (MOE_BM - 1)) // MOE_BM)
    block_starts = jnp.arange(n_blocks, dtype=jnp.int32) * MOE_BM
    block_e = jnp.sum((block_starts[:, None] >= pad_ends[None, :]).astype(jnp.int32), axis=1)
    block_e = jnp.minimum(block_e, MOE_EXPERTS - 1)
    n_used = (pad_ends[-1:] // MOE_BM).astype(jnp.int32)
    plan = _block_plan(block_e, n_used, pad_starts, counts)
    xs = _dispatch(dest, pad_info, h2, n_blocks * MOE_BM, nblk)
    hmid = _moe1(plan, n_used, xs, w_gate, w_up, layer)
    yb = _moe2(plan, n_used, hmid, w_down, layer)
    return _combine(dest, yb, x_all, route, mod, final_g, seg_of, nblk, final)


def _rope_tables(S):
    pos = jnp.arange(S, dtype=jnp.int32)
    rows = (pos // GRID_W).astype(F32)
    cols = (pos % GRID_W).astype(F32)
    n_freq = HEAD_DIM // 4
    inv = 1.0 / (ROPE_THETA ** (jnp.arange(n_freq, dtype=F32) / n_freq))
    cr, sr = jnp.cos(rows[:, None] * inv), jnp.sin(rows[:, None] * inv)
    cc, sc = jnp.cos(cols[:, None] * inv), jnp.sin(cols[:, None] * inv)
    cos_t = jnp.concatenate([cr, cc, cr, cc], axis=1)
    sin_t = jnp.concatenate([-sr, -sc, sr, sc], axis=1)
    cos_t = jnp.concatenate([cos_t, jnp.ones((TM, HEAD_DIM), F32)], axis=0)
    sin_t = jnp.concatenate([sin_t, jnp.zeros((TM, HEAD_DIM), F32)], axis=0)
    return cos_t, sin_t


def kernel(x, c, ctx, c_ctx, ada_w, ada_b, norm1_g, norm2_g, w_in, w_out, att_q_norm, att_k_norm, conv_w, conv_b, lru_wr, lru_br, lru_wi, lru_bi, lru_lambda, na_rpb, router_wg, router_bg, router_we, router_be, moe_w_gate, moe_w_up, moe_w_down, final_g):
    B, S, D = x.shape
    C = ctx.shape[1]
    L = ada_w.shape[0]
    assert C == TM and S % TM == 0 and S // TM >= 3 and B + 1 <= 8
    nbs = S // TM
    nlat = B * nbs
    m_lat = B * S
    M = m_lat + B * C

    def seg_of(i):
        return jnp.minimum(i // nbs, B)

    x_rows, ctx_rows = x.reshape(m_lat, D), ctx.reshape(B * C, D)
    cvec = jnp.zeros((8, D), F32).at[:B].set(c).at[B].set(c_ctx)
    mod_all = _adaln(cvec, ada_w, ada_b)[:, :B + 1].reshape(L, B + 1, 6, D)
    cos_t, sin_t = _rope_tables(S)
    g_att = ATT_Q_HEADS // ATT_KV_HEADS

    out = None
    for l in range(L):
        last = l == L - 1
        mod = mod_all[l]
        n_qk = ATT_Q_HEADS + ATT_KV_HEADS
        w_in_l = jnp.concatenate([_rope_order(w_in[l][:, :COL_VA], n_qk), w_in[l][:, COL_VA:]], axis=1)
        proj = _inproj(x_rows, ctx_rows, mod, norm1_g[l], w_in_l.astype(BF16), B, S)

        q, k_all, v_all = _qkprep(proj, cos_t, sin_t, _rope_order(att_q_norm[l], 1),
                                  _rope_order(att_k_norm[l], 1), B, S)
        oa = _gqa_latent(q, k_all, v_all, B, S)

        sp = jax.nn.softplus(-lru_lambda[l].astype(F32))
        hrev = _lru_dir(proj, None, conv_w[l], conv_b[l], lru_wr[l, 1], lru_br[l, 1], lru_wi[l, 1],
                        lru_bi[l, 1], sp[1], B, S, reverse=True)
        ob = _lru_dir(proj, hrev, conv_w[l], conv_b[l], lru_wr[l, 0], lru_br[l, 0], lru_wi[l, 0],
                      lru_bi[l, 0], sp[0], B, S, reverse=False)

        tl, tr = _na_bias_tables(na_rpb[l])
        oc = _na_latent(proj, tl, tr, B, S)

        if not last:
            oa = _ctx_attn(
                q, lambda b, h: (nlat + b, h),
                k_all, pl.BlockSpec((None, TM, HEAD_DIM), lambda b, h: (b, nbs, h // g_att)),
                v_all, pl.BlockSpec((None, TM, HEAD_DIM), lambda b, h: (b, nbs, 2 * (h // g_att))),
                oa, B, ATT_Q_HEADS, nlat, None)
            oc = _ctx_attn(
                proj, lambda b, h: (nlat + b, COL_QN // HEAD_DIM + h),
                proj, pl.BlockSpec((TM, HEAD_DIM), lambda b, h: (nlat + b, COL_KN // HEAD_DIM + h)),
                proj, pl.BlockSpec((TM, HEAD_DIM), lambda b, h: (nlat + b, COL_VN // HEAD_DIM + h)),
                oc, B, NA_HEADS, nlat, HEAD_DIM ** -0.5)

        nblk = nlat if last else M // TM
        w_router = jnp.zeros((D, 128), F32).at[:, :MOE_GROUPS].set(router_wg[l])
        w_router = w_router.at[:, MOE_GROUPS:MOE_GROUPS + MOE_EXPERTS].set(router_we[l])
        b_router = jnp.zeros((1, 128), F32).at[0, :MOE_GROUPS].set(router_bg[l])
        b_router = b_router.at[0, MOE_GROUPS:MOE_GROUPS + MOE_EXPERTS].set(router_be[l])
        x_all, h2, route, cnt = _outproj(oa, ob, oc, x_rows, ctx_rows, mod, norm2_g[l], w_out[l].astype(BF16),
                                         w_router.astype(BF16), b_router, B, S, nblk * TM)
        res = _moe_layer(x_all, h2, route, cnt, mod, final_g, moe_w_gate, moe_w_up, moe_w_down, l,
                         seg_of, nblk, last)
        if last:
            out = res
        else:
            x_rows, ctx_rows = res, None
    return out.reshape(B, S, D)
```

```python
import functools

import numpy as np
import jax
import jax.numpy as jnp
from jax import lax
from jax.experimental import pallas as pl
from jax.experimental.pallas import tpu as pltpu

F32 = jnp.float32
BF16 = jnp.bfloat16

GRID_W = 64
HEAD_DIM = 128
ATT_Q_HEADS = 4
ATT_KV_HEADS = 2
LRU_WIDTH = 1024
LRU_BLOCKS = 8
LRU_C = 8.0
CONV_W = 4
NA_HEADS = 4
NA_ROWS = 8
NA_COLS = 16
ATT_Q_W = ATT_Q_HEADS * HEAD_DIM
ATT_KV_W = ATT_KV_HEADS * HEAD_DIM
NA_W = NA_HEADS * HEAD_DIM
MOE_GROUPS = 4
MOE_EXPERTS_PER_GROUP = 8
MOE_EXPERTS = MOE_GROUPS * MOE_EXPERTS_PER_GROUP
MOE_HIDDEN = 1024
ROPE_THETA = 10000.0
EPS = 1e-6
LOG2E = 1.4426950408889634

COL_QA = 0
COL_KA = COL_QA + ATT_Q_W
COL_VA = COL_KA + ATT_KV_W
COL_UB = COL_VA + ATT_KV_W
COL_GB = COL_UB + LRU_WIDTH
COL_QN = COL_GB + LRU_WIDTH
COL_KN = COL_QN + NA_W
COL_VN = COL_KN + NA_W
IN_WIDTH = COL_VN + NA_W

TM = 256
MOE_BM = 256
NA_QROWS = 4
NA_WROWS = 12
NEG_BIAS = -1e30
VMEM_LIMIT = 56 * 1024 * 1024


def _cparams(sem, vmem=None):
    return pltpu.CompilerParams(dimension_semantics=sem, vmem_limit_bytes=vmem)


def _rms(x):
    return x * lax.rsqrt(jnp.mean(x * x, axis=-1, keepdims=True) + EPS)


def _adaln_kernel(c_ref, w_ref, b_ref, o_ref):
    cv = c_ref[...]
    s = cv * jax.nn.sigmoid(cv)
    o_ref[0] = jnp.dot(s.astype(BF16), w_ref[0].astype(BF16), preferred_element_type=F32) + b_ref[0]


def _adaln(cvec, ada_w, ada_b):
    L, D, N = ada_w.shape
    tn = 1024
    return pl.pallas_call(
        _adaln_kernel,
        grid=(L, N // tn),
        in_specs=[
            pl.BlockSpec((8, D), lambda l, j: (0, 0)),
            pl.BlockSpec((1, D, tn), lambda l, j: (l, 0, j)),
            pl.BlockSpec((1, 1, tn), lambda l, j: (l, 0, j)),
        ],
        out_specs=pl.BlockSpec((1, 8, tn), lambda l, j: (l, 0, j)),
        out_shape=jax.ShapeDtypeStruct((L, 8, N), F32),
        compiler_params=_cparams(("parallel", "parallel"), VMEM_LIMIT),
        name="adaln",
    )(cvec, ada_w, ada_b.reshape(L, 1, N))


def _inproj_kernel(*refs, n_first, two_sources):
    if two_sources:
        x_ref, xb_ref, mod_ref, g_ref, w_ref, o_ref = refs
        x = jnp.where(pl.program_id(0) < n_first, x_ref[...], xb_ref[...])
    else:
        x_ref, mod_ref, g_ref, w_ref, o_ref = refs
        x = x_ref[...]
    h = _rms(x) * g_ref[...]
    h = h * (1.0 + mod_ref[0, 1:2, :]) + mod_ref[0, 0:1, :]
    o_ref[...] = jnp.dot(h.astype(BF16), w_ref[...], preferred_element_type=F32)


def _inproj(x_rows, ctx_rows, mod, g1, w_in_bf, B, S):
    two = ctx_rows is not None
    D = x_rows.shape[1]
    M = x_rows.shape[0] + (ctx_rows.shape[0] if two else 0)
    N = w_in_bf.shape[1]
    tm = 512 if (S % 512 == 0 and (M - B * S) % 512 == 0) else TM
    per_seq = S // tm
    n_first = B * per_seq
    if two:
        n_ctx_blocks = ctx_rows.shape[0] // tm
        x_specs = [pl.BlockSpec((tm, D), lambda i: (jnp.minimum(i, n_first - 1), 0)),
                   pl.BlockSpec((tm, D), lambda i: (jnp.maximum(i - n_first, 0), 0),
                                pipeline_mode=pl.Buffered(1 if n_ctx_blocks == 1 else 2))]
        x_args = [x_rows, ctx_rows]
    else:
        x_specs = [pl.BlockSpec((tm, D), lambda i: (i, 0))]
        x_args = [x_rows]
    return pl.pallas_call(
        functools.partial(_inproj_kernel, n_first=n_first, two_sources=two),
        grid=(M // tm,),
        in_specs=[
            *x_specs,
            pl.BlockSpec((1, 6, D), lambda i: (jnp.minimum(i // per_seq, B), 0, 0)),
            pl.BlockSpec((1, D), lambda i: (0, 0)),
            pl.BlockSpec((D, N), lambda i: (0, 0), pipeline_mode=pl.Buffered(1)),
        ],
        out_specs=pl.BlockSpec((tm, N), lambda i: (i, 0)),
        out_shape=jax.ShapeDtypeStruct((M, N), F32),
        compiler_params=_cparams(("parallel",), VMEM_LIMIT),
        name="inproj",
    )(*x_args, mod, g1.reshape(1, D), w_in_bf)


def _norm_rope(t, g, cos, sin):
    y = _rms(t) * g
    lane = lax.broadcasted_iota(jnp.int32, y.shape, 1)
    first_half = (lane % 64) < 32
    partner = jnp.where(first_half, pltpu.roll(y, 96, 1), pltpu.roll(y, 32, 1))
    return y * cos + partner * sin


def _qkprep_kernel(p_ref, cos_ref, sin_ref, qg_ref, kg_ref, q_ref, k_ref, v_ref):
    cos = cos_ref[...]
    sin = sin_ref[...]
    scale = HEAD_DIM ** -0.5 * LOG2E
    for h in range(ATT_Q_HEADS):
        t = p_ref[:, COL_QA + h * HEAD_DIM:COL_QA + (h + 1) * HEAD_DIM]
        q_ref[:, h * HEAD_DIM:(h + 1) * HEAD_DIM] = (_norm_rope(t, qg_ref[...], cos, sin) * scale).astype(BF16)
    for h in range(ATT_KV_HEADS):
        t = p_ref[:, COL_KA + h * HEAD_DIM:COL_KA + (h + 1) * HEAD_DIM]
        k_ref[0, :, h * HEAD_DIM:(h + 1) * HEAD_DIM] = _norm_rope(t, kg_ref[...], cos, sin).astype(BF16)
        v_ref[0, :, 2 * h * HEAD_DIM:(2 * h + 1) * HEAD_DIM] = (
            p_ref[:, COL_VA + h * HEAD_DIM:COL_VA + (h + 1) * HEAD_DIM].astype(BF16))
        v_ref[0, :, (2 * h + 1) * HEAD_DIM:(2 * h + 2) * HEAD_DIM] = jnp.ones((p_ref.shape[0], HEAD_DIM), BF16)


def _qkprep(proj, cos_t, sin_t, qg, kg, B, S):
    M = proj.shape[0]
    nbs = S // TM
    nlat = B * nbs

    def b_of(i):
        return jnp.where(i < nlat, i // nbs, i - nlat)

    def pos_of(i):
        return jnp.where(i < nlat, i % nbs, nbs)

    kl = S + TM
    return pl.pallas_call(
        _qkprep_kernel,
        grid=(M // TM,),
        in_specs=[
            pl.BlockSpec((TM, COL_UB), lambda i: (i, 0)),
            pl.BlockSpec((TM, HEAD_DIM), lambda i: (pos_of(i), 0)),
            pl.BlockSpec((TM, HEAD_DIM), lambda i: (pos_of(i), 0)),
            pl.BlockSpec((1, HEAD_DIM), lambda i: (0, 0)),
            pl.BlockSpec((1, HEAD_DIM), lambda i: (0, 0)),
        ],
        out_specs=[
            pl.BlockSpec((TM, ATT_Q_W), lambda i: (i, 0)),
            pl.BlockSpec((1, TM, ATT_KV_W), lambda i: (b_of(i), pos_of(i), 0)),
            pl.BlockSpec((1, TM, 2 * ATT_KV_W), lambda i: (b_of(i), pos_of(i), 0)),
        ],
        out_shape=[
            jax.ShapeDtypeStruct((M, ATT_Q_W), BF16),
            jax.ShapeDtypeStruct((B, kl, ATT_KV_W), BF16),
            jax.ShapeDtypeStruct((B, kl, 2 * ATT_KV_W), BF16),
        ],
        compiler_params=_cparams(("parallel",), VMEM_LIMIT),
        name="qkprep",
    )(proj, cos_t, sin_t, qg.reshape(1, HEAD_DIM), kg.reshape(1, HEAD_DIM))


def _gqa_kernel(q_ref, k_ref, v_ref, o_in_ref, o_ref, q2_ref, sa_ref, sb_ref, m_ref, acc_ref, *, tk):
    del o_in_ref
    tq = q_ref.shape[0]
    q2_ref[0:tq, :] = q_ref[:, :HEAD_DIM]
    q2_ref[tq:, :] = q_ref[:, HEAD_DIM:]
    nk = k_ref.shape[1] // tk
    m_ref[...] = jnp.full(m_ref.shape, -jnp.inf, F32)
    acc_ref[...] = jnp.zeros(acc_ref.shape, F32)

    def scores(c, s_ref):
        off = pl.multiple_of(c * tk, tk)
        s_ref[...] = lax.dot_general(q2_ref[...], k_ref[0, pl.ds(off, tk), :], (((1,), (1,)), ((), ())),
                                     preferred_element_type=F32)

    def update(c, s_ref):
        off = pl.multiple_of(c * tk, tk)
        vc = v_ref[0, pl.ds(off, tk), :]
        for hh in range(2):
            rows = slice(hh * tq, (hh + 1) * tq)
            m = m_ref[rows, :]
            m_new = jnp.maximum(m, jnp.max(s_ref[rows, :], axis=-1, keepdims=True))
            m_ref[rows, :] = m_new
            p = jnp.exp2((s_ref[rows, :] - jnp.concatenate([m_new] * (tk // HEAD_DIM), axis=1)).astype(BF16))
            alpha = jnp.exp2(m - m_new)
            acc_ref[rows, :] = (jnp.concatenate([alpha, alpha], axis=1) * acc_ref[rows, :]
                                + jnp.dot(p, vc, preferred_element_type=F32))

    scores(0, sa_ref)
    n_pairs = (nk - 1) // 2

    def body(j, carry):
        scores(2 * j + 1, sb_ref)
        update(2 * j, sa_ref)
        scores(2 * j + 2, sa_ref)
        update(2 * j + 1, sb_ref)
        return carry

    lax.fori_loop(0, n_pairs, body, 0)
    if nk % 2 == 0:
        scores(nk - 1, sb_ref)
        update(nk - 2, sa_ref)
        update(nk - 1, sb_ref)
    else:
        update(nk - 1, sa_ref)
    acc = acc_ref[...]
    o = acc[:, :HEAD_DIM] / acc[:, HEAD_DIM:]
    o_ref[:, :HEAD_DIM] = o[:tq].astype(BF16)
    o_ref[:, HEAD_DIM:] = o[tq:].astype(BF16)


def _gqa_latent(q, k_all, v_all, B, S):
    M = q.shape[0]
    kl = k_all.shape[1]
    tq = 512 if S % 512 == 0 else TM
    nbs = S // tq
    tk = 768 if kl % 768 == 0 else TM
    g = ATT_Q_HEADS // ATT_KV_HEADS
    return pl.pallas_call(
        functools.partial(_gqa_kernel, tk=tk),
        grid=(B, ATT_KV_HEADS, nbs),
        in_specs=[
            pl.BlockSpec((tq, g * HEAD_DIM), lambda b, h, i: (b * nbs + i, h)),
            pl.BlockSpec((1, kl, HEAD_DIM), lambda b, h, i: (b, 0, h)),
            pl.BlockSpec((1, kl, 2 * HEAD_DIM), lambda b, h, i: (b, 0, h)),
            pl.BlockSpec(memory_space=pl.ANY),
        ],
        out_specs=pl.BlockSpec((tq, g * HEAD_DIM), lambda b, h, i: (b * nbs + i, h)),
        out_shape=jax.ShapeDtypeStruct((M, ATT_Q_W), BF16),
        input_output_aliases={3: 0},
        scratch_shapes=[
            pltpu.VMEM((g * tq, HEAD_DIM), BF16),
            pltpu.VMEM((g * tq, tk), F32),
            pltpu.VMEM((g * tq, tk), F32),
            pltpu.VMEM((g * tq, HEAD_DIM), F32),
            pltpu.VMEM((g * tq, 2 * HEAD_DIM), F32),
        ],
        compiler_params=_cparams(("parallel", "parallel", "parallel"), VMEM_LIMIT),
        name="gqa_latent",
    )(q, k_all, v_all, jnp.zeros((M, ATT_Q_W), BF16))


def _ctx_attn_kernel(q_ref, k_ref, v_ref, o_in_ref, o_ref, *, scale):
    del o_in_ref
    if scale is None:
        q = q_ref[...]
    else:
        q = (q_ref[...] * (scale * LOG2E)).astype(BF16)
    k = k_ref[...].astype(BF16)
    s = lax.dot_general(q, k, (((1,), (1,)), ((), ())), preferred_element_type=F32)
    p = jnp.exp2(s - jnp.max(s, axis=-1, keepdims=True))
    l = jnp.sum(p, axis=-1, keepdims=True)
    pv = jnp.dot(p.astype(BF16), v_ref[...].astype(BF16), preferred_element_type=F32)
    o_ref[...] = (pv / l).astype(BF16)


def _ctx_attn(q_arr, q_map, k_arr, k_spec, v_arr, v_spec, o_arr, B, n_heads, ctx_blk0, scale):
    return pl.pallas_call(
        functools.partial(_ctx_attn_kernel, scale=scale),
        grid=(B, n_heads),
        in_specs=[
            pl.BlockSpec((TM, HEAD_DIM), q_map),
            k_spec,
            v_spec,
            pl.BlockSpec(memory_space=pl.ANY),
        ],
        out_specs=pl.BlockSpec((TM, HEAD_DIM), lambda b, h: (ctx_blk0 + b, h)),
        out_shape=jax.ShapeDtypeStruct(o_arr.shape, o_arr.dtype),
        input_output_aliases={3: 0},
        compiler_params=_cparams(("parallel", "parallel"), VMEM_LIMIT),
        name="ctx_attn",
    )(q_arr, k_arr, v_arr, o_arr)


def _lru_kernel(u_ref, up_ref, un_ref, *rest, reverse, final):
    if final:
        (g_ref, hrev_ref, cw_ref, cb_ref, wr_ref, br_ref, wi_ref, bi_ref, sp_ref,
         o_ref, xpad, a_scr, b_scr, h_scr, hst) = rest
    else:
        (cw_ref, cb_ref, wr_ref, br_ref, wi_ref, bi_ref, sp_ref,
         o_ref, xpad, a_scr, b_scr, h_scr, hst) = rest
    s = pl.program_id(1)
    ns = pl.num_programs(1)
    T = u_ref.shape[0]
    W = u_ref.shape[1]

    @pl.when(s == 0)
    def _():
        hst[...] = jnp.zeros_like(hst)

    if reverse:
        j = ns - 1 - s
    else:
        j = s - 1
    is_first = jnp.logical_or(s == 0, j == 0)
    is_last = jnp.logical_or(s == 0, j == ns - 2)
    prev_rows = jnp.where(is_first, 0.0, up_ref[...])
    next_rows = jnp.where(is_last, 0.0, un_ref[...])
    xpad[0:8, :] = prev_rows
    xpad[8:8 + T, :] = u_ref[...]
    xpad[8 + T:16 + T, :] = next_rows
    u = (cb_ref[...]
         + xpad[7:7 + T, :] * cw_ref[0:1, :]
         + xpad[8:8 + T, :] * cw_ref[1:2, :]
         + xpad[9:9 + T, :] * cw_ref[2:3, :]
         + xpad[10:10 + T, :] * cw_ref[3:4, :])

    ub = u.astype(BF16)
    nb = LRU_WIDTH // LRU_BLOCKS
    for n in range(LRU_BLOCKS):
        sl = slice(n * nb, (n + 1) * nb)
        un_ = ub[:, sl]
        tr = jnp.tanh(jnp.dot(un_, wr_ref[n], preferred_element_type=F32) + br_ref[:, sl])
        ti = jnp.tanh(jnp.dot(un_, wi_ref[n], preferred_element_type=F32) + bi_ref[:, sl])
        ig = 0.5 * ti + 0.5
        log_a = sp_ref[:, sl] * tr + sp_ref[:, sl]
        a = jnp.exp(log_a)
        a_scr[:, sl] = a
        one_minus_a2 = -jnp.tanh(log_a) * (a * a + 1.0)
        root = one_minus_a2 * lax.rsqrt(jnp.maximum(one_minus_a2, 1e-30))
        b_scr[:, sl] = root * (ig * u[:, sl])

    def step(t, h):
        row = (T - 1 - t) if reverse else t
        h = a_scr[pl.ds(row, 1), :] * h + b_scr[pl.ds(row, 1), :]
        h_scr[pl.ds(row, 1), :] = h
        return h

    hst[...] = lax.fori_loop(0, T, step, hst[...], unroll=8)

    if final:
        y = h_scr[...] + hrev_ref[...]
        o_ref[...] = (y * jax.nn.gelu(g_ref[...])).astype(o_ref.dtype)
    else:
        o_ref[...] = h_scr[...]
    del W


def _lru_dir(proj, hrev, conv_w, conv_b, wr, br, wi, bi, sp, B, S, reverse):
    M = proj.shape[0]
    nbs = S // TM
    nlat = B * nbs
    final = hrev is not None
    ub_blk = COL_UB // LRU_WIDTH
    gb_blk = COL_GB // LRU_WIDTH
    r8 = TM // 8
    n8 = M // 8

    def blk(b, s):
        j = (nbs - s) if reverse else (s - 1)
        return jnp.where(s == 0, nlat + b, b * nbs + j)

    in_specs = [
        pl.BlockSpec((TM, LRU_WIDTH), lambda b, s: (blk(b, s), ub_blk)),
        pl.BlockSpec((8, LRU_WIDTH), lambda b, s: (jnp.maximum(blk(b, s) * r8 - 1, 0), ub_blk)),
        pl.BlockSpec((8, LRU_WIDTH), lambda b, s: (jnp.minimum((blk(b, s) + 1) * r8, n8 - 1), ub_blk)),
    ]
    args = [proj, proj, proj]
    if final:
        in_specs += [
            pl.BlockSpec((TM, LRU_WIDTH), lambda b, s: (blk(b, s), gb_blk)),
            pl.BlockSpec((TM, LRU_WIDTH), lambda b, s: (blk(b, s), 0)),
        ]
        args += [proj, hrev]
    const2 = lambda b, s: (0, 0)
    in_specs += [
        pl.BlockSpec((CONV_W, LRU_WIDTH), const2),
        pl.BlockSpec((1, LRU_WIDTH), const2),
        pl.BlockSpec((LRU_BLOCKS, LRU_WIDTH // LRU_BLOCKS, LRU_WIDTH // LRU_BLOCKS), lambda b, s: (0, 0, 0)),
        pl.BlockSpec((1, LRU_WIDTH), const2),
        pl.BlockSpec((LRU_BLOCKS, LRU_WIDTH // LRU_BLOCKS, LRU_WIDTH // LRU_BLOCKS), lambda b, s: (0, 0, 0)),
        pl.BlockSpec((1, LRU_WIDTH), const2),
        pl.BlockSpec((1, LRU_WIDTH), const2),
    ]
    args += [conv_w, conv_b.reshape(1, -1), (0.5 * wr).astype(BF16), 0.5 * br.reshape(1, -1),
             (0.5 * wi).astype(BF16), 0.5 * bi.reshape(1, -1), (-0.5 * LRU_C) * sp.reshape(1, -1)]
    return pl.pallas_call(
        functools.partial(_lru_kernel, reverse=reverse, final=final),
        grid=(B, nbs + 1),
        in_specs=in_specs,
        out_specs=pl.BlockSpec((TM, LRU_WIDTH), lambda b, s: (blk(b, s), 0)),
        out_shape=jax.ShapeDtypeStruct((M, LRU_WIDTH), BF16 if final else F32),
        scratch_shapes=[
            pltpu.VMEM((TM + 16, LRU_WIDTH), F32),
            pltpu.VMEM((TM, LRU_WIDTH), F32),
            pltpu.VMEM((TM, LRU_WIDTH), F32),
            pltpu.VMEM((TM, LRU_WIDTH), F32),
            pltpu.VMEM((1, LRU_WIDTH), F32),
        ],
        compiler_params=_cparams(("parallel", "arbitrary"), VMEM_LIMIT),
        name="lru_fwd" if final else "lru_rev",
    )(*args)


def _na_kernel(q_ref, k0_ref, k1_ref, k2_ref, v0_ref, v1_ref, v2_ref, kc_ref, vc_ref, tl_ref,
               o_in_ref, o_ref, s_scr0, s_scr1, *, n_rows):
    del o_in_ref
    i = pl.program_id(1)
    nbs = n_rows // NA_QROWS
    r0 = NA_QROWS * i
    ws = NA_QROWS * jnp.clip(i - 1, 0, nbs - 3)
    scale = HEAD_DIM ** -0.5 * LOG2E
    nkeys = s_scr0.shape[1]
    ones = jnp.ones((nkeys, HEAD_DIM), BF16)
    left_half = lax.broadcasted_iota(jnp.int32, (GRID_W, 2 * GRID_W), 1) < GRID_W

    def scores(h, s_ref):
        hs = slice(h * HEAD_DIM, (h + 1) * HEAD_DIM)
        qh = (q_ref[:, hs] * scale).astype(BF16)
        kh = jnp.concatenate([k0_ref[:, hs], k1_ref[:, hs], k2_ref[:, hs], kc_ref[:, hs]], axis=0).astype(BF16)
        s_ref[...] = lax.dot_general(qh, kh, (((1,), (1,)), ((), ())), preferred_element_type=F32)

    def finish(h, s_scr):
        hs = slice(h * HEAD_DIM, (h + 1) * HEAD_DIM)
        vh = jnp.concatenate([v0_ref[:, hs], v1_ref[:, hs], v2_ref[:, hs], vc_ref[:, hs]], axis=0).astype(BF16)
        vh = jnp.concatenate([vh, ones], axis=1)
        for qr in range(NA_QROWS):
            r = r0 + qr
            rs = jnp.clip(r - NA_ROWS // 2, 0, n_rows - NA_ROWS)

            def tile_of(a):
                kr = ws + a
                valid = jnp.logical_and(kr >= rs, kr < rs + NA_ROWS)
                return jnp.where(valid, kr - r + (NA_ROWS - 1), 2 * NA_ROWS - 1)

            for pr in range(NA_WROWS // 2):
                bias = jnp.where(left_half, tl_ref[h, tile_of(2 * pr)], tl_ref[h, tile_of(2 * pr + 1)])
                rsl = slice(qr * GRID_W, (qr + 1) * GRID_W)
                csl = slice(pr * 2 * GRID_W, (pr + 1) * 2 * GRID_W)
                s_scr[rsl, csl] = s_scr[rsl, csl] + bias
        s = s_scr[...]
        p = jnp.exp2((s - jnp.max(s, axis=-1, keepdims=True)).astype(BF16))
        pv = jnp.dot(p, vh, preferred_element_type=F32)
        o_ref[:, hs] = (pv[:, :HEAD_DIM] / pv[:, HEAD_DIM:]).astype(BF16)

    bufs = (s_scr0, s_scr1)
    scores(0, bufs[0])
    for h in range(NA_HEADS):
        if h + 1 < NA_HEADS:
            scores(h + 1, bufs[(h + 1) % 2])
        finish(h, bufs[h % 2])


def _na_latent(proj, tl, B, S):
    M = proj.shape[0]
    nbs = S // TM
    nlat = B * nbs
    n_rows = S // GRID_W
    qb, kb, vb = COL_QN // NA_W, COL_KN // NA_W, COL_VN // NA_W

    def wblk(b, i, t):
        return b * nbs + jnp.clip(i - 1, 0, nbs - 3) + t

    blk = (TM, NA_W)
    in_specs = [pl.BlockSpec(blk, lambda b, i: (b * nbs + i, qb))]
    in_specs += [pl.BlockSpec(blk, functools.partial(lambda b, i, t: (wblk(b, i, t), kb), t=t)) for t in range(3)]
    in_specs += [pl.BlockSpec(blk, functools.partial(lambda b, i, t: (wblk(b, i, t), vb), t=t)) for t in range(3)]
    in_specs += [
        pl.BlockSpec(blk, lambda b, i: (nlat + b, kb)),
        pl.BlockSpec(blk, lambda b, i: (nlat + b, vb)),
        pl.BlockSpec(tl.shape, lambda b, i: (0, 0, 0, 0)),
        pl.BlockSpec(memory_space=pl.ANY),
    ]
    return pl.pallas_call(
        functools.partial(_na_kernel, n_rows=n_rows),
        grid=(B, nbs),
        in_specs=in_specs,
        out_specs=pl.BlockSpec(blk, lambda b, i: (b * nbs + i, 0)),
        out_shape=jax.ShapeDtypeStruct((M, NA_W), BF16),
        input_output_aliases={10: 0},
        scratch_shapes=[pltpu.VMEM((TM, NA_WROWS * GRID_W + TM), F32),
                        pltpu.VMEM((TM, NA_WROWS * GRID_W + TM), F32)],
        compiler_params=_cparams(("parallel", "parallel"), VMEM_LIMIT),
        name="na_latent",
    )(proj, proj, proj, proj, proj, proj, proj, proj, proj, tl, jnp.zeros((M, NA_W), BF16))


def _na_bias_tables(rpb):
    j = np.arange(GRID_W)
    cs = np.clip(j - NA_COLS // 2, 0, GRID_W - NA_COLS)
    kc = np.arange(GRID_W)
    inside = (kc[None, :] >= cs[:, None]) & (kc[None, :] < cs[:, None] + NA_COLS)
    dc = kc[None, :] - j[:, None] + (NA_COLS - 1)
    sel = (dc[:, :, None] == np.arange(2 * NA_COLS - 1)[None, None, :]) & inside[:, :, None]
    t = jnp.einsum("lhrd,jkd->lhrjk", rpb * LOG2E, jnp.asarray(sel, F32), precision=lax.Precision.HIGHEST)
    t = jnp.where(inside, t, NEG_BIAS)
    t = jnp.concatenate([t, jnp.full_like(t[:, :, :1], NEG_BIAS)], axis=2)
    return jnp.concatenate([t, t], axis=-1)


def _outproj_kernel(*refs, n_first, n_steps, two_sources):
    if two_sources:
        (oa_ref, ob_ref, oc_ref, x_ref, xb_ref, mod_ref, g_ref, w_ref, wr_ref, br_ref,
         xo_ref, h2_ref, rt_ref, cnt_ref, carry, lg_scr, mix_scr) = refs
    else:
        (oa_ref, ob_ref, oc_ref, x_ref, mod_ref, g_ref, w_ref, wr_ref, br_ref,
         xo_ref, h2_ref, rt_ref, cnt_ref, carry, lg_scr, mix_scr) = refs
    step = pl.program_id(0)
    i = jnp.minimum(step, n_steps - 1)

    @pl.when(step == 0)
    def _():
        carry[...] = jnp.zeros_like(carry)
        lg_scr[...] = jnp.zeros_like(lg_scr)

    lg_prev = lg_scr[...]

    k1 = ATT_Q_W
    k2 = k1 + LRU_WIDTH
    for r0 in range(0, x_ref.shape[0], TM):
        rows = slice(r0, r0 + TM)
        mix_scr[rows, 0:k1] = oa_ref[rows, :]
        mix_scr[rows, k1:k2] = ob_ref[rows, :]
        mix_scr[rows, k2:] = oc_ref[rows, :]
        acc = jnp.dot(mix_scr[rows, :], w_ref[...], preferred_element_type=F32)
        x_in = x_ref[rows, :]
        if two_sources:
            x_in = jnp.where(i < n_first, x_in, xb_ref[rows, :])
        xn = x_in + mod_ref[0, 2:3, :] * acc
        xo_ref[rows, :] = xn
        h2 = _rms(xn) * g_ref[...]
        h2 = h2 * (1.0 + mod_ref[0, 4:5, :]) + mod_ref[0, 3:4, :]
        h2_ref[rows, :] = _pack_bf16_pair(h2)
        lg_scr[rows, :] = jnp.dot(h2.astype(BF16), wr_ref[...], preferred_element_type=F32) + br_ref[...]

    route = _route_math(lg_prev)
    rt_ref[...] = _rank_math(route, carry, step > 0)
    cnt_ref[...] = jnp.broadcast_to(carry[...], cnt_ref.shape)


def _pack_bf16_pair(x):
    w = x.shape[1] // 2
    lo = pltpu.bitcast(x[:, :w].astype(BF16).astype(F32), jnp.uint32)
    hi = pltpu.bitcast(x[:, w:].astype(BF16).astype(F32), jnp.uint32)
    return hi | (lo >> 16)


def _unpack_bf16_pair(words):
    lo = pltpu.bitcast(words << 16, F32)
    hi = pltpu.bitcast(words & jnp.uint32(0xFFFF0000), F32)
    return jnp.concatenate([lo, hi], axis=1).astype(BF16)


def _outproj(oa, ob, oc, x_rows, ctx_rows, mod, g2, w_out_bf, w_router, b_router, B, S, n_rows):
    M = oa.shape[0]
    D = x_rows.shape[1]
    two = ctx_rows is not None
    tm = 512 if (S % 512 == 0 and (M - B * S) % 512 == 0) else TM
    per_seq = S // tm
    n_first = B * per_seq
    packed = jax.eval_shape(_pack_bf16_pair, jax.ShapeDtypeStruct((tm, D), F32))
    n_steps = n_rows // tm

    def blk(s):
        return jnp.minimum(s, n_steps - 1)

    if two:
        x_specs = [pl.BlockSpec((tm, D), lambda s: (jnp.minimum(blk(s), n_first - 1), 0)),
                   pl.BlockSpec((tm, D), lambda s: (jnp.maximum(blk(s) - n_first, 0), 0))]
        x_args = [x_rows, ctx_rows]
    else:
        x_specs = [pl.BlockSpec((tm, D), lambda s: (blk(s), 0))]
        x_args = [x_rows]
    return pl.pallas_call(
        functools.partial(_outproj_kernel, n_first=n_first, n_steps=n_steps, two_sources=two),
        grid=(n_steps + 1,),
        in_specs=[
            pl.BlockSpec((tm, ATT_Q_W), lambda s: (blk(s), 0)),
            pl.BlockSpec((tm, LRU_WIDTH), lambda s: (blk(s), 0)),
            pl.BlockSpec((tm, NA_W), lambda s: (blk(s), 0)),
            *x_specs,
            pl.BlockSpec((1, 6, D), lambda s: (jnp.minimum(blk(s) // per_seq, B), 0, 0)),
            pl.BlockSpec((1, D), lambda s: (0, 0)),
            pl.BlockSpec(w_out_bf.shape, lambda s: (0, 0)),
            pl.BlockSpec(w_router.shape, lambda s: (0, 0)),
            pl.BlockSpec((1, 128), lambda s: (0, 0)),
        ],
        out_specs=[
            pl.BlockSpec((tm, D), lambda s: (blk(s), 0)),
            pl.BlockSpec(packed.shape, lambda s: (blk(s), 0)),
            pl.BlockSpec((tm, 128), lambda s: (jnp.maximum(s - 1, 0), 0)),
            pl.BlockSpec((8, 128), lambda s: (0, 0)),
        ],
        out_shape=[
            jax.ShapeDtypeStruct((n_rows, D), F32),
            jax.ShapeDtypeStruct((n_rows, packed.shape[1]), packed.dtype),
            jax.ShapeDtypeStruct((n_rows, 128), F32),
            jax.ShapeDtypeStruct((8, 128), F32),
        ],
        scratch_shapes=[pltpu.VMEM((1, 128), F32), pltpu.VMEM((tm, 128), F32),
                        pltpu.VMEM((tm, w_out_bf.shape[0]), BF16)],
        compiler_params=_cparams(("arbitrary",), VMEM_LIMIT),
        name="outproj",
    )(oa, ob, oc, *x_args, mod, g2.reshape(1, D), w_out_bf, w_router, b_router)


def _route_math(x):
    lane = lax.broadcasted_iota(jnp.int32, x.shape, 1)
    big = jnp.int32(1 << 20)
    ninf = -jnp.inf

    def first_argmax(vals):
        m = jnp.max(vals, axis=-1, keepdims=True)
        idx = jnp.min(jnp.where(vals == m, lane, big), axis=-1, keepdims=True)
        return m, idx

    lg = jnp.where(lane < MOE_GROUPS, x, ninf)
    mg, g_star = first_argmax(lg)
    g_gate = 1.0 / jnp.sum(jnp.exp(lg - mg), axis=-1, keepdims=True)
    lo = MOE_GROUPS + MOE_EXPERTS_PER_GROUP * g_star
    le = jnp.where(jnp.logical_and(lane >= lo, lane < lo + MOE_EXPERTS_PER_GROUP), x, ninf)
    v1, i1 = first_argmax(le)
    v2, i2 = first_argmax(jnp.where(lane == i1, ninf, le))
    e21 = jnp.exp(v2 - v1)
    w1 = g_gate / (1.0 + e21)
    w2 = g_gate * e21 / (1.0 + e21)
    e1 = (i1 - MOE_GROUPS).astype(F32)
    e2 = (i2 - MOE_GROUPS).astype(F32)
    return jnp.where(lane == 0, e1, jnp.where(lane == 1, e2, jnp.where(lane == 2, w1,
                     jnp.where(lane == 3, w2, 0.0))))


def _rank_math(r, carry, live):
    lane = lax.broadcasted_iota(jnp.int32, r.shape, 1).astype(F32)
    oh0 = jnp.where(lane == r[:, 0:1], 1.0, 0.0)
    oh1 = jnp.where(lane == r[:, 1:2], 1.0, 0.0)
    oh = oh0 + oh1
    n = r.shape[0]
    row = lax.broadcasted_iota(jnp.int32, (n, n), 0)
    col = lax.broadcasted_iota(jnp.int32, (n, n), 1)
    tri = jnp.where(col < row, 1.0, 0.0).astype(BF16)
    tot = carry[...] + jnp.dot(tri, oh.astype(BF16), preferred_element_type=F32)
    rank0 = jnp.sum(oh0 * tot, axis=-1, keepdims=True)
    rank1 = jnp.sum(oh1 * tot, axis=-1, keepdims=True)
    lane_i = lax.broadcasted_iota(jnp.int32, r.shape, 1)
    carry[...] = carry[...] + jnp.where(live, jnp.sum(oh, axis=0, keepdims=True), 0.0)
    return jnp.where(lane_i == 4, rank0, jnp.where(lane_i == 5, rank1, r))


def _dispatch_kernel(dest_ref, pad_ref, h_hbm, xs_hbm, hbuf, zeros, fsem, rsem, zsem, *, n_tok):
    i = pl.program_id(0)

    @pl.when(i == 0)
    def _():
        zeros[...] = jnp.zeros_like(zeros)
        pieces = [1 << b for b in range(MOE_BM.bit_length() - 2, 2, -1)]

        def copies(e):
            start, n = pad_ref[0, e], pad_ref[1, e]
            head = jnp.minimum((-start) & 7, n)
            out = []
            for r in range(7):
                out.append((r < head, pltpu.make_async_copy(zeros.at[pl.ds(0, 1)],
                                                            xs_hbm.at[pl.ds(start + r, 1)], zsem)))
            rest = n - head
            for p in pieces:
                at = pl.multiple_of(start + head + (rest & ~(2 * p - 1)), 8)
                out.append(((rest & p) != 0, pltpu.make_async_copy(zeros.at[pl.ds(0, p)],
                                                                   xs_hbm.at[pl.ds(at, p)], zsem)))
            return out

        def fill(e, c):
            for on, cp in copies(e):
                @pl.when(on)
                def _():
                    cp.start()
            return c

        def drain(e, c):
            for on, cp in copies(e):
                @pl.when(on)
                def _():
                    cp.wait()
            return c

        lax.fori_loop(0, MOE_EXPERTS, fill, 0)
        lax.fori_loop(0, MOE_EXPERTS, drain, 0)

        half = zeros.shape[0]
        n_tail = (xs_hbm.shape[0] - pad_ref[2, 0]) // half

        def tail_copy(j):
            at = pl.multiple_of(pad_ref[2, 0] + j * half, half)
            return pltpu.make_async_copy(zeros, xs_hbm.at[pl.ds(at, half)], zsem)

        def tail_fill(j, c):
            tail_copy(j).start()
            return c

        def tail_drain(j, c):
            tail_copy(j).wait()
            return c

        lax.fori_loop(0, n_tail, tail_fill, 0)
        lax.fori_loop(0, n_tail, tail_drain, 0)

    n = pl.num_programs(0)
    slot = i % 3
    nslot = (i + 1) % 3

    def fetch(blk, s):
        return pltpu.make_async_copy(h_hbm.at[pl.ds(pl.multiple_of(blk * TM, TM), TM)], hbuf.at[s], fsem.at[s])

    def rows_done(s):
        for k in range(2):
            pltpu.make_async_copy(hbuf.at[s], xs_hbm.at[pl.ds(0, TM)], rsem.at[s]).wait()

    @pl.when(i == 0)
    def _():
        fetch(0, 0).start()

    @pl.when(i >= 2)
    def _():
        rows_done(nslot)

    @pl.when(i + 1 < n)
    def _():
        fetch(i + 1, nslot).start()

    fetch(i, slot).wait()

    def issue(t, c):
        tok = i * TM + t
        for k in range(2):
            pltpu.make_async_copy(hbuf.at[slot, pl.ds(t, 1)], xs_hbm.at[pl.ds(dest_ref[k * n_tok + tok], 1)],
                                  rsem.at[slot]).start()
        return c

    lax.fori_loop(0, TM, issue, 0, unroll=8)

    @pl.when(i == n - 1)
    def _():
        rows_done(slot)

    @pl.when(jnp.logical_and(i == n - 1, i >= 1))
    def _():
        rows_done((i + 2) % 3)


def _dispatch(dest_flat, pad_info, h2, n_rows, nblk):
    W = h2.shape[1]
    return pl.pallas_call(
        functools.partial(_dispatch_kernel, n_tok=nblk * TM),
        grid_spec=pltpu.PrefetchScalarGridSpec(
            num_scalar_prefetch=2,
            grid=(nblk,),
            in_specs=[pl.BlockSpec(memory_space=pl.ANY)],
            out_specs=pl.BlockSpec(memory_space=pl.ANY),
            scratch_shapes=[pltpu.VMEM((3, TM, W), h2.dtype), pltpu.VMEM((MOE_BM // 2, W), h2.dtype),
                            pltpu.SemaphoreType.DMA((3,)), pltpu.SemaphoreType.DMA((3,)),
                            pltpu.SemaphoreType.DMA(())],
        ),
        out_shape=jax.ShapeDtypeStruct((n_rows, W), h2.dtype),
        compiler_params=pltpu.CompilerParams(dimension_semantics=("arbitrary",), has_side_effects=True),
        name="dispatch",
    )(dest_flat, pad_info, h2)


PLAN_EXPERT, PLAN_FIRST, PLAN_SLOT, PLAN_NEXT, PLAN_HALF = 0, 1, 2, 3, 4


def _block_plan(block_e, n_used, pad_starts, counts):
    nb = block_e.shape[0]
    idx = jnp.arange(nb, dtype=jnp.int32)
    is_first = (idx < n_used[0]) & ((idx == 0) | (block_e != jnp.roll(block_e, 1)))
    seg = jnp.cumsum(is_first.astype(jnp.int32)) - 1
    first_pos = jnp.where(is_first, idx, nb)
    at_or_after = jnp.flip(lax.cummin(jnp.flip(first_pos)))
    next_first = jnp.concatenate([at_or_after[1:], jnp.full((1,), nb, jnp.int32)])
    next_e = jnp.where(next_first < nb, block_e[jnp.minimum(next_first, nb - 1)], -1)
    valid = counts[block_e] - (idx * MOE_BM - pad_starts[block_e])
    half_only = (valid <= MOE_BM // 2).astype(jnp.int32)
    return jnp.stack([block_e, is_first.astype(jnp.int32), seg % 2, next_e, half_only]).astype(jnp.int32)


def _expert_weights_step(i, plan_ref, w_hbms, wbuf, wbf, sem, layer):
    def copies(e, slot):
        return [pltpu.make_async_copy(w.at[layer, e], wbuf.at[slot, n], sem.at[slot, n])
                for n, w in enumerate(w_hbms)]

    @pl.when(plan_ref[PLAN_FIRST, i] == 1)
    def _():
        slot = plan_ref[PLAN_SLOT, i]

        @pl.when(i == 0)
        def _():
            for c in copies(plan_ref[PLAN_EXPERT, 0], 0):
                c.start()

        for c in copies(plan_ref[PLAN_EXPERT, i], slot):
            c.wait()
        nrow, ncol = wbuf.shape[2], wbuf.shape[3]
        rows = 256

        def cast_rows(r, c):
            off = pl.multiple_of(r * rows, rows)
            for n in range(len(w_hbms)):
                wbf[pl.ds(off, rows), n * ncol:(n + 1) * ncol] = wbuf[slot, n, pl.ds(off, rows), :].astype(BF16)
            return c

        lax.fori_loop(0, nrow // rows, cast_rows, 0)
        nxt = plan_ref[PLAN_NEXT, i]

        @pl.when(nxt >= 0)
        def _():
            for c in copies(nxt, 1 - slot):
                c.start()


def _moe1_kernel(plan_ref, nu_ref, x_ref, wg_hbm, wu_hbm, o_ref, wbuf, wcat, sem, *, layer):
    i = pl.program_id(0)

    @pl.when(i < nu_ref[0])
    def _():
        _expert_weights_step(i, plan_ref, (wg_hbm, wu_hbm), wbuf, wcat, sem, layer)

        def gated(rows):
            x = _unpack_bf16_pair(x_ref[rows, :])
            hc = MOE_HIDDEN // 2
            for c in range(2):
                g = jnp.dot(x, wcat[:, c * hc:(c + 1) * hc], preferred_element_type=F32)
                u = jnp.dot(x, wcat[:, MOE_HIDDEN + c * hc:MOE_HIDDEN + (c + 1) * hc],
                            preferred_element_type=F32)
                o_ref[rows, c * hc:(c + 1) * hc] = (g * jax.nn.sigmoid(g) * u).astype(BF16)

        half = MOE_BM // 2

        @pl.when(plan_ref[PLAN_HALF, i] == 1)
        def _():
            gated(slice(0, half))
            o_ref[half:, :] = jnp.zeros((MOE_BM - half, MOE_HIDDEN), BF16)

        @pl.when(plan_ref[PLAN_HALF, i] == 0)
        def _():
            gated(slice(None))

    @pl.when(i >= nu_ref[0])
    def _():
        o_ref[...] = jnp.zeros_like(o_ref)


def _moe1(plan, n_used, xs, w_gate, w_up, layer):
    P, W = xs.shape
    D = w_gate.shape[2]
    nb = P // MOE_BM
    used_blk = lambda i, plan, nu: (jnp.minimum(i, nu[0] - 1), 0)
    return pl.pallas_call(
        functools.partial(_moe1_kernel, layer=layer),
        grid_spec=pltpu.PrefetchScalarGridSpec(
            num_scalar_prefetch=2,
            grid=(nb,),
            in_specs=[pl.BlockSpec((MOE_BM, W), used_blk), pl.BlockSpec(memory_space=pl.ANY),
                      pl.BlockSpec(memory_space=pl.ANY)],
            out_specs=pl.BlockSpec((MOE_BM, MOE_HIDDEN), lambda i, plan, nu: (i, 0)),
            scratch_shapes=[pltpu.VMEM((2, 2, D, MOE_HIDDEN), F32), pltpu.VMEM((D, 2 * MOE_HIDDEN), BF16),
                            pltpu.SemaphoreType.DMA((2, 2))],
        ),
        out_shape=jax.ShapeDtypeStruct((P, MOE_HIDDEN), BF16),
        compiler_params=_cparams(("arbitrary",), VMEM_LIMIT),
        name="moe_up",
    )(plan, n_used, xs, w_gate, w_up)


def _moe2_kernel(plan_ref, nu_ref, h_ref, wd_hbm, o_ref, wbuf, wbf, sem, *, layer):
    i = pl.program_id(0)

    @pl.when(i < nu_ref[0])
    def _():
        _expert_weights_step(i, plan_ref, (wd_hbm,), wbuf, wbf, sem, layer)
        half = MOE_BM // 2

        @pl.when(plan_ref[PLAN_HALF, i] == 1)
        def _():
            o_ref[0:half, :] = jnp.dot(h_ref[0:half, :], wbf[...], preferred_element_type=F32)
            o_ref[half:, :] = jnp.zeros((MOE_BM - half, o_ref.shape[1]), F32)

        @pl.when(plan_ref[PLAN_HALF, i] == 0)
        def _():
            o_ref[...] = jnp.dot(h_ref[...], wbf[...], preferred_element_type=F32)

    @pl.when(i >= nu_ref[0])
    def _():
        o_ref[...] = jnp.zeros_like(o_ref)


def _moe2(plan, n_used, hmid, w_down, layer):
    P = hmid.shape[0]
    D = w_down.shape[3]
    nb = P // MOE_BM
    return pl.pallas_call(
        functools.partial(_moe2_kernel, layer=layer),
        grid_spec=pltpu.PrefetchScalarGridSpec(
            num_scalar_prefetch=2,
            grid=(nb,),
            in_specs=[
                pl.BlockSpec((MOE_BM, MOE_HIDDEN), lambda i, plan, nu: (i, 0)),
                pl.BlockSpec(memory_space=pl.ANY),
            ],
            out_specs=pl.BlockSpec((MOE_BM, D), lambda i, plan, nu: (i, 0)),
            scratch_shapes=[pltpu.VMEM((2, 1, MOE_HIDDEN, D), F32), pltpu.VMEM((MOE_HIDDEN, D), BF16),
                            pltpu.SemaphoreType.DMA((2, 1))],
        ),
        out_shape=jax.ShapeDtypeStruct((P, D), F32),
        compiler_params=_cparams(("arbitrary",), VMEM_LIMIT),
        name="moe_down",
    )(plan, n_used, hmid, w_down)


def _combine_kernel(dest_ref, yb_hbm, x_ref, r_ref, mod_ref, fg_ref, o_ref, ybuf, sem, *, final):
    i = pl.program_id(0)
    n = pl.num_programs(0)

    n_tok = n * TM

    def row_copy(tok, k, t, slot):
        return pltpu.make_async_copy(yb_hbm.at[pl.ds(dest_ref[k * n_tok + tok], 1)],
                                     ybuf.at[slot, k, pl.ds(t, 1)], sem.at[slot])

    def issue_rows(blk, slot, t0):
        for r in range(8):
            for k in range(2):
                row_copy(blk * TM + t0 + r, k, t0 + r, slot).start()

    def wait_slot(slot):
        for k in range(2):
            pltpu.make_async_copy(yb_hbm.at[pl.ds(0, TM)], ybuf.at[slot, k], sem.at[slot]).wait()

    @pl.when(i == 0)
    def _():
        def first(j, c):
            issue_rows(0, 0, pl.multiple_of(j * 8, 8))
            return c
        lax.fori_loop(0, TM // 8, first, 0)

    def step(cur):
        @pl.when(i + 1 < n)
        def _():
            def body(j, c):
                issue_rows(i + 1, 1 - cur, pl.multiple_of(j * 8, 8))
                return c
            lax.fori_loop(0, TM // 8, body, 0)

        wait_slot(cur)
        r = r_ref[...]
        y = r[:, 2:3] * ybuf[cur, 0] + r[:, 3:4] * ybuf[cur, 1]
        xn = x_ref[...] + mod_ref[0, 5:6, :] * y
        if final:
            xn = _rms(xn) * fg_ref[...]
        o_ref[...] = xn

    @pl.when(i % 2 == 0)
    def _():
        step(0)

    @pl.when(i % 2 == 1)
    def _():
        step(1)


def _combine(dest_flat, yb, x_all, route, mod, final_g, seg_of, nblk, final):
    M, D = x_all.shape
    out_rows = nblk * TM if final else M
    kwargs = {} if final else {"input_output_aliases": {2: 0}}
    return pl.pallas_call(
        functools.partial(_combine_kernel, final=final),
        grid_spec=pltpu.PrefetchScalarGridSpec(
            num_scalar_prefetch=1,
            grid=(nblk,),
            in_specs=[
                pl.BlockSpec(memory_space=pl.ANY),
                pl.BlockSpec((TM, D), lambda i, d: (i, 0)),
                pl.BlockSpec((TM, 128), lambda i, d: (i, 0)),
                pl.BlockSpec((1, 6, D), lambda i, d: (seg_of(i), 0, 0)),
                pl.BlockSpec((1, D), lambda i, d: (0, 0)),
            ],
            out_specs=pl.BlockSpec((TM, D), lambda i, d: (i, 0)),
            scratch_shapes=[pltpu.VMEM((2, 2, TM, D), F32), pltpu.SemaphoreType.DMA((2,))],
        ),
        out_shape=jax.ShapeDtypeStruct((out_rows, D), F32),
        compiler_params=_cparams(("arbitrary",), VMEM_LIMIT),
        name="combine",
        **kwargs,
    )(dest_flat, yb, x_all, route, mod, final_g.reshape(1, D))


def _moe_layer(x_all, h2, route, cnt, mod, final_g, w_gate, w_up, w_down, layer, seg_of, nblk, final):
    T = nblk * TM
    e_ids = route[:, 0:2].T.astype(jnp.int32)
    ranks = route[:, 4:6].T.astype(jnp.int32)
    counts = cnt[0, :MOE_EXPERTS].astype(jnp.int32)
    padded = (counts + MOE_BM - 1) // MOE_BM * MOE_BM
    pad_ends = jnp.cumsum(padded)
    pad_starts = pad_ends - padded
    expert = jnp.arange(MOE_EXPERTS, dtype=jnp.int32)[:, None, None]
    start_of = jnp.sum(jnp.where(e_ids[None] == expert, pad_starts[:, None, None], 0), axis=0)
    dest = (start_of + ranks).reshape(-1)
    pad_info = jnp.stack([pad_starts + counts, padded - counts,
                          jnp.broadcast_to(pad_ends[-1], (MOE_EXPERTS,))]).astype(jnp.int32)
    n_blocks = -(-(2 * T + MOE_EXPERTS * (MOE_BM - 1)) // MOE_BM)
    block_starts = jnp.arange(n_blocks, dtype=jnp.int32) * MOE_BM
    block_e = jnp.sum((block_starts[:, None] >= pad_ends[None, :]).astype(jnp.int32), axis=1)
    block_e = jnp.minimum(block_e, MOE_EXPERTS - 1)
    n_used = (pad_ends[-1:] // MOE_BM).astype(jnp.int32)
    plan = _block_plan(block_e, n_used, pad_starts, counts)
    xs = _dispatch(dest, pad_info, h2, n_blocks * MOE_BM, nblk)
    hmid = _moe1(plan, n_used, xs, w_gate, w_up, layer)
    yb = _moe2(plan, n_used, hmid, w_down, layer)
    return _combine(dest, yb, x_all, route, mod, final_g, seg_of, nblk, final)


def _rope_tables(S):
    pos = jnp.arange(S, dtype=jnp.int32)
    rows = (pos // GRID_W).astype(F32)
    cols = (pos % GRID_W).astype(F32)
    n_freq = HEAD_DIM // 4
    inv = 1.0 / (ROPE_THETA ** (jnp.arange(n_freq, dtype=F32) / n_freq))
    cr, sr = jnp.cos(rows[:, None] * inv), jnp.sin(rows[:, None] * inv)
    cc, sc = jnp.cos(cols[:, None] * inv), jnp.sin(cols[:, None] * inv)
    cos_t = jnp.concatenate([cr, cr, cc, cc], axis=1)
    sin_t = jnp.concatenate([-sr, sr, -sc, sc], axis=1)
    cos_t = jnp.concatenate([cos_t, jnp.ones((TM, HEAD_DIM), F32)], axis=0)
    sin_t = jnp.concatenate([sin_t, jnp.zeros((TM, HEAD_DIM), F32)], axis=0)
    return cos_t, sin_t


def kernel(x, c, ctx, c_ctx, ada_w, ada_b, norm1_g, norm2_g, w_in, w_out, att_q_norm, att_k_norm, conv_w, conv_b, lru_wr, lru_br, lru_wi, lru_bi, lru_lambda, na_rpb, router_wg, router_bg, router_we, router_be, moe_w_gate, moe_w_up, moe_w_down, final_g):
    B, S, D = x.shape
    C = ctx.shape[1]
    L = ada_w.shape[0]
    assert C == TM and S % TM == 0 and S // TM >= 3 and B + 1 <= 8
    nbs = S // TM
    nlat = B * nbs
    m_lat = B * S
    M = m_lat + B * C

    def seg_of(i):
        return jnp.minimum(i // nbs, B)

    x_rows, ctx_rows = x.reshape(m_lat, D), ctx.reshape(B * C, D)
    cvec = jnp.zeros((8, D), F32).at[:B].set(c).at[B].set(c_ctx)
    mod_all = _adaln(cvec, ada_w, ada_b)[:, :B + 1].reshape(L, B + 1, 6, D)
    cos_t, sin_t = _rope_tables(S)
    na_tables = _na_bias_tables(na_rpb)
    g_att = ATT_Q_HEADS // ATT_KV_HEADS

    out = None
    for l in range(L):
        last = l == L - 1
        mod = mod_all[l]
        proj = _inproj(x_rows, ctx_rows, mod, norm1_g[l], w_in[l].astype(BF16), B, S)

        q, k_all, v_all = _qkprep(proj, cos_t, sin_t, att_q_norm[l], att_k_norm[l], B, S)
        oa = _gqa_latent(q, k_all, v_all, B, S)

        sp = jax.nn.softplus(-lru_lambda[l].astype(F32))
        hrev = _lru_dir(proj, None, conv_w[l], conv_b[l], lru_wr[l, 1], lru_br[l, 1], lru_wi[l, 1],
                        lru_bi[l, 1], sp[1], B, S, reverse=True)
        ob = _lru_dir(proj, hrev, conv_w[l], conv_b[l], lru_wr[l, 0], lru_br[l, 0], lru_wi[l, 0],
                      lru_bi[l, 0], sp[0], B, S, reverse=False)

        oc = _na_latent(proj, na_tables[l], B, S)

        if not last:
            oa = _ctx_attn(
                q, lambda b, h: (nlat + b, h),
                k_all, pl.BlockSpec((None, TM, HEAD_DIM), lambda b, h: (b, nbs, h // g_att)),
                v_all, pl.BlockSpec((None, TM, HEAD_DIM), lambda b, h: (b, nbs, 2 * (h // g_att))),
                oa, B, ATT_Q_HEADS, nlat, None)
            oc = _ctx_attn(
                proj, lambda b, h: (nlat + b, COL_QN // HEAD_DIM + h),
                proj, pl.BlockSpec((TM, HEAD_DIM), lambda b, h: (nlat + b, COL_KN // HEAD_DIM + h)),
                proj, pl.BlockSpec((TM, HEAD_DIM), lambda b, h: (nlat + b, COL_VN // HEAD_DIM + h)),
                oc, B, NA_HEADS, nlat, HEAD_DIM ** -0.5)

        nblk = nlat if last else M // TM
        w_router = jnp.zeros((D, 128), F32).at[:, :MOE_GROUPS].set(router_wg[l])
        w_router = w_router.at[:, MOE_GROUPS:MOE_GROUPS + MOE_EXPERTS].set(router_we[l])
        b_router = jnp.zeros((1, 128), F32).at[0, :MOE_GROUPS].set(router_bg[l])
        b_router = b_router.at[0, MOE_GROUPS:MOE_GROUPS + MOE_EXPERTS].set(router_be[l])
        x_all, h2, route, cnt = _outproj(oa, ob, oc, x_rows, ctx_rows, mod, norm2_g[l], w_out[l].astype(BF16),
                                         w_router.astype(BF16), b_router, B, S, nblk * TM)
        res = _moe_layer(x_all, h2, route, cnt, mod, final_g, moe_w_gate, moe_w_up, moe_w_down, l,
                         seg_of, nblk, last)
        if last:
            out = res
        else:
            x_rows, ctx_rows = res, None
    return out.reshape(B, S, D)
```

```python
import functools

import numpy as np
import jax
import jax.numpy as jnp
from jax import lax
from jax.experimental import pallas as pl
from jax.experimental.pallas import tpu as pltpu

F32 = jnp.float32
BF16 = jnp.bfloat16

GRID_W = 64
HEAD_DIM = 128
ATT_Q_HEADS = 4
ATT_KV_HEADS = 2
LRU_WIDTH = 1024
LRU_BLOCKS = 8
LRU_C = 8.0
CONV_W = 4
NA_HEADS = 4
NA_ROWS = 8
NA_COLS = 16
ATT_Q_W = ATT_Q_HEADS * HEAD_DIM
ATT_KV_W = ATT_KV_HEADS * HEAD_DIM
NA_W = NA_HEADS * HEAD_DIM
MOE_GROUPS = 4
MOE_EXPERTS_PER_GROUP = 8
MOE_EXPERTS = MOE_GROUPS * MOE_EXPERTS_PER_GROUP
MOE_HIDDEN = 1024
ROPE_THETA = 10000.0
EPS = 1e-6
LOG2E = 1.4426950408889634

COL_QA = 0
COL_KA = COL_QA + ATT_Q_W
COL_VA = COL_KA + ATT_KV_W
COL_UB = COL_VA + ATT_KV_W
COL_GB = COL_UB + LRU_WIDTH
COL_QN = COL_GB + LRU_WIDTH
COL_KN = COL_QN + NA_W
COL_VN = COL_KN + NA_W
IN_WIDTH = COL_VN + NA_W

TM = 256
MOE_BM = 256
NA_QROWS = 4
NA_WROWS = 12
NEG_BIAS = -1e30
VMEM_LIMIT = 56 * 1024 * 1024


def _cparams(sem, vmem=None):
    return pltpu.CompilerParams(dimension_semantics=sem, vmem_limit_bytes=vmem)


def _rms(x):
    return x * lax.rsqrt(jnp.mean(x * x, axis=-1, keepdims=True) + EPS)


def _adaln_kernel(c_ref, w_ref, b_ref, o_ref):
    cv = c_ref[...]
    s = cv * jax.nn.sigmoid(cv)
    o_ref[0] = jnp.dot(s.astype(BF16), w_ref[0].astype(BF16), preferred_element_type=F32) + b_ref[0]


def _adaln(cvec, ada_w, ada_b):
    L, D, N = ada_w.shape
    tn = 1024
    return pl.pallas_call(
        _adaln_kernel,
        grid=(L, N // tn),
        in_specs=[
            pl.BlockSpec((8, D), lambda l, j: (0, 0)),
            pl.BlockSpec((1, D, tn), lambda l, j: (l, 0, j)),
            pl.BlockSpec((1, 1, tn), lambda l, j: (l, 0, j)),
        ],
        out_specs=pl.BlockSpec((1, 8, tn), lambda l, j: (l, 0, j)),
        out_shape=jax.ShapeDtypeStruct((L, 8, N), F32),
        compiler_params=_cparams(("parallel", "parallel"), VMEM_LIMIT),
        name="adaln",
    )(cvec, ada_w, ada_b.reshape(L, 1, N))


def _inproj_kernel(*refs, n_first, two_sources):
    if two_sources:
        x_ref, xb_ref, mod_ref, g_ref, w_ref, o_ref = refs
        x = jnp.where(pl.program_id(0) < n_first, x_ref[...], xb_ref[...])
    else:
        x_ref, mod_ref, g_ref, w_ref, o_ref = refs
        x = x_ref[...]
    h = _rms(x) * g_ref[...]
    h = h * (1.0 + mod_ref[0, 1:2, :]) + mod_ref[0, 0:1, :]
    o_ref[...] = jnp.dot(h.astype(BF16), w_ref[...], preferred_element_type=F32)


def _inproj(x_rows, ctx_rows, mod, g1, w_in_bf, layer, B, S):
    two = ctx_rows is not None
    D = x_rows.shape[1]
    M = x_rows.shape[0] + (ctx_rows.shape[0] if two else 0)
    N = w_in_bf.shape[2]
    tm = 512 if (S % 512 == 0 and (M - B * S) % 512 == 0) else TM
    per_seq = S // tm
    n_first = B * per_seq
    if two:
        n_ctx_blocks = ctx_rows.shape[0] // tm
        x_specs = [pl.BlockSpec((tm, D), lambda i: (jnp.minimum(i, n_first - 1), 0)),
                   pl.BlockSpec((tm, D), lambda i: (jnp.maximum(i - n_first, 0), 0),
                                pipeline_mode=pl.Buffered(1 if n_ctx_blocks == 1 else 2))]
        x_args = [x_rows, ctx_rows]
    else:
        x_specs = [pl.BlockSpec((tm, D), lambda i: (i, 0))]
        x_args = [x_rows]
    return pl.pallas_call(
        functools.partial(_inproj_kernel, n_first=n_first, two_sources=two),
        grid=(M // tm,),
        in_specs=[
            *x_specs,
            pl.BlockSpec((1, 6, D), lambda i: (jnp.minimum(i // per_seq, B), 0, 0)),
            pl.BlockSpec((1, D), lambda i: (0, 0)),
            pl.BlockSpec((None, D, N), lambda i: (layer, 0, 0), pipeline_mode=pl.Buffered(1)),
        ],
        out_specs=pl.BlockSpec((tm, N), lambda i: (i, 0)),
        out_shape=jax.ShapeDtypeStruct((M, N), F32),
        compiler_params=_cparams(("parallel",), VMEM_LIMIT),
        name="inproj",
    )(*x_args, mod, g1.reshape(1, D), w_in_bf)


def _norm_rope(t, g, cos, sin):
    y = _rms(t) * g
    lane = lax.broadcasted_iota(jnp.int32, y.shape, 1)
    first_half = (lane % 64) < 32
    partner = jnp.where(first_half, pltpu.roll(y, 96, 1), pltpu.roll(y, 32, 1))
    return y * cos + partner * sin


def _qkprep_kernel(p_ref, cos_ref, sin_ref, qg_ref, kg_ref, q_ref, k_ref, v_ref):
    cos = cos_ref[...]
    sin = sin_ref[...]
    scale = HEAD_DIM ** -0.5 * LOG2E
    for h in range(ATT_Q_HEADS):
        t = p_ref[:, COL_QA + h * HEAD_DIM:COL_QA + (h + 1) * HEAD_DIM]
        q_ref[:, h * HEAD_DIM:(h + 1) * HEAD_DIM] = (_norm_rope(t, qg_ref[...], cos, sin) * scale).astype(BF16)
    for h in range(ATT_KV_HEADS):
        t = p_ref[:, COL_KA + h * HEAD_DIM:COL_KA + (h + 1) * HEAD_DIM]
        k_ref[0, :, h * HEAD_DIM:(h + 1) * HEAD_DIM] = _norm_rope(t, kg_ref[...], cos, sin).astype(BF16)
        v_ref[0, :, 2 * h * HEAD_DIM:(2 * h + 1) * HEAD_DIM] = (
            p_ref[:, COL_VA + h * HEAD_DIM:COL_VA + (h + 1) * HEAD_DIM].astype(BF16))
        v_ref[0, :, (2 * h + 1) * HEAD_DIM:(2 * h + 2) * HEAD_DIM] = jnp.ones((p_ref.shape[0], HEAD_DIM), BF16)


def _qkprep(proj, cos_t, sin_t, qg, kg, B, S):
    M = proj.shape[0]
    nbs = S // TM
    nlat = B * nbs

    def b_of(i):
        return jnp.where(i < nlat, i // nbs, i - nlat)

    def pos_of(i):
        return jnp.where(i < nlat, i % nbs, nbs)

    kl = S + TM
    return pl.pallas_call(
        _qkprep_kernel,
        grid=(M // TM,),
        in_specs=[
            pl.BlockSpec((TM, COL_UB), lambda i: (i, 0)),
            pl.BlockSpec((TM, HEAD_DIM), lambda i: (pos_of(i), 0)),
            pl.BlockSpec((TM, HEAD_DIM), lambda i: (pos_of(i), 0)),
            pl.BlockSpec((1, HEAD_DIM), lambda i: (0, 0)),
            pl.BlockSpec((1, HEAD_DIM), lambda i: (0, 0)),
        ],
        out_specs=[
            pl.BlockSpec((TM, ATT_Q_W), lambda i: (i, 0)),
            pl.BlockSpec((1, TM, ATT_KV_W), lambda i: (b_of(i), pos_of(i), 0)),
            pl.BlockSpec((1, TM, 2 * ATT_KV_W), lambda i: (b_of(i), pos_of(i), 0)),
        ],
        out_shape=[
            jax.ShapeDtypeStruct((M, ATT_Q_W), BF16),
            jax.ShapeDtypeStruct((B, kl, ATT_KV_W), BF16),
            jax.ShapeDtypeStruct((B, kl, 2 * ATT_KV_W), BF16),
        ],
        compiler_params=_cparams(("parallel",), VMEM_LIMIT),
        name="qkprep",
    )(proj, cos_t, sin_t, qg.reshape(1, HEAD_DIM), kg.reshape(1, HEAD_DIM))


def _gqa_kernel(q_ref, k_ref, v_ref, o_in_ref, o_ref, q2_ref, sa_ref, sb_ref, m_ref, acc_ref, *, tk):
    del o_in_ref
    tq = q_ref.shape[0]
    q2_ref[0:tq, :] = q_ref[:, :HEAD_DIM]
    q2_ref[tq:, :] = q_ref[:, HEAD_DIM:]
    nk = k_ref.shape[1] // tk
    m_ref[...] = jnp.full(m_ref.shape, -jnp.inf, F32)
    acc_ref[...] = jnp.zeros(acc_ref.shape, F32)

    def scores(c, s_ref):
        off = pl.multiple_of(c * tk, tk)
        s_ref[...] = lax.dot_general(q2_ref[...], k_ref[0, pl.ds(off, tk), :], (((1,), (1,)), ((), ())),
                                     preferred_element_type=F32)

    def update(c, s_ref):
        off = pl.multiple_of(c * tk, tk)
        vc = v_ref[0, pl.ds(off, tk), :]
        for hh in range(2):
            rows = slice(hh * tq, (hh + 1) * tq)
            m = m_ref[rows, :]
            m_new = jnp.maximum(m, jnp.max(s_ref[rows, :], axis=-1, keepdims=True))
            m_ref[rows, :] = m_new
            p = jnp.exp2((s_ref[rows, :] - jnp.concatenate([m_new] * (tk // HEAD_DIM), axis=1)).astype(BF16))
            alpha = jnp.exp2(m - m_new)
            acc_ref[rows, :] = (jnp.concatenate([alpha, alpha], axis=1) * acc_ref[rows, :]
                                + jnp.dot(p, vc, preferred_element_type=F32))

    scores(0, sa_ref)
    n_pairs = (nk - 1) // 2

    def body(j, carry):
        scores(2 * j + 1, sb_ref)
        update(2 * j, sa_ref)
        scores(2 * j + 2, sa_ref)
        update(2 * j + 1, sb_ref)
        return carry

    lax.fori_loop(0, n_pairs, body, 0)
    if nk % 2 == 0:
        scores(nk - 1, sb_ref)
        update(nk - 2, sa_ref)
        update(nk - 1, sb_ref)
    else:
        update(nk - 1, sa_ref)
    acc = acc_ref[...]
    o = acc[:, :HEAD_DIM] / acc[:, HEAD_DIM:]
    o_ref[:, :HEAD_DIM] = o[:tq].astype(BF16)
    o_ref[:, HEAD_DIM:] = o[tq:].astype(BF16)


def _gqa_latent(q, k_all, v_all, B, S):
    M = q.shape[0]
    kl = k_all.shape[1]
    tq = 512 if S % 512 == 0 else TM
    nbs = S // tq
    tk = 768 if kl % 768 == 0 else TM
    g = ATT_Q_HEADS // ATT_KV_HEADS
    return pl.pallas_call(
        functools.partial(_gqa_kernel, tk=tk),
        grid=(B, ATT_KV_HEADS, nbs),
        in_specs=[
            pl.BlockSpec((tq, g * HEAD_DIM), lambda b, h, i: (b * nbs + i, h)),
            pl.BlockSpec((1, kl, HEAD_DIM), lambda b, h, i: (b, 0, h)),
            pl.BlockSpec((1, kl, 2 * HEAD_DIM), lambda b, h, i: (b, 0, h)),
            pl.BlockSpec(memory_space=pl.ANY),
        ],
        out_specs=pl.BlockSpec((tq, g * HEAD_DIM), lambda b, h, i: (b * nbs + i, h)),
        out_shape=jax.ShapeDtypeStruct((M, ATT_Q_W), BF16),
        input_output_aliases={3: 0},
        scratch_shapes=[
            pltpu.VMEM((g * tq, HEAD_DIM), BF16),
            pltpu.VMEM((g * tq, tk), F32),
            pltpu.VMEM((g * tq, tk), F32),
            pltpu.VMEM((g * tq, HEAD_DIM), F32),
            pltpu.VMEM((g * tq, 2 * HEAD_DIM), F32),
        ],
        compiler_params=_cparams(("parallel", "parallel", "parallel"), VMEM_LIMIT),
        name="gqa_latent",
    )(q, k_all, v_all, jnp.zeros((M, ATT_Q_W), BF16))


def _ctx_attn_kernel(q_ref, k_ref, v_ref, o_in_ref, o_ref, *, scale):
    del o_in_ref
    if scale is None:
        q = q_ref[...]
    else:
        q = (q_ref[...] * (scale * LOG2E)).astype(BF16)
    k = k_ref[...].astype(BF16)
    s = lax.dot_general(q, k, (((1,), (1,)), ((), ())), preferred_element_type=F32)
    p = jnp.exp2(s - jnp.max(s, axis=-1, keepdims=True))
    l = jnp.sum(p, axis=-1, keepdims=True)
    pv = jnp.dot(p.astype(BF16), v_ref[...].astype(BF16), preferred_element_type=F32)
    o_ref[...] = (pv / l).astype(BF16)


def _ctx_attn(q_arr, q_map, k_arr, k_spec, v_arr, v_spec, o_arr, B, n_heads, ctx_blk0, scale):
    return pl.pallas_call(
        functools.partial(_ctx_attn_kernel, scale=scale),
        grid=(B, n_heads),
        in_specs=[
            pl.BlockSpec((TM, HEAD_DIM), q_map),
            k_spec,
            v_spec,
            pl.BlockSpec(memory_space=pl.ANY),
        ],
        out_specs=pl.BlockSpec((TM, HEAD_DIM), lambda b, h: (ctx_blk0 + b, h)),
        out_shape=jax.ShapeDtypeStruct(o_arr.shape, o_arr.dtype),
        input_output_aliases={3: 0},
        compiler_params=_cparams(("parallel", "parallel"), VMEM_LIMIT),
        name="ctx_attn",
    )(q_arr, k_arr, v_arr, o_arr)


def _lru_kernel(u_ref, up_ref, un_ref, *rest, reverse, final):
    if final:
        (g_ref, hrev_ref, cw_ref, cb_ref, wr_ref, br_ref, wi_ref, bi_ref, sp_ref,
         o_ref, xpad, a_scr, b_scr, h_scr, hst) = rest
    else:
        (cw_ref, cb_ref, wr_ref, br_ref, wi_ref, bi_ref, sp_ref,
         o_ref, xpad, a_scr, b_scr, h_scr, hst) = rest
    s = pl.program_id(1)
    ns = pl.num_programs(1)
    T = u_ref.shape[0]
    W = u_ref.shape[1]

    @pl.when(s == 0)
    def _():
        hst[...] = jnp.zeros_like(hst)

    if reverse:
        j = ns - 1 - s
    else:
        j = s - 1
    is_first = jnp.logical_or(s == 0, j == 0)
    is_last = jnp.logical_or(s == 0, j == ns - 2)
    prev_rows = jnp.where(is_first, 0.0, up_ref[...])
    next_rows = jnp.where(is_last, 0.0, un_ref[...])
    xpad[0:8, :] = prev_rows
    xpad[8:8 + T, :] = u_ref[...]
    xpad[8 + T:16 + T, :] = next_rows
    u = (cb_ref[...]
         + xpad[7:7 + T, :] * cw_ref[0:1, :]
         + xpad[8:8 + T, :] * cw_ref[1:2, :]
         + xpad[9:9 + T, :] * cw_ref[2:3, :]
         + xpad[10:10 + T, :] * cw_ref[3:4, :])

    ub = u.astype(BF16)
    nb = LRU_WIDTH // LRU_BLOCKS
    for n in range(LRU_BLOCKS):
        sl = slice(n * nb, (n + 1) * nb)
        un_ = ub[:, sl]
        tr = jnp.tanh(jnp.dot(un_, wr_ref[n], preferred_element_type=F32) + br_ref[:, sl])
        ti = jnp.tanh(jnp.dot(un_, wi_ref[n], preferred_element_type=F32) + bi_ref[:, sl])
        ig = 0.5 * ti + 0.5
        log_a = sp_ref[:, sl] * tr + sp_ref[:, sl]
        a = jnp.exp(log_a)
        a_scr[:, sl] = a
        one_minus_a2 = -jnp.tanh(log_a) * (a * a + 1.0)
        root = one_minus_a2 * lax.rsqrt(jnp.maximum(one_minus_a2, 1e-30))
        b_scr[:, sl] = root * (ig * u[:, sl])

    def step(t, h):
        row = (T - 1 - t) if reverse else t
        h = a_scr[pl.ds(row, 1), :] * h + b_scr[pl.ds(row, 1), :]
        h_scr[pl.ds(row, 1), :] = h
        return h

    hst[...] = lax.fori_loop(0, T, step, hst[...], unroll=8)

    if final:
        y = h_scr[...] + hrev_ref[...]
        o_ref[...] = (y * jax.nn.gelu(g_ref[...])).astype(o_ref.dtype)
    else:
        o_ref[...] = h_scr[...]
    del W


def _lru_dir(proj, hrev, conv_w, conv_b, wr, br, wi, bi, sp, B, S, reverse):
    M = proj.shape[0]
    nbs = S // TM
    nlat = B * nbs
    final = hrev is not None
    ub_blk = COL_UB // LRU_WIDTH
    gb_blk = COL_GB // LRU_WIDTH
    r8 = TM // 8
    n8 = M // 8

    def blk(b, s):
        j = (nbs - s) if reverse else (s - 1)
        return jnp.where(s == 0, nlat + b, b * nbs + j)

    in_specs = [
        pl.BlockSpec((TM, LRU_WIDTH), lambda b, s: (blk(b, s), ub_blk)),
        pl.BlockSpec((8, LRU_WIDTH), lambda b, s: (jnp.maximum(blk(b, s) * r8 - 1, 0), ub_blk)),
        pl.BlockSpec((8, LRU_WIDTH), lambda b, s: (jnp.minimum((blk(b, s) + 1) * r8, n8 - 1), ub_blk)),
    ]
    args = [proj, proj, proj]
    if final:
        in_specs += [
            pl.BlockSpec((TM, LRU_WIDTH), lambda b, s: (blk(b, s), gb_blk)),
            pl.BlockSpec((TM, LRU_WIDTH), lambda b, s: (blk(b, s), 0)),
        ]
        args += [proj, hrev]
    const2 = lambda b, s: (0, 0)
    in_specs += [
        pl.BlockSpec((CONV_W, LRU_WIDTH), const2),
        pl.BlockSpec((1, LRU_WIDTH), const2),
        pl.BlockSpec((LRU_BLOCKS, LRU_WIDTH // LRU_BLOCKS, LRU_WIDTH // LRU_BLOCKS), lambda b, s: (0, 0, 0)),
        pl.BlockSpec((1, LRU_WIDTH), const2),
        pl.BlockSpec((LRU_BLOCKS, LRU_WIDTH // LRU_BLOCKS, LRU_WIDTH // LRU_BLOCKS), lambda b, s: (0, 0, 0)),
        pl.BlockSpec((1, LRU_WIDTH), const2),
        pl.BlockSpec((1, LRU_WIDTH), const2),
    ]
    args += [conv_w, conv_b.reshape(1, -1), (0.5 * wr).astype(BF16), 0.5 * br.reshape(1, -1),
             (0.5 * wi).astype(BF16), 0.5 * bi.reshape(1, -1), (-0.5 * LRU_C) * sp.reshape(1, -1)]
    return pl.pallas_call(
        functools.partial(_lru_kernel, reverse=reverse, final=final),
        grid=(B, nbs + 1),
        in_specs=in_specs,
        out_specs=pl.BlockSpec((TM, LRU_WIDTH), lambda b, s: (blk(b, s), 0)),
        out_shape=jax.ShapeDtypeStruct((M, LRU_WIDTH), BF16 if final else F32),
        scratch_shapes=[
            pltpu.VMEM((TM + 16, LRU_WIDTH), F32),
            pltpu.VMEM((TM, LRU_WIDTH), F32),
            pltpu.VMEM((TM, LRU_WIDTH), F32),
            pltpu.VMEM((TM, LRU_WIDTH), F32),
            pltpu.VMEM((1, LRU_WIDTH), F32),
        ],
        compiler_params=_cparams(("parallel", "arbitrary"), VMEM_LIMIT),
        name="lru_fwd" if final else "lru_rev",
    )(*args)


def _na_kernel(q_ref, k0_ref, k1_ref, k2_ref, v0_ref, v1_ref, v2_ref, kc_ref, vc_ref, tl_ref,
               o_in_ref, o_ref, s_scr0, s_scr1, *, n_rows):
    del o_in_ref
    i = pl.program_id(1)
    nbs = n_rows // NA_QROWS
    r0 = NA_QROWS * i
    ws = NA_QROWS * jnp.clip(i - 1, 0, nbs - 3)
    scale = HEAD_DIM ** -0.5 * LOG2E
    nkeys = s_scr0.shape[1]
    ones = jnp.ones((nkeys, HEAD_DIM), BF16)
    left_half = lax.broadcasted_iota(jnp.int32, (GRID_W, 2 * GRID_W), 1) < GRID_W

    def scores(h, s_ref):
        hs = slice(h * HEAD_DIM, (h + 1) * HEAD_DIM)
        qh = (q_ref[:, hs] * scale).astype(BF16)
        kh = jnp.concatenate([k0_ref[:, hs], k1_ref[:, hs], k2_ref[:, hs], kc_ref[:, hs]], axis=0).astype(BF16)
        s_ref[...] = lax.dot_general(qh, kh, (((1,), (1,)), ((), ())), preferred_element_type=F32)

    def finish(h, s_scr):
        hs = slice(h * HEAD_DIM, (h + 1) * HEAD_DIM)
        vh = jnp.concatenate([v0_ref[:, hs], v1_ref[:, hs], v2_ref[:, hs], vc_ref[:, hs]], axis=0).astype(BF16)
        vh = jnp.concatenate([vh, ones], axis=1)
        for qr in range(NA_QROWS):
            r = r0 + qr
            rs = jnp.clip(r - NA_ROWS // 2, 0, n_rows - NA_ROWS)

            def tile_of(a):
                kr = ws + a
                valid = jnp.logical_and(kr >= rs, kr < rs + NA_ROWS)
                return jnp.where(valid, kr - r + (NA_ROWS - 1), 2 * NA_ROWS - 1)

            for pr in range(NA_WROWS // 2):
                bias = jnp.where(left_half, tl_ref[h, tile_of(2 * pr)], tl_ref[h, tile_of(2 * pr + 1)])
                rsl = slice(qr * GRID_W, (qr + 1) * GRID_W)
                csl = slice(pr * 2 * GRID_W, (pr + 1) * 2 * GRID_W)
                s_scr[rsl, csl] = s_scr[rsl, csl] + bias
        s = s_scr[...]
        p = jnp.exp2((s - jnp.max(s, axis=-1, keepdims=True)).astype(BF16))
        pv = jnp.dot(p, vh, preferred_element_type=F32)
        o_ref[:, hs] = (pv[:, :HEAD_DIM] / pv[:, HEAD_DIM:]).astype(BF16)

    bufs = (s_scr0, s_scr1)
    scores(0, bufs[0])
    for h in range(NA_HEADS):
        if h + 1 < NA_HEADS:
            scores(h + 1, bufs[(h + 1) % 2])
        finish(h, bufs[h % 2])


def _na_latent(proj, tl, B, S):
    M = proj.shape[0]
    nbs = S // TM
    nlat = B * nbs
    n_rows = S // GRID_W
    qb, kb, vb = COL_QN // NA_W, COL_KN // NA_W, COL_VN // NA_W

    def wblk(b, i, t):
        return b * nbs + jnp.clip(i - 1, 0, nbs - 3) + t

    blk = (TM, NA_W)
    in_specs = [pl.BlockSpec(blk, lambda b, i: (b * nbs + i, qb))]
    in_specs += [pl.BlockSpec(blk, functools.partial(lambda b, i, t: (wblk(b, i, t), kb), t=t)) for t in range(3)]
    in_specs += [pl.BlockSpec(blk, functools.partial(lambda b, i, t: (wblk(b, i, t), vb), t=t)) for t in range(3)]
    in_specs += [
        pl.BlockSpec(blk, lambda b, i: (nlat + b, kb)),
        pl.BlockSpec(blk, lambda b, i: (nlat + b, vb)),
        pl.BlockSpec(tl.shape, lambda b, i: (0, 0, 0, 0)),
        pl.BlockSpec(memory_space=pl.ANY),
    ]
    return pl.pallas_call(
        functools.partial(_na_kernel, n_rows=n_rows),
        grid=(B, nbs),
        in_specs=in_specs,
        out_specs=pl.BlockSpec(blk, lambda b, i: (b * nbs + i, 0)),
        out_shape=jax.ShapeDtypeStruct((M, NA_W), BF16),
        input_output_aliases={10: 0},
        scratch_shapes=[pltpu.VMEM((TM, NA_WROWS * GRID_W + TM), F32),
                        pltpu.VMEM((TM, NA_WROWS * GRID_W + TM), F32)],
        compiler_params=_cparams(("parallel", "parallel"), VMEM_LIMIT),
        name="na_latent",
    )(proj, proj, proj, proj, proj, proj, proj, proj, proj, tl, jnp.zeros((M, NA_W), BF16))


def _na_bias_tables(rpb):
    j = np.arange(GRID_W)
    cs = np.clip(j - NA_COLS // 2, 0, GRID_W - NA_COLS)
    kc = np.arange(GRID_W)
    inside = (kc[None, :] >= cs[:, None]) & (kc[None, :] < cs[:, None] + NA_COLS)
    dc = kc[None, :] - j[:, None] + (NA_COLS - 1)
    sel = (dc[:, :, None] == np.arange(2 * NA_COLS - 1)[None, None, :]) & inside[:, :, None]
    t = jnp.einsum("lhrd,jkd->lhrjk", rpb * LOG2E, jnp.asarray(sel, F32), precision=lax.Precision.HIGHEST)
    t = jnp.where(inside, t, NEG_BIAS)
    t = jnp.concatenate([t, jnp.full_like(t[:, :, :1], NEG_BIAS)], axis=2)
    return jnp.concatenate([t, t], axis=-1)


def _outproj_kernel(*refs, n_first, n_steps, two_sources):
    if two_sources:
        (oa_ref, ob_ref, oc_ref, x_ref, xb_ref, mod_ref, g_ref, w_ref, wr_ref, br_ref,
         xo_ref, h2_ref, rt_ref, cnt_ref, carry, lg_scr, mix_scr) = refs
    else:
        (oa_ref, ob_ref, oc_ref, x_ref, mod_ref, g_ref, w_ref, wr_ref, br_ref,
         xo_ref, h2_ref, rt_ref, cnt_ref, carry, lg_scr, mix_scr) = refs
    step = pl.program_id(0)
    i = jnp.minimum(step, n_steps - 1)

    @pl.when(step == 0)
    def _():
        carry[...] = jnp.zeros_like(carry)
        lg_scr[...] = jnp.zeros_like(lg_scr)

    lg_prev = lg_scr[...]

    k1 = ATT_Q_W
    k2 = k1 + LRU_WIDTH
    for r0 in range(0, x_ref.shape[0], TM):
        rows = slice(r0, r0 + TM)
        mix_scr[rows, 0:k1] = oa_ref[rows, :]
        mix_scr[rows, k1:k2] = ob_ref[rows, :]
        mix_scr[rows, k2:] = oc_ref[rows, :]
        acc = jnp.dot(mix_scr[rows, :], w_ref[...], preferred_element_type=F32)
        x_in = x_ref[rows, :]
        if two_sources:
            x_in = jnp.where(i < n_first, x_in, xb_ref[rows, :])
        xn = x_in + mod_ref[0, 2:3, :] * acc
        xo_ref[rows, :] = xn
        h2 = _rms(xn) * g_ref[...]
        h2 = h2 * (1.0 + mod_ref[0, 4:5, :]) + mod_ref[0, 3:4, :]
        h2_ref[rows, :] = _pack_bf16_pair(h2)
        lg_scr[rows, :] = jnp.dot(h2.astype(BF16), wr_ref[...], preferred_element_type=F32) + br_ref[...]

    route = _route_math(lg_prev)
    rt_ref[...] = _rank_math(route, carry, step > 0)
    cnt_ref[...] = jnp.broadcast_to(carry[...], cnt_ref.shape)


def _pack_bf16_pair(x):
    w = x.shape[1] // 2
    lo = pltpu.bitcast(x[:, :w].astype(BF16).astype(F32), jnp.uint32)
    hi = pltpu.bitcast(x[:, w:].astype(BF16).astype(F32), jnp.uint32)
    return hi | (lo >> 16)


def _unpack_bf16_pair(words):
    lo = pltpu.bitcast(words << 16, F32)
    hi = pltpu.bitcast(words & jnp.uint32(0xFFFF0000), F32)
    return jnp.concatenate([lo, hi], axis=1).astype(BF16)


def _outproj(oa, ob, oc, x_rows, ctx_rows, mod, g2, w_out_bf, layer, w_router, b_router, B, S, n_rows):
    M = oa.shape[0]
    D = x_rows.shape[1]
    two = ctx_rows is not None
    tm = 512 if (S % 512 == 0 and (M - B * S) % 512 == 0) else TM
    per_seq = S // tm
    n_first = B * per_seq
    packed = jax.eval_shape(_pack_bf16_pair, jax.ShapeDtypeStruct((tm, D), F32))
    n_steps = n_rows // tm

    def blk(s):
        return jnp.minimum(s, n_steps - 1)

    if two:
        x_specs = [pl.BlockSpec((tm, D), lambda s: (jnp.minimum(blk(s), n_first - 1), 0)),
                   pl.BlockSpec((tm, D), lambda s: (jnp.maximum(blk(s) - n_first, 0), 0))]
        x_args = [x_rows, ctx_rows]
    else:
        x_specs = [pl.BlockSpec((tm, D), lambda s: (blk(s), 0))]
        x_args = [x_rows]
    return pl.pallas_call(
        functools.partial(_outproj_kernel, n_first=n_first, n_steps=n_steps, two_sources=two),
        grid=(n_steps + 1,),
        in_specs=[
            pl.BlockSpec((tm, ATT_Q_W), lambda s: (blk(s), 0)),
            pl.BlockSpec((tm, LRU_WIDTH), lambda s: (blk(s), 0)),
            pl.BlockSpec((tm, NA_W), lambda s: (blk(s), 0)),
            *x_specs,
            pl.BlockSpec((1, 6, D), lambda s: (jnp.minimum(blk(s) // per_seq, B), 0, 0)),
            pl.BlockSpec((1, D), lambda s: (0, 0)),
            pl.BlockSpec((None,) + w_out_bf.shape[1:], lambda s: (layer, 0, 0)),
            pl.BlockSpec(w_router.shape, lambda s: (0, 0)),
            pl.BlockSpec((1, 128), lambda s: (0, 0)),
        ],
        out_specs=[
            pl.BlockSpec((tm, D), lambda s: (blk(s), 0)),
            pl.BlockSpec(packed.shape, lambda s: (blk(s), 0)),
            pl.BlockSpec((tm, 128), lambda s: (jnp.maximum(s - 1, 0), 0)),
            pl.BlockSpec((8, 128), lambda s: (0, 0)),
        ],
        out_shape=[
            jax.ShapeDtypeStruct((n_rows, D), F32),
            jax.ShapeDtypeStruct((n_rows, packed.shape[1]), packed.dtype),
            jax.ShapeDtypeStruct((n_rows, 128), F32),
            jax.ShapeDtypeStruct((8, 128), F32),
        ],
        scratch_shapes=[pltpu.VMEM((1, 128), F32), pltpu.VMEM((tm, 128), F32),
                        pltpu.VMEM((tm, w_out_bf.shape[1]), BF16)],
        compiler_params=_cparams(("arbitrary",), VMEM_LIMIT),
        name="outproj",
    )(oa, ob, oc, *x_args, mod, g2.reshape(1, D), w_out_bf, w_router, b_router)


def _route_math(x):
    lane = lax.broadcasted_iota(jnp.int32, x.shape, 1)
    big = jnp.int32(1 << 20)
    ninf = -jnp.inf

    def first_argmax(vals):
        m = jnp.max(vals, axis=-1, keepdims=True)
        idx = jnp.min(jnp.where(vals == m, lane, big), axis=-1, keepdims=True)
        return m, idx

    lg = jnp.where(lane < MOE_GROUPS, x, ninf)
    mg, g_star = first_argmax(lg)
    g_gate = 1.0 / jnp.sum(jnp.exp(lg - mg), axis=-1, keepdims=True)
    lo = MOE_GROUPS + MOE_EXPERTS_PER_GROUP * g_star
    le = jnp.where(jnp.logical_and(lane >= lo, lane < lo + MOE_EXPERTS_PER_GROUP), x, ninf)
    v1, i1 = first_argmax(le)
    v2, i2 = first_argmax(jnp.where(lane == i1, ninf, le))
    e21 = jnp.exp(v2 - v1)
    w1 = g_gate / (1.0 + e21)
    w2 = g_gate * e21 / (1.0 + e21)
    e1 = (i1 - MOE_GROUPS).astype(F32)
    e2 = (i2 - MOE_GROUPS).astype(F32)
    return jnp.where(lane == 0, e1, jnp.where(lane == 1, e2, jnp.where(lane == 2, w1,
                     jnp.where(lane == 3, w2, 0.0))))


def _rank_math(r, carry, live):
    lane = lax.broadcasted_iota(jnp.int32, r.shape, 1).astype(F32)
    oh0 = jnp.where(lane == r[:, 0:1], 1.0, 0.0)
    oh1 = jnp.where(lane == r[:, 1:2], 1.0, 0.0)
    oh = oh0 + oh1
    n = r.shape[0]
    row = lax.broadcasted_iota(jnp.int32, (n, n), 0)
    col = lax.broadcasted_iota(jnp.int32, (n, n), 1)
    tri = jnp.where(col < row, 1.0, 0.0).astype(BF16)
    tot = carry[...] + jnp.dot(tri, oh.astype(BF16), preferred_element_type=F32)
    rank0 = jnp.sum(oh0 * tot, axis=-1, keepdims=True)
    rank1 = jnp.sum(oh1 * tot, axis=-1, keepdims=True)
    lane_i = lax.broadcasted_iota(jnp.int32, r.shape, 1)
    carry[...] = carry[...] + jnp.where(live, jnp.sum(oh, axis=0, keepdims=True), 0.0)
    return jnp.where(lane_i == 4, rank0, jnp.where(lane_i == 5, rank1, r))


def _dispatch_kernel(dest_ref, pad_ref, h_hbm, xs_hbm, hbuf, zeros, fsem, rsem, zsem, *, n_tok):
    i = pl.program_id(0)

    @pl.when(i == 0)
    def _():
        zeros[...] = jnp.zeros_like(zeros)
        pieces = [1 << b for b in range(MOE_BM.bit_length() - 2, 2, -1)]

        def copies(e):
            start, n = pad_ref[0, e], pad_ref[1, e]
            head = jnp.minimum((-start) & 7, n)
            out = []
            for r in range(7):
                out.append((r < head, pltpu.make_async_copy(zeros.at[pl.ds(0, 1)],
                                                            xs_hbm.at[pl.ds(start + r, 1)], zsem)))
            rest = n - head
            for p in pieces:
                at = pl.multiple_of(start + head + (rest & ~(2 * p - 1)), 8)
                out.append(((rest & p) != 0, pltpu.make_async_copy(zeros.at[pl.ds(0, p)],
                                                                   xs_hbm.at[pl.ds(at, p)], zsem)))
            return out

        def fill(e, c):
            for on, cp in copies(e):
                @pl.when(on)
                def _():
                    cp.start()
            return c

        def drain(e, c):
            for on, cp in copies(e):
                @pl.when(on)
                def _():
                    cp.wait()
            return c

        lax.fori_loop(0, MOE_EXPERTS, fill, 0)
        lax.fori_loop(0, MOE_EXPERTS, drain, 0)

        half = zeros.shape[0]
        n_tail = (xs_hbm.shape[0] - pad_ref[2, 0]) // half

        def tail_copy(j):
            at = pl.multiple_of(pad_ref[2, 0] + j * half, half)
            return pltpu.make_async_copy(zeros, xs_hbm.at[pl.ds(at, half)], zsem)

        def tail_fill(j, c):
            tail_copy(j).start()
            return c

        def tail_drain(j, c):
            tail_copy(j).wait()
            return c

        lax.fori_loop(0, n_tail, tail_fill, 0)
        lax.fori_loop(0, n_tail, tail_drain, 0)

    n = pl.num_programs(0)
    slot = i % 3
    nslot = (i + 1) % 3

    def fetch(blk, s):
        return pltpu.make_async_copy(h_hbm.at[pl.ds(pl.multiple_of(blk * TM, TM), TM)], hbuf.at[s], fsem.at[s])

    def rows_done(s):
        for k in range(2):
            pltpu.make_async_copy(hbuf.at[s], xs_hbm.at[pl.ds(0, TM)], rsem.at[s]).wait()

    @pl.when(i == 0)
    def _():
        fetch(0, 0).start()

    @pl.when(i >= 2)
    def _():
        rows_done(nslot)

    @pl.when(i + 1 < n)
    def _():
        fetch(i + 1, nslot).start()

    fetch(i, slot).wait()

    def issue(t, c):
        tok = i * TM + t
        for k in range(2):
            pltpu.make_async_copy(hbuf.at[slot, pl.ds(t, 1)], xs_hbm.at[pl.ds(dest_ref[k * n_tok + tok], 1)],
                                  rsem.at[slot]).start()
        return c

    lax.fori_loop(0, TM, issue, 0, unroll=8)

    @pl.when(i == n - 1)
    def _():
        rows_done(slot)

    @pl.when(jnp.logical_and(i == n - 1, i >= 1))
    def _():
        rows_done((i + 2) % 3)


def _dispatch(dest_flat, pad_info, h2, n_rows, nblk):
    W = h2.shape[1]
    return pl.pallas_call(
        functools.partial(_dispatch_kernel, n_tok=nblk * TM),
        grid_spec=pltpu.PrefetchScalarGridSpec(
            num_scalar_prefetch=2,
            grid=(nblk,),
            in_specs=[pl.BlockSpec(memory_space=pl.ANY)],
            out_specs=pl.BlockSpec(memory_space=pl.ANY),
            scratch_shapes=[pltpu.VMEM((3, TM, W), h2.dtype), pltpu.VMEM((MOE_BM // 2, W), h2.dtype),
                            pltpu.SemaphoreType.DMA((3,)), pltpu.SemaphoreType.DMA((3,)),
                            pltpu.SemaphoreType.DMA(())],
        ),
        out_shape=jax.ShapeDtypeStruct((n_rows, W), h2.dtype),
        compiler_params=pltpu.CompilerParams(dimension_semantics=("arbitrary",), has_side_effects=True),
        name="dispatch",
    )(dest_flat, pad_info, h2)


PLAN_EXPERT, PLAN_FIRST, PLAN_SLOT, PLAN_NEXT, PLAN_HALF = 0, 1, 2, 3, 4


def _block_plan(block_e, n_used, pad_starts, counts):
    nb = block_e.shape[0]
    idx = jnp.arange(nb, dtype=jnp.int32)
    is_first = (idx < n_used[0]) & ((idx == 0) | (block_e != jnp.roll(block_e, 1)))
    seg = jnp.cumsum(is_first.astype(jnp.int32)) - 1
    first_pos = jnp.where(is_first, idx, nb)
    at_or_after = jnp.flip(lax.cummin(jnp.flip(first_pos)))
    next_first = jnp.concatenate([at_or_after[1:], jnp.full((1,), nb, jnp.int32)])
    next_e = jnp.where(next_first < nb, block_e[jnp.minimum(next_first, nb - 1)], -1)
    valid = counts[block_e] - (idx * MOE_BM - pad_starts[block_e])
    half_only = (valid <= MOE_BM // 2).astype(jnp.int32)
    return jnp.stack([block_e, is_first.astype(jnp.int32), seg % 2, next_e, half_only]).astype(jnp.int32)


def _expert_weights_step(i, plan_ref, w_hbms, wbuf, wbf, sem, layer):
    def copies(e, slot):
        return [pltpu.make_async_copy(w.at[layer, e], wbuf.at[slot, n], sem.at[slot, n])
                for n, w in enumerate(w_hbms)]

    @pl.when(plan_ref[PLAN_FIRST, i] == 1)
    def _():
        slot = plan_ref[PLAN_SLOT, i]

        @pl.when(i == 0)
        def _():
            for c in copies(plan_ref[PLAN_EXPERT, 0], 0):
                c.start()

        for c in copies(plan_ref[PLAN_EXPERT, i], slot):
            c.wait()
        nrow, ncol = wbuf.shape[2], wbuf.shape[3]
        rows = 256

        def cast_rows(r, c):
            off = pl.multiple_of(r * rows, rows)
            for n in range(len(w_hbms)):
                wbf[pl.ds(off, rows), n * ncol:(n + 1) * ncol] = wbuf[slot, n, pl.ds(off, rows), :].astype(BF16)
            return c

        lax.fori_loop(0, nrow // rows, cast_rows, 0)
        nxt = plan_ref[PLAN_NEXT, i]

        @pl.when(nxt >= 0)
        def _():
            for c in copies(nxt, 1 - slot):
                c.start()


def _moe1_kernel(plan_ref, nu_ref, x_ref, wg_hbm, wu_hbm, o_ref, wbuf, wcat, sem, *, layer):
    i = pl.program_id(0)

    @pl.when(i < nu_ref[0])
    def _():
        _expert_weights_step(i, plan_ref, (wg_hbm, wu_hbm), wbuf, wcat, sem, layer)

        def gated(rows):
            x = _unpack_bf16_pair(x_ref[rows, :])
            hc = MOE_HIDDEN // 2
            for c in range(2):
                g = jnp.dot(x, wcat[:, c * hc:(c + 1) * hc], preferred_element_type=F32)
                u = jnp.dot(x, wcat[:, MOE_HIDDEN + c * hc:MOE_HIDDEN + (c + 1) * hc],
                            preferred_element_type=F32)
                o_ref[rows, c * hc:(c + 1) * hc] = (g * jax.nn.sigmoid(g) * u).astype(BF16)

        half = MOE_BM // 2

        @pl.when(plan_ref[PLAN_HALF, i] == 1)
        def _():
            gated(slice(0, half))
            o_ref[half:, :] = jnp.zeros((MOE_BM - half, MOE_HIDDEN), BF16)

        @pl.when(plan_ref[PLAN_HALF, i] == 0)
        def _():
            gated(slice(None))

    @pl.when(i >= nu_ref[0])
    def _():
        o_ref[...] = jnp.zeros_like(o_ref)


def _moe1(plan, n_used, xs, w_gate, w_up, layer):
    P, W = xs.shape
    D = w_gate.shape[2]
    nb = P // MOE_BM
    used_blk = lambda i, plan, nu: (jnp.minimum(i, nu[0] - 1), 0)
    return pl.pallas_call(
        functools.partial(_moe1_kernel, layer=layer),
        grid_spec=pltpu.PrefetchScalarGridSpec(
            num_scalar_prefetch=2,
            grid=(nb,),
            in_specs=[pl.BlockSpec((MOE_BM, W), used_blk), pl.BlockSpec(memory_space=pl.ANY),
                      pl.BlockSpec(memory_space=pl.ANY)],
            out_specs=pl.BlockSpec((MOE_BM, MOE_HIDDEN), lambda i, plan, nu: (i, 0)),
            scratch_shapes=[pltpu.VMEM((2, 2, D, MOE_HIDDEN), F32), pltpu.VMEM((D, 2 * MOE_HIDDEN), BF16),
                            pltpu.SemaphoreType.DMA((2, 2))],
        ),
        out_shape=jax.ShapeDtypeStruct((P, MOE_HIDDEN), BF16),
        compiler_params=_cparams(("arbitrary",), VMEM_LIMIT),
        name="moe_up",
    )(plan, n_used, xs, w_gate, w_up)


def _moe2_kernel(plan_ref, nu_ref, h_ref, wd_hbm, o_ref, wbuf, wbf, sem, *, layer):
    i = pl.program_id(0)

    @pl.when(i < nu_ref[0])
    def _():
        _expert_weights_step(i, plan_ref, (wd_hbm,), wbuf, wbf, sem, layer)
        half = MOE_BM // 2

        @pl.when(plan_ref[PLAN_HALF, i] == 1)
        def _():
            o_ref[0:half, :] = jnp.dot(h_ref[0:half, :], wbf[...], preferred_element_type=F32)
            o_ref[half:, :] = jnp.zeros((MOE_BM - half, o_ref.shape[1]), F32)

        @pl.when(plan_ref[PLAN_HALF, i] == 0)
        def _():
            o_ref[...] = jnp.dot(h_ref[...], wbf[...], preferred_element_type=F32)

    @pl.when(i >= nu_ref[0])
    def _():
        o_ref[...] = jnp.zeros_like(o_ref)


def _moe2(plan, n_used, hmid, w_down, layer):
    P = hmid.shape[0]
    D = w_down.shape[3]
    nb = P // MOE_BM
    return pl.pallas_call(
        functools.partial(_moe2_kernel, layer=layer),
        grid_spec=pltpu.PrefetchScalarGridSpec(
            num_scalar_prefetch=2,
            grid=(nb,),
            in_specs=[
                pl.BlockSpec((MOE_BM, MOE_HIDDEN), lambda i, plan, nu: (i, 0)),
                pl.BlockSpec(memory_space=pl.ANY),
            ],
            out_specs=pl.BlockSpec((MOE_BM, D), lambda i, plan, nu: (i, 0)),
            scratch_shapes=[pltpu.VMEM((2, 1, MOE_HIDDEN, D), F32), pltpu.VMEM((MOE_HIDDEN, D), BF16),
                            pltpu.SemaphoreType.DMA((2, 1))],
        ),
        out_shape=jax.ShapeDtypeStruct((P, D), F32),
        compiler_params=_cparams(("arbitrary",), VMEM_LIMIT),
        name="moe_down",
    )(plan, n_used, hmid, w_down)


def _combine_kernel(dest_ref, yb_hbm, x_ref, r_ref, mod_ref, fg_ref, o_ref, ybuf, sem, *, final):
    i = pl.program_id(0)
    n = pl.num_programs(0)

    n_tok = n * TM

    def row_copy(tok, k, t, slot):
        return pltpu.make_async_copy(yb_hbm.at[pl.ds(dest_ref[k * n_tok + tok], 1)],
                                     ybuf.at[slot, k, pl.ds(t, 1)], sem.at[slot])

    def issue_rows(blk, slot, t0):
        for r in range(8):
            for k in range(2):
                row_copy(blk * TM + t0 + r, k, t0 + r, slot).start()

    def wait_slot(slot):
        for k in range(2):
            pltpu.make_async_copy(yb_hbm.at[pl.ds(0, TM)], ybuf.at[slot, k], sem.at[slot]).wait()

    @pl.when(i == 0)
    def _():
        def first(j, c):
            issue_rows(0, 0, pl.multiple_of(j * 8, 8))
            return c
        lax.fori_loop(0, TM // 8, first, 0)

    def step(cur):
        @pl.when(i + 1 < n)
        def _():
            def body(j, c):
                issue_rows(i + 1, 1 - cur, pl.multiple_of(j * 8, 8))
                return c
            lax.fori_loop(0, TM // 8, body, 0)

        wait_slot(cur)
        r = r_ref[...]
        y = r[:, 2:3] * ybuf[cur, 0] + r[:, 3:4] * ybuf[cur, 1]
        xn = x_ref[...] + mod_ref[0, 5:6, :] * y
        if final:
            xn = _rms(xn) * fg_ref[...]
        o_ref[...] = xn

    @pl.when(i % 2 == 0)
    def _():
        step(0)

    @pl.when(i % 2 == 1)
    def _():
        step(1)


def _combine(dest_flat, yb, x_all, route, mod, final_g, seg_of, nblk, final):
    M, D = x_all.shape
    out_rows = nblk * TM if final else M
    kwargs = {} if final else {"input_output_aliases": {2: 0}}
    return pl.pallas_call(
        functools.partial(_combine_kernel, final=final),
        grid_spec=pltpu.PrefetchScalarGridSpec(
            num_scalar_prefetch=1,
            grid=(nblk,),
            in_specs=[
                pl.BlockSpec(memory_space=pl.ANY),
                pl.BlockSpec((TM, D), lambda i, d: (i, 0)),
                pl.BlockSpec((TM, 128), lambda i, d: (i, 0)),
                pl.BlockSpec((1, 6, D), lambda i, d: (seg_of(i), 0, 0)),
                pl.BlockSpec((1, D), lambda i, d: (0, 0)),
            ],
            out_specs=pl.BlockSpec((TM, D), lambda i, d: (i, 0)),
            scratch_shapes=[pltpu.VMEM((2, 2, TM, D), F32), pltpu.SemaphoreType.DMA((2,))],
        ),
        out_shape=jax.ShapeDtypeStruct((out_rows, D), F32),
        compiler_params=_cparams(("arbitrary",), VMEM_LIMIT),
        name="combine",
        **kwargs,
    )(dest_flat, yb, x_all, route, mod, final_g.reshape(1, D))


def _moe_layer(x_all, h2, route, cnt, mod, final_g, w_gate, w_up, w_down, layer, seg_of, nblk, final):
    T = nblk * TM
    e_ids = route[:, 0:2].T.astype(jnp.int32)
    ranks = route[:, 4:6].T.astype(jnp.int32)
    counts = cnt[0, :MOE_EXPERTS].astype(jnp.int32)
    padded = (counts + MOE_BM - 1) // MOE_BM * MOE_BM
    pad_ends = jnp.cumsum(padded)
    pad_starts = pad_ends - padded
    expert = jnp.arange(MOE_EXPERTS, dtype=jnp.int32)[:, None, None]
    start_of = jnp.sum(jnp.where(e_ids[None] == expert, pad_starts[:, None, None], 0), axis=0)
    dest = (start_of + ranks).reshape(-1)
    pad_info = jnp.stack([pad_starts + counts, padded - counts,
                          jnp.broadcast_to(pad_ends[-1], (MOE_EXPERTS,))]).astype(jnp.int32)
    n_blocks = -(-(2 * T + MOE_EXPERTS * (MOE_BM - 1)) // MOE_BM)
    block_starts = jnp.arange(n_blocks, dtype=jnp.int32) * MOE_BM
    block_e = jnp.sum((block_starts[:, None] >= pad_ends[None, :]).astype(jnp.int32), axis=1)
    block_e = jnp.minimum(block_e, MOE_EXPERTS - 1)
    n_used = (pad_ends[-1:] // MOE_BM).astype(jnp.int32)
    plan = _block_plan(block_e, n_used, pad_starts, counts)
    xs = _dispatch(dest, pad_info, h2, n_blocks * MOE_BM, nblk)
    hmid = _moe1(plan, n_used, xs, w_gate, w_up, layer)
    yb = _moe2(plan, n_used, hmid, w_down, layer)
    return _combine(dest, yb, x_all, route, mod, final_g, seg_of, nblk, final)


def _rope_tables(S):
    n_rows = S // GRID_W
    n_freq = HEAD_DIM // 4
    inv = 1.0 / (ROPE_THETA ** (jnp.arange(n_freq, dtype=F32) / n_freq))
    row_ang = jnp.arange(n_rows, dtype=jnp.int32).astype(F32)[:, None] * inv
    col_ang = jnp.arange(GRID_W, dtype=jnp.int32).astype(F32)[:, None] * inv
    cr, sr = jnp.repeat(jnp.cos(row_ang), GRID_W, axis=0), jnp.repeat(jnp.sin(row_ang), GRID_W, axis=0)
    cc, sc = jnp.tile(jnp.cos(col_ang), (n_rows, 1)), jnp.tile(jnp.sin(col_ang), (n_rows, 1))
    cos_t = jnp.concatenate([cr, cr, cc, cc], axis=1)
    sin_t = jnp.concatenate([-sr, sr, -sc, sc], axis=1)
    cos_t = jnp.concatenate([cos_t, jnp.ones((TM, HEAD_DIM), F32)], axis=0)
    sin_t = jnp.concatenate([sin_t, jnp.zeros((TM, HEAD_DIM), F32)], axis=0)
    return cos_t, sin_t


def kernel(x, c, ctx, c_ctx, ada_w, ada_b, norm1_g, norm2_g, w_in, w_out, att_q_norm, att_k_norm, conv_w, conv_b, lru_wr, lru_br, lru_wi, lru_bi, lru_lambda, na_rpb, router_wg, router_bg, router_we, router_be, moe_w_gate, moe_w_up, moe_w_down, final_g):
    B, S, D = x.shape
    C = ctx.shape[1]
    L = ada_w.shape[0]
    assert C == TM and S % TM == 0 and S // TM >= 3 and B + 1 <= 8
    nbs = S // TM
    nlat = B * nbs
    m_lat = B * S
    M = m_lat + B * C

    def seg_of(i):
        return jnp.minimum(i // nbs, B)

    x_rows, ctx_rows = x.reshape(m_lat, D), ctx.reshape(B * C, D)
    cvec = jnp.zeros((8, D), F32).at[:B].set(c).at[B].set(c_ctx)
    mod_all = _adaln(cvec, ada_w, ada_b)[:, :B + 1].reshape(L, B + 1, 6, D)
    cos_t, sin_t = _rope_tables(S)
    na_tables = _na_bias_tables(na_rpb)
    w_in_bf = w_in.astype(BF16)
    w_out_bf = w_out.astype(BF16)
    g_att = ATT_Q_HEADS // ATT_KV_HEADS

    out = None
    for l in range(L):
        last = l == L - 1
        mod = mod_all[l]
        proj = _inproj(x_rows, ctx_rows, mod, norm1_g[l], w_in_bf, l, B, S)

        q, k_all, v_all = _qkprep(proj, cos_t, sin_t, att_q_norm[l], att_k_norm[l], B, S)
        oa = _gqa_latent(q, k_all, v_all, B, S)

        sp = jax.nn.softplus(-lru_lambda[l].astype(F32))
        hrev = _lru_dir(proj, None, conv_w[l], conv_b[l], lru_wr[l, 1], lru_br[l, 1], lru_wi[l, 1],
                        lru_bi[l, 1], sp[1], B, S, reverse=True)
        ob = _lru_dir(proj, hrev, conv_w[l], conv_b[l], lru_wr[l, 0], lru_br[l, 0], lru_wi[l, 0],
                      lru_bi[l, 0], sp[0], B, S, reverse=False)

        oc = _na_latent(proj, na_tables[l], B, S)

        if not last:
            oa = _ctx_attn(
                q, lambda b, h: (nlat + b, h),
                k_all, pl.BlockSpec((None, TM, HEAD_DIM), lambda b, h: (b, nbs, h // g_att)),
                v_all, pl.BlockSpec((None, TM, HEAD_DIM), lambda b, h: (b, nbs, 2 * (h // g_att))),
                oa, B, ATT_Q_HEADS, nlat, None)
            oc = _ctx_attn(
                proj, lambda b, h: (nlat + b, COL_QN // HEAD_DIM + h),
                proj, pl.BlockSpec((TM, HEAD_DIM), lambda b, h: (nlat + b, COL_KN // HEAD_DIM + h)),
                proj, pl.BlockSpec((TM, HEAD_DIM), lambda b, h: (nlat + b, COL_VN // HEAD_DIM + h)),
                oc, B, NA_HEADS, nlat, HEAD_DIM ** -0.5)

        nblk = nlat if last else M // TM
        w_router = jnp.zeros((D, 128), F32).at[:, :MOE_GROUPS].set(router_wg[l])
        w_router = w_router.at[:, MOE_GROUPS:MOE_GROUPS + MOE_EXPERTS].set(router_we[l])
        b_router = jnp.zeros((1, 128), F32).at[0, :MOE_GROUPS].set(router_bg[l])
        b_router = b_router.at[0, MOE_GROUPS:MOE_GROUPS + MOE_EXPERTS].set(router_be[l])
        x_all, h2, route, cnt = _outproj(oa, ob, oc, x_rows, ctx_rows, mod, norm2_g[l], w_out_bf, l,
                                         w_router.astype(BF16), b_router, B, S, nblk * TM)
        res = _moe_layer(x_all, h2, route, cnt, mod, final_g, moe_w_gate, moe_w_up, moe_w_down, l,
                         seg_of, nblk, last)
        if last:
            out = res
        else:
            x_rows, ctx_rows = res, None
    return out.reshape(B, S, D)
```

```python
import functools

import numpy as np
import jax
import jax.numpy as jnp
from jax import lax
from jax.experimental import pallas as pl
from jax.experimental.pallas import tpu as pltpu

F32 = jnp.float32
BF16 = jnp.bfloat16

GRID_W = 64
HEAD_DIM = 128
ATT_Q_HEADS = 4
ATT_KV_HEADS = 2
LRU_WIDTH = 1024
LRU_BLOCKS = 8
LRU_C = 8.0
CONV_W = 4
NA_HEADS = 4
NA_ROWS = 8
NA_COLS = 16
ATT_Q_W = ATT_Q_HEADS * HEAD_DIM
ATT_KV_W = ATT_KV_HEADS * HEAD_DIM
NA_W = NA_HEADS * HEAD_DIM
MOE_GROUPS = 4
MOE_EXPERTS_PER_GROUP = 8
MOE_EXPERTS = MOE_GROUPS * MOE_EXPERTS_PER_GROUP
MOE_HIDDEN = 1024
ROPE_THETA = 10000.0
EPS = 1e-6
LOG2E = 1.4426950408889634

COL_QA = 0
COL_KA = COL_QA + ATT_Q_W
COL_VA = COL_KA + ATT_KV_W
COL_UB = COL_VA + ATT_KV_W
COL_GB = COL_UB + LRU_WIDTH
COL_QN = COL_GB + LRU_WIDTH
COL_KN = COL_QN + NA_W
COL_VN = COL_KN + NA_W
IN_WIDTH = COL_VN + NA_W

TM = 256
MOE_BM = 256
NA_QROWS = 4
NA_WROWS = 12
NEG_BIAS = -1e30
VMEM_LIMIT = 56 * 1024 * 1024


def _cparams(sem, vmem=None):
    return pltpu.CompilerParams(dimension_semantics=sem, vmem_limit_bytes=vmem)


def _rms(x):
    return x * lax.rsqrt(jnp.mean(x * x, axis=-1, keepdims=True) + EPS)


def _adaln_kernel(c_ref, w_ref, b_ref, o_ref):
    cv = c_ref[...]
    s = cv * jax.nn.sigmoid(cv)
    o_ref[0] = jnp.dot(s.astype(BF16), w_ref[0].astype(BF16), preferred_element_type=F32) + b_ref[0]


def _adaln(cvec, ada_w, ada_b):
    L, D, N = ada_w.shape
    tn = 1024
    return pl.pallas_call(
        _adaln_kernel,
        grid=(L, N // tn),
        in_specs=[
            pl.BlockSpec((8, D), lambda l, j: (0, 0)),
            pl.BlockSpec((1, D, tn), lambda l, j: (l, 0, j)),
            pl.BlockSpec((1, 1, tn), lambda l, j: (l, 0, j)),
        ],
        out_specs=pl.BlockSpec((1, 8, tn), lambda l, j: (l, 0, j)),
        out_shape=jax.ShapeDtypeStruct((L, 8, N), F32),
        compiler_params=_cparams(("parallel", "parallel"), VMEM_LIMIT),
        name="adaln",
    )(cvec, ada_w, ada_b.reshape(L, 1, N))


def _inproj_kernel(*refs, n_first, two_sources):
    if two_sources:
        x_ref, xb_ref, mod_ref, g_ref, w_ref, o_ref = refs
        x = jnp.where(pl.program_id(0) < n_first, x_ref[...], xb_ref[...])
    else:
        x_ref, mod_ref, g_ref, w_ref, o_ref = refs
        x = x_ref[...]
    h = _rms(x) * g_ref[...]
    h = h * (1.0 + mod_ref[0, 1:2, :]) + mod_ref[0, 0:1, :]
    o_ref[...] = jnp.dot(h.astype(BF16), w_ref[...], preferred_element_type=F32)


def _inproj(x_rows, ctx_rows, mod, g1, w_in_bf, layer, B, S):
    two = ctx_rows is not None
    D = x_rows.shape[1]
    M = x_rows.shape[0] + (ctx_rows.shape[0] if two else 0)
    N = w_in_bf.shape[2]
    tm = 512 if (S % 512 == 0 and (M - B * S) % 512 == 0) else TM
    per_seq = S // tm
    n_first = B * per_seq
    if two:
        n_ctx_blocks = ctx_rows.shape[0] // tm
        x_specs = [pl.BlockSpec((tm, D), lambda i: (jnp.minimum(i, n_first - 1), 0)),
                   pl.BlockSpec((tm, D), lambda i: (jnp.maximum(i - n_first, 0), 0),
                                pipeline_mode=pl.Buffered(1 if n_ctx_blocks == 1 else 2))]
        x_args = [x_rows, ctx_rows]
    else:
        x_specs = [pl.BlockSpec((tm, D), lambda i: (i, 0))]
        x_args = [x_rows]
    return pl.pallas_call(
        functools.partial(_inproj_kernel, n_first=n_first, two_sources=two),
        grid=(M // tm,),
        in_specs=[
            *x_specs,
            pl.BlockSpec((1, 6, D), lambda i: (jnp.minimum(i // per_seq, B), 0, 0)),
            pl.BlockSpec((1, D), lambda i: (0, 0)),
            pl.BlockSpec((None, D, N), lambda i: (layer, 0, 0), pipeline_mode=pl.Buffered(1)),
        ],
        out_specs=pl.BlockSpec((tm, N), lambda i: (i, 0)),
        out_shape=jax.ShapeDtypeStruct((M, N), F32),
        compiler_params=_cparams(("parallel",), VMEM_LIMIT),
        name="inproj",
    )(*x_args, mod, g1.reshape(1, D), w_in_bf)


def _norm_rope(t, g, cos, sin):
    y = _rms(t) * g
    lane = lax.broadcasted_iota(jnp.int32, y.shape, 1)
    first_half = (lane % 64) < 32
    partner = jnp.where(first_half, pltpu.roll(y, 96, 1), pltpu.roll(y, 32, 1))
    return y * cos + partner * sin


def _qkprep_kernel(p_ref, cos_ref, sin_ref, qg_ref, kg_ref, q_ref, k_ref, v_ref):
    cos = cos_ref[...]
    sin = sin_ref[...]
    scale = HEAD_DIM ** -0.5 * LOG2E
    for h in range(ATT_Q_HEADS):
        t = p_ref[:, COL_QA + h * HEAD_DIM:COL_QA + (h + 1) * HEAD_DIM]
        q_ref[:, h * HEAD_DIM:(h + 1) * HEAD_DIM] = (_norm_rope(t, qg_ref[...], cos, sin) * scale).astype(BF16)
    for h in range(ATT_KV_HEADS):
        t = p_ref[:, COL_KA + h * HEAD_DIM:COL_KA + (h + 1) * HEAD_DIM]
        k_ref[0, :, h * HEAD_DIM:(h + 1) * HEAD_DIM] = _norm_rope(t, kg_ref[...], cos, sin).astype(BF16)
        v_ref[0, :, 2 * h * HEAD_DIM:(2 * h + 1) * HEAD_DIM] = (
            p_ref[:, COL_VA + h * HEAD_DIM:COL_VA + (h + 1) * HEAD_DIM].astype(BF16))
        v_ref[0, :, (2 * h + 1) * HEAD_DIM:(2 * h + 2) * HEAD_DIM] = jnp.ones((p_ref.shape[0], HEAD_DIM), BF16)


def _qkprep(proj, cos_t, sin_t, qg, kg, B, S):
    M = proj.shape[0]
    nbs = S // TM
    nlat = B * nbs

    def b_of(i):
        return jnp.where(i < nlat, i // nbs, i - nlat)

    def pos_of(i):
        return jnp.where(i < nlat, i % nbs, nbs)

    kl = S + TM
    return pl.pallas_call(
        _qkprep_kernel,
        grid=(M // TM,),
        in_specs=[
            pl.BlockSpec((TM, COL_UB), lambda i: (i, 0)),
            pl.BlockSpec((TM, HEAD_DIM), lambda i: (pos_of(i), 0)),
            pl.BlockSpec((TM, HEAD_DIM), lambda i: (pos_of(i), 0)),
            pl.BlockSpec((1, HEAD_DIM), lambda i: (0, 0)),
            pl.BlockSpec((1, HEAD_DIM), lambda i: (0, 0)),
        ],
        out_specs=[
            pl.BlockSpec((TM, ATT_Q_W), lambda i: (i, 0)),
            pl.BlockSpec((1, TM, ATT_KV_W), lambda i: (b_of(i), pos_of(i), 0)),
            pl.BlockSpec((1, TM, 2 * ATT_KV_W), lambda i: (b_of(i), pos_of(i), 0)),
        ],
        out_shape=[
            jax.ShapeDtypeStruct((M, ATT_Q_W), BF16),
            jax.ShapeDtypeStruct((B, kl, ATT_KV_W), BF16),
            jax.ShapeDtypeStruct((B, kl, 2 * ATT_KV_W), BF16),
        ],
        compiler_params=_cparams(("parallel",), VMEM_LIMIT),
        name="qkprep",
    )(proj, cos_t, sin_t, qg.reshape(1, HEAD_DIM), kg.reshape(1, HEAD_DIM))


def _gqa_kernel(q_ref, k_ref, v_ref, o_in_ref, o_ref, q2_ref, sa_ref, sb_ref, m_ref, acc_ref, *, tk):
    del o_in_ref
    tq = q_ref.shape[0]
    q2_ref[0:tq, :] = q_ref[:, :HEAD_DIM]
    q2_ref[tq:, :] = q_ref[:, HEAD_DIM:]
    nk = k_ref.shape[1] // tk
    m_ref[...] = jnp.full(m_ref.shape, -jnp.inf, F32)
    acc_ref[...] = jnp.zeros(acc_ref.shape, F32)

    def scores(c, s_ref):
        off = pl.multiple_of(c * tk, tk)
        s_ref[...] = lax.dot_general(q2_ref[...], k_ref[0, pl.ds(off, tk), :], (((1,), (1,)), ((), ())),
                                     preferred_element_type=F32)

    def update(c, s_ref):
        off = pl.multiple_of(c * tk, tk)
        vc = v_ref[0, pl.ds(off, tk), :]
        for hh in range(2):
            rows = slice(hh * tq, (hh + 1) * tq)
            m = m_ref[rows, :]
            m_new = jnp.maximum(m, jnp.max(s_ref[rows, :], axis=-1, keepdims=True))
            m_ref[rows, :] = m_new
            p = jnp.exp2((s_ref[rows, :] - jnp.concatenate([m_new] * (tk // HEAD_DIM), axis=1)).astype(BF16))
            alpha = jnp.exp2(m - m_new)
            acc_ref[rows, :] = (jnp.concatenate([alpha, alpha], axis=1) * acc_ref[rows, :]
                                + jnp.dot(p, vc, preferred_element_type=F32))

    scores(0, sa_ref)
    n_pairs = (nk - 1) // 2

    def body(j, carry):
        scores(2 * j + 1, sb_ref)
        update(2 * j, sa_ref)
        scores(2 * j + 2, sa_ref)
        update(2 * j + 1, sb_ref)
        return carry

    lax.fori_loop(0, n_pairs, body, 0)
    if nk % 2 == 0:
        scores(nk - 1, sb_ref)
        update(nk - 2, sa_ref)
        update(nk - 1, sb_ref)
    else:
        update(nk - 1, sa_ref)
    acc = acc_ref[...]
    o = acc[:, :HEAD_DIM] / acc[:, HEAD_DIM:]
    o_ref[:, :HEAD_DIM] = o[:tq].astype(BF16)
    o_ref[:, HEAD_DIM:] = o[tq:].astype(BF16)


def _gqa_latent(q, k_all, v_all, B, S):
    M = q.shape[0]
    kl = k_all.shape[1]
    tq = 512 if S % 512 == 0 else TM
    nbs = S // tq
    tk = 768 if kl % 768 == 0 else TM
    g = ATT_Q_HEADS // ATT_KV_HEADS
    return pl.pallas_call(
        functools.partial(_gqa_kernel, tk=tk),
        grid=(B, ATT_KV_HEADS, nbs),
        in_specs=[
            pl.BlockSpec((tq, g * HEAD_DIM), lambda b, h, i: (b * nbs + i, h)),
            pl.BlockSpec((1, kl, HEAD_DIM), lambda b, h, i: (b, 0, h)),
            pl.BlockSpec((1, kl, 2 * HEAD_DIM), lambda b, h, i: (b, 0, h)),
            pl.BlockSpec(memory_space=pl.ANY),
        ],
        out_specs=pl.BlockSpec((tq, g * HEAD_DIM), lambda b, h, i: (b * nbs + i, h)),
        out_shape=jax.ShapeDtypeStruct((M, ATT_Q_W), BF16),
        input_output_aliases={3: 0},
        scratch_shapes=[
            pltpu.VMEM((g * tq, HEAD_DIM), BF16),
            pltpu.VMEM((g * tq, tk), F32),
            pltpu.VMEM((g * tq, tk), F32),
            pltpu.VMEM((g * tq, HEAD_DIM), F32),
            pltpu.VMEM((g * tq, 2 * HEAD_DIM), F32),
        ],
        compiler_params=_cparams(("parallel", "parallel", "parallel"), VMEM_LIMIT),
        name="gqa_latent",
    )(q, k_all, v_all, jnp.zeros((M, ATT_Q_W), BF16))


def _ctx_attn_kernel(q_ref, k_ref, v_ref, o_in_ref, o_ref, *, scale):
    del o_in_ref
    if scale is None:
        q = q_ref[...]
    else:
        q = (q_ref[...] * (scale * LOG2E)).astype(BF16)
    k = k_ref[...].astype(BF16)
    s = lax.dot_general(q, k, (((1,), (1,)), ((), ())), preferred_element_type=F32)
    p = jnp.exp2(s - jnp.max(s, axis=-1, keepdims=True))
    l = jnp.sum(p, axis=-1, keepdims=True)
    pv = jnp.dot(p.astype(BF16), v_ref[...].astype(BF16), preferred_element_type=F32)
    o_ref[...] = (pv / l).astype(BF16)


def _ctx_attn(q_arr, q_map, k_arr, k_spec, v_arr, v_spec, o_arr, B, n_heads, ctx_blk0, scale):
    return pl.pallas_call(
        functools.partial(_ctx_attn_kernel, scale=scale),
        grid=(B, n_heads),
        in_specs=[
            pl.BlockSpec((TM, HEAD_DIM), q_map),
            k_spec,
            v_spec,
            pl.BlockSpec(memory_space=pl.ANY),
        ],
        out_specs=pl.BlockSpec((TM, HEAD_DIM), lambda b, h: (ctx_blk0 + b, h)),
        out_shape=jax.ShapeDtypeStruct(o_arr.shape, o_arr.dtype),
        input_output_aliases={3: 0},
        compiler_params=_cparams(("parallel", "parallel"), VMEM_LIMIT),
        name="ctx_attn",
    )(q_arr, k_arr, v_arr, o_arr)


def _lru_kernel(u_ref, up_ref, un_ref, *rest, reverse, final):
    if final:
        (g_ref, hrev_ref, cw_ref, cb_ref, wr_ref, br_ref, wi_ref, bi_ref, sp_ref,
         o_ref, xpad, a_scr, b_scr, h_scr, hst) = rest
    else:
        (cw_ref, cb_ref, wr_ref, br_ref, wi_ref, bi_ref, sp_ref,
         o_ref, xpad, a_scr, b_scr, h_scr, hst) = rest
    s = pl.program_id(1)
    ns = pl.num_programs(1)
    T = u_ref.shape[0]
    W = u_ref.shape[1]

    @pl.when(s == 0)
    def _():
        hst[...] = jnp.zeros_like(hst)

    if reverse:
        j = ns - 1 - s
    else:
        j = s - 1
    is_first = jnp.logical_or(s == 0, j == 0)
    is_last = jnp.logical_or(s == 0, j == ns - 2)
    prev_rows = jnp.where(is_first, 0.0, up_ref[...])
    next_rows = jnp.where(is_last, 0.0, un_ref[...])
    xpad[0:8, :] = prev_rows
    xpad[8:8 + T, :] = u_ref[...]
    xpad[8 + T:16 + T, :] = next_rows
    u = (cb_ref[...]
         + xpad[7:7 + T, :] * cw_ref[0:1, :]
         + xpad[8:8 + T, :] * cw_ref[1:2, :]
         + xpad[9:9 + T, :] * cw_ref[2:3, :]
         + xpad[10:10 + T, :] * cw_ref[3:4, :])

    ub = u.astype(BF16)
    nb = LRU_WIDTH // LRU_BLOCKS
    for n in range(LRU_BLOCKS):
        sl = slice(n * nb, (n + 1) * nb)
        un_ = ub[:, sl]
        tr = jnp.tanh(jnp.dot(un_, wr_ref[n], preferred_element_type=F32) + br_ref[:, sl])
        ti = jnp.tanh(jnp.dot(un_, wi_ref[n], preferred_element_type=F32) + bi_ref[:, sl])
        ig = 0.5 * ti + 0.5
        log_a = sp_ref[:, sl] * tr + sp_ref[:, sl]
        a = jnp.exp(log_a)
        a_scr[:, sl] = a
        one_minus_a2 = -jnp.tanh(log_a) * (a * a + 1.0)
        root = one_minus_a2 * lax.rsqrt(jnp.maximum(one_minus_a2, 1e-30))
        b_scr[:, sl] = root * (ig * u[:, sl])

    def step(t, h):
        row = (T - 1 - t) if reverse else t
        h = a_scr[pl.ds(row, 1), :] * h + b_scr[pl.ds(row, 1), :]
        h_scr[pl.ds(row, 1), :] = h
        return h

    hst[...] = lax.fori_loop(0, T, step, hst[...], unroll=8)

    if final:
        y = h_scr[...] + hrev_ref[...]
        o_ref[...] = (y * jax.nn.gelu(g_ref[...])).astype(o_ref.dtype)
    else:
        o_ref[...] = h_scr[...]
    del W


def _lru_dir(proj, hrev, conv_w, conv_b, wr, br, wi, bi, sp, B, S, reverse):
    M = proj.shape[0]
    nbs = S // TM
    nlat = B * nbs
    final = hrev is not None
    ub_blk = COL_UB // LRU_WIDTH
    gb_blk = COL_GB // LRU_WIDTH
    r8 = TM // 8
    n8 = M // 8

    def blk(b, s):
        j = (nbs - s) if reverse else (s - 1)
        return jnp.where(s == 0, nlat + b, b * nbs + j)

    in_specs = [
        pl.BlockSpec((TM, LRU_WIDTH), lambda b, s: (blk(b, s), ub_blk)),
        pl.BlockSpec((8, LRU_WIDTH), lambda b, s: (jnp.maximum(blk(b, s) * r8 - 1, 0), ub_blk)),
        pl.BlockSpec((8, LRU_WIDTH), lambda b, s: (jnp.minimum((blk(b, s) + 1) * r8, n8 - 1), ub_blk)),
    ]
    args = [proj, proj, proj]
    if final:
        in_specs += [
            pl.BlockSpec((TM, LRU_WIDTH), lambda b, s: (blk(b, s), gb_blk)),
            pl.BlockSpec((TM, LRU_WIDTH), lambda b, s: (blk(b, s), 0)),
        ]
        args += [proj, hrev]
    const2 = lambda b, s: (0, 0)
    in_specs += [
        pl.BlockSpec((CONV_W, LRU_WIDTH), const2),
        pl.BlockSpec((1, LRU_WIDTH), const2),
        pl.BlockSpec((LRU_BLOCKS, LRU_WIDTH // LRU_BLOCKS, LRU_WIDTH // LRU_BLOCKS), lambda b, s: (0, 0, 0)),
        pl.BlockSpec((1, LRU_WIDTH), const2),
        pl.BlockSpec((LRU_BLOCKS, LRU_WIDTH // LRU_BLOCKS, LRU_WIDTH // LRU_BLOCKS), lambda b, s: (0, 0, 0)),
        pl.BlockSpec((1, LRU_WIDTH), const2),
        pl.BlockSpec((1, LRU_WIDTH), const2),
    ]
    args += [conv_w, conv_b.reshape(1, -1), (0.5 * wr).astype(BF16), 0.5 * br.reshape(1, -1),
             (0.5 * wi).astype(BF16), 0.5 * bi.reshape(1, -1), (-0.5 * LRU_C) * sp.reshape(1, -1)]
    return pl.pallas_call(
        functools.partial(_lru_kernel, reverse=reverse, final=final),
        grid=(B, nbs + 1),
        in_specs=in_specs,
        out_specs=pl.BlockSpec((TM, LRU_WIDTH), lambda b, s: (blk(b, s), 0)),
        out_shape=jax.ShapeDtypeStruct((M, LRU_WIDTH), BF16 if final else F32),
        scratch_shapes=[
            pltpu.VMEM((TM + 16, LRU_WIDTH), F32),
            pltpu.VMEM((TM, LRU_WIDTH), F32),
            pltpu.VMEM((TM, LRU_WIDTH), F32),
            pltpu.VMEM((TM, LRU_WIDTH), F32),
            pltpu.VMEM((1, LRU_WIDTH), F32),
        ],
        compiler_params=_cparams(("parallel", "arbitrary"), VMEM_LIMIT),
        name="lru_fwd" if final else "lru_rev",
    )(*args)


def _na_kernel(q_ref, k0_ref, k1_ref, k2_ref, v0_ref, v1_ref, v2_ref, kc_ref, vc_ref, tl_ref,
               o_in_ref, o_ref, s_scr0, s_scr1, *, n_rows):
    del o_in_ref
    i = pl.program_id(1)
    nbs = n_rows // NA_QROWS
    r0 = NA_QROWS * i
    ws = NA_QROWS * jnp.clip(i - 1, 0, nbs - 3)
    scale = HEAD_DIM ** -0.5 * LOG2E
    nkeys = s_scr0.shape[1]
    ones = jnp.ones((nkeys, HEAD_DIM), BF16)
    left_half = lax.broadcasted_iota(jnp.int32, (GRID_W, 2 * GRID_W), 1) < GRID_W

    def scores(h, s_ref):
        hs = slice(h * HEAD_DIM, (h + 1) * HEAD_DIM)
        qh = (q_ref[:, hs] * scale).astype(BF16)
        kh = jnp.concatenate([k0_ref[:, hs], k1_ref[:, hs], k2_ref[:, hs], kc_ref[:, hs]], axis=0).astype(BF16)
        s_ref[...] = lax.dot_general(qh, kh, (((1,), (1,)), ((), ())), preferred_element_type=F32)

    def finish(h, s_scr):
        hs = slice(h * HEAD_DIM, (h + 1) * HEAD_DIM)
        vh = jnp.concatenate([v0_ref[:, hs], v1_ref[:, hs], v2_ref[:, hs], vc_ref[:, hs]], axis=0).astype(BF16)
        vh = jnp.concatenate([vh, ones], axis=1)
        for qr in range(NA_QROWS):
            r = r0 + qr
            rs = jnp.clip(r - NA_ROWS // 2, 0, n_rows - NA_ROWS)

            def tile_of(a):
                kr = ws + a
                valid = jnp.logical_and(kr >= rs, kr < rs + NA_ROWS)
                return jnp.where(valid, kr - r + (NA_ROWS - 1), 2 * NA_ROWS - 1)

            for pr in range(NA_WROWS // 2):
                bias = jnp.where(left_half, tl_ref[h, tile_of(2 * pr)], tl_ref[h, tile_of(2 * pr + 1)])
                rsl = slice(qr * GRID_W, (qr + 1) * GRID_W)
                csl = slice(pr * 2 * GRID_W, (pr + 1) * 2 * GRID_W)
                s_scr[rsl, csl] = s_scr[rsl, csl] + bias
        s = s_scr[...]
        p = jnp.exp2((s - jnp.max(s, axis=-1, keepdims=True)).astype(BF16))
        pv = jnp.dot(p, vh, preferred_element_type=F32)
        o_ref[:, hs] = (pv[:, :HEAD_DIM] / pv[:, HEAD_DIM:]).astype(BF16)

    bufs = (s_scr0, s_scr1)
    scores(0, bufs[0])
    for h in range(NA_HEADS):
        if h + 1 < NA_HEADS:
            scores(h + 1, bufs[(h + 1) % 2])
        finish(h, bufs[h % 2])


def _na_latent(proj, tl, B, S):
    M = proj.shape[0]
    nbs = S // TM
    nlat = B * nbs
    n_rows = S // GRID_W
    qb, kb, vb = COL_QN // NA_W, COL_KN // NA_W, COL_VN // NA_W

    def wblk(b, i, t):
        return b * nbs + jnp.clip(i - 1, 0, nbs - 3) + t

    blk = (TM, NA_W)
    in_specs = [pl.BlockSpec(blk, lambda b, i: (b * nbs + i, qb))]
    in_specs += [pl.BlockSpec(blk, functools.partial(lambda b, i, t: (wblk(b, i, t), kb), t=t)) for t in range(3)]
    in_specs += [pl.BlockSpec(blk, functools.partial(lambda b, i, t: (wblk(b, i, t), vb), t=t)) for t in range(3)]
    in_specs += [
        pl.BlockSpec(blk, lambda b, i: (nlat + b, kb)),
        pl.BlockSpec(blk, lambda b, i: (nlat + b, vb)),
        pl.BlockSpec(tl.shape, lambda b, i: (0, 0, 0, 0)),
        pl.BlockSpec(memory_space=pl.ANY),
    ]
    return pl.pallas_call(
        functools.partial(_na_kernel, n_rows=n_rows),
        grid=(B, nbs),
        in_specs=in_specs,
        out_specs=pl.BlockSpec(blk, lambda b, i: (b * nbs + i, 0)),
        out_shape=jax.ShapeDtypeStruct((M, NA_W), BF16),
        input_output_aliases={10: 0},
        scratch_shapes=[pltpu.VMEM((TM, NA_WROWS * GRID_W + TM), F32),
                        pltpu.VMEM((TM, NA_WROWS * GRID_W + TM), F32)],
        compiler_params=_cparams(("parallel", "parallel"), VMEM_LIMIT),
        name="na_latent",
    )(proj, proj, proj, proj, proj, proj, proj, proj, proj, tl, jnp.zeros((M, NA_W), BF16))


def _na_bias_tables(rpb):
    j = np.arange(GRID_W)
    cs = np.clip(j - NA_COLS // 2, 0, GRID_W - NA_COLS)
    kc = np.arange(GRID_W)
    inside = (kc[None, :] >= cs[:, None]) & (kc[None, :] < cs[:, None] + NA_COLS)
    dc = kc[None, :] - j[:, None] + (NA_COLS - 1)
    sel = (dc[:, :, None] == np.arange(2 * NA_COLS - 1)[None, None, :]) & inside[:, :, None]
    t = jnp.einsum("lhrd,jkd->lhrjk", rpb * LOG2E, jnp.asarray(sel, F32), precision=lax.Precision.HIGHEST)
    t = jnp.where(inside, t, NEG_BIAS)
    t = jnp.concatenate([t, jnp.full_like(t[:, :, :1], NEG_BIAS)], axis=2)
    return jnp.concatenate([t, t], axis=-1)


def _outproj_kernel(*refs, n_first, n_steps, two_sources):
    if two_sources:
        (oa_ref, ob_ref, oc_ref, x_ref, xb_ref, mod_ref, g_ref, w_ref, wr_ref, br_ref,
         xo_ref, h2_ref, rt_ref, cnt_ref, carry, lg_scr, mix_scr) = refs
    else:
        (oa_ref, ob_ref, oc_ref, x_ref, mod_ref, g_ref, w_ref, wr_ref, br_ref,
         xo_ref, h2_ref, rt_ref, cnt_ref, carry, lg_scr, mix_scr) = refs
    step = pl.program_id(0)
    i = jnp.minimum(step, n_steps - 1)

    @pl.when(step == 0)
    def _():
        carry[...] = jnp.zeros_like(carry)
        lg_scr[...] = jnp.zeros_like(lg_scr)

    lg_prev = lg_scr[...]

    k1 = ATT_Q_W
    k2 = k1 + LRU_WIDTH
    for r0 in range(0, x_ref.shape[0], TM):
        rows = slice(r0, r0 + TM)
        mix_scr[rows, 0:k1] = oa_ref[rows, :]
        mix_scr[rows, k1:k2] = ob_ref[rows, :]
        mix_scr[rows, k2:] = oc_ref[rows, :]
        acc = jnp.dot(mix_scr[rows, :], w_ref[...], preferred_element_type=F32)
        x_in = x_ref[rows, :]
        if two_sources:
            x_in = jnp.where(i < n_first, x_in, xb_ref[rows, :])
        xn = x_in + mod_ref[0, 2:3, :] * acc
        xo_ref[rows, :] = xn
        h2 = _rms(xn) * g_ref[...]
        h2 = h2 * (1.0 + mod_ref[0, 4:5, :]) + mod_ref[0, 3:4, :]
        h2_ref[rows, :] = _pack_bf16_pair(h2)
        lg_scr[rows, :] = jnp.dot(h2.astype(BF16), wr_ref[...], preferred_element_type=F32) + br_ref[...]

    route = _route_math(lg_prev)
    rt_ref[...] = _rank_math(route, carry, step > 0)
    cnt_ref[...] = jnp.broadcast_to(carry[...], cnt_ref.shape)


def _pack_bf16_pair(x):
    w = x.shape[1] // 2
    lo = pltpu.bitcast(x[:, :w].astype(BF16).astype(F32), jnp.uint32)
    hi = pltpu.bitcast(x[:, w:].astype(BF16).astype(F32), jnp.uint32)
    return hi | (lo >> 16)


def _unpack_bf16_pair(words):
    lo = pltpu.bitcast(words << 16, F32)
    hi = pltpu.bitcast(words & jnp.uint32(0xFFFF0000), F32)
    return jnp.concatenate([lo, hi], axis=1).astype(BF16)


def _outproj(oa, ob, oc, x_rows, ctx_rows, mod, g2, w_out_bf, layer, w_router, b_router, B, S, n_rows):
    M = oa.shape[0]
    D = x_rows.shape[1]
    two = ctx_rows is not None
    tm = 512 if (S % 512 == 0 and (M - B * S) % 512 == 0) else TM
    per_seq = S // tm
    n_first = B * per_seq
    packed = jax.eval_shape(_pack_bf16_pair, jax.ShapeDtypeStruct((tm, D), F32))
    n_steps = n_rows // tm

    def blk(s):
        return jnp.minimum(s, n_steps - 1)

    if two:
        x_specs = [pl.BlockSpec((tm, D), lambda s: (jnp.minimum(blk(s), n_first - 1), 0)),
                   pl.BlockSpec((tm, D), lambda s: (jnp.maximum(blk(s) - n_first, 0), 0))]
        x_args = [x_rows, ctx_rows]
    else:
        x_specs = [pl.BlockSpec((tm, D), lambda s: (blk(s), 0))]
        x_args = [x_rows]
    return pl.pallas_call(
        functools.partial(_outproj_kernel, n_first=n_first, n_steps=n_steps, two_sources=two),
        grid=(n_steps + 1,),
        in_specs=[
            pl.BlockSpec((tm, ATT_Q_W), lambda s: (blk(s), 0)),
            pl.BlockSpec((tm, LRU_WIDTH), lambda s: (blk(s), 0)),
            pl.BlockSpec((tm, NA_W), lambda s: (blk(s), 0)),
            *x_specs,
            pl.BlockSpec((1, 6, D), lambda s: (jnp.minimum(blk(s) // per_seq, B), 0, 0)),
            pl.BlockSpec((1, D), lambda s: (0, 0)),
            pl.BlockSpec((None,) + w_out_bf.shape[1:], lambda s: (layer, 0, 0)),
            pl.BlockSpec(w_router.shape, lambda s: (0, 0)),
            pl.BlockSpec((1, 128), lambda s: (0, 0)),
        ],
        out_specs=[
            pl.BlockSpec((tm, D), lambda s: (blk(s), 0)),
            pl.BlockSpec(packed.shape, lambda s: (blk(s), 0)),
            pl.BlockSpec((tm, 128), lambda s: (jnp.maximum(s - 1, 0), 0)),
            pl.BlockSpec((8, 128), lambda s: (0, 0)),
        ],
        out_shape=[
            jax.ShapeDtypeStruct((n_rows, D), F32),
            jax.ShapeDtypeStruct((n_rows, packed.shape[1]), packed.dtype),
            jax.ShapeDtypeStruct((n_rows, 128), F32),
            jax.ShapeDtypeStruct((8, 128), F32),
        ],
        scratch_shapes=[pltpu.VMEM((1, 128), F32), pltpu.VMEM((tm, 128), F32),
                        pltpu.VMEM((tm, w_out_bf.shape[1]), BF16)],
        compiler_params=_cparams(("arbitrary",), VMEM_LIMIT),
        name="outproj",
    )(oa, ob, oc, *x_args, mod, g2.reshape(1, D), w_out_bf, w_router, b_router)


def _route_math(x):
    lane = lax.broadcasted_iota(jnp.int32, x.shape, 1)
    big = jnp.int32(1 << 20)
    ninf = -jnp.inf

    def first_argmax(vals):
        m = jnp.max(vals, axis=-1, keepdims=True)
        idx = jnp.min(jnp.where(vals == m, lane, big), axis=-1, keepdims=True)
        return m, idx

    lg = jnp.where(lane < MOE_GROUPS, x, ninf)
    mg, g_star = first_argmax(lg)
    g_gate = 1.0 / jnp.sum(jnp.exp(lg - mg), axis=-1, keepdims=True)
    lo = MOE_GROUPS + MOE_EXPERTS_PER_GROUP * g_star
    le = jnp.where(jnp.logical_and(lane >= lo, lane < lo + MOE_EXPERTS_PER_GROUP), x, ninf)
    v1, i1 = first_argmax(le)
    v2, i2 = first_argmax(jnp.where(lane == i1, ninf, le))
    e21 = jnp.exp(v2 - v1)
    w1 = g_gate / (1.0 + e21)
    w2 = g_gate * e21 / (1.0 + e21)
    e1 = (i1 - MOE_GROUPS).astype(F32)
    e2 = (i2 - MOE_GROUPS).astype(F32)
    return jnp.where(lane == 0, e1, jnp.where(lane == 1, e2, jnp.where(lane == 2, w1,
                     jnp.where(lane == 3, w2, 0.0))))


def _rank_math(r, carry, live):
    lane = lax.broadcasted_iota(jnp.int32, r.shape, 1).astype(F32)
    oh0 = jnp.where(lane == r[:, 0:1], 1.0, 0.0)
    oh1 = jnp.where(lane == r[:, 1:2], 1.0, 0.0)
    oh = oh0 + oh1
    n = r.shape[0]
    row = lax.broadcasted_iota(jnp.int32, (n, n), 0)
    col = lax.broadcasted_iota(jnp.int32, (n, n), 1)
    tri = jnp.where(col < row, 1.0, 0.0).astype(BF16)
    tot = carry[...] + jnp.dot(tri, oh.astype(BF16), preferred_element_type=F32)
    rank0 = jnp.sum(oh0 * tot, axis=-1, keepdims=True)
    rank1 = jnp.sum(oh1 * tot, axis=-1, keepdims=True)
    lane_i = lax.broadcasted_iota(jnp.int32, r.shape, 1)
    carry[...] = carry[...] + jnp.where(live, jnp.sum(oh, axis=0, keepdims=True), 0.0)
    return jnp.where(lane_i == 4, rank0, jnp.where(lane_i == 5, rank1, r))


def _dispatch_kernel(dest_ref, pad_ref, h_hbm, xs_hbm, hbuf, zeros, fsem, rsem, zsem, *, n_tok):
    i = pl.program_id(0)

    @pl.when(i == 0)
    def _():
        zeros[...] = jnp.zeros_like(zeros)
        pieces = [1 << b for b in range(MOE_BM.bit_length() - 2, 2, -1)]

        def copies(e):
            start, n = pad_ref[0, e], pad_ref[1, e]
            head = jnp.minimum((-start) & 7, n)
            out = []
            for r in range(7):
                out.append((r < head, pltpu.make_async_copy(zeros.at[pl.ds(0, 1)],
                                                            xs_hbm.at[pl.ds(start + r, 1)], zsem)))
            rest = n - head
            for p in pieces:
                at = pl.multiple_of(start + head + (rest & ~(2 * p - 1)), 8)
                out.append(((rest & p) != 0, pltpu.make_async_copy(zeros.at[pl.ds(0, p)],
                                                                   xs_hbm.at[pl.ds(at, p)], zsem)))
            return out

        def fill(e, c):
            for on, cp in copies(e):
                @pl.when(on)
                def _():
                    cp.start()
            return c

        def drain(e, c):
            for on, cp in copies(e):
                @pl.when(on)
                def _():
                    cp.wait()
            return c

        lax.fori_loop(0, MOE_EXPERTS, fill, 0)
        lax.fori_loop(0, MOE_EXPERTS, drain, 0)

        half = zeros.shape[0]
        n_tail = (xs_hbm.shape[0] - pad_ref[2, 0]) // half

        def tail_copy(j):
            at = pl.multiple_of(pad_ref[2, 0] + j * half, half)
            return pltpu.make_async_copy(zeros, xs_hbm.at[pl.ds(at, half)], zsem)

        def tail_fill(j, c):
            tail_copy(j).start()
            return c

        def tail_drain(j, c):
            tail_copy(j).wait()
            return c

        lax.fori_loop(0, n_tail, tail_fill, 0)
        lax.fori_loop(0, n_tail, tail_drain, 0)

    n = pl.num_programs(0)
    slot = i % 3
    nslot = (i + 1) % 3

    def fetch(blk, s):
        return pltpu.make_async_copy(h_hbm.at[pl.ds(pl.multiple_of(blk * TM, TM), TM)], hbuf.at[s], fsem.at[s])

    def rows_done(s):
        for k in range(2):
            pltpu.make_async_copy(hbuf.at[s], xs_hbm.at[pl.ds(0, TM)], rsem.at[s]).wait()

    @pl.when(i == 0)
    def _():
        fetch(0, 0).start()

    @pl.when(i >= 2)
    def _():
        rows_done(nslot)

    @pl.when(i + 1 < n)
    def _():
        fetch(i + 1, nslot).start()

    fetch(i, slot).wait()

    def issue(t, c):
        tok = i * TM + t
        for k in range(2):
            pltpu.make_async_copy(hbuf.at[slot, pl.ds(t, 1)], xs_hbm.at[pl.ds(dest_ref[k * n_tok + tok], 1)],
                                  rsem.at[slot]).start()
        return c

    lax.fori_loop(0, TM, issue, 0, unroll=8)

    @pl.when(i == n - 1)
    def _():
        rows_done(slot)

    @pl.when(jnp.logical_and(i == n - 1, i >= 1))
    def _():
        rows_done((i + 2) % 3)


def _dispatch(dest_flat, pad_info, h2, n_rows, nblk):
    W = h2.shape[1]
    return pl.pallas_call(
        functools.partial(_dispatch_kernel, n_tok=nblk * TM),
        grid_spec=pltpu.PrefetchScalarGridSpec(
            num_scalar_prefetch=2,
            grid=(nblk,),
            in_specs=[pl.BlockSpec(memory_space=pl.ANY)],
            out_specs=pl.BlockSpec(memory_space=pl.ANY),
            scratch_shapes=[pltpu.VMEM((3, TM, W), h2.dtype), pltpu.VMEM((MOE_BM // 2, W), h2.dtype),
                            pltpu.SemaphoreType.DMA((3,)), pltpu.SemaphoreType.DMA((3,)),
                            pltpu.SemaphoreType.DMA(())],
        ),
        out_shape=jax.ShapeDtypeStruct((n_rows, W), h2.dtype),
        compiler_params=pltpu.CompilerParams(dimension_semantics=("arbitrary",), has_side_effects=True),
        name="dispatch",
    )(dest_flat, pad_info, h2)


PLAN_EXPERT, PLAN_FIRST, PLAN_SLOT, PLAN_NEXT, PLAN_HALF = 0, 1, 2, 3, 4


def _block_plan(block_e, n_used, pad_starts, counts):
    nb = block_e.shape[0]
    idx = jnp.arange(nb, dtype=jnp.int32)
    is_first = (idx < n_used[0]) & ((idx == 0) | (block_e != jnp.roll(block_e, 1)))
    seg = jnp.cumsum(is_first.astype(jnp.int32)) - 1
    first_pos = jnp.where(is_first, idx, nb)
    at_or_after = jnp.flip(lax.cummin(jnp.flip(first_pos)))
    next_first = jnp.concatenate([at_or_after[1:], jnp.full((1,), nb, jnp.int32)])
    pick_blk = next_first[:, None] == idx[None, :]
    next_e = jnp.where(next_first < nb, jnp.sum(jnp.where(pick_blk, block_e[None, :], 0), axis=1), -1)
    pick_e = block_e[:, None] == jnp.arange(counts.shape[0], dtype=jnp.int32)[None, :]
    valid = jnp.sum(jnp.where(pick_e, (counts + pad_starts)[None, :], 0), axis=1) - idx * MOE_BM
    half_only = (valid <= MOE_BM // 2).astype(jnp.int32)
    return jnp.stack([block_e, is_first.astype(jnp.int32), seg % 2, next_e, half_only]).astype(jnp.int32)


def _expert_weights_step(i, plan_ref, w_hbms, wbuf, wbf, sem, layer):
    def copies(e, slot):
        return [pltpu.make_async_copy(w.at[layer, e], wbuf.at[slot, n], sem.at[slot, n])
                for n, w in enumerate(w_hbms)]

    @pl.when(plan_ref[PLAN_FIRST, i] == 1)
    def _():
        slot = plan_ref[PLAN_SLOT, i]

        @pl.when(i == 0)
        def _():
            for c in copies(plan_ref[PLAN_EXPERT, 0], 0):
                c.start()

        for c in copies(plan_ref[PLAN_EXPERT, i], slot):
            c.wait()
        nrow, ncol = wbuf.shape[2], wbuf.shape[3]
        rows = 256

        def cast_rows(r, c):
            off = pl.multiple_of(r * rows, rows)
            for n in range(len(w_hbms)):
                wbf[pl.ds(off, rows), n * ncol:(n + 1) * ncol] = wbuf[slot, n, pl.ds(off, rows), :].astype(BF16)
            return c

        lax.fori_loop(0, nrow // rows, cast_rows, 0)
        nxt = plan_ref[PLAN_NEXT, i]

        @pl.when(nxt >= 0)
        def _():
            for c in copies(nxt, 1 - slot):
                c.start()


def _moe1_kernel(plan_ref, nu_ref, x_ref, wg_hbm, wu_hbm, o_ref, wbuf, wcat, sem, *, layer):
    i = pl.program_id(0)

    @pl.when(i < nu_ref[0])
    def _():
        _expert_weights_step(i, plan_ref, (wg_hbm, wu_hbm), wbuf, wcat, sem, layer)

        def gated(rows):
            x = _unpack_bf16_pair(x_ref[rows, :])
            hc = MOE_HIDDEN // 2
            for c in range(2):
                g = jnp.dot(x, wcat[:, c * hc:(c + 1) * hc], preferred_element_type=F32)
                u = jnp.dot(x, wcat[:, MOE_HIDDEN + c * hc:MOE_HIDDEN + (c + 1) * hc],
                            preferred_element_type=F32)
                o_ref[rows, c * hc:(c + 1) * hc] = (g * jax.nn.sigmoid(g) * u).astype(BF16)

        half = MOE_BM // 2

        @pl.when(plan_ref[PLAN_HALF, i] == 1)
        def _():
            gated(slice(0, half))
            o_ref[half:, :] = jnp.zeros((MOE_BM - half, MOE_HIDDEN), BF16)

        @pl.when(plan_ref[PLAN_HALF, i] == 0)
        def _():
            gated(slice(None))

    @pl.when(i >= nu_ref[0])
    def _():
        o_ref[...] = jnp.zeros_like(o_ref)


def _moe1(plan, n_used, xs, w_gate, w_up, layer):
    P, W = xs.shape
    D = w_gate.shape[2]
    nb = P // MOE_BM
    used_blk = lambda i, plan, nu: (jnp.minimum(i, nu[0] - 1), 0)
    return pl.pallas_call(
        functools.partial(_moe1_kernel, layer=layer),
        grid_spec=pltpu.PrefetchScalarGridSpec(
            num_scalar_prefetch=2,
            grid=(nb,),
            in_specs=[pl.BlockSpec((MOE_BM, W), used_blk), pl.BlockSpec(memory_space=pl.ANY),
                      pl.BlockSpec(memory_space=pl.ANY)],
            out_specs=pl.BlockSpec((MOE_BM, MOE_HIDDEN), lambda i, plan, nu: (i, 0)),
            scratch_shapes=[pltpu.VMEM((2, 2, D, MOE_HIDDEN), F32), pltpu.VMEM((D, 2 * MOE_HIDDEN), BF16),
                            pltpu.SemaphoreType.DMA((2, 2))],
        ),
        out_shape=jax.ShapeDtypeStruct((P, MOE_HIDDEN), BF16),
        compiler_params=_cparams(("arbitrary",), VMEM_LIMIT),
        name="moe_up",
    )(plan, n_used, xs, w_gate, w_up)


def _moe2_kernel(plan_ref, nu_ref, h_ref, wd_hbm, o_ref, wbuf, wbf, sem, *, layer):
    i = pl.program_id(0)

    @pl.when(i < nu_ref[0])
    def _():
        _expert_weights_step(i, plan_ref, (wd_hbm,), wbuf, wbf, sem, layer)
        half = MOE_BM // 2

        @pl.when(plan_ref[PLAN_HALF, i] == 1)
        def _():
            o_ref[0:half, :] = jnp.dot(h_ref[0:half, :], wbf[...], preferred_element_type=F32)
            o_ref[half:, :] = jnp.zeros((MOE_BM - half, o_ref.shape[1]), F32)

        @pl.when(plan_ref[PLAN_HALF, i] == 0)
        def _():
            o_ref[...] = jnp.dot(h_ref[...], wbf[...], preferred_element_type=F32)

    @pl.when(i >= nu_ref[0])
    def _():
        o_ref[...] = jnp.zeros_like(o_ref)


def _moe2(plan, n_used, hmid, w_down, layer):
    P = hmid.shape[0]
    D = w_down.shape[3]
    nb = P // MOE_BM
    return pl.pallas_call(
        functools.partial(_moe2_kernel, layer=layer),
        grid_spec=pltpu.PrefetchScalarGridSpec(
            num_scalar_prefetch=2,
            grid=(nb,),
            in_specs=[
                pl.BlockSpec((MOE_BM, MOE_HIDDEN), lambda i, plan, nu: (i, 0)),
                pl.BlockSpec(memory_space=pl.ANY),
            ],
            out_specs=pl.BlockSpec((MOE_BM, D), lambda i, plan, nu: (i, 0)),
            scratch_shapes=[pltpu.VMEM((2, 1, MOE_HIDDEN, D), F32), pltpu.VMEM((MOE_HIDDEN, D), BF16),
                            pltpu.SemaphoreType.DMA((2, 1))],
        ),
        out_shape=jax.ShapeDtypeStruct((P, D), F32),
        compiler_params=_cparams(("arbitrary",), VMEM_LIMIT),
        name="moe_down",
    )(plan, n_used, hmid, w_down)


def _combine_kernel(dest_ref, yb_hbm, x_ref, r_ref, mod_ref, fg_ref, o_ref, ybuf, sem, *, final):
    i = pl.program_id(0)
    n = pl.num_programs(0)

    n_tok = n * TM

    def row_copy(tok, k, t, slot):
        return pltpu.make_async_copy(yb_hbm.at[pl.ds(dest_ref[k * n_tok + tok], 1)],
                                     ybuf.at[slot, k, pl.ds(t, 1)], sem.at[slot])

    def issue_rows(blk, slot, t0):
        for r in range(8):
            for k in range(2):
                row_copy(blk * TM + t0 + r, k, t0 + r, slot).start()

    def wait_slot(slot):
        for k in range(2):
            pltpu.make_async_copy(yb_hbm.at[pl.ds(0, TM)], ybuf.at[slot, k], sem.at[slot]).wait()

    @pl.when(i == 0)
    def _():
        def first(j, c):
            issue_rows(0, 0, pl.multiple_of(j * 8, 8))
            return c
        lax.fori_loop(0, TM // 8, first, 0)

    def step(cur):
        @pl.when(i + 1 < n)
        def _():
            def body(j, c):
                issue_rows(i + 1, 1 - cur, pl.multiple_of(j * 8, 8))
                return c
            lax.fori_loop(0, TM // 8, body, 0)

        wait_slot(cur)
        r = r_ref[...]
        y = r[:, 2:3] * ybuf[cur, 0] + r[:, 3:4] * ybuf[cur, 1]
        xn = x_ref[...] + mod_ref[0, 5:6, :] * y
        if final:
            xn = _rms(xn) * fg_ref[...]
        o_ref[...] = xn

    @pl.when(i % 2 == 0)
    def _():
        step(0)

    @pl.when(i % 2 == 1)
    def _():
        step(1)


def _combine(dest_flat, yb, x_all, route, mod, final_g, seg_of, nblk, final):
    M, D = x_all.shape
    out_rows = nblk * TM if final else M
    kwargs = {} if final else {"input_output_aliases": {2: 0}}
    return pl.pallas_call(
        functools.partial(_combine_kernel, final=final),
        grid_spec=pltpu.PrefetchScalarGridSpec(
            num_scalar_prefetch=1,
            grid=(nblk,),
            in_specs=[
                pl.BlockSpec(memory_space=pl.ANY),
                pl.BlockSpec((TM, D), lambda i, d: (i, 0)),
                pl.BlockSpec((TM, 128), lambda i, d: (i, 0)),
                pl.BlockSpec((1, 6, D), lambda i, d: (seg_of(i), 0, 0)),
                pl.BlockSpec((1, D), lambda i, d: (0, 0)),
            ],
            out_specs=pl.BlockSpec((TM, D), lambda i, d: (i, 0)),
            scratch_shapes=[pltpu.VMEM((2, 2, TM, D), F32), pltpu.SemaphoreType.DMA((2,))],
        ),
        out_shape=jax.ShapeDtypeStruct((out_rows, D), F32),
        compiler_params=_cparams(("arbitrary",), VMEM_LIMIT),
        name="combine",
        **kwargs,
    )(dest_flat, yb, x_all, route, mod, final_g.reshape(1, D))


def _moe_layer(x_all, h2, route, cnt, mod, final_g, w_gate, w_up, w_down, layer, seg_of, nblk, final):
    T = nblk * TM
    e_ids = route[:, 0:2].T.astype(jnp.int32)
    ranks = route[:, 4:6].T.astype(jnp.int32)
    counts = cnt[0, :MOE_EXPERTS].astype(jnp.int32)
    padded = (counts + MOE_BM - 1) // MOE_BM * MOE_BM
    pad_ends = jnp.cumsum(padded)
    pad_starts = pad_ends - padded
    expert = jnp.arange(MOE_EXPERTS, dtype=jnp.int32)[:, None, None]
    start_of = jnp.sum(jnp.where(e_ids[None] == expert, pad_starts[:, None, None], 0), axis=0)
    dest = (start_of + ranks).reshape(-1)
    pad_info = jnp.stack([pad_starts + counts, padded - counts,
                          jnp.broadcast_to(pad_ends[-1], (MOE_EXPERTS,))]).astype(jnp.int32)
    n_blocks = -(-(2 * T + MOE_EXPERTS * (MOE_BM - 1)) // MOE_BM)
    block_starts = jnp.arange(n_blocks, dtype=jnp.int32) * MOE_BM
    block_e = jnp.sum((block_starts[:, None] >= pad_ends[None, :]).astype(jnp.int32), axis=1)
    block_e = jnp.minimum(block_e, MOE_EXPERTS - 1)
    n_used = (pad_ends[-1:] // MOE_BM).astype(jnp.int32)
    plan = _block_plan(block_e, n_used, pad_starts, counts)
    xs = _dispatch(dest, pad_info, h2, n_blocks * MOE_BM, nblk)
    hmid = _moe1(plan, n_used, xs, w_gate, w_up, layer)
    yb = _moe2(plan, n_used, hmid, w_down, layer)
    return _combine(dest, yb, x_all, route, mod, final_g, seg_of, nblk, final)


def _rope_tables(S):
    n_rows = S // GRID_W
    n_freq = HEAD_DIM // 4
    inv = 1.0 / (ROPE_THETA ** (jnp.arange(n_freq, dtype=F32) / n_freq))
    row_ang = jnp.arange(n_rows, dtype=jnp.int32).astype(F32)[:, None] * inv
    col_ang = jnp.arange(GRID_W, dtype=jnp.int32).astype(F32)[:, None] * inv
    cr, sr = jnp.repeat(jnp.cos(row_ang), GRID_W, axis=0), jnp.repeat(jnp.sin(row_ang), GRID_W, axis=0)
    cc, sc = jnp.tile(jnp.cos(col_ang), (n_rows, 1)), jnp.tile(jnp.sin(col_ang), (n_rows, 1))
    cos_t = jnp.concatenate([cr, cr, cc, cc], axis=1)
    sin_t = jnp.concatenate([-sr, sr, -sc, sc], axis=1)
    cos_t = jnp.concatenate([cos_t, jnp.ones((TM, HEAD_DIM), F32)], axis=0)
    sin_t = jnp.concatenate([sin_t, jnp.zeros((TM, HEAD_DIM), F32)], axis=0)
    return cos_t, sin_t


def kernel(x, c, ctx, c_ctx, ada_w, ada_b, norm1_g, norm2_g, w_in, w_out, att_q_norm, att_k_norm, conv_w, conv_b, lru_wr, lru_br, lru_wi, lru_bi, lru_lambda, na_rpb, router_wg, router_bg, router_we, router_be, moe_w_gate, moe_w_up, moe_w_down, final_g):
    B, S, D = x.shape
    C = ctx.shape[1]
    L = ada_w.shape[0]
    assert C == TM and S % TM == 0 and S // TM >= 3 and B + 1 <= 8
    nbs = S // TM
    nlat = B * nbs
    m_lat = B * S
    M = m_lat + B * C

    def seg_of(i):
        return jnp.minimum(i // nbs, B)

    x_rows, ctx_rows = x.reshape(m_lat, D), ctx.reshape(B * C, D)
    cvec = jnp.zeros((8, D), F32).at[:B].set(c).at[B].set(c_ctx)
    mod_all = _adaln(cvec, ada_w, ada_b)[:, :B + 1].reshape(L, B + 1, 6, D)
    cos_t, sin_t = _rope_tables(S)
    na_tables = _na_bias_tables(na_rpb)
    w_in_bf = w_in.astype(BF16)
    w_out_bf = w_out.astype(BF16)
    g_att = ATT_Q_HEADS // ATT_KV_HEADS

    out = None
    for l in range(L):
        last = l == L - 1
        mod = mod_all[l]
        proj = _inproj(x_rows, ctx_rows, mod, norm1_g[l], w_in_bf, l, B, S)

        q, k_all, v_all = _qkprep(proj, cos_t, sin_t, att_q_norm[l], att_k_norm[l], B, S)
        oa = _gqa_latent(q, k_all, v_all, B, S)

        sp = jax.nn.softplus(-lru_lambda[l].astype(F32))
        hrev = _lru_dir(proj, None, conv_w[l], conv_b[l], lru_wr[l, 1], lru_br[l, 1], lru_wi[l, 1],
                        lru_bi[l, 1], sp[1], B, S, reverse=True)
        ob = _lru_dir(proj, hrev, conv_w[l], conv_b[l], lru_wr[l, 0], lru_br[l, 0], lru_wi[l, 0],
                      lru_bi[l, 0], sp[0], B, S, reverse=False)

        oc = _na_latent(proj, na_tables[l], B, S)

        if not last:
            oa = _ctx_attn(
                q, lambda b, h: (nlat + b, h),
                k_all, pl.BlockSpec((None, TM, HEAD_DIM), lambda b, h: (b, nbs, h // g_att)),
                v_all, pl.BlockSpec((None, TM, HEAD_DIM), lambda b, h: (b, nbs, 2 * (h // g_att))),
                oa, B, ATT_Q_HEADS, nlat, None)
            oc = _ctx_attn(
                proj, lambda b, h: (nlat + b, COL_QN // HEAD_DIM + h),
                proj, pl.BlockSpec((TM, HEAD_DIM), lambda b, h: (nlat + b, COL_KN // HEAD_DIM + h)),
                proj, pl.BlockSpec((TM, HEAD_DIM), lambda b, h: (nlat + b, COL_VN // HEAD_DIM + h)),
                oc, B, NA_HEADS, nlat, HEAD_DIM ** -0.5)

        nblk = nlat if last else M // TM
        w_router = jnp.zeros((D, 128), F32).at[:, :MOE_GROUPS].set(router_wg[l])
        w_router = w_router.at[:, MOE_GROUPS:MOE_GROUPS + MOE_EXPERTS].set(router_we[l])
        b_router = jnp.zeros((1, 128), F32).at[0, :MOE_GROUPS].set(router_bg[l])
        b_router = b_router.at[0, MOE_GROUPS:MOE_GROUPS + MOE_EXPERTS].set(router_be[l])
        x_all, h2, route, cnt = _outproj(oa, ob, oc, x_rows, ctx_rows, mod, norm2_g[l], w_out_bf, l,
                                         w_router.astype(BF16), b_router, B, S, nblk * TM)
        res = _moe_layer(x_all, h2, route, cnt, mod, final_g, moe_w_gate, moe_w_up, moe_w_down, l,
                         seg_of, nblk, last)
        if last:
            out = res
        else:
            x_rows, ctx_rows = res, None
    return out.reshape(B, S, D)
```

```python
import functools

import numpy as np
import jax
import jax.numpy as jnp
from jax import lax
from jax.experimental import pallas as pl
from jax.experimental.pallas import tpu as pltpu

F32 = jnp.float32
BF16 = jnp.bfloat16

GRID_W = 64
HEAD_DIM = 128
ATT_Q_HEADS = 4
ATT_KV_HEADS = 2
LRU_WIDTH = 1024
LRU_BLOCKS = 8
LRU_C = 8.0
CONV_W = 4
NA_HEADS = 4
NA_ROWS = 8
NA_COLS = 16
ATT_Q_W = ATT_Q_HEADS * HEAD_DIM
ATT_KV_W = ATT_KV_HEADS * HEAD_DIM
NA_W = NA_HEADS * HEAD_DIM
MOE_GROUPS = 4
MOE_EXPERTS_PER_GROUP = 8
MOE_EXPERTS = MOE_GROUPS * MOE_EXPERTS_PER_GROUP
MOE_HIDDEN = 1024
ROPE_THETA = 10000.0
EPS = 1e-6
LOG2E = 1.4426950408889634

COL_QA = 0
COL_KA = COL_QA + ATT_Q_W
COL_VA = COL_KA + ATT_KV_W
COL_UB = COL_VA + ATT_KV_W
COL_GB = COL_UB + LRU_WIDTH
COL_QN = COL_GB + LRU_WIDTH
COL_KN = COL_QN + NA_W
COL_VN = COL_KN + NA_W
IN_WIDTH = COL_VN + NA_W

TM = 256
MOE_BM = 256
NA_QROWS = 4
NA_WROWS = 12
NEG_BIAS = -1e30
VMEM_LIMIT = 56 * 1024 * 1024


def _cparams(sem, vmem=None):
    return pltpu.CompilerParams(dimension_semantics=sem, vmem_limit_bytes=vmem)


def _rms(x):
    return x * lax.rsqrt(jnp.mean(x * x, axis=-1, keepdims=True) + EPS)


def _adaln_kernel(c_ref, w_ref, b_ref, o_ref):
    cv = c_ref[...]
    s = cv * jax.nn.sigmoid(cv)
    o_ref[0] = jnp.dot(s.astype(BF16), w_ref[0].astype(BF16), preferred_element_type=F32) + b_ref[0]


def _adaln(cvec, ada_w, ada_b):
    L, D, N = ada_w.shape
    tn = 1024
    return pl.pallas_call(
        _adaln_kernel,
        grid=(L, N // tn),
        in_specs=[
            pl.BlockSpec((8, D), lambda l, j: (0, 0)),
            pl.BlockSpec((1, D, tn), lambda l, j: (l, 0, j)),
            pl.BlockSpec((1, 1, tn), lambda l, j: (l, 0, j)),
        ],
        out_specs=pl.BlockSpec((1, 8, tn), lambda l, j: (l, 0, j)),
        out_shape=jax.ShapeDtypeStruct((L, 8, N), F32),
        compiler_params=_cparams(("parallel", "parallel"), VMEM_LIMIT),
        name="adaln",
    )(cvec, ada_w, ada_b.reshape(L, 1, N))


def _inproj_kernel(*refs, n_first, two_sources):
    if two_sources:
        x_ref, xb_ref, mod_ref, g_ref, w_ref, o_ref = refs
        x = jnp.where(pl.program_id(0) < n_first, x_ref[...], xb_ref[...])
    else:
        x_ref, mod_ref, g_ref, w_ref, o_ref = refs
        x = x_ref[...]
    h = _rms(x) * g_ref[...]
    h = h * (1.0 + mod_ref[0, 1:2, :]) + mod_ref[0, 0:1, :]
    o_ref[...] = jnp.dot(h.astype(BF16), w_ref[...], preferred_element_type=F32)


def _inproj(x_rows, ctx_rows, mod, g1, w_in_bf, layer, B, S):
    two = ctx_rows is not None
    D = x_rows.shape[1]
    M = x_rows.shape[0] + (ctx_rows.shape[0] if two else 0)
    N = w_in_bf.shape[2]
    tm = 512 if (S % 512 == 0 and (M - B * S) % 512 == 0) else TM
    per_seq = S // tm
    n_first = B * per_seq
    if two:
        n_ctx_blocks = ctx_rows.shape[0] // tm
        x_specs = [pl.BlockSpec((tm, D), lambda i: (jnp.minimum(i, n_first - 1), 0)),
                   pl.BlockSpec((tm, D), lambda i: (jnp.maximum(i - n_first, 0), 0),
                                pipeline_mode=pl.Buffered(1 if n_ctx_blocks == 1 else 2))]
        x_args = [x_rows, ctx_rows]
    else:
        x_specs = [pl.BlockSpec((tm, D), lambda i: (i, 0))]
        x_args = [x_rows]
    return pl.pallas_call(
        functools.partial(_inproj_kernel, n_first=n_first, two_sources=two),
        grid=(M // tm,),
        in_specs=[
            *x_specs,
            pl.BlockSpec((1, 6, D), lambda i: (jnp.minimum(i // per_seq, B), 0, 0)),
            pl.BlockSpec((1, D), lambda i: (0, 0)),
            pl.BlockSpec((None, D, N), lambda i: (layer, 0, 0), pipeline_mode=pl.Buffered(1)),
        ],
        out_specs=pl.BlockSpec((tm, N), lambda i: (i, 0)),
        out_shape=jax.ShapeDtypeStruct((M, N), F32),
        compiler_params=_cparams(("parallel",), VMEM_LIMIT),
        name="inproj",
    )(*x_args, mod, g1.reshape(1, D), w_in_bf)


def _norm_rope(t, g, cos, sin):
    y = _rms(t) * g
    lane = lax.broadcasted_iota(jnp.int32, y.shape, 1)
    first_half = (lane % 64) < 32
    partner = jnp.where(first_half, pltpu.roll(y, 96, 1), pltpu.roll(y, 32, 1))
    return y * cos + partner * sin


def _qkprep_kernel(p_ref, cos_ref, sin_ref, qg_ref, kg_ref, q_ref, k_ref, v_ref):
    cos = cos_ref[...]
    sin = sin_ref[...]
    scale = HEAD_DIM ** -0.5 * LOG2E
    for h in range(ATT_Q_HEADS):
        t = p_ref[:, COL_QA + h * HEAD_DIM:COL_QA + (h + 1) * HEAD_DIM]
        q_ref[:, h * HEAD_DIM:(h + 1) * HEAD_DIM] = (_norm_rope(t, qg_ref[...], cos, sin) * scale).astype(BF16)
    for h in range(ATT_KV_HEADS):
        t = p_ref[:, COL_KA + h * HEAD_DIM:COL_KA + (h + 1) * HEAD_DIM]
        k_ref[0, :, h * HEAD_DIM:(h + 1) * HEAD_DIM] = _norm_rope(t, kg_ref[...], cos, sin).astype(BF16)
        v_ref[0, :, 2 * h * HEAD_DIM:(2 * h + 1) * HEAD_DIM] = (
            p_ref[:, COL_VA + h * HEAD_DIM:COL_VA + (h + 1) * HEAD_DIM].astype(BF16))
        v_ref[0, :, (2 * h + 1) * HEAD_DIM:(2 * h + 2) * HEAD_DIM] = jnp.ones((p_ref.shape[0], HEAD_DIM), BF16)


def _qkprep(proj, cos_t, sin_t, qg, kg, B, S):
    M = proj.shape[0]
    nbs = S // TM
    nlat = B * nbs

    def b_of(i):
        return jnp.where(i < nlat, i // nbs, i - nlat)

    def pos_of(i):
        return jnp.where(i < nlat, i % nbs, nbs)

    kl = S + TM
    return pl.pallas_call(
        _qkprep_kernel,
        grid=(M // TM,),
        in_specs=[
            pl.BlockSpec((TM, COL_UB), lambda i: (i, 0)),
            pl.BlockSpec((TM, HEAD_DIM), lambda i: (pos_of(i), 0)),
            pl.BlockSpec((TM, HEAD_DIM), lambda i: (pos_of(i), 0)),
            pl.BlockSpec((1, HEAD_DIM), lambda i: (0, 0)),
            pl.BlockSpec((1, HEAD_DIM), lambda i: (0, 0)),
        ],
        out_specs=[
            pl.BlockSpec((TM, ATT_Q_W), lambda i: (i, 0)),
            pl.BlockSpec((1, TM, ATT_KV_W), lambda i: (b_of(i), pos_of(i), 0)),
            pl.BlockSpec((1, TM, 2 * ATT_KV_W), lambda i: (b_of(i), pos_of(i), 0)),
        ],
        out_shape=[
            jax.ShapeDtypeStruct((M, ATT_Q_W), BF16),
            jax.ShapeDtypeStruct((B, kl, ATT_KV_W), BF16),
            jax.ShapeDtypeStruct((B, kl, 2 * ATT_KV_W), BF16),
        ],
        compiler_params=_cparams(("parallel",), VMEM_LIMIT),
        name="qkprep",
    )(proj, cos_t, sin_t, qg.reshape(1, HEAD_DIM), kg.reshape(1, HEAD_DIM))


def _gqa_kernel(q_ref, k_ref, v_ref, o_in_ref, o_ref, q2_ref, sa_ref, sb_ref, m_ref, acc_ref, *, tk):
    del o_in_ref
    tq = q_ref.shape[0]
    q2_ref[0:tq, :] = q_ref[:, :HEAD_DIM]
    q2_ref[tq:, :] = q_ref[:, HEAD_DIM:]
    nk = k_ref.shape[1] // tk
    m_ref[...] = jnp.full(m_ref.shape, -jnp.inf, F32)
    acc_ref[...] = jnp.zeros(acc_ref.shape, F32)

    def scores(c, s_ref):
        off = pl.multiple_of(c * tk, tk)
        s_ref[...] = lax.dot_general(q2_ref[...], k_ref[0, pl.ds(off, tk), :], (((1,), (1,)), ((), ())),
                                     preferred_element_type=F32)

    def update(c, s_ref):
        off = pl.multiple_of(c * tk, tk)
        vc = v_ref[0, pl.ds(off, tk), :]
        for hh in range(2):
            rows = slice(hh * tq, (hh + 1) * tq)
            m = m_ref[rows, :]
            m_new = jnp.maximum(m, jnp.max(s_ref[rows, :], axis=-1, keepdims=True))
            m_ref[rows, :] = m_new
            p = jnp.exp2((s_ref[rows, :] - jnp.concatenate([m_new] * (tk // HEAD_DIM), axis=1)).astype(BF16))
            alpha = jnp.exp2(m - m_new)
            acc_ref[rows, :] = (jnp.concatenate([alpha, alpha], axis=1) * acc_ref[rows, :]
                                + jnp.dot(p, vc, preferred_element_type=F32))

    scores(0, sa_ref)
    n_pairs = (nk - 1) // 2

    def body(j, carry):
        scores(2 * j + 1, sb_ref)
        update(2 * j, sa_ref)
        scores(2 * j + 2, sa_ref)
        update(2 * j + 1, sb_ref)
        return carry

    lax.fori_loop(0, n_pairs, body, 0)
    if nk % 2 == 0:
        scores(nk - 1, sb_ref)
        update(nk - 2, sa_ref)
        update(nk - 1, sb_ref)
    else:
        update(nk - 1, sa_ref)
    acc = acc_ref[...]
    o = acc[:, :HEAD_DIM] / acc[:, HEAD_DIM:]
    o_ref[:, :HEAD_DIM] = o[:tq].astype(BF16)
    o_ref[:, HEAD_DIM:] = o[tq:].astype(BF16)


def _gqa_latent(q, k_all, v_all, B, S):
    M = q.shape[0]
    kl = k_all.shape[1]
    tq = 512 if S % 512 == 0 else TM
    nbs = S // tq
    tk = 768 if kl % 768 == 0 else TM
    g = ATT_Q_HEADS // ATT_KV_HEADS
    return pl.pallas_call(
        functools.partial(_gqa_kernel, tk=tk),
        grid=(B, ATT_KV_HEADS, nbs),
        in_specs=[
            pl.BlockSpec((tq, g * HEAD_DIM), lambda b, h, i: (b * nbs + i, h)),
            pl.BlockSpec((1, kl, HEAD_DIM), lambda b, h, i: (b, 0, h)),
            pl.BlockSpec((1, kl, 2 * HEAD_DIM), lambda b, h, i: (b, 0, h)),
            pl.BlockSpec(memory_space=pl.ANY),
        ],
        out_specs=pl.BlockSpec((tq, g * HEAD_DIM), lambda b, h, i: (b * nbs + i, h)),
        out_shape=jax.ShapeDtypeStruct((M, ATT_Q_W), BF16),
        input_output_aliases={3: 0},
        scratch_shapes=[
            pltpu.VMEM((g * tq, HEAD_DIM), BF16),
            pltpu.VMEM((g * tq, tk), F32),
            pltpu.VMEM((g * tq, tk), F32),
            pltpu.VMEM((g * tq, HEAD_DIM), F32),
            pltpu.VMEM((g * tq, 2 * HEAD_DIM), F32),
        ],
        compiler_params=_cparams(("parallel", "parallel", "parallel"), VMEM_LIMIT),
        name="gqa_latent",
    )(q, k_all, v_all, jnp.zeros((M, ATT_Q_W), BF16))


def _ctx_attn_kernel(q_ref, k_ref, v_ref, o_in_ref, o_ref, *, scale):
    del o_in_ref
    if scale is None:
        q = q_ref[...]
    else:
        q = (q_ref[...] * (scale * LOG2E)).astype(BF16)
    k = k_ref[...].astype(BF16)
    s = lax.dot_general(q, k, (((1,), (1,)), ((), ())), preferred_element_type=F32)
    p = jnp.exp2(s - jnp.max(s, axis=-1, keepdims=True))
    l = jnp.sum(p, axis=-1, keepdims=True)
    pv = jnp.dot(p.astype(BF16), v_ref[...].astype(BF16), preferred_element_type=F32)
    o_ref[...] = (pv / l).astype(BF16)


def _ctx_attn(q_arr, q_map, k_arr, k_spec, v_arr, v_spec, o_arr, B, n_heads, ctx_blk0, scale):
    return pl.pallas_call(
        functools.partial(_ctx_attn_kernel, scale=scale),
        grid=(B, n_heads),
        in_specs=[
            pl.BlockSpec((TM, HEAD_DIM), q_map),
            k_spec,
            v_spec,
            pl.BlockSpec(memory_space=pl.ANY),
        ],
        out_specs=pl.BlockSpec((TM, HEAD_DIM), lambda b, h: (ctx_blk0 + b, h)),
        out_shape=jax.ShapeDtypeStruct(o_arr.shape, o_arr.dtype),
        input_output_aliases={3: 0},
        compiler_params=_cparams(("parallel", "parallel"), VMEM_LIMIT),
        name="ctx_attn",
    )(q_arr, k_arr, v_arr, o_arr)


def _lru_kernel(u_ref, up_ref, un_ref, *rest, reverse, final):
    if final:
        (g_ref, hrev_ref, cw_ref, cb_ref, wr_ref, br_ref, wi_ref, bi_ref, sp_ref,
         o_ref, xpad, a_scr, b_scr, h_scr, hst) = rest
    else:
        (cw_ref, cb_ref, wr_ref, br_ref, wi_ref, bi_ref, sp_ref,
         o_ref, xpad, a_scr, b_scr, h_scr, hst) = rest
    s = pl.program_id(1)
    ns = pl.num_programs(1)
    T = u_ref.shape[0]
    W = u_ref.shape[1]

    @pl.when(s == 0)
    def _():
        hst[...] = jnp.zeros_like(hst)

    if reverse:
        j = ns - 1 - s
    else:
        j = s - 1
    is_first = jnp.logical_or(s == 0, j == 0)
    is_last = jnp.logical_or(s == 0, j == ns - 2)
    prev_rows = jnp.where(is_first, 0.0, up_ref[...])
    next_rows = jnp.where(is_last, 0.0, un_ref[...])
    xpad[0:8, :] = prev_rows
    xpad[8:8 + T, :] = u_ref[...]
    xpad[8 + T:16 + T, :] = next_rows
    u = (cb_ref[...]
         + xpad[7:7 + T, :] * cw_ref[0:1, :]
         + xpad[8:8 + T, :] * cw_ref[1:2, :]
         + xpad[9:9 + T, :] * cw_ref[2:3, :]
         + xpad[10:10 + T, :] * cw_ref[3:4, :])

    ub = u.astype(BF16)
    nb = LRU_WIDTH // LRU_BLOCKS
    for n in range(LRU_BLOCKS):
        sl = slice(n * nb, (n + 1) * nb)
        un_ = ub[:, sl]
        tr = jnp.tanh(jnp.dot(un_, wr_ref[n], preferred_element_type=F32) + br_ref[:, sl])
        ti = jnp.tanh(jnp.dot(un_, wi_ref[n], preferred_element_type=F32) + bi_ref[:, sl])
        ig = 0.5 * ti + 0.5
        log_a = sp_ref[:, sl] * tr + sp_ref[:, sl]
        a = jnp.exp(log_a)
        a_scr[:, sl] = a
        one_minus_a2 = -jnp.tanh(log_a) * (a * a + 1.0)
        root = one_minus_a2 * lax.rsqrt(jnp.maximum(one_minus_a2, 1e-30))
        b_scr[:, sl] = root * (ig * u[:, sl])

    def step(t, h):
        row = (T - 1 - t) if reverse else t
        h = a_scr[pl.ds(row, 1), :] * h + b_scr[pl.ds(row, 1), :]
        h_scr[pl.ds(row, 1), :] = h
        return h

    hst[...] = lax.fori_loop(0, T, step, hst[...], unroll=8)

    if final:
        y = h_scr[...] + hrev_ref[...]
        o_ref[...] = (y * jax.nn.gelu(g_ref[...])).astype(o_ref.dtype)
    else:
        o_ref[...] = h_scr[...]
    del W


def _lru_dir(proj, hrev, conv_w, conv_b, wr, br, wi, bi, sp, B, S, reverse):
    M = proj.shape[0]
    nbs = S // TM
    nlat = B * nbs
    final = hrev is not None
    ub_blk = COL_UB // LRU_WIDTH
    gb_blk = COL_GB // LRU_WIDTH
    r8 = TM // 8
    n8 = M // 8

    def blk(b, s):
        j = (nbs - s) if reverse else (s - 1)
        return jnp.where(s == 0, nlat + b, b * nbs + j)

    in_specs = [
        pl.BlockSpec((TM, LRU_WIDTH), lambda b, s: (blk(b, s), ub_blk)),
        pl.BlockSpec((8, LRU_WIDTH), lambda b, s: (jnp.maximum(blk(b, s) * r8 - 1, 0), ub_blk)),
        pl.BlockSpec((8, LRU_WIDTH), lambda b, s: (jnp.minimum((blk(b, s) + 1) * r8, n8 - 1), ub_blk)),
    ]
    args = [proj, proj, proj]
    if final:
        in_specs += [
            pl.BlockSpec((TM, LRU_WIDTH), lambda b, s: (blk(b, s), gb_blk)),
            pl.BlockSpec((TM, LRU_WIDTH), lambda b, s: (blk(b, s), 0)),
        ]
        args += [proj, hrev]
    const2 = lambda b, s: (0, 0)
    in_specs += [
        pl.BlockSpec((CONV_W, LRU_WIDTH), const2),
        pl.BlockSpec((1, LRU_WIDTH), const2),
        pl.BlockSpec((LRU_BLOCKS, LRU_WIDTH // LRU_BLOCKS, LRU_WIDTH // LRU_BLOCKS), lambda b, s: (0, 0, 0)),
        pl.BlockSpec((1, LRU_WIDTH), const2),
        pl.BlockSpec((LRU_BLOCKS, LRU_WIDTH // LRU_BLOCKS, LRU_WIDTH // LRU_BLOCKS), lambda b, s: (0, 0, 0)),
        pl.BlockSpec((1, LRU_WIDTH), const2),
        pl.BlockSpec((1, LRU_WIDTH), const2),
    ]
    args += [conv_w, conv_b.reshape(1, -1), (0.5 * wr).astype(BF16), 0.5 * br.reshape(1, -1),
             (0.5 * wi).astype(BF16), 0.5 * bi.reshape(1, -1), (-0.5 * LRU_C) * sp.reshape(1, -1)]
    return pl.pallas_call(
        functools.partial(_lru_kernel, reverse=reverse, final=final),
        grid=(B, nbs + 1),
        in_specs=in_specs,
        out_specs=pl.BlockSpec((TM, LRU_WIDTH), lambda b, s: (blk(b, s), 0)),
        out_shape=jax.ShapeDtypeStruct((M, LRU_WIDTH), BF16 if final else F32),
        scratch_shapes=[
            pltpu.VMEM((TM + 16, LRU_WIDTH), F32),
            pltpu.VMEM((TM, LRU_WIDTH), F32),
            pltpu.VMEM((TM, LRU_WIDTH), F32),
            pltpu.VMEM((TM, LRU_WIDTH), F32),
            pltpu.VMEM((1, LRU_WIDTH), F32),
        ],
        compiler_params=_cparams(("parallel", "arbitrary"), VMEM_LIMIT),
        name="lru_fwd" if final else "lru_rev",
    )(*args)


def _na_kernel(q_ref, k0_ref, k1_ref, k2_ref, v0_ref, v1_ref, v2_ref, kc_ref, vc_ref, tl_ref,
               o_in_ref, o_ref, s_scr0, s_scr1, *, n_rows):
    del o_in_ref
    i = pl.program_id(1)
    nbs = n_rows // NA_QROWS
    r0 = NA_QROWS * i
    ws = NA_QROWS * jnp.clip(i - 1, 0, nbs - 3)
    scale = HEAD_DIM ** -0.5 * LOG2E
    nkeys = s_scr0.shape[1]
    ones = jnp.ones((nkeys, HEAD_DIM), BF16)
    left_half = lax.broadcasted_iota(jnp.int32, (GRID_W, 2 * GRID_W), 1) < GRID_W

    def scores(h, s_ref):
        hs = slice(h * HEAD_DIM, (h + 1) * HEAD_DIM)
        qh = (q_ref[:, hs] * scale).astype(BF16)
        kh = jnp.concatenate([k0_ref[:, hs], k1_ref[:, hs], k2_ref[:, hs], kc_ref[:, hs]], axis=0).astype(BF16)
        s_ref[...] = lax.dot_general(qh, kh, (((1,), (1,)), ((), ())), preferred_element_type=F32)

    def finish(h, s_scr):
        hs = slice(h * HEAD_DIM, (h + 1) * HEAD_DIM)
        vh = jnp.concatenate([v0_ref[:, hs], v1_ref[:, hs], v2_ref[:, hs], vc_ref[:, hs]], axis=0).astype(BF16)
        vh = jnp.concatenate([vh, ones], axis=1)
        for qr in range(NA_QROWS):
            r = r0 + qr
            rs = jnp.clip(r - NA_ROWS // 2, 0, n_rows - NA_ROWS)

            def tile_of(a):
                kr = ws + a
                valid = jnp.logical_and(kr >= rs, kr < rs + NA_ROWS)
                return jnp.where(valid, kr - r + (NA_ROWS - 1), 2 * NA_ROWS - 1)

            for pr in range(NA_WROWS // 2):
                bias = jnp.where(left_half, tl_ref[h, tile_of(2 * pr)], tl_ref[h, tile_of(2 * pr + 1)])
                rsl = slice(qr * GRID_W, (qr + 1) * GRID_W)
                csl = slice(pr * 2 * GRID_W, (pr + 1) * 2 * GRID_W)
                s_scr[rsl, csl] = s_scr[rsl, csl] + bias
        s = s_scr[...]
        p = jnp.exp2((s - jnp.max(s, axis=-1, keepdims=True)).astype(BF16))
        pv = jnp.dot(p, vh, preferred_element_type=F32)
        o_ref[:, hs] = (pv[:, :HEAD_DIM] / pv[:, HEAD_DIM:]).astype(BF16)

    bufs = (s_scr0, s_scr1)
    scores(0, bufs[0])
    for h in range(NA_HEADS):
        if h + 1 < NA_HEADS:
            scores(h + 1, bufs[(h + 1) % 2])
        finish(h, bufs[h % 2])


def _na_latent(proj, tl, B, S):
    M = proj.shape[0]
    nbs = S // TM
    nlat = B * nbs
    n_rows = S // GRID_W
    qb, kb, vb = COL_QN // NA_W, COL_KN // NA_W, COL_VN // NA_W

    def wblk(b, i, t):
        return b * nbs + jnp.clip(i - 1, 0, nbs - 3) + t

    blk = (TM, NA_W)
    in_specs = [pl.BlockSpec(blk, lambda b, i: (b * nbs + i, qb))]
    in_specs += [pl.BlockSpec(blk, functools.partial(lambda b, i, t: (wblk(b, i, t), kb), t=t)) for t in range(3)]
    in_specs += [pl.BlockSpec(blk, functools.partial(lambda b, i, t: (wblk(b, i, t), vb), t=t)) for t in range(3)]
    in_specs += [
        pl.BlockSpec(blk, lambda b, i: (nlat + b, kb)),
        pl.BlockSpec(blk, lambda b, i: (nlat + b, vb)),
        pl.BlockSpec(tl.shape, lambda b, i: (0, 0, 0, 0)),
        pl.BlockSpec(memory_space=pl.ANY),
    ]
    return pl.pallas_call(
        functools.partial(_na_kernel, n_rows=n_rows),
        grid=(B, nbs),
        in_specs=in_specs,
        out_specs=pl.BlockSpec(blk, lambda b, i: (b * nbs + i, 0)),
        out_shape=jax.ShapeDtypeStruct((M, NA_W), BF16),
        input_output_aliases={10: 0},
        scratch_shapes=[pltpu.VMEM((TM, NA_WROWS * GRID_W + TM), F32),
                        pltpu.VMEM((TM, NA_WROWS * GRID_W + TM), F32)],
        compiler_params=_cparams(("parallel", "parallel"), VMEM_LIMIT),
        name="na_latent",
    )(proj, proj, proj, proj, proj, proj, proj, proj, proj, tl, jnp.zeros((M, NA_W), BF16))


def _na_bias_tables(rpb):
    j = np.arange(GRID_W)
    cs = np.clip(j - NA_COLS // 2, 0, GRID_W - NA_COLS)
    kc = np.arange(GRID_W)
    inside = (kc[None, :] >= cs[:, None]) & (kc[None, :] < cs[:, None] + NA_COLS)
    dc = kc[None, :] - j[:, None] + (NA_COLS - 1)
    sel = (dc[:, :, None] == np.arange(2 * NA_COLS - 1)[None, None, :]) & inside[:, :, None]
    t = jnp.einsum("lhrd,jkd->lhrjk", rpb * LOG2E, jnp.asarray(sel, F32), precision=lax.Precision.HIGHEST)
    t = jnp.where(inside, t, NEG_BIAS)
    t = jnp.concatenate([t, jnp.full_like(t[:, :, :1], NEG_BIAS)], axis=2)
    return jnp.concatenate([t, t], axis=-1)


def _outproj_kernel(*refs, n_first, n_steps, two_sources):
    if two_sources:
        (oa_ref, ob_ref, oc_ref, x_ref, xb_ref, mod_ref, g_ref, w_ref, wr_ref, br_ref,
         xo_ref, h2_ref, rt_ref, cnt_ref, carry, lg_scr, mix_scr) = refs
    else:
        (oa_ref, ob_ref, oc_ref, x_ref, mod_ref, g_ref, w_ref, wr_ref, br_ref,
         xo_ref, h2_ref, rt_ref, cnt_ref, carry, lg_scr, mix_scr) = refs
    step = pl.program_id(0)
    i = jnp.minimum(step, n_steps - 1)

    @pl.when(step == 0)
    def _():
        carry[...] = jnp.zeros_like(carry)
        lg_scr[...] = jnp.zeros_like(lg_scr)

    lg_prev = lg_scr[...]

    k1 = ATT_Q_W
    k2 = k1 + LRU_WIDTH
    for r0 in range(0, x_ref.shape[0], TM):
        rows = slice(r0, r0 + TM)
        mix_scr[rows, 0:k1] = oa_ref[rows, :]
        mix_scr[rows, k1:k2] = ob_ref[rows, :]
        mix_scr[rows, k2:] = oc_ref[rows, :]
        acc = jnp.dot(mix_scr[rows, :], w_ref[...], preferred_element_type=F32)
        x_in = x_ref[rows, :]
        if two_sources:
            x_in = jnp.where(i < n_first, x_in, xb_ref[rows, :])
        xn = x_in + mod_ref[0, 2:3, :] * acc
        xo_ref[rows, :] = xn
        h2 = _rms(xn) * g_ref[...]
        h2 = h2 * (1.0 + mod_ref[0, 4:5, :]) + mod_ref[0, 3:4, :]
        h2_ref[rows, :] = _pack_bf16_pair(h2)
        lg_scr[rows, :] = jnp.dot(h2.astype(BF16), wr_ref[...], preferred_element_type=F32) + br_ref[...]

    route = _route_math(lg_prev)
    rt_ref[...] = _rank_math(route, carry, step > 0)
    cnt_ref[...] = jnp.broadcast_to(carry[...], cnt_ref.shape)


def _pack_bf16_pair(x):
    w = x.shape[1] // 2
    lo = pltpu.bitcast(x[:, :w].astype(BF16).astype(F32), jnp.uint32)
    hi = pltpu.bitcast(x[:, w:].astype(BF16).astype(F32), jnp.uint32)
    return hi | (lo >> 16)


def _unpack_bf16_pair(words):
    lo = pltpu.bitcast(words << 16, F32)
    hi = pltpu.bitcast(words & jnp.uint32(0xFFFF0000), F32)
    return jnp.concatenate([lo, hi], axis=1).astype(BF16)


def _outproj(oa, ob, oc, x_rows, ctx_rows, mod, g2, w_out_bf, layer, w_router, b_router, B, S, n_rows):
    M = oa.shape[0]
    D = x_rows.shape[1]
    two = ctx_rows is not None
    tm = 512 if (S % 512 == 0 and (M - B * S) % 512 == 0) else TM
    per_seq = S // tm
    n_first = B * per_seq
    packed = jax.eval_shape(_pack_bf16_pair, jax.ShapeDtypeStruct((tm, D), F32))
    n_steps = n_rows // tm

    def blk(s):
        return jnp.minimum(s, n_steps - 1)

    if two:
        x_specs = [pl.BlockSpec((tm, D), lambda s: (jnp.minimum(blk(s), n_first - 1), 0)),
                   pl.BlockSpec((tm, D), lambda s: (jnp.maximum(blk(s) - n_first, 0), 0))]
        x_args = [x_rows, ctx_rows]
    else:
        x_specs = [pl.BlockSpec((tm, D), lambda s: (blk(s), 0))]
        x_args = [x_rows]
    return pl.pallas_call(
        functools.partial(_outproj_kernel, n_first=n_first, n_steps=n_steps, two_sources=two),
        grid=(n_steps + 1,),
        in_specs=[
            pl.BlockSpec((tm, ATT_Q_W), lambda s: (blk(s), 0)),
            pl.BlockSpec((tm, LRU_WIDTH), lambda s: (blk(s), 0)),
            pl.BlockSpec((tm, NA_W), lambda s: (blk(s), 0)),
            *x_specs,
            pl.BlockSpec((1, 6, D), lambda s: (jnp.minimum(blk(s) // per_seq, B), 0, 0)),
            pl.BlockSpec((1, D), lambda s: (0, 0)),
            pl.BlockSpec((None,) + w_out_bf.shape[1:], lambda s: (layer, 0, 0)),
            pl.BlockSpec(w_router.shape, lambda s: (0, 0)),
            pl.BlockSpec((1, 128), lambda s: (0, 0)),
        ],
        out_specs=[
            pl.BlockSpec((tm, D), lambda s: (blk(s), 0)),
            pl.BlockSpec(packed.shape, lambda s: (blk(s), 0)),
            pl.BlockSpec((tm, 128), lambda s: (jnp.maximum(s - 1, 0), 0)),
            pl.BlockSpec((8, 128), lambda s: (0, 0)),
        ],
        out_shape=[
            jax.ShapeDtypeStruct((n_rows, D), F32),
            jax.ShapeDtypeStruct((n_rows, packed.shape[1]), packed.dtype),
            jax.ShapeDtypeStruct((n_rows, 128), F32),
            jax.ShapeDtypeStruct((8, 128), F32),
        ],
        scratch_shapes=[pltpu.VMEM((1, 128), F32), pltpu.VMEM((tm, 128), F32),
                        pltpu.VMEM((tm, w_out_bf.shape[1]), BF16)],
        compiler_params=_cparams(("arbitrary",), VMEM_LIMIT),
        name="outproj",
    )(oa, ob, oc, *x_args, mod, g2.reshape(1, D), w_out_bf, w_router, b_router)


def _route_math(x):
    lane = lax.broadcasted_iota(jnp.int32, x.shape, 1)
    big = jnp.int32(1 << 20)
    ninf = -jnp.inf

    def first_argmax(vals):
        m = jnp.max(vals, axis=-1, keepdims=True)
        idx = jnp.min(jnp.where(vals == m, lane, big), axis=-1, keepdims=True)
        return m, idx

    lg = jnp.where(lane < MOE_GROUPS, x, ninf)
    mg, g_star = first_argmax(lg)
    g_gate = 1.0 / jnp.sum(jnp.exp(lg - mg), axis=-1, keepdims=True)
    lo = MOE_GROUPS + MOE_EXPERTS_PER_GROUP * g_star
    le = jnp.where(jnp.logical_and(lane >= lo, lane < lo + MOE_EXPERTS_PER_GROUP), x, ninf)
    v1, i1 = first_argmax(le)
    v2, i2 = first_argmax(jnp.where(lane == i1, ninf, le))
    e21 = jnp.exp(v2 - v1)
    w1 = g_gate / (1.0 + e21)
    w2 = g_gate * e21 / (1.0 + e21)
    e1 = (i1 - MOE_GROUPS).astype(F32)
    e2 = (i2 - MOE_GROUPS).astype(F32)
    return jnp.where(lane == 0, e1, jnp.where(lane == 1, e2, jnp.where(lane == 2, w1,
                     jnp.where(lane == 3, w2, 0.0))))


def _rank_math(r, carry, live):
    lane = lax.broadcasted_iota(jnp.int32, r.shape, 1).astype(F32)
    oh0 = jnp.where(lane == r[:, 0:1], 1.0, 0.0)
    oh1 = jnp.where(lane == r[:, 1:2], 1.0, 0.0)
    oh = oh0 + oh1
    n = r.shape[0]
    row = lax.broadcasted_iota(jnp.int32, (n, n), 0)
    col = lax.broadcasted_iota(jnp.int32, (n, n), 1)
    tri = jnp.where(col < row, 1.0, 0.0).astype(BF16)
    tot = carry[...] + jnp.dot(tri, oh.astype(BF16), preferred_element_type=F32)
    rank0 = jnp.sum(oh0 * tot, axis=-1, keepdims=True)
    rank1 = jnp.sum(oh1 * tot, axis=-1, keepdims=True)
    lane_i = lax.broadcasted_iota(jnp.int32, r.shape, 1)
    carry[...] = carry[...] + jnp.where(live, jnp.sum(oh, axis=0, keepdims=True), 0.0)
    return jnp.where(lane_i == 4, rank0, jnp.where(lane_i == 5, rank1, r))


def _dispatch_kernel(dest_ref, pad_ref, h_hbm, xs_hbm, hbuf, zeros, fsem, rsem, zsem, *, n_tok):
    i = pl.program_id(0)

    @pl.when(i == 0)
    def _():
        zeros[...] = jnp.zeros_like(zeros)
        pieces = [1 << b for b in range(MOE_BM.bit_length() - 2, 2, -1)]

        def copies(e):
            start, n = pad_ref[0, e], pad_ref[1, e]
            head = jnp.minimum((-start) & 7, n)
            out = []
            for r in range(7):
                out.append((r < head, pltpu.make_async_copy(zeros.at[pl.ds(0, 1)],
                                                            xs_hbm.at[pl.ds(start + r, 1)], zsem)))
            rest = n - head
            for p in pieces:
                at = pl.multiple_of(start + head + (rest & ~(2 * p - 1)), 8)
                out.append(((rest & p) != 0, pltpu.make_async_copy(zeros.at[pl.ds(0, p)],
                                                                   xs_hbm.at[pl.ds(at, p)], zsem)))
            return out

        def fill(e, c):
            for on, cp in copies(e):
                @pl.when(on)
                def _():
                    cp.start()
            return c

        def drain(e, c):
            for on, cp in copies(e):
                @pl.when(on)
                def _():
                    cp.wait()
            return c

        lax.fori_loop(0, MOE_EXPERTS, fill, 0)
        lax.fori_loop(0, MOE_EXPERTS, drain, 0)

        half = zeros.shape[0]
        n_tail = (xs_hbm.shape[0] - pad_ref[2, 0]) // half

        def tail_copy(j):
            at = pl.multiple_of(pad_ref[2, 0] + j * half, half)
            return pltpu.make_async_copy(zeros, xs_hbm.at[pl.ds(at, half)], zsem)

        def tail_fill(j, c):
            tail_copy(j).start()
            return c

        def tail_drain(j, c):
            tail_copy(j).wait()
            return c

        lax.fori_loop(0, n_tail, tail_fill, 0)
        lax.fori_loop(0, n_tail, tail_drain, 0)

    n = pl.num_programs(0)
    slot = i % 3
    nslot = (i + 1) % 3

    def fetch(blk, s):
        return pltpu.make_async_copy(h_hbm.at[pl.ds(pl.multiple_of(blk * TM, TM), TM)], hbuf.at[s], fsem.at[s])

    def rows_done(s):
        for k in range(2):
            pltpu.make_async_copy(hbuf.at[s], xs_hbm.at[pl.ds(0, TM)], rsem.at[s]).wait()

    @pl.when(i == 0)
    def _():
        fetch(0, 0).start()

    @pl.when(i >= 2)
    def _():
        rows_done(nslot)

    @pl.when(i + 1 < n)
    def _():
        fetch(i + 1, nslot).start()

    fetch(i, slot).wait()

    def issue(t, c):
        tok = i * TM + t
        for k in range(2):
            pltpu.make_async_copy(hbuf.at[slot, pl.ds(t, 1)], xs_hbm.at[pl.ds(dest_ref[k * n_tok + tok], 1)],
                                  rsem.at[slot]).start(priority=k)
        return c

    lax.fori_loop(0, TM, issue, 0, unroll=8)

    @pl.when(i == n - 1)
    def _():
        rows_done(slot)

    @pl.when(jnp.logical_and(i == n - 1, i >= 1))
    def _():
        rows_done((i + 2) % 3)


def _dispatch(dest_flat, pad_info, h2, n_rows, nblk):
    W = h2.shape[1]
    return pl.pallas_call(
        functools.partial(_dispatch_kernel, n_tok=nblk * TM),
        grid_spec=pltpu.PrefetchScalarGridSpec(
            num_scalar_prefetch=2,
            grid=(nblk,),
            in_specs=[pl.BlockSpec(memory_space=pl.ANY)],
            out_specs=pl.BlockSpec(memory_space=pl.ANY),
            scratch_shapes=[pltpu.VMEM((3, TM, W), h2.dtype), pltpu.VMEM((MOE_BM // 2, W), h2.dtype),
                            pltpu.SemaphoreType.DMA((3,)), pltpu.SemaphoreType.DMA((3,)),
                            pltpu.SemaphoreType.DMA(())],
        ),
        out_shape=jax.ShapeDtypeStruct((n_rows, W), h2.dtype),
        compiler_params=pltpu.CompilerParams(dimension_semantics=("arbitrary",), has_side_effects=True),
        name="dispatch",
    )(dest_flat, pad_info, h2)


PLAN_EXPERT, PLAN_FIRST, PLAN_SLOT, PLAN_NEXT, PLAN_HALF = 0, 1, 2, 3, 4


def _block_plan(block_e, n_used, pad_starts, counts):
    nb = block_e.shape[0]
    idx = jnp.arange(nb, dtype=jnp.int32)
    is_first = (idx < n_used[0]) & ((idx == 0) | (block_e != jnp.roll(block_e, 1)))
    seg = jnp.cumsum(is_first.astype(jnp.int32)) - 1
    first_pos = jnp.where(is_first, idx, nb)
    at_or_after = jnp.flip(lax.cummin(jnp.flip(first_pos)))
    next_first = jnp.concatenate([at_or_after[1:], jnp.full((1,), nb, jnp.int32)])
    pick_blk = next_first[:, None] == idx[None, :]
    next_e = jnp.where(next_first < nb, jnp.sum(jnp.where(pick_blk, block_e[None, :], 0), axis=1), -1)
    pick_e = block_e[:, None] == jnp.arange(counts.shape[0], dtype=jnp.int32)[None, :]
    valid = jnp.sum(jnp.where(pick_e, (counts + pad_starts)[None, :], 0), axis=1) - idx * MOE_BM
    half_only = (valid <= MOE_BM // 2).astype(jnp.int32)
    return jnp.stack([block_e, is_first.astype(jnp.int32), seg % 2, next_e, half_only]).astype(jnp.int32)


def _expert_weights_step(i, plan_ref, w_hbms, wbuf, wbf, sem, layer):
    def copies(e, slot):
        return [pltpu.make_async_copy(w.at[layer, e], wbuf.at[slot, n], sem.at[slot, n])
                for n, w in enumerate(w_hbms)]

    @pl.when(plan_ref[PLAN_FIRST, i] == 1)
    def _():
        slot = plan_ref[PLAN_SLOT, i]

        @pl.when(i == 0)
        def _():
            for c in copies(plan_ref[PLAN_EXPERT, 0], 0):
                c.start()

        for c in copies(plan_ref[PLAN_EXPERT, i], slot):
            c.wait()
        nrow, ncol = wbuf.shape[2], wbuf.shape[3]
        rows = 256

        def cast_rows(r, c):
            off = pl.multiple_of(r * rows, rows)
            for n in range(len(w_hbms)):
                wbf[pl.ds(off, rows), n * ncol:(n + 1) * ncol] = wbuf[slot, n, pl.ds(off, rows), :].astype(BF16)
            return c

        lax.fori_loop(0, nrow // rows, cast_rows, 0)
        nxt = plan_ref[PLAN_NEXT, i]

        @pl.when(nxt >= 0)
        def _():
            for c in copies(nxt, 1 - slot):
                c.start()


def _moe1_kernel(plan_ref, nu_ref, x_ref, wg_hbm, wu_hbm, o_ref, wbuf, wcat, sem, *, layer):
    i = pl.program_id(0)

    @pl.when(i < nu_ref[0])
    def _():
        _expert_weights_step(i, plan_ref, (wg_hbm, wu_hbm), wbuf, wcat, sem, layer)

        def gated(rows):
            x = _unpack_bf16_pair(x_ref[rows, :])
            hc = MOE_HIDDEN // 2
            for c in range(2):
                g = jnp.dot(x, wcat[:, c * hc:(c + 1) * hc], preferred_element_type=F32)
                u = jnp.dot(x, wcat[:, MOE_HIDDEN + c * hc:MOE_HIDDEN + (c + 1) * hc],
                            preferred_element_type=F32)
                o_ref[rows, c * hc:(c + 1) * hc] = (g * jax.nn.sigmoid(g) * u).astype(BF16)

        half = MOE_BM // 2

        @pl.when(plan_ref[PLAN_HALF, i] == 1)
        def _():
            gated(slice(0, half))
            o_ref[half:, :] = jnp.zeros((MOE_BM - half, MOE_HIDDEN), BF16)

        @pl.when(plan_ref[PLAN_HALF, i] == 0)
        def _():
            gated(slice(None))

    @pl.when(i >= nu_ref[0])
    def _():
        o_ref[...] = jnp.zeros_like(o_ref)


def _moe1(plan, n_used, xs, w_gate, w_up, layer):
    P, W = xs.shape
    D = w_gate.shape[2]
    nb = P // MOE_BM
    used_blk = lambda i, plan, nu: (jnp.minimum(i, nu[0] - 1), 0)
    return pl.pallas_call(
        functools.partial(_moe1_kernel, layer=layer),
        grid_spec=pltpu.PrefetchScalarGridSpec(
            num_scalar_prefetch=2,
            grid=(nb,),
            in_specs=[pl.BlockSpec((MOE_BM, W), used_blk), pl.BlockSpec(memory_space=pl.ANY),
                      pl.BlockSpec(memory_space=pl.ANY)],
            out_specs=pl.BlockSpec((MOE_BM, MOE_HIDDEN), lambda i, plan, nu: (i, 0)),
            scratch_shapes=[pltpu.VMEM((2, 2, D, MOE_HIDDEN), F32), pltpu.VMEM((D, 2 * MOE_HIDDEN), BF16),
                            pltpu.SemaphoreType.DMA((2, 2))],
        ),
        out_shape=jax.ShapeDtypeStruct((P, MOE_HIDDEN), BF16),
        compiler_params=_cparams(("arbitrary",), VMEM_LIMIT),
        name="moe_up",
    )(plan, n_used, xs, w_gate, w_up)


def _moe2_kernel(plan_ref, nu_ref, h_ref, wd_hbm, o_ref, wbuf, wbf, sem, *, layer):
    i = pl.program_id(0)

    @pl.when(i < nu_ref[0])
    def _():
        _expert_weights_step(i, plan_ref, (wd_hbm,), wbuf, wbf, sem, layer)
        half = MOE_BM // 2

        @pl.when(plan_ref[PLAN_HALF, i] == 1)
        def _():
            o_ref[0:half, :] = jnp.dot(h_ref[0:half, :], wbf[...], preferred_element_type=F32)
            o_ref[half:, :] = jnp.zeros((MOE_BM - half, o_ref.shape[1]), F32)

        @pl.when(plan_ref[PLAN_HALF, i] == 0)
        def _():
            o_ref[...] = jnp.dot(h_ref[...], wbf[...], preferred_element_type=F32)

    @pl.when(i >= nu_ref[0])
    def _():
        o_ref[...] = jnp.zeros_like(o_ref)


def _moe2(plan, n_used, hmid, w_down, layer):
    P = hmid.shape[0]
    D = w_down.shape[3]
    nb = P // MOE_BM
    return pl.pallas_call(
        functools.partial(_moe2_kernel, layer=layer),
        grid_spec=pltpu.PrefetchScalarGridSpec(
            num_scalar_prefetch=2,
            grid=(nb,),
            in_specs=[
                pl.BlockSpec((MOE_BM, MOE_HIDDEN), lambda i, plan, nu: (i, 0)),
                pl.BlockSpec(memory_space=pl.ANY),
            ],
            out_specs=pl.BlockSpec((MOE_BM, D), lambda i, plan, nu: (i, 0)),
            scratch_shapes=[pltpu.VMEM((2, 1, MOE_HIDDEN, D), F32), pltpu.VMEM((MOE_HIDDEN, D), BF16),
                            pltpu.SemaphoreType.DMA((2, 1))],
        ),
        out_shape=jax.ShapeDtypeStruct((P, D), F32),
        compiler_params=_cparams(("arbitrary",), VMEM_LIMIT),
        name="moe_down",
    )(plan, n_used, hmid, w_down)


def _combine_kernel(dest_ref, yb_hbm, x_ref, r_ref, mod_ref, fg_ref, o_ref, ybuf, sem, *, final):
    i = pl.program_id(0)
    n = pl.num_programs(0)

    n_tok = n * TM

    def row_copy(tok, k, t, slot):
        return pltpu.make_async_copy(yb_hbm.at[pl.ds(dest_ref[k * n_tok + tok], 1)],
                                     ybuf.at[slot, k, pl.ds(t, 1)], sem.at[slot])

    def issue_rows(blk, slot, t0):
        for r in range(8):
            for k in range(2):
                row_copy(blk * TM + t0 + r, k, t0 + r, slot).start(priority=k)

    def wait_slot(slot):
        for k in range(2):
            pltpu.make_async_copy(yb_hbm.at[pl.ds(0, TM)], ybuf.at[slot, k], sem.at[slot]).wait()

    @pl.when(i == 0)
    def _():
        def first(j, c):
            issue_rows(0, 0, pl.multiple_of(j * 8, 8))
            return c
        lax.fori_loop(0, TM // 8, first, 0)

    def step(cur):
        @pl.when(i + 1 < n)
        def _():
            def body(j, c):
                issue_rows(i + 1, 1 - cur, pl.multiple_of(j * 8, 8))
                return c
            lax.fori_loop(0, TM // 8, body, 0)

        wait_slot(cur)
        r = r_ref[...]
        y = r[:, 2:3] * ybuf[cur, 0] + r[:, 3:4] * ybuf[cur, 1]
        xn = x_ref[...] + mod_ref[0, 5:6, :] * y
        if final:
            xn = _rms(xn) * fg_ref[...]
        o_ref[...] = xn

    @pl.when(i % 2 == 0)
    def _():
        step(0)

    @pl.when(i % 2 == 1)
    def _():
        step(1)


def _combine(dest_flat, yb, x_all, route, mod, final_g, seg_of, nblk, final):
    M, D = x_all.shape
    out_rows = nblk * TM if final else M
    kwargs = {} if final else {"input_output_aliases": {2: 0}}
    return pl.pallas_call(
        functools.partial(_combine_kernel, final=final),
        grid_spec=pltpu.PrefetchScalarGridSpec(
            num_scalar_prefetch=1,
            grid=(nblk,),
            in_specs=[
                pl.BlockSpec(memory_space=pl.ANY),
                pl.BlockSpec((TM, D), lambda i, d: (i, 0)),
                pl.BlockSpec((TM, 128), lambda i, d: (i, 0)),
                pl.BlockSpec((1, 6, D), lambda i, d: (seg_of(i), 0, 0)),
                pl.BlockSpec((1, D), lambda i, d: (0, 0)),
            ],
            out_specs=pl.BlockSpec((TM, D), lambda i, d: (i, 0)),
            scratch_shapes=[pltpu.VMEM((2, 2, TM, D), F32), pltpu.SemaphoreType.DMA((2,))],
        ),
        out_shape=jax.ShapeDtypeStruct((out_rows, D), F32),
        compiler_params=_cparams(("arbitrary",), VMEM_LIMIT),
        name="combine",
        **kwargs,
    )(dest_flat, yb, x_all, route, mod, final_g.reshape(1, D))


def _moe_layer(x_all, h2, route, cnt, mod, final_g, w_gate, w_up, w_down, layer, seg_of, nblk, final):
    T = nblk * TM
    e_ids = route[:, 0:2].T.astype(jnp.int32)
    ranks = route[:, 4:6].T.astype(jnp.int32)
    counts = cnt[0, :MOE_EXPERTS].astype(jnp.int32)
    padded = (counts + MOE_BM - 1) // MOE_BM * MOE_BM
    pad_ends = jnp.cumsum(padded)
    pad_starts = pad_ends - padded
    expert = jnp.arange(MOE_EXPERTS, dtype=jnp.int32)[:, None, None]
    start_of = jnp.sum(jnp.where(e_ids[None] == expert, pad_starts[:, None, None], 0), axis=0)
    dest = (start_of + ranks).reshape(-1)
    pad_info = jnp.stack([pad_starts + counts, padded - counts,
                          jnp.broadcast_to(pad_ends[-1], (MOE_EXPERTS,))]).astype(jnp.int32)
    n_blocks = -(-(2 * T + MOE_EXPERTS * (MOE_BM - 1)) // MOE_BM)
    block_starts = jnp.arange(n_blocks, dtype=jnp.int32) * MOE_BM
    block_e = jnp.sum((block_starts[:, None] >= pad_ends[None, :]).astype(jnp.int32), axis=1)
    block_e = jnp.minimum(block_e, MOE_EXPERTS - 1)
    n_used = (pad_ends[-1:] // MOE_BM).astype(jnp.int32)
    plan = _block_plan(block_e, n_used, pad_starts, counts)
    xs = _dispatch(dest, pad_info, h2, n_blocks * MOE_BM, nblk)
    hmid = _moe1(plan, n_used, xs, w_gate, w_up, layer)
    yb = _moe2(plan, n_used, hmid, w_down, layer)
    return _combine(dest, yb, x_all, route, mod, final_g, seg_of, nblk, final)


def _rope_tables(S):
    n_rows = S // GRID_W
    n_freq = HEAD_DIM // 4
    inv = 1.0 / (ROPE_THETA ** (jnp.arange(n_freq, dtype=F32) / n_freq))
    row_ang = jnp.arange(n_rows, dtype=jnp.int32).astype(F32)[:, None] * inv
    col_ang = jnp.arange(GRID_W, dtype=jnp.int32).astype(F32)[:, None] * inv
    cr, sr = jnp.repeat(jnp.cos(row_ang), GRID_W, axis=0), jnp.repeat(jnp.sin(row_ang), GRID_W, axis=0)
    cc, sc = jnp.tile(jnp.cos(col_ang), (n_rows, 1)), jnp.tile(jnp.sin(col_ang), (n_rows, 1))
    cos_t = jnp.concatenate([cr, cr, cc, cc], axis=1)
    sin_t = jnp.concatenate([-sr, sr, -sc, sc], axis=1)
    cos_t = jnp.concatenate([cos_t, jnp.ones((TM, HEAD_DIM), F32)], axis=0)
    sin_t = jnp.concatenate([sin_t, jnp.zeros((TM, HEAD_DIM), F32)], axis=0)
    return cos_t, sin_t


def kernel(x, c, ctx, c_ctx, ada_w, ada_b, norm1_g, norm2_g, w_in, w_out, att_q_norm, att_k_norm, conv_w, conv_b, lru_wr, lru_br, lru_wi, lru_bi, lru_lambda, na_rpb, router_wg, router_bg, router_we, router_be, moe_w_gate, moe_w_up, moe_w_down, final_g):
    B, S, D = x.shape
    C = ctx.shape[1]
    L = ada_w.shape[0]
    assert C == TM and S % TM == 0 and S // TM >= 3 and B + 1 <= 8
    nbs = S // TM
    nlat = B * nbs
    m_lat = B * S
    M = m_lat + B * C

    def seg_of(i):
        return jnp.minimum(i // nbs, B)

    x_rows, ctx_rows = x.reshape(m_lat, D), ctx.reshape(B * C, D)
    cvec = jnp.zeros((8, D), F32).at[:B].set(c).at[B].set(c_ctx)
    mod_all = _adaln(cvec, ada_w, ada_b)[:, :B + 1].reshape(L, B + 1, 6, D)
    cos_t, sin_t = _rope_tables(S)
    na_tables = _na_bias_tables(na_rpb)
    w_in_bf = w_in.astype(BF16)
    w_out_bf = w_out.astype(BF16)
    g_att = ATT_Q_HEADS // ATT_KV_HEADS

    out = None
    for l in range(L):
        last = l == L - 1
        mod = mod_all[l]
        proj = _inproj(x_rows, ctx_rows, mod, norm1_g[l], w_in_bf, l, B, S)

        q, k_all, v_all = _qkprep(proj, cos_t, sin_t, att_q_norm[l], att_k_norm[l], B, S)
        oa = _gqa_latent(q, k_all, v_all, B, S)

        sp = jax.nn.softplus(-lru_lambda[l].astype(F32))
        hrev = _lru_dir(proj, None, conv_w[l], conv_b[l], lru_wr[l, 1], lru_br[l, 1], lru_wi[l, 1],
                        lru_bi[l, 1], sp[1], B, S, reverse=True)
        ob = _lru_dir(proj, hrev, conv_w[l], conv_b[l], lru_wr[l, 0], lru_br[l, 0], lru_wi[l, 0],
                      lru_bi[l, 0], sp[0], B, S, reverse=False)

        oc = _na_latent(proj, na_tables[l], B, S)

        if not last:
            oa = _ctx_attn(
                q, lambda b, h: (nlat + b, h),
                k_all, pl.BlockSpec((None, TM, HEAD_DIM), lambda b, h: (b, nbs, h // g_att)),
                v_all, pl.BlockSpec((None, TM, HEAD_DIM), lambda b, h: (b, nbs, 2 * (h // g_att))),
                oa, B, ATT_Q_HEADS, nlat, None)
            oc = _ctx_attn(
                proj, lambda b, h: (nlat + b, COL_QN // HEAD_DIM + h),
                proj, pl.BlockSpec((TM, HEAD_DIM), lambda b, h: (nlat + b, COL_KN // HEAD_DIM + h)),
                proj, pl.BlockSpec((TM, HEAD_DIM), lambda b, h: (nlat + b, COL_VN // HEAD_DIM + h)),
                oc, B, NA_HEADS, nlat, HEAD_DIM ** -0.5)

        nblk = nlat if last else M // TM
        w_router = jnp.zeros((D, 128), F32).at[:, :MOE_GROUPS].set(router_wg[l])
        w_router = w_router.at[:, MOE_GROUPS:MOE_GROUPS + MOE_EXPERTS].set(router_we[l])
        b_router = jnp.zeros((1, 128), F32).at[0, :MOE_GROUPS].set(router_bg[l])
        b_router = b_router.at[0, MOE_GROUPS:MOE_GROUPS + MOE_EXPERTS].set(router_be[l])
        x_all, h2, route, cnt = _outproj(oa, ob, oc, x_rows, ctx_rows, mod, norm2_g[l], w_out_bf, l,
                                         w_router.astype(BF16), b_router, B, S, nblk * TM)
        res = _moe_layer(x_all, h2, route, cnt, mod, final_g, moe_w_gate, moe_w_up, moe_w_down, l,
                         seg_of, nblk, last)
        if last:
            out = res
        else:
            x_rows, ctx_rows = res, None
    return out.reshape(B, S, D)
```
